```python
import jax, jax.numpy as jnp
from jax import lax
import numpy as np

D_MODEL = 1024
BATCH = 8
SEQ = 4096
DEPTH = 1

CHUNK = 64
MIX_WIDTH = D_MODEL
ATTN_WIDTH = MIX_WIDTH // 2
N_HEADS = 8
HEAD_DIM = ATTN_WIDTH // N_HEADS
POOL_WIDTH = MIX_WIDTH - ATTN_WIDTH
POOL_WINDOWS = (2, 4, 8, 16)
N_POOL_GROUPS = len(POOL_WINDOWS)
POOL_GROUP = POOL_WIDTH // N_POOL_GROUPS
IN_WIDTH = 3 * ATTN_WIDTH + N_HEADS + POOL_WIDTH
D_FF = -(-(8 * D_MODEL) // (3 * 256)) * 256
Q_BLOCK = 128
EPS = 1e-6

kernel_name = "fox_pool_hybrid_block"


def rmsnorm(x, g):
    xf = x.astype(jnp.float32)
    y = xf * lax.rsqrt(jnp.mean(xf * xf, axis=-1, keepdims=True) + EPS)
    return (y * g.astype(jnp.float32)).astype(x.dtype)


def forgetting_attention(q, k, v, log_f):
    S, Dh = q.shape[2], q.shape[3]
    c = jnp.cumsum(log_f, axis=-1)
    qf = q.astype(jnp.float32) * (Dh ** -0.5)
    kf = k.astype(jnp.float32)
    vf = v.astype(jnp.float32)
    outs = []
    for i in range(S // Q_BLOCK):
        q0, q1 = i * Q_BLOCK, (i + 1) * Q_BLOCK
        logits = jnp.einsum('bhqd,bhkd->bhqk', qf[:, :, q0:q1], kf[:, :, :q1])
        logits = logits + c[:, :, q0:q1, None] - c[:, :, None, :q1]
        t_pos = jnp.arange(q0, q1)[:, None]
        s_pos = jnp.arange(q1)[None, :]
        logits = jnp.where(s_pos <= t_pos, logits, -jnp.inf)
        p = jax.nn.softmax(logits, axis=-1)
        outs.append(jnp.einsum('bhqk,bhkd->bhqd', p, vf[:, :, :q1]))
    return jnp.concatenate(outs, axis=2).astype(q.dtype)


def multiscale_pool(u, w_pool, pool_scale):
    B, S, C = u.shape
    uf = u.astype(jnp.float32)
    cs = jnp.concatenate([jnp.zeros((B, 1, C), jnp.float32), jnp.cumsum(uf, axis=1)], axis=1)
    t = jnp.arange(S)
    groups = []
    for g, w in enumerate(POOL_WINDOWS):
        lo, hi = g * POOL_GROUP, (g + 1) * POOL_GROUP
        start = jnp.maximum(t + 1 - w, 0)
        csg = cs[:, :, lo:hi]
        window_sum = csg[:, 1:] - csg[:, start]
        count = (t + 1 - start).astype(jnp.float32)[None, :, None]
        groups.append(window_sum / count - uf[:, :, lo:hi])
    pooled = jnp.stack(groups, axis=2)
    mixed = jnp.einsum('bsgc,gcd->bsgd', pooled, w_pool.astype(jnp.float32)).reshape(B, S, C)
    return (mixed * pool_scale.astype(jnp.float32)).astype(u.dtype)


def _fwd_setup_inputs(seed: int = 0) -> dict:
    key = jax.random.key(seed)
    ks = jax.random.split(key, 13)
    f32 = jnp.float32
    nrm = lambda k, shape, scale: jax.random.normal(k, shape, f32) * scale
    return {
        "x": jax.random.normal(ks[0], (BATCH, SEQ, D_MODEL), f32),
        "norm1_g": 1.0 + nrm(ks[1], (DEPTH, D_MODEL), 0.05),
        "w_in": nrm(ks[2], (DEPTH, D_MODEL, IN_WIDTH), D_MODEL ** -0.5),
        "b_forget": 2.0 + nrm(ks[3], (DEPTH, N_HEADS), 0.5),
        "w_pool": nrm(ks[4], (DEPTH, N_POOL_GROUPS, POOL_GROUP, POOL_GROUP), POOL_GROUP ** -0.5),
        "pool_scale": 1.0 + nrm(ks[5], (DEPTH, POOL_WIDTH), 0.1),
        "w_out": nrm(ks[6], (DEPTH, MIX_WIDTH, D_MODEL), MIX_WIDTH ** -0.5),
        "norm2_g": 1.0 + nrm(ks[7], (DEPTH, D_MODEL), 0.05),
        "w_gate": nrm(ks[8], (DEPTH, D_MODEL, D_FF), D_MODEL ** -0.5),
        "w_up": nrm(ks[9], (DEPTH, D_MODEL, D_FF), D_MODEL ** -0.5),
        "w_down": nrm(ks[10], (DEPTH, D_FF, D_MODEL), D_FF ** -0.5),
        "final_g": 1.0 + nrm(ks[11], (D_MODEL,), 0.05),
    }


def _fwd_reference(x, norm1_g, w_in, b_forget, w_pool, pool_scale, w_out, norm2_g, w_gate, w_up, w_down, final_g):
    B, S, _ = x.shape
    for layer in range(DEPTH):
        h = rmsnorm(x, norm1_g[layer])
        proj = jnp.einsum('bsd,de->bse', h, w_in[layer])
        a0 = ATTN_WIDTH
        q = proj[..., 0:a0]
        k = proj[..., a0:2 * a0]
        v = proj[..., 2 * a0:3 * a0]
        f_logit = proj[..., 3 * a0:3 * a0 + N_HEADS]
        u = proj[..., 3 * a0 + N_HEADS:]
        to_heads = lambda t: t.reshape(B, S, N_HEADS, HEAD_DIM).transpose(0, 2, 1, 3)
        log_f = jax.nn.log_sigmoid(f_logit.astype(jnp.float32) + b_forget[layer].astype(jnp.float32))
        log_f = log_f.transpose(0, 2, 1)
        attn = forgetting_attention(to_heads(q), to_heads(k), to_heads(v), log_f)
        attn = attn.transpose(0, 2, 1, 3).reshape(B, S, ATTN_WIDTH)
        pool = multiscale_pool(u, w_pool[layer], pool_scale[layer])
        mixed = jnp.concatenate([attn, pool], axis=-1)
        x = x + jnp.einsum('bse,ed->bsd', mixed, w_out[layer])
        h2 = rmsnorm(x, norm2_g[layer])
        gate = jnp.einsum('bsd,df->bsf', h2, w_gate[layer])
        up = jnp.einsum('bsd,df->bsf', h2, w_up[layer])
        x = x + jnp.einsum('bsf,fd->bsd', jax.nn.silu(gate) * up, w_down[layer])
    return rmsnorm(x, final_g)


import jax as _jax
import jax.numpy as _jnp

TWIN_FORMAT = 'train_step'
FWD_PARAMS = ['x', 'norm1_g', 'w_in', 'b_forget', 'w_pool', 'pool_scale', 'w_out', 'norm2_g', 'w_gate', 'w_up', 'w_down', 'final_g']
TWIN_WEIGHTS = ['norm1_g', 'w_in', 'b_forget', 'w_pool', 'pool_scale', 'w_out', 'norm2_g', 'w_gate', 'w_up', 'w_down', 'final_g']
TWIN_DIFF_INPUT = 'x'
TWIN_INPUTS = ['x', 'norm1_g', 'w_in', 'b_forget', 'w_pool', 'pool_scale', 'w_out', 'norm2_g', 'w_gate', 'w_up', 'w_down', 'final_g', 'loss_target', 'm_norm1_g', 'm_w_in', 'm_b_forget', 'm_w_pool', 'm_pool_scale', 'm_w_out', 'm_norm2_g', 'm_w_gate', 'm_w_up', 'm_w_down', 'm_final_g', 'v_norm1_g', 'v_w_in', 'v_b_forget', 'v_w_pool', 'v_pool_scale', 'v_w_out', 'v_norm2_g', 'v_w_gate', 'v_w_up', 'v_w_down', 'v_final_g']
TWIN_OUTPUTS = ['loss', 'grad_x', 'grad_norm1_g', 'grad_w_in', 'grad_b_forget', 'grad_w_pool', 'grad_pool_scale', 'grad_w_out', 'grad_norm2_g', 'grad_w_gate', 'grad_w_up', 'grad_w_down', 'grad_final_g', 'delta_norm1_g', 'delta_w_in', 'delta_b_forget', 'delta_w_pool', 'delta_pool_scale', 'delta_w_out', 'delta_norm2_g', 'delta_w_gate', 'delta_w_up', 'delta_w_down', 'delta_final_g', 'new_m_norm1_g', 'new_m_w_in', 'new_m_b_forget', 'new_m_w_pool', 'new_m_pool_scale', 'new_m_w_out', 'new_m_norm2_g', 'new_m_w_gate', 'new_m_w_up', 'new_m_w_down', 'new_m_final_g', 'new_v_norm1_g', 'new_v_w_in', 'new_v_b_forget', 'new_v_w_pool', 'new_v_pool_scale', 'new_v_w_out', 'new_v_norm2_g', 'new_v_w_gate', 'new_v_w_up', 'new_v_w_down', 'new_v_final_g']
TWIN_LEAF_KINDS = {'loss': 'loss', 'grad_x': 'grad_x', 'grad_norm1_g': 'grad_w', 'grad_w_in': 'grad_w', 'grad_b_forget': 'grad_w', 'grad_w_pool': 'grad_w', 'grad_pool_scale': 'grad_w', 'grad_w_out': 'grad_w', 'grad_norm2_g': 'grad_w', 'grad_w_gate': 'grad_w', 'grad_w_up': 'grad_w', 'grad_w_down': 'grad_w', 'grad_final_g': 'grad_w', 'delta_norm1_g': 'delta_w', 'delta_w_in': 'delta_w', 'delta_b_forget': 'delta_w', 'delta_w_pool': 'delta_w', 'delta_pool_scale': 'delta_w', 'delta_w_out': 'delta_w', 'delta_norm2_g': 'delta_w', 'delta_w_gate': 'delta_w', 'delta_w_up': 'delta_w', 'delta_w_down': 'delta_w', 'delta_final_g': 'delta_w', 'new_m_norm1_g': 'new_m', 'new_m_w_in': 'new_m', 'new_m_b_forget': 'new_m', 'new_m_w_pool': 'new_m', 'new_m_pool_scale': 'new_m', 'new_m_w_out': 'new_m', 'new_m_norm2_g': 'new_m', 'new_m_w_gate': 'new_m', 'new_m_w_up': 'new_m', 'new_m_w_down': 'new_m', 'new_m_final_g': 'new_m', 'new_v_norm1_g': 'new_v', 'new_v_w_in': 'new_v', 'new_v_b_forget': 'new_v', 'new_v_w_pool': 'new_v', 'new_v_pool_scale': 'new_v', 'new_v_w_out': 'new_v', 'new_v_norm2_g': 'new_v', 'new_v_w_gate': 'new_v', 'new_v_w_up': 'new_v', 'new_v_w_down': 'new_v', 'new_v_final_g': 'new_v'}


def _forward(args):
    return _fwd_reference(*[args[k] for k in FWD_PARAMS])


def _output_shape():
    out = _jax.eval_shape(lambda: _forward(_fwd_setup_inputs(0)))
    return out.shape, out.dtype

N_MICROBATCH = 1
ADAM_LR = 0.001
ADAM_B1 = 0.9
ADAM_B2 = 0.999
ADAM_EPS = 1e-08
ADAM_WD = 0.01
ADAM_STEP = 10
PER_EXAMPLE_BATCH_AXIS = {'x': 0, 'loss_target': 0}
SHARED_INPUTS = []
_WEIGHT_DTYPES = {'norm1_g': _jnp.float32, 'w_in': _jnp.float32, 'b_forget': _jnp.float32, 'w_pool': _jnp.float32, 'pool_scale': _jnp.float32, 'w_out': _jnp.float32, 'norm2_g': _jnp.float32, 'w_gate': _jnp.float32, 'w_up': _jnp.float32, 'w_down': _jnp.float32, 'final_g': _jnp.float32}
MOMENT_SCALE = {'norm1_g': 1.212740e-01, 'w_in': 8.693172e-02, 'b_forget': 6.418034e-01, 'w_pool': 1.438518e-01, 'pool_scale': 1.576581e-01, 'w_out': 1.139927e-01, 'norm2_g': 1.133142e-01, 'w_gate': 4.904316e-02, 'w_up': 4.766971e-02, 'w_down': 7.945436e-02, 'final_g': 3.208134e+01}


def _to_microbatches(a, axis):
    t = _jnp.moveaxis(a, axis, 0)
    t = t.reshape((N_MICROBATCH, t.shape[0] // N_MICROBATCH) + t.shape[1:])
    return _jnp.moveaxis(t, 1, axis + 1)


def setup_inputs(seed: int = 0) -> dict:
    inp = _fwd_setup_inputs(seed)
    key = _jax.random.fold_in(_jax.random.key(seed), 7919)
    shape, _ = _output_shape()
    out = dict(inp)
    out["loss_target"] = _jax.random.normal(_jax.random.fold_in(key, 0), shape, _jnp.float32)
    for i, name in enumerate(TWIN_WEIGHTS):
        w = inp[name].astype(_jnp.float32)
        if MOMENT_SCALE is None:
            s = _jnp.sqrt(_jnp.mean(_jnp.square(w)) + 1e-30)
        else:
            s = MOMENT_SCALE[name]
        km, kv = _jax.random.split(_jax.random.fold_in(key, i + 1))
        out[name] = w
        out["m_" + name] = s * _jax.random.normal(km, w.shape, _jnp.float32)
        out["v_" + name] = (s * s) * _jax.random.uniform(kv, w.shape, _jnp.float32, 0.5, 1.5)
    if N_MICROBATCH > 1:
        for name, axis in PER_EXAMPLE_BATCH_AXIS.items():
            out[name] = _to_microbatches(out[name], axis)
    return {'x': out['x'], 'norm1_g': out['norm1_g'], 'w_in': out['w_in'], 'b_forget': out['b_forget'], 'w_pool': out['w_pool'], 'pool_scale': out['pool_scale'], 'w_out': out['w_out'], 'norm2_g': out['norm2_g'], 'w_gate': out['w_gate'], 'w_up': out['w_up'], 'w_down': out['w_down'], 'final_g': out['final_g'], 'loss_target': out['loss_target'], 'm_norm1_g': out['m_norm1_g'], 'm_w_in': out['m_w_in'], 'm_b_forget': out['m_b_forget'], 'm_w_pool': out['m_w_pool'], 'm_pool_scale': out['m_pool_scale'], 'm_w_out': out['m_w_out'], 'm_norm2_g': out['m_norm2_g'], 'm_w_gate': out['m_w_gate'], 'm_w_up': out['m_w_up'], 'm_w_down': out['m_w_down'], 'm_final_g': out['m_final_g'], 'v_norm1_g': out['v_norm1_g'], 'v_w_in': out['v_w_in'], 'v_b_forget': out['v_b_forget'], 'v_w_pool': out['v_w_pool'], 'v_pool_scale': out['v_pool_scale'], 'v_w_out': out['v_w_out'], 'v_norm2_g': out['v_norm2_g'], 'v_w_gate': out['v_w_gate'], 'v_w_up': out['v_w_up'], 'v_w_down': out['v_w_down'], 'v_final_g': out['v_final_g']}


def _loss(weights, diff, rest, loss_target):
    with _jax.named_scope("forward"):
        args = {**rest, TWIN_DIFF_INPUT: diff, **{k: w.astype(_WEIGHT_DTYPES[k]) for k, w in weights.items()}}
        y = _forward(args)
    with _jax.named_scope("loss_head"):
        err = _jnp.square(y.astype(_jnp.float32) - loss_target)
        return 0.5 * _jnp.sum(_jnp.mean(err, axis=-1)) if err.ndim else 0.5 * err


def _adamw(w, g, m, v):
    m = ADAM_B1 * m + (1.0 - ADAM_B1) * g
    v = ADAM_B2 * v + (1.0 - ADAM_B2) * _jnp.square(g)
    m_hat = m / (1.0 - ADAM_B1 ** ADAM_STEP)
    v_hat = v / (1.0 - ADAM_B2 ** ADAM_STEP)
    delta = -ADAM_LR * (m_hat / (_jnp.sqrt(v_hat) + ADAM_EPS) + ADAM_WD * w)
    return delta, m, v


def reference(x, norm1_g, w_in, b_forget, w_pool, pool_scale, w_out, norm2_g, w_gate, w_up, w_down, final_g, loss_target, m_norm1_g, m_w_in, m_b_forget, m_w_pool, m_pool_scale, m_w_out, m_norm2_g, m_w_gate, m_w_up, m_w_down, m_final_g, v_norm1_g, v_w_in, v_b_forget, v_w_pool, v_pool_scale, v_w_out, v_norm2_g, v_w_gate, v_w_up, v_w_down, v_final_g):
    given = dict(x=x, norm1_g=norm1_g, w_in=w_in, b_forget=b_forget, w_pool=w_pool, pool_scale=pool_scale, w_out=w_out, norm2_g=norm2_g, w_gate=w_gate, w_up=w_up, w_down=w_down, final_g=final_g, loss_target=loss_target, m_norm1_g=m_norm1_g, m_w_in=m_w_in, m_b_forget=m_b_forget, m_w_pool=m_w_pool, m_pool_scale=m_pool_scale, m_w_out=m_w_out, m_norm2_g=m_norm2_g, m_w_gate=m_w_gate, m_w_up=m_w_up, m_w_down=m_w_down, m_final_g=m_final_g, v_norm1_g=v_norm1_g, v_w_in=v_w_in, v_b_forget=v_b_forget, v_w_pool=v_w_pool, v_pool_scale=v_pool_scale, v_w_out=v_w_out, v_norm2_g=v_norm2_g, v_w_gate=v_w_gate, v_w_up=v_w_up, v_w_down=v_w_down, v_final_g=v_final_g)
    weights = {n: given[n] for n in TWIN_WEIGHTS}
    shared = {n: given[n] for n in SHARED_INPUTS}
    per_example = {n: given[n] for n in ['x']}
    grad_fn = _jax.value_and_grad(_loss, argnums=(0, 1))

    def one_microbatch(ex, loss_target):
        ex = dict(ex)
        diff = ex.pop(TWIN_DIFF_INPUT)
        return grad_fn(weights, diff, {**shared, **ex}, loss_target)

    if N_MICROBATCH == 1:
        loss, (grad_w, grad_x) = one_microbatch(per_example, given["loss_target"])
    else:
        def body(carry, xs):
            loss_sum, grad_sum = carry
            l_k, (gw_k, gx_k) = one_microbatch(xs[0], xs[1])
            with _jax.named_scope("update"):
                return (loss_sum + l_k, _jax.tree.map(_jnp.add, grad_sum, gw_k)), gx_k

        init = (_jnp.zeros((), _jnp.float32), _jax.tree.map(_jnp.zeros_like, weights))
        (loss, grad_w), grad_x = _jax.lax.scan(body, init, (per_example, given["loss_target"]))
    with _jax.named_scope("update"):
        delta_w, new_m, new_v = {}, {}, {}
        for n in TWIN_WEIGHTS:
            delta_w[n], new_m[n], new_v[n] = _adamw(weights[n], grad_w[n], given["m_" + n], given["v_" + n])
    return (loss, grad_x, *[grad_w[n] for n in TWIN_WEIGHTS], *[delta_w[n] for n in TWIN_WEIGHTS],
            *[new_m[n] for n in TWIN_WEIGHTS], *[new_v[n] for n in TWIN_WEIGHTS])
```

```python
import functools

import jax
import jax.numpy as jnp
from jax import lax
from jax.experimental import pallas as pl
from jax.experimental.pallas import tpu as pltpu

F32 = jnp.float32
BF16 = jnp.bfloat16
SDS = jax.ShapeDtypeStruct

D_MODEL = 1024
ATTN_W = 512
N_HEADS = 8
HEAD_DIM = 64
N_PAIRS = N_HEADS // 2
POOL_W = 512
POOL_WINDOWS = (2, 4, 8, 16)
POOL_G = 128
HALO = 16
IN_W = 3 * ATTN_W + N_HEADS + POOL_W
QKV_W = 3 * ATTN_W
U_OFF = QKV_W
F_OFF = QKV_W + POOL_W
IN_PAD = F_OFF + 128
D_FF = 2816
EPS = 1e-6
NEG = -1e30
N_DEV = 8
LANES = 128

ADAM_LR = 0.001
ADAM_B1 = 0.9
ADAM_B2 = 0.999
ADAM_EPS = 1e-08
ADAM_WD = 0.01
ADAM_STEP = 10

VMEM_LIMIT_BYTES = 56 * 1024 * 1024
MESH = pl.DeviceIdType.MESH
NT = (((1,), (1,)), ((), ()))
TN = (((0,), (0,)), ((), ()))


def _cparams(*sem):
    return pltpu.CompilerParams(dimension_semantics=sem or None, vmem_limit_bytes=VMEM_LIMIT_BYTES)


def _split3(a):
    hi = a.astype(BF16)
    r1 = a - hi.astype(F32)
    mid = r1.astype(BF16)
    lo = (r1 - mid.astype(F32)).astype(BF16)
    return hi, mid, lo


def _dot_sel(a, sel, dims=None):
    sb = sel.astype(BF16)
    if dims is None:
        return sum(jnp.dot(p, sb, preferred_element_type=F32) for p in _split3(a))
    return sum(lax.dot_general(p, sb, dims, preferred_element_type=F32) for p in _split3(a))


def _sel_dot(sel, a, dims=None):
    sb = sel.astype(BF16)
    if dims is None:
        return sum(jnp.dot(sb, p, preferred_element_type=F32) for p in _split3(a))
    return sum(lax.dot_general(sb, p, dims, preferred_element_type=F32) for p in _split3(a))


def _iota2(shape, dim):
    return lax.broadcasted_iota(jnp.int32, shape, dim)


def _norm_proj(x, g1, w_in_p, *, tm):
    s = x.shape[0]

    def body(x_ref, g_ref, w_ref, h_ref, r_ref, qkv_ref, u_ref, fl_ref):
        xv = x_ref[...]
        r = lax.rsqrt(jnp.mean(xv * xv, axis=-1, keepdims=True) + EPS)
        h = (xv * r * g_ref[...]).astype(BF16)
        h_ref[...] = h
        r_ref[...] = r
        qkv_ref[...] = jnp.dot(h, w_ref[:, 0:QKV_W], preferred_element_type=F32).astype(BF16)
        u_ref[...] = jnp.dot(h, w_ref[:, U_OFF:F_OFF], preferred_element_type=F32)
        fl_ref[...] = jnp.dot(h, w_ref[:, F_OFF:IN_PAD], preferred_element_type=F32)

    row = lambda w: pl.BlockSpec((tm, w), lambda i: (i, 0))
    full = lambda a, b: pl.BlockSpec((a, b), lambda i: (0, 0))
    return pl.pallas_call(
        body,
        grid=(s // tm,),
        in_specs=[row(D_MODEL), full(1, D_MODEL), full(D_MODEL, IN_PAD)],
        out_specs=[row(D_MODEL), row(1), row(QKV_W), row(POOL_W), row(LANES)],
        out_shape=[SDS((s, D_MODEL), BF16), SDS((s, 1), F32), SDS((s, QKV_W), BF16), SDS((s, POOL_W), F32),
                   SDS((s, LANES), F32)],
        compiler_params=_cparams("arbitrary"),
        name="norm_proj",
    )(x, g1, w_in_p)


def _head_block_masks(rows, nb):
    shift = nb.bit_length() - 1
    rr, cc = _iota2((rows, rows), 0), _iota2((rows, rows), 1)
    same = lax.shift_right_logical(rr, shift) == lax.shift_right_logical(cc, shift)
    return rr, cc, same


def _forget_cumsum(fl_t, b_rows):
    rows = fl_t.shape[0]
    nb = rows // N_HEADS

    def body(fl_ref, b_ref, c_ref):
        z = fl_ref[...] + b_ref[...]
        lf = jnp.minimum(z, 0.0) - jnp.log1p(jnp.exp(-jnp.abs(z)))
        upper = _iota2((LANES, LANES), 0) <= _iota2((LANES, LANES), 1)
        within = _dot_sel(lf, upper)
        tot = _dot_sel(lf, jnp.ones((LANES, LANES), F32))
        rr, cc, same = _head_block_masks(rows, nb)
        c_ref[...] = within + _sel_dot(same & (cc < rr), tot)

    return pl.pallas_call(body, out_shape=SDS(fl_t.shape, F32), compiler_params=_cparams(), name="forget_cumsum")(
        fl_t, b_rows)


def _attn_fwd(qkv, c_col, c_rowblk, *, tq):
    s = qkv.shape[0]
    tk = tq
    nb = s // tq

    def body(q_ref, k_ref, v_ref, cq_ref, ck_ref, o_ref, lse_ref):
        i = pl.program_id(1)
        lane = _iota2((tq, LANES), 1)
        q2 = q_ref[...]
        zq = jnp.zeros_like(q2)
        qh = (jnp.where(lane < HEAD_DIM, q2, zq), jnp.where(lane >= HEAD_DIM, q2, zq))
        cq = cq_ref[...]

        def step(j, carry, masked):
            start = pl.multiple_of(j * tk, tk)
            k2 = k_ref[pl.ds(start, tk), :]
            v2 = v_ref[pl.ds(start, tk), :]
            ckb = ck_ref[j]
            out = []
            for h in range(2):
                m, l, acc = carry[h]
                sc = lax.dot_general(qh[h], k2, NT, preferred_element_type=F32) * 0.125
                sc = sc + cq[:, h:h + 1] - ckb[h:h + 1, :]
                if masked:
                    sc = jnp.where(_iota2((tq, tk), 1) <= _iota2((tq, tk), 0), sc, NEG)
                m_new = jnp.maximum(m, jnp.max(sc, axis=1, keepdims=True))
                alpha = jnp.exp(m - m_new)
                p = jnp.exp(sc - m_new)
                l = alpha * l + jnp.sum(p, axis=1, keepdims=True)
                acc = alpha * acc + jnp.dot(p.astype(BF16), v2, preferred_element_type=F32)
                out.append((m_new, l, acc))
            return tuple(out)

        init = tuple((jnp.full((tq, 1), NEG, F32), jnp.zeros((tq, 1), F32), jnp.zeros((tq, LANES), F32))
                     for _ in range(2))
        carry = lax.fori_loop(0, i, lambda j, c: step(j, c, False), init)
        (ma, la, acca), (mb, lb, accb) = step(i, carry, True)
        o_ref[...] = jnp.where(lane < HEAD_DIM, acca / la, accb / lb).astype(BF16)
        lse_ref[...] = jnp.where(_iota2((tq, 2), 1) == 0, ma + jnp.log(la), mb + jnp.log(lb))

    return pl.pallas_call(
        body,
        grid=(N_PAIRS, nb),
        in_specs=[
            pl.BlockSpec((tq, LANES), lambda p, i: (i, p)),
            pl.BlockSpec((s, LANES), lambda p, i: (0, N_PAIRS + p)),
            pl.BlockSpec((s, LANES), lambda p, i: (0, 2 * N_PAIRS + p)),
            pl.BlockSpec((None, tq, 2), lambda p, i: (p, i, 0)),
            pl.BlockSpec((None, nb, 2, tk), lambda p, i: (p, 0, 0, 0)),
        ],
        out_specs=[
            pl.BlockSpec((tq, LANES), lambda p, i: (i, p)),
            pl.BlockSpec((None, tq, 2), lambda p, i: (p, i, 0)),
        ],
        out_shape=[SDS((s, ATTN_W), BF16), SDS((N_PAIRS, s, 2), F32)],
        compiler_params=_cparams("arbitrary", "arbitrary"),
        name="attn_fwd",
    )(qkv, qkv, qkv, c_col, c_rowblk)


def _pool_counts(row0, tm, w):
    t = row0 + _iota2((tm, 1), 0)
    return jnp.minimum(t + 1, w).astype(F32)


def _pool_fwd(u, w_pool, pool_scale, *, tm):
    s = u.shape[0]

    def body(u_ref, w_ref, sc_ref, pooled_ref, po_ref, tail_ref):
        i = pl.program_id(0)

        @pl.when(i == 0)
        def _():
            tail_ref[...] = jnp.zeros_like(tail_ref)

        uv = u_ref[...]
        ext = jnp.concatenate([tail_ref[...], uv], axis=0)
        tail_ref[...] = uv[tm - HALO:, :]
        for g, w in enumerate(POOL_WINDOWS):
            cols = slice(g * POOL_G, (g + 1) * POOL_G)
            acc = ext[:, cols]
            k = 1
            while k < w:
                acc = acc + pltpu.roll(acc, k, axis=0)
                k *= 2
            pooled = (acc[HALO:, :] / _pool_counts(i * tm, tm, w) - uv[:, cols]).astype(BF16)
            pooled_ref[:, cols] = pooled
            mixed = jnp.dot(pooled, w_ref[g].astype(BF16), preferred_element_type=F32)
            po_ref[:, cols] = (mixed * sc_ref[:, cols]).astype(BF16)

    row = pl.BlockSpec((tm, POOL_W), lambda i: (i, 0))
    return pl.pallas_call(
        body,
        grid=(s // tm,),
        in_specs=[row, pl.BlockSpec((len(POOL_WINDOWS), POOL_G, POOL_G), lambda i: (0, 0, 0)),
                  pl.BlockSpec((1, POOL_W), lambda i: (0, 0))],
        out_specs=[row, row],
        out_shape=[SDS((s, POOL_W), BF16), SDS((s, POOL_W), BF16)],
        scratch_shapes=[pltpu.VMEM((HALO, POOL_W), F32)],
        compiler_params=_cparams("arbitrary"),
        name="pool_fwd",
    )(u, w_pool, pool_scale)


def _out_norm2(attn_o, pool_o, w_out, x, g2, *, tm):
    s = x.shape[0]

    def body(a_ref, p_ref, w_ref, x_ref, g_ref, x1_ref, h2_ref, r_ref):
        x1 = (x_ref[...] + jnp.dot(a_ref[...], w_ref[0:ATTN_W, :], preferred_element_type=F32)
              + jnp.dot(p_ref[...], w_ref[ATTN_W:, :], preferred_element_type=F32))
        r = lax.rsqrt(jnp.mean(x1 * x1, axis=-1, keepdims=True) + EPS)
        x1_ref[...] = x1
        r_ref[...] = r
        h2_ref[...] = (x1 * r * g_ref[...]).astype(BF16)

    row = lambda w: pl.BlockSpec((tm, w), lambda i: (i, 0))
    full = lambda a, b: pl.BlockSpec((a, b), lambda i: (0, 0))
    return pl.pallas_call(
        body,
        grid=(s // tm,),
        in_specs=[row(ATTN_W), row(POOL_W), full(D_MODEL, D_MODEL), row(D_MODEL), full(1, D_MODEL)],
        out_specs=[row(D_MODEL), row(D_MODEL), row(1)],
        out_shape=[SDS((s, D_MODEL), F32), SDS((s, D_MODEL), BF16), SDS((s, 1), F32)],
        compiler_params=_cparams("arbitrary"),
        name="out_norm2",
    )(attn_o, pool_o, w_out, x, g2)


def _gate_up(h2, wg, wu, *, tm, tn):
    s = h2.shape[0]

    def body(h_ref, wg_ref, wu_ref, gate_ref, up_ref, act_ref):
        h = h_ref[...]
        gate = jnp.dot(h, wg_ref[...], preferred_element_type=F32)
        up = jnp.dot(h, wu_ref[...], preferred_element_type=F32)
        gate_ref[...] = gate
        up_ref[...] = up
        act_ref[...] = (gate * jax.nn.sigmoid(gate) * up).astype(BF16)

    wspec = pl.BlockSpec((D_MODEL, tn), lambda c, r: (0, c))
    ospec = pl.BlockSpec((tm, tn), lambda c, r: (r, c))
    return pl.pallas_call(
        body,
        grid=(D_FF // tn, s // tm),
        in_specs=[pl.BlockSpec((tm, D_MODEL), lambda c, r: (r, 0)), wspec, wspec],
        out_specs=[ospec, ospec, ospec],
        out_shape=[SDS((s, D_FF), F32), SDS((s, D_FF), F32), SDS((s, D_FF), BF16)],
        compiler_params=_cparams("arbitrary", "arbitrary"),
        name="gate_up",
    )(h2, wg, wu)


def _down_final(act, wd, x1, gf, tgt, *, tm):
    s = x1.shape[0]

    def body(a_ref, w_ref, x1_ref, g_ref, t_ref, dx2_ref, loss_ref, dgf_ref):
        @pl.when(pl.program_id(0) == 0)
        def _():
            loss_ref[...] = jnp.zeros_like(loss_ref)
            dgf_ref[...] = jnp.zeros_like(dgf_ref)

        x2 = x1_ref[...] + jnp.dot(a_ref[...], w_ref[...], preferred_element_type=F32)
        r = lax.rsqrt(jnp.mean(x2 * x2, axis=-1, keepdims=True) + EPS)
        xn = x2 * r
        g = g_ref[...]
        diff = xn * g - t_ref[...]
        loss_ref[...] += jnp.sum(diff * diff, axis=0, keepdims=True)
        dy = diff * (1.0 / D_MODEL)
        dgf_ref[...] += jnp.sum(dy * xn, axis=0, keepdims=True)
        dxn = dy * g
        dx2_ref[...] = r * (dxn - xn * jnp.mean(dxn * xn, axis=-1, keepdims=True))

    row = lambda w: pl.BlockSpec((tm, w), lambda i: (i, 0))
    full = lambda a, b: pl.BlockSpec((a, b), lambda i: (0, 0))
    return pl.pallas_call(
        body,
        grid=(s // tm,),
        in_specs=[row(D_FF), full(D_FF, D_MODEL), row(D_MODEL), full(1, D_MODEL), row(D_MODEL)],
        out_specs=[row(D_MODEL), full(1, D_MODEL), full(1, D_MODEL)],
        out_shape=[SDS((s, D_MODEL), F32), SDS((1, D_MODEL), F32), SDS((1, D_MODEL), F32)],
        compiler_params=_cparams("arbitrary"),
        name="down_final",
    )(act, wd, x1, gf, tgt)


def _swiglu_bwd(dx2, wd, gate, up, *, tm, tn):
    s = dx2.shape[0]

    def body(d_ref, w_ref, gate_ref, up_ref, dgate_ref, dup_ref):
        dact = lax.dot_general(d_ref[...].astype(BF16), w_ref[...], NT, preferred_element_type=F32)
        gate = gate_ref[...]
        sg = jax.nn.sigmoid(gate)
        dup_ref[...] = (dact * (gate * sg)).astype(BF16)
        dgate_ref[...] = (dact * up_ref[...] * (sg * (1.0 + gate * (1.0 - sg)))).astype(BF16)

    ospec = pl.BlockSpec((tm, tn), lambda c, r: (r, c))
    return pl.pallas_call(
        body,
        grid=(D_FF // tn, s // tm),
        in_specs=[pl.BlockSpec((tm, D_MODEL), lambda c, r: (r, 0)), pl.BlockSpec((tn, D_MODEL), lambda c, r: (c, 0)),
                  ospec, ospec],
        out_specs=[ospec, ospec],
        out_shape=[SDS((s, D_FF), BF16), SDS((s, D_FF), BF16)],
        compiler_params=_cparams("arbitrary", "arbitrary"),
        name="swiglu_bwd",
    )(dx2, wd, gate, up)


def _mm_tn(a, bs, *, ta, ts, name):
    s, ka = a.shape
    n = len(bs)

    def body(a_ref, *refs):
        b_refs, o_refs = refs[:n], refs[n:]

        @pl.when(pl.program_id(1) == 0)
        def _():
            for o_ref in o_refs:
                o_ref[...] = jnp.zeros_like(o_ref)

        av = a_ref[...].astype(BF16)
        for b_ref, o_ref in zip(b_refs, o_refs):
            o_ref[...] += lax.dot_general(av, b_ref[...].astype(BF16), TN, preferred_element_type=F32)

    return pl.pallas_call(
        body,
        grid=(ka // ta, s // ts),
        in_specs=[pl.BlockSpec((ts, ta), lambda i, k: (k, i))]
        + [pl.BlockSpec((ts, b.shape[1]), lambda i, k: (k, 0)) for b in bs],
        out_specs=[pl.BlockSpec((ta, b.shape[1]), lambda i, k: (i, 0)) for b in bs],
        out_shape=[SDS((ka, b.shape[1]), F32) for b in bs],
        compiler_params=_cparams("arbitrary", "arbitrary"),
        name=name,
    )(a, *bs)


def _norm_bwd(dh, x, r, g, dres):
    xn = x * r
    dxn = dh * g
    dx = dres + r * (dxn - xn * jnp.mean(dxn * xn, axis=-1, keepdims=True))
    return dx, jnp.sum(dh * xn, axis=0, keepdims=True)


def _mlp_in_bwd(dgate, dup, wg, wu, w_out, x1, r2, g2, dx2, *, tm):
    s = x1.shape[0]

    def body(dg_ref, du_ref, wg_ref, wu_ref, wo_ref, x_ref, r_ref, g_ref, d_ref, dx1_ref, dmix_ref, dg2_ref):
        @pl.when(pl.program_id(0) == 0)
        def _():
            dg2_ref[...] = jnp.zeros_like(dg2_ref)

        dh2 = (lax.dot_general(dg_ref[...], wg_ref[...], NT, preferred_element_type=F32)
               + lax.dot_general(du_ref[...], wu_ref[...], NT, preferred_element_type=F32))
        dx1, dg2 = _norm_bwd(dh2, x_ref[...], r_ref[...], g_ref[...], d_ref[...])
        dg2_ref[...] += dg2
        dx1_ref[...] = dx1
        dmix_ref[...] = lax.dot_general(dx1.astype(BF16), wo_ref[...], NT, preferred_element_type=F32)

    row = lambda w: pl.BlockSpec((tm, w), lambda i: (i, 0))
    full = lambda a, b: pl.BlockSpec((a, b), lambda i: (0, 0))
    return pl.pallas_call(
        body,
        grid=(s // tm,),
        in_specs=[row(D_FF), row(D_FF), full(D_MODEL, D_FF), full(D_MODEL, D_FF), full(D_MODEL, D_MODEL),
                  row(D_MODEL), row(1), full(1, D_MODEL), row(D_MODEL)],
        out_specs=[row(D_MODEL), row(D_MODEL), full(1, D_MODEL)],
        out_shape=[SDS((s, D_MODEL), F32), SDS((s, D_MODEL), F32), SDS((1, D_MODEL), F32)],
        compiler_params=_cparams("arbitrary"),
        name="mlp_in_bwd",
    )(dgate, dup, wg, wu, w_out, x1, r2, g2, dx2)


def _pool_bwd(dmixed, pooled, w_pool, pool_scale, *, tm):
    s = pooled.shape[0]
    nt = s // tm
    ng = len(POOL_WINDOWS)

    def body(d_ref, p_ref, w_ref, sc_ref, du_ref, dw_ref, dsc_ref, head_ref):
        i = pl.program_id(0)

        @pl.when(i == 0)
        def _():
            head_ref[...] = jnp.zeros_like(head_ref)
            dw_ref[...] = jnp.zeros_like(dw_ref)
            dsc_ref[...] = jnp.zeros_like(dsc_ref)

        row0 = (nt - 1 - i) * tm
        for g, w in enumerate(POOL_WINDOWS):
            cols = slice(g * POOL_G, (g + 1) * POOL_G)
            wb = w_ref[g].astype(BF16)
            pooled_g = p_ref[:, cols]
            dpo = d_ref[:, cols]
            mixed = jnp.dot(pooled_g, wb, preferred_element_type=F32)
            dsc_ref[:, cols] += jnp.sum(dpo * mixed, axis=0, keepdims=True)
            dmp = (dpo * sc_ref[:, cols]).astype(BF16)
            dw_ref[g] += lax.dot_general(pooled_g, dmp, TN, preferred_element_type=F32)
            dpooled = lax.dot_general(dmp, wb, NT, preferred_element_type=F32)
            a = dpooled / _pool_counts(row0, tm, w)
            acc = jnp.concatenate([a, head_ref[:, cols]], axis=0)
            head_ref[:, cols] = a[0:HALO, :]
            k = 1
            while k < w:
                acc = acc + pltpu.roll(acc, tm + HALO - k, axis=0)
                k *= 2
            du_ref[:, cols] = (acc[0:tm, :] - dpooled).astype(BF16)

    rev = lambda i: (nt - 1 - i, 0)
    return pl.pallas_call(
        body,
        grid=(nt,),
        in_specs=[pl.BlockSpec((tm, POOL_W), lambda i: (nt - 1 - i, 1)), pl.BlockSpec((tm, POOL_W), rev),
                  pl.BlockSpec((ng, POOL_G, POOL_G), lambda i: (0, 0, 0)), pl.BlockSpec((1, POOL_W), lambda i: (0, 0))],
        out_specs=[pl.BlockSpec((tm, POOL_W), rev), pl.BlockSpec((ng, POOL_G, POOL_G), lambda i: (0, 0, 0)),
                   pl.BlockSpec((1, POOL_W), lambda i: (0, 0))],
        out_shape=[SDS((s, POOL_W), BF16), SDS((ng, POOL_G, POOL_G), F32), SDS((1, POOL_W), F32)],
        scratch_shapes=[pltpu.VMEM((HALO, POOL_W), F32)],
        compiler_params=_cparams("arbitrary"),
        name="pool_bwd",
    )(dmixed, pooled, w_pool, pool_scale)


def _attn_bwd(qkv, attn_o, dmixed, rowb, ck_col, *, tq):
    s = qkv.shape[0]
    tk = tq
    nb = s // tq

    def body(q_ref, k_ref, v_ref, o_ref, do_ref, rowb_ref, ck_ref, dq_ref, dk_ref, dv_ref, dck_ref, dcq_ref, dq_acc,
             delta_ref):
        lane = _iota2((tq, LANES), 1)
        lo = lane < HEAD_DIM
        first = _iota2((8, LANES), 1) < HEAD_DIM
        sel = jnp.where(_iota2((8, LANES), 0) < 4, jnp.where(first, 1.0, 0.0), jnp.where(first, 0.0, 1.0))
        row8 = _iota2((8, tq), 0)

        def prep(i, _):
            st = pl.multiple_of(i * tq, tq)
            prod = do_ref[pl.ds(st, tq), :] * o_ref[pl.ds(st, tq), :].astype(F32)
            delta_ref[i] = _sel_dot(sel, prod, NT)
            dq_acc[pl.ds(st, tq), :] = jnp.zeros((tq, LANES), F32)
            dcq_ref[i] = jnp.zeros((8, tq), F32)
            return 0

        lax.fori_loop(0, nb, prep, 0)

        def split(t):
            z = jnp.zeros_like(t)
            return jnp.where(lo, t, z), jnp.where(lo, z, t)

        def kv_block(j, _):
            st_j = pl.multiple_of(j * tk, tk)
            ks = split(k_ref[pl.ds(st_j, tk), :])
            vs = split(v_ref[pl.ds(st_j, tk), :])
            kcat = jnp.concatenate(ks, axis=0)
            ck = ck_ref[pl.ds(st_j, tk), :]

            def q_block(i, carry, masked):
                dk_acc, dv_acc, dca, dcb = carry
                st_i = pl.multiple_of(i * tq, tq)
                q2 = q_ref[pl.ds(st_i, tq), :]
                do2 = do_ref[pl.ds(st_i, tq), :].astype(BF16)
                rb = rowb_ref[i]
                dl = delta_ref[i]
                pts, dsts = [], []
                for h in range(2):
                    st = lax.dot_general(ks[h], q2, NT, preferred_element_type=F32) * 0.125
                    st = st + rb[h:h + 1, :] - ck[:, h:h + 1]
                    if masked:
                        st = jnp.where(_iota2((tk, tq), 0) <= _iota2((tk, tq), 1), st, NEG)
                    pt = jnp.exp(st)
                    dpt = lax.dot_general(vs[h], do2, NT, preferred_element_type=F32)
                    pts.append(pt.astype(BF16))
                    dsts.append(pt * (dpt - dl[4 * h:4 * h + 1, :]))
                dca = dca + jnp.sum(dsts[0], axis=1, keepdims=True)
                dcb = dcb + jnp.sum(dsts[1], axis=1, keepdims=True)
                dcq_ref[i] += jnp.where(row8 == 0, jnp.sum(dsts[0], axis=0, keepdims=True),
                                        jnp.where(row8 == 4, jnp.sum(dsts[1], axis=0, keepdims=True), 0.0))
                dsb = [d.astype(BF16) for d in dsts]
                dv_acc = dv_acc + jnp.dot(jnp.concatenate(pts, axis=1), jnp.concatenate(split(do2), axis=0),
                                          preferred_element_type=F32)
                dk_acc = dk_acc + jnp.dot(jnp.concatenate(dsb, axis=1), jnp.concatenate(split(q2), axis=0),
                                          preferred_element_type=F32)
                dq_acc[pl.ds(st_i, tq), :] += lax.dot_general(jnp.concatenate(dsb, axis=0), kcat, TN,
                                                              preferred_element_type=F32)
                return dk_acc, dv_acc, dca, dcb

            zt = jnp.zeros((tk, LANES), F32)
            zc = jnp.zeros((tk, 1), F32)
            carry = q_block(j, (zt, zt, zc, zc), True)
            dk_acc, dv_acc, dca, dcb = lax.fori_loop(j + 1, nb, lambda i, c: q_block(i, c, False), carry)
            dk_ref[pl.ds(st_j, tk), :] = (dk_acc * 0.125).astype(BF16)
            dv_ref[pl.ds(st_j, tk), :] = dv_acc.astype(BF16)
            dck_ref[pl.ds(st_j, tk), :] = jnp.where(_iota2((tk, 2), 1) == 0, dca, dcb)
            return 0

        lax.fori_loop(0, nb, kv_block, 0)
        dq_ref[...] = (dq_acc[...] * 0.125).astype(BF16)

    col = lambda off: pl.BlockSpec((s, LANES), lambda p: (0, off + p))
    return pl.pallas_call(
        body,
        grid=(N_PAIRS,),
        in_specs=[col(0), col(N_PAIRS), col(2 * N_PAIRS), col(0), col(0),
                  pl.BlockSpec((None, nb, 2, tq), lambda p: (p, 0, 0, 0)),
                  pl.BlockSpec((None, s, 2), lambda p: (p, 0, 0))],
        out_specs=[col(0), col(0), col(0), pl.BlockSpec((None, s, 2), lambda p: (p, 0, 0)),
                   pl.BlockSpec((None, nb, 8, tq), lambda p: (p, 0, 0, 0))],
        out_shape=[SDS((s, ATTN_W), BF16), SDS((s, ATTN_W), BF16), SDS((s, ATTN_W), BF16), SDS((N_PAIRS, s, 2), F32),
                   SDS((N_PAIRS, nb, 8, tq), F32)],
        scratch_shapes=[pltpu.VMEM((s, LANES), F32), pltpu.VMEM((nb, 8, tq), F32)],
        compiler_params=_cparams("arbitrary"),
        name="attn_bwd",
    )(qkv, qkv, qkv, attn_o, dmixed, rowb, ck_col)


def _forget_bwd(dc_t, fl_t, b_rows):
    rows = fl_t.shape[0]
    nb = rows // N_HEADS

    def body(dc_ref, fl_ref, b_ref, dfl_ref, db_ref):
        dc = dc_ref[...]
        lower = _iota2((LANES, LANES), 0) >= _iota2((LANES, LANES), 1)
        ones = jnp.ones((LANES, LANES), F32)
        rr, cc, same = _head_block_masks(rows, nb)
        dlf = _dot_sel(dc, lower) + _sel_dot(same & (cc > rr), _dot_sel(dc, ones))
        dfl = dlf / (1.0 + jnp.exp(fl_ref[...] + b_ref[...]))
        dfl_ref[...] = dfl
        shift = nb.bit_length() - 1
        hsel = lax.shift_right_logical(_iota2((N_HEADS, rows), 1), shift) == _iota2((N_HEADS, rows), 0)
        db_ref[...] = _sel_dot(hsel, _dot_sel(dfl, ones))

    return pl.pallas_call(body, out_shape=[SDS(fl_t.shape, F32), SDS((N_HEADS, LANES), F32)],
                          compiler_params=_cparams(), name="forget_bwd")(dc_t, fl_t, b_rows)


def _in_bwd(dq, dk, dv, du, dfl, w_in_p, x, r1, g1, dx1, *, tm):
    s = x.shape[0]
    pieces = ((0, ATTN_W), (ATTN_W, 2 * ATTN_W), (2 * ATTN_W, QKV_W), (U_OFF, F_OFF), (F_OFF, IN_PAD))

    def body(dq_ref, dk_ref, dv_ref, du_ref, df_ref, w_ref, x_ref, r_ref, g_ref, d_ref, dx_ref, dg1_ref):
        @pl.when(pl.program_id(0) == 0)
        def _():
            dg1_ref[...] = jnp.zeros_like(dg1_ref)

        dh = None
        for ref, (c0, c1) in zip((dq_ref, dk_ref, dv_ref, du_ref, df_ref), pieces):
            t = lax.dot_general(ref[...], w_ref[:, c0:c1], NT, preferred_element_type=F32)
            dh = t if dh is None else dh + t
        dx, dg1 = _norm_bwd(dh, x_ref[...], r_ref[...], g_ref[...], d_ref[...])
        dx_ref[...] = dx
        dg1_ref[...] += dg1

    row = lambda w: pl.BlockSpec((tm, w), lambda i: (i, 0))
    full = lambda a, b: pl.BlockSpec((a, b), lambda i: (0, 0))
    return pl.pallas_call(
        body,
        grid=(s // tm,),
        in_specs=[row(ATTN_W), row(ATTN_W), row(ATTN_W), row(POOL_W), row(LANES), full(D_MODEL, IN_PAD),
                  row(D_MODEL), row(1), full(1, D_MODEL), row(D_MODEL)],
        out_specs=[row(D_MODEL), full(1, D_MODEL)],
        out_shape=[SDS((s, D_MODEL), F32), SDS((1, D_MODEL), F32)],
        compiler_params=_cparams("arbitrary"),
        name="in_bwd",
    )(dq, dk, dv, du, dfl, w_in_p, x, r1, g1, dx1)


def _tiles(s):
    big = min(512, s)
    return dict(row=big, attn=min(256, s // 2), mlp_bwd=min(256, s))


def _local_step(x, tgt, g1, w_in_p, b_f, w_pool, pool_scale, w_out, g2, wg, wu, wd, gf):
    s = x.shape[0]
    t = _tiles(s)
    tm, tq = t["row"], t["attn"]
    nb = s // LANES
    nqb = s // tq

    h, r1, qkv, u, fl = _norm_proj(x, g1, w_in_p, tm=tm)
    fl_t = fl[:, :N_HEADS].T.reshape(N_HEADS * nb, LANES)
    b_rows = jnp.repeat(b_f.reshape(N_HEADS), nb).reshape(N_HEADS * nb, 1)
    c = _forget_cumsum(fl_t, b_rows).reshape(N_PAIRS, 2, s)
    c_col = c.transpose(0, 2, 1)
    c_rowblk = c.reshape(N_PAIRS, 2, nqb, tq).transpose(0, 2, 1, 3)
    attn_o, lse = _attn_fwd(qkv, c_col, c_rowblk, tq=tq)
    pooled, pool_o = _pool_fwd(u, w_pool, pool_scale, tm=tm)
    x1, h2, r2 = _out_norm2(attn_o, pool_o, w_out, x, g2, tm=tm)
    gate, up, act = _gate_up(h2, wg, wu, tm=tm, tn=D_FF // 2)
    dx2, loss_row, d_gf = _down_final(act, wd, x1, gf, tgt, tm=tm)

    dgate, dup = _swiglu_bwd(dx2, wd, gate, up, tm=tm, tn=D_FF // 2)
    (d_wd,) = _mm_tn(act, [dx2], ta=D_FF // 2, ts=tm, name="grad_w_down")
    d_wg, d_wu = _mm_tn(h2, [dgate, dup], ta=tm, ts=tm, name="grad_w_gate_up")
    dx1, dmixed, d_g2 = _mlp_in_bwd(dgate, dup, wg, wu, w_out, x1, r2, g2, dx2, tm=t["mlp_bwd"])
    (d_wo_a,) = _mm_tn(attn_o, [dx1], ta=ATTN_W, ts=tm, name="grad_w_out_attn")
    (d_wo_p,) = _mm_tn(pool_o, [dx1], ta=POOL_W, ts=tm, name="grad_w_out_pool")
    du, d_wpool, d_pscale = _pool_bwd(dmixed, pooled, w_pool, pool_scale, tm=tm)
    rowb = (c_col - lse).reshape(N_PAIRS, nqb, tq, 2).transpose(0, 1, 3, 2)
    dq, dk, dv, dck, dcq = _attn_bwd(qkv, attn_o, dmixed, rowb, c_col, tq=tq)
    dcq = dcq[:, :, 0::4, :].transpose(0, 2, 1, 3).reshape(N_PAIRS, 2, s)
    dc_t = (dcq - dck.transpose(0, 2, 1)).reshape(N_HEADS * nb, LANES)
    dfl_t, db = _forget_bwd(dc_t, fl_t, b_rows)
    dfl = jnp.pad(dfl_t.reshape(N_HEADS, s).T, ((0, 0), (0, LANES - N_HEADS))).astype(BF16)
    d_wq, d_wk, d_wv, d_wu_in, d_wf = _mm_tn(h, [dq, dk, dv, du, dfl], ta=D_MODEL, ts=tm, name="grad_w_in")
    dx, d_g1 = _in_bwd(dq, dk, dv, du, dfl, w_in_p, x, r1, g1, dx1, tm=tm)

    d_w_in = jnp.concatenate([d_wq, d_wk, d_wv, d_wf[:, :N_HEADS], d_wu_in], axis=1)
    grads = dict(norm1_g=d_g1, w_in=d_w_in, b_forget=db[:, 0].reshape(1, N_HEADS), w_pool=d_wpool,
                 pool_scale=d_pscale, w_out=jnp.concatenate([d_wo_a, d_wo_p], axis=0), norm2_g=d_g2,
                 w_gate=d_wg, w_up=d_wu, w_down=d_wd, final_g=d_gf)
    return loss_row, dx, grads


def _my_index():
    return 4 * lax.axis_index("x") + 2 * lax.axis_index("y") + lax.axis_index("c")


def _peer(k):
    pos = [lax.axis_index(a) for a in ("x", "y", "c")]
    flipped = tuple(1 - p if (k >> b) & 1 else p for p, b in zip(pos, (2, 1, 0)))
    return flipped, 4 * flipped[0] + 2 * flipped[1] + flipped[2]


def _exchange(arrays, scatter, name):
    n = len(arrays)
    shapes = [a.shape[1:] if sc else a.shape for a, sc in zip(arrays, scatter)]

    def body(*refs):
        ins, outs = refs[:n], refs[n:2 * n]
        send_sems, recv_sems, local_sems = refs[2 * n:]
        me = _my_index()

        def src(w, idx):
            return ins[w].at[idx] if scatter[w] else ins[w]

        local = [pltpu.make_async_copy(src(w, me), outs[w].at[me], local_sems.at[w]) for w in range(n)]
        for cp in local:
            cp.start()
        sends, recvs = [], []
        for w in range(n):
            for k in range(1, N_DEV):
                dev, idx = _peer(k)
                sem = w * (N_DEV - 1) + k - 1
                sends.append(pltpu.make_async_remote_copy(
                    src_ref=src(w, idx), dst_ref=outs[w].at[me], send_sem=send_sems.at[sem],
                    recv_sem=recv_sems.at[sem], device_id=dev, device_id_type=MESH))
                recvs.append(pltpu.make_async_remote_copy(
                    src_ref=src(w, idx), dst_ref=outs[w].at[idx], send_sem=send_sems.at[sem],
                    recv_sem=recv_sems.at[sem], device_id=dev, device_id_type=MESH))
        for cp in sends:
            cp.start()
        for cp in recvs:
            cp.wait_recv()
        for cp in sends:
            cp.wait_send()
        for cp in local:
            cp.wait()

    any_spec = pl.BlockSpec(memory_space=pl.ANY)
    return pl.pallas_call(
        body,
        in_specs=[any_spec] * n,
        out_specs=[any_spec] * n,
        out_shape=[SDS((N_DEV,) + tuple(sh), a.dtype) for sh, a in zip(shapes, arrays)],
        scratch_shapes=[pltpu.SemaphoreType.DMA((n * (N_DEV - 1),)), pltpu.SemaphoreType.DMA((n * (N_DEV - 1),)),
                        pltpu.SemaphoreType.DMA((n,))],
        name=name,
    )(*arrays)


def _adamw(parts, w, m, v, name):
    rows, cols = w.shape
    tr = rows // 4 if rows % 32 == 0 else rows

    def body(p_ref, w_ref, m_ref, v_ref, g_ref, d_ref, mo_ref, vo_ref):
        g = p_ref[0]
        for d in range(1, N_DEV):
            g = g + p_ref[d]
        m_new = ADAM_B1 * m_ref[...] + (1.0 - ADAM_B1) * g
        v_new = ADAM_B2 * v_ref[...] + (1.0 - ADAM_B2) * (g * g)
        m_hat = m_new / (1.0 - ADAM_B1 ** ADAM_STEP)
        v_hat = v_new / (1.0 - ADAM_B2 ** ADAM_STEP)
        g_ref[...] = g
        d_ref[...] = -ADAM_LR * (m_hat / (jnp.sqrt(v_hat) + ADAM_EPS) + ADAM_WD * w_ref[...])
        mo_ref[...] = m_new
        vo_ref[...] = v_new

    blk = pl.BlockSpec((tr, cols), lambda i: (i, 0))
    return pl.pallas_call(
        body,
        grid=(rows // tr,),
        in_specs=[pl.BlockSpec((N_DEV, tr, cols), lambda i: (0, i, 0)), blk, blk, blk],
        out_specs=[blk] * 4,
        out_shape=[SDS((rows, cols), F32)] * 4,
        compiler_params=_cparams("arbitrary"),
        name=name,
    )(parts, w, m, v)


_SMALL = (("w_pool", 512), ("norm1_g", 8), ("norm2_g", 8), ("final_g", 8), ("pool_scale", 8), ("b_forget", 8))
_SMALL_ROWS = sum(r for _, r in _SMALL)


def _pack_small(vals):
    parts = []
    for name, rows in _SMALL:
        flat = vals[name].reshape(-1).astype(F32)
        parts.append(jnp.pad(flat, (0, rows * LANES - flat.shape[0])).reshape(rows, LANES))
    return jnp.concatenate(parts, axis=0)


def _unpack_small(packed, like):
    out, r0 = {}, 0
    for name, rows in _SMALL:
        n = like[name].size
        out[name] = packed[r0:r0 + rows].reshape(-1)[:n].reshape(like[name].shape)
        r0 += rows
    return out


def kernel(x, norm1_g, w_in, b_forget, w_pool, pool_scale, w_out, norm2_g, w_gate, w_up, w_down, final_g, loss_target, m_norm1_g, m_w_in, m_b_forget, m_w_pool, m_pool_scale, m_w_out, m_norm2_g, m_w_gate, m_w_up, m_w_down, m_final_g, v_norm1_g, v_w_in, v_b_forget, v_w_pool, v_pool_scale, v_w_out, v_norm2_g, v_w_gate, v_w_up, v_w_down, v_final_g):
    big = ("w_in", "w_out", "w_gate", "w_up", "w_down")
    order = ("norm1_g", "w_in", "b_forget", "w_pool", "pool_scale", "w_out", "norm2_g", "w_gate", "w_up", "w_down",
             "final_g")
    w = dict(norm1_g=norm1_g, w_in=w_in, b_forget=b_forget, w_pool=w_pool, pool_scale=pool_scale, w_out=w_out,
             norm2_g=norm2_g, w_gate=w_gate, w_up=w_up, w_down=w_down, final_g=final_g)
    m = dict(norm1_g=m_norm1_g, w_in=m_w_in, b_forget=m_b_forget, w_pool=m_w_pool, pool_scale=m_pool_scale,
             w_out=m_w_out, norm2_g=m_norm2_g, w_gate=m_w_gate, w_up=m_w_up, w_down=m_w_down, final_g=m_final_g)
    v = dict(norm1_g=v_norm1_g, w_in=v_w_in, b_forget=v_b_forget, w_pool=v_w_pool, pool_scale=v_pool_scale,
             w_out=v_w_out, norm2_g=v_norm2_g, w_gate=v_w_gate, w_up=v_w_up, w_down=v_w_down, final_g=v_final_g)

    gathered = _exchange([w[n][0].astype(BF16) for n in big], [False] * len(big), "gather_weights")
    gw = dict(zip(big, gathered))
    w_in_full = gw["w_in"].transpose(1, 0, 2).reshape(D_MODEL, IN_W)
    f0 = QKV_W + N_HEADS
    w_in_p = jnp.concatenate([w_in_full[:, :QKV_W], w_in_full[:, f0:], w_in_full[:, QKV_W:f0],
                              jnp.zeros((D_MODEL, IN_PAD - IN_W), BF16)], axis=1)
    w_out_full = gw["w_out"].reshape(D_MODEL, D_MODEL)
    wg_full = gw["w_gate"].transpose(1, 0, 2).reshape(D_MODEL, D_FF)
    wu_full = gw["w_up"].transpose(1, 0, 2).reshape(D_MODEL, D_FF)
    wd_full = gw["w_down"].reshape(D_FF, D_MODEL)

    loss_row, dx, grads = _local_step(x[0], loss_target[0], norm1_g, w_in_p, b_forget[0], w_pool[0], pool_scale,
                                      w_out_full, norm2_g, wg_full, wu_full, wd_full, final_g.reshape(1, D_MODEL))
    loss = lax.psum(0.5 / D_MODEL * jnp.sum(loss_row), ("x", "y", "c"))

    cols = lambda g: g.reshape(g.shape[0], N_DEV, g.shape[1] // N_DEV).transpose(1, 0, 2)
    rows = lambda g: g.reshape(N_DEV, g.shape[0] // N_DEV, g.shape[1])
    slots = [cols(grads["w_in"]), rows(grads["w_out"]), cols(grads["w_gate"]), cols(grads["w_up"]),
             rows(grads["w_down"])]
    received = _exchange(slots + [_pack_small(grads)], [True] * len(big) + [False], "exchange_grads")

    outs = {}
    for name, parts in zip(big, received[:-1]):
        outs[name] = [a[None] for a in _adamw(parts, w[name][0], m[name][0], v[name][0], "adamw_" + name)]
    small = _adamw(received[-1], _pack_small(w), _pack_small(m), _pack_small(v), "adamw_replicated")
    small = [_unpack_small(p, w) for p in small]
    for name in order:
        if name not in outs:
            outs[name] = [p[name] for p in small]

    return (loss, dx[None]) + tuple(outs[n][k] for k in range(4) for n in order)
```

```python
import functools

import jax
import jax.numpy as jnp
from jax import lax
from jax.experimental import pallas as pl
from jax.experimental.pallas import tpu as pltpu

F32 = jnp.float32
BF16 = jnp.bfloat16
SDS = jax.ShapeDtypeStruct

D_MODEL = 1024
ATTN_W = 512
N_HEADS = 8
HEAD_DIM = 64
N_PAIRS = N_HEADS // 2
POOL_W = 512
POOL_WINDOWS = (2, 4, 8, 16)
POOL_G = 128
HALO = 16
IN_W = 3 * ATTN_W + N_HEADS + POOL_W
QKV_W = 3 * ATTN_W
U_OFF = QKV_W
F_OFF = QKV_W + POOL_W
IN_PAD = F_OFF + 128
D_FF = 2816
EPS = 1e-6
NEG = -1e30
N_DEV = 8
LANES = 128

ADAM_LR = 0.001
ADAM_B1 = 0.9
ADAM_B2 = 0.999
ADAM_EPS = 1e-08
ADAM_WD = 0.01
ADAM_STEP = 10

VMEM_LIMIT_BYTES = 56 * 1024 * 1024
MESH = pl.DeviceIdType.MESH
NT = (((1,), (1,)), ((), ()))
TN = (((0,), (0,)), ((), ()))


def _cparams(*sem):
    return pltpu.CompilerParams(dimension_semantics=sem or None, vmem_limit_bytes=VMEM_LIMIT_BYTES)


def _split3(a):
    hi = a.astype(BF16)
    r1 = a - hi.astype(F32)
    mid = r1.astype(BF16)
    lo = (r1 - mid.astype(F32)).astype(BF16)
    return hi, mid, lo


def _dot_sel(a, sel, dims=None):
    sb = sel.astype(BF16)
    if dims is None:
        return sum(jnp.dot(p, sb, preferred_element_type=F32) for p in _split3(a))
    return sum(lax.dot_general(p, sb, dims, preferred_element_type=F32) for p in _split3(a))


def _sel_dot(sel, a, dims=None):
    sb = sel.astype(BF16)
    if dims is None:
        return sum(jnp.dot(sb, p, preferred_element_type=F32) for p in _split3(a))
    return sum(lax.dot_general(sb, p, dims, preferred_element_type=F32) for p in _split3(a))


def _iota2(shape, dim):
    return lax.broadcasted_iota(jnp.int32, shape, dim)


def _norm_proj(x, g1, w_in_p, *, tm):
    s = x.shape[0]

    def body(x_ref, g_ref, w_ref, h_ref, r_ref, qkv_ref, u_ref, fl_ref):
        xv = x_ref[...]
        r = lax.rsqrt(jnp.mean(xv * xv, axis=-1, keepdims=True) + EPS)
        h = (xv * r * g_ref[...]).astype(BF16)
        h_ref[...] = h
        r_ref[...] = r
        qkv_ref[...] = jnp.dot(h, w_ref[:, 0:QKV_W], preferred_element_type=F32).astype(BF16)
        u_ref[...] = jnp.dot(h, w_ref[:, U_OFF:F_OFF], preferred_element_type=F32)
        fl_ref[...] = jnp.dot(h, w_ref[:, F_OFF:IN_PAD], preferred_element_type=F32)

    row = lambda w: pl.BlockSpec((tm, w), lambda i: (i, 0))
    full = lambda a, b: pl.BlockSpec((a, b), lambda i: (0, 0))
    return pl.pallas_call(
        body,
        grid=(s // tm,),
        in_specs=[row(D_MODEL), full(1, D_MODEL), full(D_MODEL, IN_PAD)],
        out_specs=[row(D_MODEL), row(1), row(QKV_W), row(POOL_W), row(LANES)],
        out_shape=[SDS((s, D_MODEL), BF16), SDS((s, 1), F32), SDS((s, QKV_W), BF16), SDS((s, POOL_W), F32),
                   SDS((s, LANES), F32)],
        compiler_params=_cparams("arbitrary"),
        name="norm_proj",
    )(x, g1, w_in_p)


def _head_block_masks(rows, nb):
    shift = nb.bit_length() - 1
    rr, cc = _iota2((rows, rows), 0), _iota2((rows, rows), 1)
    same = lax.shift_right_logical(rr, shift) == lax.shift_right_logical(cc, shift)
    return rr, cc, same


def _forget_cumsum(fl_t, b_rows):
    rows = fl_t.shape[0]
    nb = rows // N_HEADS

    def body(fl_ref, b_ref, c_ref):
        z = fl_ref[...] + b_ref[...]
        lf = jnp.minimum(z, 0.0) - jnp.log1p(jnp.exp(-jnp.abs(z)))
        upper = _iota2((LANES, LANES), 0) <= _iota2((LANES, LANES), 1)
        within = _dot_sel(lf, upper)
        tot = _dot_sel(lf, jnp.ones((LANES, LANES), F32))
        rr, cc, same = _head_block_masks(rows, nb)
        c_ref[...] = within + _sel_dot(same & (cc < rr), tot)

    return pl.pallas_call(body, out_shape=SDS(fl_t.shape, F32), compiler_params=_cparams(), name="forget_cumsum")(
        fl_t, b_rows)


def _attn_fwd(qkv, c_col, c_rowblk, *, tq):
    s = qkv.shape[0]
    tk = tq
    nb = s // tq

    def body(q_ref, k_ref, v_ref, cq_ref, ck_ref, o_ref, lse_ref):
        i = pl.program_id(1)
        lane = _iota2((tq, LANES), 1)
        q2 = q_ref[...]
        zq = jnp.zeros_like(q2)
        qh = (jnp.where(lane < HEAD_DIM, q2, zq), jnp.where(lane >= HEAD_DIM, q2, zq))
        cq = cq_ref[...]

        def step(j, carry, masked):
            start = pl.multiple_of(j * tk, tk)
            k2 = k_ref[pl.ds(start, tk), :]
            v2 = v_ref[pl.ds(start, tk), :]
            ckb = ck_ref[j]
            out = []
            for h in range(2):
                m, l, acc = carry[h]
                sc = lax.dot_general(qh[h], k2, NT, preferred_element_type=F32) * 0.125
                sc = sc + cq[:, h:h + 1] - ckb[h:h + 1, :]
                if masked:
                    sc = jnp.where(_iota2((tq, tk), 1) <= _iota2((tq, tk), 0), sc, NEG)
                m_new = jnp.maximum(m, jnp.max(sc, axis=1, keepdims=True))
                alpha = jnp.exp(m - m_new)
                p = jnp.exp(sc - m_new)
                l = alpha * l + jnp.sum(p, axis=1, keepdims=True)
                acc = alpha * acc + jnp.dot(p.astype(BF16), v2, preferred_element_type=F32)
                out.append((m_new, l, acc))
            return tuple(out)

        init = tuple((jnp.full((tq, 1), NEG, F32), jnp.zeros((tq, 1), F32), jnp.zeros((tq, LANES), F32))
                     for _ in range(2))
        carry = lax.fori_loop(0, i, lambda j, c: step(j, c, False), init)
        (ma, la, acca), (mb, lb, accb) = step(i, carry, True)
        o_ref[...] = jnp.where(lane < HEAD_DIM, acca / la, accb / lb).astype(BF16)
        lse_ref[...] = jnp.where(_iota2((tq, 2), 1) == 0, ma + jnp.log(la), mb + jnp.log(lb))

    return pl.pallas_call(
        body,
        grid=(N_PAIRS, nb),
        in_specs=[
            pl.BlockSpec((tq, LANES), lambda p, i: (i, p)),
            pl.BlockSpec((s, LANES), lambda p, i: (0, N_PAIRS + p)),
            pl.BlockSpec((s, LANES), lambda p, i: (0, 2 * N_PAIRS + p)),
            pl.BlockSpec((None, tq, 2), lambda p, i: (p, i, 0)),
            pl.BlockSpec((None, nb, 2, tk), lambda p, i: (p, 0, 0, 0)),
        ],
        out_specs=[
            pl.BlockSpec((tq, LANES), lambda p, i: (i, p)),
            pl.BlockSpec((None, tq, 2), lambda p, i: (p, i, 0)),
        ],
        out_shape=[SDS((s, ATTN_W), BF16), SDS((N_PAIRS, s, 2), F32)],
        compiler_params=_cparams("arbitrary", "arbitrary"),
        name="attn_fwd",
    )(qkv, qkv, qkv, c_col, c_rowblk)


def _pool_counts(row0, tm, w):
    t = row0 + _iota2((tm, 1), 0)
    return jnp.minimum(t + 1, w).astype(F32)


def _pool_fwd(u, w_pool, pool_scale, *, tm):
    s = u.shape[0]

    def body(u_ref, w_ref, sc_ref, pooled_ref, po_ref, tail_ref):
        i = pl.program_id(0)

        @pl.when(i == 0)
        def _():
            tail_ref[...] = jnp.zeros_like(tail_ref)

        uv = u_ref[...]
        ext = jnp.concatenate([tail_ref[...], uv], axis=0)
        tail_ref[...] = uv[tm - HALO:, :]
        for g, w in enumerate(POOL_WINDOWS):
            cols = slice(g * POOL_G, (g + 1) * POOL_G)
            acc = ext[:, cols]
            k = 1
            while k < w:
                acc = acc + pltpu.roll(acc, k, axis=0)
                k *= 2
            pooled = (acc[HALO:, :] / _pool_counts(i * tm, tm, w) - uv[:, cols]).astype(BF16)
            pooled_ref[:, cols] = pooled
            mixed = jnp.dot(pooled, w_ref[g].astype(BF16), preferred_element_type=F32)
            po_ref[:, cols] = (mixed * sc_ref[:, cols]).astype(BF16)

    row = pl.BlockSpec((tm, POOL_W), lambda i: (i, 0))
    return pl.pallas_call(
        body,
        grid=(s // tm,),
        in_specs=[row, pl.BlockSpec((len(POOL_WINDOWS), POOL_G, POOL_G), lambda i: (0, 0, 0)),
                  pl.BlockSpec((1, POOL_W), lambda i: (0, 0))],
        out_specs=[row, row],
        out_shape=[SDS((s, POOL_W), BF16), SDS((s, POOL_W), BF16)],
        scratch_shapes=[pltpu.VMEM((HALO, POOL_W), F32)],
        compiler_params=_cparams("arbitrary"),
        name="pool_fwd",
    )(u, w_pool, pool_scale)


def _out_norm2(attn_o, pool_o, w_out, x, g2, *, tm):
    s = x.shape[0]

    def body(a_ref, p_ref, w_ref, x_ref, g_ref, x1_ref, h2_ref, r_ref):
        x1 = (x_ref[...] + jnp.dot(a_ref[...], w_ref[0:ATTN_W, :], preferred_element_type=F32)
              + jnp.dot(p_ref[...], w_ref[ATTN_W:, :], preferred_element_type=F32))
        r = lax.rsqrt(jnp.mean(x1 * x1, axis=-1, keepdims=True) + EPS)
        x1_ref[...] = x1
        r_ref[...] = r
        h2_ref[...] = (x1 * r * g_ref[...]).astype(BF16)

    row = lambda w: pl.BlockSpec((tm, w), lambda i: (i, 0))
    full = lambda a, b: pl.BlockSpec((a, b), lambda i: (0, 0))
    return pl.pallas_call(
        body,
        grid=(s // tm,),
        in_specs=[row(ATTN_W), row(POOL_W), full(D_MODEL, D_MODEL), row(D_MODEL), full(1, D_MODEL)],
        out_specs=[row(D_MODEL), row(D_MODEL), row(1)],
        out_shape=[SDS((s, D_MODEL), F32), SDS((s, D_MODEL), BF16), SDS((s, 1), F32)],
        compiler_params=_cparams("arbitrary"),
        name="out_norm2",
    )(attn_o, pool_o, w_out, x, g2)


def _gate_up(h2, wg, wu, *, tm, tn):
    s = h2.shape[0]

    def body(h_ref, wg_ref, wu_ref, gate_ref, up_ref, act_ref):
        h = h_ref[...]
        gate = jnp.dot(h, wg_ref[...], preferred_element_type=F32)
        up = jnp.dot(h, wu_ref[...], preferred_element_type=F32)
        gate_ref[...] = gate
        up_ref[...] = up
        act_ref[...] = (gate * jax.nn.sigmoid(gate) * up).astype(BF16)

    wspec = pl.BlockSpec((D_MODEL, tn), lambda c, r: (0, c))
    ospec = pl.BlockSpec((tm, tn), lambda c, r: (r, c))
    return pl.pallas_call(
        body,
        grid=(D_FF // tn, s // tm),
        in_specs=[pl.BlockSpec((tm, D_MODEL), lambda c, r: (r, 0)), wspec, wspec],
        out_specs=[ospec, ospec, ospec],
        out_shape=[SDS((s, D_FF), F32), SDS((s, D_FF), F32), SDS((s, D_FF), BF16)],
        compiler_params=_cparams("arbitrary", "arbitrary"),
        name="gate_up",
    )(h2, wg, wu)


def _down_final(act, wd, x1, gf, tgt, *, tm):
    s = x1.shape[0]

    def body(a_ref, w_ref, x1_ref, g_ref, t_ref, dx2_ref, loss_ref, dgf_ref):
        @pl.when(pl.program_id(0) == 0)
        def _():
            loss_ref[...] = jnp.zeros_like(loss_ref)
            dgf_ref[...] = jnp.zeros_like(dgf_ref)

        x2 = x1_ref[...] + jnp.dot(a_ref[...], w_ref[...], preferred_element_type=F32)
        r = lax.rsqrt(jnp.mean(x2 * x2, axis=-1, keepdims=True) + EPS)
        xn = x2 * r
        g = g_ref[...]
        diff = xn * g - t_ref[...]
        loss_ref[...] += jnp.sum(diff * diff, axis=0, keepdims=True)
        dy = diff * (1.0 / D_MODEL)
        dgf_ref[...] += jnp.sum(dy * xn, axis=0, keepdims=True)
        dxn = dy * g
        dx2_ref[...] = r * (dxn - xn * jnp.mean(dxn * xn, axis=-1, keepdims=True))

    row = lambda w: pl.BlockSpec((tm, w), lambda i: (i, 0))
    full = lambda a, b: pl.BlockSpec((a, b), lambda i: (0, 0))
    return pl.pallas_call(
        body,
        grid=(s // tm,),
        in_specs=[row(D_FF), full(D_FF, D_MODEL), row(D_MODEL), full(1, D_MODEL), row(D_MODEL)],
        out_specs=[row(D_MODEL), full(1, D_MODEL), full(1, D_MODEL)],
        out_shape=[SDS((s, D_MODEL), F32), SDS((1, D_MODEL), F32), SDS((1, D_MODEL), F32)],
        compiler_params=_cparams("arbitrary"),
        name="down_final",
    )(act, wd, x1, gf, tgt)


def _swiglu_bwd(dx2, wd, gate, up, *, tm, tn):
    s = dx2.shape[0]

    def body(d_ref, w_ref, gate_ref, up_ref, dgate_ref, dup_ref):
        dact = lax.dot_general(d_ref[...].astype(BF16), w_ref[...], NT, preferred_element_type=F32)
        gate = gate_ref[...]
        sg = jax.nn.sigmoid(gate)
        dup_ref[...] = (dact * (gate * sg)).astype(BF16)
        dgate_ref[...] = (dact * up_ref[...] * (sg * (1.0 + gate * (1.0 - sg)))).astype(BF16)

    ospec = pl.BlockSpec((tm, tn), lambda c, r: (r, c))
    return pl.pallas_call(
        body,
        grid=(D_FF // tn, s // tm),
        in_specs=[pl.BlockSpec((tm, D_MODEL), lambda c, r: (r, 0)), pl.BlockSpec((tn, D_MODEL), lambda c, r: (c, 0)),
                  ospec, ospec],
        out_specs=[ospec, ospec],
        out_shape=[SDS((s, D_FF), BF16), SDS((s, D_FF), BF16)],
        compiler_params=_cparams("arbitrary", "arbitrary"),
        name="swiglu_bwd",
    )(dx2, wd, gate, up)


def _mm_tn(a, bs, *, ta, ts, name):
    s, ka = a.shape
    n = len(bs)

    def body(a_ref, *refs):
        b_refs, o_refs = refs[:n], refs[n:]

        @pl.when(pl.program_id(1) == 0)
        def _():
            for o_ref in o_refs:
                o_ref[...] = jnp.zeros_like(o_ref)

        av = a_ref[...].astype(BF16)
        for b_ref, o_ref in zip(b_refs, o_refs):
            o_ref[...] += lax.dot_general(av, b_ref[...].astype(BF16), TN, preferred_element_type=F32)

    return pl.pallas_call(
        body,
        grid=(ka // ta, s // ts),
        in_specs=[pl.BlockSpec((ts, ta), lambda i, k: (k, i))]
        + [pl.BlockSpec((ts, b.shape[1]), lambda i, k: (k, 0)) for b in bs],
        out_specs=[pl.BlockSpec((ta, b.shape[1]), lambda i, k: (i, 0)) for b in bs],
        out_shape=[SDS((ka, b.shape[1]), F32) for b in bs],
        compiler_params=_cparams("arbitrary", "arbitrary"),
        name=name,
    )(a, *bs)


def _norm_bwd(dh, x, r, g, dres):
    xn = x * r
    dxn = dh * g
    dx = dres + r * (dxn - xn * jnp.mean(dxn * xn, axis=-1, keepdims=True))
    return dx, jnp.sum(dh * xn, axis=0, keepdims=True)


def _mlp_in_bwd(dgate, dup, wg, wu, w_out, x1, r2, g2, dx2, *, tm):
    s = x1.shape[0]

    def body(dg_ref, du_ref, wg_ref, wu_ref, wo_ref, x_ref, r_ref, g_ref, d_ref, dx1_ref, dmix_ref, dg2_ref):
        @pl.when(pl.program_id(0) == 0)
        def _():
            dg2_ref[...] = jnp.zeros_like(dg2_ref)

        dh2 = (lax.dot_general(dg_ref[...], wg_ref[...], NT, preferred_element_type=F32)
               + lax.dot_general(du_ref[...], wu_ref[...], NT, preferred_element_type=F32))
        dx1, dg2 = _norm_bwd(dh2, x_ref[...], r_ref[...], g_ref[...], d_ref[...])
        dg2_ref[...] += dg2
        dx1_ref[...] = dx1
        dmix_ref[...] = lax.dot_general(dx1.astype(BF16), wo_ref[...], NT, preferred_element_type=F32)

    row = lambda w: pl.BlockSpec((tm, w), lambda i: (i, 0))
    full = lambda a, b: pl.BlockSpec((a, b), lambda i: (0, 0))
    return pl.pallas_call(
        body,
        grid=(s // tm,),
        in_specs=[row(D_FF), row(D_FF), full(D_MODEL, D_FF), full(D_MODEL, D_FF), full(D_MODEL, D_MODEL),
                  row(D_MODEL), row(1), full(1, D_MODEL), row(D_MODEL)],
        out_specs=[row(D_MODEL), row(D_MODEL), full(1, D_MODEL)],
        out_shape=[SDS((s, D_MODEL), F32), SDS((s, D_MODEL), F32), SDS((1, D_MODEL), F32)],
        compiler_params=_cparams("arbitrary"),
        name="mlp_in_bwd",
    )(dgate, dup, wg, wu, w_out, x1, r2, g2, dx2)


def _pool_bwd(dmixed, pooled, w_pool, pool_scale, *, tm):
    s = pooled.shape[0]
    nt = s // tm
    ng = len(POOL_WINDOWS)

    def body(d_ref, p_ref, w_ref, sc_ref, du_ref, dw_ref, dsc_ref, head_ref):
        i = pl.program_id(0)

        @pl.when(i == 0)
        def _():
            head_ref[...] = jnp.zeros_like(head_ref)
            dw_ref[...] = jnp.zeros_like(dw_ref)
            dsc_ref[...] = jnp.zeros_like(dsc_ref)

        row0 = (nt - 1 - i) * tm
        for g, w in enumerate(POOL_WINDOWS):
            cols = slice(g * POOL_G, (g + 1) * POOL_G)
            wb = w_ref[g].astype(BF16)
            pooled_g = p_ref[:, cols]
            dpo = d_ref[:, cols]
            mixed = jnp.dot(pooled_g, wb, preferred_element_type=F32)
            dsc_ref[:, cols] += jnp.sum(dpo * mixed, axis=0, keepdims=True)
            dmp = (dpo * sc_ref[:, cols]).astype(BF16)
            dw_ref[g] += lax.dot_general(pooled_g, dmp, TN, preferred_element_type=F32)
            dpooled = lax.dot_general(dmp, wb, NT, preferred_element_type=F32)
            a = dpooled / _pool_counts(row0, tm, w)
            acc = jnp.concatenate([a, head_ref[:, cols]], axis=0)
            head_ref[:, cols] = a[0:HALO, :]
            k = 1
            while k < w:
                acc = acc + pltpu.roll(acc, tm + HALO - k, axis=0)
                k *= 2
            du_ref[:, cols] = (acc[0:tm, :] - dpooled).astype(BF16)

    rev = lambda i: (nt - 1 - i, 0)
    return pl.pallas_call(
        body,
        grid=(nt,),
        in_specs=[pl.BlockSpec((tm, POOL_W), lambda i: (nt - 1 - i, 1)), pl.BlockSpec((tm, POOL_W), rev),
                  pl.BlockSpec((ng, POOL_G, POOL_G), lambda i: (0, 0, 0)), pl.BlockSpec((1, POOL_W), lambda i: (0, 0))],
        out_specs=[pl.BlockSpec((tm, POOL_W), rev), pl.BlockSpec((ng, POOL_G, POOL_G), lambda i: (0, 0, 0)),
                   pl.BlockSpec((1, POOL_W), lambda i: (0, 0))],
        out_shape=[SDS((s, POOL_W), BF16), SDS((ng, POOL_G, POOL_G), F32), SDS((1, POOL_W), F32)],
        scratch_shapes=[pltpu.VMEM((HALO, POOL_W), F32)],
        compiler_params=_cparams("arbitrary"),
        name="pool_bwd",
    )(dmixed, pooled, w_pool, pool_scale)


def _attn_bwd(qkv, attn_o, dmixed, rowb, ck_col, *, tq):
    s = qkv.shape[0]
    tk = tq
    nb = s // tq

    def body(q_ref, k_ref, v_ref, o_ref, do_ref, rowb_ref, ck_ref, dq_ref, dk_ref, dv_ref, dck_ref, dcq_ref, dq_acc,
             delta_ref):
        lane = _iota2((tq, LANES), 1)
        lo = lane < HEAD_DIM
        first = _iota2((8, LANES), 1) < HEAD_DIM
        sel = jnp.where(_iota2((8, LANES), 0) < 4, jnp.where(first, 1.0, 0.0), jnp.where(first, 0.0, 1.0))
        row8 = _iota2((8, tq), 0)

        def prep(i, _):
            st = pl.multiple_of(i * tq, tq)
            prod = do_ref[pl.ds(st, tq), :] * o_ref[pl.ds(st, tq), :].astype(F32)
            delta_ref[i] = _sel_dot(sel, prod, NT)
            dq_acc[pl.ds(st, tq), :] = jnp.zeros((tq, LANES), F32)
            dcq_ref[i] = jnp.zeros((8, tq), F32)
            return 0

        lax.fori_loop(0, nb, prep, 0)

        def split(t):
            z = jnp.zeros_like(t)
            return jnp.where(lo, t, z), jnp.where(lo, z, t)

        def kv_block(j, _):
            st_j = pl.multiple_of(j * tk, tk)
            ks = split(k_ref[pl.ds(st_j, tk), :])
            vs = split(v_ref[pl.ds(st_j, tk), :])
            kcat = jnp.concatenate(ks, axis=0)
            ck = ck_ref[pl.ds(st_j, tk), :]

            def q_block(i, carry, masked):
                dk_acc, dv_acc, dca, dcb = carry
                st_i = pl.multiple_of(i * tq, tq)
                q2 = q_ref[pl.ds(st_i, tq), :]
                do2 = do_ref[pl.ds(st_i, tq), :].astype(BF16)
                rb = rowb_ref[i]
                dl = delta_ref[i]
                pts, dsts = [], []
                for h in range(2):
                    st = lax.dot_general(ks[h], q2, NT, preferred_element_type=F32) * 0.125
                    st = st + rb[h:h + 1, :] - ck[:, h:h + 1]
                    if masked:
                        st = jnp.where(_iota2((tk, tq), 0) <= _iota2((tk, tq), 1), st, NEG)
                    pt = jnp.exp(st)
                    dpt = lax.dot_general(vs[h], do2, NT, preferred_element_type=F32)
                    pts.append(pt.astype(BF16))
                    dsts.append(pt * (dpt - dl[4 * h:4 * h + 1, :]))
                dca = dca + jnp.sum(dsts[0], axis=1, keepdims=True)
                dcb = dcb + jnp.sum(dsts[1], axis=1, keepdims=True)
                dcq_ref[i] += jnp.where(row8 == 0, jnp.sum(dsts[0], axis=0, keepdims=True),
                                        jnp.where(row8 == 4, jnp.sum(dsts[1], axis=0, keepdims=True), 0.0))
                dsb = [d.astype(BF16) for d in dsts]
                dv_acc = dv_acc + jnp.dot(jnp.concatenate(pts, axis=1), jnp.concatenate(split(do2), axis=0),
                                          preferred_element_type=F32)
                dk_acc = dk_acc + jnp.dot(jnp.concatenate(dsb, axis=1), jnp.concatenate(split(q2), axis=0),
                                          preferred_element_type=F32)
                dq_acc[pl.ds(st_i, tq), :] += lax.dot_general(jnp.concatenate(dsb, axis=0), kcat, TN,
                                                              preferred_element_type=F32)
                return dk_acc, dv_acc, dca, dcb

            zt = jnp.zeros((tk, LANES), F32)
            zc = jnp.zeros((tk, 1), F32)
            carry = q_block(j, (zt, zt, zc, zc), True)
            dk_acc, dv_acc, dca, dcb = lax.fori_loop(j + 1, nb, lambda i, c: q_block(i, c, False), carry)
            dk_ref[pl.ds(st_j, tk), :] = (dk_acc * 0.125).astype(BF16)
            dv_ref[pl.ds(st_j, tk), :] = dv_acc.astype(BF16)
            dck_ref[pl.ds(st_j, tk), :] = jnp.where(_iota2((tk, 2), 1) == 0, dca, dcb)
            return 0

        lax.fori_loop(0, nb, kv_block, 0)
        dq_ref[...] = (dq_acc[...] * 0.125).astype(BF16)

    col = lambda off: pl.BlockSpec((s, LANES), lambda p: (0, off + p))
    return pl.pallas_call(
        body,
        grid=(N_PAIRS,),
        in_specs=[col(0), col(N_PAIRS), col(2 * N_PAIRS), col(0), col(0),
                  pl.BlockSpec((None, nb, 2, tq), lambda p: (p, 0, 0, 0)),
                  pl.BlockSpec((None, s, 2), lambda p: (p, 0, 0))],
        out_specs=[col(0), col(0), col(0), pl.BlockSpec((None, s, 2), lambda p: (p, 0, 0)),
                   pl.BlockSpec((None, nb, 8, tq), lambda p: (p, 0, 0, 0))],
        out_shape=[SDS((s, ATTN_W), BF16), SDS((s, ATTN_W), BF16), SDS((s, ATTN_W), BF16), SDS((N_PAIRS, s, 2), F32),
                   SDS((N_PAIRS, nb, 8, tq), F32)],
        scratch_shapes=[pltpu.VMEM((s, LANES), F32), pltpu.VMEM((nb, 8, tq), F32)],
        compiler_params=_cparams("arbitrary"),
        name="attn_bwd",
    )(qkv, qkv, qkv, attn_o, dmixed, rowb, ck_col)


def _forget_bwd(dc_t, fl_t, b_rows):
    rows = fl_t.shape[0]
    nb = rows // N_HEADS

    def body(dc_ref, fl_ref, b_ref, dfl_ref, db_ref):
        dc = dc_ref[...]
        lower = _iota2((LANES, LANES), 0) >= _iota2((LANES, LANES), 1)
        ones = jnp.ones((LANES, LANES), F32)
        rr, cc, same = _head_block_masks(rows, nb)
        dlf = _dot_sel(dc, lower) + _sel_dot(same & (cc > rr), _dot_sel(dc, ones))
        dfl = dlf / (1.0 + jnp.exp(fl_ref[...] + b_ref[...]))
        dfl_ref[...] = dfl
        shift = nb.bit_length() - 1
        hsel = lax.shift_right_logical(_iota2((N_HEADS, rows), 1), shift) == _iota2((N_HEADS, rows), 0)
        db_ref[...] = _sel_dot(hsel, _dot_sel(dfl, ones))

    return pl.pallas_call(body, out_shape=[SDS(fl_t.shape, F32), SDS((N_HEADS, LANES), F32)],
                          compiler_params=_cparams(), name="forget_bwd")(dc_t, fl_t, b_rows)


def _in_bwd(dq, dk, dv, du, dfl, w_in_p, x, r1, g1, dx1, *, tm):
    s = x.shape[0]
    pieces = ((0, ATTN_W), (ATTN_W, 2 * ATTN_W), (2 * ATTN_W, QKV_W), (U_OFF, F_OFF), (F_OFF, IN_PAD))

    def body(dq_ref, dk_ref, dv_ref, du_ref, df_ref, w_ref, x_ref, r_ref, g_ref, d_ref, dx_ref, dg1_ref):
        @pl.when(pl.program_id(0) == 0)
        def _():
            dg1_ref[...] = jnp.zeros_like(dg1_ref)

        dh = None
        for ref, (c0, c1) in zip((dq_ref, dk_ref, dv_ref, du_ref, df_ref), pieces):
            t = lax.dot_general(ref[...], w_ref[:, c0:c1], NT, preferred_element_type=F32)
            dh = t if dh is None else dh + t
        dx, dg1 = _norm_bwd(dh, x_ref[...], r_ref[...], g_ref[...], d_ref[...])
        dx_ref[...] = dx
        dg1_ref[...] += dg1

    row = lambda w: pl.BlockSpec((tm, w), lambda i: (i, 0))
    full = lambda a, b: pl.BlockSpec((a, b), lambda i: (0, 0))
    return pl.pallas_call(
        body,
        grid=(s // tm,),
        in_specs=[row(ATTN_W), row(ATTN_W), row(ATTN_W), row(POOL_W), row(LANES), full(D_MODEL, IN_PAD),
                  row(D_MODEL), row(1), full(1, D_MODEL), row(D_MODEL)],
        out_specs=[row(D_MODEL), full(1, D_MODEL)],
        out_shape=[SDS((s, D_MODEL), F32), SDS((1, D_MODEL), F32)],
        compiler_params=_cparams("arbitrary"),
        name="in_bwd",
    )(dq, dk, dv, du, dfl, w_in_p, x, r1, g1, dx1)


def _tiles(s):
    big = min(512, s)
    return dict(row=big, attn=min(256, s // 2), mlp_bwd=min(256, s))


def _tie(a, token):
    return a + token[0:1, 0:1].astype(a.dtype)


def _local_step(x, tgt, p, weight, emit):
    s = x.shape[0]
    t = _tiles(s)
    tm, tq = t["row"], t["attn"]
    nb = s // LANES
    nqb = s // tq
    g1, g2, gf = p["norm1_g"], p["norm2_g"], p["final_g"].reshape(1, D_MODEL)
    w_pool, pool_scale = p["w_pool"][0], p["pool_scale"]

    w_in_p = weight("w_in", x)
    h, r1, qkv, u, fl = _norm_proj(x, g1, w_in_p, tm=tm)
    fl_t = fl[:, :N_HEADS].T.reshape(N_HEADS * nb, LANES)
    b_rows = jnp.repeat(p["b_forget"].reshape(N_HEADS), nb).reshape(N_HEADS * nb, 1)
    c = _forget_cumsum(fl_t, b_rows).reshape(N_PAIRS, 2, s)
    c_col = c.transpose(0, 2, 1)
    c_rowblk = c.reshape(N_PAIRS, 2, nqb, tq).transpose(0, 2, 1, 3)
    attn_o, lse = _attn_fwd(qkv, c_col, c_rowblk, tq=tq)
    pooled, pool_o = _pool_fwd(u, w_pool, pool_scale, tm=tm)
    w_out = weight("w_out", attn_o)
    x1, h2, r2 = _out_norm2(attn_o, pool_o, w_out, x, g2, tm=tm)
    wg, wu = weight("w_gate_up", h2)
    gate, up, act = _gate_up(h2, wg, wu, tm=tm, tn=D_FF // 2)
    wd = weight("w_down", act)
    dx2, loss_row, d_gf = _down_final(act, wd, x1, gf, tgt, tm=tm)

    dgate, dup = _swiglu_bwd(dx2, wd, gate, up, tm=tm, tn=D_FF // 2)
    (d_wd,) = _mm_tn(act, [dx2], ta=D_FF // 2, ts=tm, name="grad_w_down")
    token = emit("w_down", d_wd)
    dx1, dmixed, d_g2 = _mlp_in_bwd(dgate, dup, wg, wu, w_out, x1, r2, _tie(g2, token), dx2, tm=t["mlp_bwd"])
    token = emit("w_gate_up", _mm_tn(h2, [dgate, dup], ta=tm, ts=tm, name="grad_w_gate_up"))
    du, d_wpool, d_pscale = _pool_bwd(dmixed, pooled, w_pool, _tie(pool_scale, token), tm=tm)
    (d_wo_a,) = _mm_tn(attn_o, [dx1], ta=ATTN_W, ts=tm, name="grad_w_out_attn")
    (d_wo_p,) = _mm_tn(pool_o, [dx1], ta=POOL_W, ts=tm, name="grad_w_out_pool")
    token = emit("w_out", jnp.concatenate([d_wo_a, d_wo_p], axis=0))
    rowb = _tie(c_col - lse, token).reshape(N_PAIRS, nqb, tq, 2).transpose(0, 1, 3, 2)
    dq, dk, dv, dck, dcq = _attn_bwd(qkv, attn_o, dmixed, rowb, c_col, tq=tq)
    dcq = dcq[:, :, 0::4, :].transpose(0, 2, 1, 3).reshape(N_PAIRS, 2, s)
    dc_t = (dcq - dck.transpose(0, 2, 1)).reshape(N_HEADS * nb, LANES)
    dfl_t, db = _forget_bwd(dc_t, fl_t, b_rows)
    dfl = jnp.pad(dfl_t.reshape(N_HEADS, s).T, ((0, 0), (0, LANES - N_HEADS))).astype(BF16)
    d_wq, d_wk, d_wv, d_wu_in, d_wf = _mm_tn(h, [dq, dk, dv, du, dfl], ta=D_MODEL, ts=tm, name="grad_w_in")
    token = emit("w_in", jnp.concatenate([d_wq, d_wk, d_wv, d_wf[:, :N_HEADS], d_wu_in], axis=1))
    dx, d_g1 = _in_bwd(dq, dk, dv, du, dfl, w_in_p, x, r1, _tie(g1, token), dx1, tm=tm)

    small = dict(norm1_g=d_g1, b_forget=db[:, 0].reshape(1, N_HEADS), w_pool=d_wpool, pool_scale=d_pscale,
                 norm2_g=d_g2, final_g=d_gf)
    return loss_row, dx, small


def _my_index():
    return 4 * lax.axis_index("x") + 2 * lax.axis_index("y") + lax.axis_index("c")


def _peer(k):
    pos = [lax.axis_index(a) for a in ("x", "y", "c")]
    flipped = tuple(1 - p if (k >> b) & 1 else p for p, b in zip(pos, (2, 1, 0)))
    return flipped, 4 * flipped[0] + 2 * flipped[1] + flipped[2]


_HBM = pl.BlockSpec(memory_space=pltpu.HBM)
_SEM = pl.BlockSpec(memory_space=pltpu.SEMAPHORE)
_DATAFLOW = pltpu.SideEffectType.DATAFLOW_SIDE_EFFECTING


def _peer_copies(ins, lands, send_sems, recv_sems, scatter):
    me = _my_index()
    sends, recvs = [], []
    for w in range(len(ins)):
        for k in range(1, N_DEV):
            dev, idx = _peer(k)
            src = ins[w].at[idx] if scatter[w] else ins[w]
            sems = dict(send_sem=send_sems[w].at[k - 1], recv_sem=recv_sems[w].at[k - 1], device_id=dev,
                        device_id_type=MESH)
            sends.append(pltpu.make_async_remote_copy(src_ref=src, dst_ref=lands[w].at[me], **sems))
            recvs.append(pltpu.make_async_remote_copy(src_ref=src, dst_ref=lands[w].at[idx], **sems))
    return sends, recvs


def _exchange_start(arrays, scatter, name):
    n = len(arrays)
    land_shapes = [(N_DEV,) + tuple(a.shape[1:] if sc else a.shape) for a, sc in zip(arrays, scatter)]

    def body(*refs):
        ins, lands = refs[:n], refs[n:2 * n]
        send_sems, recv_sems = refs[2 * n:3 * n], refs[3 * n:4 * n]
        token = refs[6 * n]
        sends, _ = _peer_copies(ins, lands, send_sems, recv_sems, scatter)
        for cp in sends:
            cp.start()
        token[...] = jnp.zeros_like(token)

    sem = pltpu.SemaphoreType.DMA((N_DEV - 1,))
    outs = pl.pallas_call(
        body,
        in_specs=[_HBM] * (2 * n),
        out_specs=[_SEM] * (2 * n) + [_HBM] * (2 * n) + [pl.BlockSpec(memory_space=pltpu.VMEM)],
        out_shape=[sem] * (2 * n) + [pltpu.HBM(a.shape, a.dtype) for a in arrays]
        + [pltpu.HBM(sh, a.dtype) for sh, a in zip(land_shapes, arrays)] + [SDS((8, LANES), F32)],
        input_output_aliases={i: 2 * n + i for i in range(2 * n)},
        compiler_params=pltpu.CompilerParams(has_side_effects=_DATAFLOW),
        name=name,
    )(*[pltpu.with_memory_space_constraint(a, pltpu.HBM) for a in arrays],
      *[pltpu.with_memory_space_constraint(lax.empty(sh, a.dtype), pltpu.HBM) for sh, a in zip(land_shapes, arrays)])
    handles = [dict(send=outs[w], recv=outs[n + w], src=outs[2 * n + w], land=outs[3 * n + w], scatter=scatter[w])
               for w in range(n)]
    return handles, outs[4 * n]


def _exchange_wait(handles, after, name):
    n = len(handles)
    scatter = [h["scatter"] for h in handles]

    def body(*refs):
        ins, lands = refs[:n], refs[n:2 * n]
        send_sems, recv_sems = refs[2 * n:3 * n], refs[3 * n:4 * n]
        sends, recvs = _peer_copies(ins, lands, send_sems, recv_sems, scatter)
        for cp in sends:
            cp.wait_send()
        for cp in recvs:
            cp.wait_recv()

    srcs, lands = [h["src"] for h in handles], [h["land"] for h in handles]
    outs = pl.pallas_call(
        body,
        in_specs=[_HBM] * (2 * n) + [_SEM] * (2 * n) + [pl.BlockSpec(memory_space=pl.ANY)],
        out_specs=[_HBM] * (2 * n),
        out_shape=[pltpu.HBM(a.shape, a.dtype) for a in srcs + lands],
        input_output_aliases={i: i for i in range(2 * n)},
        compiler_params=pltpu.CompilerParams(has_side_effects=_DATAFLOW),
        name=name,
    )(*srcs, *lands, *[h["send"] for h in handles], *[h["recv"] for h in handles], after)
    return outs[n:]


def _with_own_slot(land, own):
    return lax.dynamic_update_slice(land, own[None], (_my_index(),) + (0,) * own.ndim)


def _adamw(parts, w, m, v, name):
    rows, cols = w.shape
    tr = rows // 4 if rows % 32 == 0 else rows

    def body(p_ref, w_ref, m_ref, v_ref, g_ref, d_ref, mo_ref, vo_ref):
        g = p_ref[0]
        for d in range(1, N_DEV):
            g = g + p_ref[d]
        m_new = ADAM_B1 * m_ref[...] + (1.0 - ADAM_B1) * g
        v_new = ADAM_B2 * v_ref[...] + (1.0 - ADAM_B2) * (g * g)
        m_hat = m_new / (1.0 - ADAM_B1 ** ADAM_STEP)
        v_hat = v_new / (1.0 - ADAM_B2 ** ADAM_STEP)
        g_ref[...] = g
        d_ref[...] = -ADAM_LR * (m_hat / (jnp.sqrt(v_hat) + ADAM_EPS) + ADAM_WD * w_ref[...])
        mo_ref[...] = m_new
        vo_ref[...] = v_new

    blk = pl.BlockSpec((tr, cols), lambda i: (i, 0))
    return pl.pallas_call(
        body,
        grid=(rows // tr,),
        in_specs=[pl.BlockSpec((N_DEV, tr, cols), lambda i: (0, i, 0)), blk, blk, blk],
        out_specs=[blk] * 4,
        out_shape=[SDS((rows, cols), F32)] * 4,
        compiler_params=_cparams("arbitrary"),
        name=name,
    )(parts, w, m, v)


_SMALL = (("w_pool", 512), ("norm1_g", 8), ("norm2_g", 8), ("final_g", 8), ("pool_scale", 8), ("b_forget", 8))
_SMALL_ROWS = sum(r for _, r in _SMALL)


def _pack_small(vals):
    parts = []
    for name, rows in _SMALL:
        flat = vals[name].reshape(-1).astype(F32)
        parts.append(jnp.pad(flat, (0, rows * LANES - flat.shape[0])).reshape(rows, LANES))
    return jnp.concatenate(parts, axis=0)


def _unpack_small(packed, like):
    out, r0 = {}, 0
    for name, rows in _SMALL:
        n = like[name].size
        out[name] = packed[r0:r0 + rows].reshape(-1)[:n].reshape(like[name].shape)
        r0 += rows
    return out


def kernel(x, norm1_g, w_in, b_forget, w_pool, pool_scale, w_out, norm2_g, w_gate, w_up, w_down, final_g, loss_target, m_norm1_g, m_w_in, m_b_forget, m_w_pool, m_pool_scale, m_w_out, m_norm2_g, m_w_gate, m_w_up, m_w_down, m_final_g, v_norm1_g, v_w_in, v_b_forget, v_w_pool, v_pool_scale, v_w_out, v_norm2_g, v_w_gate, v_w_up, v_w_down, v_final_g):
    big = ("w_in", "w_out", "w_gate", "w_up", "w_down")
    order = ("norm1_g", "w_in", "b_forget", "w_pool", "pool_scale", "w_out", "norm2_g", "w_gate", "w_up", "w_down",
             "final_g")
    w = dict(norm1_g=norm1_g, w_in=w_in, b_forget=b_forget, w_pool=w_pool, pool_scale=pool_scale, w_out=w_out,
             norm2_g=norm2_g, w_gate=w_gate, w_up=w_up, w_down=w_down, final_g=final_g)
    m = dict(norm1_g=m_norm1_g, w_in=m_w_in, b_forget=m_b_forget, w_pool=m_w_pool, pool_scale=m_pool_scale,
             w_out=m_w_out, norm2_g=m_norm2_g, w_gate=m_w_gate, w_up=m_w_up, w_down=m_w_down, final_g=m_final_g)
    v = dict(norm1_g=v_norm1_g, w_in=v_w_in, b_forget=v_b_forget, w_pool=v_w_pool, pool_scale=v_pool_scale,
             w_out=v_w_out, norm2_g=v_norm2_g, w_gate=v_w_gate, w_up=v_w_up, w_down=v_w_down, final_g=v_final_g)

    shards = [w[n][0].astype(BF16) for n in big]
    gather, _ = _exchange_start(shards, [False] * len(big), "gather_start")
    gather = dict(zip(big, zip(gather, shards)))

    def gathered(names, after):
        lands = _exchange_wait([gather[n][0] for n in names], after, "gather_wait_" + names[0])
        return [_with_own_slot(land, gather[n][1]) for n, land in zip(names, lands)]

    def weight(name, after):
        if name == "w_in":
            full = gathered(["w_in"], after)[0].transpose(1, 0, 2).reshape(D_MODEL, IN_W)
            f0 = QKV_W + N_HEADS
            return jnp.concatenate([full[:, :QKV_W], full[:, f0:], full[:, QKV_W:f0],
                                    jnp.zeros((D_MODEL, IN_PAD - IN_W), BF16)], axis=1)
        if name == "w_out":
            return gathered(["w_out"], after)[0].reshape(D_MODEL, D_MODEL)
        if name == "w_gate_up":
            return [g.transpose(1, 0, 2).reshape(D_MODEL, D_FF) for g in gathered(["w_gate", "w_up"], after)]
        return gathered(["w_down"], after)[0].reshape(D_FF, D_MODEL)

    cols = lambda g: g.reshape(g.shape[0], N_DEV, g.shape[1] // N_DEV).transpose(1, 0, 2)
    rows = lambda g: g.reshape(N_DEV, g.shape[0] // N_DEV, g.shape[1])
    sent = {}

    def emit(name, grad):
        if name == "w_gate_up":
            names, slots = ["w_gate", "w_up"], [cols(g) for g in grad]
        else:
            names, slots = [name], [cols(grad) if name == "w_in" else rows(grad)]
        handles, token = _exchange_start(slots, [True] * len(slots), "grads_start_" + name)
        for n, hd, sl in zip(names, handles, slots):
            sent[n] = (hd, sl)
        return token

    loss_row, dx, small_grads = _local_step(x[0], loss_target[0], w, weight, emit)
    loss = lax.psum(0.5 / D_MODEL * jnp.sum(loss_row), ("x", "y", "c"))

    packed = _pack_small(small_grads)
    (small_handle,), _ = _exchange_start([packed], [False], "grads_start_replicated")

    me = _my_index()
    outs = {}
    for name in ("w_down", "w_gate", "w_up", "w_out", "w_in"):
        handle, slots = sent[name]
        (land,) = _exchange_wait([handle], dx, "grads_wait_" + name)
        parts = _with_own_slot(land, lax.dynamic_index_in_dim(slots, me, 0, keepdims=False))
        outs[name] = [a[None] for a in _adamw(parts, w[name][0], m[name][0], v[name][0], "adamw_" + name)]
    (land,) = _exchange_wait([small_handle], dx, "grads_wait_replicated")
    small = _adamw(_with_own_slot(land, packed), _pack_small(w), _pack_small(m), _pack_small(v), "adamw_replicated")
    small = [_unpack_small(p, w) for p in small]
    for name in order:
        if name not in outs:
            outs[name] = [p[name] for p in small]

    return (loss, dx[None]) + tuple(outs[n][k] for k in range(4) for n in order)
```

```python
import functools

import jax
import jax.numpy as jnp
from jax import lax
from jax.experimental import pallas as pl
from jax.experimental.pallas import tpu as pltpu

F32 = jnp.float32
BF16 = jnp.bfloat16
SDS = jax.ShapeDtypeStruct

D_MODEL = 1024
ATTN_W = 512
N_HEADS = 8
HEAD_DIM = 64
N_PAIRS = N_HEADS // 2
POOL_W = 512
POOL_WINDOWS = (2, 4, 8, 16)
POOL_G = 128
HALO = 16
IN_W = 3 * ATTN_W + N_HEADS + POOL_W
QKV_W = 3 * ATTN_W
U_OFF = QKV_W
F_OFF = QKV_W + POOL_W
IN_PAD = F_OFF + 128
D_FF = 2816
EPS = 1e-6
NEG = -1e30
N_DEV = 8
LANES = 128

ADAM_LR = 0.001
ADAM_B1 = 0.9
ADAM_B2 = 0.999
ADAM_EPS = 1e-08
ADAM_WD = 0.01
ADAM_STEP = 10

VMEM_LIMIT_BYTES = 56 * 1024 * 1024
MESH = pl.DeviceIdType.MESH
NT = (((1,), (1,)), ((), ()))
TN = (((0,), (0,)), ((), ()))


def _cparams(*sem):
    return pltpu.CompilerParams(dimension_semantics=sem or None, vmem_limit_bytes=VMEM_LIMIT_BYTES)


def _split3(a):
    hi = a.astype(BF16)
    r1 = a - hi.astype(F32)
    mid = r1.astype(BF16)
    lo = (r1 - mid.astype(F32)).astype(BF16)
    return hi, mid, lo


def _dot_sel(a, sel, dims=None):
    sb = sel.astype(BF16)
    if dims is None:
        return sum(jnp.dot(p, sb, preferred_element_type=F32) for p in _split3(a))
    return sum(lax.dot_general(p, sb, dims, preferred_element_type=F32) for p in _split3(a))


def _sel_dot(sel, a, dims=None):
    sb = sel.astype(BF16)
    if dims is None:
        return sum(jnp.dot(sb, p, preferred_element_type=F32) for p in _split3(a))
    return sum(lax.dot_general(sb, p, dims, preferred_element_type=F32) for p in _split3(a))


def _iota2(shape, dim):
    return lax.broadcasted_iota(jnp.int32, shape, dim)


def _norm_proj(x, g1, w_in_p, *, tm):
    s = x.shape[0]

    def body(x_ref, g_ref, w_ref, h_ref, r_ref, qkv_ref, u_ref, fl_ref):
        xv = x_ref[...]
        r = lax.rsqrt(jnp.mean(xv * xv, axis=-1, keepdims=True) + EPS)
        h = (xv * r * g_ref[...]).astype(BF16)
        h_ref[...] = h
        r_ref[...] = r
        qkv_ref[...] = jnp.dot(h, w_ref[:, 0:QKV_W], preferred_element_type=F32).astype(BF16)
        u_ref[...] = jnp.dot(h, w_ref[:, U_OFF:F_OFF], preferred_element_type=F32)
        fl_ref[...] = jnp.dot(h, w_ref[:, F_OFF:IN_PAD], preferred_element_type=F32)

    row = lambda w: pl.BlockSpec((tm, w), lambda i: (i, 0))
    full = lambda a, b: pl.BlockSpec((a, b), lambda i: (0, 0))
    return pl.pallas_call(
        body,
        grid=(s // tm,),
        in_specs=[row(D_MODEL), full(1, D_MODEL), full(D_MODEL, IN_PAD)],
        out_specs=[row(D_MODEL), row(1), row(QKV_W), row(POOL_W), row(LANES)],
        out_shape=[SDS((s, D_MODEL), BF16), SDS((s, 1), F32), SDS((s, QKV_W), BF16), SDS((s, POOL_W), F32),
                   SDS((s, LANES), F32)],
        compiler_params=_cparams("arbitrary"),
        name="norm_proj",
    )(x, g1, w_in_p)


def _head_block_masks(rows, nb):
    shift = nb.bit_length() - 1
    rr, cc = _iota2((rows, rows), 0), _iota2((rows, rows), 1)
    same = lax.shift_right_logical(rr, shift) == lax.shift_right_logical(cc, shift)
    return rr, cc, same


def _forget_cumsum(fl_t, b_rows):
    rows = fl_t.shape[0]
    nb = rows // N_HEADS

    def body(fl_ref, b_ref, c_ref):
        z = fl_ref[...] + b_ref[...]
        lf = jnp.minimum(z, 0.0) - jnp.log1p(jnp.exp(-jnp.abs(z)))
        upper = _iota2((LANES, LANES), 0) <= _iota2((LANES, LANES), 1)
        within = _dot_sel(lf, upper)
        tot = _dot_sel(lf, jnp.ones((LANES, LANES), F32))
        rr, cc, same = _head_block_masks(rows, nb)
        c_ref[...] = within + _sel_dot(same & (cc < rr), tot)

    return pl.pallas_call(body, out_shape=SDS(fl_t.shape, F32), compiler_params=_cparams(), name="forget_cumsum")(
        fl_t, b_rows)


BIAS_LANES = 3


def _augment(t, h, col, col_first):
    n = t.shape[0]
    lane = _iota2((n, LANES), 1)
    own = (lane < HEAD_DIM) if h == 0 else (lane >= HEAD_DIM)
    b0 = HEAD_DIM if h == 0 else 0
    c0, o0 = (b0, b0 + BIAS_LANES) if col_first else (b0 + BIAS_LANES, b0)
    x = jnp.where(own, t, 0.0)
    for off, piece in enumerate(_split3(col)):
        x = jnp.where(lane == c0 + off, piece.astype(F32), x)
    x = jnp.where((lane >= o0) & (lane < o0 + BIAS_LANES), 1.0, x)
    return x.astype(BF16)


def _attn_fwd(qkv, c_col, *, tq):
    s = qkv.shape[0]
    tk = tq
    nb = s // tq

    def body(q_ref, k_ref, v_ref, cq_ref, ck_ref, o_ref, lse_ref, kp_ref, vt_ref, st_ref):
        i = pl.program_id(1)

        @pl.when(i == 0)
        def _():
            def prep(jb, _):
                st = pl.multiple_of(jb * tk, tk)
                k2 = k_ref[pl.ds(st, tk), :].astype(F32)
                ck = ck_ref[pl.ds(st, tk), :]
                for h in range(2):
                    kp_ref[h * nb + jb] = _augment(k2, h, -ck[:, h:h + 1], True)
                vt_ref[jb] = v_ref[pl.ds(st, tk), :].astype(F32).T.astype(BF16)
                return 0

            lax.fori_loop(0, nb, prep, 0)

        qs = q_ref[...].astype(F32) * 0.125
        cq = cq_ref[...]
        qp = [_augment(qs, h, cq[:, h:h + 1], False) for h in range(2)]

        def logits(j):
            return tuple(lax.dot_general(kp_ref[h * nb + j], qp[h], NT, preferred_element_type=F32) for h in range(2))

        def softmax_pv(j, sts, stats, masked):
            out = []
            for h in range(2):
                m, l, acc = stats[h]
                st = sts[h]
                if masked:
                    st = jnp.where(_iota2((tk, tq), 0) <= _iota2((tk, tq), 1), st, NEG)
                m_new = jnp.maximum(m, jnp.max(st, axis=0, keepdims=True))
                alpha = jnp.exp(m - m_new)
                p = jnp.exp(st - m_new)
                l = alpha * l + jnp.sum(p, axis=0, keepdims=True)
                vt = vt_ref[j, h * HEAD_DIM:(h + 1) * HEAD_DIM, :]
                acc = alpha * acc + jnp.dot(vt, p.astype(BF16), preferred_element_type=F32)
                out.append((m_new, l, acc))
            return tuple(out)

        def put(slot, sts):
            for h in range(2):
                st_ref[2 * slot + h] = sts[h]

        def get(slot):
            return tuple(st_ref[2 * slot + h] for h in range(2))

        def step(j, stats):
            nxt = logits(j + 1)
            stats = softmax_pv(j, get(j % 2), stats, False)
            put((j + 1) % 2, nxt)
            return stats

        init = tuple((jnp.full((1, tq), NEG, F32), jnp.zeros((1, tq), F32), jnp.zeros((HEAD_DIM, tq), F32))
                     for _ in range(2))
        put(0, logits(0))
        stats = lax.fori_loop(0, i, step, init)
        (ma, la, acca), (mb, lb, accb) = softmax_pv(i, get(i % 2), stats, True)
        o_ref[...] = jnp.concatenate([acca / la, accb / lb], axis=0).T.astype(BF16)
        lse_ref[...] = jnp.where(_iota2((2, tq), 0) == 0, ma + jnp.log(la), mb + jnp.log(lb))

    return pl.pallas_call(
        body,
        grid=(N_PAIRS, nb),
        in_specs=[
            pl.BlockSpec((tq, LANES), lambda p, i: (i, p)),
            pl.BlockSpec((s, LANES), lambda p, i: (0, N_PAIRS + p)),
            pl.BlockSpec((s, LANES), lambda p, i: (0, 2 * N_PAIRS + p)),
            pl.BlockSpec((None, tq, 2), lambda p, i: (p, i, 0)),
            pl.BlockSpec((None, s, 2), lambda p, i: (p, 0, 0)),
        ],
        out_specs=[
            pl.BlockSpec((tq, LANES), lambda p, i: (i, p)),
            pl.BlockSpec((None, None, 2, tq), lambda p, i: (p, i, 0, 0)),
        ],
        out_shape=[SDS((s, ATTN_W), BF16), SDS((N_PAIRS, nb, 2, tq), F32)],
        scratch_shapes=[pltpu.VMEM((2 * nb, tk, LANES), BF16), pltpu.VMEM((nb, LANES, tk), BF16),
                        pltpu.VMEM((4, tk, tq), F32)],
        compiler_params=_cparams("arbitrary", "arbitrary"),
        name="attn_fwd",
    )(qkv, qkv, qkv, c_col, c_col)


def _pool_counts(row0, tm, w):
    t = row0 + _iota2((tm, 1), 0)
    return jnp.minimum(t + 1, w).astype(F32)


def _pool_fwd(u, w_pool, pool_scale, *, tm):
    s = u.shape[0]

    def body(u_ref, w_ref, sc_ref, pooled_ref, po_ref, tail_ref):
        i = pl.program_id(0)

        @pl.when(i == 0)
        def _():
            tail_ref[...] = jnp.zeros_like(tail_ref)

        uv = u_ref[...]
        ext = jnp.concatenate([tail_ref[...], uv], axis=0)
        tail_ref[...] = uv[tm - HALO:, :]
        for g, w in enumerate(POOL_WINDOWS):
            cols = slice(g * POOL_G, (g + 1) * POOL_G)
            acc = ext[:, cols]
            k = 1
            while k < w:
                acc = acc + pltpu.roll(acc, k, axis=0)
                k *= 2
            pooled = (acc[HALO:, :] / _pool_counts(i * tm, tm, w) - uv[:, cols]).astype(BF16)
            pooled_ref[:, cols] = pooled
            mixed = jnp.dot(pooled, w_ref[g].astype(BF16), preferred_element_type=F32)
            po_ref[:, cols] = (mixed * sc_ref[:, cols]).astype(BF16)

    row = pl.BlockSpec((tm, POOL_W), lambda i: (i, 0))
    return pl.pallas_call(
        body,
        grid=(s // tm,),
        in_specs=[row, pl.BlockSpec((len(POOL_WINDOWS), POOL_G, POOL_G), lambda i: (0, 0, 0)),
                  pl.BlockSpec((1, POOL_W), lambda i: (0, 0))],
        out_specs=[row, row],
        out_shape=[SDS((s, POOL_W), BF16), SDS((s, POOL_W), BF16)],
        scratch_shapes=[pltpu.VMEM((HALO, POOL_W), F32)],
        compiler_params=_cparams("arbitrary"),
        name="pool_fwd",
    )(u, w_pool, pool_scale)


def _out_norm2(attn_o, pool_o, w_out, x, g2, *, tm):
    s = x.shape[0]

    def body(a_ref, p_ref, w_ref, x_ref, g_ref, x1_ref, h2_ref, r_ref):
        x1 = (x_ref[...] + jnp.dot(a_ref[...], w_ref[0:ATTN_W, :], preferred_element_type=F32)
              + jnp.dot(p_ref[...], w_ref[ATTN_W:, :], preferred_element_type=F32))
        r = lax.rsqrt(jnp.mean(x1 * x1, axis=-1, keepdims=True) + EPS)
        x1_ref[...] = x1
        r_ref[...] = r
        h2_ref[...] = (x1 * r * g_ref[...]).astype(BF16)

    row = lambda w: pl.BlockSpec((tm, w), lambda i: (i, 0))
    full = lambda a, b: pl.BlockSpec((a, b), lambda i: (0, 0))
    return pl.pallas_call(
        body,
        grid=(s // tm,),
        in_specs=[row(ATTN_W), row(POOL_W), full(D_MODEL, D_MODEL), row(D_MODEL), full(1, D_MODEL)],
        out_specs=[row(D_MODEL), row(D_MODEL), row(1)],
        out_shape=[SDS((s, D_MODEL), F32), SDS((s, D_MODEL), BF16), SDS((s, 1), F32)],
        compiler_params=_cparams("arbitrary"),
        name="out_norm2",
    )(attn_o, pool_o, w_out, x, g2)


def _gate_up(h2, wg, wu, *, tm, tn):
    s = h2.shape[0]

    def body(h_ref, wg_ref, wu_ref, gate_ref, up_ref, act_ref):
        h = h_ref[...]
        gate = jnp.dot(h, wg_ref[...], preferred_element_type=F32)
        up = jnp.dot(h, wu_ref[...], preferred_element_type=F32)
        gate_ref[...] = gate
        up_ref[...] = up
        act_ref[...] = (gate * jax.nn.sigmoid(gate) * up).astype(BF16)

    wspec = pl.BlockSpec((D_MODEL, tn), lambda c, r: (0, c))
    ospec = pl.BlockSpec((tm, tn), lambda c, r: (r, c))
    return pl.pallas_call(
        body,
        grid=(D_FF // tn, s // tm),
        in_specs=[pl.BlockSpec((tm, D_MODEL), lambda c, r: (r, 0)), wspec, wspec],
        out_specs=[ospec, ospec, ospec],
        out_shape=[SDS((s, D_FF), F32), SDS((s, D_FF), F32), SDS((s, D_FF), BF16)],
        compiler_params=_cparams("arbitrary", "arbitrary"),
        name="gate_up",
    )(h2, wg, wu)


def _down_final(act, wd, x1, gf, tgt, *, tm):
    s = x1.shape[0]

    def body(a_ref, w_ref, x1_ref, g_ref, t_ref, dx2_ref, loss_ref, dgf_ref):
        @pl.when(pl.program_id(0) == 0)
        def _():
            loss_ref[...] = jnp.zeros_like(loss_ref)
            dgf_ref[...] = jnp.zeros_like(dgf_ref)

        x2 = x1_ref[...] + jnp.dot(a_ref[...], w_ref[...], preferred_element_type=F32)
        r = lax.rsqrt(jnp.mean(x2 * x2, axis=-1, keepdims=True) + EPS)
        xn = x2 * r
        g = g_ref[...]
        diff = xn * g - t_ref[...]
        loss_ref[...] += jnp.sum(diff * diff, axis=0, keepdims=True)
        dy = diff * (1.0 / D_MODEL)
        dgf_ref[...] += jnp.sum(dy * xn, axis=0, keepdims=True)
        dxn = dy * g
        dx2_ref[...] = r * (dxn - xn * jnp.mean(dxn * xn, axis=-1, keepdims=True))

    row = lambda w: pl.BlockSpec((tm, w), lambda i: (i, 0))
    full = lambda a, b: pl.BlockSpec((a, b), lambda i: (0, 0))
    return pl.pallas_call(
        body,
        grid=(s // tm,),
        in_specs=[row(D_FF), full(D_FF, D_MODEL), row(D_MODEL), full(1, D_MODEL), row(D_MODEL)],
        out_specs=[row(D_MODEL), full(1, D_MODEL), full(1, D_MODEL)],
        out_shape=[SDS((s, D_MODEL), F32), SDS((1, D_MODEL), F32), SDS((1, D_MODEL), F32)],
        compiler_params=_cparams("arbitrary"),
        name="down_final",
    )(act, wd, x1, gf, tgt)


def _swiglu_bwd(dx2, wd, gate, up, *, tm, tn):
    s = dx2.shape[0]

    def body(d_ref, w_ref, gate_ref, up_ref, dgate_ref, dup_ref):
        dact = lax.dot_general(d_ref[...].astype(BF16), w_ref[...], NT, preferred_element_type=F32)
        gate = gate_ref[...]
        sg = jax.nn.sigmoid(gate)
        dup_ref[...] = (dact * (gate * sg)).astype(BF16)
        dgate_ref[...] = (dact * up_ref[...] * (sg * (1.0 + gate * (1.0 - sg)))).astype(BF16)

    ospec = pl.BlockSpec((tm, tn), lambda c, r: (r, c))
    return pl.pallas_call(
        body,
        grid=(D_FF // tn, s // tm),
        in_specs=[pl.BlockSpec((tm, D_MODEL), lambda c, r: (r, 0)), pl.BlockSpec((tn, D_MODEL), lambda c, r: (c, 0)),
                  ospec, ospec],
        out_specs=[ospec, ospec],
        out_shape=[SDS((s, D_FF), BF16), SDS((s, D_FF), BF16)],
        compiler_params=_cparams("arbitrary", "arbitrary"),
        name="swiglu_bwd",
    )(dx2, wd, gate, up)


def _mm_tn(a, bs, *, ta, ts, name):
    s, ka = a.shape
    n = len(bs)

    def body(a_ref, *refs):
        b_refs, o_refs = refs[:n], refs[n:]

        @pl.when(pl.program_id(1) == 0)
        def _():
            for o_ref in o_refs:
                o_ref[...] = jnp.zeros_like(o_ref)

        av = a_ref[...].astype(BF16)
        for b_ref, o_ref in zip(b_refs, o_refs):
            o_ref[...] += lax.dot_general(av, b_ref[...].astype(BF16), TN, preferred_element_type=F32)

    return pl.pallas_call(
        body,
        grid=(ka // ta, s // ts),
        in_specs=[pl.BlockSpec((ts, ta), lambda i, k: (k, i))]
        + [pl.BlockSpec((ts, b.shape[1]), lambda i, k: (k, 0)) for b in bs],
        out_specs=[pl.BlockSpec((ta, b.shape[1]), lambda i, k: (i, 0)) for b in bs],
        out_shape=[SDS((ka, b.shape[1]), F32) for b in bs],
        compiler_params=_cparams("arbitrary", "arbitrary"),
        name=name,
    )(a, *bs)


def _norm_bwd(dh, x, r, g, dres):
    xn = x * r
    dxn = dh * g
    dx = dres + r * (dxn - xn * jnp.mean(dxn * xn, axis=-1, keepdims=True))
    return dx, jnp.sum(dh * xn, axis=0, keepdims=True)


def _mlp_in_bwd(dgate, dup, wg, wu, w_out, x1, r2, g2, dx2, *, tm):
    s = x1.shape[0]

    def body(dg_ref, du_ref, wg_ref, wu_ref, wo_ref, x_ref, r_ref, g_ref, d_ref, dx1_ref, dmix_ref, dg2_ref):
        @pl.when(pl.program_id(0) == 0)
        def _():
            dg2_ref[...] = jnp.zeros_like(dg2_ref)

        dh2 = (lax.dot_general(dg_ref[...], wg_ref[...], NT, preferred_element_type=F32)
               + lax.dot_general(du_ref[...], wu_ref[...], NT, preferred_element_type=F32))
        dx1, dg2 = _norm_bwd(dh2, x_ref[...], r_ref[...], g_ref[...], d_ref[...])
        dg2_ref[...] += dg2
        dx1_ref[...] = dx1
        dmix_ref[...] = lax.dot_general(dx1.astype(BF16), wo_ref[...], NT, preferred_element_type=F32)

    row = lambda w: pl.BlockSpec((tm, w), lambda i: (i, 0))
    full = lambda a, b: pl.BlockSpec((a, b), lambda i: (0, 0))
    return pl.pallas_call(
        body,
        grid=(s // tm,),
        in_specs=[row(D_FF), row(D_FF), full(D_MODEL, D_FF), full(D_MODEL, D_FF), full(D_MODEL, D_MODEL),
                  row(D_MODEL), row(1), full(1, D_MODEL), row(D_MODEL)],
        out_specs=[row(D_MODEL), row(D_MODEL), full(1, D_MODEL)],
        out_shape=[SDS((s, D_MODEL), F32), SDS((s, D_MODEL), F32), SDS((1, D_MODEL), F32)],
        compiler_params=_cparams("arbitrary"),
        name="mlp_in_bwd",
    )(dgate, dup, wg, wu, w_out, x1, r2, g2, dx2)


def _pool_bwd(dmixed, pooled, w_pool, pool_scale, *, tm):
    s = pooled.shape[0]
    nt = s // tm
    ng = len(POOL_WINDOWS)

    def body(d_ref, p_ref, w_ref, sc_ref, du_ref, dw_ref, dsc_ref, head_ref):
        i = pl.program_id(0)

        @pl.when(i == 0)
        def _():
            head_ref[...] = jnp.zeros_like(head_ref)
            dw_ref[...] = jnp.zeros_like(dw_ref)
            dsc_ref[...] = jnp.zeros_like(dsc_ref)

        row0 = (nt - 1 - i) * tm
        for g, w in enumerate(POOL_WINDOWS):
            cols = slice(g * POOL_G, (g + 1) * POOL_G)
            wb = w_ref[g].astype(BF16)
            pooled_g = p_ref[:, cols]
            dpo = d_ref[:, cols]
            mixed = jnp.dot(pooled_g, wb, preferred_element_type=F32)
            dsc_ref[:, cols] += jnp.sum(dpo * mixed, axis=0, keepdims=True)
            dmp = (dpo * sc_ref[:, cols]).astype(BF16)
            dw_ref[g] += lax.dot_general(pooled_g, dmp, TN, preferred_element_type=F32)
            dpooled = lax.dot_general(dmp, wb, NT, preferred_element_type=F32)
            a = dpooled / _pool_counts(row0, tm, w)
            acc = jnp.concatenate([a, head_ref[:, cols]], axis=0)
            head_ref[:, cols] = a[0:HALO, :]
            k = 1
            while k < w:
                acc = acc + pltpu.roll(acc, tm + HALO - k, axis=0)
                k *= 2
            du_ref[:, cols] = (acc[0:tm, :] - dpooled).astype(BF16)

    rev = lambda i: (nt - 1 - i, 0)
    return pl.pallas_call(
        body,
        grid=(nt,),
        in_specs=[pl.BlockSpec((tm, POOL_W), lambda i: (nt - 1 - i, 1)), pl.BlockSpec((tm, POOL_W), rev),
                  pl.BlockSpec((ng, POOL_G, POOL_G), lambda i: (0, 0, 0)), pl.BlockSpec((1, POOL_W), lambda i: (0, 0))],
        out_specs=[pl.BlockSpec((tm, POOL_W), rev), pl.BlockSpec((ng, POOL_G, POOL_G), lambda i: (0, 0, 0)),
                   pl.BlockSpec((1, POOL_W), lambda i: (0, 0))],
        out_shape=[SDS((s, POOL_W), BF16), SDS((ng, POOL_G, POOL_G), F32), SDS((1, POOL_W), F32)],
        scratch_shapes=[pltpu.VMEM((HALO, POOL_W), F32)],
        compiler_params=_cparams("arbitrary"),
        name="pool_bwd",
    )(dmixed, pooled, w_pool, pool_scale)


def _attn_bwd(qkv, attn_o, dmixed, rowb, ck_col, *, tq):
    s = qkv.shape[0]
    tk = tq
    nb = s // tq

    def body(q_ref, k_ref, v_ref, o_ref, do_ref, rowb_ref, ck_ref, dq_ref, dk_ref, dv_ref, dck_ref, dcq_ref, dq_acc,
             delta_ref):
        lane = _iota2((tq, LANES), 1)
        lo = lane < HEAD_DIM
        first = _iota2((8, LANES), 1) < HEAD_DIM
        sel = jnp.where(_iota2((8, LANES), 0) < 4, jnp.where(first, 1.0, 0.0), jnp.where(first, 0.0, 1.0))
        row8 = _iota2((8, tq), 0)

        def prep(i, _):
            st = pl.multiple_of(i * tq, tq)
            prod = do_ref[pl.ds(st, tq), :] * o_ref[pl.ds(st, tq), :].astype(F32)
            delta_ref[i] = _sel_dot(sel, prod, NT)
            dq_acc[pl.ds(st, tq), :] = jnp.zeros((tq, LANES), F32)
            dcq_ref[i] = jnp.zeros((8, tq), F32)
            return 0

        lax.fori_loop(0, nb, prep, 0)

        def split(t):
            z = jnp.zeros_like(t)
            return jnp.where(lo, t, z), jnp.where(lo, z, t)

        def kv_block(j, _):
            st_j = pl.multiple_of(j * tk, tk)
            ks = split(k_ref[pl.ds(st_j, tk), :])
            vs = split(v_ref[pl.ds(st_j, tk), :])
            kcat = jnp.concatenate(ks, axis=0)
            ck = ck_ref[pl.ds(st_j, tk), :]

            def q_block(i, carry, masked):
                dk_acc, dv_acc, dca, dcb = carry
                st_i = pl.multiple_of(i * tq, tq)
                q2 = q_ref[pl.ds(st_i, tq), :]
                do2 = do_ref[pl.ds(st_i, tq), :].astype(BF16)
                rb = rowb_ref[i]
                dl = delta_ref[i]
                pts, dsts = [], []
                for h in range(2):
                    st = lax.dot_general(ks[h], q2, NT, preferred_element_type=F32) * 0.125
                    st = st + rb[h:h + 1, :] - ck[:, h:h + 1]
                    if masked:
                        st = jnp.where(_iota2((tk, tq), 0) <= _iota2((tk, tq), 1), st, NEG)
                    pt = jnp.exp(st)
                    dpt = lax.dot_general(vs[h], do2, NT, preferred_element_type=F32)
                    pts.append(pt.astype(BF16))
                    dsts.append(pt * (dpt - dl[4 * h:4 * h + 1, :]))
                dca = dca + jnp.sum(dsts[0], axis=1, keepdims=True)
                dcb = dcb + jnp.sum(dsts[1], axis=1, keepdims=True)
                dcq_ref[i] += jnp.where(row8 == 0, jnp.sum(dsts[0], axis=0, keepdims=True),
                                        jnp.where(row8 == 4, jnp.sum(dsts[1], axis=0, keepdims=True), 0.0))
                dsb = [d.astype(BF16) for d in dsts]
                dv_acc = dv_acc + jnp.dot(jnp.concatenate(pts, axis=1), jnp.concatenate(split(do2), axis=0),
                                          preferred_element_type=F32)
                dk_acc = dk_acc + jnp.dot(jnp.concatenate(dsb, axis=1), jnp.concatenate(split(q2), axis=0),
                                          preferred_element_type=F32)
                dq_acc[pl.ds(st_i, tq), :] += lax.dot_general(jnp.concatenate(dsb, axis=0), kcat, TN,
                                                              preferred_element_type=F32)
                return dk_acc, dv_acc, dca, dcb

            zt = jnp.zeros((tk, LANES), F32)
            zc = jnp.zeros((tk, 1), F32)
            carry = q_block(j, (zt, zt, zc, zc), True)
            dk_acc, dv_acc, dca, dcb = lax.fori_loop(j + 1, nb, lambda i, c: q_block(i, c, False), carry)
            dk_ref[pl.ds(st_j, tk), :] = (dk_acc * 0.125).astype(BF16)
            dv_ref[pl.ds(st_j, tk), :] = dv_acc.astype(BF16)
            dck_ref[pl.ds(st_j, tk), :] = jnp.where(_iota2((tk, 2), 1) == 0, dca, dcb)
            return 0

        lax.fori_loop(0, nb, kv_block, 0)
        dq_ref[...] = (dq_acc[...] * 0.125).astype(BF16)

    col = lambda off: pl.BlockSpec((s, LANES), lambda p: (0, off + p))
    return pl.pallas_call(
        body,
        grid=(N_PAIRS,),
        in_specs=[col(0), col(N_PAIRS), col(2 * N_PAIRS), col(0), col(0),
                  pl.BlockSpec((None, nb, 2, tq), lambda p: (p, 0, 0, 0)),
                  pl.BlockSpec((None, s, 2), lambda p: (p, 0, 0))],
        out_specs=[col(0), col(0), col(0), pl.BlockSpec((None, s, 2), lambda p: (p, 0, 0)),
                   pl.BlockSpec((None, nb, 8, tq), lambda p: (p, 0, 0, 0))],
        out_shape=[SDS((s, ATTN_W), BF16), SDS((s, ATTN_W), BF16), SDS((s, ATTN_W), BF16), SDS((N_PAIRS, s, 2), F32),
                   SDS((N_PAIRS, nb, 8, tq), F32)],
        scratch_shapes=[pltpu.VMEM((s, LANES), F32), pltpu.VMEM((nb, 8, tq), F32)],
        compiler_params=_cparams("arbitrary"),
        name="attn_bwd",
    )(qkv, qkv, qkv, attn_o, dmixed, rowb, ck_col)


def _forget_bwd(dc_t, fl_t, b_rows):
    rows = fl_t.shape[0]
    nb = rows // N_HEADS

    def body(dc_ref, fl_ref, b_ref, dfl_ref, db_ref):
        dc = dc_ref[...]
        lower = _iota2((LANES, LANES), 0) >= _iota2((LANES, LANES), 1)
        ones = jnp.ones((LANES, LANES), F32)
        rr, cc, same = _head_block_masks(rows, nb)
        dlf = _dot_sel(dc, lower) + _sel_dot(same & (cc > rr), _dot_sel(dc, ones))
        dfl = dlf / (1.0 + jnp.exp(fl_ref[...] + b_ref[...]))
        dfl_ref[...] = dfl
        shift = nb.bit_length() - 1
        hsel = lax.shift_right_logical(_iota2((N_HEADS, rows), 1), shift) == _iota2((N_HEADS, rows), 0)
        db_ref[...] = _sel_dot(hsel, _dot_sel(dfl, ones))

    return pl.pallas_call(body, out_shape=[SDS(fl_t.shape, F32), SDS((N_HEADS, LANES), F32)],
                          compiler_params=_cparams(), name="forget_bwd")(dc_t, fl_t, b_rows)


def _in_bwd(dq, dk, dv, du, dfl, w_in_p, x, r1, g1, dx1, *, tm):
    s = x.shape[0]
    pieces = ((0, ATTN_W), (ATTN_W, 2 * ATTN_W), (2 * ATTN_W, QKV_W), (U_OFF, F_OFF), (F_OFF, IN_PAD))

    def body(dq_ref, dk_ref, dv_ref, du_ref, df_ref, w_ref, x_ref, r_ref, g_ref, d_ref, dx_ref, dg1_ref):
        @pl.when(pl.program_id(0) == 0)
        def _():
            dg1_ref[...] = jnp.zeros_like(dg1_ref)

        dh = None
        for ref, (c0, c1) in zip((dq_ref, dk_ref, dv_ref, du_ref, df_ref), pieces):
            t = lax.dot_general(ref[...], w_ref[:, c0:c1], NT, preferred_element_type=F32)
            dh = t if dh is None else dh + t
        dx, dg1 = _norm_bwd(dh, x_ref[...], r_ref[...], g_ref[...], d_ref[...])
        dx_ref[...] = dx
        dg1_ref[...] += dg1

    row = lambda w: pl.BlockSpec((tm, w), lambda i: (i, 0))
    full = lambda a, b: pl.BlockSpec((a, b), lambda i: (0, 0))
    return pl.pallas_call(
        body,
        grid=(s // tm,),
        in_specs=[row(ATTN_W), row(ATTN_W), row(ATTN_W), row(POOL_W), row(LANES), full(D_MODEL, IN_PAD),
                  row(D_MODEL), row(1), full(1, D_MODEL), row(D_MODEL)],
        out_specs=[row(D_MODEL), full(1, D_MODEL)],
        out_shape=[SDS((s, D_MODEL), F32), SDS((1, D_MODEL), F32)],
        compiler_params=_cparams("arbitrary"),
        name="in_bwd",
    )(dq, dk, dv, du, dfl, w_in_p, x, r1, g1, dx1)


def _tiles(s):
    big = min(512, s)
    return dict(row=big, attn=min(256, s // 2), mlp_bwd=min(256, s))


def _tie(a, token):
    return a + token[0:1, 0:1].astype(a.dtype)


def _local_step(x, tgt, p, weight, emit):
    s = x.shape[0]
    t = _tiles(s)
    tm, tq = t["row"], t["attn"]
    nb = s // LANES
    nqb = s // tq
    g1, g2, gf = p["norm1_g"], p["norm2_g"], p["final_g"].reshape(1, D_MODEL)
    w_pool, pool_scale = p["w_pool"][0], p["pool_scale"]

    w_in_p = weight("w_in", x)
    h, r1, qkv, u, fl = _norm_proj(x, g1, w_in_p, tm=tm)
    fl_t = fl[:, :N_HEADS].T.reshape(N_HEADS * nb, LANES)
    b_rows = jnp.repeat(p["b_forget"].reshape(N_HEADS), nb).reshape(N_HEADS * nb, 1)
    c = _forget_cumsum(fl_t, b_rows).reshape(N_PAIRS, 2, s)
    c_col = c.transpose(0, 2, 1)
    c_rowblk = c.reshape(N_PAIRS, 2, nqb, tq).transpose(0, 2, 1, 3)
    attn_o, lse = _attn_fwd(qkv, c_col, tq=tq)
    pooled, pool_o = _pool_fwd(u, w_pool, pool_scale, tm=tm)
    w_out = weight("w_out", attn_o)
    x1, h2, r2 = _out_norm2(attn_o, pool_o, w_out, x, g2, tm=tm)
    wg, wu = weight("w_gate_up", h2)
    gate, up, act = _gate_up(h2, wg, wu, tm=tm, tn=D_FF // 2)
    wd = weight("w_down", act)
    dx2, loss_row, d_gf = _down_final(act, wd, x1, gf, tgt, tm=tm)

    dgate, dup = _swiglu_bwd(dx2, wd, gate, up, tm=tm, tn=D_FF // 2)
    (d_wd,) = _mm_tn(act, [dx2], ta=D_FF // 2, ts=tm, name="grad_w_down")
    token = emit("w_down", d_wd)
    dx1, dmixed, d_g2 = _mlp_in_bwd(dgate, dup, wg, wu, w_out, x1, r2, _tie(g2, token), dx2, tm=t["mlp_bwd"])
    token = emit("w_gate_up", _mm_tn(h2, [dgate, dup], ta=tm, ts=tm, name="grad_w_gate_up"))
    du, d_wpool, d_pscale = _pool_bwd(dmixed, pooled, w_pool, _tie(pool_scale, token), tm=tm)
    (d_wo_a,) = _mm_tn(attn_o, [dx1], ta=ATTN_W, ts=tm, name="grad_w_out_attn")
    (d_wo_p,) = _mm_tn(pool_o, [dx1], ta=POOL_W, ts=tm, name="grad_w_out_pool")
    token = emit("w_out", jnp.concatenate([d_wo_a, d_wo_p], axis=0))
    rowb = _tie(c_rowblk - lse, token)
    dq, dk, dv, dck, dcq = _attn_bwd(qkv, attn_o, dmixed, rowb, c_col, tq=tq)
    dcq = dcq[:, :, 0::4, :].transpose(0, 2, 1, 3).reshape(N_PAIRS, 2, s)
    dc_t = (dcq - dck.transpose(0, 2, 1)).reshape(N_HEADS * nb, LANES)
    dfl_t, db = _forget_bwd(dc_t, fl_t, b_rows)
    dfl = jnp.pad(dfl_t.reshape(N_HEADS, s).T, ((0, 0), (0, LANES - N_HEADS))).astype(BF16)
    d_wq, d_wk, d_wv, d_wu_in, d_wf = _mm_tn(h, [dq, dk, dv, du, dfl], ta=D_MODEL, ts=tm, name="grad_w_in")
    token = emit("w_in", jnp.concatenate([d_wq, d_wk, d_wv, d_wf[:, :N_HEADS], d_wu_in], axis=1))
    dx, d_g1 = _in_bwd(dq, dk, dv, du, dfl, w_in_p, x, r1, _tie(g1, token), dx1, tm=tm)

    small = dict(norm1_g=d_g1, b_forget=db[:, 0].reshape(1, N_HEADS), w_pool=d_wpool, pool_scale=d_pscale,
                 norm2_g=d_g2, final_g=d_gf)
    return loss_row, dx, small


def _my_index():
    return 4 * lax.axis_index("x") + 2 * lax.axis_index("y") + lax.axis_index("c")


def _peer(k):
    pos = [lax.axis_index(a) for a in ("x", "y", "c")]
    flipped = tuple(1 - p if (k >> b) & 1 else p for p, b in zip(pos, (2, 1, 0)))
    return flipped, 4 * flipped[0] + 2 * flipped[1] + flipped[2]


_HBM = pl.BlockSpec(memory_space=pltpu.HBM)
_SEM = pl.BlockSpec(memory_space=pltpu.SEMAPHORE)
_DATAFLOW = pltpu.SideEffectType.DATAFLOW_SIDE_EFFECTING


def _peer_copies(ins, lands, send_sems, recv_sems, scatter, arrivals):
    me = _my_index()
    copies = []
    for w in range(len(ins)):
        for k in range(1, N_DEV):
            dev, idx = _peer(k)
            copies.append(pltpu.make_async_remote_copy(
                src_ref=ins[w].at[idx] if scatter[w] else ins[w], dst_ref=lands[w].at[idx if arrivals else me],
                send_sem=send_sems[w].at[k - 1], recv_sem=recv_sems[w].at[k - 1], device_id=dev, device_id_type=MESH))
    return copies


def _exchange_start(arrays, scatter, name):
    n = len(arrays)
    land_shapes = [(N_DEV,) + tuple(a.shape[1:] if sc else a.shape) for a, sc in zip(arrays, scatter)]

    def body(*refs):
        ins, lands = refs[:n], refs[n:2 * n]
        send_sems, recv_sems = refs[2 * n:3 * n], refs[3 * n:4 * n]
        token = refs[6 * n]
        for cp in _peer_copies(ins, lands, send_sems, recv_sems, scatter, False):
            cp.start()
        token[...] = jnp.zeros_like(token)

    sem = pltpu.SemaphoreType.DMA((N_DEV - 1,))
    outs = pl.pallas_call(
        body,
        in_specs=[_HBM] * (2 * n),
        out_specs=[_SEM] * (2 * n) + [_HBM] * (2 * n) + [pl.BlockSpec(memory_space=pltpu.VMEM)],
        out_shape=[sem] * (2 * n) + [pltpu.HBM(a.shape, a.dtype) for a in arrays]
        + [pltpu.HBM(sh, a.dtype) for sh, a in zip(land_shapes, arrays)] + [SDS((8, LANES), F32)],
        input_output_aliases={i: 2 * n + i for i in range(2 * n)},
        compiler_params=pltpu.CompilerParams(has_side_effects=_DATAFLOW),
        name=name,
    )(*[pltpu.with_memory_space_constraint(a, pltpu.HBM) for a in arrays],
      *[pltpu.with_memory_space_constraint(lax.empty(sh, a.dtype), pltpu.HBM) for sh, a in zip(land_shapes, arrays)])
    handles = [dict(send=outs[w], recv=outs[n + w], src=outs[2 * n + w], land=outs[3 * n + w], scatter=scatter[w])
               for w in range(n)]
    return handles, outs[4 * n]


def _exchange_wait(handles, after, name):
    n = len(handles)
    scatter = [h["scatter"] for h in handles]

    def body(*refs):
        ins, lands = refs[:n], refs[n:2 * n]
        send_sems, recv_sems = refs[2 * n:3 * n], refs[3 * n:4 * n]
        for cp in _peer_copies(ins, lands, send_sems, recv_sems, scatter, False):
            cp.wait_send()
        for cp in _peer_copies(ins, lands, send_sems, recv_sems, scatter, True):
            cp.wait_recv()

    srcs, lands = [h["src"] for h in handles], [h["land"] for h in handles]
    outs = pl.pallas_call(
        body,
        in_specs=[_HBM] * (2 * n) + [_SEM] * (2 * n) + [pl.BlockSpec(memory_space=pl.ANY)],
        out_specs=[_HBM] * (2 * n),
        out_shape=[pltpu.HBM(a.shape, a.dtype) for a in srcs + lands],
        input_output_aliases={i: i for i in range(2 * n)},
        compiler_params=pltpu.CompilerParams(has_side_effects=_DATAFLOW),
        name=name,
    )(*srcs, *lands, *[h["send"] for h in handles], *[h["recv"] for h in handles], after)
    return outs[n:]


def _with_own_slot(land, own):
    return lax.dynamic_update_slice(land, own[None], (_my_index(),) + (0,) * own.ndim)


def _adamw(parts, w, m, v, name):
    rows, cols = w.shape
    tr = rows // 4 if rows % 32 == 0 else rows

    def body(p_ref, w_ref, m_ref, v_ref, g_ref, d_ref, mo_ref, vo_ref):
        g = p_ref[0].astype(F32)
        for d in range(1, N_DEV):
            g = g + p_ref[d].astype(F32)
        m_new = ADAM_B1 * m_ref[...] + (1.0 - ADAM_B1) * g
        v_new = ADAM_B2 * v_ref[...] + (1.0 - ADAM_B2) * (g * g)
        m_hat = m_new / (1.0 - ADAM_B1 ** ADAM_STEP)
        v_hat = v_new / (1.0 - ADAM_B2 ** ADAM_STEP)
        g_ref[...] = g
        d_ref[...] = -ADAM_LR * (m_hat / (jnp.sqrt(v_hat) + ADAM_EPS) + ADAM_WD * w_ref[...])
        mo_ref[...] = m_new
        vo_ref[...] = v_new

    blk = pl.BlockSpec((tr, cols), lambda i: (i, 0))
    return pl.pallas_call(
        body,
        grid=(rows // tr,),
        in_specs=[pl.BlockSpec((N_DEV, tr, cols), lambda i: (0, i, 0)), blk, blk, blk],
        out_specs=[blk] * 4,
        out_shape=[SDS((rows, cols), F32)] * 4,
        compiler_params=_cparams("arbitrary"),
        name=name,
    )(parts, w, m, v)


_SMALL = (("w_pool", 512), ("norm1_g", 8), ("norm2_g", 8), ("final_g", 8), ("pool_scale", 8), ("b_forget", 8),
          ("loss", 8))
_SMALL_ROWS = sum(r for _, r in _SMALL)


def _pack_small(vals):
    parts = []
    for name, rows in _SMALL:
        flat = vals[name].reshape(-1).astype(F32)
        parts.append(jnp.pad(flat, (0, rows * LANES - flat.shape[0])).reshape(rows, LANES))
    return jnp.concatenate(parts, axis=0)


def _unpack_small(packed, like):
    out, r0 = {}, 0
    for name, rows in _SMALL:
        n = like[name].size
        out[name] = packed[r0:r0 + rows].reshape(-1)[:n].reshape(like[name].shape)
        r0 += rows
    return out


def kernel(x, norm1_g, w_in, b_forget, w_pool, pool_scale, w_out, norm2_g, w_gate, w_up, w_down, final_g, loss_target, m_norm1_g, m_w_in, m_b_forget, m_w_pool, m_pool_scale, m_w_out, m_norm2_g, m_w_gate, m_w_up, m_w_down, m_final_g, v_norm1_g, v_w_in, v_b_forget, v_w_pool, v_pool_scale, v_w_out, v_norm2_g, v_w_gate, v_w_up, v_w_down, v_final_g):
    big = ("w_in", "w_out", "w_gate", "w_up", "w_down")
    order = ("norm1_g", "w_in", "b_forget", "w_pool", "pool_scale", "w_out", "norm2_g", "w_gate", "w_up", "w_down",
             "final_g")
    w = dict(norm1_g=norm1_g, w_in=w_in, b_forget=b_forget, w_pool=w_pool, pool_scale=pool_scale, w_out=w_out,
             norm2_g=norm2_g, w_gate=w_gate, w_up=w_up, w_down=w_down, final_g=final_g)
    m = dict(norm1_g=m_norm1_g, w_in=m_w_in, b_forget=m_b_forget, w_pool=m_w_pool, pool_scale=m_pool_scale,
             w_out=m_w_out, norm2_g=m_norm2_g, w_gate=m_w_gate, w_up=m_w_up, w_down=m_w_down, final_g=m_final_g)
    v = dict(norm1_g=v_norm1_g, w_in=v_w_in, b_forget=v_b_forget, w_pool=v_w_pool, pool_scale=v_pool_scale,
             w_out=v_w_out, norm2_g=v_norm2_g, w_gate=v_w_gate, w_up=v_w_up, w_down=v_w_down, final_g=v_final_g)

    shards = [w[n][0].astype(BF16) for n in big]
    gather, _ = _exchange_start(shards, [False] * len(big), "gather_start")
    gather = dict(zip(big, zip(gather, shards)))

    def gathered(names, after):
        lands = _exchange_wait([gather[n][0] for n in names], after, "gather_wait_" + names[0])
        return [_with_own_slot(land, gather[n][1]) for n, land in zip(names, lands)]

    def weight(name, after):
        if name == "w_in":
            full = gathered(["w_in"], after)[0].transpose(1, 0, 2).reshape(D_MODEL, IN_W)
            f0 = QKV_W + N_HEADS
            return jnp.concatenate([full[:, :QKV_W], full[:, f0:], full[:, QKV_W:f0],
                                    jnp.zeros((D_MODEL, IN_PAD - IN_W), BF16)], axis=1)
        if name == "w_out":
            return gathered(["w_out"], after)[0].reshape(D_MODEL, D_MODEL)
        if name == "w_gate_up":
            return [g.transpose(1, 0, 2).reshape(D_MODEL, D_FF) for g in gathered(["w_gate", "w_up"], after)]
        return gathered(["w_down"], after)[0].reshape(D_FF, D_MODEL)

    cols = lambda g: g.reshape(g.shape[0], N_DEV, g.shape[1] // N_DEV).transpose(1, 0, 2)
    rows = lambda g: g.reshape(N_DEV, g.shape[0] // N_DEV, g.shape[1])
    sent = {}

    def emit(name, grad):
        if name == "w_gate_up":
            names, slots = ["w_gate", "w_up"], [cols(g) for g in grad]
        else:
            names, slots = [name], [cols(grad).astype(BF16) if name == "w_in" else rows(grad)]
        handles, token = _exchange_start(slots, [True] * len(slots), "grads_start_" + name)
        for n, hd, sl in zip(names, handles, slots):
            sent[n] = (hd, sl)
        return token

    loss_row, dx, small_grads = _local_step(x[0], loss_target[0], w, weight, emit)

    packed = _pack_small(dict(small_grads, loss=0.5 / D_MODEL * jnp.sum(loss_row)))
    (small_handle,), _ = _exchange_start([packed], [False], "grads_start_replicated")

    me = _my_index()
    outs = {}
    for name in ("w_down", "w_gate", "w_up", "w_out", "w_in"):
        handle, slots = sent[name]
        (land,) = _exchange_wait([handle], dx, "grads_wait_" + name)
        parts = _with_own_slot(land, lax.dynamic_index_in_dim(slots, me, 0, keepdims=False))
        outs[name] = [a[None] for a in _adamw(parts, w[name][0], m[name][0], v[name][0], "adamw_" + name)]
    (land,) = _exchange_wait([small_handle], dx, "grads_wait_replicated")
    zero = dict(loss=jnp.zeros((), F32))
    small = _adamw(_with_own_slot(land, packed), _pack_small(dict(w, **zero)), _pack_small(dict(m, **zero)),
                   _pack_small(dict(v, **zero)), "adamw_replicated")
    small = [_unpack_small(p, dict(w, **zero)) for p in small]
    loss = small[0]["loss"]
    for name in order:
        if name not in outs:
            outs[name] = [p[name] for p in small]

    return (loss, dx[None]) + tuple(outs[n][k] for k in range(4) for n in order)
```

```python
import functools

import jax
import jax.numpy as jnp
from jax import lax
from jax.experimental import pallas as pl
from jax.experimental.pallas import tpu as pltpu

F32 = jnp.float32
BF16 = jnp.bfloat16
SDS = jax.ShapeDtypeStruct

D_MODEL = 1024
ATTN_W = 512
N_HEADS = 8
HEAD_DIM = 64
N_PAIRS = N_HEADS // 2
POOL_W = 512
POOL_WINDOWS = (2, 4, 8, 16)
POOL_G = 128
HALO = 16
IN_W = 3 * ATTN_W + N_HEADS + POOL_W
QKV_W = 3 * ATTN_W
U_OFF = QKV_W
F_OFF = QKV_W + POOL_W
IN_PAD = F_OFF + 128
D_FF = 2816
EPS = 1e-6
NEG = -1e30
N_DEV = 8
LANES = 128

ADAM_LR = 0.001
ADAM_B1 = 0.9
ADAM_B2 = 0.999
ADAM_EPS = 1e-08
ADAM_WD = 0.01
ADAM_STEP = 10

VMEM_LIMIT_BYTES = 56 * 1024 * 1024
MESH = pl.DeviceIdType.MESH
NT = (((1,), (1,)), ((), ()))
TN = (((0,), (0,)), ((), ()))


def _cparams(*sem):
    return pltpu.CompilerParams(dimension_semantics=sem or None, vmem_limit_bytes=VMEM_LIMIT_BYTES)


def _split3(a):
    hi = a.astype(BF16)
    r1 = a - hi.astype(F32)
    mid = r1.astype(BF16)
    lo = (r1 - mid.astype(F32)).astype(BF16)
    return hi, mid, lo


def _dot_sel(a, sel, dims=None):
    sb = sel.astype(BF16)
    if dims is None:
        return sum(jnp.dot(p, sb, preferred_element_type=F32) for p in _split3(a))
    return sum(lax.dot_general(p, sb, dims, preferred_element_type=F32) for p in _split3(a))


def _sel_dot(sel, a, dims=None):
    sb = sel.astype(BF16)
    if dims is None:
        return sum(jnp.dot(sb, p, preferred_element_type=F32) for p in _split3(a))
    return sum(lax.dot_general(sb, p, dims, preferred_element_type=F32) for p in _split3(a))


def _iota2(shape, dim):
    return lax.broadcasted_iota(jnp.int32, shape, dim)


def _norm_proj(x, g1, w_in_p, *, tm):
    s = x.shape[0]

    def body(x_ref, g_ref, w_ref, h_ref, r_ref, qkv_ref, u_ref, fl_ref):
        xv = x_ref[...]
        r = lax.rsqrt(jnp.mean(xv * xv, axis=-1, keepdims=True) + EPS)
        h = (xv * r * g_ref[...]).astype(BF16)
        h_ref[...] = h
        r_ref[...] = r
        qkv_ref[...] = jnp.dot(h, w_ref[:, 0:QKV_W], preferred_element_type=F32).astype(BF16)
        u_ref[...] = jnp.dot(h, w_ref[:, U_OFF:F_OFF], preferred_element_type=F32)
        fl_ref[...] = jnp.dot(h, w_ref[:, F_OFF:IN_PAD], preferred_element_type=F32)

    row = lambda w: pl.BlockSpec((tm, w), lambda i: (i, 0))
    full = lambda a, b: pl.BlockSpec((a, b), lambda i: (0, 0))
    return pl.pallas_call(
        body,
        grid=(s // tm,),
        in_specs=[row(D_MODEL), full(1, D_MODEL), full(D_MODEL, IN_PAD)],
        out_specs=[row(D_MODEL), row(1), row(QKV_W), row(POOL_W), row(LANES)],
        out_shape=[SDS((s, D_MODEL), BF16), SDS((s, 1), F32), SDS((s, QKV_W), BF16), SDS((s, POOL_W), F32),
                   SDS((s, LANES), F32)],
        compiler_params=_cparams("arbitrary"),
        name="norm_proj",
    )(x, g1, w_in_p)


def _head_block_masks(rows, nb):
    shift = nb.bit_length() - 1
    rr, cc = _iota2((rows, rows), 0), _iota2((rows, rows), 1)
    same = lax.shift_right_logical(rr, shift) == lax.shift_right_logical(cc, shift)
    return rr, cc, same


def _forget_cumsum(fl_t, b_rows):
    rows = fl_t.shape[0]
    nb = rows // N_HEADS

    def body(fl_ref, b_ref, c_ref):
        z = fl_ref[...] + b_ref[...]
        lf = jnp.minimum(z, 0.0) - jnp.log1p(jnp.exp(-jnp.abs(z)))
        upper = _iota2((LANES, LANES), 0) <= _iota2((LANES, LANES), 1)
        within = _dot_sel(lf, upper)
        tot = _dot_sel(lf, jnp.ones((LANES, LANES), F32))
        rr, cc, same = _head_block_masks(rows, nb)
        c_ref[...] = within + _sel_dot(same & (cc < rr), tot)

    return pl.pallas_call(body, out_shape=SDS(fl_t.shape, F32), compiler_params=_cparams(), name="forget_cumsum")(
        fl_t, b_rows)


BIAS_LANES = 3


def _augment(t, h, col, col_first):
    n = t.shape[0]
    lane = _iota2((n, LANES), 1)
    own = (lane < HEAD_DIM) if h == 0 else (lane >= HEAD_DIM)
    b0 = HEAD_DIM if h == 0 else 0
    c0, o0 = (b0, b0 + BIAS_LANES) if col_first else (b0 + BIAS_LANES, b0)
    x = jnp.where(own, t, 0.0)
    for off, piece in enumerate(_split3(col)):
        x = jnp.where(lane == c0 + off, piece.astype(F32), x)
    x = jnp.where((lane >= o0) & (lane < o0 + BIAS_LANES), 1.0, x)
    return x.astype(BF16)


def _attn_fwd(qkv, c_col, *, tq):
    s = qkv.shape[0]
    tk = tq
    nb = s // tq

    def body(q_ref, k_ref, v_ref, cq_ref, ck_ref, o_ref, lse_ref, kp_ref, vt_ref, st_ref):
        i = pl.program_id(1)

        @pl.when(i == 0)
        def _():
            def prep(jb, _):
                st = pl.multiple_of(jb * tk, tk)
                k2 = k_ref[pl.ds(st, tk), :].astype(F32)
                ck = ck_ref[pl.ds(st, tk), :]
                for h in range(2):
                    kp_ref[h * nb + jb] = _augment(k2, h, -ck[:, h:h + 1], True)
                vt_ref[jb] = v_ref[pl.ds(st, tk), :].astype(F32).T.astype(BF16)
                return 0

            lax.fori_loop(0, nb, prep, 0)

        qs = q_ref[...].astype(F32) * 0.125
        cq = cq_ref[...]
        qp = [_augment(qs, h, cq[:, h:h + 1], False) for h in range(2)]

        def logits(j):
            return tuple(lax.dot_general(kp_ref[h * nb + j], qp[h], NT, preferred_element_type=F32) for h in range(2))

        def softmax_pv(j, sts, stats, masked):
            out = []
            for h in range(2):
                m, l, acc = stats[h]
                st = sts[h]
                if masked:
                    st = jnp.where(_iota2((tk, tq), 0) <= _iota2((tk, tq), 1), st, NEG)
                m_new = jnp.maximum(m, jnp.max(st, axis=0, keepdims=True))
                alpha = jnp.exp(m - m_new)
                p = jnp.exp(st - m_new)
                l = alpha * l + jnp.sum(p, axis=0, keepdims=True)
                vt = vt_ref[j, h * HEAD_DIM:(h + 1) * HEAD_DIM, :]
                acc = alpha * acc + jnp.dot(vt, p.astype(BF16), preferred_element_type=F32)
                out.append((m_new, l, acc))
            return tuple(out)

        def put(slot, sts):
            for h in range(2):
                st_ref[2 * slot + h] = sts[h]

        def get(slot):
            return tuple(st_ref[2 * slot + h] for h in range(2))

        def step(j, stats):
            nxt = logits(j + 1)
            stats = softmax_pv(j, get(j % 2), stats, False)
            put((j + 1) % 2, nxt)
            return stats

        init = tuple((jnp.full((1, tq), NEG, F32), jnp.zeros((1, tq), F32), jnp.zeros((HEAD_DIM, tq), F32))
                     for _ in range(2))
        put(0, logits(0))
        stats = lax.fori_loop(0, i, step, init)
        (ma, la, acca), (mb, lb, accb) = softmax_pv(i, get(i % 2), stats, True)
        o_ref[...] = jnp.concatenate([acca / la, accb / lb], axis=0).T.astype(BF16)
        lse_ref[...] = jnp.where(_iota2((2, tq), 0) == 0, ma + jnp.log(la), mb + jnp.log(lb))

    return pl.pallas_call(
        body,
        grid=(N_PAIRS, nb),
        in_specs=[
            pl.BlockSpec((tq, LANES), lambda p, i: (i, p)),
            pl.BlockSpec((s, LANES), lambda p, i: (0, N_PAIRS + p)),
            pl.BlockSpec((s, LANES), lambda p, i: (0, 2 * N_PAIRS + p)),
            pl.BlockSpec((None, tq, 2), lambda p, i: (p, i, 0)),
            pl.BlockSpec((None, s, 2), lambda p, i: (p, 0, 0)),
        ],
        out_specs=[
            pl.BlockSpec((tq, LANES), lambda p, i: (i, p)),
            pl.BlockSpec((None, None, 2, tq), lambda p, i: (p, i, 0, 0)),
        ],
        out_shape=[SDS((s, ATTN_W), BF16), SDS((N_PAIRS, nb, 2, tq), F32)],
        scratch_shapes=[pltpu.VMEM((2 * nb, tk, LANES), BF16), pltpu.VMEM((nb, LANES, tk), BF16),
                        pltpu.VMEM((4, tk, tq), F32)],
        compiler_params=_cparams("arbitrary", "arbitrary"),
        name="attn_fwd",
    )(qkv, qkv, qkv, c_col, c_col)


def _pool_counts(row0, tm, w):
    t = row0 + _iota2((tm, 1), 0)
    return jnp.minimum(t + 1, w).astype(F32)


def _pool_fwd(u, w_pool, pool_scale, *, tm):
    s = u.shape[0]

    def body(u_ref, w_ref, sc_ref, pooled_ref, po_ref, tail_ref):
        i = pl.program_id(0)

        @pl.when(i == 0)
        def _():
            tail_ref[...] = jnp.zeros_like(tail_ref)

        uv = u_ref[...]
        ext = jnp.concatenate([tail_ref[...], uv], axis=0)
        tail_ref[...] = uv[tm - HALO:, :]
        for g, w in enumerate(POOL_WINDOWS):
            cols = slice(g * POOL_G, (g + 1) * POOL_G)
            acc = ext[:, cols]
            k = 1
            while k < w:
                acc = acc + pltpu.roll(acc, k, axis=0)
                k *= 2
            pooled = (acc[HALO:, :] / _pool_counts(i * tm, tm, w) - uv[:, cols]).astype(BF16)
            pooled_ref[:, cols] = pooled
            mixed = jnp.dot(pooled, w_ref[g].astype(BF16), preferred_element_type=F32)
            po_ref[:, cols] = (mixed * sc_ref[:, cols]).astype(BF16)

    row = pl.BlockSpec((tm, POOL_W), lambda i: (i, 0))
    return pl.pallas_call(
        body,
        grid=(s // tm,),
        in_specs=[row, pl.BlockSpec((len(POOL_WINDOWS), POOL_G, POOL_G), lambda i: (0, 0, 0)),
                  pl.BlockSpec((1, POOL_W), lambda i: (0, 0))],
        out_specs=[row, row],
        out_shape=[SDS((s, POOL_W), BF16), SDS((s, POOL_W), BF16)],
        scratch_shapes=[pltpu.VMEM((HALO, POOL_W), F32)],
        compiler_params=_cparams("arbitrary"),
        name="pool_fwd",
    )(u, w_pool, pool_scale)


def _out_norm2(attn_o, pool_o, w_out, x, g2, *, tm):
    s = x.shape[0]

    def body(a_ref, p_ref, w_ref, x_ref, g_ref, x1_ref, h2_ref, r_ref):
        x1 = (x_ref[...] + jnp.dot(a_ref[...], w_ref[0:ATTN_W, :], preferred_element_type=F32)
              + jnp.dot(p_ref[...], w_ref[ATTN_W:, :], preferred_element_type=F32))
        r = lax.rsqrt(jnp.mean(x1 * x1, axis=-1, keepdims=True) + EPS)
        x1_ref[...] = x1
        r_ref[...] = r
        h2_ref[...] = (x1 * r * g_ref[...]).astype(BF16)

    row = lambda w: pl.BlockSpec((tm, w), lambda i: (i, 0))
    full = lambda a, b: pl.BlockSpec((a, b), lambda i: (0, 0))
    return pl.pallas_call(
        body,
        grid=(s // tm,),
        in_specs=[row(ATTN_W), row(POOL_W), full(D_MODEL, D_MODEL), row(D_MODEL), full(1, D_MODEL)],
        out_specs=[row(D_MODEL), row(D_MODEL), row(1)],
        out_shape=[SDS((s, D_MODEL), F32), SDS((s, D_MODEL), BF16), SDS((s, 1), F32)],
        compiler_params=_cparams("arbitrary"),
        name="out_norm2",
    )(attn_o, pool_o, w_out, x, g2)


def _gate_up(h2, wg, wu, *, tm, tn):
    s = h2.shape[0]

    def body(h_ref, wg_ref, wu_ref, gate_ref, up_ref, act_ref):
        h = h_ref[...]
        gate = jnp.dot(h, wg_ref[...], preferred_element_type=F32)
        up = jnp.dot(h, wu_ref[...], preferred_element_type=F32)
        gate_ref[...] = gate
        up_ref[...] = up
        act_ref[...] = (gate * jax.nn.sigmoid(gate) * up).astype(BF16)

    wspec = pl.BlockSpec((D_MODEL, tn), lambda c, r: (0, c))
    ospec = pl.BlockSpec((tm, tn), lambda c, r: (r, c))
    return pl.pallas_call(
        body,
        grid=(D_FF // tn, s // tm),
        in_specs=[pl.BlockSpec((tm, D_MODEL), lambda c, r: (r, 0)), wspec, wspec],
        out_specs=[ospec, ospec, ospec],
        out_shape=[SDS((s, D_FF), F32), SDS((s, D_FF), F32), SDS((s, D_FF), BF16)],
        compiler_params=_cparams("arbitrary", "arbitrary"),
        name="gate_up",
    )(h2, wg, wu)


def _down_final(act, wd, x1, gf, tgt, *, tm):
    s = x1.shape[0]

    def body(a_ref, w_ref, x1_ref, g_ref, t_ref, dx2_ref, loss_ref, dgf_ref):
        @pl.when(pl.program_id(0) == 0)
        def _():
            loss_ref[...] = jnp.zeros_like(loss_ref)
            dgf_ref[...] = jnp.zeros_like(dgf_ref)

        x2 = x1_ref[...] + jnp.dot(a_ref[...], w_ref[...], preferred_element_type=F32)
        r = lax.rsqrt(jnp.mean(x2 * x2, axis=-1, keepdims=True) + EPS)
        xn = x2 * r
        g = g_ref[...]
        diff = xn * g - t_ref[...]
        loss_ref[...] += jnp.sum(diff * diff, axis=0, keepdims=True)
        dy = diff * (1.0 / D_MODEL)
        dgf_ref[...] += jnp.sum(dy * xn, axis=0, keepdims=True)
        dxn = dy * g
        dx2_ref[...] = r * (dxn - xn * jnp.mean(dxn * xn, axis=-1, keepdims=True))

    row = lambda w: pl.BlockSpec((tm, w), lambda i: (i, 0))
    full = lambda a, b: pl.BlockSpec((a, b), lambda i: (0, 0))
    return pl.pallas_call(
        body,
        grid=(s // tm,),
        in_specs=[row(D_FF), full(D_FF, D_MODEL), row(D_MODEL), full(1, D_MODEL), row(D_MODEL)],
        out_specs=[row(D_MODEL), full(1, D_MODEL), full(1, D_MODEL)],
        out_shape=[SDS((s, D_MODEL), F32), SDS((1, D_MODEL), F32), SDS((1, D_MODEL), F32)],
        compiler_params=_cparams("arbitrary"),
        name="down_final",
    )(act, wd, x1, gf, tgt)


def _swiglu_bwd(dx2, wd, gate, up, *, tm, tn):
    s = dx2.shape[0]

    def body(d_ref, w_ref, gate_ref, up_ref, dgate_ref, dup_ref):
        dact = lax.dot_general(d_ref[...].astype(BF16), w_ref[...], NT, preferred_element_type=F32)
        gate = gate_ref[...]
        sg = jax.nn.sigmoid(gate)
        dup_ref[...] = (dact * (gate * sg)).astype(BF16)
        dgate_ref[...] = (dact * up_ref[...] * (sg * (1.0 + gate * (1.0 - sg)))).astype(BF16)

    ospec = pl.BlockSpec((tm, tn), lambda c, r: (r, c))
    return pl.pallas_call(
        body,
        grid=(D_FF // tn, s // tm),
        in_specs=[pl.BlockSpec((tm, D_MODEL), lambda c, r: (r, 0)), pl.BlockSpec((tn, D_MODEL), lambda c, r: (c, 0)),
                  ospec, ospec],
        out_specs=[ospec, ospec],
        out_shape=[SDS((s, D_FF), BF16), SDS((s, D_FF), BF16)],
        compiler_params=_cparams("arbitrary", "arbitrary"),
        name="swiglu_bwd",
    )(dx2, wd, gate, up)


def _mm_tn(a, bs, *, ta, ts, name):
    s, ka = a.shape
    n = len(bs)

    def body(a_ref, *refs):
        b_refs, o_refs = refs[:n], refs[n:]

        @pl.when(pl.program_id(1) == 0)
        def _():
            for o_ref in o_refs:
                o_ref[...] = jnp.zeros_like(o_ref)

        av = a_ref[...].astype(BF16)
        for b_ref, o_ref in zip(b_refs, o_refs):
            o_ref[...] += lax.dot_general(av, b_ref[...].astype(BF16), TN, preferred_element_type=F32)

    return pl.pallas_call(
        body,
        grid=(ka // ta, s // ts),
        in_specs=[pl.BlockSpec((ts, ta), lambda i, k: (k, i))]
        + [pl.BlockSpec((ts, b.shape[1]), lambda i, k: (k, 0)) for b in bs],
        out_specs=[pl.BlockSpec((ta, b.shape[1]), lambda i, k: (i, 0)) for b in bs],
        out_shape=[SDS((ka, b.shape[1]), F32) for b in bs],
        compiler_params=_cparams("arbitrary", "arbitrary"),
        name=name,
    )(a, *bs)


def _norm_bwd(dh, x, r, g, dres):
    xn = x * r
    dxn = dh * g
    dx = dres + r * (dxn - xn * jnp.mean(dxn * xn, axis=-1, keepdims=True))
    return dx, jnp.sum(dh * xn, axis=0, keepdims=True)


def _mlp_in_bwd(dgate, dup, wg, wu, w_out, x1, r2, g2, dx2, *, tm):
    s = x1.shape[0]

    def body(dg_ref, du_ref, wg_ref, wu_ref, wo_ref, x_ref, r_ref, g_ref, d_ref, dx1_ref, dmix_ref, dg2_ref):
        @pl.when(pl.program_id(0) == 0)
        def _():
            dg2_ref[...] = jnp.zeros_like(dg2_ref)

        dh2 = (lax.dot_general(dg_ref[...], wg_ref[...], NT, preferred_element_type=F32)
               + lax.dot_general(du_ref[...], wu_ref[...], NT, preferred_element_type=F32))
        dx1, dg2 = _norm_bwd(dh2, x_ref[...], r_ref[...], g_ref[...], d_ref[...])
        dg2_ref[...] += dg2
        dx1_ref[...] = dx1
        dmix_ref[...] = lax.dot_general(dx1.astype(BF16), wo_ref[...], NT, preferred_element_type=F32)

    row = lambda w: pl.BlockSpec((tm, w), lambda i: (i, 0))
    full = lambda a, b: pl.BlockSpec((a, b), lambda i: (0, 0))
    return pl.pallas_call(
        body,
        grid=(s // tm,),
        in_specs=[row(D_FF), row(D_FF), full(D_MODEL, D_FF), full(D_MODEL, D_FF), full(D_MODEL, D_MODEL),
                  row(D_MODEL), row(1), full(1, D_MODEL), row(D_MODEL)],
        out_specs=[row(D_MODEL), row(D_MODEL), full(1, D_MODEL)],
        out_shape=[SDS((s, D_MODEL), F32), SDS((s, D_MODEL), F32), SDS((1, D_MODEL), F32)],
        compiler_params=_cparams("arbitrary"),
        name="mlp_in_bwd",
    )(dgate, dup, wg, wu, w_out, x1, r2, g2, dx2)


def _pool_bwd(dmixed, pooled, w_pool, pool_scale, *, tm):
    s = pooled.shape[0]
    nt = s // tm
    ng = len(POOL_WINDOWS)

    def body(d_ref, p_ref, w_ref, sc_ref, du_ref, dw_ref, dsc_ref, head_ref):
        i = pl.program_id(0)

        @pl.when(i == 0)
        def _():
            head_ref[...] = jnp.zeros_like(head_ref)
            dw_ref[...] = jnp.zeros_like(dw_ref)
            dsc_ref[...] = jnp.zeros_like(dsc_ref)

        row0 = (nt - 1 - i) * tm
        for g, w in enumerate(POOL_WINDOWS):
            cols = slice(g * POOL_G, (g + 1) * POOL_G)
            wb = w_ref[g].astype(BF16)
            pooled_g = p_ref[:, cols]
            dpo = d_ref[:, cols]
            mixed = jnp.dot(pooled_g, wb, preferred_element_type=F32)
            dsc_ref[:, cols] += jnp.sum(dpo * mixed, axis=0, keepdims=True)
            dmp = (dpo * sc_ref[:, cols]).astype(BF16)
            dw_ref[g] += lax.dot_general(pooled_g, dmp, TN, preferred_element_type=F32)
            dpooled = lax.dot_general(dmp, wb, NT, preferred_element_type=F32)
            a = dpooled / _pool_counts(row0, tm, w)
            acc = jnp.concatenate([a, head_ref[:, cols]], axis=0)
            head_ref[:, cols] = a[0:HALO, :]
            k = 1
            while k < w:
                acc = acc + pltpu.roll(acc, tm + HALO - k, axis=0)
                k *= 2
            du_ref[:, cols] = (acc[0:tm, :] - dpooled).astype(BF16)

    rev = lambda i: (nt - 1 - i, 0)
    return pl.pallas_call(
        body,
        grid=(nt,),
        in_specs=[pl.BlockSpec((tm, POOL_W), lambda i: (nt - 1 - i, 1)), pl.BlockSpec((tm, POOL_W), rev),
                  pl.BlockSpec((ng, POOL_G, POOL_G), lambda i: (0, 0, 0)), pl.BlockSpec((1, POOL_W), lambda i: (0, 0))],
        out_specs=[pl.BlockSpec((tm, POOL_W), rev), pl.BlockSpec((ng, POOL_G, POOL_G), lambda i: (0, 0, 0)),
                   pl.BlockSpec((1, POOL_W), lambda i: (0, 0))],
        out_shape=[SDS((s, POOL_W), BF16), SDS((ng, POOL_G, POOL_G), F32), SDS((1, POOL_W), F32)],
        scratch_shapes=[pltpu.VMEM((HALO, POOL_W), F32)],
        compiler_params=_cparams("arbitrary"),
        name="pool_bwd",
    )(dmixed, pooled, w_pool, pool_scale)


def _attn_bwd(qkv, attn_o, dmixed, rowb, ck_col, *, tq):
    s = qkv.shape[0]
    tk = tq
    nb = s // tq

    def body(q_ref, k_ref, v_ref, o_ref, do_ref, rowb_ref, ck_ref, dq_ref, dk_ref, dv_ref, dck_ref, dcq_ref, dq_acc,
             delta_ref):
        lane = _iota2((tq, LANES), 1)
        lo = lane < HEAD_DIM
        first = _iota2((8, LANES), 1) < HEAD_DIM
        sel = jnp.where(_iota2((8, LANES), 0) < 4, jnp.where(first, 1.0, 0.0), jnp.where(first, 0.0, 1.0))
        row8 = _iota2((8, tq), 0)

        def prep(i, _):
            st = pl.multiple_of(i * tq, tq)
            prod = do_ref[pl.ds(st, tq), :] * o_ref[pl.ds(st, tq), :].astype(F32)
            delta_ref[i] = _sel_dot(sel, prod, NT)
            dq_acc[pl.ds(st, tq), :] = jnp.zeros((tq, LANES), F32)
            dcq_ref[i] = jnp.zeros((8, tq), F32)
            return 0

        lax.fori_loop(0, nb, prep, 0)

        def split(t):
            z = jnp.zeros_like(t)
            return jnp.where(lo, t, z), jnp.where(lo, z, t)

        def kv_block(j, _):
            st_j = pl.multiple_of(j * tk, tk)
            ks = split(k_ref[pl.ds(st_j, tk), :])
            vs = split(v_ref[pl.ds(st_j, tk), :])
            kcat = jnp.concatenate(ks, axis=0)
            ck = ck_ref[pl.ds(st_j, tk), :]

            def q_block(i, carry, masked):
                dk_acc, dv_acc, dca, dcb = carry
                st_i = pl.multiple_of(i * tq, tq)
                q2 = q_ref[pl.ds(st_i, tq), :]
                do2 = do_ref[pl.ds(st_i, tq), :].astype(BF16)
                rb = rowb_ref[i]
                dl = delta_ref[i]
                pts, dsts = [], []
                for h in range(2):
                    st = lax.dot_general(ks[h], q2, NT, preferred_element_type=F32) * 0.125
                    st = st + rb[h:h + 1, :] - ck[:, h:h + 1]
                    if masked:
                        st = jnp.where(_iota2((tk, tq), 0) <= _iota2((tk, tq), 1), st, NEG)
                    pt = jnp.exp(st)
                    dpt = lax.dot_general(vs[h], do2, NT, preferred_element_type=F32)
                    pts.append(pt.astype(BF16))
                    dsts.append(pt * (dpt - dl[4 * h:4 * h + 1, :]))
                dca = dca + jnp.sum(dsts[0], axis=1, keepdims=True)
                dcb = dcb + jnp.sum(dsts[1], axis=1, keepdims=True)
                dcq_ref[i] += jnp.where(row8 == 0, jnp.sum(dsts[0], axis=0, keepdims=True),
                                        jnp.where(row8 == 4, jnp.sum(dsts[1], axis=0, keepdims=True), 0.0))
                dsb = [d.astype(BF16) for d in dsts]
                dv_acc = dv_acc + jnp.dot(jnp.concatenate(pts, axis=1), jnp.concatenate(split(do2), axis=0),
                                          preferred_element_type=F32)
                dk_acc = dk_acc + jnp.dot(jnp.concatenate(dsb, axis=1), jnp.concatenate(split(q2), axis=0),
                                          preferred_element_type=F32)
                dq_acc[pl.ds(st_i, tq), :] += lax.dot_general(jnp.concatenate(dsb, axis=0), kcat, TN,
                                                              preferred_element_type=F32)
                return dk_acc, dv_acc, dca, dcb

            zt = jnp.zeros((tk, LANES), F32)
            zc = jnp.zeros((tk, 1), F32)
            carry = q_block(j, (zt, zt, zc, zc), True)
            dk_acc, dv_acc, dca, dcb = lax.fori_loop(j + 1, nb, lambda i, c: q_block(i, c, False), carry)
            dk_ref[pl.ds(st_j, tk), :] = (dk_acc * 0.125).astype(BF16)
            dv_ref[pl.ds(st_j, tk), :] = dv_acc.astype(BF16)
            dck_ref[pl.ds(st_j, tk), :] = jnp.where(_iota2((tk, 2), 1) == 0, dca, dcb)
            return 0

        lax.fori_loop(0, nb, kv_block, 0)
        dq_ref[...] = (dq_acc[...] * 0.125).astype(BF16)

    col = lambda off: pl.BlockSpec((s, LANES), lambda p: (0, off + p))
    return pl.pallas_call(
        body,
        grid=(N_PAIRS,),
        in_specs=[col(0), col(N_PAIRS), col(2 * N_PAIRS), col(0), col(0),
                  pl.BlockSpec((None, nb, 2, tq), lambda p: (p, 0, 0, 0)),
                  pl.BlockSpec((None, s, 2), lambda p: (p, 0, 0))],
        out_specs=[col(0), col(0), col(0), pl.BlockSpec((None, s, 2), lambda p: (p, 0, 0)),
                   pl.BlockSpec((None, nb, 8, tq), lambda p: (p, 0, 0, 0))],
        out_shape=[SDS((s, ATTN_W), BF16), SDS((s, ATTN_W), BF16), SDS((s, ATTN_W), BF16), SDS((N_PAIRS, s, 2), F32),
                   SDS((N_PAIRS, nb, 8, tq), F32)],
        scratch_shapes=[pltpu.VMEM((s, LANES), F32), pltpu.VMEM((nb, 8, tq), F32)],
        compiler_params=_cparams("arbitrary"),
        name="attn_bwd",
    )(qkv, qkv, qkv, attn_o, dmixed, rowb, ck_col)


def _forget_bwd(dc_t, fl_t, b_rows):
    rows = fl_t.shape[0]
    nb = rows // N_HEADS

    def body(dc_ref, fl_ref, b_ref, dfl_ref, db_ref):
        dc = dc_ref[...]
        lower = _iota2((LANES, LANES), 0) >= _iota2((LANES, LANES), 1)
        ones = jnp.ones((LANES, LANES), F32)
        rr, cc, same = _head_block_masks(rows, nb)
        dlf = _dot_sel(dc, lower) + _sel_dot(same & (cc > rr), _dot_sel(dc, ones))
        dfl = dlf / (1.0 + jnp.exp(fl_ref[...] + b_ref[...]))
        dfl_ref[...] = dfl
        shift = nb.bit_length() - 1
        hsel = lax.shift_right_logical(_iota2((N_HEADS, rows), 1), shift) == _iota2((N_HEADS, rows), 0)
        db_ref[...] = _sel_dot(hsel, _dot_sel(dfl, ones))

    return pl.pallas_call(body, out_shape=[SDS(fl_t.shape, F32), SDS((N_HEADS, LANES), F32)],
                          compiler_params=_cparams(), name="forget_bwd")(dc_t, fl_t, b_rows)


def _in_bwd(dq, dk, dv, du, dfl, w_in_p, x, r1, g1, dx1, *, tm):
    s = x.shape[0]
    pieces = ((0, ATTN_W), (ATTN_W, 2 * ATTN_W), (2 * ATTN_W, QKV_W), (U_OFF, F_OFF), (F_OFF, IN_PAD))

    def body(dq_ref, dk_ref, dv_ref, du_ref, df_ref, w_ref, x_ref, r_ref, g_ref, d_ref, dx_ref, dg1_ref):
        @pl.when(pl.program_id(0) == 0)
        def _():
            dg1_ref[...] = jnp.zeros_like(dg1_ref)

        dh = None
        for ref, (c0, c1) in zip((dq_ref, dk_ref, dv_ref, du_ref, df_ref), pieces):
            t = lax.dot_general(ref[...], w_ref[:, c0:c1], NT, preferred_element_type=F32)
            dh = t if dh is None else dh + t
        dx, dg1 = _norm_bwd(dh, x_ref[...], r_ref[...], g_ref[...], d_ref[...])
        dx_ref[...] = dx
        dg1_ref[...] += dg1

    row = lambda w: pl.BlockSpec((tm, w), lambda i: (i, 0))
    full = lambda a, b: pl.BlockSpec((a, b), lambda i: (0, 0))
    return pl.pallas_call(
        body,
        grid=(s // tm,),
        in_specs=[row(ATTN_W), row(ATTN_W), row(ATTN_W), row(POOL_W), row(LANES), full(D_MODEL, IN_PAD),
                  row(D_MODEL), row(1), full(1, D_MODEL), row(D_MODEL)],
        out_specs=[row(D_MODEL), full(1, D_MODEL)],
        out_shape=[SDS((s, D_MODEL), F32), SDS((1, D_MODEL), F32)],
        compiler_params=_cparams("arbitrary"),
        name="in_bwd",
    )(dq, dk, dv, du, dfl, w_in_p, x, r1, g1, dx1)


def _tiles(s):
    big = min(512, s)
    return dict(row=big, attn=min(256, s // 2), mlp_bwd=min(256, s))


def _tie(a, token):
    return a + token[0:1, 0:1].astype(a.dtype)


def _local_step(x, tgt, p, weight, emit):
    s = x.shape[0]
    t = _tiles(s)
    tm, tq = t["row"], t["attn"]
    nb = s // LANES
    nqb = s // tq
    g1, g2, gf = p["norm1_g"], p["norm2_g"], p["final_g"].reshape(1, D_MODEL)
    w_pool, pool_scale = p["w_pool"][0], p["pool_scale"]

    w_in_p = weight("w_in", x)
    h, r1, qkv, u, fl = _norm_proj(x, g1, w_in_p, tm=tm)
    fl_t = fl[:, :N_HEADS].T.reshape(N_HEADS * nb, LANES)
    b_rows = jnp.repeat(p["b_forget"].reshape(N_HEADS), nb).reshape(N_HEADS * nb, 1)
    c = _forget_cumsum(fl_t, b_rows).reshape(N_PAIRS, 2, s)
    c_col = c.transpose(0, 2, 1)
    c_rowblk = c.reshape(N_PAIRS, 2, nqb, tq).transpose(0, 2, 1, 3)
    attn_o, lse = _attn_fwd(qkv, c_col, tq=tq)
    pooled, pool_o = _pool_fwd(u, w_pool, pool_scale, tm=tm)
    w_out = weight("w_out", attn_o)
    x1, h2, r2 = _out_norm2(attn_o, pool_o, w_out, x, g2, tm=tm)
    wg, wu = weight("w_gate_up", h2)
    gate, up, act = _gate_up(h2, wg, wu, tm=tm, tn=D_FF // 2)
    wd = weight("w_down", act)
    dx2, loss_row, d_gf = _down_final(act, wd, x1, gf, tgt, tm=tm)

    dgate, dup = _swiglu_bwd(dx2, wd, gate, up, tm=tm, tn=D_FF // 2)
    (d_wd,) = _mm_tn(act, [dx2], ta=D_FF // 2, ts=tm, name="grad_w_down")
    token = emit("w_down", d_wd)
    token = token + emit("w_gate_up", _mm_tn(h2, [dgate, dup], ta=tm, ts=tm, name="grad_w_gate_up"))
    dx1, dmixed, d_g2 = _mlp_in_bwd(dgate, dup, wg, wu, w_out, x1, r2, _tie(g2, token), dx2, tm=t["mlp_bwd"])
    du, d_wpool, d_pscale = _pool_bwd(dmixed, pooled, w_pool, pool_scale, tm=tm)
    (d_wo_a,) = _mm_tn(attn_o, [dx1], ta=ATTN_W, ts=tm, name="grad_w_out_attn")
    (d_wo_p,) = _mm_tn(pool_o, [dx1], ta=POOL_W, ts=tm, name="grad_w_out_pool")
    token = emit("w_out", jnp.concatenate([d_wo_a, d_wo_p], axis=0))
    rowb = _tie(c_rowblk - lse, token)
    dq, dk, dv, dck, dcq = _attn_bwd(qkv, attn_o, dmixed, rowb, c_col, tq=tq)
    dcq = dcq[:, :, 0::4, :].transpose(0, 2, 1, 3).reshape(N_PAIRS, 2, s)
    dc_t = (dcq - dck.transpose(0, 2, 1)).reshape(N_HEADS * nb, LANES)
    dfl_t, db = _forget_bwd(dc_t, fl_t, b_rows)
    dfl = jnp.pad(dfl_t.reshape(N_HEADS, s).T, ((0, 0), (0, LANES - N_HEADS))).astype(BF16)
    d_wq, d_wk, d_wv, d_wu_in, d_wf = _mm_tn(h, [dq, dk, dv, du, dfl], ta=D_MODEL, ts=tm, name="grad_w_in")
    token = emit("w_in", jnp.concatenate([d_wq, d_wk, d_wv, d_wf[:, :N_HEADS], d_wu_in], axis=1))
    dx, d_g1 = _in_bwd(dq, dk, dv, du, dfl, w_in_p, x, r1, _tie(g1, token), dx1, tm=tm)

    small = dict(norm1_g=d_g1, b_forget=db[:, 0].reshape(1, N_HEADS), w_pool=d_wpool, pool_scale=d_pscale,
                 norm2_g=d_g2, final_g=d_gf)
    return loss_row, dx, small


def _my_index():
    return 4 * lax.axis_index("x") + 2 * lax.axis_index("y") + lax.axis_index("c")


def _peer(k):
    pos = [lax.axis_index(a) for a in ("x", "y", "c")]
    flipped = tuple(1 - p if (k >> b) & 1 else p for p, b in zip(pos, (2, 1, 0)))
    return flipped, 4 * flipped[0] + 2 * flipped[1] + flipped[2]


_HBM = pl.BlockSpec(memory_space=pltpu.HBM)
_SEM = pl.BlockSpec(memory_space=pltpu.SEMAPHORE)
_DATAFLOW = pltpu.SideEffectType.DATAFLOW_SIDE_EFFECTING


def _peer_copies(ins, lands, send_sems, recv_sems, scatter, arrivals):
    me = _my_index()
    copies = []
    for w in range(len(ins)):
        for k in range(1, N_DEV):
            dev, idx = _peer(k)
            copies.append(pltpu.make_async_remote_copy(
                src_ref=ins[w].at[idx] if scatter[w] else ins[w], dst_ref=lands[w].at[idx if arrivals else me],
                send_sem=send_sems[w].at[k - 1], recv_sem=recv_sems[w].at[k - 1], device_id=dev, device_id_type=MESH))
    return copies


def _exchange_start(arrays, scatter, name):
    n = len(arrays)
    land_shapes = [(N_DEV,) + tuple(a.shape[1:] if sc else a.shape) for a, sc in zip(arrays, scatter)]

    def body(*refs):
        ins, lands = refs[:n], refs[n:2 * n]
        send_sems, recv_sems = refs[2 * n:3 * n], refs[3 * n:4 * n]
        token = refs[6 * n]
        for cp in _peer_copies(ins, lands, send_sems, recv_sems, scatter, False):
            cp.start()
        token[...] = jnp.zeros_like(token)

    sem = pltpu.SemaphoreType.DMA((N_DEV - 1,))
    outs = pl.pallas_call(
        body,
        in_specs=[_HBM] * (2 * n),
        out_specs=[_SEM] * (2 * n) + [_HBM] * (2 * n) + [pl.BlockSpec(memory_space=pltpu.VMEM)],
        out_shape=[sem] * (2 * n) + [pltpu.HBM(a.shape, a.dtype) for a in arrays]
        + [pltpu.HBM(sh, a.dtype) for sh, a in zip(land_shapes, arrays)] + [SDS((8, LANES), F32)],
        input_output_aliases={i: 2 * n + i for i in range(2 * n)},
        compiler_params=pltpu.CompilerParams(has_side_effects=_DATAFLOW),
        name=name,
    )(*[pltpu.with_memory_space_constraint(a, pltpu.HBM) for a in arrays],
      *[pltpu.with_memory_space_constraint(lax.empty(sh, a.dtype), pltpu.HBM) for sh, a in zip(land_shapes, arrays)])
    handles = [dict(send=outs[w], recv=outs[n + w], src=outs[2 * n + w], land=outs[3 * n + w], scatter=scatter[w])
               for w in range(n)]
    return handles, outs[4 * n]


def _exchange_wait(handles, after, name):
    n = len(handles)
    scatter = [h["scatter"] for h in handles]

    def body(*refs):
        ins, lands = refs[:n], refs[n:2 * n]
        send_sems, recv_sems = refs[2 * n:3 * n], refs[3 * n:4 * n]
        for cp in _peer_copies(ins, lands, send_sems, recv_sems, scatter, False):
            cp.wait_send()
        for cp in _peer_copies(ins, lands, send_sems, recv_sems, scatter, True):
            cp.wait_recv()

    srcs, lands = [h["src"] for h in handles], [h["land"] for h in handles]
    outs = pl.pallas_call(
        body,
        in_specs=[_HBM] * (2 * n) + [_SEM] * (2 * n) + [pl.BlockSpec(memory_space=pl.ANY)],
        out_specs=[_HBM] * (2 * n),
        out_shape=[pltpu.HBM(a.shape, a.dtype) for a in srcs + lands],
        input_output_aliases={i: i for i in range(2 * n)},
        compiler_params=pltpu.CompilerParams(has_side_effects=_DATAFLOW),
        name=name,
    )(*srcs, *lands, *[h["send"] for h in handles], *[h["recv"] for h in handles], after)
    return outs[n:]


def _with_own_slot(land, own):
    return lax.dynamic_update_slice(land, own[None], (_my_index(),) + (0,) * own.ndim)


def _adamw(parts, w, m, v, name):
    rows, cols = w.shape
    tr = rows // 4 if rows % 32 == 0 else rows

    def body(p_ref, w_ref, m_ref, v_ref, g_ref, d_ref, mo_ref, vo_ref):
        g = p_ref[0].astype(F32)
        for d in range(1, N_DEV):
            g = g + p_ref[d].astype(F32)
        m_new = ADAM_B1 * m_ref[...] + (1.0 - ADAM_B1) * g
        v_new = ADAM_B2 * v_ref[...] + (1.0 - ADAM_B2) * (g * g)
        m_hat = m_new / (1.0 - ADAM_B1 ** ADAM_STEP)
        v_hat = v_new / (1.0 - ADAM_B2 ** ADAM_STEP)
        g_ref[...] = g
        d_ref[...] = -ADAM_LR * (m_hat / (jnp.sqrt(v_hat) + ADAM_EPS) + ADAM_WD * w_ref[...])
        mo_ref[...] = m_new
        vo_ref[...] = v_new

    blk = pl.BlockSpec((tr, cols), lambda i: (i, 0))
    return pl.pallas_call(
        body,
        grid=(rows // tr,),
        in_specs=[pl.BlockSpec((N_DEV, tr, cols), lambda i: (0, i, 0)), blk, blk, blk],
        out_specs=[blk] * 4,
        out_shape=[SDS((rows, cols), F32)] * 4,
        compiler_params=_cparams("arbitrary"),
        name=name,
    )(parts, w, m, v)


_SMALL = (("w_pool", 512), ("norm1_g", 8), ("norm2_g", 8), ("final_g", 8), ("pool_scale", 8), ("b_forget", 8),
          ("loss", 8))
_SMALL_ROWS = sum(r for _, r in _SMALL)


def _pack_small(vals):
    parts = []
    for name, rows in _SMALL:
        flat = vals[name].reshape(-1).astype(F32)
        parts.append(jnp.pad(flat, (0, rows * LANES - flat.shape[0])).reshape(rows, LANES))
    return jnp.concatenate(parts, axis=0)


def _unpack_small(packed, like):
    out, r0 = {}, 0
    for name, rows in _SMALL:
        n = like[name].size
        out[name] = packed[r0:r0 + rows].reshape(-1)[:n].reshape(like[name].shape)
        r0 += rows
    return out


def kernel(x, norm1_g, w_in, b_forget, w_pool, pool_scale, w_out, norm2_g, w_gate, w_up, w_down, final_g, loss_target, m_norm1_g, m_w_in, m_b_forget, m_w_pool, m_pool_scale, m_w_out, m_norm2_g, m_w_gate, m_w_up, m_w_down, m_final_g, v_norm1_g, v_w_in, v_b_forget, v_w_pool, v_pool_scale, v_w_out, v_norm2_g, v_w_gate, v_w_up, v_w_down, v_final_g):
    big = ("w_in", "w_out", "w_gate", "w_up", "w_down")
    order = ("norm1_g", "w_in", "b_forget", "w_pool", "pool_scale", "w_out", "norm2_g", "w_gate", "w_up", "w_down",
             "final_g")
    w = dict(norm1_g=norm1_g, w_in=w_in, b_forget=b_forget, w_pool=w_pool, pool_scale=pool_scale, w_out=w_out,
             norm2_g=norm2_g, w_gate=w_gate, w_up=w_up, w_down=w_down, final_g=final_g)
    m = dict(norm1_g=m_norm1_g, w_in=m_w_in, b_forget=m_b_forget, w_pool=m_w_pool, pool_scale=m_pool_scale,
             w_out=m_w_out, norm2_g=m_norm2_g, w_gate=m_w_gate, w_up=m_w_up, w_down=m_w_down, final_g=m_final_g)
    v = dict(norm1_g=v_norm1_g, w_in=v_w_in, b_forget=v_b_forget, w_pool=v_w_pool, pool_scale=v_pool_scale,
             w_out=v_w_out, norm2_g=v_norm2_g, w_gate=v_w_gate, w_up=v_w_up, w_down=v_w_down, final_g=v_final_g)

    shards = [w[n][0].astype(BF16) for n in big]
    gather, _ = _exchange_start(shards, [False] * len(big), "gather_start")
    gather = dict(zip(big, zip(gather, shards)))

    def gathered(names, after):
        lands = _exchange_wait([gather[n][0] for n in names], after, "gather_wait_" + names[0])
        return [_with_own_slot(land, gather[n][1]) for n, land in zip(names, lands)]

    def weight(name, after):
        if name == "w_in":
            full = gathered(["w_in"], after)[0].transpose(1, 0, 2).reshape(D_MODEL, IN_W)
            f0 = QKV_W + N_HEADS
            return jnp.concatenate([full[:, :QKV_W], full[:, f0:], full[:, QKV_W:f0],
                                    jnp.zeros((D_MODEL, IN_PAD - IN_W), BF16)], axis=1)
        if name == "w_out":
            return gathered(["w_out"], after)[0].reshape(D_MODEL, D_MODEL)
        if name == "w_gate_up":
            return [g.transpose(1, 0, 2).reshape(D_MODEL, D_FF) for g in gathered(["w_gate", "w_up"], after)]
        return gathered(["w_down"], after)[0].reshape(D_FF, D_MODEL)

    cols = lambda g: g.reshape(g.shape[0], N_DEV, g.shape[1] // N_DEV).transpose(1, 0, 2)
    rows = lambda g: g.reshape(N_DEV, g.shape[0] // N_DEV, g.shape[1])
    sent = {}

    def emit(name, grad):
        if name == "w_gate_up":
            names, slots = ["w_gate", "w_up"], [cols(g) for g in grad]
        else:
            names, slots = [name], [cols(grad).astype(BF16) if name == "w_in" else rows(grad)]
        handles, token = _exchange_start(slots, [True] * len(slots), "grads_start_" + name)
        for n, hd, sl in zip(names, handles, slots):
            sent[n] = (hd, sl)
        return token

    loss_row, dx, small_grads = _local_step(x[0], loss_target[0], w, weight, emit)

    packed = _pack_small(dict(small_grads, loss=0.5 / D_MODEL * jnp.sum(loss_row)))
    (small_handle,), _ = _exchange_start([packed], [False], "grads_start_replicated")

    me = _my_index()
    outs = {}
    after = dx
    for name in ("w_down", "w_gate", "w_up", "w_out", "w_in"):
        handle, slots = sent[name]
        (land,) = _exchange_wait([handle], after, "grads_wait_" + name)
        parts = _with_own_slot(land, lax.dynamic_index_in_dim(slots, me, 0, keepdims=False))
        outs[name] = _adamw(parts, w[name][0], m[name][0], v[name][0], "adamw_" + name)
        after = outs[name][0]
        outs[name] = [a[None] for a in outs[name]]
    (land,) = _exchange_wait([small_handle], after, "grads_wait_replicated")
    zero = dict(loss=jnp.zeros((), F32))
    small = _adamw(_with_own_slot(land, packed), _pack_small(dict(w, **zero)), _pack_small(dict(m, **zero)),
                   _pack_small(dict(v, **zero)), "adamw_replicated")
    small = [_unpack_small(p, dict(w, **zero)) for p in small]
    loss = small[0]["loss"]
    for name in order:
        if name not in outs:
            outs[name] = [p[name] for p in small]

    return (loss, dx[None]) + tuple(outs[n][k] for k in range(4) for n in order)
```

```python
import functools

import jax
import jax.numpy as jnp
from jax import lax
from jax.experimental import pallas as pl
from jax.experimental.pallas import tpu as pltpu

F32 = jnp.float32
BF16 = jnp.bfloat16
SDS = jax.ShapeDtypeStruct

D_MODEL = 1024
ATTN_W = 512
N_HEADS = 8
HEAD_DIM = 64
N_PAIRS = N_HEADS // 2
POOL_W = 512
POOL_WINDOWS = (2, 4, 8, 16)
POOL_G = 128
HALO = 16
IN_W = 3 * ATTN_W + N_HEADS + POOL_W
QKV_W = 3 * ATTN_W
U_OFF = QKV_W
F_OFF = QKV_W + POOL_W
IN_PAD = F_OFF + 128
D_FF = 2816
EPS = 1e-6
NEG = -1e30
N_DEV = 8
LANES = 128

ADAM_LR = 0.001
ADAM_B1 = 0.9
ADAM_B2 = 0.999
ADAM_EPS = 1e-08
ADAM_WD = 0.01
ADAM_STEP = 10

VMEM_LIMIT_BYTES = 56 * 1024 * 1024
MESH = pl.DeviceIdType.MESH
NT = (((1,), (1,)), ((), ()))
TN = (((0,), (0,)), ((), ()))


def _cparams(*sem):
    return pltpu.CompilerParams(dimension_semantics=sem or None, vmem_limit_bytes=VMEM_LIMIT_BYTES)


def _split3(a):
    hi = a.astype(BF16)
    r1 = a - hi.astype(F32)
    mid = r1.astype(BF16)
    lo = (r1 - mid.astype(F32)).astype(BF16)
    return hi, mid, lo


def _dot_sel(a, sel, dims=None):
    sb = sel.astype(BF16)
    if dims is None:
        return sum(jnp.dot(p, sb, preferred_element_type=F32) for p in _split3(a))
    return sum(lax.dot_general(p, sb, dims, preferred_element_type=F32) for p in _split3(a))


def _sel_dot(sel, a, dims=None):
    sb = sel.astype(BF16)
    if dims is None:
        return sum(jnp.dot(sb, p, preferred_element_type=F32) for p in _split3(a))
    return sum(lax.dot_general(sb, p, dims, preferred_element_type=F32) for p in _split3(a))


def _iota2(shape, dim):
    return lax.broadcasted_iota(jnp.int32, shape, dim)


def _norm_proj(x, g1, w_in_p, *, tm):
    s = x.shape[0]

    def body(x_ref, g_ref, w_ref, h_ref, r_ref, qkv_ref, u_ref, fl_ref):
        xv = x_ref[...]
        r = lax.rsqrt(jnp.mean(xv * xv, axis=-1, keepdims=True) + EPS)
        h = (xv * r * g_ref[...]).astype(BF16)
        h_ref[...] = h
        r_ref[...] = r
        qkv_ref[...] = jnp.dot(h, w_ref[:, 0:QKV_W], preferred_element_type=F32).astype(BF16)
        u_ref[...] = jnp.dot(h, w_ref[:, U_OFF:F_OFF], preferred_element_type=F32)
        fl_ref[...] = jnp.dot(h, w_ref[:, F_OFF:IN_PAD], preferred_element_type=F32)

    row = lambda w: pl.BlockSpec((tm, w), lambda i: (i, 0))
    full = lambda a, b: pl.BlockSpec((a, b), lambda i: (0, 0))
    return pl.pallas_call(
        body,
        grid=(s // tm,),
        in_specs=[row(D_MODEL), full(1, D_MODEL), full(D_MODEL, IN_PAD)],
        out_specs=[row(D_MODEL), row(1), row(QKV_W), row(POOL_W), row(LANES)],
        out_shape=[SDS((s, D_MODEL), BF16), SDS((s, 1), F32), SDS((s, QKV_W), BF16), SDS((s, POOL_W), F32),
                   SDS((s, LANES), F32)],
        compiler_params=_cparams("arbitrary"),
        name="norm_proj",
    )(x, g1, w_in_p)


def _head_block_masks(rows, nb):
    shift = nb.bit_length() - 1
    rr, cc = _iota2((rows, rows), 0), _iota2((rows, rows), 1)
    same = lax.shift_right_logical(rr, shift) == lax.shift_right_logical(cc, shift)
    return rr, cc, same


def _forget_cumsum(fl_t, b_rows):
    rows = fl_t.shape[0]
    nb = rows // N_HEADS

    def body(fl_ref, b_ref, c_ref):
        z = fl_ref[...] + b_ref[...]
        lf = jnp.minimum(z, 0.0) - jnp.log1p(jnp.exp(-jnp.abs(z)))
        upper = _iota2((LANES, LANES), 0) <= _iota2((LANES, LANES), 1)
        within = _dot_sel(lf, upper)
        tot = _dot_sel(lf, jnp.ones((LANES, LANES), F32))
        rr, cc, same = _head_block_masks(rows, nb)
        c_ref[...] = within + _sel_dot(same & (cc < rr), tot)

    return pl.pallas_call(body, out_shape=SDS(fl_t.shape, F32), compiler_params=_cparams(), name="forget_cumsum")(
        fl_t, b_rows)


BIAS_LANES = 3


def _augment(t, h, col, col_first):
    n = t.shape[0]
    lane = _iota2((n, LANES), 1)
    own = (lane < HEAD_DIM) if h == 0 else (lane >= HEAD_DIM)
    b0 = HEAD_DIM if h == 0 else 0
    c0, o0 = (b0, b0 + BIAS_LANES) if col_first else (b0 + BIAS_LANES, b0)
    x = jnp.where(own, t, 0.0)
    for off, piece in enumerate(_split3(col)):
        x = jnp.where(lane == c0 + off, piece.astype(F32), x)
    x = jnp.where((lane >= o0) & (lane < o0 + BIAS_LANES), 1.0, x)
    return x.astype(BF16)


def _attn_fwd(qkv, c_col, *, tq):
    s = qkv.shape[0]
    tk = tq
    nb = s // tq

    def body(q_ref, k_ref, v_ref, cq_ref, ck_ref, o_ref, lse_ref, kp_ref, vt_ref, st_ref):
        i = pl.program_id(1)

        @pl.when(i == 0)
        def _():
            def prep(jb, _):
                st = pl.multiple_of(jb * tk, tk)
                k2 = k_ref[pl.ds(st, tk), :].astype(F32)
                ck = ck_ref[pl.ds(st, tk), :]
                for h in range(2):
                    kp_ref[h * nb + jb] = _augment(k2, h, -ck[:, h:h + 1], True)
                vt_ref[jb] = v_ref[pl.ds(st, tk), :].astype(F32).T.astype(BF16)
                return 0

            lax.fori_loop(0, nb, prep, 0)

        qs = q_ref[...].astype(F32) * 0.125
        cq = cq_ref[...]
        qp = [_augment(qs, h, cq[:, h:h + 1], False) for h in range(2)]

        def logits(j):
            return tuple(lax.dot_general(kp_ref[h * nb + j], qp[h], NT, preferred_element_type=F32) for h in range(2))

        def softmax_pv(j, sts, stats, masked):
            out = []
            for h in range(2):
                m, l, acc = stats[h]
                st = sts[h]
                if masked:
                    st = jnp.where(_iota2((tk, tq), 0) <= _iota2((tk, tq), 1), st, NEG)
                m_new = jnp.maximum(m, jnp.max(st, axis=0, keepdims=True))
                alpha = jnp.exp(m - m_new)
                p = jnp.exp(st - m_new)
                l = alpha * l + jnp.sum(p, axis=0, keepdims=True)
                vt = vt_ref[j, h * HEAD_DIM:(h + 1) * HEAD_DIM, :]
                acc = alpha * acc + jnp.dot(vt, p.astype(BF16), preferred_element_type=F32)
                out.append((m_new, l, acc))
            return tuple(out)

        def put(slot, sts):
            for h in range(2):
                st_ref[2 * slot + h] = sts[h]

        def get(slot):
            return tuple(st_ref[2 * slot + h] for h in range(2))

        def step(j, stats):
            nxt = logits(j + 1)
            stats = softmax_pv(j, get(j % 2), stats, False)
            put((j + 1) % 2, nxt)
            return stats

        init = tuple((jnp.full((1, tq), NEG, F32), jnp.zeros((1, tq), F32), jnp.zeros((HEAD_DIM, tq), F32))
                     for _ in range(2))
        put(0, logits(0))
        stats = lax.fori_loop(0, i, step, init)
        (ma, la, acca), (mb, lb, accb) = softmax_pv(i, get(i % 2), stats, True)
        o_ref[...] = jnp.concatenate([acca / la, accb / lb], axis=0).T.astype(BF16)
        lse_ref[...] = jnp.where(_iota2((2, tq), 0) == 0, ma + jnp.log(la), mb + jnp.log(lb))

    return pl.pallas_call(
        body,
        grid=(N_PAIRS, nb),
        in_specs=[
            pl.BlockSpec((tq, LANES), lambda p, i: (i, p)),
            pl.BlockSpec((s, LANES), lambda p, i: (0, N_PAIRS + p)),
            pl.BlockSpec((s, LANES), lambda p, i: (0, 2 * N_PAIRS + p)),
            pl.BlockSpec((None, tq, 2), lambda p, i: (p, i, 0)),
            pl.BlockSpec((None, s, 2), lambda p, i: (p, 0, 0)),
        ],
        out_specs=[
            pl.BlockSpec((tq, LANES), lambda p, i: (i, p)),
            pl.BlockSpec((None, None, 2, tq), lambda p, i: (p, i, 0, 0)),
        ],
        out_shape=[SDS((s, ATTN_W), BF16), SDS((N_PAIRS, nb, 2, tq), F32)],
        scratch_shapes=[pltpu.VMEM((2 * nb, tk, LANES), BF16), pltpu.VMEM((nb, LANES, tk), BF16),
                        pltpu.VMEM((4, tk, tq), F32)],
        compiler_params=_cparams("arbitrary", "arbitrary"),
        name="attn_fwd",
    )(qkv, qkv, qkv, c_col, c_col)


def _pool_counts(row0, tm, w):
    t = row0 + _iota2((tm, 1), 0)
    return jnp.minimum(t + 1, w).astype(F32)


def _pool_fwd(u, w_pool, pool_scale, *, tm):
    s = u.shape[0]

    def body(u_ref, w_ref, sc_ref, pooled_ref, po_ref, tail_ref):
        i = pl.program_id(0)

        @pl.when(i == 0)
        def _():
            tail_ref[...] = jnp.zeros_like(tail_ref)

        uv = u_ref[...]
        ext = jnp.concatenate([tail_ref[...], uv], axis=0)
        tail_ref[...] = uv[tm - HALO:, :]
        for g, w in enumerate(POOL_WINDOWS):
            cols = slice(g * POOL_G, (g + 1) * POOL_G)
            acc = ext[:, cols]
            k = 1
            while k < w:
                acc = acc + pltpu.roll(acc, k, axis=0)
                k *= 2
            pooled = (acc[HALO:, :] / _pool_counts(i * tm, tm, w) - uv[:, cols]).astype(BF16)
            pooled_ref[:, cols] = pooled
            mixed = jnp.dot(pooled, w_ref[g].astype(BF16), preferred_element_type=F32)
            po_ref[:, cols] = (mixed * sc_ref[:, cols]).astype(BF16)

    row = pl.BlockSpec((tm, POOL_W), lambda i: (i, 0))
    return pl.pallas_call(
        body,
        grid=(s // tm,),
        in_specs=[row, pl.BlockSpec((len(POOL_WINDOWS), POOL_G, POOL_G), lambda i: (0, 0, 0)),
                  pl.BlockSpec((1, POOL_W), lambda i: (0, 0))],
        out_specs=[row, row],
        out_shape=[SDS((s, POOL_W), BF16), SDS((s, POOL_W), BF16)],
        scratch_shapes=[pltpu.VMEM((HALO, POOL_W), F32)],
        compiler_params=_cparams("arbitrary"),
        name="pool_fwd",
    )(u, w_pool, pool_scale)


def _out_norm2(attn_o, pool_o, w_out, x, g2, *, tm):
    s = x.shape[0]

    def body(a_ref, p_ref, w_ref, x_ref, g_ref, x1_ref, h2_ref, r_ref):
        x1 = (x_ref[...] + jnp.dot(a_ref[...], w_ref[0:ATTN_W, :], preferred_element_type=F32)
              + jnp.dot(p_ref[...], w_ref[ATTN_W:, :], preferred_element_type=F32))
        r = lax.rsqrt(jnp.mean(x1 * x1, axis=-1, keepdims=True) + EPS)
        x1_ref[...] = x1
        r_ref[...] = r
        h2_ref[...] = (x1 * r * g_ref[...]).astype(BF16)

    row = lambda w: pl.BlockSpec((tm, w), lambda i: (i, 0))
    full = lambda a, b: pl.BlockSpec((a, b), lambda i: (0, 0))
    return pl.pallas_call(
        body,
        grid=(s // tm,),
        in_specs=[row(ATTN_W), row(POOL_W), full(D_MODEL, D_MODEL), row(D_MODEL), full(1, D_MODEL)],
        out_specs=[row(D_MODEL), row(D_MODEL), row(1)],
        out_shape=[SDS((s, D_MODEL), F32), SDS((s, D_MODEL), BF16), SDS((s, 1), F32)],
        compiler_params=_cparams("arbitrary"),
        name="out_norm2",
    )(attn_o, pool_o, w_out, x, g2)


def _gate_up(h2, wg, wu, *, tm, tn):
    s = h2.shape[0]

    def body(h_ref, wg_ref, wu_ref, gate_ref, up_ref, act_ref):
        h = h_ref[...]
        gate = jnp.dot(h, wg_ref[...], preferred_element_type=F32)
        up = jnp.dot(h, wu_ref[...], preferred_element_type=F32)
        gate_ref[...] = gate
        up_ref[...] = up
        act_ref[...] = (gate * jax.nn.sigmoid(gate) * up).astype(BF16)

    wspec = pl.BlockSpec((D_MODEL, tn), lambda c, r: (0, c))
    ospec = pl.BlockSpec((tm, tn), lambda c, r: (r, c))
    return pl.pallas_call(
        body,
        grid=(D_FF // tn, s // tm),
        in_specs=[pl.BlockSpec((tm, D_MODEL), lambda c, r: (r, 0)), wspec, wspec],
        out_specs=[ospec, ospec, ospec],
        out_shape=[SDS((s, D_FF), F32), SDS((s, D_FF), F32), SDS((s, D_FF), BF16)],
        compiler_params=_cparams("arbitrary", "arbitrary"),
        name="gate_up",
    )(h2, wg, wu)


def _down_final(act, wd, x1, gf, tgt, *, tm):
    s = x1.shape[0]

    def body(a_ref, w_ref, x1_ref, g_ref, t_ref, dx2_ref, loss_ref, dgf_ref):
        @pl.when(pl.program_id(0) == 0)
        def _():
            loss_ref[...] = jnp.zeros_like(loss_ref)
            dgf_ref[...] = jnp.zeros_like(dgf_ref)

        x2 = x1_ref[...] + jnp.dot(a_ref[...], w_ref[...], preferred_element_type=F32)
        r = lax.rsqrt(jnp.mean(x2 * x2, axis=-1, keepdims=True) + EPS)
        xn = x2 * r
        g = g_ref[...]
        diff = xn * g - t_ref[...]
        loss_ref[...] += jnp.sum(diff * diff, axis=0, keepdims=True)
        dy = diff * (1.0 / D_MODEL)
        dgf_ref[...] += jnp.sum(dy * xn, axis=0, keepdims=True)
        dxn = dy * g
        dx2_ref[...] = r * (dxn - xn * jnp.mean(dxn * xn, axis=-1, keepdims=True))

    row = lambda w: pl.BlockSpec((tm, w), lambda i: (i, 0))
    full = lambda a, b: pl.BlockSpec((a, b), lambda i: (0, 0))
    return pl.pallas_call(
        body,
        grid=(s // tm,),
        in_specs=[row(D_FF), full(D_FF, D_MODEL), row(D_MODEL), full(1, D_MODEL), row(D_MODEL)],
        out_specs=[row(D_MODEL), full(1, D_MODEL), full(1, D_MODEL)],
        out_shape=[SDS((s, D_MODEL), F32), SDS((1, D_MODEL), F32), SDS((1, D_MODEL), F32)],
        compiler_params=_cparams("arbitrary"),
        name="down_final",
    )(act, wd, x1, gf, tgt)


def _swiglu_bwd(dx2, wd, gate, up, *, tm, tn):
    s = dx2.shape[0]

    def body(d_ref, w_ref, gate_ref, up_ref, dgate_ref, dup_ref):
        dact = lax.dot_general(d_ref[...].astype(BF16), w_ref[...], NT, preferred_element_type=F32)
        gate = gate_ref[...]
        sg = jax.nn.sigmoid(gate)
        dup_ref[...] = (dact * (gate * sg)).astype(BF16)
        dgate_ref[...] = (dact * up_ref[...] * (sg * (1.0 + gate * (1.0 - sg)))).astype(BF16)

    ospec = pl.BlockSpec((tm, tn), lambda c, r: (r, c))
    return pl.pallas_call(
        body,
        grid=(D_FF // tn, s // tm),
        in_specs=[pl.BlockSpec((tm, D_MODEL), lambda c, r: (r, 0)), pl.BlockSpec((tn, D_MODEL), lambda c, r: (c, 0)),
                  ospec, ospec],
        out_specs=[ospec, ospec],
        out_shape=[SDS((s, D_FF), BF16), SDS((s, D_FF), BF16)],
        compiler_params=_cparams("arbitrary", "arbitrary"),
        name="swiglu_bwd",
    )(dx2, wd, gate, up)


def _mm_tn(a, bs, *, ta, ts, name):
    s, ka = a.shape
    n = len(bs)

    def body(a_ref, *refs):
        b_refs, o_refs = refs[:n], refs[n:]

        @pl.when(pl.program_id(1) == 0)
        def _():
            for o_ref in o_refs:
                o_ref[...] = jnp.zeros_like(o_ref)

        av = a_ref[...].astype(BF16)
        for b_ref, o_ref in zip(b_refs, o_refs):
            o_ref[...] += lax.dot_general(av, b_ref[...].astype(BF16), TN, preferred_element_type=F32)

    return pl.pallas_call(
        body,
        grid=(ka // ta, s // ts),
        in_specs=[pl.BlockSpec((ts, ta), lambda i, k: (k, i))]
        + [pl.BlockSpec((ts, b.shape[1]), lambda i, k: (k, 0)) for b in bs],
        out_specs=[pl.BlockSpec((ta, b.shape[1]), lambda i, k: (i, 0)) for b in bs],
        out_shape=[SDS((ka, b.shape[1]), F32) for b in bs],
        compiler_params=_cparams("arbitrary", "arbitrary"),
        name=name,
    )(a, *bs)


def _norm_bwd(dh, x, r, g, dres):
    xn = x * r
    dxn = dh * g
    dx = dres + r * (dxn - xn * jnp.mean(dxn * xn, axis=-1, keepdims=True))
    return dx, jnp.sum(dh * xn, axis=0, keepdims=True)


def _mlp_in_bwd(dgate, dup, wg, wu, w_out, x1, r2, g2, dx2, *, tm):
    s = x1.shape[0]

    def body(dg_ref, du_ref, wg_ref, wu_ref, wo_ref, x_ref, r_ref, g_ref, d_ref, dx1_ref, dmix_ref, dg2_ref):
        @pl.when(pl.program_id(0) == 0)
        def _():
            dg2_ref[...] = jnp.zeros_like(dg2_ref)

        dh2 = (lax.dot_general(dg_ref[...], wg_ref[...], NT, preferred_element_type=F32)
               + lax.dot_general(du_ref[...], wu_ref[...], NT, preferred_element_type=F32))
        dx1, dg2 = _norm_bwd(dh2, x_ref[...], r_ref[...], g_ref[...], d_ref[...])
        dg2_ref[...] += dg2
        dx1_ref[...] = dx1
        dmix_ref[...] = lax.dot_general(dx1.astype(BF16), wo_ref[...], NT, preferred_element_type=F32)

    row = lambda w: pl.BlockSpec((tm, w), lambda i: (i, 0))
    full = lambda a, b: pl.BlockSpec((a, b), lambda i: (0, 0))
    return pl.pallas_call(
        body,
        grid=(s // tm,),
        in_specs=[row(D_FF), row(D_FF), full(D_MODEL, D_FF), full(D_MODEL, D_FF), full(D_MODEL, D_MODEL),
                  row(D_MODEL), row(1), full(1, D_MODEL), row(D_MODEL)],
        out_specs=[row(D_MODEL), row(D_MODEL), full(1, D_MODEL)],
        out_shape=[SDS((s, D_MODEL), F32), SDS((s, D_MODEL), F32), SDS((1, D_MODEL), F32)],
        compiler_params=_cparams("arbitrary"),
        name="mlp_in_bwd",
    )(dgate, dup, wg, wu, w_out, x1, r2, g2, dx2)


def _pool_bwd(dmixed, pooled, w_pool, pool_scale, *, tm):
    s = pooled.shape[0]
    nt = s // tm
    ng = len(POOL_WINDOWS)

    def body(d_ref, p_ref, w_ref, sc_ref, du_ref, dw_ref, dsc_ref, head_ref):
        i = pl.program_id(0)

        @pl.when(i == 0)
        def _():
            head_ref[...] = jnp.zeros_like(head_ref)
            dw_ref[...] = jnp.zeros_like(dw_ref)
            dsc_ref[...] = jnp.zeros_like(dsc_ref)

        row0 = (nt - 1 - i) * tm
        for g, w in enumerate(POOL_WINDOWS):
            cols = slice(g * POOL_G, (g + 1) * POOL_G)
            wb = w_ref[g].astype(BF16)
            pooled_g = p_ref[:, cols]
            dpo = d_ref[:, cols]
            mixed = jnp.dot(pooled_g, wb, preferred_element_type=F32)
            dsc_ref[:, cols] += jnp.sum(dpo * mixed, axis=0, keepdims=True)
            dmp = (dpo * sc_ref[:, cols]).astype(BF16)
            dw_ref[g] += lax.dot_general(pooled_g, dmp, TN, preferred_element_type=F32)
            dpooled = lax.dot_general(dmp, wb, NT, preferred_element_type=F32)
            a = dpooled / _pool_counts(row0, tm, w)
            acc = jnp.concatenate([a, head_ref[:, cols]], axis=0)
            head_ref[:, cols] = a[0:HALO, :]
            k = 1
            while k < w:
                acc = acc + pltpu.roll(acc, tm + HALO - k, axis=0)
                k *= 2
            du_ref[:, cols] = (acc[0:tm, :] - dpooled).astype(BF16)

    rev = lambda i: (nt - 1 - i, 0)
    return pl.pallas_call(
        body,
        grid=(nt,),
        in_specs=[pl.BlockSpec((tm, POOL_W), lambda i: (nt - 1 - i, 1)), pl.BlockSpec((tm, POOL_W), rev),
                  pl.BlockSpec((ng, POOL_G, POOL_G), lambda i: (0, 0, 0)), pl.BlockSpec((1, POOL_W), lambda i: (0, 0))],
        out_specs=[pl.BlockSpec((tm, POOL_W), rev), pl.BlockSpec((ng, POOL_G, POOL_G), lambda i: (0, 0, 0)),
                   pl.BlockSpec((1, POOL_W), lambda i: (0, 0))],
        out_shape=[SDS((s, POOL_W), BF16), SDS((ng, POOL_G, POOL_G), F32), SDS((1, POOL_W), F32)],
        scratch_shapes=[pltpu.VMEM((HALO, POOL_W), F32)],
        compiler_params=_cparams("arbitrary"),
        name="pool_bwd",
    )(dmixed, pooled, w_pool, pool_scale)


def _attn_bwd(qkv, attn_o, dmixed, rowb, ck_col, *, tq):
    s = qkv.shape[0]
    tk = tq
    nb = s // tq

    def body(q_ref, k_ref, v_ref, o_ref, do_ref, rowb_ref, ck_ref, dq_ref, dk_ref, dv_ref, dck_ref, dcq_ref, dq_acc,
             delta_ref, kp_ref, qp_ref, dob_ref, front_ref):
        lane = _iota2((tq, LANES), 1)
        lo = lane < HEAD_DIM
        first = _iota2((8, LANES), 1) < HEAD_DIM
        sel = jnp.where(_iota2((8, LANES), 0) < 4, jnp.where(first, 1.0, 0.0), jnp.where(first, 0.0, 1.0))
        row8 = _iota2((8, tq), 0)

        def prep(b, _):
            st = pl.multiple_of(b * tq, tq)
            do2 = do_ref[pl.ds(st, tq), :]
            delta_ref[b] = _sel_dot(sel, do2 * o_ref[pl.ds(st, tq), :].astype(F32), NT)
            dob_ref[pl.ds(st, tq), :] = do2.astype(BF16)
            dq_acc[pl.ds(st, tq), :] = jnp.zeros((tq, LANES), F32)
            dcq_ref[b] = jnp.zeros((8, tq), F32)
            k2 = k_ref[pl.ds(st, tq), :].astype(F32)
            qs = q_ref[pl.ds(st, tq), :].astype(F32) * 0.125
            ck = ck_ref[pl.ds(st, tq), :]
            for h in range(2):
                kp_ref[h * nb + b] = _augment(k2, h, -ck[:, h:h + 1], True)
                qp_ref[h * nb + b] = _augment(qs, h, jnp.zeros((tq, 1), F32), False)
            return 0

        lax.fori_loop(0, nb, prep, 0)

        def split(t):
            z = jnp.zeros_like(t)
            return jnp.where(lo, t, z), jnp.where(lo, z, t)

        def kv_block(j, _):
            st_j = pl.multiple_of(j * tk, tk)
            ks = split(k_ref[pl.ds(st_j, tk), :])
            vs = split(v_ref[pl.ds(st_j, tk), :])
            kcat = jnp.concatenate(ks, axis=0)

            def stage(i, slot):
                ic = jnp.minimum(i, nb - 1)
                do2 = dob_ref[pl.ds(pl.multiple_of(ic * tq, tq), tq), :]
                for h in range(2):
                    front_ref[4 * slot + h] = lax.dot_general(kp_ref[h * nb + j], qp_ref[h * nb + ic], NT,
                                                              preferred_element_type=F32)
                    front_ref[4 * slot + 2 + h] = lax.dot_general(vs[h], do2, NT, preferred_element_type=F32)

            def q_block(i, slot, carry, diagonal):
                dk_acc, dv_acc, dca, dcb = carry
                ic = jnp.minimum(i, nb - 1)
                st_i = pl.multiple_of(ic * tq, tq)
                q2 = q_ref[pl.ds(st_i, tq), :]
                do2 = dob_ref[pl.ds(st_i, tq), :]
                rb = rowb_ref[ic] + jnp.where(i < nb, 0.0, NEG)
                dl = delta_ref[ic]
                pts, dsts = [], []
                for h in range(2):
                    st = front_ref[4 * slot + h] + rb[h:h + 1, :]
                    if diagonal:
                        st = jnp.where(_iota2((tk, tq), 0) <= _iota2((tk, tq), 1), st, NEG)
                    pt = jnp.exp(st)
                    pts.append(pt.astype(BF16))
                    dsts.append(pt * (front_ref[4 * slot + 2 + h] - dl[4 * h:4 * h + 1, :]))
                dca = dca + jnp.sum(dsts[0], axis=1, keepdims=True)
                dcb = dcb + jnp.sum(dsts[1], axis=1, keepdims=True)
                dcq_ref[ic] += jnp.where(row8 == 0, jnp.sum(dsts[0], axis=0, keepdims=True),
                                         jnp.where(row8 == 4, jnp.sum(dsts[1], axis=0, keepdims=True), 0.0))
                dsb = [d.astype(BF16) for d in dsts]
                dv_acc = dv_acc + jnp.dot(jnp.concatenate(pts, axis=1), jnp.concatenate(split(do2), axis=0),
                                          preferred_element_type=F32)
                dk_acc = dk_acc + jnp.dot(jnp.concatenate(dsb, axis=1), jnp.concatenate(split(q2), axis=0),
                                          preferred_element_type=F32)
                dq_acc[pl.ds(st_i, tq), :] += lax.dot_general(jnp.concatenate(dsb, axis=0), kcat, TN,
                                                              preferred_element_type=F32)
                return dk_acc, dv_acc, dca, dcb

            def pair(t, carry):
                i0 = j + 1 + 2 * t
                stage(i0 + 1, 0)
                carry = q_block(i0, 1, carry, False)
                stage(i0 + 2, 1)
                return q_block(i0 + 1, 0, carry, False)

            zt = jnp.zeros((tk, LANES), F32)
            zc = jnp.zeros((tk, 1), F32)
            stage(j, 0)
            stage(j + 1, 1)
            carry = q_block(j, 0, (zt, zt, zc, zc), True)
            dk_acc, dv_acc, dca, dcb = lax.fori_loop(0, lax.shift_right_logical(nb - j, 1), pair, carry)
            dk_ref[pl.ds(st_j, tk), :] = (dk_acc * 0.125).astype(BF16)
            dv_ref[pl.ds(st_j, tk), :] = dv_acc.astype(BF16)
            dck_ref[pl.ds(st_j, tk), :] = jnp.where(_iota2((tk, 2), 1) == 0, dca, dcb)
            return 0

        lax.fori_loop(0, nb, kv_block, 0)
        dq_ref[...] = (dq_acc[...] * 0.125).astype(BF16)

    col = lambda off: pl.BlockSpec((s, LANES), lambda p: (0, off + p))
    return pl.pallas_call(
        body,
        grid=(N_PAIRS,),
        in_specs=[col(0), col(N_PAIRS), col(2 * N_PAIRS), col(0), col(0),
                  pl.BlockSpec((None, nb, 2, tq), lambda p: (p, 0, 0, 0)),
                  pl.BlockSpec((None, s, 2), lambda p: (p, 0, 0))],
        out_specs=[col(0), col(0), col(0), pl.BlockSpec((None, s, 2), lambda p: (p, 0, 0)),
                   pl.BlockSpec((None, nb, 8, tq), lambda p: (p, 0, 0, 0))],
        out_shape=[SDS((s, ATTN_W), BF16), SDS((s, ATTN_W), BF16), SDS((s, ATTN_W), BF16), SDS((N_PAIRS, s, 2), F32),
                   SDS((N_PAIRS, nb, 8, tq), F32)],
        scratch_shapes=[pltpu.VMEM((s, LANES), F32), pltpu.VMEM((nb, 8, tq), F32),
                        pltpu.VMEM((2 * nb, tk, LANES), BF16), pltpu.VMEM((2 * nb, tq, LANES), BF16),
                        pltpu.VMEM((s, LANES), BF16), pltpu.VMEM((8, tk, tq), F32)],
        compiler_params=_cparams("arbitrary"),
        name="attn_bwd",
    )(qkv, qkv, qkv, attn_o, dmixed, rowb, ck_col)


def _forget_bwd(dc_t, fl_t, b_rows):
    rows = fl_t.shape[0]
    nb = rows // N_HEADS

    def body(dc_ref, fl_ref, b_ref, dfl_ref, db_ref):
        dc = dc_ref[...]
        lower = _iota2((LANES, LANES), 0) >= _iota2((LANES, LANES), 1)
        ones = jnp.ones((LANES, LANES), F32)
        rr, cc, same = _head_block_masks(rows, nb)
        dlf = _dot_sel(dc, lower) + _sel_dot(same & (cc > rr), _dot_sel(dc, ones))
        dfl = dlf / (1.0 + jnp.exp(fl_ref[...] + b_ref[...]))
        dfl_ref[...] = dfl
        shift = nb.bit_length() - 1
        hsel = lax.shift_right_logical(_iota2((N_HEADS, rows), 1), shift) == _iota2((N_HEADS, rows), 0)
        db_ref[...] = _sel_dot(hsel, _dot_sel(dfl, ones))

    return pl.pallas_call(body, out_shape=[SDS(fl_t.shape, F32), SDS((N_HEADS, LANES), F32)],
                          compiler_params=_cparams(), name="forget_bwd")(dc_t, fl_t, b_rows)


def _in_bwd(dq, dk, dv, du, dfl, w_in_p, x, r1, g1, dx1, *, tm):
    s = x.shape[0]
    pieces = ((0, ATTN_W), (ATTN_W, 2 * ATTN_W), (2 * ATTN_W, QKV_W), (U_OFF, F_OFF), (F_OFF, IN_PAD))

    def body(dq_ref, dk_ref, dv_ref, du_ref, df_ref, w_ref, x_ref, r_ref, g_ref, d_ref, dx_ref, dg1_ref):
        @pl.when(pl.program_id(0) == 0)
        def _():
            dg1_ref[...] = jnp.zeros_like(dg1_ref)

        dh = None
        for ref, (c0, c1) in zip((dq_ref, dk_ref, dv_ref, du_ref, df_ref), pieces):
            t = lax.dot_general(ref[...], w_ref[:, c0:c1], NT, preferred_element_type=F32)
            dh = t if dh is None else dh + t
        dx, dg1 = _norm_bwd(dh, x_ref[...], r_ref[...], g_ref[...], d_ref[...])
        dx_ref[...] = dx
        dg1_ref[...] += dg1

    row = lambda w: pl.BlockSpec((tm, w), lambda i: (i, 0))
    full = lambda a, b: pl.BlockSpec((a, b), lambda i: (0, 0))
    return pl.pallas_call(
        body,
        grid=(s // tm,),
        in_specs=[row(ATTN_W), row(ATTN_W), row(ATTN_W), row(POOL_W), row(LANES), full(D_MODEL, IN_PAD),
                  row(D_MODEL), row(1), full(1, D_MODEL), row(D_MODEL)],
        out_specs=[row(D_MODEL), full(1, D_MODEL)],
        out_shape=[SDS((s, D_MODEL), F32), SDS((1, D_MODEL), F32)],
        compiler_params=_cparams("arbitrary"),
        name="in_bwd",
    )(dq, dk, dv, du, dfl, w_in_p, x, r1, g1, dx1)


def _tiles(s):
    big = min(512, s)
    return dict(row=big, attn=min(256, s // 2), mlp_bwd=min(256, s))


def _tie(a, token):
    return a + token[0:1, 0:1].astype(a.dtype)


def _local_step(x, tgt, p, weight, emit):
    s = x.shape[0]
    t = _tiles(s)
    tm, tq = t["row"], t["attn"]
    nb = s // LANES
    nqb = s // tq
    g1, g2, gf = p["norm1_g"], p["norm2_g"], p["final_g"].reshape(1, D_MODEL)
    w_pool, pool_scale = p["w_pool"][0], p["pool_scale"]

    w_in_p = weight("w_in", x)
    h, r1, qkv, u, fl = _norm_proj(x, g1, w_in_p, tm=tm)
    fl_t = fl[:, :N_HEADS].T.reshape(N_HEADS * nb, LANES)
    b_rows = jnp.repeat(p["b_forget"].reshape(N_HEADS), nb).reshape(N_HEADS * nb, 1)
    c = _forget_cumsum(fl_t, b_rows).reshape(N_PAIRS, 2, s)
    c_col = c.transpose(0, 2, 1)
    c_rowblk = c.reshape(N_PAIRS, 2, nqb, tq).transpose(0, 2, 1, 3)
    attn_o, lse = _attn_fwd(qkv, c_col, tq=tq)
    pooled, pool_o = _pool_fwd(u, w_pool, pool_scale, tm=tm)
    w_out = weight("w_out", attn_o)
    x1, h2, r2 = _out_norm2(attn_o, pool_o, w_out, x, g2, tm=tm)
    wg, wu = weight("w_gate_up", h2)
    gate, up, act = _gate_up(h2, wg, wu, tm=tm, tn=D_FF // 2)
    wd = weight("w_down", act)
    dx2, loss_row, d_gf = _down_final(act, wd, x1, gf, tgt, tm=tm)

    dgate, dup = _swiglu_bwd(dx2, wd, gate, up, tm=tm, tn=D_FF // 2)
    (d_wd,) = _mm_tn(act, [dx2], ta=D_FF // 2, ts=tm, name="grad_w_down")
    token = emit("w_down", d_wd)
    token = token + emit("w_gate_up", _mm_tn(h2, [dgate, dup], ta=tm, ts=tm, name="grad_w_gate_up"))
    dx1, dmixed, d_g2 = _mlp_in_bwd(dgate, dup, wg, wu, w_out, x1, r2, _tie(g2, token), dx2, tm=t["mlp_bwd"])
    du, d_wpool, d_pscale = _pool_bwd(dmixed, pooled, w_pool, pool_scale, tm=tm)
    (d_wo_a,) = _mm_tn(attn_o, [dx1], ta=ATTN_W, ts=tm, name="grad_w_out_attn")
    (d_wo_p,) = _mm_tn(pool_o, [dx1], ta=POOL_W, ts=tm, name="grad_w_out_pool")
    token = emit("w_out", jnp.concatenate([d_wo_a, d_wo_p], axis=0))
    rowb = _tie(c_rowblk - lse, token)
    dq, dk, dv, dck, dcq = _attn_bwd(qkv, attn_o, dmixed, rowb, c_col, tq=tq)
    dcq = dcq[:, :, 0::4, :].transpose(0, 2, 1, 3).reshape(N_PAIRS, 2, s)
    dc_t = (dcq - dck.transpose(0, 2, 1)).reshape(N_HEADS * nb, LANES)
    dfl_t, db = _forget_bwd(dc_t, fl_t, b_rows)
    dfl = jnp.pad(dfl_t.reshape(N_HEADS, s).T, ((0, 0), (0, LANES - N_HEADS))).astype(BF16)
    d_wq, d_wk, d_wv, d_wu_in, d_wf = _mm_tn(h, [dq, dk, dv, du, dfl], ta=D_MODEL, ts=tm, name="grad_w_in")
    token = emit("w_in", jnp.concatenate([d_wq, d_wk, d_wv, d_wf[:, :N_HEADS], d_wu_in], axis=1))
    dx, d_g1 = _in_bwd(dq, dk, dv, du, dfl, w_in_p, x, r1, _tie(g1, token), dx1, tm=tm)

    small = dict(norm1_g=d_g1, b_forget=db[:, 0].reshape(1, N_HEADS), w_pool=d_wpool, pool_scale=d_pscale,
                 norm2_g=d_g2, final_g=d_gf)
    return loss_row, dx, small


def _my_index():
    return 4 * lax.axis_index("x") + 2 * lax.axis_index("y") + lax.axis_index("c")


def _peer(k):
    pos = [lax.axis_index(a) for a in ("x", "y", "c")]
    flipped = tuple(1 - p if (k >> b) & 1 else p for p, b in zip(pos, (2, 1, 0)))
    return flipped, 4 * flipped[0] + 2 * flipped[1] + flipped[2]


_HBM = pl.BlockSpec(memory_space=pltpu.HBM)
_SEM = pl.BlockSpec(memory_space=pltpu.SEMAPHORE)
_DATAFLOW = pltpu.SideEffectType.DATAFLOW_SIDE_EFFECTING


def _peer_copies(ins, lands, send_sems, recv_sems, scatter, arrivals):
    me = _my_index()
    copies = []
    for w in range(len(ins)):
        for k in range(1, N_DEV):
            dev, idx = _peer(k)
            copies.append(pltpu.make_async_remote_copy(
                src_ref=ins[w].at[idx] if scatter[w] else ins[w], dst_ref=lands[w].at[idx if arrivals else me],
                send_sem=send_sems[w].at[k - 1], recv_sem=recv_sems[w].at[k - 1], device_id=dev, device_id_type=MESH))
    return copies


def _exchange_start(arrays, scatter, name):
    n = len(arrays)
    land_shapes = [(N_DEV,) + tuple(a.shape[1:] if sc else a.shape) for a, sc in zip(arrays, scatter)]

    def body(*refs):
        ins, lands = refs[:n], refs[n:2 * n]
        send_sems, recv_sems = refs[2 * n:3 * n], refs[3 * n:4 * n]
        token = refs[6 * n]
        for cp in _peer_copies(ins, lands, send_sems, recv_sems, scatter, False):
            cp.start()
        token[...] = jnp.zeros_like(token)

    sem = pltpu.SemaphoreType.DMA((N_DEV - 1,))
    outs = pl.pallas_call(
        body,
        in_specs=[_HBM] * (2 * n),
        out_specs=[_SEM] * (2 * n) + [_HBM] * (2 * n) + [pl.BlockSpec(memory_space=pltpu.VMEM)],
        out_shape=[sem] * (2 * n) + [pltpu.HBM(a.shape, a.dtype) for a in arrays]
        + [pltpu.HBM(sh, a.dtype) for sh, a in zip(land_shapes, arrays)] + [SDS((8, LANES), F32)],
        input_output_aliases={i: 2 * n + i for i in range(2 * n)},
        compiler_params=pltpu.CompilerParams(has_side_effects=_DATAFLOW),
        name=name,
    )(*[pltpu.with_memory_space_constraint(a, pltpu.HBM) for a in arrays],
      *[pltpu.with_memory_space_constraint(lax.empty(sh, a.dtype), pltpu.HBM) for sh, a in zip(land_shapes, arrays)])
    handles = [dict(send=outs[w], recv=outs[n + w], src=outs[2 * n + w], land=outs[3 * n + w], scatter=scatter[w])
               for w in range(n)]
    return handles, outs[4 * n]


def _exchange_wait(handles, after, name):
    n = len(handles)
    scatter = [h["scatter"] for h in handles]

    def body(*refs):
        ins, lands = refs[:n], refs[n:2 * n]
        send_sems, recv_sems = refs[2 * n:3 * n], refs[3 * n:4 * n]
        for cp in _peer_copies(ins, lands, send_sems, recv_sems, scatter, False):
            cp.wait_send()
        for cp in _peer_copies(ins, lands, send_sems, recv_sems, scatter, True):
            cp.wait_recv()

    srcs, lands = [h["src"] for h in handles], [h["land"] for h in handles]
    outs = pl.pallas_call(
        body,
        in_specs=[_HBM] * (2 * n) + [_SEM] * (2 * n) + [pl.BlockSpec(memory_space=pl.ANY)],
        out_specs=[_HBM] * (2 * n),
        out_shape=[pltpu.HBM(a.shape, a.dtype) for a in srcs + lands],
        input_output_aliases={i: i for i in range(2 * n)},
        compiler_params=pltpu.CompilerParams(has_side_effects=_DATAFLOW),
        name=name,
    )(*srcs, *lands, *[h["send"] for h in handles], *[h["recv"] for h in handles], after)
    me = _my_index()
    full = []
    for src, land, sc in zip(outs[:n], outs[n:], scatter):
        own = lax.dynamic_index_in_dim(src, me, 0, keepdims=True) if sc else src[None]
        full.append(lax.dynamic_update_slice(land, own, (me,) + (0,) * (land.ndim - 1)))
    return full


def _adamw(parts, w, m, v, name):
    rows, cols = w.shape
    tr = rows // 4 if rows % 32 == 0 else rows

    def body(p_ref, w_ref, m_ref, v_ref, g_ref, d_ref, mo_ref, vo_ref):
        g = p_ref[0].astype(F32)
        for d in range(1, N_DEV):
            g = g + p_ref[d].astype(F32)
        m_new = ADAM_B1 * m_ref[...] + (1.0 - ADAM_B1) * g
        v_new = ADAM_B2 * v_ref[...] + (1.0 - ADAM_B2) * (g * g)
        m_hat = m_new / (1.0 - ADAM_B1 ** ADAM_STEP)
        v_hat = v_new / (1.0 - ADAM_B2 ** ADAM_STEP)
        g_ref[...] = g
        d_ref[...] = -ADAM_LR * (m_hat / (jnp.sqrt(v_hat) + ADAM_EPS) + ADAM_WD * w_ref[...])
        mo_ref[...] = m_new
        vo_ref[...] = v_new

    blk = pl.BlockSpec((tr, cols), lambda i: (i, 0))
    return pl.pallas_call(
        body,
        grid=(rows // tr,),
        in_specs=[pl.BlockSpec((N_DEV, tr, cols), lambda i: (0, i, 0)), blk, blk, blk],
        out_specs=[blk] * 4,
        out_shape=[SDS((rows, cols), F32)] * 4,
        compiler_params=_cparams("arbitrary"),
        name=name,
    )(parts, w, m, v)


_SMALL = (("w_pool", 512), ("norm1_g", 8), ("norm2_g", 8), ("final_g", 8), ("pool_scale", 8), ("b_forget", 8),
          ("loss", 8))
_SMALL_ROWS = sum(r for _, r in _SMALL)


def _pack_small(vals):
    parts = []
    for name, rows in _SMALL:
        flat = vals[name].reshape(-1).astype(F32)
        parts.append(jnp.pad(flat, (0, rows * LANES - flat.shape[0])).reshape(rows, LANES))
    return jnp.concatenate(parts, axis=0)


def _unpack_small(packed, like):
    out, r0 = {}, 0
    for name, rows in _SMALL:
        n = like[name].size
        out[name] = packed[r0:r0 + rows].reshape(-1)[:n].reshape(like[name].shape)
        r0 += rows
    return out


def kernel(x, norm1_g, w_in, b_forget, w_pool, pool_scale, w_out, norm2_g, w_gate, w_up, w_down, final_g, loss_target, m_norm1_g, m_w_in, m_b_forget, m_w_pool, m_pool_scale, m_w_out, m_norm2_g, m_w_gate, m_w_up, m_w_down, m_final_g, v_norm1_g, v_w_in, v_b_forget, v_w_pool, v_pool_scale, v_w_out, v_norm2_g, v_w_gate, v_w_up, v_w_down, v_final_g):
    big = ("w_in", "w_out", "w_gate", "w_up", "w_down")
    order = ("norm1_g", "w_in", "b_forget", "w_pool", "pool_scale", "w_out", "norm2_g", "w_gate", "w_up", "w_down",
             "final_g")
    w = dict(norm1_g=norm1_g, w_in=w_in, b_forget=b_forget, w_pool=w_pool, pool_scale=pool_scale, w_out=w_out,
             norm2_g=norm2_g, w_gate=w_gate, w_up=w_up, w_down=w_down, final_g=final_g)
    m = dict(norm1_g=m_norm1_g, w_in=m_w_in, b_forget=m_b_forget, w_pool=m_w_pool, pool_scale=m_pool_scale,
             w_out=m_w_out, norm2_g=m_norm2_g, w_gate=m_w_gate, w_up=m_w_up, w_down=m_w_down, final_g=m_final_g)
    v = dict(norm1_g=v_norm1_g, w_in=v_w_in, b_forget=v_b_forget, w_pool=v_w_pool, pool_scale=v_pool_scale,
             w_out=v_w_out, norm2_g=v_norm2_g, w_gate=v_w_gate, w_up=v_w_up, w_down=v_w_down, final_g=v_final_g)

    shards = [w[n][0].astype(BF16) for n in big]
    gather, _ = _exchange_start(shards, [False] * len(big), "gather_start")
    gather = dict(zip(big, gather))

    def gathered(names, after):
        return _exchange_wait([gather[n] for n in names], after, "gather_wait_" + names[0])

    def weight(name, after):
        if name == "w_in":
            full = gathered(["w_in"], after)[0].transpose(1, 0, 2).reshape(D_MODEL, IN_W)
            f0 = QKV_W + N_HEADS
            return jnp.concatenate([full[:, :QKV_W], full[:, f0:], full[:, QKV_W:f0],
                                    jnp.zeros((D_MODEL, IN_PAD - IN_W), BF16)], axis=1)
        if name == "w_out":
            return gathered(["w_out"], after)[0].reshape(D_MODEL, D_MODEL)
        if name == "w_gate_up":
            return [g.transpose(1, 0, 2).reshape(D_MODEL, D_FF) for g in gathered(["w_gate", "w_up"], after)]
        return gathered(["w_down"], after)[0].reshape(D_FF, D_MODEL)

    cols = lambda g: g.reshape(g.shape[0], N_DEV, g.shape[1] // N_DEV).transpose(1, 0, 2)
    rows = lambda g: g.reshape(N_DEV, g.shape[0] // N_DEV, g.shape[1])
    sent = {}

    def emit(name, grad):
        if name == "w_gate_up":
            names, slots = ["w_gate", "w_up"], [cols(g) for g in grad]
        else:
            names, slots = [name], [cols(grad).astype(BF16) if name == "w_in" else rows(grad)]
        handles, token = _exchange_start(slots, [True] * len(slots), "grads_start_" + name)
        sent.update(zip(names, handles))
        return token

    loss_row, dx, small_grads = _local_step(x[0], loss_target[0], w, weight, emit)

    packed = _pack_small(dict(small_grads, loss=0.5 / D_MODEL * jnp.sum(loss_row)))
    (small_handle,), _ = _exchange_start([packed], [False], "grads_start_replicated")

    outs = {}
    after = dx
    for name in ("w_down", "w_gate", "w_up", "w_out", "w_in"):
        (parts,) = _exchange_wait([sent[name]], after, "grads_wait_" + name)
        outs[name] = _adamw(parts, w[name][0], m[name][0], v[name][0], "adamw_" + name)
        after = outs[name][0]
        outs[name] = [a[None] for a in outs[name]]
    (parts,) = _exchange_wait([small_handle], after, "grads_wait_replicated")
    zero = dict(loss=jnp.zeros((), F32))
    small = _adamw(parts, _pack_small(dict(w, **zero)), _pack_small(dict(m, **zero)),
                   _pack_small(dict(v, **zero)), "adamw_replicated")
    small = [_unpack_small(p, dict(w, **zero)) for p in small]
    loss = small[0]["loss"]
    for name in order:
        if name not in outs:
            outs[name] = [p[name] for p in small]

    return (loss, dx[None]) + tuple(outs[n][k] for k in range(4) for n in order)
```

```python
import functools

import jax
import jax.numpy as jnp
from jax import lax
from jax.experimental import pallas as pl
from jax.experimental.pallas import tpu as pltpu

F32 = jnp.float32
BF16 = jnp.bfloat16
SDS = jax.ShapeDtypeStruct

D_MODEL = 1024
ATTN_W = 512
N_HEADS = 8
HEAD_DIM = 64
N_PAIRS = N_HEADS // 2
POOL_W = 512
POOL_WINDOWS = (2, 4, 8, 16)
POOL_G = 128
HALO = 16
IN_W = 3 * ATTN_W + N_HEADS + POOL_W
QKV_W = 3 * ATTN_W
U_OFF = QKV_W
F_OFF = QKV_W + POOL_W
IN_PAD = F_OFF + 128
D_FF = 2816
EPS = 1e-6
NEG = -1e30
N_DEV = 8
LANES = 128

ADAM_LR = 0.001
ADAM_B1 = 0.9
ADAM_B2 = 0.999
ADAM_EPS = 1e-08
ADAM_WD = 0.01
ADAM_STEP = 10

VMEM_LIMIT_BYTES = 56 * 1024 * 1024
MESH = pl.DeviceIdType.MESH
NT = (((1,), (1,)), ((), ()))
TN = (((0,), (0,)), ((), ()))


def _cparams(*sem):
    return pltpu.CompilerParams(dimension_semantics=sem or None, vmem_limit_bytes=VMEM_LIMIT_BYTES)


def _split3(a):
    hi = a.astype(BF16)
    r1 = a - hi.astype(F32)
    mid = r1.astype(BF16)
    lo = (r1 - mid.astype(F32)).astype(BF16)
    return hi, mid, lo


def _dot_sel(a, sel, dims=None):
    sb = sel.astype(BF16)
    if dims is None:
        return sum(jnp.dot(p, sb, preferred_element_type=F32) for p in _split3(a))
    return sum(lax.dot_general(p, sb, dims, preferred_element_type=F32) for p in _split3(a))


def _sel_dot(sel, a, dims=None):
    sb = sel.astype(BF16)
    if dims is None:
        return sum(jnp.dot(sb, p, preferred_element_type=F32) for p in _split3(a))
    return sum(lax.dot_general(sb, p, dims, preferred_element_type=F32) for p in _split3(a))


def _iota2(shape, dim):
    return lax.broadcasted_iota(jnp.int32, shape, dim)


def _norm_proj(x, g1, w_in_t, *, tm):
    s = x.shape[0]

    def body(x_ref, g_ref, w_ref, h_ref, r_ref, qkv_ref, u_ref, fl_ref):
        xv = x_ref[...]
        r = lax.rsqrt(jnp.mean(xv * xv, axis=-1, keepdims=True) + EPS)
        h = (xv * r * g_ref[...]).astype(BF16)
        h_ref[...] = h
        r_ref[...] = r
        qkv_ref[...] = lax.dot_general(h, w_ref[0:QKV_W, :], NT, preferred_element_type=F32).astype(BF16)
        u_ref[...] = lax.dot_general(h, w_ref[U_OFF:F_OFF, :], NT, preferred_element_type=F32)
        fl_ref[...] = lax.dot_general(h, w_ref[F_OFF:IN_PAD, :], NT, preferred_element_type=F32)

    row = lambda w: pl.BlockSpec((tm, w), lambda i: (i, 0))
    full = lambda a, b: pl.BlockSpec((a, b), lambda i: (0, 0))
    return pl.pallas_call(
        body,
        grid=(s // tm,),
        in_specs=[row(D_MODEL), full(1, D_MODEL), full(IN_PAD, D_MODEL)],
        out_specs=[row(D_MODEL), row(1), row(QKV_W), row(POOL_W), row(LANES)],
        out_shape=[SDS((s, D_MODEL), BF16), SDS((s, 1), F32), SDS((s, QKV_W), BF16), SDS((s, POOL_W), F32),
                   SDS((s, LANES), F32)],
        compiler_params=_cparams("arbitrary"),
        name="norm_proj",
    )(x, g1, w_in_t)


def _head_block_masks(rows, nb):
    shift = nb.bit_length() - 1
    rr, cc = _iota2((rows, rows), 0), _iota2((rows, rows), 1)
    same = lax.shift_right_logical(rr, shift) == lax.shift_right_logical(cc, shift)
    return rr, cc, same


def _forget_cumsum(fl_t, b_rows):
    rows = fl_t.shape[0]
    nb = rows // N_HEADS

    def body(fl_ref, b_ref, c_ref):
        z = fl_ref[...] + b_ref[...]
        lf = jnp.minimum(z, 0.0) - jnp.log1p(jnp.exp(-jnp.abs(z)))
        upper = _iota2((LANES, LANES), 0) <= _iota2((LANES, LANES), 1)
        within = _dot_sel(lf, upper)
        tot = _dot_sel(lf, jnp.ones((LANES, LANES), F32))
        rr, cc, same = _head_block_masks(rows, nb)
        c_ref[...] = within + _sel_dot(same & (cc < rr), tot)

    return pl.pallas_call(body, out_shape=SDS(fl_t.shape, F32), compiler_params=_cparams(), name="forget_cumsum")(
        fl_t, b_rows)


BIAS_LANES = 3


def _augment(t, h, col, col_first):
    n = t.shape[0]
    lane = _iota2((n, LANES), 1)
    own = (lane < HEAD_DIM) if h == 0 else (lane >= HEAD_DIM)
    b0 = HEAD_DIM if h == 0 else 0
    c0, o0 = (b0, b0 + BIAS_LANES) if col_first else (b0 + BIAS_LANES, b0)
    x = jnp.where(own, t, 0.0)
    for off, piece in enumerate(_split3(col)):
        x = jnp.where(lane == c0 + off, piece.astype(F32), x)
    x = jnp.where((lane >= o0) & (lane < o0 + BIAS_LANES), 1.0, x)
    return x.astype(BF16)


def _attn_fwd(qkv, c_col, *, tq):
    s = qkv.shape[0]
    tk = tq
    nb = s // tq

    def body(q_ref, k_ref, v_ref, cq_ref, ck_ref, o_ref, lse_ref, kp_ref, vt_ref, st_ref):
        i = pl.program_id(1)

        @pl.when(i == 0)
        def _():
            def prep(jb, _):
                st = pl.multiple_of(jb * tk, tk)
                k2 = k_ref[pl.ds(st, tk), :].astype(F32)
                ck = ck_ref[pl.ds(st, tk), :]
                for h in range(2):
                    kp_ref[h * nb + jb] = _augment(k2, h, -ck[:, h:h + 1], True)
                vt_ref[jb] = v_ref[pl.ds(st, tk), :].astype(F32).T.astype(BF16)
                return 0

            lax.fori_loop(0, nb, prep, 0)

        qs = q_ref[...].astype(F32) * 0.125
        cq = cq_ref[...]
        qp = [_augment(qs, h, cq[:, h:h + 1], False) for h in range(2)]

        def logits(j):
            return tuple(lax.dot_general(kp_ref[h * nb + j], qp[h], NT, preferred_element_type=F32) for h in range(2))

        def softmax_pv(j, sts, stats, masked):
            out = []
            for h in range(2):
                m, l, acc = stats[h]
                st = sts[h]
                if masked:
                    st = jnp.where(_iota2((tk, tq), 0) <= _iota2((tk, tq), 1), st, NEG)
                m_new = jnp.maximum(m, jnp.max(st, axis=0, keepdims=True))
                alpha = jnp.exp(m - m_new)
                p = jnp.exp(st - m_new)
                l = alpha * l + jnp.sum(p, axis=0, keepdims=True)
                vt = vt_ref[j, h * HEAD_DIM:(h + 1) * HEAD_DIM, :]
                acc = alpha * acc + jnp.dot(vt, p.astype(BF16), preferred_element_type=F32)
                out.append((m_new, l, acc))
            return tuple(out)

        def put(slot, sts):
            for h in range(2):
                st_ref[2 * slot + h] = sts[h]

        def get(slot):
            return tuple(st_ref[2 * slot + h] for h in range(2))

        def step(j, stats):
            nxt = logits(j + 1)
            stats = softmax_pv(j, get(j % 2), stats, False)
            put((j + 1) % 2, nxt)
            return stats

        init = tuple((jnp.full((1, tq), NEG, F32), jnp.zeros((1, tq), F32), jnp.zeros((HEAD_DIM, tq), F32))
                     for _ in range(2))
        put(0, logits(0))
        stats = lax.fori_loop(0, i, step, init)
        (ma, la, acca), (mb, lb, accb) = softmax_pv(i, get(i % 2), stats, True)
        o_ref[...] = jnp.concatenate([acca / la, accb / lb], axis=0).T.astype(BF16)
        lse_ref[...] = jnp.where(_iota2((2, tq), 0) == 0, ma + jnp.log(la), mb + jnp.log(lb))

    return pl.pallas_call(
        body,
        grid=(N_PAIRS, nb),
        in_specs=[
            pl.BlockSpec((tq, LANES), lambda p, i: (i, p)),
            pl.BlockSpec((s, LANES), lambda p, i: (0, N_PAIRS + p)),
            pl.BlockSpec((s, LANES), lambda p, i: (0, 2 * N_PAIRS + p)),
            pl.BlockSpec((None, tq, 2), lambda p, i: (p, i, 0)),
            pl.BlockSpec((None, s, 2), lambda p, i: (p, 0, 0)),
        ],
        out_specs=[
            pl.BlockSpec((tq, LANES), lambda p, i: (i, p)),
            pl.BlockSpec((None, None, 2, tq), lambda p, i: (p, i, 0, 0)),
        ],
        out_shape=[SDS((s, ATTN_W), BF16), SDS((N_PAIRS, nb, 2, tq), F32)],
        scratch_shapes=[pltpu.VMEM((2 * nb, tk, LANES), BF16), pltpu.VMEM((nb, LANES, tk), BF16),
                        pltpu.VMEM((4, tk, tq), F32)],
        compiler_params=_cparams("arbitrary", "arbitrary"),
        name="attn_fwd",
    )(qkv, qkv, qkv, c_col, c_col)


def _pool_counts(row0, tm, w):
    t = row0 + _iota2((tm, 1), 0)
    return jnp.minimum(t + 1, w).astype(F32)


def _pool_fwd(u, w_pool, pool_scale, *, tm):
    s = u.shape[0]

    def body(u_ref, w_ref, sc_ref, pooled_ref, po_ref, tail_ref):
        i = pl.program_id(0)

        @pl.when(i == 0)
        def _():
            tail_ref[...] = jnp.zeros_like(tail_ref)

        uv = u_ref[...]
        ext = jnp.concatenate([tail_ref[...], uv], axis=0)
        tail_ref[...] = uv[tm - HALO:, :]
        for g, w in enumerate(POOL_WINDOWS):
            cols = slice(g * POOL_G, (g + 1) * POOL_G)
            acc = ext[:, cols]
            k = 1
            while k < w:
                acc = acc + pltpu.roll(acc, k, axis=0)
                k *= 2
            pooled = (acc[HALO:, :] / _pool_counts(i * tm, tm, w) - uv[:, cols]).astype(BF16)
            pooled_ref[:, cols] = pooled
            mixed = jnp.dot(pooled, w_ref[g].astype(BF16), preferred_element_type=F32)
            po_ref[:, cols] = (mixed * sc_ref[:, cols]).astype(BF16)

    row = pl.BlockSpec((tm, POOL_W), lambda i: (i, 0))
    return pl.pallas_call(
        body,
        grid=(s // tm,),
        in_specs=[row, pl.BlockSpec((len(POOL_WINDOWS), POOL_G, POOL_G), lambda i: (0, 0, 0)),
                  pl.BlockSpec((1, POOL_W), lambda i: (0, 0))],
        out_specs=[row, row],
        out_shape=[SDS((s, POOL_W), BF16), SDS((s, POOL_W), BF16)],
        scratch_shapes=[pltpu.VMEM((HALO, POOL_W), F32)],
        compiler_params=_cparams("arbitrary"),
        name="pool_fwd",
    )(u, w_pool, pool_scale)


def _out_norm2(attn_o, pool_o, w_out, x, g2, *, tm):
    s = x.shape[0]

    def body(a_ref, p_ref, w_ref, x_ref, g_ref, x1_ref, h2_ref, r_ref):
        x1 = (x_ref[...] + jnp.dot(a_ref[...], w_ref[0:ATTN_W, :], preferred_element_type=F32)
              + jnp.dot(p_ref[...], w_ref[ATTN_W:, :], preferred_element_type=F32))
        r = lax.rsqrt(jnp.mean(x1 * x1, axis=-1, keepdims=True) + EPS)
        x1_ref[...] = x1
        r_ref[...] = r
        h2_ref[...] = (x1 * r * g_ref[...]).astype(BF16)

    row = lambda w: pl.BlockSpec((tm, w), lambda i: (i, 0))
    full = lambda a, b: pl.BlockSpec((a, b), lambda i: (0, 0))
    return pl.pallas_call(
        body,
        grid=(s // tm,),
        in_specs=[row(ATTN_W), row(POOL_W), full(D_MODEL, D_MODEL), row(D_MODEL), full(1, D_MODEL)],
        out_specs=[row(D_MODEL), row(D_MODEL), row(1)],
        out_shape=[SDS((s, D_MODEL), F32), SDS((s, D_MODEL), BF16), SDS((s, 1), F32)],
        compiler_params=_cparams("arbitrary"),
        name="out_norm2",
    )(attn_o, pool_o, w_out, x, g2)


def _gate_up(h2, wg_t, wu_t, *, tm, tn):
    s = h2.shape[0]

    def body(h_ref, wg_ref, wu_ref, gate_ref, up_ref, act_ref):
        h = h_ref[...]
        gate = lax.dot_general(h, wg_ref[...], NT, preferred_element_type=F32)
        up = lax.dot_general(h, wu_ref[...], NT, preferred_element_type=F32)
        gate_ref[...] = gate
        up_ref[...] = up
        act_ref[...] = (gate * jax.nn.sigmoid(gate) * up).astype(BF16)

    wspec = pl.BlockSpec((tn, D_MODEL), lambda c, r: (c, 0))
    ospec = pl.BlockSpec((tm, tn), lambda c, r: (r, c))
    return pl.pallas_call(
        body,
        grid=(D_FF // tn, s // tm),
        in_specs=[pl.BlockSpec((tm, D_MODEL), lambda c, r: (r, 0)), wspec, wspec],
        out_specs=[ospec, ospec, ospec],
        out_shape=[SDS((s, D_FF), F32), SDS((s, D_FF), F32), SDS((s, D_FF), BF16)],
        compiler_params=_cparams("arbitrary", "arbitrary"),
        name="gate_up",
    )(h2, wg_t, wu_t)


def _down_final(act, wd, x1, gf, tgt, *, tm):
    s = x1.shape[0]

    def body(a_ref, w_ref, x1_ref, g_ref, t_ref, dx2_ref, loss_ref, dgf_ref):
        @pl.when(pl.program_id(0) == 0)
        def _():
            loss_ref[...] = jnp.zeros_like(loss_ref)
            dgf_ref[...] = jnp.zeros_like(dgf_ref)

        x2 = x1_ref[...] + jnp.dot(a_ref[...], w_ref[...], preferred_element_type=F32)
        r = lax.rsqrt(jnp.mean(x2 * x2, axis=-1, keepdims=True) + EPS)
        xn = x2 * r
        g = g_ref[...]
        diff = xn * g - t_ref[...]
        loss_ref[...] += jnp.sum(diff * diff, axis=0, keepdims=True)
        dy = diff * (1.0 / D_MODEL)
        dgf_ref[...] += jnp.sum(dy * xn, axis=0, keepdims=True)
        dxn = dy * g
        dx2_ref[...] = r * (dxn - xn * jnp.mean(dxn * xn, axis=-1, keepdims=True))

    row = lambda w: pl.BlockSpec((tm, w), lambda i: (i, 0))
    full = lambda a, b: pl.BlockSpec((a, b), lambda i: (0, 0))
    return pl.pallas_call(
        body,
        grid=(s // tm,),
        in_specs=[row(D_FF), full(D_FF, D_MODEL), row(D_MODEL), full(1, D_MODEL), row(D_MODEL)],
        out_specs=[row(D_MODEL), full(1, D_MODEL), full(1, D_MODEL)],
        out_shape=[SDS((s, D_MODEL), F32), SDS((1, D_MODEL), F32), SDS((1, D_MODEL), F32)],
        compiler_params=_cparams("arbitrary"),
        name="down_final",
    )(act, wd, x1, gf, tgt)


def _swiglu_bwd(dx2, wd, gate, up, *, tm, tn):
    s = dx2.shape[0]

    def body(d_ref, w_ref, gate_ref, up_ref, dgate_ref, dup_ref):
        dact = lax.dot_general(d_ref[...].astype(BF16), w_ref[...], NT, preferred_element_type=F32)
        gate = gate_ref[...]
        sg = jax.nn.sigmoid(gate)
        dup_ref[...] = (dact * (gate * sg)).astype(BF16)
        dgate_ref[...] = (dact * up_ref[...] * (sg * (1.0 + gate * (1.0 - sg)))).astype(BF16)

    ospec = pl.BlockSpec((tm, tn), lambda c, r: (r, c))
    return pl.pallas_call(
        body,
        grid=(D_FF // tn, s // tm),
        in_specs=[pl.BlockSpec((tm, D_MODEL), lambda c, r: (r, 0)), pl.BlockSpec((tn, D_MODEL), lambda c, r: (c, 0)),
                  ospec, ospec],
        out_specs=[ospec, ospec],
        out_shape=[SDS((s, D_FF), BF16), SDS((s, D_FF), BF16)],
        compiler_params=_cparams("arbitrary", "arbitrary"),
        name="swiglu_bwd",
    )(dx2, wd, gate, up)


def _mm_tn(a, bs, *, ta, ts, name):
    s, ka = a.shape
    n = len(bs)

    def body(a_ref, *refs):
        b_refs, o_refs = refs[:n], refs[n:]

        @pl.when(pl.program_id(1) == 0)
        def _():
            for o_ref in o_refs:
                o_ref[...] = jnp.zeros_like(o_ref)

        av = a_ref[...].astype(BF16)
        for b_ref, o_ref in zip(b_refs, o_refs):
            o_ref[...] += lax.dot_general(av, b_ref[...].astype(BF16), TN, preferred_element_type=F32)

    return pl.pallas_call(
        body,
        grid=(ka // ta, s // ts),
        in_specs=[pl.BlockSpec((ts, ta), lambda i, k: (k, i))]
        + [pl.BlockSpec((ts, b.shape[1]), lambda i, k: (k, 0)) for b in bs],
        out_specs=[pl.BlockSpec((ta, b.shape[1]), lambda i, k: (i, 0)) for b in bs],
        out_shape=[SDS((ka, b.shape[1]), F32) for b in bs],
        compiler_params=_cparams("arbitrary", "arbitrary"),
        name=name,
    )(a, *bs)


def _mm_tn_shared(as_, b, *, ts, name):
    s, nb_ = b.shape
    n = len(as_)

    def body(*refs):
        a_refs, b_ref, o_refs = refs[:n], refs[n], refs[n + 1:]

        @pl.when(pl.program_id(0) == 0)
        def _():
            for o_ref in o_refs:
                o_ref[...] = jnp.zeros_like(o_ref)

        bv = b_ref[...].astype(BF16)
        for a_ref, o_ref in zip(a_refs, o_refs):
            o_ref[...] += lax.dot_general(a_ref[...].astype(BF16), bv, TN, preferred_element_type=F32)

    return pl.pallas_call(
        body,
        grid=(s // ts,),
        in_specs=[pl.BlockSpec((ts, a.shape[1]), lambda k: (k, 0)) for a in as_] + [pl.BlockSpec((ts, nb_), lambda k: (k, 0))],
        out_specs=[pl.BlockSpec((a.shape[1], nb_), lambda k: (0, 0)) for a in as_],
        out_shape=[SDS((a.shape[1], nb_), F32) for a in as_],
        compiler_params=_cparams("arbitrary"),
        name=name,
    )(*as_, b)


def _norm_bwd(dh, x, r, g, dres):
    xn = x * r
    dxn = dh * g
    dx = dres + r * (dxn - xn * jnp.mean(dxn * xn, axis=-1, keepdims=True))
    return dx, jnp.sum(dh * xn, axis=0, keepdims=True)


def _mlp_in_bwd(dgate, dup, wg_t, wu_t, w_out, x1, r2, g2, dx2, *, tm):
    s = x1.shape[0]

    def body(dg_ref, du_ref, wg_ref, wu_ref, wo_ref, x_ref, r_ref, g_ref, d_ref, dx1_ref, dmix_ref, dg2_ref):
        @pl.when(pl.program_id(0) == 0)
        def _():
            dg2_ref[...] = jnp.zeros_like(dg2_ref)

        dh2 = (jnp.dot(dg_ref[...], wg_ref[...], preferred_element_type=F32)
               + jnp.dot(du_ref[...], wu_ref[...], preferred_element_type=F32))
        dx1, dg2 = _norm_bwd(dh2, x_ref[...], r_ref[...], g_ref[...], d_ref[...])
        dg2_ref[...] += dg2
        dx1_ref[...] = dx1
        dmix_ref[...] = lax.dot_general(dx1.astype(BF16), wo_ref[...], NT, preferred_element_type=F32)

    row = lambda w: pl.BlockSpec((tm, w), lambda i: (i, 0))
    full = lambda a, b: pl.BlockSpec((a, b), lambda i: (0, 0))
    return pl.pallas_call(
        body,
        grid=(s // tm,),
        in_specs=[row(D_FF), row(D_FF), full(D_FF, D_MODEL), full(D_FF, D_MODEL), full(D_MODEL, D_MODEL),
                  row(D_MODEL), row(1), full(1, D_MODEL), row(D_MODEL)],
        out_specs=[row(D_MODEL), row(D_MODEL), full(1, D_MODEL)],
        out_shape=[SDS((s, D_MODEL), F32), SDS((s, D_MODEL), F32), SDS((1, D_MODEL), F32)],
        compiler_params=_cparams("arbitrary"),
        name="mlp_in_bwd",
    )(dgate, dup, wg_t, wu_t, w_out, x1, r2, g2, dx2)


def _pool_bwd(dmixed, pooled, w_pool, pool_scale, *, tm):
    s = pooled.shape[0]
    nt = s // tm
    ng = len(POOL_WINDOWS)

    def body(d_ref, p_ref, w_ref, sc_ref, du_ref, dw_ref, dsc_ref, head_ref):
        i = pl.program_id(0)

        @pl.when(i == 0)
        def _():
            head_ref[...] = jnp.zeros_like(head_ref)
            dw_ref[...] = jnp.zeros_like(dw_ref)
            dsc_ref[...] = jnp.zeros_like(dsc_ref)

        row0 = (nt - 1 - i) * tm
        for g, w in enumerate(POOL_WINDOWS):
            cols = slice(g * POOL_G, (g + 1) * POOL_G)
            wb = w_ref[g].astype(BF16)
            pooled_g = p_ref[:, cols]
            dpo = d_ref[:, cols]
            mixed = jnp.dot(pooled_g, wb, preferred_element_type=F32)
            dsc_ref[:, cols] += jnp.sum(dpo * mixed, axis=0, keepdims=True)
            dmp = (dpo * sc_ref[:, cols]).astype(BF16)
            dw_ref[g] += lax.dot_general(pooled_g, dmp, TN, preferred_element_type=F32)
            dpooled = lax.dot_general(dmp, wb, NT, preferred_element_type=F32)
            a = dpooled / _pool_counts(row0, tm, w)
            acc = jnp.concatenate([a, head_ref[:, cols]], axis=0)
            head_ref[:, cols] = a[0:HALO, :]
            k = 1
            while k < w:
                acc = acc + pltpu.roll(acc, tm + HALO - k, axis=0)
                k *= 2
            du_ref[:, cols] = (acc[0:tm, :] - dpooled).astype(BF16)

    rev = lambda i: (nt - 1 - i, 0)
    return pl.pallas_call(
        body,
        grid=(nt,),
        in_specs=[pl.BlockSpec((tm, POOL_W), lambda i: (nt - 1 - i, 1)), pl.BlockSpec((tm, POOL_W), rev),
                  pl.BlockSpec((ng, POOL_G, POOL_G), lambda i: (0, 0, 0)), pl.BlockSpec((1, POOL_W), lambda i: (0, 0))],
        out_specs=[pl.BlockSpec((tm, POOL_W), rev), pl.BlockSpec((ng, POOL_G, POOL_G), lambda i: (0, 0, 0)),
                   pl.BlockSpec((1, POOL_W), lambda i: (0, 0))],
        out_shape=[SDS((s, POOL_W), BF16), SDS((ng, POOL_G, POOL_G), F32), SDS((1, POOL_W), F32)],
        scratch_shapes=[pltpu.VMEM((HALO, POOL_W), F32)],
        compiler_params=_cparams("arbitrary"),
        name="pool_bwd",
    )(dmixed, pooled, w_pool, pool_scale)


def _attn_bwd(qkv, attn_o, dmixed, rowb, ck_col, *, tq):
    s = qkv.shape[0]
    tk = tq
    nb = s // tq

    def body(q_ref, k_ref, v_ref, o_ref, do_ref, rowb_ref, ck_ref, dq_ref, dk_ref, dv_ref, dck_ref, dcq_ref, dq_acc,
             delta_ref, kp_ref, qp_ref, dob_ref, front_ref):
        lane = _iota2((tq, LANES), 1)
        lo = lane < HEAD_DIM
        first = _iota2((8, LANES), 1) < HEAD_DIM
        sel = jnp.where(_iota2((8, LANES), 0) < 4, jnp.where(first, 1.0, 0.0), jnp.where(first, 0.0, 1.0))
        row8 = _iota2((8, tq), 0)

        def prep(b, _):
            st = pl.multiple_of(b * tq, tq)
            do2 = do_ref[pl.ds(st, tq), :]
            delta_ref[b] = _sel_dot(sel, do2 * o_ref[pl.ds(st, tq), :].astype(F32), NT)
            dob_ref[pl.ds(st, tq), :] = do2.astype(BF16)
            dq_acc[pl.ds(st, tq), :] = jnp.zeros((tq, LANES), F32)
            dcq_ref[b] = jnp.zeros((8, tq), F32)
            k2 = k_ref[pl.ds(st, tq), :].astype(F32)
            qs = q_ref[pl.ds(st, tq), :].astype(F32) * 0.125
            ck = ck_ref[pl.ds(st, tq), :]
            for h in range(2):
                kp_ref[h * nb + b] = _augment(k2, h, -ck[:, h:h + 1], True)
                qp_ref[h * nb + b] = _augment(qs, h, jnp.zeros((tq, 1), F32), False)
            return 0

        lax.fori_loop(0, nb, prep, 0)

        def split(t):
            z = jnp.zeros_like(t)
            return jnp.where(lo, t, z), jnp.where(lo, z, t)

        def kv_block(j, _):
            st_j = pl.multiple_of(j * tk, tk)
            ks = split(k_ref[pl.ds(st_j, tk), :])
            vs = split(v_ref[pl.ds(st_j, tk), :])
            kcat = jnp.concatenate(ks, axis=0)

            def stage(i, slot):
                ic = jnp.minimum(i, nb - 1)
                do2 = dob_ref[pl.ds(pl.multiple_of(ic * tq, tq), tq), :]
                for h in range(2):
                    front_ref[4 * slot + h] = lax.dot_general(kp_ref[h * nb + j], qp_ref[h * nb + ic], NT,
                                                              preferred_element_type=F32)
                    front_ref[4 * slot + 2 + h] = lax.dot_general(vs[h], do2, NT, preferred_element_type=F32)

            def q_block(i, slot, carry, diagonal):
                dk_acc, dv_acc, dca, dcb = carry
                ic = jnp.minimum(i, nb - 1)
                st_i = pl.multiple_of(ic * tq, tq)
                q2 = q_ref[pl.ds(st_i, tq), :]
                do2 = dob_ref[pl.ds(st_i, tq), :]
                rb = rowb_ref[ic] + jnp.where(i < nb, 0.0, NEG)
                dl = delta_ref[ic]
                pts, dsts = [], []
                for h in range(2):
                    st = front_ref[4 * slot + h] + rb[h:h + 1, :]
                    if diagonal:
                        st = jnp.where(_iota2((tk, tq), 0) <= _iota2((tk, tq), 1), st, NEG)
                    pt = jnp.exp(st)
                    pts.append(pt.astype(BF16))
                    dsts.append(pt * (front_ref[4 * slot + 2 + h] - dl[4 * h:4 * h + 1, :]))
                dca = dca + jnp.sum(dsts[0], axis=1, keepdims=True)
                dcb = dcb + jnp.sum(dsts[1], axis=1, keepdims=True)
                dcq_ref[ic] += jnp.where(row8 == 0, jnp.sum(dsts[0], axis=0, keepdims=True),
                                         jnp.where(row8 == 4, jnp.sum(dsts[1], axis=0, keepdims=True), 0.0))
                dsb = [d.astype(BF16) for d in dsts]
                dv_acc = dv_acc + jnp.dot(jnp.concatenate(pts, axis=1), jnp.concatenate(split(do2), axis=0),
                                          preferred_element_type=F32)
                dk_acc = dk_acc + jnp.dot(jnp.concatenate(dsb, axis=1), jnp.concatenate(split(q2), axis=0),
                                          preferred_element_type=F32)
                dq_acc[pl.ds(st_i, tq), :] += lax.dot_general(jnp.concatenate(dsb, axis=0), kcat, TN,
                                                              preferred_element_type=F32)
                return dk_acc, dv_acc, dca, dcb

            def pair(t, carry):
                i0 = j + 1 + 2 * t
                stage(i0 + 1, 0)
                carry = q_block(i0, 1, carry, False)
                stage(i0 + 2, 1)
                return q_block(i0 + 1, 0, carry, False)

            zt = jnp.zeros((tk, LANES), F32)
            zc = jnp.zeros((tk, 1), F32)
            stage(j, 0)
            stage(j + 1, 1)
            carry = q_block(j, 0, (zt, zt, zc, zc), True)
            dk_acc, dv_acc, dca, dcb = lax.fori_loop(0, lax.shift_right_logical(nb - j, 1), pair, carry)
            dk_ref[pl.ds(st_j, tk), :] = (dk_acc * 0.125).astype(BF16)
            dv_ref[pl.ds(st_j, tk), :] = dv_acc.astype(BF16)
            dck_ref[pl.ds(st_j, tk), :] = jnp.where(_iota2((tk, 2), 1) == 0, dca, dcb)
            return 0

        lax.fori_loop(0, nb, kv_block, 0)
        dq_ref[...] = (dq_acc[...] * 0.125).astype(BF16)

    col = lambda off: pl.BlockSpec((s, LANES), lambda p: (0, off + p))
    return pl.pallas_call(
        body,
        grid=(N_PAIRS,),
        in_specs=[col(0), col(N_PAIRS), col(2 * N_PAIRS), col(0), col(0),
                  pl.BlockSpec((None, nb, 2, tq), lambda p: (p, 0, 0, 0)),
                  pl.BlockSpec((None, s, 2), lambda p: (p, 0, 0))],
        out_specs=[col(0), col(0), col(0), pl.BlockSpec((None, s, 2), lambda p: (p, 0, 0)),
                   pl.BlockSpec((None, nb, 8, tq), lambda p: (p, 0, 0, 0))],
        out_shape=[SDS((s, ATTN_W), BF16), SDS((s, ATTN_W), BF16), SDS((s, ATTN_W), BF16), SDS((N_PAIRS, s, 2), F32),
                   SDS((N_PAIRS, nb, 8, tq), F32)],
        scratch_shapes=[pltpu.VMEM((s, LANES), F32), pltpu.VMEM((nb, 8, tq), F32),
                        pltpu.VMEM((2 * nb, tk, LANES), BF16), pltpu.VMEM((2 * nb, tq, LANES), BF16),
                        pltpu.VMEM((s, LANES), BF16), pltpu.VMEM((8, tk, tq), F32)],
        compiler_params=_cparams("arbitrary"),
        name="attn_bwd",
    )(qkv, qkv, qkv, attn_o, dmixed, rowb, ck_col)


def _forget_bwd(dc_t, fl_t, b_rows):
    rows = fl_t.shape[0]
    nb = rows // N_HEADS

    def body(dc_ref, fl_ref, b_ref, dfl_ref, db_ref):
        dc = dc_ref[...]
        lower = _iota2((LANES, LANES), 0) >= _iota2((LANES, LANES), 1)
        ones = jnp.ones((LANES, LANES), F32)
        rr, cc, same = _head_block_masks(rows, nb)
        dlf = _dot_sel(dc, lower) + _sel_dot(same & (cc > rr), _dot_sel(dc, ones))
        dfl = dlf / (1.0 + jnp.exp(fl_ref[...] + b_ref[...]))
        dfl_ref[...] = dfl
        shift = nb.bit_length() - 1
        hsel = lax.shift_right_logical(_iota2((N_HEADS, rows), 1), shift) == _iota2((N_HEADS, rows), 0)
        db_ref[...] = _sel_dot(hsel, _dot_sel(dfl, ones))

    return pl.pallas_call(body, out_shape=[SDS(fl_t.shape, F32), SDS((N_HEADS, LANES), F32)],
                          compiler_params=_cparams(), name="forget_bwd")(dc_t, fl_t, b_rows)


def _in_bwd(dq, dk, dv, du, dfl, w_in_t, x, r1, g1, dx1, *, tm):
    s = x.shape[0]
    pieces = ((0, ATTN_W), (ATTN_W, 2 * ATTN_W), (2 * ATTN_W, QKV_W), (U_OFF, F_OFF), (F_OFF, IN_PAD))

    def body(dq_ref, dk_ref, dv_ref, du_ref, df_ref, w_ref, x_ref, r_ref, g_ref, d_ref, dx_ref, dg1_ref):
        @pl.when(pl.program_id(0) == 0)
        def _():
            dg1_ref[...] = jnp.zeros_like(dg1_ref)

        dh = None
        for ref, (c0, c1) in zip((dq_ref, dk_ref, dv_ref, du_ref, df_ref), pieces):
            t = jnp.dot(ref[...], w_ref[c0:c1, :], preferred_element_type=F32)
            dh = t if dh is None else dh + t
        dx, dg1 = _norm_bwd(dh, x_ref[...], r_ref[...], g_ref[...], d_ref[...])
        dx_ref[...] = dx
        dg1_ref[...] += dg1

    row = lambda w: pl.BlockSpec((tm, w), lambda i: (i, 0))
    full = lambda a, b: pl.BlockSpec((a, b), lambda i: (0, 0))
    return pl.pallas_call(
        body,
        grid=(s // tm,),
        in_specs=[row(ATTN_W), row(ATTN_W), row(ATTN_W), row(POOL_W), row(LANES), full(IN_PAD, D_MODEL),
                  row(D_MODEL), row(1), full(1, D_MODEL), row(D_MODEL)],
        out_specs=[row(D_MODEL), full(1, D_MODEL)],
        out_shape=[SDS((s, D_MODEL), F32), SDS((1, D_MODEL), F32)],
        compiler_params=_cparams("arbitrary"),
        name="in_bwd",
    )(dq, dk, dv, du, dfl, w_in_t, x, r1, g1, dx1)


def _tiles(s):
    big = min(512, s)
    return dict(row=big, attn=min(256, s // 2), mlp_bwd=min(256, s))


def _tie(a, token):
    return a + token[0:1, 0:1].astype(a.dtype)


def _local_step(x, tgt, p, weight, emit):
    s = x.shape[0]
    t = _tiles(s)
    tm, tq = t["row"], t["attn"]
    nb = s // LANES
    nqb = s // tq
    g1, g2, gf = p["norm1_g"], p["norm2_g"], p["final_g"].reshape(1, D_MODEL)
    w_pool, pool_scale = p["w_pool"][0], p["pool_scale"]

    w_in_t = weight("w_in", x)
    h, r1, qkv, u, fl = _norm_proj(x, g1, w_in_t, tm=tm)
    fl_t = fl[:, :N_HEADS].T.reshape(N_HEADS * nb, LANES)
    b_rows = jnp.repeat(p["b_forget"].reshape(N_HEADS), nb).reshape(N_HEADS * nb, 1)
    c = _forget_cumsum(fl_t, b_rows).reshape(N_PAIRS, 2, s)
    c_col = c.transpose(0, 2, 1)
    c_rowblk = c.reshape(N_PAIRS, 2, nqb, tq).transpose(0, 2, 1, 3)
    attn_o, lse = _attn_fwd(qkv, c_col, tq=tq)
    pooled, pool_o = _pool_fwd(u, w_pool, pool_scale, tm=tm)
    w_out = weight("w_out", attn_o)
    x1, h2, r2 = _out_norm2(attn_o, pool_o, w_out, x, g2, tm=tm)
    wg_t, wu_t = weight("w_gate_up", h2)
    gate, up, act = _gate_up(h2, wg_t, wu_t, tm=tm, tn=D_FF // 2)
    wd = weight("w_down", act)
    dx2, loss_row, d_gf = _down_final(act, wd, x1, gf, tgt, tm=tm)

    dgate, dup = _swiglu_bwd(dx2, wd, gate, up, tm=tm, tn=D_FF // 2)
    (d_wd,) = _mm_tn(act, [dx2], ta=D_FF // 2, ts=tm, name="grad_w_down")
    token = emit("w_down", d_wd)
    (d_wg_t,) = _mm_tn(dgate, [h2], ta=D_FF // 2, ts=tm, name="grad_w_gate")
    (d_wu_t,) = _mm_tn(dup, [h2], ta=D_FF // 2, ts=tm, name="grad_w_up")
    token = token + emit("w_gate_up", (d_wg_t, d_wu_t))
    dx1, dmixed, d_g2 = _mlp_in_bwd(dgate, dup, wg_t, wu_t, w_out, x1, r2, _tie(g2, token), dx2, tm=t["mlp_bwd"])
    du, d_wpool, d_pscale = _pool_bwd(dmixed, pooled, w_pool, pool_scale, tm=tm)
    (d_wo_a,) = _mm_tn(attn_o, [dx1], ta=ATTN_W, ts=tm, name="grad_w_out_attn")
    (d_wo_p,) = _mm_tn(pool_o, [dx1], ta=POOL_W, ts=tm, name="grad_w_out_pool")
    token = emit("w_out", jnp.concatenate([d_wo_a, d_wo_p], axis=0))
    rowb = _tie(c_rowblk - lse, token)
    dq, dk, dv, dck, dcq = _attn_bwd(qkv, attn_o, dmixed, rowb, c_col, tq=tq)
    dcq = dcq[:, :, 0::4, :].transpose(0, 2, 1, 3).reshape(N_PAIRS, 2, s)
    dc_t = (dcq - dck.transpose(0, 2, 1)).reshape(N_HEADS * nb, LANES)
    dfl_t, db = _forget_bwd(dc_t, fl_t, b_rows)
    dfl = jnp.pad(dfl_t.reshape(N_HEADS, s).T, ((0, 0), (0, LANES - N_HEADS))).astype(BF16)
    d_wq, d_wk, d_wv, d_wu_in, d_wf = _mm_tn_shared([dq, dk, dv, du, dfl], h, ts=tm, name="grad_w_in")
    token = emit("w_in", jnp.concatenate([d_wq, d_wk, d_wv, d_wf[:N_HEADS], d_wu_in], axis=0))
    dx, d_g1 = _in_bwd(dq, dk, dv, du, dfl, w_in_t, x, r1, _tie(g1, token), dx1, tm=tm)

    small = dict(norm1_g=d_g1, b_forget=db[:, 0].reshape(1, N_HEADS), w_pool=d_wpool, pool_scale=d_pscale,
                 norm2_g=d_g2, final_g=d_gf)
    return loss_row, dx, small


def _my_index():
    return 4 * lax.axis_index("x") + 2 * lax.axis_index("y") + lax.axis_index("c")


def _peer(k):
    pos = [lax.axis_index(a) for a in ("x", "y", "c")]
    flipped = tuple(1 - p if (k >> b) & 1 else p for p, b in zip(pos, (2, 1, 0)))
    return flipped, 4 * flipped[0] + 2 * flipped[1] + flipped[2]


_HBM = pl.BlockSpec(memory_space=pltpu.HBM)
_SEM = pl.BlockSpec(memory_space=pltpu.SEMAPHORE)
_DATAFLOW = pltpu.SideEffectType.DATAFLOW_SIDE_EFFECTING


def _peer_copies(ins, lands, send_sems, recv_sems, scatter, arrivals):
    me = _my_index()
    copies = []
    for w in range(len(ins)):
        for k in range(1, N_DEV):
            dev, idx = _peer(k)
            copies.append(pltpu.make_async_remote_copy(
                src_ref=ins[w].at[idx] if scatter[w] else ins[w], dst_ref=lands[w].at[idx if arrivals else me],
                send_sem=send_sems[w].at[k - 1], recv_sem=recv_sems[w].at[k - 1], device_id=dev, device_id_type=MESH))
    return copies


def _exchange_start(arrays, scatter, name):
    n = len(arrays)
    land_shapes = [(N_DEV,) + tuple(a.shape[1:] if sc else a.shape) for a, sc in zip(arrays, scatter)]

    def body(*refs):
        ins, lands = refs[:n], refs[n:2 * n]
        send_sems, recv_sems = refs[2 * n:3 * n], refs[3 * n:4 * n]
        token = refs[6 * n]
        for cp in _peer_copies(ins, lands, send_sems, recv_sems, scatter, False):
            cp.start()
        token[...] = jnp.zeros_like(token)

    sem = pltpu.SemaphoreType.DMA((N_DEV - 1,))
    outs = pl.pallas_call(
        body,
        in_specs=[_HBM] * (2 * n),
        out_specs=[_SEM] * (2 * n) + [_HBM] * (2 * n) + [pl.BlockSpec(memory_space=pltpu.VMEM)],
        out_shape=[sem] * (2 * n) + [pltpu.HBM(a.shape, a.dtype) for a in arrays]
        + [pltpu.HBM(sh, a.dtype) for sh, a in zip(land_shapes, arrays)] + [SDS((8, LANES), F32)],
        input_output_aliases={i: 2 * n + i for i in range(2 * n)},
        compiler_params=pltpu.CompilerParams(has_side_effects=_DATAFLOW),
        name=name,
    )(*[pltpu.with_memory_space_constraint(a, pltpu.HBM) for a in arrays],
      *[pltpu.with_memory_space_constraint(lax.empty(sh, a.dtype), pltpu.HBM) for sh, a in zip(land_shapes, arrays)])
    handles = [dict(send=outs[w], recv=outs[n + w], src=outs[2 * n + w], land=outs[3 * n + w], scatter=scatter[w])
               for w in range(n)]
    return handles, outs[4 * n]


def _exchange_wait(handles, after, name):
    n = len(handles)
    scatter = [h["scatter"] for h in handles]

    def body(*refs):
        ins, lands = refs[:n], refs[n:2 * n]
        send_sems, recv_sems = refs[2 * n:3 * n], refs[3 * n:4 * n]
        for cp in _peer_copies(ins, lands, send_sems, recv_sems, scatter, False):
            cp.wait_send()
        for cp in _peer_copies(ins, lands, send_sems, recv_sems, scatter, True):
            cp.wait_recv()

    srcs, lands = [h["src"] for h in handles], [h["land"] for h in handles]
    outs = pl.pallas_call(
        body,
        in_specs=[_HBM] * (2 * n) + [_SEM] * (2 * n) + [pl.BlockSpec(memory_space=pl.ANY)],
        out_specs=[_HBM] * (2 * n),
        out_shape=[pltpu.HBM(a.shape, a.dtype) for a in srcs + lands],
        input_output_aliases={i: i for i in range(2 * n)},
        compiler_params=pltpu.CompilerParams(has_side_effects=_DATAFLOW),
        name=name,
    )(*srcs, *lands, *[h["send"] for h in handles], *[h["recv"] for h in handles], after)
    me = _my_index()
    full = []
    for src, land, sc in zip(outs[:n], outs[n:], scatter):
        own = lax.dynamic_index_in_dim(src, me, 0, keepdims=True) if sc else src[None]
        full.append(lax.dynamic_update_slice(land, own, (me,) + (0,) * (land.ndim - 1)))
    return full


def _adamw(parts, w, m, v, name):
    rows, cols = w.shape
    tr = rows // 4 if rows % 32 == 0 else rows

    def body(p_ref, w_ref, m_ref, v_ref, g_ref, d_ref, mo_ref, vo_ref):
        g = p_ref[0].astype(F32)
        for d in range(1, N_DEV):
            g = g + p_ref[d].astype(F32)
        m_new = ADAM_B1 * m_ref[...] + (1.0 - ADAM_B1) * g
        v_new = ADAM_B2 * v_ref[...] + (1.0 - ADAM_B2) * (g * g)
        m_hat = m_new / (1.0 - ADAM_B1 ** ADAM_STEP)
        v_hat = v_new / (1.0 - ADAM_B2 ** ADAM_STEP)
        g_ref[...] = g
        d_ref[...] = -ADAM_LR * (m_hat / (jnp.sqrt(v_hat) + ADAM_EPS) + ADAM_WD * w_ref[...])
        mo_ref[...] = m_new
        vo_ref[...] = v_new

    blk = pl.BlockSpec((tr, cols), lambda i: (i, 0))
    return pl.pallas_call(
        body,
        grid=(rows // tr,),
        in_specs=[pl.BlockSpec((N_DEV, tr, cols), lambda i: (0, i, 0)), blk, blk, blk],
        out_specs=[blk] * 4,
        out_shape=[SDS((rows, cols), F32)] * 4,
        compiler_params=_cparams("arbitrary"),
        name=name,
    )(parts, w, m, v)


_SMALL = (("w_pool", 512), ("norm1_g", 8), ("norm2_g", 8), ("final_g", 8), ("pool_scale", 8), ("b_forget", 8),
          ("loss", 8))
_SMALL_ROWS = sum(r for _, r in _SMALL)


def _pack_small(vals):
    parts = []
    for name, rows in _SMALL:
        flat = vals[name].reshape(-1).astype(F32)
        parts.append(jnp.pad(flat, (0, rows * LANES - flat.shape[0])).reshape(rows, LANES))
    return jnp.concatenate(parts, axis=0)


def _unpack_small(packed, like):
    out, r0 = {}, 0
    for name, rows in _SMALL:
        n = like[name].size
        out[name] = packed[r0:r0 + rows].reshape(-1)[:n].reshape(like[name].shape)
        r0 += rows
    return out


def kernel(x, norm1_g, w_in, b_forget, w_pool, pool_scale, w_out, norm2_g, w_gate, w_up, w_down, final_g, loss_target, m_norm1_g, m_w_in, m_b_forget, m_w_pool, m_pool_scale, m_w_out, m_norm2_g, m_w_gate, m_w_up, m_w_down, m_final_g, v_norm1_g, v_w_in, v_b_forget, v_w_pool, v_pool_scale, v_w_out, v_norm2_g, v_w_gate, v_w_up, v_w_down, v_final_g):
    big = ("w_in", "w_out", "w_gate", "w_up", "w_down")
    order = ("norm1_g", "w_in", "b_forget", "w_pool", "pool_scale", "w_out", "norm2_g", "w_gate", "w_up", "w_down",
             "final_g")
    w = dict(norm1_g=norm1_g, w_in=w_in, b_forget=b_forget, w_pool=w_pool, pool_scale=pool_scale, w_out=w_out,
             norm2_g=norm2_g, w_gate=w_gate, w_up=w_up, w_down=w_down, final_g=final_g)
    m = dict(norm1_g=m_norm1_g, w_in=m_w_in, b_forget=m_b_forget, w_pool=m_w_pool, pool_scale=m_pool_scale,
             w_out=m_w_out, norm2_g=m_norm2_g, w_gate=m_w_gate, w_up=m_w_up, w_down=m_w_down, final_g=m_final_g)
    v = dict(norm1_g=v_norm1_g, w_in=v_w_in, b_forget=v_b_forget, w_pool=v_w_pool, pool_scale=v_pool_scale,
             w_out=v_w_out, norm2_g=v_norm2_g, w_gate=v_w_gate, w_up=v_w_up, w_down=v_w_down, final_g=v_final_g)

    flipped = ("w_in", "w_gate", "w_up")
    shard = lambda d, n: d[n][0].T if n in flipped else d[n][0]
    gather, _ = _exchange_start([shard(w, n).astype(BF16) for n in big], [False] * len(big), "gather_start")
    gather = dict(zip(big, gather))

    def gathered(names, after):
        return _exchange_wait([gather[n] for n in names], after, "gather_wait_" + names[0])

    def weight(name, after):
        if name == "w_in":
            full = gathered(["w_in"], after)[0].reshape(IN_W, D_MODEL)
            f0 = QKV_W + N_HEADS
            return jnp.concatenate([full[:QKV_W], full[f0:], full[QKV_W:f0],
                                    jnp.zeros((IN_PAD - IN_W, D_MODEL), BF16)], axis=0)
        if name == "w_out":
            return gathered(["w_out"], after)[0].reshape(D_MODEL, D_MODEL)
        if name == "w_gate_up":
            return [g.reshape(D_FF, D_MODEL) for g in gathered(["w_gate", "w_up"], after)]
        return gathered(["w_down"], after)[0].reshape(D_FF, D_MODEL)

    rows = lambda g: g.reshape(N_DEV, g.shape[0] // N_DEV, g.shape[1])
    sent = {}

    def emit(name, grad):
        if name == "w_gate_up":
            names, slots = ["w_gate", "w_up"], [rows(g) for g in grad]
        else:
            names, slots = [name], [rows(grad).astype(BF16) if name == "w_in" else rows(grad)]
        handles, token = _exchange_start(slots, [True] * len(slots), "grads_start_" + name)
        sent.update(zip(names, handles))
        return token

    loss_row, dx, small_grads = _local_step(x[0], loss_target[0], w, weight, emit)

    packed = _pack_small(dict(small_grads, loss=0.5 / D_MODEL * jnp.sum(loss_row)))
    (small_handle,), _ = _exchange_start([packed], [False], "grads_start_replicated")

    outs = {}
    after = dx
    for name in ("w_down", "w_gate", "w_up", "w_out", "w_in"):
        (parts,) = _exchange_wait([sent[name]], after, "grads_wait_" + name)
        outs[name] = _adamw(parts, shard(w, name), shard(m, name), shard(v, name), "adamw_" + name)
        after = outs[name][0]
        outs[name] = [(a.T if name in flipped else a)[None] for a in outs[name]]
    (parts,) = _exchange_wait([small_handle], after, "grads_wait_replicated")
    zero = dict(loss=jnp.zeros((), F32))
    small = _adamw(parts, _pack_small(dict(w, **zero)), _pack_small(dict(m, **zero)),
                   _pack_small(dict(v, **zero)), "adamw_replicated")
    small = [_unpack_small(p, dict(w, **zero)) for p in small]
    loss = small[0]["loss"]
    for name in order:
        if name not in outs:
            outs[name] = [p[name] for p in small]

    return (loss, dx[None]) + tuple(outs[n][k] for k in range(4) for n in order)
```

```python
import functools

import jax
import jax.numpy as jnp
from jax import lax
from jax.experimental import pallas as pl
from jax.experimental.pallas import tpu as pltpu

F32 = jnp.float32
BF16 = jnp.bfloat16
SDS = jax.ShapeDtypeStruct

D_MODEL = 1024
ATTN_W = 512
N_HEADS = 8
HEAD_DIM = 64
N_PAIRS = N_HEADS // 2
POOL_W = 512
POOL_WINDOWS = (2, 4, 8, 16)
POOL_G = 128
HALO = 16
IN_W = 3 * ATTN_W + N_HEADS + POOL_W
QKV_W = 3 * ATTN_W
U_OFF = QKV_W
F_OFF = QKV_W + POOL_W
IN_PAD = F_OFF + 128
D_FF = 2816
EPS = 1e-6
NEG = -1e30
N_DEV = 8
LANES = 128

ADAM_LR = 0.001
ADAM_B1 = 0.9
ADAM_B2 = 0.999
ADAM_EPS = 1e-08
ADAM_WD = 0.01
ADAM_STEP = 10

VMEM_LIMIT_BYTES = 56 * 1024 * 1024
MESH = pl.DeviceIdType.MESH
NT = (((1,), (1,)), ((), ()))
TN = (((0,), (0,)), ((), ()))


def _cparams(*sem):
    return pltpu.CompilerParams(dimension_semantics=sem or None, vmem_limit_bytes=VMEM_LIMIT_BYTES)


def _split3(a):
    hi = a.astype(BF16)
    r1 = a - hi.astype(F32)
    mid = r1.astype(BF16)
    lo = (r1 - mid.astype(F32)).astype(BF16)
    return hi, mid, lo


def _dot_sel(a, sel, dims=None):
    sb = sel.astype(BF16)
    if dims is None:
        return sum(jnp.dot(p, sb, preferred_element_type=F32) for p in _split3(a))
    return sum(lax.dot_general(p, sb, dims, preferred_element_type=F32) for p in _split3(a))


def _sel_dot(sel, a, dims=None):
    sb = sel.astype(BF16)
    if dims is None:
        return sum(jnp.dot(sb, p, preferred_element_type=F32) for p in _split3(a))
    return sum(lax.dot_general(sb, p, dims, preferred_element_type=F32) for p in _split3(a))


def _iota2(shape, dim):
    return lax.broadcasted_iota(jnp.int32, shape, dim)


def _norm_proj(x, g1, w_in_t, *, tm):
    s = x.shape[0]

    def body(x_ref, g_ref, w_ref, h_ref, r_ref, qkv_ref, u_ref, fl_ref):
        xv = x_ref[...]
        r = lax.rsqrt(jnp.mean(xv * xv, axis=-1, keepdims=True) + EPS)
        h = (xv * r * g_ref[...]).astype(BF16)
        h_ref[...] = h
        r_ref[...] = r
        qkv_ref[...] = lax.dot_general(h, w_ref[0:QKV_W, :], NT, preferred_element_type=F32).astype(BF16)
        u_ref[...] = lax.dot_general(h, w_ref[U_OFF:F_OFF, :], NT, preferred_element_type=F32)
        fl_ref[...] = lax.dot_general(h, w_ref[F_OFF:IN_PAD, :], NT, preferred_element_type=F32)

    row = lambda w: pl.BlockSpec((tm, w), lambda i: (i, 0))
    full = lambda a, b: pl.BlockSpec((a, b), lambda i: (0, 0))
    return pl.pallas_call(
        body,
        grid=(s // tm,),
        in_specs=[row(D_MODEL), full(1, D_MODEL), full(IN_PAD, D_MODEL)],
        out_specs=[row(D_MODEL), row(1), row(QKV_W), row(POOL_W), row(LANES)],
        out_shape=[SDS((s, D_MODEL), BF16), SDS((s, 1), F32), SDS((s, QKV_W), BF16), SDS((s, POOL_W), F32),
                   SDS((s, LANES), F32)],
        compiler_params=_cparams("arbitrary"),
        name="norm_proj",
    )(x, g1, w_in_t)


def _head_block_masks(rows, nb):
    shift = nb.bit_length() - 1
    rr, cc = _iota2((rows, rows), 0), _iota2((rows, rows), 1)
    same = lax.shift_right_logical(rr, shift) == lax.shift_right_logical(cc, shift)
    return rr, cc, same


def _forget_cumsum(fl_t, b_rows):
    rows = fl_t.shape[0]
    nb = rows // N_HEADS

    def body(fl_ref, b_ref, c_ref):
        z = fl_ref[...] + b_ref[...]
        lf = jnp.minimum(z, 0.0) - jnp.log1p(jnp.exp(-jnp.abs(z)))
        upper = _iota2((LANES, LANES), 0) <= _iota2((LANES, LANES), 1)
        within = _dot_sel(lf, upper)
        tot = _dot_sel(lf, jnp.ones((LANES, LANES), F32))
        rr, cc, same = _head_block_masks(rows, nb)
        c_ref[...] = within + _sel_dot(same & (cc < rr), tot)

    return pl.pallas_call(body, out_shape=SDS(fl_t.shape, F32), compiler_params=_cparams(), name="forget_cumsum")(
        fl_t, b_rows)


BIAS_LANES = 3


def _augment(t, h, col, col_first):
    n = t.shape[0]
    lane = _iota2((n, LANES), 1)
    own = (lane < HEAD_DIM) if h == 0 else (lane >= HEAD_DIM)
    b0 = HEAD_DIM if h == 0 else 0
    c0, o0 = (b0, b0 + BIAS_LANES) if col_first else (b0 + BIAS_LANES, b0)
    x = jnp.where(own, t, 0.0)
    for off, piece in enumerate(_split3(col)):
        x = jnp.where(lane == c0 + off, piece.astype(F32), x)
    x = jnp.where((lane >= o0) & (lane < o0 + BIAS_LANES), 1.0, x)
    return x.astype(BF16)


def _attn_fwd(qkv, c_col, *, tq):
    s = qkv.shape[0]
    tk = tq
    nb = s // tq

    def body(q_ref, k_ref, v_ref, cq_ref, ck_ref, o_ref, lse_ref, kp_ref, vt_ref, st_ref):
        i = pl.program_id(1)

        @pl.when(i == 0)
        def _():
            def prep(jb, _):
                st = pl.multiple_of(jb * tk, tk)
                k2 = k_ref[pl.ds(st, tk), :].astype(F32)
                ck = ck_ref[pl.ds(st, tk), :]
                for h in range(2):
                    kp_ref[h * nb + jb] = _augment(k2, h, -ck[:, h:h + 1], True)
                vt_ref[jb] = v_ref[pl.ds(st, tk), :].astype(F32).T.astype(BF16)
                return 0

            lax.fori_loop(0, nb, prep, 0)

        qs = q_ref[...].astype(F32) * 0.125
        cq = cq_ref[...]
        qp = [_augment(qs, h, cq[:, h:h + 1], False) for h in range(2)]

        def logits(j):
            return tuple(lax.dot_general(kp_ref[h * nb + j], qp[h], NT, preferred_element_type=F32) for h in range(2))

        def softmax_pv(j, sts, stats, masked):
            out = []
            for h in range(2):
                m, l, acc = stats[h]
                st = sts[h]
                if masked:
                    st = jnp.where(_iota2((tk, tq), 0) <= _iota2((tk, tq), 1), st, NEG)
                m_new = jnp.maximum(m, jnp.max(st, axis=0, keepdims=True))
                alpha = jnp.exp(m - m_new)
                p = jnp.exp(st - m_new)
                l = alpha * l + jnp.sum(p, axis=0, keepdims=True)
                vt = vt_ref[j, h * HEAD_DIM:(h + 1) * HEAD_DIM, :]
                acc = alpha * acc + jnp.dot(vt, p.astype(BF16), preferred_element_type=F32)
                out.append((m_new, l, acc))
            return tuple(out)

        def put(slot, sts):
            for h in range(2):
                st_ref[2 * slot + h] = sts[h]

        def get(slot):
            return tuple(st_ref[2 * slot + h] for h in range(2))

        def step(j, stats):
            nxt = logits(j + 1)
            stats = softmax_pv(j, get(j % 2), stats, False)
            put((j + 1) % 2, nxt)
            return stats

        init = tuple((jnp.full((1, tq), NEG, F32), jnp.zeros((1, tq), F32), jnp.zeros((HEAD_DIM, tq), F32))
                     for _ in range(2))
        put(0, logits(0))
        stats = lax.fori_loop(0, i, step, init)
        (ma, la, acca), (mb, lb, accb) = softmax_pv(i, get(i % 2), stats, True)
        o_ref[...] = jnp.concatenate([acca / la, accb / lb], axis=0).T.astype(BF16)
        lse_ref[...] = jnp.where(_iota2((2, tq), 0) == 0, ma + jnp.log(la), mb + jnp.log(lb))

    return pl.pallas_call(
        body,
        grid=(N_PAIRS, nb),
        in_specs=[
            pl.BlockSpec((tq, LANES), lambda p, i: (i, p)),
            pl.BlockSpec((s, LANES), lambda p, i: (0, N_PAIRS + p)),
            pl.BlockSpec((s, LANES), lambda p, i: (0, 2 * N_PAIRS + p)),
            pl.BlockSpec((None, tq, 2), lambda p, i: (p, i, 0)),
            pl.BlockSpec((None, s, 2), lambda p, i: (p, 0, 0)),
        ],
        out_specs=[
            pl.BlockSpec((tq, LANES), lambda p, i: (i, p)),
            pl.BlockSpec((None, None, 2, tq), lambda p, i: (p, i, 0, 0)),
        ],
        out_shape=[SDS((s, ATTN_W), BF16), SDS((N_PAIRS, nb, 2, tq), F32)],
        scratch_shapes=[pltpu.VMEM((2 * nb, tk, LANES), BF16), pltpu.VMEM((nb, LANES, tk), BF16),
                        pltpu.VMEM((4, tk, tq), F32)],
        compiler_params=_cparams("arbitrary", "arbitrary"),
        name="attn_fwd",
    )(qkv, qkv, qkv, c_col, c_col)


def _pool_counts(row0, tm, w):
    t = row0 + _iota2((tm, 1), 0)
    return jnp.minimum(t + 1, w).astype(F32)


def _pool_fwd(u, w_pool, pool_scale, *, tm):
    s = u.shape[0]

    def body(u_ref, w_ref, sc_ref, pooled_ref, po_ref, tail_ref):
        i = pl.program_id(0)

        @pl.when(i == 0)
        def _():
            tail_ref[...] = jnp.zeros_like(tail_ref)

        uv = u_ref[...]
        ext = jnp.concatenate([tail_ref[...], uv], axis=0)
        tail_ref[...] = uv[tm - HALO:, :]
        for g, w in enumerate(POOL_WINDOWS):
            cols = slice(g * POOL_G, (g + 1) * POOL_G)
            acc = ext[:, cols]
            k = 1
            while k < w:
                acc = acc + pltpu.roll(acc, k, axis=0)
                k *= 2
            pooled = (acc[HALO:, :] / _pool_counts(i * tm, tm, w) - uv[:, cols]).astype(BF16)
            pooled_ref[:, cols] = pooled
            mixed = jnp.dot(pooled, w_ref[g].astype(BF16), preferred_element_type=F32)
            po_ref[:, cols] = (mixed * sc_ref[:, cols]).astype(BF16)

    row = pl.BlockSpec((tm, POOL_W), lambda i: (i, 0))
    return pl.pallas_call(
        body,
        grid=(s // tm,),
        in_specs=[row, pl.BlockSpec((len(POOL_WINDOWS), POOL_G, POOL_G), lambda i: (0, 0, 0)),
                  pl.BlockSpec((1, POOL_W), lambda i: (0, 0))],
        out_specs=[row, row],
        out_shape=[SDS((s, POOL_W), BF16), SDS((s, POOL_W), BF16)],
        scratch_shapes=[pltpu.VMEM((HALO, POOL_W), F32)],
        compiler_params=_cparams("arbitrary"),
        name="pool_fwd",
    )(u, w_pool, pool_scale)


def _out_norm2(attn_o, pool_o, w_out, x, g2, *, tm):
    s = x.shape[0]

    def body(a_ref, p_ref, w_ref, x_ref, g_ref, x1_ref, h2_ref, r_ref):
        x1 = (x_ref[...] + jnp.dot(a_ref[...], w_ref[0:ATTN_W, :], preferred_element_type=F32)
              + jnp.dot(p_ref[...], w_ref[ATTN_W:, :], preferred_element_type=F32))
        r = lax.rsqrt(jnp.mean(x1 * x1, axis=-1, keepdims=True) + EPS)
        x1_ref[...] = x1
        r_ref[...] = r
        h2_ref[...] = (x1 * r * g_ref[...]).astype(BF16)

    row = lambda w: pl.BlockSpec((tm, w), lambda i: (i, 0))
    full = lambda a, b: pl.BlockSpec((a, b), lambda i: (0, 0))
    return pl.pallas_call(
        body,
        grid=(s // tm,),
        in_specs=[row(ATTN_W), row(POOL_W), full(D_MODEL, D_MODEL), row(D_MODEL), full(1, D_MODEL)],
        out_specs=[row(D_MODEL), row(D_MODEL), row(1)],
        out_shape=[SDS((s, D_MODEL), F32), SDS((s, D_MODEL), BF16), SDS((s, 1), F32)],
        compiler_params=_cparams("arbitrary"),
        name="out_norm2",
    )(attn_o, pool_o, w_out, x, g2)


def _gate_up(h2, wg_t, wu_t, *, tm, tn):
    s = h2.shape[0]

    def body(h_ref, wg_ref, wu_ref, gate_ref, up_ref, act_ref):
        h = h_ref[...]
        gate = lax.dot_general(h, wg_ref[...], NT, preferred_element_type=F32)
        up = lax.dot_general(h, wu_ref[...], NT, preferred_element_type=F32)
        gate_ref[...] = gate.astype(BF16)
        up_ref[...] = up.astype(BF16)
        act_ref[...] = (gate * jax.nn.sigmoid(gate) * up).astype(BF16)

    wspec = pl.BlockSpec((tn, D_MODEL), lambda c, r: (c, 0))
    ospec = pl.BlockSpec((tm, tn), lambda c, r: (r, c))
    return pl.pallas_call(
        body,
        grid=(D_FF // tn, s // tm),
        in_specs=[pl.BlockSpec((tm, D_MODEL), lambda c, r: (r, 0)), wspec, wspec],
        out_specs=[ospec, ospec, ospec],
        out_shape=[SDS((s, D_FF), BF16), SDS((s, D_FF), BF16), SDS((s, D_FF), BF16)],
        compiler_params=_cparams("arbitrary", "arbitrary"),
        name="gate_up",
    )(h2, wg_t, wu_t)


def _down_final(act, wd, x1, gf, tgt, *, tm):
    s = x1.shape[0]

    def body(a_ref, w_ref, x1_ref, g_ref, t_ref, dx2_ref, loss_ref, dgf_ref):
        @pl.when(pl.program_id(0) == 0)
        def _():
            loss_ref[...] = jnp.zeros_like(loss_ref)
            dgf_ref[...] = jnp.zeros_like(dgf_ref)

        x2 = x1_ref[...] + jnp.dot(a_ref[...], w_ref[...], preferred_element_type=F32)
        r = lax.rsqrt(jnp.mean(x2 * x2, axis=-1, keepdims=True) + EPS)
        xn = x2 * r
        g = g_ref[...]
        diff = xn * g - t_ref[...]
        loss_ref[...] += jnp.sum(diff * diff, axis=0, keepdims=True)
        dy = diff * (1.0 / D_MODEL)
        dgf_ref[...] += jnp.sum(dy * xn, axis=0, keepdims=True)
        dxn = dy * g
        dx2_ref[...] = r * (dxn - xn * jnp.mean(dxn * xn, axis=-1, keepdims=True))

    row = lambda w: pl.BlockSpec((tm, w), lambda i: (i, 0))
    full = lambda a, b: pl.BlockSpec((a, b), lambda i: (0, 0))
    return pl.pallas_call(
        body,
        grid=(s // tm,),
        in_specs=[row(D_FF), full(D_FF, D_MODEL), row(D_MODEL), full(1, D_MODEL), row(D_MODEL)],
        out_specs=[row(D_MODEL), full(1, D_MODEL), full(1, D_MODEL)],
        out_shape=[SDS((s, D_MODEL), F32), SDS((1, D_MODEL), F32), SDS((1, D_MODEL), F32)],
        compiler_params=_cparams("arbitrary"),
        name="down_final",
    )(act, wd, x1, gf, tgt)


def _swiglu_bwd(dx2, wd, gate, up, *, tm, tn):
    s = dx2.shape[0]

    def body(d_ref, w_ref, gate_ref, up_ref, dgate_ref, dup_ref):
        dact = lax.dot_general(d_ref[...].astype(BF16), w_ref[...], NT, preferred_element_type=F32)
        gate = gate_ref[...].astype(F32)
        sg = jax.nn.sigmoid(gate)
        dup_ref[...] = (dact * (gate * sg)).astype(BF16)
        dgate_ref[...] = (dact * up_ref[...].astype(F32) * (sg * (1.0 + gate * (1.0 - sg)))).astype(BF16)

    ospec = pl.BlockSpec((tm, tn), lambda c, r: (r, c))
    return pl.pallas_call(
        body,
        grid=(D_FF // tn, s // tm),
        in_specs=[pl.BlockSpec((tm, D_MODEL), lambda c, r: (r, 0)), pl.BlockSpec((tn, D_MODEL), lambda c, r: (c, 0)),
                  ospec, ospec],
        out_specs=[ospec, ospec],
        out_shape=[SDS((s, D_FF), BF16), SDS((s, D_FF), BF16)],
        compiler_params=_cparams("arbitrary", "arbitrary"),
        name="swiglu_bwd",
    )(dx2, wd, gate, up)


def _mm_tn(a, bs, *, ta, ts, name):
    s, ka = a.shape
    n = len(bs)

    def body(a_ref, *refs):
        b_refs, o_refs = refs[:n], refs[n:]

        @pl.when(pl.program_id(1) == 0)
        def _():
            for o_ref in o_refs:
                o_ref[...] = jnp.zeros_like(o_ref)

        av = a_ref[...].astype(BF16)
        for b_ref, o_ref in zip(b_refs, o_refs):
            o_ref[...] += lax.dot_general(av, b_ref[...].astype(BF16), TN, preferred_element_type=F32)

    return pl.pallas_call(
        body,
        grid=(ka // ta, s // ts),
        in_specs=[pl.BlockSpec((ts, ta), lambda i, k: (k, i))]
        + [pl.BlockSpec((ts, b.shape[1]), lambda i, k: (k, 0)) for b in bs],
        out_specs=[pl.BlockSpec((ta, b.shape[1]), lambda i, k: (i, 0)) for b in bs],
        out_shape=[SDS((ka, b.shape[1]), F32) for b in bs],
        compiler_params=_cparams("arbitrary", "arbitrary"),
        name=name,
    )(a, *bs)


def _mm_tn_shared(as_, b, *, ts, name):
    s, nb_ = b.shape
    n = len(as_)

    def body(*refs):
        a_refs, b_ref, o_refs = refs[:n], refs[n], refs[n + 1:]

        @pl.when(pl.program_id(0) == 0)
        def _():
            for o_ref in o_refs:
                o_ref[...] = jnp.zeros_like(o_ref)

        bv = b_ref[...].astype(BF16)
        for a_ref, o_ref in zip(a_refs, o_refs):
            o_ref[...] += lax.dot_general(a_ref[...].astype(BF16), bv, TN, preferred_element_type=F32)

    return pl.pallas_call(
        body,
        grid=(s // ts,),
        in_specs=[pl.BlockSpec((ts, a.shape[1]), lambda k: (k, 0)) for a in as_] + [pl.BlockSpec((ts, nb_), lambda k: (k, 0))],
        out_specs=[pl.BlockSpec((a.shape[1], nb_), lambda k: (0, 0)) for a in as_],
        out_shape=[SDS((a.shape[1], nb_), F32) for a in as_],
        compiler_params=_cparams("arbitrary"),
        name=name,
    )(*as_, b)


def _norm_bwd(dh, x, r, g, dres):
    xn = x * r
    dxn = dh * g
    dx = dres + r * (dxn - xn * jnp.mean(dxn * xn, axis=-1, keepdims=True))
    return dx, jnp.sum(dh * xn, axis=0, keepdims=True)


def _mlp_in_bwd(dgate, dup, wg_t, wu_t, w_out, x1, r2, g2, dx2, *, tm):
    s = x1.shape[0]

    def body(dg_ref, du_ref, wg_ref, wu_ref, wo_ref, x_ref, r_ref, g_ref, d_ref, dx1_ref, dmix_ref, dg2_ref):
        @pl.when(pl.program_id(0) == 0)
        def _():
            dg2_ref[...] = jnp.zeros_like(dg2_ref)

        dh2 = (jnp.dot(dg_ref[...], wg_ref[...], preferred_element_type=F32)
               + jnp.dot(du_ref[...], wu_ref[...], preferred_element_type=F32))
        dx1, dg2 = _norm_bwd(dh2, x_ref[...], r_ref[...], g_ref[...], d_ref[...])
        dg2_ref[...] += dg2
        dx1_ref[...] = dx1
        dmix_ref[...] = lax.dot_general(dx1.astype(BF16), wo_ref[...], NT, preferred_element_type=F32)

    row = lambda w: pl.BlockSpec((tm, w), lambda i: (i, 0))
    full = lambda a, b: pl.BlockSpec((a, b), lambda i: (0, 0))
    return pl.pallas_call(
        body,
        grid=(s // tm,),
        in_specs=[row(D_FF), row(D_FF), full(D_FF, D_MODEL), full(D_FF, D_MODEL), full(D_MODEL, D_MODEL),
                  row(D_MODEL), row(1), full(1, D_MODEL), row(D_MODEL)],
        out_specs=[row(D_MODEL), row(D_MODEL), full(1, D_MODEL)],
        out_shape=[SDS((s, D_MODEL), F32), SDS((s, D_MODEL), F32), SDS((1, D_MODEL), F32)],
        compiler_params=_cparams("arbitrary"),
        name="mlp_in_bwd",
    )(dgate, dup, wg_t, wu_t, w_out, x1, r2, g2, dx2)


def _pool_bwd(dmixed, pooled, w_pool, pool_scale, *, tm):
    s = pooled.shape[0]
    nt = s // tm
    ng = len(POOL_WINDOWS)

    def body(d_ref, p_ref, w_ref, sc_ref, du_ref, dw_ref, dsc_ref, head_ref):
        i = pl.program_id(0)

        @pl.when(i == 0)
        def _():
            head_ref[...] = jnp.zeros_like(head_ref)
            dw_ref[...] = jnp.zeros_like(dw_ref)
            dsc_ref[...] = jnp.zeros_like(dsc_ref)

        row0 = (nt - 1 - i) * tm
        for g, w in enumerate(POOL_WINDOWS):
            cols = slice(g * POOL_G, (g + 1) * POOL_G)
            wb = w_ref[g].astype(BF16)
            pooled_g = p_ref[:, cols]
            dpo = d_ref[:, cols]
            mixed = jnp.dot(pooled_g, wb, preferred_element_type=F32)
            dsc_ref[:, cols] += jnp.sum(dpo * mixed, axis=0, keepdims=True)
            dmp = (dpo * sc_ref[:, cols]).astype(BF16)
            dw_ref[g] += lax.dot_general(pooled_g, dmp, TN, preferred_element_type=F32)
            dpooled = lax.dot_general(dmp, wb, NT, preferred_element_type=F32)
            a = dpooled / _pool_counts(row0, tm, w)
            acc = jnp.concatenate([a, head_ref[:, cols]], axis=0)
            head_ref[:, cols] = a[0:HALO, :]
            k = 1
            while k < w:
                acc = acc + pltpu.roll(acc, tm + HALO - k, axis=0)
                k *= 2
            du_ref[:, cols] = (acc[0:tm, :] - dpooled).astype(BF16)

    rev = lambda i: (nt - 1 - i, 0)
    return pl.pallas_call(
        body,
        grid=(nt,),
        in_specs=[pl.BlockSpec((tm, POOL_W), lambda i: (nt - 1 - i, 1)), pl.BlockSpec((tm, POOL_W), rev),
                  pl.BlockSpec((ng, POOL_G, POOL_G), lambda i: (0, 0, 0)), pl.BlockSpec((1, POOL_W), lambda i: (0, 0))],
        out_specs=[pl.BlockSpec((tm, POOL_W), rev), pl.BlockSpec((ng, POOL_G, POOL_G), lambda i: (0, 0, 0)),
                   pl.BlockSpec((1, POOL_W), lambda i: (0, 0))],
        out_shape=[SDS((s, POOL_W), BF16), SDS((ng, POOL_G, POOL_G), F32), SDS((1, POOL_W), F32)],
        scratch_shapes=[pltpu.VMEM((HALO, POOL_W), F32)],
        compiler_params=_cparams("arbitrary"),
        name="pool_bwd",
    )(dmixed, pooled, w_pool, pool_scale)


SUM_ROWS = 16


def _heads_t(t):
    n = t.shape[0]
    lane = _iota2((n, LANES), 1)
    tf = t.astype(F32)
    halves = jnp.concatenate([jnp.where(lane < HEAD_DIM, tf, 0.0).T, jnp.where(lane < HEAD_DIM, 0.0, tf).T], axis=1)
    r, c = _iota2((SUM_ROWS, 2 * n), 0), _iota2((SUM_ROWS, 2 * n), 1)
    ones = jnp.where(((r == 0) & (c < n)) | ((r == 4) & (c >= n)), 1.0, 0.0)
    return jnp.concatenate([halves, ones], axis=0).astype(BF16)


def _attn_bwd(qkv, attn_o, dmixed, rowb, ck_col, *, tq):
    s = qkv.shape[0]
    tk = tq
    nb = s // tq
    rows_t = LANES + SUM_ROWS

    def body(q_ref, k_ref, v_ref, o_ref, do_ref, rowb_ref, ck_ref, dq_ref, dk_ref, dv_ref, dck_ref, dcq_ref,
             dqt_ref, delta_ref, kp_ref, qp_ref, dob_ref, qt_ref, kt_ref, dot_ref, front_ref):
        lane = _iota2((tq, LANES), 1)
        lo = lane < HEAD_DIM
        first = _iota2((8, LANES), 1) < HEAD_DIM
        sel = jnp.where(_iota2((8, LANES), 0) < 4, jnp.where(first, 1.0, 0.0), jnp.where(first, 0.0, 1.0))

        def prep(b, _):
            st = pl.multiple_of(b * tq, tq)
            do2 = do_ref[pl.ds(st, tq), :]
            delta_ref[b] = _sel_dot(sel, do2 * o_ref[pl.ds(st, tq), :].astype(F32), NT)
            dob_ref[pl.ds(st, tq), :] = do2.astype(BF16)
            dqt_ref[b] = jnp.zeros((rows_t, tq), F32)
            k2 = k_ref[pl.ds(st, tq), :].astype(F32)
            q2 = q_ref[pl.ds(st, tq), :].astype(F32)
            ck = ck_ref[pl.ds(st, tq), :]
            for h in range(2):
                kp_ref[h * nb + b] = _augment(k2, h, -ck[:, h:h + 1], True)
                qp_ref[h * nb + b] = _augment(q2 * 0.125, h, jnp.zeros((tq, 1), F32), False)
            qt_ref[b] = _heads_t(q2)
            kt_ref[b] = _heads_t(k2)
            dot_ref[b] = _heads_t(do2)[0:LANES, :]
            return 0

        lax.fori_loop(0, nb, prep, 0)

        def split(t):
            z = jnp.zeros_like(t)
            return jnp.where(lo, t, z), jnp.where(lo, z, t)

        def kv_block(j, _):
            st_j = pl.multiple_of(j * tk, tk)
            vs = split(v_ref[pl.ds(st_j, tk), :])
            kt = kt_ref[j]

            def stage(i, slot):
                ic = jnp.minimum(i, nb - 1)
                do2 = dob_ref[pl.ds(pl.multiple_of(ic * tq, tq), tq), :]
                for h in range(2):
                    front_ref[4 * slot + h] = lax.dot_general(kp_ref[h * nb + j], qp_ref[h * nb + ic], NT,
                                                              preferred_element_type=F32)
                    front_ref[4 * slot + 2 + h] = lax.dot_general(vs[h], do2, NT, preferred_element_type=F32)

            def q_block(i, slot, carry, diagonal):
                dkt, dvt = carry
                ic = jnp.minimum(i, nb - 1)
                rb = rowb_ref[ic] + jnp.where(i < nb, 0.0, NEG)
                dl = delta_ref[ic]
                pts, dsts = [], []
                for h in range(2):
                    st = front_ref[4 * slot + h] + rb[h:h + 1, :]
                    if diagonal:
                        st = jnp.where(_iota2((tk, tq), 0) <= _iota2((tk, tq), 1), st, NEG)
                    pt = jnp.exp(st)
                    pts.append(pt.astype(BF16))
                    dsts.append((pt * (front_ref[4 * slot + 2 + h] - dl[4 * h:4 * h + 1, :])).astype(BF16))
                dvt = dvt + lax.dot_general(dot_ref[ic], jnp.concatenate(pts, axis=1), NT, preferred_element_type=F32)
                dkt = dkt + lax.dot_general(qt_ref[ic], jnp.concatenate(dsts, axis=1), NT, preferred_element_type=F32)
                dqt_ref[ic] += jnp.dot(kt, jnp.concatenate(dsts, axis=0), preferred_element_type=F32)
                return dkt, dvt

            def pair(t, carry):
                i0 = j + 1 + 2 * t
                stage(i0 + 1, 0)
                carry = q_block(i0, 1, carry, False)
                stage(i0 + 2, 1)
                return q_block(i0 + 1, 0, carry, False)

            stage(j, 0)
            stage(j + 1, 1)
            carry = q_block(j, 0, (jnp.zeros((rows_t, tk), F32), jnp.zeros((LANES, tk), F32)), True)
            dkt, dvt = lax.fori_loop(0, lax.shift_right_logical(nb - j, 1), pair, carry)
            dk_ref[pl.ds(st_j, tk), :] = (dkt[0:LANES, :].T * 0.125).astype(BF16)
            dv_ref[pl.ds(st_j, tk), :] = dvt.T.astype(BF16)
            dck_ref[j] = dkt[LANES:LANES + 8, :]
            return 0

        lax.fori_loop(0, nb, kv_block, 0)

        def finish(b, _):
            acc = dqt_ref[b]
            dq_ref[pl.ds(pl.multiple_of(b * tq, tq), tq), :] = (acc[0:LANES, :].T * 0.125).astype(BF16)
            dcq_ref[b] = acc[LANES:LANES + 8, :]
            return 0

        lax.fori_loop(0, nb, finish, 0)

    col = lambda off: pl.BlockSpec((s, LANES), lambda p: (0, off + p))
    sums = pl.BlockSpec((None, nb, 8, tq), lambda p: (p, 0, 0, 0))
    return pl.pallas_call(
        body,
        grid=(N_PAIRS,),
        in_specs=[col(0), col(N_PAIRS), col(2 * N_PAIRS), col(0), col(0),
                  pl.BlockSpec((None, nb, 2, tq), lambda p: (p, 0, 0, 0)),
                  pl.BlockSpec((None, s, 2), lambda p: (p, 0, 0))],
        out_specs=[col(0), col(0), col(0), sums, sums],
        out_shape=[SDS((s, ATTN_W), BF16), SDS((s, ATTN_W), BF16), SDS((s, ATTN_W), BF16),
                   SDS((N_PAIRS, nb, 8, tq), F32), SDS((N_PAIRS, nb, 8, tq), F32)],
        scratch_shapes=[pltpu.VMEM((nb, rows_t, tq), F32), pltpu.VMEM((nb, 8, tq), F32),
                        pltpu.VMEM((2 * nb, tk, LANES), BF16), pltpu.VMEM((2 * nb, tq, LANES), BF16),
                        pltpu.VMEM((s, LANES), BF16), pltpu.VMEM((nb, rows_t, 2 * tq), BF16),
                        pltpu.VMEM((nb, rows_t, 2 * tk), BF16), pltpu.VMEM((nb, LANES, 2 * tq), BF16),
                        pltpu.VMEM((8, tk, tq), F32)],
        compiler_params=_cparams("arbitrary"),
        name="attn_bwd",
    )(qkv, qkv, qkv, attn_o, dmixed, rowb, ck_col)


def _forget_bwd(dc_t, fl_t, b_rows):
    rows = fl_t.shape[0]
    nb = rows // N_HEADS

    def body(dc_ref, fl_ref, b_ref, dfl_ref, db_ref):
        dc = dc_ref[...]
        lower = _iota2((LANES, LANES), 0) >= _iota2((LANES, LANES), 1)
        ones = jnp.ones((LANES, LANES), F32)
        rr, cc, same = _head_block_masks(rows, nb)
        dlf = _dot_sel(dc, lower) + _sel_dot(same & (cc > rr), _dot_sel(dc, ones))
        dfl = dlf / (1.0 + jnp.exp(fl_ref[...] + b_ref[...]))
        dfl_ref[...] = dfl
        shift = nb.bit_length() - 1
        hsel = lax.shift_right_logical(_iota2((N_HEADS, rows), 1), shift) == _iota2((N_HEADS, rows), 0)
        db_ref[...] = _sel_dot(hsel, _dot_sel(dfl, ones))

    return pl.pallas_call(body, out_shape=[SDS(fl_t.shape, F32), SDS((N_HEADS, LANES), F32)],
                          compiler_params=_cparams(), name="forget_bwd")(dc_t, fl_t, b_rows)


def _in_bwd(dq, dk, dv, du, dfl, w_in_t, x, r1, g1, dx1, *, tm):
    s = x.shape[0]
    pieces = ((0, ATTN_W), (ATTN_W, 2 * ATTN_W), (2 * ATTN_W, QKV_W), (U_OFF, F_OFF), (F_OFF, IN_PAD))

    def body(dq_ref, dk_ref, dv_ref, du_ref, df_ref, w_ref, x_ref, r_ref, g_ref, d_ref, dx_ref, dg1_ref):
        @pl.when(pl.program_id(0) == 0)
        def _():
            dg1_ref[...] = jnp.zeros_like(dg1_ref)

        dh = None
        for ref, (c0, c1) in zip((dq_ref, dk_ref, dv_ref, du_ref, df_ref), pieces):
            t = jnp.dot(ref[...], w_ref[c0:c1, :], preferred_element_type=F32)
            dh = t if dh is None else dh + t
        dx, dg1 = _norm_bwd(dh, x_ref[...], r_ref[...], g_ref[...], d_ref[...])
        dx_ref[...] = dx
        dg1_ref[...] += dg1

    row = lambda w: pl.BlockSpec((tm, w), lambda i: (i, 0))
    full = lambda a, b: pl.BlockSpec((a, b), lambda i: (0, 0))
    return pl.pallas_call(
        body,
        grid=(s // tm,),
        in_specs=[row(ATTN_W), row(ATTN_W), row(ATTN_W), row(POOL_W), row(LANES), full(IN_PAD, D_MODEL),
                  row(D_MODEL), row(1), full(1, D_MODEL), row(D_MODEL)],
        out_specs=[row(D_MODEL), full(1, D_MODEL)],
        out_shape=[SDS((s, D_MODEL), F32), SDS((1, D_MODEL), F32)],
        compiler_params=_cparams("arbitrary"),
        name="in_bwd",
    )(dq, dk, dv, du, dfl, w_in_t, x, r1, g1, dx1)


def _tiles(s):
    big = min(512, s)
    return dict(row=big, attn=min(256, s // 2), mlp_bwd=min(256, s))


def _tie(a, token):
    return a + token[0:1, 0:1].astype(a.dtype)


def _local_step(x, tgt, p, weight, emit):
    s = x.shape[0]
    t = _tiles(s)
    tm, tq = t["row"], t["attn"]
    nb = s // LANES
    nqb = s // tq
    g1, g2, gf = p["norm1_g"], p["norm2_g"], p["final_g"].reshape(1, D_MODEL)
    w_pool, pool_scale = p["w_pool"][0], p["pool_scale"]

    w_in_t = weight("w_in", x)
    h, r1, qkv, u, fl = _norm_proj(x, g1, w_in_t, tm=tm)
    fl_t = fl[:, :N_HEADS].T.reshape(N_HEADS * nb, LANES)
    b_rows = jnp.repeat(p["b_forget"].reshape(N_HEADS), nb).reshape(N_HEADS * nb, 1)
    c = _forget_cumsum(fl_t, b_rows).reshape(N_PAIRS, 2, s)
    c_col = c.transpose(0, 2, 1)
    c_rowblk = c.reshape(N_PAIRS, 2, nqb, tq).transpose(0, 2, 1, 3)
    attn_o, lse = _attn_fwd(qkv, c_col, tq=tq)
    pooled, pool_o = _pool_fwd(u, w_pool, pool_scale, tm=tm)
    w_out = weight("w_out", attn_o)
    x1, h2, r2 = _out_norm2(attn_o, pool_o, w_out, x, g2, tm=tm)
    wg_t, wu_t = weight("w_gate_up", h2)
    gate, up, act = _gate_up(h2, wg_t, wu_t, tm=tm, tn=D_FF // 2)
    wd = weight("w_down", act)
    dx2, loss_row, d_gf = _down_final(act, wd, x1, gf, tgt, tm=tm)

    dgate, dup = _swiglu_bwd(dx2, wd, gate, up, tm=tm, tn=D_FF // 2)
    (d_wd,) = _mm_tn(act, [dx2], ta=D_FF // 2, ts=tm, name="grad_w_down")
    token = emit("w_down", d_wd)
    (d_wg_t,) = _mm_tn(dgate, [h2], ta=D_FF // 2, ts=tm, name="grad_w_gate")
    (d_wu_t,) = _mm_tn(dup, [h2], ta=D_FF // 2, ts=tm, name="grad_w_up")
    token = token + emit("w_gate_up", (d_wg_t, d_wu_t))
    dx1, dmixed, d_g2 = _mlp_in_bwd(dgate, dup, wg_t, wu_t, w_out, x1, r2, _tie(g2, token), dx2, tm=t["mlp_bwd"])
    du, d_wpool, d_pscale = _pool_bwd(dmixed, pooled, w_pool, pool_scale, tm=tm)
    (d_wo_a,) = _mm_tn(attn_o, [dx1], ta=ATTN_W, ts=tm, name="grad_w_out_attn")
    (d_wo_p,) = _mm_tn(pool_o, [dx1], ta=POOL_W, ts=tm, name="grad_w_out_pool")
    token = emit("w_out", jnp.concatenate([d_wo_a, d_wo_p], axis=0))
    rowb = _tie(c_rowblk - lse, token)
    dq, dk, dv, dck, dcq = _attn_bwd(qkv, attn_o, dmixed, rowb, c_col, tq=tq)
    dc_t = (dcq - dck)[:, :, 0::4, :].transpose(0, 2, 1, 3).reshape(N_HEADS * nb, LANES)
    dfl_t, db = _forget_bwd(dc_t, fl_t, b_rows)
    dfl = jnp.pad(dfl_t.reshape(N_HEADS, s).T, ((0, 0), (0, LANES - N_HEADS))).astype(BF16)
    d_wq, d_wk, d_wv, d_wu_in, d_wf = _mm_tn_shared([dq, dk, dv, du, dfl], h, ts=tm, name="grad_w_in")
    token = emit("w_in", jnp.concatenate([d_wq, d_wk, d_wv, d_wf[:N_HEADS], d_wu_in], axis=0))
    dx, d_g1 = _in_bwd(dq, dk, dv, du, dfl, w_in_t, x, r1, _tie(g1, token), dx1, tm=tm)

    small = dict(norm1_g=d_g1, b_forget=db[:, 0].reshape(1, N_HEADS), w_pool=d_wpool, pool_scale=d_pscale,
                 norm2_g=d_g2, final_g=d_gf)
    return loss_row, dx, small


def _my_index():
    return 4 * lax.axis_index("x") + 2 * lax.axis_index("y") + lax.axis_index("c")


def _peer(k):
    pos = [lax.axis_index(a) for a in ("x", "y", "c")]
    flipped = tuple(1 - p if (k >> b) & 1 else p for p, b in zip(pos, (2, 1, 0)))
    return flipped, 4 * flipped[0] + 2 * flipped[1] + flipped[2]


_HBM = pl.BlockSpec(memory_space=pltpu.HBM)
_SEM = pl.BlockSpec(memory_space=pltpu.SEMAPHORE)
_DATAFLOW = pltpu.SideEffectType.DATAFLOW_SIDE_EFFECTING


def _peer_copies(ins, lands, send_sems, recv_sems, scatter, arrivals):
    me = _my_index()
    copies = []
    for w in range(len(ins)):
        for k in range(1, N_DEV):
            dev, idx = _peer(k)
            copies.append(pltpu.make_async_remote_copy(
                src_ref=ins[w].at[idx] if scatter[w] else ins[w], dst_ref=lands[w].at[idx if arrivals else me],
                send_sem=send_sems[w].at[k - 1], recv_sem=recv_sems[w].at[k - 1], device_id=dev, device_id_type=MESH))
    return copies


def _exchange_start(arrays, scatter, name):
    n = len(arrays)
    land_shapes = [(N_DEV,) + tuple(a.shape[1:] if sc else a.shape) for a, sc in zip(arrays, scatter)]

    def body(*refs):
        ins, lands = refs[:n], refs[n:2 * n]
        send_sems, recv_sems = refs[2 * n:3 * n], refs[3 * n:4 * n]
        token = refs[6 * n]
        for cp in _peer_copies(ins, lands, send_sems, recv_sems, scatter, False):
            cp.start()
        token[...] = jnp.zeros_like(token)

    sem = pltpu.SemaphoreType.DMA((N_DEV - 1,))
    outs = pl.pallas_call(
        body,
        in_specs=[_HBM] * (2 * n),
        out_specs=[_SEM] * (2 * n) + [_HBM] * (2 * n) + [pl.BlockSpec(memory_space=pltpu.VMEM)],
        out_shape=[sem] * (2 * n) + [pltpu.HBM(a.shape, a.dtype) for a in arrays]
        + [pltpu.HBM(sh, a.dtype) for sh, a in zip(land_shapes, arrays)] + [SDS((8, LANES), F32)],
        input_output_aliases={i: 2 * n + i for i in range(2 * n)},
        compiler_params=pltpu.CompilerParams(has_side_effects=_DATAFLOW),
        name=name,
    )(*[pltpu.with_memory_space_constraint(a, pltpu.HBM) for a in arrays],
      *[pltpu.with_memory_space_constraint(lax.empty(sh, a.dtype), pltpu.HBM) for sh, a in zip(land_shapes, arrays)])
    handles = [dict(send=outs[w], recv=outs[n + w], src=outs[2 * n + w], land=outs[3 * n + w], scatter=scatter[w])
               for w in range(n)]
    return handles, outs[4 * n]


def _exchange_wait(handles, after, name):
    n = len(handles)
    scatter = [h["scatter"] for h in handles]

    def body(*refs):
        ins, lands = refs[:n], refs[n:2 * n]
        send_sems, recv_sems = refs[2 * n:3 * n], refs[3 * n:4 * n]
        for cp in _peer_copies(ins, lands, send_sems, recv_sems, scatter, False):
            cp.wait_send()
        for cp in _peer_copies(ins, lands, send_sems, recv_sems, scatter, True):
            cp.wait_recv()

    srcs, lands = [h["src"] for h in handles], [h["land"] for h in handles]
    outs = pl.pallas_call(
        body,
        in_specs=[_HBM] * (2 * n) + [_SEM] * (2 * n) + [pl.BlockSpec(memory_space=pl.ANY)],
        out_specs=[_HBM] * (2 * n),
        out_shape=[pltpu.HBM(a.shape, a.dtype) for a in srcs + lands],
        input_output_aliases={i: i for i in range(2 * n)},
        compiler_params=pltpu.CompilerParams(has_side_effects=_DATAFLOW),
        name=name,
    )(*srcs, *lands, *[h["send"] for h in handles], *[h["recv"] for h in handles], after)
    me = _my_index()
    full = []
    for src, land, sc in zip(outs[:n], outs[n:], scatter):
        own = lax.dynamic_index_in_dim(src, me, 0, keepdims=True) if sc else src[None]
        full.append(lax.dynamic_update_slice(land, own, (me,) + (0,) * (land.ndim - 1)))
    return full


def _adamw(parts, w, m, v, name):
    rows, cols = w.shape
    tr = rows // 4 if rows % 32 == 0 else rows

    def body(p_ref, w_ref, m_ref, v_ref, g_ref, d_ref, mo_ref, vo_ref):
        g = p_ref[0].astype(F32)
        for d in range(1, N_DEV):
            g = g + p_ref[d].astype(F32)
        g_ref[...] = g
        d_ref[...], mo_ref[...], vo_ref[...] = _adam_update(g, w_ref[...], m_ref[...], v_ref[...])

    blk = pl.BlockSpec((tr, cols), lambda i: (i, 0))
    return pl.pallas_call(
        body,
        grid=(rows // tr,),
        in_specs=[pl.BlockSpec((N_DEV, tr, cols), lambda i: (0, i, 0)), blk, blk, blk],
        out_specs=[blk] * 4,
        out_shape=[SDS((rows, cols), F32)] * 4,
        compiler_params=_cparams("arbitrary"),
        name=name,
    )(parts, w, m, v)


_ROW_OF = dict(norm1_g=(0, D_MODEL), norm2_g=(1, D_MODEL), final_g=(2, D_MODEL), pool_scale=(3, POOL_W),
               b_forget=(4, N_HEADS), loss=(5, 1))


def _pack_rows(vals):
    rows = [jnp.pad(vals[n].reshape(1, width).astype(F32), ((0, 0), (0, D_MODEL - width)))
            for n, (_, width) in sorted(_ROW_OF.items(), key=lambda kv: kv[1][0])]
    return jnp.concatenate(rows + [jnp.zeros((8 - len(rows), D_MODEL), F32)], axis=0)


def _adam_update(g, w, m, v):
    m_new = ADAM_B1 * m + (1.0 - ADAM_B1) * g
    v_new = ADAM_B2 * v + (1.0 - ADAM_B2) * (g * g)
    m_hat = m_new / (1.0 - ADAM_B1 ** ADAM_STEP)
    v_hat = v_new / (1.0 - ADAM_B2 ** ADAM_STEP)
    return -ADAM_LR * (m_hat / (jnp.sqrt(v_hat) + ADAM_EPS) + ADAM_WD * w), m_new, v_new


def _adamw_replicated(parts_rows, parts_pool, w, m, v):
    names = ("norm1_g", "norm2_g", "final_g", "pool_scale", "b_forget", "w_pool")
    shapes = {n: ((len(POOL_WINDOWS), POOL_G, POOL_G) if n == "w_pool" else (1, _ROW_OF[n][1])) for n in names}

    def body(rows_ref, pool_ref, *refs):
        ins, outs = refs[:3 * len(names)], refs[3 * len(names):]

        def total(n):
            if n == "w_pool":
                pieces = [pool_ref[d] for d in range(N_DEV)]
            else:
                row, width = _ROW_OF[n]
                pieces = [rows_ref[d, row:row + 1, 0:width] for d in range(N_DEV)]
            g = pieces[0]
            for p in pieces[1:]:
                g = g + p
            return g

        outs[0][...] = total("loss")
        for k, n in enumerate(names):
            g = total(n)
            delta, m_new, v_new = _adam_update(g, ins[3 * k][...], ins[3 * k + 1][...], ins[3 * k + 2][...])
            for o_ref, val in zip(outs[1 + 4 * k:5 + 4 * k], (g, delta, m_new, v_new)):
                o_ref[...] = val

    args = [d[n].reshape(shapes[n]) for n in names for d in (w, m, v)]
    res = pl.pallas_call(
        body,
        out_shape=[SDS((1, 1), F32)] + [SDS(shapes[n], F32) for n in names for _ in range(4)],
        compiler_params=_cparams(),
        name="adamw_replicated",
    )(parts_rows, parts_pool, *args)
    return res[0], {n: [r.reshape(w[n].shape) for r in res[1 + 4 * k:5 + 4 * k]] for k, n in enumerate(names)}


def kernel(x, norm1_g, w_in, b_forget, w_pool, pool_scale, w_out, norm2_g, w_gate, w_up, w_down, final_g, loss_target, m_norm1_g, m_w_in, m_b_forget, m_w_pool, m_pool_scale, m_w_out, m_norm2_g, m_w_gate, m_w_up, m_w_down, m_final_g, v_norm1_g, v_w_in, v_b_forget, v_w_pool, v_pool_scale, v_w_out, v_norm2_g, v_w_gate, v_w_up, v_w_down, v_final_g):
    big = ("w_in", "w_out", "w_gate", "w_up", "w_down")
    order = ("norm1_g", "w_in", "b_forget", "w_pool", "pool_scale", "w_out", "norm2_g", "w_gate", "w_up", "w_down",
             "final_g")
    w = dict(norm1_g=norm1_g, w_in=w_in, b_forget=b_forget, w_pool=w_pool, pool_scale=pool_scale, w_out=w_out,
             norm2_g=norm2_g, w_gate=w_gate, w_up=w_up, w_down=w_down, final_g=final_g)
    m = dict(norm1_g=m_norm1_g, w_in=m_w_in, b_forget=m_b_forget, w_pool=m_w_pool, pool_scale=m_pool_scale,
             w_out=m_w_out, norm2_g=m_norm2_g, w_gate=m_w_gate, w_up=m_w_up, w_down=m_w_down, final_g=m_final_g)
    v = dict(norm1_g=v_norm1_g, w_in=v_w_in, b_forget=v_b_forget, w_pool=v_w_pool, pool_scale=v_pool_scale,
             w_out=v_w_out, norm2_g=v_norm2_g, w_gate=v_w_gate, w_up=v_w_up, w_down=v_w_down, final_g=v_final_g)

    flipped = ("w_in", "w_gate", "w_up")
    shard = lambda d, n: d[n][0].T if n in flipped else d[n][0]
    gather, _ = _exchange_start([shard(w, n).astype(BF16) for n in big], [False] * len(big), "gather_start")
    gather = dict(zip(big, gather))

    def gathered(names, after):
        return _exchange_wait([gather[n] for n in names], after, "gather_wait_" + names[0])

    def weight(name, after):
        if name == "w_in":
            full = gathered(["w_in"], after)[0].reshape(IN_W, D_MODEL)
            f0 = QKV_W + N_HEADS
            return jnp.concatenate([full[:QKV_W], full[f0:], full[QKV_W:f0],
                                    jnp.zeros((IN_PAD - IN_W, D_MODEL), BF16)], axis=0)
        if name == "w_out":
            return gathered(["w_out"], after)[0].reshape(D_MODEL, D_MODEL)
        if name == "w_gate_up":
            return [g.reshape(D_FF, D_MODEL) for g in gathered(["w_gate", "w_up"], after)]
        return gathered(["w_down"], after)[0].reshape(D_FF, D_MODEL)

    rows = lambda g: g.reshape(N_DEV, g.shape[0] // N_DEV, g.shape[1])
    sent = {}

    def emit(name, grad):
        if name == "w_gate_up":
            names, slots = ["w_gate", "w_up"], [rows(g) for g in grad]
        else:
            names, slots = [name], [rows(grad).astype(BF16) if name == "w_in" else rows(grad)]
        handles, token = _exchange_start(slots, [True] * len(slots), "grads_start_" + name)
        sent.update(zip(names, handles))
        return token

    loss_row, dx, small_grads = _local_step(x[0], loss_target[0], w, weight, emit)

    packed = _pack_rows(dict(small_grads, loss=0.5 / D_MODEL * jnp.sum(loss_row)))
    small_handles, _ = _exchange_start([packed, small_grads["w_pool"]], [False, False], "grads_start_replicated")

    outs = {}
    after = dx
    for name in ("w_down", "w_gate", "w_up", "w_out", "w_in"):
        (parts,) = _exchange_wait([sent[name]], after, "grads_wait_" + name)
        outs[name] = _adamw(parts, shard(w, name), shard(m, name), shard(v, name), "adamw_" + name)
        after = outs[name][0]
        outs[name] = [(a.T if name in flipped else a)[None] for a in outs[name]]
    parts_rows, parts_pool = _exchange_wait(small_handles, after, "grads_wait_replicated")
    loss, small = _adamw_replicated(parts_rows, parts_pool, w, m, v)
    outs.update(small)

    return (loss.reshape(()), dx[None]) + tuple(outs[n][k] for k in range(4) for n in order)
```

```python
import functools

import jax
import jax.numpy as jnp
from jax import lax
from jax.experimental import pallas as pl
from jax.experimental.pallas import tpu as pltpu

F32 = jnp.float32
BF16 = jnp.bfloat16
SDS = jax.ShapeDtypeStruct

D_MODEL = 1024
ATTN_W = 512
N_HEADS = 8
HEAD_DIM = 64
N_PAIRS = N_HEADS // 2
POOL_W = 512
POOL_WINDOWS = (2, 4, 8, 16)
POOL_G = 128
HALO = 16
IN_W = 3 * ATTN_W + N_HEADS + POOL_W
QKV_W = 3 * ATTN_W
U_OFF = QKV_W
F_OFF = QKV_W + POOL_W
IN_PAD = F_OFF + 128
D_FF = 2816
EPS = 1e-6
NEG = -1e30
N_DEV = 8
LANES = 128

ADAM_LR = 0.001
ADAM_B1 = 0.9
ADAM_B2 = 0.999
ADAM_EPS = 1e-08
ADAM_WD = 0.01
ADAM_STEP = 10

VMEM_LIMIT_BYTES = 56 * 1024 * 1024
MESH = pl.DeviceIdType.MESH
NT = (((1,), (1,)), ((), ()))
TN = (((0,), (0,)), ((), ()))


def _cparams(*sem):
    return pltpu.CompilerParams(dimension_semantics=sem or None, vmem_limit_bytes=VMEM_LIMIT_BYTES)


def _split3(a):
    hi = a.astype(BF16)
    r1 = a - hi.astype(F32)
    mid = r1.astype(BF16)
    lo = (r1 - mid.astype(F32)).astype(BF16)
    return hi, mid, lo


def _dot_sel(a, sel, dims=None):
    sb = sel.astype(BF16)
    if dims is None:
        return sum(jnp.dot(p, sb, preferred_element_type=F32) for p in _split3(a))
    return sum(lax.dot_general(p, sb, dims, preferred_element_type=F32) for p in _split3(a))


def _sel_dot(sel, a, dims=None):
    sb = sel.astype(BF16)
    if dims is None:
        return sum(jnp.dot(sb, p, preferred_element_type=F32) for p in _split3(a))
    return sum(lax.dot_general(sb, p, dims, preferred_element_type=F32) for p in _split3(a))


def _iota2(shape, dim):
    return lax.broadcasted_iota(jnp.int32, shape, dim)


def _norm1(x, g1, *, tm):
    s = x.shape[0]

    def body(x_ref, g_ref, h_ref, r_ref):
        xv = x_ref[...]
        r = lax.rsqrt(jnp.mean(xv * xv, axis=-1, keepdims=True) + EPS)
        h_ref[...] = (xv * r * g_ref[...]).astype(BF16)
        r_ref[...] = r

    row = lambda w: pl.BlockSpec((tm, w), lambda i: (i, 0))
    return pl.pallas_call(
        body,
        grid=(s // tm,),
        in_specs=[row(D_MODEL), pl.BlockSpec((1, D_MODEL), lambda i: (0, 0))],
        out_specs=[row(D_MODEL), row(1)],
        out_shape=[SDS((s, D_MODEL), BF16), SDS((s, 1), F32)],
        compiler_params=_cparams("arbitrary"),
        name="norm1",
    )(x, g1)


def _in_proj(h, w_in_t, *, tm):
    s = h.shape[0]

    def body(h_ref, w_ref, qkv_ref, u_ref, fl_ref):
        h = h_ref[...]
        qkv_ref[...] = lax.dot_general(h, w_ref[0:QKV_W, :], NT, preferred_element_type=F32).astype(BF16)
        u_ref[...] = lax.dot_general(h, w_ref[U_OFF:F_OFF, :], NT, preferred_element_type=F32)
        fl_ref[...] = lax.dot_general(h, w_ref[F_OFF:IN_PAD, :], NT, preferred_element_type=F32)

    row = lambda w: pl.BlockSpec((tm, w), lambda i: (i, 0))
    return pl.pallas_call(
        body,
        grid=(s // tm,),
        in_specs=[row(D_MODEL), pl.BlockSpec((IN_PAD, D_MODEL), lambda i: (0, 0))],
        out_specs=[row(QKV_W), row(POOL_W), row(LANES)],
        out_shape=[SDS((s, QKV_W), BF16), SDS((s, POOL_W), F32), SDS((s, LANES), F32)],
        compiler_params=_cparams("arbitrary"),
        name="in_proj",
    )(h, w_in_t)


def _head_block_masks(rows, nb):
    shift = nb.bit_length() - 1
    rr, cc = _iota2((rows, rows), 0), _iota2((rows, rows), 1)
    same = lax.shift_right_logical(rr, shift) == lax.shift_right_logical(cc, shift)
    return rr, cc, same


def _forget_cumsum(fl_t, b_rows):
    rows = fl_t.shape[0]
    nb = rows // N_HEADS

    def body(fl_ref, b_ref, c_ref):
        z = fl_ref[...] + b_ref[...]
        lf = jnp.minimum(z, 0.0) - jnp.log1p(jnp.exp(-jnp.abs(z)))
        upper = _iota2((LANES, LANES), 0) <= _iota2((LANES, LANES), 1)
        within = _dot_sel(lf, upper)
        tot = _dot_sel(lf, jnp.ones((LANES, LANES), F32))
        rr, cc, same = _head_block_masks(rows, nb)
        c_ref[...] = within + _sel_dot(same & (cc < rr), tot)

    return pl.pallas_call(body, out_shape=SDS(fl_t.shape, F32), compiler_params=_cparams(), name="forget_cumsum")(
        fl_t, b_rows)


BIAS_LANES = 3


def _augment(t, h, col, col_first):
    n = t.shape[0]
    lane = _iota2((n, LANES), 1)
    own = (lane < HEAD_DIM) if h == 0 else (lane >= HEAD_DIM)
    b0 = HEAD_DIM if h == 0 else 0
    c0, o0 = (b0, b0 + BIAS_LANES) if col_first else (b0 + BIAS_LANES, b0)
    x = jnp.where(own, t, 0.0)
    for off, piece in enumerate(_split3(col)):
        x = jnp.where(lane == c0 + off, piece.astype(F32), x)
    x = jnp.where((lane >= o0) & (lane < o0 + BIAS_LANES), 1.0, x)
    return x.astype(BF16)


def _attn_fwd(qkv, c_col, *, tq):
    s = qkv.shape[0]
    tk = tq
    nb = s // tq

    def body(q_ref, k_ref, v_ref, cq_ref, ck_ref, o_ref, lse_ref, kp_ref, vt_ref, st_ref):
        i = pl.program_id(1)

        @pl.when(i == 0)
        def _():
            def prep(jb, _):
                st = pl.multiple_of(jb * tk, tk)
                k2 = k_ref[pl.ds(st, tk), :].astype(F32)
                ck = ck_ref[pl.ds(st, tk), :]
                for h in range(2):
                    kp_ref[h * nb + jb] = _augment(k2, h, -ck[:, h:h + 1], True)
                vt_ref[jb] = v_ref[pl.ds(st, tk), :].astype(F32).T.astype(BF16)
                return 0

            lax.fori_loop(0, nb, prep, 0)

        qs = q_ref[...].astype(F32) * 0.125
        cq = cq_ref[...]
        qp = [_augment(qs, h, cq[:, h:h + 1], False) for h in range(2)]

        def logits(j):
            return tuple(lax.dot_general(kp_ref[h * nb + j], qp[h], NT, preferred_element_type=F32) for h in range(2))

        def softmax_pv(j, sts, stats, masked):
            out = []
            for h in range(2):
                m, l, acc = stats[h]
                st = sts[h]
                if masked:
                    st = jnp.where(_iota2((tk, tq), 0) <= _iota2((tk, tq), 1), st, NEG)
                m_new = jnp.maximum(m, jnp.max(st, axis=0, keepdims=True))
                alpha = jnp.exp(m - m_new)
                p = jnp.exp(st - m_new)
                l = alpha * l + jnp.sum(p, axis=0, keepdims=True)
                vt = vt_ref[j, h * HEAD_DIM:(h + 1) * HEAD_DIM, :]
                acc = alpha * acc + jnp.dot(vt, p.astype(BF16), preferred_element_type=F32)
                out.append((m_new, l, acc))
            return tuple(out)

        def put(slot, sts):
            for h in range(2):
                st_ref[2 * slot + h] = sts[h]

        def get(slot):
            return tuple(st_ref[2 * slot + h] for h in range(2))

        def step(j, stats):
            nxt = logits(j + 1)
            stats = softmax_pv(j, get(j % 2), stats, False)
            put((j + 1) % 2, nxt)
            return stats

        init = tuple((jnp.full((1, tq), NEG, F32), jnp.zeros((1, tq), F32), jnp.zeros((HEAD_DIM, tq), F32))
                     for _ in range(2))
        put(0, logits(0))
        stats = lax.fori_loop(0, i, step, init)
        (ma, la, acca), (mb, lb, accb) = softmax_pv(i, get(i % 2), stats, True)
        o_ref[...] = jnp.concatenate([acca / la, accb / lb], axis=0).T.astype(BF16)
        lse_ref[...] = jnp.where(_iota2((2, tq), 0) == 0, ma + jnp.log(la), mb + jnp.log(lb))

    return pl.pallas_call(
        body,
        grid=(N_PAIRS, nb),
        in_specs=[
            pl.BlockSpec((tq, LANES), lambda p, i: (i, p)),
            pl.BlockSpec((s, LANES), lambda p, i: (0, N_PAIRS + p)),
            pl.BlockSpec((s, LANES), lambda p, i: (0, 2 * N_PAIRS + p)),
            pl.BlockSpec((None, tq, 2), lambda p, i: (p, i, 0)),
            pl.BlockSpec((None, s, 2), lambda p, i: (p, 0, 0)),
        ],
        out_specs=[
            pl.BlockSpec((tq, LANES), lambda p, i: (i, p)),
            pl.BlockSpec((None, None, 2, tq), lambda p, i: (p, i, 0, 0)),
        ],
        out_shape=[SDS((s, ATTN_W), BF16), SDS((N_PAIRS, nb, 2, tq), F32)],
        scratch_shapes=[pltpu.VMEM((2 * nb, tk, LANES), BF16), pltpu.VMEM((nb, LANES, tk), BF16),
                        pltpu.VMEM((4, tk, tq), F32)],
        compiler_params=_cparams("arbitrary", "arbitrary"),
        name="attn_fwd",
    )(qkv, qkv, qkv, c_col, c_col)


def _pool_counts(row0, tm, w):
    t = row0 + _iota2((tm, 1), 0)
    return jnp.minimum(t + 1, w).astype(F32)


def _pool_fwd(u, w_pool, pool_scale, *, tm):
    s = u.shape[0]

    def body(u_ref, w_ref, sc_ref, pooled_ref, po_ref, tail_ref):
        i = pl.program_id(0)

        @pl.when(i == 0)
        def _():
            tail_ref[...] = jnp.zeros_like(tail_ref)

        uv = u_ref[...]
        ext = jnp.concatenate([tail_ref[...], uv], axis=0)
        tail_ref[...] = uv[tm - HALO:, :]
        for g, w in enumerate(POOL_WINDOWS):
            cols = slice(g * POOL_G, (g + 1) * POOL_G)
            acc = ext[:, cols]
            k = 1
            while k < w:
                acc = acc + pltpu.roll(acc, k, axis=0)
                k *= 2
            pooled = (acc[HALO:, :] / _pool_counts(i * tm, tm, w) - uv[:, cols]).astype(BF16)
            pooled_ref[:, cols] = pooled
            mixed = jnp.dot(pooled, w_ref[g].astype(BF16), preferred_element_type=F32)
            po_ref[:, cols] = (mixed * sc_ref[:, cols]).astype(BF16)

    row = pl.BlockSpec((tm, POOL_W), lambda i: (i, 0))
    return pl.pallas_call(
        body,
        grid=(s // tm,),
        in_specs=[row, pl.BlockSpec((len(POOL_WINDOWS), POOL_G, POOL_G), lambda i: (0, 0, 0)),
                  pl.BlockSpec((1, POOL_W), lambda i: (0, 0))],
        out_specs=[row, row],
        out_shape=[SDS((s, POOL_W), BF16), SDS((s, POOL_W), BF16)],
        scratch_shapes=[pltpu.VMEM((HALO, POOL_W), F32)],
        compiler_params=_cparams("arbitrary"),
        name="pool_fwd",
    )(u, w_pool, pool_scale)


def _out_norm2(attn_o, pool_o, w_out, x, g2, *, tm):
    s = x.shape[0]

    def body(a_ref, p_ref, w_ref, x_ref, g_ref, x1_ref, h2_ref, r_ref):
        x1 = (x_ref[...] + jnp.dot(a_ref[...], w_ref[0:ATTN_W, :], preferred_element_type=F32)
              + jnp.dot(p_ref[...], w_ref[ATTN_W:, :], preferred_element_type=F32))
        r = lax.rsqrt(jnp.mean(x1 * x1, axis=-1, keepdims=True) + EPS)
        x1_ref[...] = x1
        r_ref[...] = r
        h2_ref[...] = (x1 * r * g_ref[...]).astype(BF16)

    row = lambda w: pl.BlockSpec((tm, w), lambda i: (i, 0))
    full = lambda a, b: pl.BlockSpec((a, b), lambda i: (0, 0))
    return pl.pallas_call(
        body,
        grid=(s // tm,),
        in_specs=[row(ATTN_W), row(POOL_W), full(D_MODEL, D_MODEL), row(D_MODEL), full(1, D_MODEL)],
        out_specs=[row(D_MODEL), row(D_MODEL), row(1)],
        out_shape=[SDS((s, D_MODEL), F32), SDS((s, D_MODEL), BF16), SDS((s, 1), F32)],
        compiler_params=_cparams("arbitrary"),
        name="out_norm2",
    )(attn_o, pool_o, w_out, x, g2)


def _gate_up(h2, wg_t, wu_t, *, tm, tn):
    s = h2.shape[0]

    def body(h_ref, wg_ref, wu_ref, gate_ref, up_ref, act_ref):
        h = h_ref[...]
        gate = lax.dot_general(h, wg_ref[...], NT, preferred_element_type=F32)
        up = lax.dot_general(h, wu_ref[...], NT, preferred_element_type=F32)
        gate_ref[...] = gate.astype(BF16)
        up_ref[...] = up.astype(BF16)
        act_ref[...] = (gate * jax.nn.sigmoid(gate) * up).astype(BF16)

    wspec = pl.BlockSpec((tn, D_MODEL), lambda c, r: (c, 0))
    ospec = pl.BlockSpec((tm, tn), lambda c, r: (r, c))
    return pl.pallas_call(
        body,
        grid=(D_FF // tn, s // tm),
        in_specs=[pl.BlockSpec((tm, D_MODEL), lambda c, r: (r, 0)), wspec, wspec],
        out_specs=[ospec, ospec, ospec],
        out_shape=[SDS((s, D_FF), BF16), SDS((s, D_FF), BF16), SDS((s, D_FF), BF16)],
        compiler_params=_cparams("arbitrary", "arbitrary"),
        name="gate_up",
    )(h2, wg_t, wu_t)


def _down_final(act, wd, x1, gf, tgt, *, tm):
    s = x1.shape[0]

    def body(a_ref, w_ref, x1_ref, g_ref, t_ref, dx2_ref, loss_ref, dgf_ref):
        @pl.when(pl.program_id(0) == 0)
        def _():
            loss_ref[...] = jnp.zeros_like(loss_ref)
            dgf_ref[...] = jnp.zeros_like(dgf_ref)

        x2 = x1_ref[...] + jnp.dot(a_ref[...], w_ref[...], preferred_element_type=F32)
        r = lax.rsqrt(jnp.mean(x2 * x2, axis=-1, keepdims=True) + EPS)
        xn = x2 * r
        g = g_ref[...]
        diff = xn * g - t_ref[...]
        loss_ref[...] += jnp.sum(diff * diff, axis=0, keepdims=True)
        dy = diff * (1.0 / D_MODEL)
        dgf_ref[...] += jnp.sum(dy * xn, axis=0, keepdims=True)
        dxn = dy * g
        dx2_ref[...] = r * (dxn - xn * jnp.mean(dxn * xn, axis=-1, keepdims=True))

    row = lambda w: pl.BlockSpec((tm, w), lambda i: (i, 0))
    full = lambda a, b: pl.BlockSpec((a, b), lambda i: (0, 0))
    return pl.pallas_call(
        body,
        grid=(s // tm,),
        in_specs=[row(D_FF), full(D_FF, D_MODEL), row(D_MODEL), full(1, D_MODEL), row(D_MODEL)],
        out_specs=[row(D_MODEL), full(1, D_MODEL), full(1, D_MODEL)],
        out_shape=[SDS((s, D_MODEL), F32), SDS((1, D_MODEL), F32), SDS((1, D_MODEL), F32)],
        compiler_params=_cparams("arbitrary"),
        name="down_final",
    )(act, wd, x1, gf, tgt)


def _swiglu_bwd(dx2, wd, gate, up, *, tm, tn):
    s = dx2.shape[0]

    def body(d_ref, w_ref, gate_ref, up_ref, dgate_ref, dup_ref):
        dact = lax.dot_general(d_ref[...].astype(BF16), w_ref[...], NT, preferred_element_type=F32)
        gate = gate_ref[...].astype(F32)
        sg = jax.nn.sigmoid(gate)
        dup_ref[...] = (dact * (gate * sg)).astype(BF16)
        dgate_ref[...] = (dact * up_ref[...].astype(F32) * (sg * (1.0 + gate * (1.0 - sg)))).astype(BF16)

    ospec = pl.BlockSpec((tm, tn), lambda c, r: (r, c))
    return pl.pallas_call(
        body,
        grid=(D_FF // tn, s // tm),
        in_specs=[pl.BlockSpec((tm, D_MODEL), lambda c, r: (r, 0)), pl.BlockSpec((tn, D_MODEL), lambda c, r: (c, 0)),
                  ospec, ospec],
        out_specs=[ospec, ospec],
        out_shape=[SDS((s, D_FF), BF16), SDS((s, D_FF), BF16)],
        compiler_params=_cparams("arbitrary", "arbitrary"),
        name="swiglu_bwd",
    )(dx2, wd, gate, up)


def _mm_tn(a, bs, *, ta, ts, name):
    s, ka = a.shape
    n = len(bs)

    def body(a_ref, *refs):
        b_refs, o_refs = refs[:n], refs[n:]

        @pl.when(pl.program_id(1) == 0)
        def _():
            for o_ref in o_refs:
                o_ref[...] = jnp.zeros_like(o_ref)

        av = a_ref[...].astype(BF16)
        for b_ref, o_ref in zip(b_refs, o_refs):
            o_ref[...] += lax.dot_general(av, b_ref[...].astype(BF16), TN, preferred_element_type=F32)

    return pl.pallas_call(
        body,
        grid=(ka // ta, s // ts),
        in_specs=[pl.BlockSpec((ts, ta), lambda i, k: (k, i))]
        + [pl.BlockSpec((ts, b.shape[1]), lambda i, k: (k, 0)) for b in bs],
        out_specs=[pl.BlockSpec((ta, b.shape[1]), lambda i, k: (i, 0)) for b in bs],
        out_shape=[SDS((ka, b.shape[1]), F32) for b in bs],
        compiler_params=_cparams("arbitrary", "arbitrary"),
        name=name,
    )(a, *bs)


def _mm_tn_shared(as_, b, *, ts, name):
    s, nb_ = b.shape
    n = len(as_)

    def body(*refs):
        a_refs, b_ref, o_refs = refs[:n], refs[n], refs[n + 1:]

        @pl.when(pl.program_id(0) == 0)
        def _():
            for o_ref in o_refs:
                o_ref[...] = jnp.zeros_like(o_ref)

        bv = b_ref[...].astype(BF16)
        for a_ref, o_ref in zip(a_refs, o_refs):
            o_ref[...] += lax.dot_general(a_ref[...].astype(BF16), bv, TN, preferred_element_type=F32)

    return pl.pallas_call(
        body,
        grid=(s // ts,),
        in_specs=[pl.BlockSpec((ts, a.shape[1]), lambda k: (k, 0)) for a in as_] + [pl.BlockSpec((ts, nb_), lambda k: (k, 0))],
        out_specs=[pl.BlockSpec((a.shape[1], nb_), lambda k: (0, 0)) for a in as_],
        out_shape=[SDS((a.shape[1], nb_), F32) for a in as_],
        compiler_params=_cparams("arbitrary"),
        name=name,
    )(*as_, b)


def _norm_bwd(dh, x, r, g, dres):
    xn = x * r
    dxn = dh * g
    dx = dres + r * (dxn - xn * jnp.mean(dxn * xn, axis=-1, keepdims=True))
    return dx, jnp.sum(dh * xn, axis=0, keepdims=True)


def _mlp_in_bwd(dgate, dup, wg_t, wu_t, w_out, x1, r2, g2, dx2, *, tm):
    s = x1.shape[0]

    def body(dg_ref, du_ref, wg_ref, wu_ref, wo_ref, x_ref, r_ref, g_ref, d_ref, dx1_ref, dmix_ref, dg2_ref):
        @pl.when(pl.program_id(0) == 0)
        def _():
            dg2_ref[...] = jnp.zeros_like(dg2_ref)

        dh2 = (jnp.dot(dg_ref[...], wg_ref[...], preferred_element_type=F32)
               + jnp.dot(du_ref[...], wu_ref[...], preferred_element_type=F32))
        dx1, dg2 = _norm_bwd(dh2, x_ref[...], r_ref[...], g_ref[...], d_ref[...])
        dg2_ref[...] += dg2
        dx1_ref[...] = dx1
        dmix_ref[...] = lax.dot_general(dx1.astype(BF16), wo_ref[...], NT, preferred_element_type=F32)

    row = lambda w: pl.BlockSpec((tm, w), lambda i: (i, 0))
    full = lambda a, b: pl.BlockSpec((a, b), lambda i: (0, 0))
    return pl.pallas_call(
        body,
        grid=(s // tm,),
        in_specs=[row(D_FF), row(D_FF), full(D_FF, D_MODEL), full(D_FF, D_MODEL), full(D_MODEL, D_MODEL),
                  row(D_MODEL), row(1), full(1, D_MODEL), row(D_MODEL)],
        out_specs=[row(D_MODEL), row(D_MODEL), full(1, D_MODEL)],
        out_shape=[SDS((s, D_MODEL), F32), SDS((s, D_MODEL), F32), SDS((1, D_MODEL), F32)],
        compiler_params=_cparams("arbitrary"),
        name="mlp_in_bwd",
    )(dgate, dup, wg_t, wu_t, w_out, x1, r2, g2, dx2)


def _pool_bwd(dmixed, pooled, w_pool, pool_scale, *, tm):
    s = pooled.shape[0]
    nt = s // tm
    ng = len(POOL_WINDOWS)

    def body(d_ref, p_ref, w_ref, sc_ref, du_ref, dw_ref, dsc_ref, head_ref):
        i = pl.program_id(0)

        @pl.when(i == 0)
        def _():
            head_ref[...] = jnp.zeros_like(head_ref)
            dw_ref[...] = jnp.zeros_like(dw_ref)
            dsc_ref[...] = jnp.zeros_like(dsc_ref)

        row0 = (nt - 1 - i) * tm
        for g, w in enumerate(POOL_WINDOWS):
            cols = slice(g * POOL_G, (g + 1) * POOL_G)
            wb = w_ref[g].astype(BF16)
            pooled_g = p_ref[:, cols]
            dpo = d_ref[:, cols]
            mixed = jnp.dot(pooled_g, wb, preferred_element_type=F32)
            dsc_ref[:, cols] += jnp.sum(dpo * mixed, axis=0, keepdims=True)
            dmp = (dpo * sc_ref[:, cols]).astype(BF16)
            dw_ref[g] += lax.dot_general(pooled_g, dmp, TN, preferred_element_type=F32)
            dpooled = lax.dot_general(dmp, wb, NT, preferred_element_type=F32)
            a = dpooled / _pool_counts(row0, tm, w)
            acc = jnp.concatenate([a, head_ref[:, cols]], axis=0)
            head_ref[:, cols] = a[0:HALO, :]
            k = 1
            while k < w:
                acc = acc + pltpu.roll(acc, tm + HALO - k, axis=0)
                k *= 2
            du_ref[:, cols] = (acc[0:tm, :] - dpooled).astype(BF16)

    rev = lambda i: (nt - 1 - i, 0)
    return pl.pallas_call(
        body,
        grid=(nt,),
        in_specs=[pl.BlockSpec((tm, POOL_W), lambda i: (nt - 1 - i, 1)), pl.BlockSpec((tm, POOL_W), rev),
                  pl.BlockSpec((ng, POOL_G, POOL_G), lambda i: (0, 0, 0)), pl.BlockSpec((1, POOL_W), lambda i: (0, 0))],
        out_specs=[pl.BlockSpec((tm, POOL_W), rev), pl.BlockSpec((ng, POOL_G, POOL_G), lambda i: (0, 0, 0)),
                   pl.BlockSpec((1, POOL_W), lambda i: (0, 0))],
        out_shape=[SDS((s, POOL_W), BF16), SDS((ng, POOL_G, POOL_G), F32), SDS((1, POOL_W), F32)],
        scratch_shapes=[pltpu.VMEM((HALO, POOL_W), F32)],
        compiler_params=_cparams("arbitrary"),
        name="pool_bwd",
    )(dmixed, pooled, w_pool, pool_scale)


SUM_ROWS = 16


def _heads_t(t):
    n = t.shape[0]
    lane = _iota2((n, LANES), 1)
    tf = t.astype(F32)
    halves = jnp.concatenate([jnp.where(lane < HEAD_DIM, tf, 0.0).T, jnp.where(lane < HEAD_DIM, 0.0, tf).T], axis=1)
    r, c = _iota2((SUM_ROWS, 2 * n), 0), _iota2((SUM_ROWS, 2 * n), 1)
    ones = jnp.where(((r == 0) & (c < n)) | ((r == 4) & (c >= n)), 1.0, 0.0)
    return jnp.concatenate([halves, ones], axis=0).astype(BF16)


def _attn_bwd(qkv, attn_o, dmixed, rowb, ck_col, *, tq):
    s = qkv.shape[0]
    tk = tq
    nb = s // tq
    rows_t = LANES + SUM_ROWS

    def body(q_ref, k_ref, v_ref, o_ref, do_ref, rowb_ref, ck_ref, dq_ref, dk_ref, dv_ref, dck_ref, dcq_ref,
             dqt_ref, delta_ref, kp_ref, qp_ref, dob_ref, qt_ref, kt_ref, dot_ref, front_ref):
        lane = _iota2((tq, LANES), 1)
        lo = lane < HEAD_DIM
        first = _iota2((8, LANES), 1) < HEAD_DIM
        sel = jnp.where(_iota2((8, LANES), 0) < 4, jnp.where(first, 1.0, 0.0), jnp.where(first, 0.0, 1.0))

        def prep(b, _):
            st = pl.multiple_of(b * tq, tq)
            do2 = do_ref[pl.ds(st, tq), :]
            delta_ref[b] = _sel_dot(sel, do2 * o_ref[pl.ds(st, tq), :].astype(F32), NT)
            dob_ref[pl.ds(st, tq), :] = do2.astype(BF16)
            dqt_ref[b] = jnp.zeros((rows_t, tq), F32)
            k2 = k_ref[pl.ds(st, tq), :].astype(F32)
            q2 = q_ref[pl.ds(st, tq), :].astype(F32)
            ck = ck_ref[pl.ds(st, tq), :]
            for h in range(2):
                kp_ref[h * nb + b] = _augment(k2, h, -ck[:, h:h + 1], True)
                qp_ref[h * nb + b] = _augment(q2 * 0.125, h, jnp.zeros((tq, 1), F32), False)
            qt_ref[b] = _heads_t(q2)
            kt_ref[b] = _heads_t(k2)
            dot_ref[b] = _heads_t(do2)[0:LANES, :]
            return 0

        lax.fori_loop(0, nb, prep, 0)

        def split(t):
            z = jnp.zeros_like(t)
            return jnp.where(lo, t, z), jnp.where(lo, z, t)

        def kv_block(j, _):
            st_j = pl.multiple_of(j * tk, tk)
            vs = split(v_ref[pl.ds(st_j, tk), :])
            kt = kt_ref[j]

            def stage(i, slot):
                ic = jnp.minimum(i, nb - 1)
                do2 = dob_ref[pl.ds(pl.multiple_of(ic * tq, tq), tq), :]
                for h in range(2):
                    front_ref[4 * slot + h] = lax.dot_general(kp_ref[h * nb + j], qp_ref[h * nb + ic], NT,
                                                              preferred_element_type=F32)
                    front_ref[4 * slot + 2 + h] = lax.dot_general(vs[h], do2, NT, preferred_element_type=F32)

            def q_block(i, slot, carry, diagonal):
                dkt, dvt = carry
                ic = jnp.minimum(i, nb - 1)
                rb = rowb_ref[ic] + jnp.where(i < nb, 0.0, NEG)
                dl = delta_ref[ic]
                pts, dsts = [], []
                for h in range(2):
                    st = front_ref[4 * slot + h] + rb[h:h + 1, :]
                    if diagonal:
                        st = jnp.where(_iota2((tk, tq), 0) <= _iota2((tk, tq), 1), st, NEG)
                    pt = jnp.exp(st)
                    pts.append(pt.astype(BF16))
                    dsts.append((pt * (front_ref[4 * slot + 2 + h] - dl[4 * h:4 * h + 1, :])).astype(BF16))
                dvt = dvt + lax.dot_general(dot_ref[ic], jnp.concatenate(pts, axis=1), NT, preferred_element_type=F32)
                dkt = dkt + lax.dot_general(qt_ref[ic], jnp.concatenate(dsts, axis=1), NT, preferred_element_type=F32)
                dqt_ref[ic] += jnp.dot(kt, jnp.concatenate(dsts, axis=0), preferred_element_type=F32)
                return dkt, dvt

            def pair(t, carry):
                i0 = j + 1 + 2 * t
                stage(i0 + 1, 0)
                carry = q_block(i0, 1, carry, False)
                stage(i0 + 2, 1)
                return q_block(i0 + 1, 0, carry, False)

            stage(j, 0)
            stage(j + 1, 1)
            carry = q_block(j, 0, (jnp.zeros((rows_t, tk), F32), jnp.zeros((LANES, tk), F32)), True)
            dkt, dvt = lax.fori_loop(0, lax.shift_right_logical(nb - j, 1), pair, carry)
            dk_ref[pl.ds(st_j, tk), :] = (dkt[0:LANES, :].T * 0.125).astype(BF16)
            dv_ref[pl.ds(st_j, tk), :] = dvt.T.astype(BF16)
            dck_ref[j] = dkt[LANES:LANES + 8, :]
            return 0

        lax.fori_loop(0, nb, kv_block, 0)

        def finish(b, _):
            acc = dqt_ref[b]
            dq_ref[pl.ds(pl.multiple_of(b * tq, tq), tq), :] = (acc[0:LANES, :].T * 0.125).astype(BF16)
            dcq_ref[b] = acc[LANES:LANES + 8, :]
            return 0

        lax.fori_loop(0, nb, finish, 0)

    col = lambda off: pl.BlockSpec((s, LANES), lambda p: (0, off + p))
    sums = pl.BlockSpec((None, nb, 8, tq), lambda p: (p, 0, 0, 0))
    return pl.pallas_call(
        body,
        grid=(N_PAIRS,),
        in_specs=[col(0), col(N_PAIRS), col(2 * N_PAIRS), col(0), col(0),
                  pl.BlockSpec((None, nb, 2, tq), lambda p: (p, 0, 0, 0)),
                  pl.BlockSpec((None, s, 2), lambda p: (p, 0, 0))],
        out_specs=[col(0), col(0), col(0), sums, sums],
        out_shape=[SDS((s, ATTN_W), BF16), SDS((s, ATTN_W), BF16), SDS((s, ATTN_W), BF16),
                   SDS((N_PAIRS, nb, 8, tq), F32), SDS((N_PAIRS, nb, 8, tq), F32)],
        scratch_shapes=[pltpu.VMEM((nb, rows_t, tq), F32), pltpu.VMEM((nb, 8, tq), F32),
                        pltpu.VMEM((2 * nb, tk, LANES), BF16), pltpu.VMEM((2 * nb, tq, LANES), BF16),
                        pltpu.VMEM((s, LANES), BF16), pltpu.VMEM((nb, rows_t, 2 * tq), BF16),
                        pltpu.VMEM((nb, rows_t, 2 * tk), BF16), pltpu.VMEM((nb, LANES, 2 * tq), BF16),
                        pltpu.VMEM((8, tk, tq), F32)],
        compiler_params=_cparams("arbitrary"),
        name="attn_bwd",
    )(qkv, qkv, qkv, attn_o, dmixed, rowb, ck_col)


def _forget_bwd(dc_t, fl_t, b_rows):
    rows = fl_t.shape[0]
    nb = rows // N_HEADS

    def body(dc_ref, fl_ref, b_ref, dfl_ref, db_ref):
        dc = dc_ref[...]
        lower = _iota2((LANES, LANES), 0) >= _iota2((LANES, LANES), 1)
        ones = jnp.ones((LANES, LANES), F32)
        rr, cc, same = _head_block_masks(rows, nb)
        dlf = _dot_sel(dc, lower) + _sel_dot(same & (cc > rr), _dot_sel(dc, ones))
        dfl = dlf / (1.0 + jnp.exp(fl_ref[...] + b_ref[...]))
        dfl_ref[...] = dfl
        shift = nb.bit_length() - 1
        hsel = lax.shift_right_logical(_iota2((N_HEADS, rows), 1), shift) == _iota2((N_HEADS, rows), 0)
        db_ref[...] = _sel_dot(hsel, _dot_sel(dfl, ones))

    return pl.pallas_call(body, out_shape=[SDS(fl_t.shape, F32), SDS((N_HEADS, LANES), F32)],
                          compiler_params=_cparams(), name="forget_bwd")(dc_t, fl_t, b_rows)


def _in_bwd(dq, dk, dv, du, dfl, w_in_t, x, r1, g1, dx1, *, tm):
    s = x.shape[0]
    pieces = ((0, ATTN_W), (ATTN_W, 2 * ATTN_W), (2 * ATTN_W, QKV_W), (U_OFF, F_OFF), (F_OFF, IN_PAD))

    def body(dq_ref, dk_ref, dv_ref, du_ref, df_ref, w_ref, x_ref, r_ref, g_ref, d_ref, dx_ref, dg1_ref):
        @pl.when(pl.program_id(0) == 0)
        def _():
            dg1_ref[...] = jnp.zeros_like(dg1_ref)

        dh = None
        for ref, (c0, c1) in zip((dq_ref, dk_ref, dv_ref, du_ref, df_ref), pieces):
            t = jnp.dot(ref[...], w_ref[c0:c1, :], preferred_element_type=F32)
            dh = t if dh is None else dh + t
        dx, dg1 = _norm_bwd(dh, x_ref[...], r_ref[...], g_ref[...], d_ref[...])
        dx_ref[...] = dx
        dg1_ref[...] += dg1

    row = lambda w: pl.BlockSpec((tm, w), lambda i: (i, 0))
    full = lambda a, b: pl.BlockSpec((a, b), lambda i: (0, 0))
    return pl.pallas_call(
        body,
        grid=(s // tm,),
        in_specs=[row(ATTN_W), row(ATTN_W), row(ATTN_W), row(POOL_W), row(LANES), full(IN_PAD, D_MODEL),
                  row(D_MODEL), row(1), full(1, D_MODEL), row(D_MODEL)],
        out_specs=[row(D_MODEL), full(1, D_MODEL)],
        out_shape=[SDS((s, D_MODEL), F32), SDS((1, D_MODEL), F32)],
        compiler_params=_cparams("arbitrary"),
        name="in_bwd",
    )(dq, dk, dv, du, dfl, w_in_t, x, r1, g1, dx1)


def _tiles(s):
    big = min(512, s)
    return dict(row=big, attn=min(256, s // 2), mlp_bwd=min(256, s))


def _tie(a, token):
    return a + token[0:1, 0:1].astype(a.dtype)


def _local_step(x, tgt, p, weight, emit, started):
    s = x.shape[0]
    t = _tiles(s)
    tm, tq = t["row"], t["attn"]
    nb = s // LANES
    nqb = s // tq
    g1, g2, gf = p["norm1_g"], p["norm2_g"], p["final_g"].reshape(1, D_MODEL)
    w_pool, pool_scale = p["w_pool"][0], p["pool_scale"]

    h, r1 = _norm1(x, _tie(g1, started), tm=tm)
    w_in_t = weight("w_in", h)
    qkv, u, fl = _in_proj(h, w_in_t, tm=tm)
    fl_t = fl[:, :N_HEADS].T.reshape(N_HEADS * nb, LANES)
    b_rows = jnp.repeat(p["b_forget"].reshape(N_HEADS), nb).reshape(N_HEADS * nb, 1)
    c = _forget_cumsum(fl_t, b_rows).reshape(N_PAIRS, 2, s)
    c_col = c.transpose(0, 2, 1)
    c_rowblk = c.reshape(N_PAIRS, 2, nqb, tq).transpose(0, 2, 1, 3)
    attn_o, lse = _attn_fwd(qkv, c_col, tq=tq)
    pooled, pool_o = _pool_fwd(u, w_pool, pool_scale, tm=tm)
    w_out = weight("w_out", attn_o)
    x1, h2, r2 = _out_norm2(attn_o, pool_o, w_out, x, g2, tm=tm)
    wg_t, wu_t = weight("w_gate_up", h2)
    gate, up, act = _gate_up(h2, wg_t, wu_t, tm=tm, tn=D_FF // 2)
    wd = weight("w_down", act)
    dx2, loss_row, d_gf = _down_final(act, wd, x1, gf, tgt, tm=tm)

    dgate, dup = _swiglu_bwd(dx2, wd, gate, up, tm=tm, tn=D_FF // 2)
    (d_wd,) = _mm_tn(act, [dx2], ta=D_FF // 2, ts=tm, name="grad_w_down")
    token = emit("w_down", d_wd)
    (d_wg_t,) = _mm_tn(dgate, [h2], ta=D_FF // 2, ts=tm, name="grad_w_gate")
    (d_wu_t,) = _mm_tn(dup, [h2], ta=D_FF // 2, ts=tm, name="grad_w_up")
    token = token + emit("w_gate_up", (d_wg_t, d_wu_t))
    dx1, dmixed, d_g2 = _mlp_in_bwd(dgate, dup, wg_t, wu_t, w_out, x1, r2, _tie(g2, token), dx2, tm=t["mlp_bwd"])
    du, d_wpool, d_pscale = _pool_bwd(dmixed, pooled, w_pool, pool_scale, tm=tm)
    token = emit("w_out", jnp.concatenate(_mm_tn_shared([attn_o, pool_o], dx1, ts=tm, name="grad_w_out"), axis=0))
    rowb = _tie(c_rowblk - lse, token)
    dq, dk, dv, dck, dcq = _attn_bwd(qkv, attn_o, dmixed, rowb, c_col, tq=tq)
    dc_t = (dcq - dck)[:, :, 0::4, :].transpose(0, 2, 1, 3).reshape(N_HEADS * nb, LANES)
    dfl_t, db = _forget_bwd(dc_t, fl_t, b_rows)
    dfl = jnp.pad(dfl_t.reshape(N_HEADS, s).T, ((0, 0), (0, LANES - N_HEADS))).astype(BF16)
    d_wq, d_wk, d_wv, d_wu_in, d_wf = _mm_tn_shared([dq, dk, dv, du, dfl], h, ts=tm, name="grad_w_in")
    token = emit("w_in", jnp.concatenate([d_wq, d_wk, d_wv, d_wf[:N_HEADS], d_wu_in], axis=0))
    dx, d_g1 = _in_bwd(dq, dk, dv, du, dfl, w_in_t, x, r1, _tie(g1, token), dx1, tm=tm)

    small = dict(norm1_g=d_g1, b_forget=db[:, 0].reshape(1, N_HEADS), w_pool=d_wpool, pool_scale=d_pscale,
                 norm2_g=d_g2, final_g=d_gf)
    return loss_row, dx, small


def _my_index():
    return 4 * lax.axis_index("x") + 2 * lax.axis_index("y") + lax.axis_index("c")


def _peer(k):
    pos = [lax.axis_index(a) for a in ("x", "y", "c")]
    flipped = tuple(1 - p if (k >> b) & 1 else p for p, b in zip(pos, (2, 1, 0)))
    return flipped, 4 * flipped[0] + 2 * flipped[1] + flipped[2]


_HBM = pl.BlockSpec(memory_space=pltpu.HBM)
_SEM = pl.BlockSpec(memory_space=pltpu.SEMAPHORE)
_DATAFLOW = pltpu.SideEffectType.DATAFLOW_SIDE_EFFECTING


def _peer_copies(ins, lands, send_sems, recv_sems, scatter, arrivals):
    me = _my_index()
    copies = []
    for w in range(len(ins)):
        for k in range(1, N_DEV):
            dev, idx = _peer(k)
            copies.append(pltpu.make_async_remote_copy(
                src_ref=ins[w].at[idx] if scatter[w] else ins[w], dst_ref=lands[w].at[idx if arrivals else me],
                send_sem=send_sems[w].at[k - 1], recv_sem=recv_sems[w].at[k - 1], device_id=dev, device_id_type=MESH))
    return copies


def _exchange_start(arrays, scatter, name):
    n = len(arrays)
    land_shapes = [(N_DEV,) + tuple(a.shape[1:] if sc else a.shape) for a, sc in zip(arrays, scatter)]

    def body(*refs):
        ins, lands = refs[:n], refs[n:2 * n]
        send_sems, recv_sems = refs[2 * n:3 * n], refs[3 * n:4 * n]
        token = refs[6 * n]
        for cp in _peer_copies(ins, lands, send_sems, recv_sems, scatter, False):
            cp.start()
        token[...] = jnp.zeros_like(token)

    sem = pltpu.SemaphoreType.DMA((N_DEV - 1,))
    outs = pl.pallas_call(
        body,
        in_specs=[_HBM] * (2 * n),
        out_specs=[_SEM] * (2 * n) + [_HBM] * (2 * n) + [pl.BlockSpec(memory_space=pltpu.VMEM)],
        out_shape=[sem] * (2 * n) + [pltpu.HBM(a.shape, a.dtype) for a in arrays]
        + [pltpu.HBM(sh, a.dtype) for sh, a in zip(land_shapes, arrays)] + [SDS((8, LANES), F32)],
        input_output_aliases={i: 2 * n + i for i in range(2 * n)},
        compiler_params=pltpu.CompilerParams(has_side_effects=_DATAFLOW),
        name=name,
    )(*[pltpu.with_memory_space_constraint(a, pltpu.HBM) for a in arrays],
      *[pltpu.with_memory_space_constraint(lax.empty(sh, a.dtype), pltpu.HBM) for sh, a in zip(land_shapes, arrays)])
    handles = [dict(send=outs[w], recv=outs[n + w], src=outs[2 * n + w], land=outs[3 * n + w], scatter=scatter[w])
               for w in range(n)]
    return handles, outs[4 * n]


def _exchange_wait(handles, after, name):
    n = len(handles)
    scatter = [h["scatter"] for h in handles]

    def body(*refs):
        ins, lands = refs[:n], refs[n:2 * n]
        send_sems, recv_sems = refs[2 * n:3 * n], refs[3 * n:4 * n]
        for cp in _peer_copies(ins, lands, send_sems, recv_sems, scatter, False):
            cp.wait_send()
        for cp in _peer_copies(ins, lands, send_sems, recv_sems, scatter, True):
            cp.wait_recv()

    srcs, lands = [h["src"] for h in handles], [h["land"] for h in handles]
    outs = pl.pallas_call(
        body,
        in_specs=[_HBM] * (2 * n) + [_SEM] * (2 * n) + [pl.BlockSpec(memory_space=pl.ANY)],
        out_specs=[_HBM] * (2 * n),
        out_shape=[pltpu.HBM(a.shape, a.dtype) for a in srcs + lands],
        input_output_aliases={i: i for i in range(2 * n)},
        compiler_params=pltpu.CompilerParams(has_side_effects=_DATAFLOW),
        name=name,
    )(*srcs, *lands, *[h["send"] for h in handles], *[h["recv"] for h in handles], after)
    me = _my_index()
    full = []
    for src, land, sc in zip(outs[:n], outs[n:], scatter):
        own = lax.dynamic_index_in_dim(src, me, 0, keepdims=True) if sc else src[None]
        full.append(lax.dynamic_update_slice(land, own, (me,) + (0,) * (land.ndim - 1)))
    return full


def _adamw(parts, w, m, v, name):
    rows, cols = w.shape
    tr = rows // 4 if rows % 32 == 0 else rows

    def body(p_ref, w_ref, m_ref, v_ref, g_ref, d_ref, mo_ref, vo_ref):
        g = p_ref[0].astype(F32)
        for d in range(1, N_DEV):
            g = g + p_ref[d].astype(F32)
        g_ref[...] = g
        d_ref[...], mo_ref[...], vo_ref[...] = _adam_update(g, w_ref[...], m_ref[...], v_ref[...])

    blk = pl.BlockSpec((tr, cols), lambda i: (i, 0))
    return pl.pallas_call(
        body,
        grid=(rows // tr,),
        in_specs=[pl.BlockSpec((N_DEV, tr, cols), lambda i: (0, i, 0)), blk, blk, blk],
        out_specs=[blk] * 4,
        out_shape=[SDS((rows, cols), F32)] * 4,
        compiler_params=_cparams("arbitrary"),
        name=name,
    )(parts, w, m, v)


_ROW_OF = dict(norm1_g=(0, D_MODEL), norm2_g=(1, D_MODEL), final_g=(2, D_MODEL), pool_scale=(3, POOL_W),
               b_forget=(4, N_HEADS), loss=(5, 1))


def _pack_rows(vals):
    rows = [jnp.pad(vals[n].reshape(1, width).astype(F32), ((0, 0), (0, D_MODEL - width)))
            for n, (_, width) in sorted(_ROW_OF.items(), key=lambda kv: kv[1][0])]
    return jnp.concatenate(rows + [jnp.zeros((8 - len(rows), D_MODEL), F32)], axis=0)


def _adam_update(g, w, m, v):
    m_new = ADAM_B1 * m + (1.0 - ADAM_B1) * g
    v_new = ADAM_B2 * v + (1.0 - ADAM_B2) * (g * g)
    m_hat = m_new / (1.0 - ADAM_B1 ** ADAM_STEP)
    v_hat = v_new / (1.0 - ADAM_B2 ** ADAM_STEP)
    return -ADAM_LR * (m_hat / (jnp.sqrt(v_hat) + ADAM_EPS) + ADAM_WD * w), m_new, v_new


def _adamw_replicated(parts_rows, parts_pool, w, m, v):
    names = ("norm1_g", "norm2_g", "final_g", "pool_scale", "b_forget", "w_pool")
    shapes = {n: ((len(POOL_WINDOWS), POOL_G, POOL_G) if n == "w_pool" else (1, _ROW_OF[n][1])) for n in names}

    def body(rows_ref, pool_ref, *refs):
        ins, outs = refs[:3 * len(names)], refs[3 * len(names):]

        def total(n):
            if n == "w_pool":
                pieces = [pool_ref[d] for d in range(N_DEV)]
            else:
                row, width = _ROW_OF[n]
                pieces = [rows_ref[d, row:row + 1, 0:width] for d in range(N_DEV)]
            g = pieces[0]
            for p in pieces[1:]:
                g = g + p
            return g

        outs[0][...] = total("loss")
        for k, n in enumerate(names):
            g = total(n)
            delta, m_new, v_new = _adam_update(g, ins[3 * k][...], ins[3 * k + 1][...], ins[3 * k + 2][...])
            for o_ref, val in zip(outs[1 + 4 * k:5 + 4 * k], (g, delta, m_new, v_new)):
                o_ref[...] = val

    args = [d[n].reshape(shapes[n]) for n in names for d in (w, m, v)]
    res = pl.pallas_call(
        body,
        out_shape=[SDS((1, 1), F32)] + [SDS(shapes[n], F32) for n in names for _ in range(4)],
        compiler_params=_cparams(),
        name="adamw_replicated",
    )(parts_rows, parts_pool, *args)
    return res[0], {n: [r.reshape(w[n].shape) for r in res[1 + 4 * k:5 + 4 * k]] for k, n in enumerate(names)}


def kernel(x, norm1_g, w_in, b_forget, w_pool, pool_scale, w_out, norm2_g, w_gate, w_up, w_down, final_g, loss_target, m_norm1_g, m_w_in, m_b_forget, m_w_pool, m_pool_scale, m_w_out, m_norm2_g, m_w_gate, m_w_up, m_w_down, m_final_g, v_norm1_g, v_w_in, v_b_forget, v_w_pool, v_pool_scale, v_w_out, v_norm2_g, v_w_gate, v_w_up, v_w_down, v_final_g):
    big = ("w_in", "w_out", "w_gate", "w_up", "w_down")
    order = ("norm1_g", "w_in", "b_forget", "w_pool", "pool_scale", "w_out", "norm2_g", "w_gate", "w_up", "w_down",
             "final_g")
    w = dict(norm1_g=norm1_g, w_in=w_in, b_forget=b_forget, w_pool=w_pool, pool_scale=pool_scale, w_out=w_out,
             norm2_g=norm2_g, w_gate=w_gate, w_up=w_up, w_down=w_down, final_g=final_g)
    m = dict(norm1_g=m_norm1_g, w_in=m_w_in, b_forget=m_b_forget, w_pool=m_w_pool, pool_scale=m_pool_scale,
             w_out=m_w_out, norm2_g=m_norm2_g, w_gate=m_w_gate, w_up=m_w_up, w_down=m_w_down, final_g=m_final_g)
    v = dict(norm1_g=v_norm1_g, w_in=v_w_in, b_forget=v_b_forget, w_pool=v_w_pool, pool_scale=v_pool_scale,
             w_out=v_w_out, norm2_g=v_norm2_g, w_gate=v_w_gate, w_up=v_w_up, w_down=v_w_down, final_g=v_final_g)

    flipped = ("w_in", "w_gate", "w_up")
    shard = lambda d, n: d[n][0].T if n in flipped else d[n][0]
    gather, started = _exchange_start([shard(w, n).astype(BF16) for n in big], [False] * len(big), "gather_start")
    gather = dict(zip(big, gather))

    def gathered(names, after):
        return _exchange_wait([gather[n] for n in names], after, "gather_wait_" + names[0])

    def weight(name, after):
        if name == "w_in":
            full = gathered(["w_in"], after)[0].reshape(IN_W, D_MODEL)
            f0 = QKV_W + N_HEADS
            return jnp.concatenate([full[:QKV_W], full[f0:], full[QKV_W:f0],
                                    jnp.zeros((IN_PAD - IN_W, D_MODEL), BF16)], axis=0)
        if name == "w_out":
            return gathered(["w_out"], after)[0].reshape(D_MODEL, D_MODEL)
        if name == "w_gate_up":
            return [g.reshape(D_FF, D_MODEL) for g in gathered(["w_gate", "w_up"], after)]
        return gathered(["w_down"], after)[0].reshape(D_FF, D_MODEL)

    rows = lambda g: g.reshape(N_DEV, g.shape[0] // N_DEV, g.shape[1])
    sent = {}

    def emit(name, grad):
        if name == "w_gate_up":
            names, slots = ["w_gate", "w_up"], [rows(g) for g in grad]
        else:
            names, slots = [name], [rows(grad).astype(BF16) if name == "w_in" else rows(grad)]
        handles, token = _exchange_start(slots, [True] * len(slots), "grads_start_" + name)
        sent.update(zip(names, handles))
        return token

    loss_row, dx, small_grads = _local_step(x[0], loss_target[0], w, weight, emit, started)

    packed = _pack_rows(dict(small_grads, loss=0.5 / D_MODEL * jnp.sum(loss_row)))
    small_handles, after = _exchange_start([packed, small_grads["w_pool"]], [False, False], "grads_start_replicated")

    outs = {}
    for name in ("w_down", "w_gate", "w_up", "w_out", "w_in"):
        (parts,) = _exchange_wait([sent[name]], after, "grads_wait_" + name)
        outs[name] = _adamw(parts, shard(w, name), shard(m, name), shard(v, name), "adamw_" + name)
        after = outs[name][0]
        outs[name] = [(a.T if name in flipped else a)[None] for a in outs[name]]
    parts_rows, parts_pool = _exchange_wait(small_handles, after, "grads_wait_replicated")
    loss, small = _adamw_replicated(parts_rows, parts_pool, w, m, v)
    outs.update(small)

    return (loss.reshape(()), dx[None]) + tuple(outs[n][k] for k in range(4) for n in order)
```

```python
import functools

import jax
import jax.numpy as jnp
from jax import lax
from jax.experimental import pallas as pl
from jax.experimental.pallas import tpu as pltpu

F32 = jnp.float32
BF16 = jnp.bfloat16
SDS = jax.ShapeDtypeStruct

D_MODEL = 1024
ATTN_W = 512
N_HEADS = 8
HEAD_DIM = 64
N_PAIRS = N_HEADS // 2
POOL_W = 512
POOL_WINDOWS = (2, 4, 8, 16)
POOL_G = 128
HALO = 16
IN_W = 3 * ATTN_W + N_HEADS + POOL_W
QKV_W = 3 * ATTN_W
U_OFF = QKV_W
F_OFF = QKV_W + POOL_W
IN_PAD = F_OFF + 128
D_FF = 2816
EPS = 1e-6
NEG = -1e30
N_DEV = 8
LANES = 128

ADAM_LR = 0.001
ADAM_B1 = 0.9
ADAM_B2 = 0.999
ADAM_EPS = 1e-08
ADAM_WD = 0.01
ADAM_STEP = 10

VMEM_LIMIT_BYTES = 56 * 1024 * 1024
MESH = pl.DeviceIdType.MESH
NT = (((1,), (1,)), ((), ()))
TN = (((0,), (0,)), ((), ()))


def _cparams(*sem):
    return pltpu.CompilerParams(dimension_semantics=sem or None, vmem_limit_bytes=VMEM_LIMIT_BYTES)


def _split3(a):
    hi = a.astype(BF16)
    r1 = a - hi.astype(F32)
    mid = r1.astype(BF16)
    lo = (r1 - mid.astype(F32)).astype(BF16)
    return hi, mid, lo


def _dot_sel(a, sel, dims=None):
    sb = sel.astype(BF16)
    if dims is None:
        return sum(jnp.dot(p, sb, preferred_element_type=F32) for p in _split3(a))
    return sum(lax.dot_general(p, sb, dims, preferred_element_type=F32) for p in _split3(a))


def _sel_dot(sel, a, dims=None):
    sb = sel.astype(BF16)
    if dims is None:
        return sum(jnp.dot(sb, p, preferred_element_type=F32) for p in _split3(a))
    return sum(lax.dot_general(sb, p, dims, preferred_element_type=F32) for p in _split3(a))


def _iota2(shape, dim):
    return lax.broadcasted_iota(jnp.int32, shape, dim)


def _norm1(x, g1, *, tm):
    s = x.shape[0]

    def body(x_ref, g_ref, h_ref, r_ref):
        xv = x_ref[...]
        r = lax.rsqrt(jnp.mean(xv * xv, axis=-1, keepdims=True) + EPS)
        h_ref[...] = (xv * r * g_ref[...]).astype(BF16)
        r_ref[...] = r

    row = lambda w: pl.BlockSpec((tm, w), lambda i: (i, 0))
    return pl.pallas_call(
        body,
        grid=(s // tm,),
        in_specs=[row(D_MODEL), pl.BlockSpec((1, D_MODEL), lambda i: (0, 0))],
        out_specs=[row(D_MODEL), row(1)],
        out_shape=[SDS((s, D_MODEL), BF16), SDS((s, 1), F32)],
        compiler_params=_cparams("arbitrary"),
        name="norm1",
    )(x, g1)


def _in_proj(h, w_in_t, *, tm):
    s = h.shape[0]

    def body(h_ref, w_ref, qkv_ref, u_ref, fl_ref):
        h = h_ref[...]
        qkv_ref[...] = lax.dot_general(h, w_ref[0:QKV_W, :], NT, preferred_element_type=F32).astype(BF16)
        u_ref[...] = lax.dot_general(h, w_ref[U_OFF:F_OFF, :], NT, preferred_element_type=F32)
        fl_ref[...] = lax.dot_general(h, w_ref[F_OFF:IN_PAD, :], NT, preferred_element_type=F32)

    row = lambda w: pl.BlockSpec((tm, w), lambda i: (i, 0))
    return pl.pallas_call(
        body,
        grid=(s // tm,),
        in_specs=[row(D_MODEL), pl.BlockSpec((IN_PAD, D_MODEL), lambda i: (0, 0))],
        out_specs=[row(QKV_W), row(POOL_W), row(LANES)],
        out_shape=[SDS((s, QKV_W), BF16), SDS((s, POOL_W), F32), SDS((s, LANES), F32)],
        compiler_params=_cparams("arbitrary"),
        name="in_proj",
    )(h, w_in_t)


def _head_block_masks(rows, nb):
    shift = nb.bit_length() - 1
    rr, cc = _iota2((rows, rows), 0), _iota2((rows, rows), 1)
    same = lax.shift_right_logical(rr, shift) == lax.shift_right_logical(cc, shift)
    return rr, cc, same


def _forget_cumsum(fl_t, b_rows):
    rows = fl_t.shape[0]
    nb = rows // N_HEADS

    def body(fl_ref, b_ref, c_ref):
        z = fl_ref[...] + b_ref[...]
        lf = jnp.minimum(z, 0.0) - jnp.log1p(jnp.exp(-jnp.abs(z)))
        upper = _iota2((LANES, LANES), 0) <= _iota2((LANES, LANES), 1)
        within = _dot_sel(lf, upper)
        tot = _dot_sel(lf, jnp.ones((LANES, LANES), F32))
        rr, cc, same = _head_block_masks(rows, nb)
        c_ref[...] = within + _sel_dot(same & (cc < rr), tot)

    return pl.pallas_call(body, out_shape=SDS(fl_t.shape, F32), compiler_params=_cparams(), name="forget_cumsum")(
        fl_t, b_rows)


BIAS_LANES = 3


def _augment(t, h, col, col_first):
    n = t.shape[0]
    lane = _iota2((n, LANES), 1)
    own = (lane < HEAD_DIM) if h == 0 else (lane >= HEAD_DIM)
    b0 = HEAD_DIM if h == 0 else 0
    c0, o0 = (b0, b0 + BIAS_LANES) if col_first else (b0 + BIAS_LANES, b0)
    x = jnp.where(own, t, 0.0)
    for off, piece in enumerate(_split3(col)):
        x = jnp.where(lane == c0 + off, piece.astype(F32), x)
    x = jnp.where((lane >= o0) & (lane < o0 + BIAS_LANES), 1.0, x)
    return x.astype(BF16)


def _attn_fwd(qkv, c_col, *, tk):
    s = qkv.shape[0]
    tq = 2 * tk
    nb = s // tk

    def body(q_ref, k_ref, v_ref, cq_ref, ck_ref, o_ref, lse_ref, kp_ref, vt_ref, st_ref):
        i = pl.program_id(1)

        @pl.when(i == 0)
        def _():
            def prep(jb, _):
                st = pl.multiple_of(jb * tk, tk)
                k2 = k_ref[pl.ds(st, tk), :].astype(F32)
                ck = ck_ref[pl.ds(st, tk), :]
                for h in range(2):
                    kp_ref[h * nb + jb] = _augment(k2, h, -ck[:, h:h + 1], True)
                vt_ref[jb] = v_ref[pl.ds(st, tk), :].astype(F32).T.astype(BF16)
                return 0

            lax.fori_loop(0, nb, prep, 0)

        qs = q_ref[...].astype(F32) * 0.125
        cq = cq_ref[...]
        qp = [_augment(qs, h, cq[:, h:h + 1], False) for h in range(2)]

        def logits(j):
            return tuple(lax.dot_general(kp_ref[h * nb + j], qp[h], NT, preferred_element_type=F32) for h in range(2))

        def softmax_pv(j, slot, stats, masked):
            out = []
            for h in range(2):
                m, l, acc = stats[h]
                st = st_ref[2 * slot + h]
                if masked:
                    st = jnp.where(j * tk + _iota2((tk, tq), 0) <= i * tq + _iota2((tk, tq), 1), st, NEG)
                m_new = jnp.maximum(m, jnp.max(st, axis=0, keepdims=True))
                alpha = jnp.exp(m - m_new)
                p = jnp.exp(st - m_new)
                l = alpha * l + jnp.sum(p, axis=0, keepdims=True)
                vt = vt_ref[j, h * HEAD_DIM:(h + 1) * HEAD_DIM, :]
                acc = alpha * acc + jnp.dot(vt, p.astype(BF16), preferred_element_type=F32)
                out.append((m_new, l, acc))
            return tuple(out)

        def put(slot, j):
            for h, st in enumerate(logits(j)):
                st_ref[2 * slot + h] = st

        def pair(t, stats):
            put(1, 2 * t + 1)
            stats = softmax_pv(2 * t, 0, stats, False)
            put(0, 2 * t + 2)
            return softmax_pv(2 * t + 1, 1, stats, False)

        init = tuple((jnp.full((1, tq), NEG, F32), jnp.zeros((1, tq), F32), jnp.zeros((HEAD_DIM, tq), F32))
                     for _ in range(2))
        put(0, 0)
        stats = lax.fori_loop(0, i, pair, init)
        put(1, 2 * i + 1)
        stats = softmax_pv(2 * i, 0, stats, True)
        (ma, la, acca), (mb, lb, accb) = softmax_pv(2 * i + 1, 1, stats, True)
        o_ref[...] = jnp.concatenate([acca / la, accb / lb], axis=0).T.astype(BF16)
        lse_ref[...] = jnp.where(_iota2((2, tq), 0) == 0, ma + jnp.log(la), mb + jnp.log(lb))

    return pl.pallas_call(
        body,
        grid=(N_PAIRS, s // tq),
        in_specs=[
            pl.BlockSpec((tq, LANES), lambda p, i: (i, p)),
            pl.BlockSpec((s, LANES), lambda p, i: (0, N_PAIRS + p)),
            pl.BlockSpec((s, LANES), lambda p, i: (0, 2 * N_PAIRS + p)),
            pl.BlockSpec((None, tq, 2), lambda p, i: (p, i, 0)),
            pl.BlockSpec((None, s, 2), lambda p, i: (p, 0, 0)),
        ],
        out_specs=[
            pl.BlockSpec((tq, LANES), lambda p, i: (i, p)),
            pl.BlockSpec((None, None, 2, tq), lambda p, i: (p, i, 0, 0)),
        ],
        out_shape=[SDS((s, ATTN_W), BF16), SDS((N_PAIRS, s // tq, 2, tq), F32)],
        scratch_shapes=[pltpu.VMEM((2 * nb, tk, LANES), BF16), pltpu.VMEM((nb, LANES, tk), BF16),
                        pltpu.VMEM((4, tk, tq), F32)],
        compiler_params=_cparams("arbitrary", "arbitrary"),
        name="attn_fwd",
    )(qkv, qkv, qkv, c_col, c_col)


def _pool_counts(row0, tm, w):
    t = row0 + _iota2((tm, 1), 0)
    return jnp.minimum(t + 1, w).astype(F32)


def _pool_fwd(u, w_pool, pool_scale, *, tm):
    s = u.shape[0]

    def body(u_ref, w_ref, sc_ref, pooled_ref, po_ref, tail_ref):
        i = pl.program_id(0)

        @pl.when(i == 0)
        def _():
            tail_ref[...] = jnp.zeros_like(tail_ref)

        uv = u_ref[...]
        ext = jnp.concatenate([tail_ref[...], uv], axis=0)
        tail_ref[...] = uv[tm - HALO:, :]
        for g, w in enumerate(POOL_WINDOWS):
            cols = slice(g * POOL_G, (g + 1) * POOL_G)
            acc = ext[:, cols]
            k = 1
            while k < w:
                acc = acc + pltpu.roll(acc, k, axis=0)
                k *= 2
            pooled = (acc[HALO:, :] / _pool_counts(i * tm, tm, w) - uv[:, cols]).astype(BF16)
            pooled_ref[:, cols] = pooled
            mixed = jnp.dot(pooled, w_ref[g].astype(BF16), preferred_element_type=F32)
            po_ref[:, cols] = (mixed * sc_ref[:, cols]).astype(BF16)

    row = pl.BlockSpec((tm, POOL_W), lambda i: (i, 0))
    return pl.pallas_call(
        body,
        grid=(s // tm,),
        in_specs=[row, pl.BlockSpec((len(POOL_WINDOWS), POOL_G, POOL_G), lambda i: (0, 0, 0)),
                  pl.BlockSpec((1, POOL_W), lambda i: (0, 0))],
        out_specs=[row, row],
        out_shape=[SDS((s, POOL_W), BF16), SDS((s, POOL_W), BF16)],
        scratch_shapes=[pltpu.VMEM((HALO, POOL_W), F32)],
        compiler_params=_cparams("arbitrary"),
        name="pool_fwd",
    )(u, w_pool, pool_scale)


def _out_norm2(attn_o, pool_o, w_out, x, g2, *, tm):
    s = x.shape[0]

    def body(a_ref, p_ref, w_ref, x_ref, g_ref, x1_ref, h2_ref, r_ref):
        x1 = (x_ref[...] + jnp.dot(a_ref[...], w_ref[0:ATTN_W, :], preferred_element_type=F32)
              + jnp.dot(p_ref[...], w_ref[ATTN_W:, :], preferred_element_type=F32))
        r = lax.rsqrt(jnp.mean(x1 * x1, axis=-1, keepdims=True) + EPS)
        x1_ref[...] = x1
        r_ref[...] = r
        h2_ref[...] = (x1 * r * g_ref[...]).astype(BF16)

    row = lambda w: pl.BlockSpec((tm, w), lambda i: (i, 0))
    full = lambda a, b: pl.BlockSpec((a, b), lambda i: (0, 0))
    return pl.pallas_call(
        body,
        grid=(s // tm,),
        in_specs=[row(ATTN_W), row(POOL_W), full(D_MODEL, D_MODEL), row(D_MODEL), full(1, D_MODEL)],
        out_specs=[row(D_MODEL), row(D_MODEL), row(1)],
        out_shape=[SDS((s, D_MODEL), F32), SDS((s, D_MODEL), BF16), SDS((s, 1), F32)],
        compiler_params=_cparams("arbitrary"),
        name="out_norm2",
    )(attn_o, pool_o, w_out, x, g2)


def _gate_up(h2, wg_t, wu_t, *, tm, tn):
    s = h2.shape[0]

    def body(h_ref, wg_ref, wu_ref, gate_ref, up_ref, act_ref):
        h = h_ref[...]
        gate = lax.dot_general(h, wg_ref[...], NT, preferred_element_type=F32)
        up = lax.dot_general(h, wu_ref[...], NT, preferred_element_type=F32)
        gate_ref[...] = gate.astype(BF16)
        up_ref[...] = up.astype(BF16)
        act_ref[...] = (gate * jax.nn.sigmoid(gate) * up).astype(BF16)

    wspec = pl.BlockSpec((tn, D_MODEL), lambda c, r: (c, 0))
    ospec = pl.BlockSpec((tm, tn), lambda c, r: (r, c))
    return pl.pallas_call(
        body,
        grid=(D_FF // tn, s // tm),
        in_specs=[pl.BlockSpec((tm, D_MODEL), lambda c, r: (r, 0)), wspec, wspec],
        out_specs=[ospec, ospec, ospec],
        out_shape=[SDS((s, D_FF), BF16), SDS((s, D_FF), BF16), SDS((s, D_FF), BF16)],
        compiler_params=_cparams("arbitrary", "arbitrary"),
        name="gate_up",
    )(h2, wg_t, wu_t)


def _down_final(act, wd, x1, gf, tgt, *, tm):
    s = x1.shape[0]

    def body(a_ref, w_ref, x1_ref, g_ref, t_ref, dx2_ref, loss_ref, dgf_ref):
        @pl.when(pl.program_id(0) == 0)
        def _():
            loss_ref[...] = jnp.zeros_like(loss_ref)
            dgf_ref[...] = jnp.zeros_like(dgf_ref)

        x2 = x1_ref[...] + jnp.dot(a_ref[...], w_ref[...], preferred_element_type=F32)
        r = lax.rsqrt(jnp.mean(x2 * x2, axis=-1, keepdims=True) + EPS)
        xn = x2 * r
        g = g_ref[...]
        diff = xn * g - t_ref[...]
        loss_ref[...] += jnp.sum(diff * diff, axis=0, keepdims=True)
        dy = diff * (1.0 / D_MODEL)
        dgf_ref[...] += jnp.sum(dy * xn, axis=0, keepdims=True)
        dxn = dy * g
        dx2_ref[...] = r * (dxn - xn * jnp.mean(dxn * xn, axis=-1, keepdims=True))

    row = lambda w: pl.BlockSpec((tm, w), lambda i: (i, 0))
    full = lambda a, b: pl.BlockSpec((a, b), lambda i: (0, 0))
    return pl.pallas_call(
        body,
        grid=(s // tm,),
        in_specs=[row(D_FF), full(D_FF, D_MODEL), row(D_MODEL), full(1, D_MODEL), row(D_MODEL)],
        out_specs=[row(D_MODEL), full(1, D_MODEL), full(1, D_MODEL)],
        out_shape=[SDS((s, D_MODEL), F32), SDS((1, D_MODEL), F32), SDS((1, D_MODEL), F32)],
        compiler_params=_cparams("arbitrary"),
        name="down_final",
    )(act, wd, x1, gf, tgt)


def _swiglu_bwd(dx2, wd, gate, up, *, tm, tn):
    s = dx2.shape[0]

    def body(d_ref, w_ref, gate_ref, up_ref, dgate_ref, dup_ref):
        dact = lax.dot_general(d_ref[...].astype(BF16), w_ref[...], NT, preferred_element_type=F32)
        gate = gate_ref[...].astype(F32)
        sg = jax.nn.sigmoid(gate)
        dup_ref[...] = (dact * (gate * sg)).astype(BF16)
        dgate_ref[...] = (dact * up_ref[...].astype(F32) * (sg * (1.0 + gate * (1.0 - sg)))).astype(BF16)

    ospec = pl.BlockSpec((tm, tn), lambda c, r: (r, c))
    return pl.pallas_call(
        body,
        grid=(D_FF // tn, s // tm),
        in_specs=[pl.BlockSpec((tm, D_MODEL), lambda c, r: (r, 0)), pl.BlockSpec((tn, D_MODEL), lambda c, r: (c, 0)),
                  ospec, ospec],
        out_specs=[ospec, ospec],
        out_shape=[SDS((s, D_FF), BF16), SDS((s, D_FF), BF16)],
        compiler_params=_cparams("arbitrary", "arbitrary"),
        name="swiglu_bwd",
    )(dx2, wd, gate, up)


def _mm_tn(a, bs, *, ta, ts, name):
    s, ka = a.shape
    n = len(bs)

    def body(a_ref, *refs):
        b_refs, o_refs = refs[:n], refs[n:]

        @pl.when(pl.program_id(1) == 0)
        def _():
            for o_ref in o_refs:
                o_ref[...] = jnp.zeros_like(o_ref)

        av = a_ref[...].astype(BF16)
        for b_ref, o_ref in zip(b_refs, o_refs):
            o_ref[...] += lax.dot_general(av, b_ref[...].astype(BF16), TN, preferred_element_type=F32)

    return pl.pallas_call(
        body,
        grid=(ka // ta, s // ts),
        in_specs=[pl.BlockSpec((ts, ta), lambda i, k: (k, i))]
        + [pl.BlockSpec((ts, b.shape[1]), lambda i, k: (k, 0)) for b in bs],
        out_specs=[pl.BlockSpec((ta, b.shape[1]), lambda i, k: (i, 0)) for b in bs],
        out_shape=[SDS((ka, b.shape[1]), F32) for b in bs],
        compiler_params=_cparams("arbitrary", "arbitrary"),
        name=name,
    )(a, *bs)


def _mm_tn_shared(as_, b, *, ts, name):
    s, nb_ = b.shape
    n = len(as_)

    def body(*refs):
        a_refs, b_ref, o_refs = refs[:n], refs[n], refs[n + 1:]

        @pl.when(pl.program_id(0) == 0)
        def _():
            for o_ref in o_refs:
                o_ref[...] = jnp.zeros_like(o_ref)

        bv = b_ref[...].astype(BF16)
        for a_ref, o_ref in zip(a_refs, o_refs):
            o_ref[...] += lax.dot_general(a_ref[...].astype(BF16), bv, TN, preferred_element_type=F32)

    return pl.pallas_call(
        body,
        grid=(s // ts,),
        in_specs=[pl.BlockSpec((ts, a.shape[1]), lambda k: (k, 0)) for a in as_] + [pl.BlockSpec((ts, nb_), lambda k: (k, 0))],
        out_specs=[pl.BlockSpec((a.shape[1], nb_), lambda k: (0, 0)) for a in as_],
        out_shape=[SDS((a.shape[1], nb_), F32) for a in as_],
        compiler_params=_cparams("arbitrary"),
        name=name,
    )(*as_, b)


def _norm_bwd(dh, x, r, g, dres):
    xn = x * r
    dxn = dh * g
    dx = dres + r * (dxn - xn * jnp.mean(dxn * xn, axis=-1, keepdims=True))
    return dx, jnp.sum(dh * xn, axis=0, keepdims=True)


def _mlp_in_bwd(dgate, dup, wg_t, wu_t, w_out, x1, r2, g2, dx2, *, tm):
    s = x1.shape[0]

    def body(dg_ref, du_ref, wg_ref, wu_ref, wo_ref, x_ref, r_ref, g_ref, d_ref, dx1_ref, dmix_ref, dg2_ref):
        @pl.when(pl.program_id(0) == 0)
        def _():
            dg2_ref[...] = jnp.zeros_like(dg2_ref)

        dh2 = (jnp.dot(dg_ref[...], wg_ref[...], preferred_element_type=F32)
               + jnp.dot(du_ref[...], wu_ref[...], preferred_element_type=F32))
        dx1, dg2 = _norm_bwd(dh2, x_ref[...], r_ref[...], g_ref[...], d_ref[...])
        dg2_ref[...] += dg2
        dx1_ref[...] = dx1
        dmix_ref[...] = lax.dot_general(dx1.astype(BF16), wo_ref[...], NT, preferred_element_type=F32)

    row = lambda w: pl.BlockSpec((tm, w), lambda i: (i, 0))
    full = lambda a, b: pl.BlockSpec((a, b), lambda i: (0, 0))
    return pl.pallas_call(
        body,
        grid=(s // tm,),
        in_specs=[row(D_FF), row(D_FF), full(D_FF, D_MODEL), full(D_FF, D_MODEL), full(D_MODEL, D_MODEL),
                  row(D_MODEL), row(1), full(1, D_MODEL), row(D_MODEL)],
        out_specs=[row(D_MODEL), row(D_MODEL), full(1, D_MODEL)],
        out_shape=[SDS((s, D_MODEL), F32), SDS((s, D_MODEL), F32), SDS((1, D_MODEL), F32)],
        compiler_params=_cparams("arbitrary"),
        name="mlp_in_bwd",
    )(dgate, dup, wg_t, wu_t, w_out, x1, r2, g2, dx2)


def _pool_bwd(dmixed, pooled, w_pool, pool_scale, *, tm):
    s = pooled.shape[0]
    nt = s // tm
    ng = len(POOL_WINDOWS)

    def body(d_ref, p_ref, w_ref, sc_ref, du_ref, dw_ref, dsc_ref, head_ref):
        i = pl.program_id(0)

        @pl.when(i == 0)
        def _():
            head_ref[...] = jnp.zeros_like(head_ref)
            dw_ref[...] = jnp.zeros_like(dw_ref)
            dsc_ref[...] = jnp.zeros_like(dsc_ref)

        row0 = (nt - 1 - i) * tm
        for g, w in enumerate(POOL_WINDOWS):
            cols = slice(g * POOL_G, (g + 1) * POOL_G)
            wb = w_ref[g].astype(BF16)
            pooled_g = p_ref[:, cols]
            dpo = d_ref[:, cols]
            mixed = jnp.dot(pooled_g, wb, preferred_element_type=F32)
            dsc_ref[:, cols] += jnp.sum(dpo * mixed, axis=0, keepdims=True)
            dmp = (dpo * sc_ref[:, cols]).astype(BF16)
            dw_ref[g] += lax.dot_general(pooled_g, dmp, TN, preferred_element_type=F32)
            dpooled = lax.dot_general(dmp, wb, NT, preferred_element_type=F32)
            a = dpooled / _pool_counts(row0, tm, w)
            acc = jnp.concatenate([a, head_ref[:, cols]], axis=0)
            head_ref[:, cols] = a[0:HALO, :]
            k = 1
            while k < w:
                acc = acc + pltpu.roll(acc, tm + HALO - k, axis=0)
                k *= 2
            du_ref[:, cols] = (acc[0:tm, :] - dpooled).astype(BF16)

    rev = lambda i: (nt - 1 - i, 0)
    return pl.pallas_call(
        body,
        grid=(nt,),
        in_specs=[pl.BlockSpec((tm, POOL_W), lambda i: (nt - 1 - i, 1)), pl.BlockSpec((tm, POOL_W), rev),
                  pl.BlockSpec((ng, POOL_G, POOL_G), lambda i: (0, 0, 0)), pl.BlockSpec((1, POOL_W), lambda i: (0, 0))],
        out_specs=[pl.BlockSpec((tm, POOL_W), rev), pl.BlockSpec((ng, POOL_G, POOL_G), lambda i: (0, 0, 0)),
                   pl.BlockSpec((1, POOL_W), lambda i: (0, 0))],
        out_shape=[SDS((s, POOL_W), BF16), SDS((ng, POOL_G, POOL_G), F32), SDS((1, POOL_W), F32)],
        scratch_shapes=[pltpu.VMEM((HALO, POOL_W), F32)],
        compiler_params=_cparams("arbitrary"),
        name="pool_bwd",
    )(dmixed, pooled, w_pool, pool_scale)


SUM_ROWS = 16


def _heads_t(t):
    n = t.shape[0]
    lane = _iota2((n, LANES), 1)
    tf = t.astype(F32)
    halves = jnp.concatenate([jnp.where(lane < HEAD_DIM, tf, 0.0).T, jnp.where(lane < HEAD_DIM, 0.0, tf).T], axis=1)
    r, c = _iota2((SUM_ROWS, 2 * n), 0), _iota2((SUM_ROWS, 2 * n), 1)
    ones = jnp.where(((r == 0) & (c < n)) | ((r == 4) & (c >= n)), 1.0, 0.0)
    return jnp.concatenate([halves, ones], axis=0).astype(BF16)


def _attn_bwd(qkv, attn_o, dmixed, rowb, ck_col, *, tq):
    s = qkv.shape[0]
    tk = tq
    nb = s // tq
    rows_t = LANES + SUM_ROWS

    def body(q_ref, k_ref, v_ref, o_ref, do_ref, rowb_ref, ck_ref, dq_ref, dk_ref, dv_ref, dck_ref, dcq_ref,
             dqt_ref, delta_ref, kp_ref, qp_ref, dob_ref, qt_ref, kt_ref, dot_ref, front_ref):
        lane = _iota2((tq, LANES), 1)
        lo = lane < HEAD_DIM
        first = _iota2((8, LANES), 1) < HEAD_DIM
        sel = jnp.where(_iota2((8, LANES), 0) < 4, jnp.where(first, 1.0, 0.0), jnp.where(first, 0.0, 1.0))

        def prep(b, _):
            st = pl.multiple_of(b * tq, tq)
            do2 = do_ref[pl.ds(st, tq), :]
            delta_ref[b] = _sel_dot(sel, do2 * o_ref[pl.ds(st, tq), :].astype(F32), NT)
            dob_ref[pl.ds(st, tq), :] = do2.astype(BF16)
            dqt_ref[b] = jnp.zeros((rows_t, tq), F32)
            k2 = k_ref[pl.ds(st, tq), :].astype(F32)
            q2 = q_ref[pl.ds(st, tq), :].astype(F32)
            ck = ck_ref[pl.ds(st, tq), :]
            for h in range(2):
                kp_ref[h * nb + b] = _augment(k2, h, -ck[:, h:h + 1], True)
                qp_ref[h * nb + b] = _augment(q2 * 0.125, h, jnp.zeros((tq, 1), F32), False)
            qt_ref[b] = _heads_t(q2)
            kt_ref[b] = _heads_t(k2)
            dot_ref[b] = _heads_t(do2)[0:LANES, :]
            return 0

        lax.fori_loop(0, nb, prep, 0)

        def split(t):
            z = jnp.zeros_like(t)
            return jnp.where(lo, t, z), jnp.where(lo, z, t)

        def kv_block(j, _):
            st_j = pl.multiple_of(j * tk, tk)
            vs = split(v_ref[pl.ds(st_j, tk), :])
            kt = kt_ref[j]

            def stage(i, slot):
                ic = jnp.minimum(i, nb - 1)
                do2 = dob_ref[pl.ds(pl.multiple_of(ic * tq, tq), tq), :]
                for h in range(2):
                    front_ref[4 * slot + h] = lax.dot_general(kp_ref[h * nb + j], qp_ref[h * nb + ic], NT,
                                                              preferred_element_type=F32)
                    front_ref[4 * slot + 2 + h] = lax.dot_general(vs[h], do2, NT, preferred_element_type=F32)

            def q_block(i, slot, carry, diagonal):
                dkt, dvt = carry
                ic = jnp.minimum(i, nb - 1)
                rb = rowb_ref[ic] + jnp.where(i < nb, 0.0, NEG)
                dl = delta_ref[ic]
                pts, dsts = [], []
                for h in range(2):
                    st = front_ref[4 * slot + h] + rb[h:h + 1, :]
                    if diagonal:
                        st = jnp.where(_iota2((tk, tq), 0) <= _iota2((tk, tq), 1), st, NEG)
                    pt = jnp.exp(st)
                    pts.append(pt.astype(BF16))
                    dsts.append((pt * (front_ref[4 * slot + 2 + h] - dl[4 * h:4 * h + 1, :])).astype(BF16))
                dvt = dvt + lax.dot_general(dot_ref[ic], jnp.concatenate(pts, axis=1), NT, preferred_element_type=F32)
                dkt = dkt + lax.dot_general(qt_ref[ic], jnp.concatenate(dsts, axis=1), NT, preferred_element_type=F32)
                dqt_ref[ic] += jnp.dot(kt, jnp.concatenate(dsts, axis=0), preferred_element_type=F32)
                return dkt, dvt

            def pair(t, carry):
                i0 = j + 1 + 2 * t
                stage(i0 + 1, 0)
                carry = q_block(i0, 1, carry, False)
                stage(i0 + 2, 1)
                return q_block(i0 + 1, 0, carry, False)

            stage(j, 0)
            stage(j + 1, 1)
            carry = q_block(j, 0, (jnp.zeros((rows_t, tk), F32), jnp.zeros((LANES, tk), F32)), True)
            dkt, dvt = lax.fori_loop(0, lax.shift_right_logical(nb - j, 1), pair, carry)
            dk_ref[pl.ds(st_j, tk), :] = (dkt[0:LANES, :].T * 0.125).astype(BF16)
            dv_ref[pl.ds(st_j, tk), :] = dvt.T.astype(BF16)
            dck_ref[j] = dkt[LANES:LANES + 8, :]
            return 0

        lax.fori_loop(0, nb, kv_block, 0)

        def finish(b, _):
            acc = dqt_ref[b]
            dq_ref[pl.ds(pl.multiple_of(b * tq, tq), tq), :] = (acc[0:LANES, :].T * 0.125).astype(BF16)
            dcq_ref[b] = acc[LANES:LANES + 8, :]
            return 0

        lax.fori_loop(0, nb, finish, 0)

    col = lambda off: pl.BlockSpec((s, LANES), lambda p: (0, off + p))
    sums = pl.BlockSpec((None, nb, 8, tq), lambda p: (p, 0, 0, 0))
    return pl.pallas_call(
        body,
        grid=(N_PAIRS,),
        in_specs=[col(0), col(N_PAIRS), col(2 * N_PAIRS), col(0), col(0),
                  pl.BlockSpec((None, nb, 2, tq), lambda p: (p, 0, 0, 0)),
                  pl.BlockSpec((None, s, 2), lambda p: (p, 0, 0))],
        out_specs=[col(0), col(0), col(0), sums, sums],
        out_shape=[SDS((s, ATTN_W), BF16), SDS((s, ATTN_W), BF16), SDS((s, ATTN_W), BF16),
                   SDS((N_PAIRS, nb, 8, tq), F32), SDS((N_PAIRS, nb, 8, tq), F32)],
        scratch_shapes=[pltpu.VMEM((nb, rows_t, tq), F32), pltpu.VMEM((nb, 8, tq), F32),
                        pltpu.VMEM((2 * nb, tk, LANES), BF16), pltpu.VMEM((2 * nb, tq, LANES), BF16),
                        pltpu.VMEM((s, LANES), BF16), pltpu.VMEM((nb, rows_t, 2 * tq), BF16),
                        pltpu.VMEM((nb, rows_t, 2 * tk), BF16), pltpu.VMEM((nb, LANES, 2 * tq), BF16),
                        pltpu.VMEM((8, tk, tq), F32)],
        compiler_params=_cparams("arbitrary"),
        name="attn_bwd",
    )(qkv, qkv, qkv, attn_o, dmixed, rowb, ck_col)


def _forget_bwd(dc_t, fl_t, b_rows):
    rows = fl_t.shape[0]
    nb = rows // N_HEADS

    def body(dc_ref, fl_ref, b_ref, dfl_ref, db_ref):
        dc = dc_ref[...]
        lower = _iota2((LANES, LANES), 0) >= _iota2((LANES, LANES), 1)
        ones = jnp.ones((LANES, LANES), F32)
        rr, cc, same = _head_block_masks(rows, nb)
        dlf = _dot_sel(dc, lower) + _sel_dot(same & (cc > rr), _dot_sel(dc, ones))
        dfl = dlf / (1.0 + jnp.exp(fl_ref[...] + b_ref[...]))
        dfl_ref[...] = dfl
        shift = nb.bit_length() - 1
        hsel = lax.shift_right_logical(_iota2((N_HEADS, rows), 1), shift) == _iota2((N_HEADS, rows), 0)
        db_ref[...] = _sel_dot(hsel, _dot_sel(dfl, ones))

    return pl.pallas_call(body, out_shape=[SDS(fl_t.shape, F32), SDS((N_HEADS, LANES), F32)],
                          compiler_params=_cparams(), name="forget_bwd")(dc_t, fl_t, b_rows)


def _in_bwd(dq, dk, dv, du, dfl, w_in_t, x, r1, g1, dx1, *, tm):
    s = x.shape[0]
    pieces = ((0, ATTN_W), (ATTN_W, 2 * ATTN_W), (2 * ATTN_W, QKV_W), (U_OFF, F_OFF), (F_OFF, IN_PAD))

    def body(dq_ref, dk_ref, dv_ref, du_ref, df_ref, w_ref, x_ref, r_ref, g_ref, d_ref, dx_ref, dg1_ref):
        @pl.when(pl.program_id(0) == 0)
        def _():
            dg1_ref[...] = jnp.zeros_like(dg1_ref)

        dh = None
        for ref, (c0, c1) in zip((dq_ref, dk_ref, dv_ref, du_ref, df_ref), pieces):
            t = jnp.dot(ref[...], w_ref[c0:c1, :], preferred_element_type=F32)
            dh = t if dh is None else dh + t
        dx, dg1 = _norm_bwd(dh, x_ref[...], r_ref[...], g_ref[...], d_ref[...])
        dx_ref[...] = dx
        dg1_ref[...] += dg1

    row = lambda w: pl.BlockSpec((tm, w), lambda i: (i, 0))
    full = lambda a, b: pl.BlockSpec((a, b), lambda i: (0, 0))
    return pl.pallas_call(
        body,
        grid=(s // tm,),
        in_specs=[row(ATTN_W), row(ATTN_W), row(ATTN_W), row(POOL_W), row(LANES), full(IN_PAD, D_MODEL),
                  row(D_MODEL), row(1), full(1, D_MODEL), row(D_MODEL)],
        out_specs=[row(D_MODEL), full(1, D_MODEL)],
        out_shape=[SDS((s, D_MODEL), F32), SDS((1, D_MODEL), F32)],
        compiler_params=_cparams("arbitrary"),
        name="in_bwd",
    )(dq, dk, dv, du, dfl, w_in_t, x, r1, g1, dx1)


def _tiles(s):
    big = min(512, s)
    return dict(row=big, attn=min(256, s // 2), mlp_bwd=min(256, s))


def _tie(a, token):
    return a + token[0:1, 0:1].astype(a.dtype)


def _local_step(x, tgt, p, weight, emit, started):
    s = x.shape[0]
    t = _tiles(s)
    tm, tq = t["row"], t["attn"]
    nb = s // LANES
    nqb = s // tq
    g1, g2, gf = p["norm1_g"], p["norm2_g"], p["final_g"].reshape(1, D_MODEL)
    w_pool, pool_scale = p["w_pool"][0], p["pool_scale"]

    h, r1 = _norm1(x, _tie(g1, started), tm=tm)
    w_in_t = weight("w_in", h)
    qkv, u, fl = _in_proj(h, w_in_t, tm=tm)
    fl_t = fl[:, :N_HEADS].T.reshape(N_HEADS * nb, LANES)
    b_rows = jnp.repeat(p["b_forget"].reshape(N_HEADS), nb).reshape(N_HEADS * nb, 1)
    c = _forget_cumsum(fl_t, b_rows).reshape(N_PAIRS, 2, s)
    c_col = c.transpose(0, 2, 1)
    c_rowblk = c.reshape(N_PAIRS, 2, nqb, tq).transpose(0, 2, 1, 3)
    attn_o, lse = _attn_fwd(qkv, c_col, tk=tq)
    lse = lse.reshape(N_PAIRS, nqb // 2, 2, 2, tq).transpose(0, 1, 3, 2, 4).reshape(N_PAIRS, nqb, 2, tq)
    pooled, pool_o = _pool_fwd(u, w_pool, pool_scale, tm=tm)
    w_out = weight("w_out", attn_o)
    x1, h2, r2 = _out_norm2(attn_o, pool_o, w_out, x, g2, tm=tm)
    wg_t, wu_t = weight("w_gate_up", h2)
    gate, up, act = _gate_up(h2, wg_t, wu_t, tm=tm, tn=D_FF // 2)
    wd = weight("w_down", act)
    dx2, loss_row, d_gf = _down_final(act, wd, x1, gf, tgt, tm=tm)

    dgate, dup = _swiglu_bwd(dx2, wd, gate, up, tm=tm, tn=D_FF // 2)
    (d_wd,) = _mm_tn(act, [dx2], ta=D_FF // 2, ts=tm, name="grad_w_down")
    token = emit("w_down", d_wd)
    (d_wg_t,) = _mm_tn(dgate, [h2], ta=D_FF // 2, ts=tm, name="grad_w_gate")
    (d_wu_t,) = _mm_tn(dup, [h2], ta=D_FF // 2, ts=tm, name="grad_w_up")
    token = token + emit("w_gate_up", (d_wg_t, d_wu_t))
    dx1, dmixed, d_g2 = _mlp_in_bwd(dgate, dup, wg_t, wu_t, w_out, x1, r2, _tie(g2, token), dx2, tm=t["mlp_bwd"])
    du, d_wpool, d_pscale = _pool_bwd(dmixed, pooled, w_pool, pool_scale, tm=tm)
    token = emit("w_out", jnp.concatenate(_mm_tn_shared([attn_o, pool_o], dx1, ts=tm, name="grad_w_out"), axis=0))
    rowb = _tie(c_rowblk - lse, token)
    dq, dk, dv, dck, dcq = _attn_bwd(qkv, attn_o, dmixed, rowb, c_col, tq=tq)
    dc_t = (dcq - dck)[:, :, 0::4, :].transpose(0, 2, 1, 3).reshape(N_HEADS * nb, LANES)
    dfl_t, db = _forget_bwd(dc_t, fl_t, b_rows)
    dfl = jnp.pad(dfl_t.reshape(N_HEADS, s).T, ((0, 0), (0, LANES - N_HEADS))).astype(BF16)
    d_wq, d_wk, d_wv, d_wu_in, d_wf = _mm_tn_shared([dq, dk, dv, du, dfl], h, ts=tm, name="grad_w_in")
    token = emit("w_in", jnp.concatenate([d_wq, d_wk, d_wv, d_wf[:N_HEADS], d_wu_in], axis=0))
    dx, d_g1 = _in_bwd(dq, dk, dv, du, dfl, w_in_t, x, r1, _tie(g1, token), dx1, tm=tm)

    small = dict(norm1_g=d_g1, b_forget=db[:, 0].reshape(1, N_HEADS), w_pool=d_wpool, pool_scale=d_pscale,
                 norm2_g=d_g2, final_g=d_gf)
    return loss_row, dx, small


def _my_index():
    return 4 * lax.axis_index("x") + 2 * lax.axis_index("y") + lax.axis_index("c")


def _peer(k):
    pos = [lax.axis_index(a) for a in ("x", "y", "c")]
    flipped = tuple(1 - p if (k >> b) & 1 else p for p, b in zip(pos, (2, 1, 0)))
    return flipped, 4 * flipped[0] + 2 * flipped[1] + flipped[2]


_HBM = pl.BlockSpec(memory_space=pltpu.HBM)
_SEM = pl.BlockSpec(memory_space=pltpu.SEMAPHORE)
_DATAFLOW = pltpu.SideEffectType.DATAFLOW_SIDE_EFFECTING


def _peer_copies(ins, lands, send_sems, recv_sems, scatter, arrivals):
    me = _my_index()
    copies = []
    for w in range(len(ins)):
        for k in range(1, N_DEV):
            dev, idx = _peer(k)
            copies.append(pltpu.make_async_remote_copy(
                src_ref=ins[w].at[idx] if scatter[w] else ins[w], dst_ref=lands[w].at[idx if arrivals else me],
                send_sem=send_sems[w].at[k - 1], recv_sem=recv_sems[w].at[k - 1], device_id=dev, device_id_type=MESH))
    return copies


def _exchange_start(arrays, scatter, name):
    n = len(arrays)
    land_shapes = [(N_DEV,) + tuple(a.shape[1:] if sc else a.shape) for a, sc in zip(arrays, scatter)]

    def body(*refs):
        ins, lands = refs[:n], refs[n:2 * n]
        send_sems, recv_sems = refs[2 * n:3 * n], refs[3 * n:4 * n]
        token = refs[6 * n]
        for cp in _peer_copies(ins, lands, send_sems, recv_sems, scatter, False):
            cp.start()
        token[...] = jnp.zeros_like(token)

    sem = pltpu.SemaphoreType.DMA((N_DEV - 1,))
    outs = pl.pallas_call(
        body,
        in_specs=[_HBM] * (2 * n),
        out_specs=[_SEM] * (2 * n) + [_HBM] * (2 * n) + [pl.BlockSpec(memory_space=pltpu.VMEM)],
        out_shape=[sem] * (2 * n) + [pltpu.HBM(a.shape, a.dtype) for a in arrays]
        + [pltpu.HBM(sh, a.dtype) for sh, a in zip(land_shapes, arrays)] + [SDS((8, LANES), F32)],
        input_output_aliases={i: 2 * n + i for i in range(2 * n)},
        compiler_params=pltpu.CompilerParams(has_side_effects=_DATAFLOW),
        name=name,
    )(*[pltpu.with_memory_space_constraint(a, pltpu.HBM) for a in arrays],
      *[pltpu.with_memory_space_constraint(lax.empty(sh, a.dtype), pltpu.HBM) for sh, a in zip(land_shapes, arrays)])
    handles = [dict(send=outs[w], recv=outs[n + w], src=outs[2 * n + w], land=outs[3 * n + w], scatter=scatter[w])
               for w in range(n)]
    return handles, outs[4 * n]


def _exchange_wait(handles, after, name):
    n = len(handles)
    scatter = [h["scatter"] for h in handles]

    def body(*refs):
        ins, lands = refs[:n], refs[n:2 * n]
        send_sems, recv_sems = refs[2 * n:3 * n], refs[3 * n:4 * n]
        for cp in _peer_copies(ins, lands, send_sems, recv_sems, scatter, False):
            cp.wait_send()
        for cp in _peer_copies(ins, lands, send_sems, recv_sems, scatter, True):
            cp.wait_recv()

    srcs, lands = [h["src"] for h in handles], [h["land"] for h in handles]
    outs = pl.pallas_call(
        body,
        in_specs=[_HBM] * (2 * n) + [_SEM] * (2 * n) + [pl.BlockSpec(memory_space=pl.ANY)],
        out_specs=[_HBM] * (2 * n),
        out_shape=[pltpu.HBM(a.shape, a.dtype) for a in srcs + lands],
        input_output_aliases={i: i for i in range(2 * n)},
        compiler_params=pltpu.CompilerParams(has_side_effects=_DATAFLOW),
        name=name,
    )(*srcs, *lands, *[h["send"] for h in handles], *[h["recv"] for h in handles], after)
    me = _my_index()
    full = []
    for src, land, sc in zip(outs[:n], outs[n:], scatter):
        own = lax.dynamic_index_in_dim(src, me, 0, keepdims=True) if sc else src[None]
        full.append(lax.dynamic_update_slice(land, own, (me,) + (0,) * (land.ndim - 1)))
    return full


def _adamw(parts, w, m, v, name):
    rows, cols = w.shape
    tr = rows // 4 if rows % 32 == 0 else rows

    def body(p_ref, w_ref, m_ref, v_ref, g_ref, d_ref, mo_ref, vo_ref):
        g = p_ref[0].astype(F32)
        for d in range(1, N_DEV):
            g = g + p_ref[d].astype(F32)
        g_ref[...] = g
        d_ref[...], mo_ref[...], vo_ref[...] = _adam_update(g, w_ref[...], m_ref[...], v_ref[...])

    blk = pl.BlockSpec((tr, cols), lambda i: (i, 0))
    return pl.pallas_call(
        body,
        grid=(rows // tr,),
        in_specs=[pl.BlockSpec((N_DEV, tr, cols), lambda i: (0, i, 0)), blk, blk, blk],
        out_specs=[blk] * 4,
        out_shape=[SDS((rows, cols), F32)] * 4,
        compiler_params=_cparams("arbitrary"),
        name=name,
    )(parts, w, m, v)


_ROW_OF = dict(norm1_g=(0, D_MODEL), norm2_g=(1, D_MODEL), final_g=(2, D_MODEL), pool_scale=(3, POOL_W),
               b_forget=(4, N_HEADS), loss=(5, 1))


def _pack_rows(vals):
    rows = [jnp.pad(vals[n].reshape(1, width).astype(F32), ((0, 0), (0, D_MODEL - width)))
            for n, (_, width) in sorted(_ROW_OF.items(), key=lambda kv: kv[1][0])]
    return jnp.concatenate(rows + [jnp.zeros((8 - len(rows), D_MODEL), F32)], axis=0)


def _adam_update(g, w, m, v):
    m_new = ADAM_B1 * m + (1.0 - ADAM_B1) * g
    v_new = ADAM_B2 * v + (1.0 - ADAM_B2) * (g * g)
    m_hat = m_new / (1.0 - ADAM_B1 ** ADAM_STEP)
    v_hat = v_new / (1.0 - ADAM_B2 ** ADAM_STEP)
    return -ADAM_LR * (m_hat / (jnp.sqrt(v_hat) + ADAM_EPS) + ADAM_WD * w), m_new, v_new


def _adamw_replicated(parts_rows, parts_pool, w, m, v):
    names = ("norm1_g", "norm2_g", "final_g", "pool_scale", "b_forget", "w_pool")
    shapes = {n: ((len(POOL_WINDOWS), POOL_G, POOL_G) if n == "w_pool" else (1, _ROW_OF[n][1])) for n in names}

    def body(rows_ref, pool_ref, *refs):
        ins, outs = refs[:3 * len(names)], refs[3 * len(names):]

        def total(n):
            if n == "w_pool":
                pieces = [pool_ref[d] for d in range(N_DEV)]
            else:
                row, width = _ROW_OF[n]
                pieces = [rows_ref[d, row:row + 1, 0:width] for d in range(N_DEV)]
            g = pieces[0]
            for p in pieces[1:]:
                g = g + p
            return g

        outs[0][...] = total("loss")
        for k, n in enumerate(names):
            g = total(n)
            delta, m_new, v_new = _adam_update(g, ins[3 * k][...], ins[3 * k + 1][...], ins[3 * k + 2][...])
            for o_ref, val in zip(outs[1 + 4 * k:5 + 4 * k], (g, delta, m_new, v_new)):
                o_ref[...] = val

    args = [d[n].reshape(shapes[n]) for n in names for d in (w, m, v)]
    res = pl.pallas_call(
        body,
        out_shape=[SDS((1, 1), F32)] + [SDS(shapes[n], F32) for n in names for _ in range(4)],
        compiler_params=_cparams(),
        name="adamw_replicated",
    )(parts_rows, parts_pool, *args)
    return res[0], {n: [r.reshape(w[n].shape) for r in res[1 + 4 * k:5 + 4 * k]] for k, n in enumerate(names)}


def kernel(x, norm1_g, w_in, b_forget, w_pool, pool_scale, w_out, norm2_g, w_gate, w_up, w_down, final_g, loss_target, m_norm1_g, m_w_in, m_b_forget, m_w_pool, m_pool_scale, m_w_out, m_norm2_g, m_w_gate, m_w_up, m_w_down, m_final_g, v_norm1_g, v_w_in, v_b_forget, v_w_pool, v_pool_scale, v_w_out, v_norm2_g, v_w_gate, v_w_up, v_w_down, v_final_g):
    big = ("w_in", "w_out", "w_gate", "w_up", "w_down")
    order = ("norm1_g", "w_in", "b_forget", "w_pool", "pool_scale", "w_out", "norm2_g", "w_gate", "w_up", "w_down",
             "final_g")
    w = dict(norm1_g=norm1_g, w_in=w_in, b_forget=b_forget, w_pool=w_pool, pool_scale=pool_scale, w_out=w_out,
             norm2_g=norm2_g, w_gate=w_gate, w_up=w_up, w_down=w_down, final_g=final_g)
    m = dict(norm1_g=m_norm1_g, w_in=m_w_in, b_forget=m_b_forget, w_pool=m_w_pool, pool_scale=m_pool_scale,
             w_out=m_w_out, norm2_g=m_norm2_g, w_gate=m_w_gate, w_up=m_w_up, w_down=m_w_down, final_g=m_final_g)
    v = dict(norm1_g=v_norm1_g, w_in=v_w_in, b_forget=v_b_forget, w_pool=v_w_pool, pool_scale=v_pool_scale,
             w_out=v_w_out, norm2_g=v_norm2_g, w_gate=v_w_gate, w_up=v_w_up, w_down=v_w_down, final_g=v_final_g)

    flipped = ("w_in", "w_gate", "w_up")
    shard = lambda d, n: d[n][0].T if n in flipped else d[n][0]
    gather, started = _exchange_start([shard(w, n).astype(BF16) for n in big], [False] * len(big), "gather_start")
    gather = dict(zip(big, gather))

    def gathered(names, after):
        return _exchange_wait([gather[n] for n in names], after, "gather_wait_" + names[0])

    def weight(name, after):
        if name == "w_in":
            full = gathered(["w_in"], after)[0].reshape(IN_W, D_MODEL)
            f0 = QKV_W + N_HEADS
            return jnp.concatenate([full[:QKV_W], full[f0:], full[QKV_W:f0],
                                    jnp.zeros((IN_PAD - IN_W, D_MODEL), BF16)], axis=0)
        if name == "w_out":
            return gathered(["w_out"], after)[0].reshape(D_MODEL, D_MODEL)
        if name == "w_gate_up":
            return [g.reshape(D_FF, D_MODEL) for g in gathered(["w_gate", "w_up"], after)]
        return gathered(["w_down"], after)[0].reshape(D_FF, D_MODEL)

    rows = lambda g: g.reshape(N_DEV, g.shape[0] // N_DEV, g.shape[1])
    sent = {}

    def emit(name, grad):
        if name == "w_gate_up":
            names, slots = ["w_gate", "w_up"], [rows(g) for g in grad]
        else:
            names, slots = [name], [rows(grad).astype(BF16) if name == "w_in" else rows(grad)]
        handles, token = _exchange_start(slots, [True] * len(slots), "grads_start_" + name)
        sent.update(zip(names, handles))
        return token

    loss_row, dx, small_grads = _local_step(x[0], loss_target[0], w, weight, emit, started)

    packed = _pack_rows(dict(small_grads, loss=0.5 / D_MODEL * jnp.sum(loss_row)))
    small_handles, after = _exchange_start([packed, small_grads["w_pool"]], [False, False], "grads_start_replicated")

    outs = {}
    for name in ("w_down", "w_gate", "w_up", "w_out", "w_in"):
        (parts,) = _exchange_wait([sent[name]], after, "grads_wait_" + name)
        outs[name] = _adamw(parts, shard(w, name), shard(m, name), shard(v, name), "adamw_" + name)
        after = outs[name][0]
        outs[name] = [(a.T if name in flipped else a)[None] for a in outs[name]]
    parts_rows, parts_pool = _exchange_wait(small_handles, after, "grads_wait_replicated")
    loss, small = _adamw_replicated(parts_rows, parts_pool, w, m, v)
    outs.update(small)

    return (loss.reshape(()), dx[None]) + tuple(outs[n][k] for k in range(4) for n in order)
```

```python
import functools

import jax
import jax.numpy as jnp
from jax import lax
from jax.experimental import pallas as pl
from jax.experimental.pallas import tpu as pltpu

F32 = jnp.float32
BF16 = jnp.bfloat16
SDS = jax.ShapeDtypeStruct

D_MODEL = 1024
ATTN_W = 512
N_HEADS = 8
HEAD_DIM = 64
N_PAIRS = N_HEADS // 2
POOL_W = 512
POOL_WINDOWS = (2, 4, 8, 16)
POOL_G = 128
HALO = 16
IN_W = 3 * ATTN_W + N_HEADS + POOL_W
QKV_W = 3 * ATTN_W
U_OFF = QKV_W
F_OFF = QKV_W + POOL_W
IN_PAD = F_OFF + 128
D_FF = 2816
EPS = 1e-6
NEG = -1e30
N_DEV = 8
LANES = 128

ADAM_LR = 0.001
ADAM_B1 = 0.9
ADAM_B2 = 0.999
ADAM_EPS = 1e-08
ADAM_WD = 0.01
ADAM_STEP = 10

VMEM_LIMIT_BYTES = 56 * 1024 * 1024
MESH = pl.DeviceIdType.MESH
NT = (((1,), (1,)), ((), ()))
TN = (((0,), (0,)), ((), ()))


def _cparams(*sem):
    return pltpu.CompilerParams(dimension_semantics=sem or None, vmem_limit_bytes=VMEM_LIMIT_BYTES)


def _split3(a):
    hi = a.astype(BF16)
    r1 = a - hi.astype(F32)
    mid = r1.astype(BF16)
    lo = (r1 - mid.astype(F32)).astype(BF16)
    return hi, mid, lo


def _dot_sel(a, sel, dims=None):
    sb = sel.astype(BF16)
    if dims is None:
        return sum(jnp.dot(p, sb, preferred_element_type=F32) for p in _split3(a))
    return sum(lax.dot_general(p, sb, dims, preferred_element_type=F32) for p in _split3(a))


def _sel_dot(sel, a, dims=None):
    sb = sel.astype(BF16)
    if dims is None:
        return sum(jnp.dot(sb, p, preferred_element_type=F32) for p in _split3(a))
    return sum(lax.dot_general(sb, p, dims, preferred_element_type=F32) for p in _split3(a))


def _iota2(shape, dim):
    return lax.broadcasted_iota(jnp.int32, shape, dim)


def _norm1(x, g1, *, tm):
    s = x.shape[0]

    def body(x_ref, g_ref, h_ref, r_ref):
        xv = x_ref[...]
        r = lax.rsqrt(jnp.mean(xv * xv, axis=-1, keepdims=True) + EPS)
        h_ref[...] = (xv * r * g_ref[...]).astype(BF16)
        r_ref[...] = r

    row = lambda w: pl.BlockSpec((tm, w), lambda i: (i, 0))
    return pl.pallas_call(
        body,
        grid=(s // tm,),
        in_specs=[row(D_MODEL), pl.BlockSpec((1, D_MODEL), lambda i: (0, 0))],
        out_specs=[row(D_MODEL), row(1)],
        out_shape=[SDS((s, D_MODEL), BF16), SDS((s, 1), F32)],
        compiler_params=_cparams("arbitrary"),
        name="norm1",
    )(x, g1)


def _in_proj(h, w_in_t, *, tm):
    s = h.shape[0]

    def body(h_ref, w_ref, qkv_ref, u_ref, fl_ref):
        h = h_ref[...]
        qkv_ref[...] = lax.dot_general(h, w_ref[0:QKV_W, :], NT, preferred_element_type=F32).astype(BF16)
        u_ref[...] = lax.dot_general(h, w_ref[U_OFF:F_OFF, :], NT, preferred_element_type=F32)
        fl_ref[...] = lax.dot_general(h, w_ref[F_OFF:IN_PAD, :], NT, preferred_element_type=F32)

    row = lambda w: pl.BlockSpec((tm, w), lambda i: (i, 0))
    return pl.pallas_call(
        body,
        grid=(s // tm,),
        in_specs=[row(D_MODEL), pl.BlockSpec((IN_PAD, D_MODEL), lambda i: (0, 0))],
        out_specs=[row(QKV_W), row(POOL_W), row(LANES)],
        out_shape=[SDS((s, QKV_W), BF16), SDS((s, POOL_W), F32), SDS((s, LANES), F32)],
        compiler_params=_cparams("arbitrary"),
        name="in_proj",
    )(h, w_in_t)


def _head_block_masks(rows, nb):
    shift = nb.bit_length() - 1
    rr, cc = _iota2((rows, rows), 0), _iota2((rows, rows), 1)
    same = lax.shift_right_logical(rr, shift) == lax.shift_right_logical(cc, shift)
    return rr, cc, same


def _forget_cumsum(fl_t, b_rows):
    rows = fl_t.shape[0]
    nb = rows // N_HEADS

    def body(fl_ref, b_ref, c_ref):
        z = fl_ref[...] + b_ref[...]
        lf = jnp.minimum(z, 0.0) - jnp.log1p(jnp.exp(-jnp.abs(z)))
        upper = _iota2((LANES, LANES), 0) <= _iota2((LANES, LANES), 1)
        within = _dot_sel(lf, upper)
        tot = _dot_sel(lf, jnp.ones((LANES, LANES), F32))
        rr, cc, same = _head_block_masks(rows, nb)
        c_ref[...] = within + _sel_dot(same & (cc < rr), tot)

    return pl.pallas_call(body, out_shape=SDS(fl_t.shape, F32), compiler_params=_cparams(), name="forget_cumsum")(
        fl_t, b_rows)


BIAS_LANES = 3


def _augment(t, h, col, col_first):
    n = t.shape[0]
    lane = _iota2((n, LANES), 1)
    own = (lane < HEAD_DIM) if h == 0 else (lane >= HEAD_DIM)
    b0 = HEAD_DIM if h == 0 else 0
    c0, o0 = (b0, b0 + BIAS_LANES) if col_first else (b0 + BIAS_LANES, b0)
    x = jnp.where(own, t, 0.0)
    for off, piece in enumerate(_split3(col)):
        x = jnp.where(lane == c0 + off, piece.astype(F32), x)
    x = jnp.where((lane >= o0) & (lane < o0 + BIAS_LANES), 1.0, x)
    return x.astype(BF16)


def _attn_fwd(qkv, c_col, *, tk):
    s = qkv.shape[0]
    tq = 2 * tk
    nb = s // tk

    def body(q_ref, k_ref, v_ref, cq_ref, ck_ref, o_ref, lse_ref, kp_ref, vt_ref, st_ref):
        i = pl.program_id(1)

        @pl.when(i == 0)
        def _():
            def prep(jb, _):
                st = pl.multiple_of(jb * tk, tk)
                k2 = k_ref[pl.ds(st, tk), :].astype(F32)
                ck = ck_ref[pl.ds(st, tk), :]
                for h in range(2):
                    kp_ref[h * nb + jb] = _augment(k2, h, -ck[:, h:h + 1], True)
                vt_ref[jb] = v_ref[pl.ds(st, tk), :].astype(F32).T.astype(BF16)
                return 0

            lax.fori_loop(0, nb, prep, 0)

        qs = q_ref[...].astype(F32) * 0.125
        cq = cq_ref[...]
        qp = [_augment(qs, h, cq[:, h:h + 1], False) for h in range(2)]

        def logits(j):
            return tuple(lax.dot_general(kp_ref[h * nb + j], qp[h], NT, preferred_element_type=F32) for h in range(2))

        def softmax_pv(j, slot, stats, masked):
            out = []
            for h in range(2):
                m, l, acc = stats[h]
                st = st_ref[2 * slot + h]
                if masked:
                    st = jnp.where(j * tk + _iota2((tk, tq), 0) <= i * tq + _iota2((tk, tq), 1), st, NEG)
                m_new = jnp.maximum(m, jnp.max(st, axis=0, keepdims=True))
                alpha = jnp.exp(m - m_new)
                p = jnp.exp(st - m_new)
                l = alpha * l + jnp.sum(p, axis=0, keepdims=True)
                vt = vt_ref[j, h * HEAD_DIM:(h + 1) * HEAD_DIM, :]
                acc = alpha * acc + jnp.dot(vt, p.astype(BF16), preferred_element_type=F32)
                out.append((m_new, l, acc))
            return tuple(out)

        def put(slot, j):
            for h, st in enumerate(logits(j)):
                st_ref[2 * slot + h] = st

        def pair(t, stats):
            put(1, 2 * t + 1)
            stats = softmax_pv(2 * t, 0, stats, False)
            put(0, 2 * t + 2)
            return softmax_pv(2 * t + 1, 1, stats, False)

        init = tuple((jnp.full((1, tq), NEG, F32), jnp.zeros((1, tq), F32), jnp.zeros((HEAD_DIM, tq), F32))
                     for _ in range(2))
        put(0, 0)
        stats = lax.fori_loop(0, i, pair, init)
        put(1, 2 * i + 1)
        stats = softmax_pv(2 * i, 0, stats, True)
        (ma, la, acca), (mb, lb, accb) = softmax_pv(2 * i + 1, 1, stats, True)
        o_ref[...] = jnp.concatenate([acca / la, accb / lb], axis=0).T.astype(BF16)
        lse_ref[...] = jnp.where(_iota2((2, tq), 0) == 0, ma + jnp.log(la), mb + jnp.log(lb))

    return pl.pallas_call(
        body,
        grid=(N_PAIRS, s // tq),
        in_specs=[
            pl.BlockSpec((tq, LANES), lambda p, i: (i, p)),
            pl.BlockSpec((s, LANES), lambda p, i: (0, N_PAIRS + p)),
            pl.BlockSpec((s, LANES), lambda p, i: (0, 2 * N_PAIRS + p)),
            pl.BlockSpec((None, tq, 2), lambda p, i: (p, i, 0)),
            pl.BlockSpec((None, s, 2), lambda p, i: (p, 0, 0)),
        ],
        out_specs=[
            pl.BlockSpec((tq, LANES), lambda p, i: (i, p)),
            pl.BlockSpec((None, None, 2, tq), lambda p, i: (p, i, 0, 0)),
        ],
        out_shape=[SDS((s, ATTN_W), BF16), SDS((N_PAIRS, s // tq, 2, tq), F32)],
        scratch_shapes=[pltpu.VMEM((2 * nb, tk, LANES), BF16), pltpu.VMEM((nb, LANES, tk), BF16),
                        pltpu.VMEM((4, tk, tq), F32)],
        compiler_params=_cparams("arbitrary", "arbitrary"),
        name="attn_fwd",
    )(qkv, qkv, qkv, c_col, c_col)


def _pool_counts(row0, tm, w):
    t = row0 + _iota2((tm, 1), 0)
    return jnp.minimum(t + 1, w).astype(F32)


def _pool_fwd(u, w_pool, pool_scale, *, tm):
    s = u.shape[0]

    def body(u_ref, w_ref, sc_ref, pooled_ref, po_ref, tail_ref):
        i = pl.program_id(0)

        @pl.when(i == 0)
        def _():
            tail_ref[...] = jnp.zeros_like(tail_ref)

        uv = u_ref[...]
        ext = jnp.concatenate([tail_ref[...], uv], axis=0)
        tail_ref[...] = uv[tm - HALO:, :]
        for g, w in enumerate(POOL_WINDOWS):
            cols = slice(g * POOL_G, (g + 1) * POOL_G)
            acc = ext[:, cols]
            k = 1
            while k < w:
                acc = acc + pltpu.roll(acc, k, axis=0)
                k *= 2
            pooled = (acc[HALO:, :] / _pool_counts(i * tm, tm, w) - uv[:, cols]).astype(BF16)
            pooled_ref[:, cols] = pooled
            mixed = jnp.dot(pooled, w_ref[g].astype(BF16), preferred_element_type=F32)
            po_ref[:, cols] = (mixed * sc_ref[:, cols]).astype(BF16)

    row = pl.BlockSpec((tm, POOL_W), lambda i: (i, 0))
    return pl.pallas_call(
        body,
        grid=(s // tm,),
        in_specs=[row, pl.BlockSpec((len(POOL_WINDOWS), POOL_G, POOL_G), lambda i: (0, 0, 0)),
                  pl.BlockSpec((1, POOL_W), lambda i: (0, 0))],
        out_specs=[row, row],
        out_shape=[SDS((s, POOL_W), BF16), SDS((s, POOL_W), BF16)],
        scratch_shapes=[pltpu.VMEM((HALO, POOL_W), F32)],
        compiler_params=_cparams("arbitrary"),
        name="pool_fwd",
    )(u, w_pool, pool_scale)


def _out_norm2(attn_o, pool_o, w_out, x, g2, *, tm):
    s = x.shape[0]

    def body(a_ref, p_ref, w_ref, x_ref, g_ref, x1_ref, h2_ref, r_ref):
        x1 = (x_ref[...] + jnp.dot(a_ref[...], w_ref[0:ATTN_W, :], preferred_element_type=F32)
              + jnp.dot(p_ref[...], w_ref[ATTN_W:, :], preferred_element_type=F32))
        r = lax.rsqrt(jnp.mean(x1 * x1, axis=-1, keepdims=True) + EPS)
        x1_ref[...] = x1
        r_ref[...] = r
        h2_ref[...] = (x1 * r * g_ref[...]).astype(BF16)

    row = lambda w: pl.BlockSpec((tm, w), lambda i: (i, 0))
    full = lambda a, b: pl.BlockSpec((a, b), lambda i: (0, 0))
    return pl.pallas_call(
        body,
        grid=(s // tm,),
        in_specs=[row(ATTN_W), row(POOL_W), full(D_MODEL, D_MODEL), row(D_MODEL), full(1, D_MODEL)],
        out_specs=[row(D_MODEL), row(D_MODEL), row(1)],
        out_shape=[SDS((s, D_MODEL), F32), SDS((s, D_MODEL), BF16), SDS((s, 1), F32)],
        compiler_params=_cparams("arbitrary"),
        name="out_norm2",
    )(attn_o, pool_o, w_out, x, g2)


def _gate_up(h2, wg_t, wu_t, *, tm, tn):
    s = h2.shape[0]

    def body(h_ref, wg_ref, wu_ref, gate_ref, up_ref, act_ref):
        h = h_ref[...]
        gate = lax.dot_general(h, wg_ref[...], NT, preferred_element_type=F32)
        up = lax.dot_general(h, wu_ref[...], NT, preferred_element_type=F32)
        gate_ref[...] = gate.astype(BF16)
        up_ref[...] = up.astype(BF16)
        act_ref[...] = (gate * jax.nn.sigmoid(gate) * up).astype(BF16)

    wspec = pl.BlockSpec((tn, D_MODEL), lambda c, r: (c, 0))
    ospec = pl.BlockSpec((tm, tn), lambda c, r: (r, c))
    return pl.pallas_call(
        body,
        grid=(D_FF // tn, s // tm),
        in_specs=[pl.BlockSpec((tm, D_MODEL), lambda c, r: (r, 0)), wspec, wspec],
        out_specs=[ospec, ospec, ospec],
        out_shape=[SDS((s, D_FF), BF16), SDS((s, D_FF), BF16), SDS((s, D_FF), BF16)],
        compiler_params=_cparams("arbitrary", "arbitrary"),
        name="gate_up",
    )(h2, wg_t, wu_t)


def _down_final(act, wd, x1, gf, tgt, *, tm):
    s = x1.shape[0]

    def body(a_ref, w_ref, x1_ref, g_ref, t_ref, dx2_ref, loss_ref, dgf_ref):
        @pl.when(pl.program_id(0) == 0)
        def _():
            loss_ref[...] = jnp.zeros_like(loss_ref)
            dgf_ref[...] = jnp.zeros_like(dgf_ref)

        x2 = x1_ref[...] + jnp.dot(a_ref[...], w_ref[...], preferred_element_type=F32)
        r = lax.rsqrt(jnp.mean(x2 * x2, axis=-1, keepdims=True) + EPS)
        xn = x2 * r
        g = g_ref[...]
        diff = xn * g - t_ref[...]
        loss_ref[...] += jnp.sum(diff * diff, axis=0, keepdims=True)
        dy = diff * (1.0 / D_MODEL)
        dgf_ref[...] += jnp.sum(dy * xn, axis=0, keepdims=True)
        dxn = dy * g
        dx2_ref[...] = r * (dxn - xn * jnp.mean(dxn * xn, axis=-1, keepdims=True))

    row = lambda w: pl.BlockSpec((tm, w), lambda i: (i, 0))
    full = lambda a, b: pl.BlockSpec((a, b), lambda i: (0, 0))
    return pl.pallas_call(
        body,
        grid=(s // tm,),
        in_specs=[row(D_FF), full(D_FF, D_MODEL), row(D_MODEL), full(1, D_MODEL), row(D_MODEL)],
        out_specs=[row(D_MODEL), full(1, D_MODEL), full(1, D_MODEL)],
        out_shape=[SDS((s, D_MODEL), F32), SDS((1, D_MODEL), F32), SDS((1, D_MODEL), F32)],
        compiler_params=_cparams("arbitrary"),
        name="down_final",
    )(act, wd, x1, gf, tgt)


def _swiglu_bwd(dx2, wd, gate, up, *, tm, tn):
    s = dx2.shape[0]

    def body(d_ref, w_ref, gate_ref, up_ref, dgate_ref, dup_ref):
        dact = lax.dot_general(d_ref[...].astype(BF16), w_ref[...], NT, preferred_element_type=F32)
        gate = gate_ref[...].astype(F32)
        sg = jax.nn.sigmoid(gate)
        dup_ref[...] = (dact * (gate * sg)).astype(BF16)
        dgate_ref[...] = (dact * up_ref[...].astype(F32) * (sg * (1.0 + gate * (1.0 - sg)))).astype(BF16)

    ospec = pl.BlockSpec((tm, tn), lambda c, r: (r, c))
    return pl.pallas_call(
        body,
        grid=(D_FF // tn, s // tm),
        in_specs=[pl.BlockSpec((tm, D_MODEL), lambda c, r: (r, 0)), pl.BlockSpec((tn, D_MODEL), lambda c, r: (c, 0)),
                  ospec, ospec],
        out_specs=[ospec, ospec],
        out_shape=[SDS((s, D_FF), BF16), SDS((s, D_FF), BF16)],
        compiler_params=_cparams("arbitrary", "arbitrary"),
        name="swiglu_bwd",
    )(dx2, wd, gate, up)


def _mm_tn(a, bs, *, ta, ts, name):
    s, ka = a.shape
    n = len(bs)

    def body(a_ref, *refs):
        b_refs, o_refs = refs[:n], refs[n:]

        @pl.when(pl.program_id(1) == 0)
        def _():
            for o_ref in o_refs:
                o_ref[...] = jnp.zeros_like(o_ref)

        av = a_ref[...].astype(BF16)
        for b_ref, o_ref in zip(b_refs, o_refs):
            o_ref[...] += lax.dot_general(av, b_ref[...].astype(BF16), TN, preferred_element_type=F32)

    return pl.pallas_call(
        body,
        grid=(ka // ta, s // ts),
        in_specs=[pl.BlockSpec((ts, ta), lambda i, k: (k, i))]
        + [pl.BlockSpec((ts, b.shape[1]), lambda i, k: (k, 0)) for b in bs],
        out_specs=[pl.BlockSpec((ta, b.shape[1]), lambda i, k: (i, 0)) for b in bs],
        out_shape=[SDS((ka, b.shape[1]), F32) for b in bs],
        compiler_params=_cparams("arbitrary", "arbitrary"),
        name=name,
    )(a, *bs)


def _mm_tn_shared(as_, b, *, ts, name):
    s, nb_ = b.shape
    n = len(as_)

    def body(*refs):
        a_refs, b_ref, o_refs = refs[:n], refs[n], refs[n + 1:]

        @pl.when(pl.program_id(0) == 0)
        def _():
            for o_ref in o_refs:
                o_ref[...] = jnp.zeros_like(o_ref)

        bv = b_ref[...].astype(BF16)
        for a_ref, o_ref in zip(a_refs, o_refs):
            o_ref[...] += lax.dot_general(a_ref[...].astype(BF16), bv, TN, preferred_element_type=F32)

    return pl.pallas_call(
        body,
        grid=(s // ts,),
        in_specs=[pl.BlockSpec((ts, a.shape[1]), lambda k: (k, 0)) for a in as_] + [pl.BlockSpec((ts, nb_), lambda k: (k, 0))],
        out_specs=[pl.BlockSpec((a.shape[1], nb_), lambda k: (0, 0)) for a in as_],
        out_shape=[SDS((a.shape[1], nb_), F32) for a in as_],
        compiler_params=_cparams("arbitrary"),
        name=name,
    )(*as_, b)


def _norm_bwd(dh, x, r, g, dres):
    xn = x * r
    dxn = dh * g
    dx = dres + r * (dxn - xn * jnp.mean(dxn * xn, axis=-1, keepdims=True))
    return dx, jnp.sum(dh * xn, axis=0, keepdims=True)


def _mlp_in_bwd(dgate, dup, wg_t, wu_t, w_out, x1, r2, g2, dx2, *, tm):
    s = x1.shape[0]

    def body(dg_ref, du_ref, wg_ref, wu_ref, wo_ref, x_ref, r_ref, g_ref, d_ref, dx1_ref, dmix_ref, dg2_ref):
        @pl.when(pl.program_id(0) == 0)
        def _():
            dg2_ref[...] = jnp.zeros_like(dg2_ref)

        dh2 = (jnp.dot(dg_ref[...], wg_ref[...], preferred_element_type=F32)
               + jnp.dot(du_ref[...], wu_ref[...], preferred_element_type=F32))
        dx1, dg2 = _norm_bwd(dh2, x_ref[...], r_ref[...], g_ref[...], d_ref[...])
        dg2_ref[...] += dg2
        dx1_ref[...] = dx1
        dmix_ref[...] = lax.dot_general(dx1.astype(BF16), wo_ref[...], NT, preferred_element_type=F32)

    row = lambda w: pl.BlockSpec((tm, w), lambda i: (i, 0))
    full = lambda a, b: pl.BlockSpec((a, b), lambda i: (0, 0))
    return pl.pallas_call(
        body,
        grid=(s // tm,),
        in_specs=[row(D_FF), row(D_FF), full(D_FF, D_MODEL), full(D_FF, D_MODEL), full(D_MODEL, D_MODEL),
                  row(D_MODEL), row(1), full(1, D_MODEL), row(D_MODEL)],
        out_specs=[row(D_MODEL), row(D_MODEL), full(1, D_MODEL)],
        out_shape=[SDS((s, D_MODEL), F32), SDS((s, D_MODEL), F32), SDS((1, D_MODEL), F32)],
        compiler_params=_cparams("arbitrary"),
        name="mlp_in_bwd",
    )(dgate, dup, wg_t, wu_t, w_out, x1, r2, g2, dx2)


def _pool_bwd(dmixed, pooled, w_pool, pool_scale, *, tm):
    s = pooled.shape[0]
    nt = s // tm
    ng = len(POOL_WINDOWS)

    def body(d_ref, p_ref, w_ref, sc_ref, du_ref, dw_ref, dsc_ref, head_ref):
        i = pl.program_id(0)

        @pl.when(i == 0)
        def _():
            head_ref[...] = jnp.zeros_like(head_ref)
            dw_ref[...] = jnp.zeros_like(dw_ref)
            dsc_ref[...] = jnp.zeros_like(dsc_ref)

        row0 = (nt - 1 - i) * tm
        for g, w in enumerate(POOL_WINDOWS):
            cols = slice(g * POOL_G, (g + 1) * POOL_G)
            wb = w_ref[g].astype(BF16)
            pooled_g = p_ref[:, cols]
            dpo = d_ref[:, cols]
            mixed = jnp.dot(pooled_g, wb, preferred_element_type=F32)
            dsc_ref[:, cols] += jnp.sum(dpo * mixed, axis=0, keepdims=True)
            dmp = (dpo * sc_ref[:, cols]).astype(BF16)
            dw_ref[g] += lax.dot_general(pooled_g, dmp, TN, preferred_element_type=F32)
            dpooled = lax.dot_general(dmp, wb, NT, preferred_element_type=F32)
            a = dpooled / _pool_counts(row0, tm, w)
            acc = jnp.concatenate([a, head_ref[:, cols]], axis=0)
            head_ref[:, cols] = a[0:HALO, :]
            k = 1
            while k < w:
                acc = acc + pltpu.roll(acc, tm + HALO - k, axis=0)
                k *= 2
            du_ref[:, cols] = (acc[0:tm, :] - dpooled).astype(BF16)

    rev = lambda i: (nt - 1 - i, 0)
    return pl.pallas_call(
        body,
        grid=(nt,),
        in_specs=[pl.BlockSpec((tm, POOL_W), lambda i: (nt - 1 - i, 1)), pl.BlockSpec((tm, POOL_W), rev),
                  pl.BlockSpec((ng, POOL_G, POOL_G), lambda i: (0, 0, 0)), pl.BlockSpec((1, POOL_W), lambda i: (0, 0))],
        out_specs=[pl.BlockSpec((tm, POOL_W), rev), pl.BlockSpec((ng, POOL_G, POOL_G), lambda i: (0, 0, 0)),
                   pl.BlockSpec((1, POOL_W), lambda i: (0, 0))],
        out_shape=[SDS((s, POOL_W), BF16), SDS((ng, POOL_G, POOL_G), F32), SDS((1, POOL_W), F32)],
        scratch_shapes=[pltpu.VMEM((HALO, POOL_W), F32)],
        compiler_params=_cparams("arbitrary"),
        name="pool_bwd",
    )(dmixed, pooled, w_pool, pool_scale)


SUM_ROWS = 16


def _heads_t(t):
    n = t.shape[0]
    lane = _iota2((n, LANES), 1)
    tf = t.astype(F32)
    halves = jnp.concatenate([jnp.where(lane < HEAD_DIM, tf, 0.0).T, jnp.where(lane < HEAD_DIM, 0.0, tf).T], axis=1)
    r, c = _iota2((SUM_ROWS, 2 * n), 0), _iota2((SUM_ROWS, 2 * n), 1)
    ones = jnp.where(((r == 0) & (c < n)) | ((r == 4) & (c >= n)), 1.0, 0.0)
    return jnp.concatenate([halves, ones], axis=0).astype(BF16)


def _attn_bwd(qkv, attn_o, dmixed, rowb, ck_col, *, tq):
    s = qkv.shape[0]
    tk = tq
    nb = s // tq
    rows_t = LANES + SUM_ROWS

    def body(q_ref, k_ref, v_ref, o_ref, do_ref, rowb_ref, ck_ref, dq_ref, dk_ref, dv_ref, dck_ref, dcq_ref,
             dqt_ref, delta_ref, kp_ref, qp_ref, dob_ref, qt_ref, kt_ref, dot_ref, front_ref):
        lane = _iota2((tq, LANES), 1)
        lo = lane < HEAD_DIM
        first = _iota2((8, LANES), 1) < HEAD_DIM
        sel = jnp.where(_iota2((8, LANES), 0) < 4, jnp.where(first, 1.0, 0.0), jnp.where(first, 0.0, 1.0))

        def prep(b, _):
            st = pl.multiple_of(b * tq, tq)
            do2 = do_ref[pl.ds(st, tq), :]
            delta_ref[b] = _sel_dot(sel, do2 * o_ref[pl.ds(st, tq), :].astype(F32), NT)
            dob_ref[pl.ds(st, tq), :] = do2.astype(BF16)
            dqt_ref[b] = jnp.zeros((rows_t, tq), F32)
            k2 = k_ref[pl.ds(st, tq), :].astype(F32)
            q2 = q_ref[pl.ds(st, tq), :].astype(F32)
            ck = ck_ref[pl.ds(st, tq), :]
            for h in range(2):
                kp_ref[h * nb + b] = _augment(k2, h, -ck[:, h:h + 1], True)
                qp_ref[h * nb + b] = _augment(q2 * 0.125, h, jnp.zeros((tq, 1), F32), False)
            qt_ref[b] = _heads_t(q2)
            kt_ref[b] = _heads_t(k2)
            dot_ref[b] = _heads_t(do2)[0:LANES, :]
            return 0

        lax.fori_loop(0, nb, prep, 0)

        def split(t):
            z = jnp.zeros_like(t)
            return jnp.where(lo, t, z), jnp.where(lo, z, t)

        def kv_block(j, _):
            st_j = pl.multiple_of(j * tk, tk)
            vs = split(v_ref[pl.ds(st_j, tk), :])
            kt = kt_ref[j]

            def stage(i, slot):
                ic = jnp.minimum(i, nb - 1)
                do2 = dob_ref[pl.ds(pl.multiple_of(ic * tq, tq), tq), :]
                for h in range(2):
                    front_ref[4 * slot + h] = lax.dot_general(kp_ref[h * nb + j], qp_ref[h * nb + ic], NT,
                                                              preferred_element_type=F32)
                    front_ref[4 * slot + 2 + h] = lax.dot_general(vs[h], do2, NT, preferred_element_type=F32)

            def q_block(i, slot, carry, diagonal):
                dkt, dvt = carry
                ic = jnp.minimum(i, nb - 1)
                rb = rowb_ref[ic] + jnp.where(i < nb, 0.0, NEG)
                dl = delta_ref[ic]
                pts, dsts = [], []
                for h in range(2):
                    st = front_ref[4 * slot + h] + rb[h:h + 1, :]
                    if diagonal:
                        st = jnp.where(_iota2((tk, tq), 0) <= _iota2((tk, tq), 1), st, NEG)
                    pt = jnp.exp(st)
                    pts.append(pt.astype(BF16))
                    dsts.append((pt * (front_ref[4 * slot + 2 + h] - dl[4 * h:4 * h + 1, :])).astype(BF16))
                dvt = dvt + lax.dot_general(dot_ref[ic], jnp.concatenate(pts, axis=1), NT, preferred_element_type=F32)
                dkt = dkt + lax.dot_general(qt_ref[ic], jnp.concatenate(dsts, axis=1), NT, preferred_element_type=F32)
                dqt_ref[ic] += jnp.dot(kt, jnp.concatenate(dsts, axis=0), preferred_element_type=F32)
                return dkt, dvt

            def pair(t, carry):
                i0 = j + 1 + 2 * t
                stage(i0 + 1, 0)
                carry = q_block(i0, 1, carry, False)
                stage(i0 + 2, 1)
                return q_block(i0 + 1, 0, carry, False)

            stage(j, 0)
            stage(j + 1, 1)
            carry = q_block(j, 0, (jnp.zeros((rows_t, tk), F32), jnp.zeros((LANES, tk), F32)), True)
            dkt, dvt = lax.fori_loop(0, lax.shift_right_logical(nb - j, 1), pair, carry)
            dk_ref[pl.ds(st_j, tk), :] = (dkt[0:LANES, :].T * 0.125).astype(BF16)
            dv_ref[pl.ds(st_j, tk), :] = dvt.T.astype(BF16)
            dck_ref[j] = dkt[LANES:LANES + 8, :]
            return 0

        lax.fori_loop(0, nb, kv_block, 0)

        def finish(b, _):
            acc = dqt_ref[b]
            dq_ref[pl.ds(pl.multiple_of(b * tq, tq), tq), :] = (acc[0:LANES, :].T * 0.125).astype(BF16)
            dcq_ref[b] = acc[LANES:LANES + 8, :]
            return 0

        lax.fori_loop(0, nb, finish, 0)

    col = lambda off: pl.BlockSpec((s, LANES), lambda p: (0, off + p))
    sums = pl.BlockSpec((None, nb, 8, tq), lambda p: (p, 0, 0, 0))
    return pl.pallas_call(
        body,
        grid=(N_PAIRS,),
        in_specs=[col(0), col(N_PAIRS), col(2 * N_PAIRS), col(0), col(0),
                  pl.BlockSpec((None, nb, 2, tq), lambda p: (p, 0, 0, 0)),
                  pl.BlockSpec((None, s, 2), lambda p: (p, 0, 0))],
        out_specs=[col(0), col(0), col(0), sums, sums],
        out_shape=[SDS((s, ATTN_W), BF16), SDS((s, ATTN_W), BF16), SDS((s, ATTN_W), BF16),
                   SDS((N_PAIRS, nb, 8, tq), F32), SDS((N_PAIRS, nb, 8, tq), F32)],
        scratch_shapes=[pltpu.VMEM((nb, rows_t, tq), F32), pltpu.VMEM((nb, 8, tq), F32),
                        pltpu.VMEM((2 * nb, tk, LANES), BF16), pltpu.VMEM((2 * nb, tq, LANES), BF16),
                        pltpu.VMEM((s, LANES), BF16), pltpu.VMEM((nb, rows_t, 2 * tq), BF16),
                        pltpu.VMEM((nb, rows_t, 2 * tk), BF16), pltpu.VMEM((nb, LANES, 2 * tq), BF16),
                        pltpu.VMEM((8, tk, tq), F32)],
        compiler_params=_cparams("arbitrary"),
        name="attn_bwd",
    )(qkv, qkv, qkv, attn_o, dmixed, rowb, ck_col)


def _forget_bwd(dc_t, fl_t, b_rows):
    rows = fl_t.shape[0]
    nb = rows // N_HEADS

    def body(dc_ref, fl_ref, b_ref, dfl_ref, db_ref):
        dc = dc_ref[...]
        lower = _iota2((LANES, LANES), 0) >= _iota2((LANES, LANES), 1)
        ones = jnp.ones((LANES, LANES), F32)
        rr, cc, same = _head_block_masks(rows, nb)
        dlf = _dot_sel(dc, lower) + _sel_dot(same & (cc > rr), _dot_sel(dc, ones))
        dfl = dlf / (1.0 + jnp.exp(fl_ref[...] + b_ref[...]))
        dfl_ref[...] = dfl
        shift = nb.bit_length() - 1
        hsel = lax.shift_right_logical(_iota2((N_HEADS, rows), 1), shift) == _iota2((N_HEADS, rows), 0)
        db_ref[...] = _sel_dot(hsel, _dot_sel(dfl, ones))

    return pl.pallas_call(body, out_shape=[SDS(fl_t.shape, F32), SDS((N_HEADS, LANES), F32)],
                          compiler_params=_cparams(), name="forget_bwd")(dc_t, fl_t, b_rows)


def _in_bwd(dq, dk, dv, du, dfl, w_in_t, x, r1, g1, dx1, *, tm):
    s = x.shape[0]
    pieces = ((0, ATTN_W), (ATTN_W, 2 * ATTN_W), (2 * ATTN_W, QKV_W), (U_OFF, F_OFF), (F_OFF, IN_PAD))

    def body(dq_ref, dk_ref, dv_ref, du_ref, df_ref, w_ref, x_ref, r_ref, g_ref, d_ref, dx_ref, dg1_ref):
        @pl.when(pl.program_id(0) == 0)
        def _():
            dg1_ref[...] = jnp.zeros_like(dg1_ref)

        dh = None
        for ref, (c0, c1) in zip((dq_ref, dk_ref, dv_ref, du_ref, df_ref), pieces):
            t = jnp.dot(ref[...], w_ref[c0:c1, :], preferred_element_type=F32)
            dh = t if dh is None else dh + t
        dx, dg1 = _norm_bwd(dh, x_ref[...], r_ref[...], g_ref[...], d_ref[...])
        dx_ref[...] = dx
        dg1_ref[...] += dg1

    row = lambda w: pl.BlockSpec((tm, w), lambda i: (i, 0))
    full = lambda a, b: pl.BlockSpec((a, b), lambda i: (0, 0))
    return pl.pallas_call(
        body,
        grid=(s // tm,),
        in_specs=[row(ATTN_W), row(ATTN_W), row(ATTN_W), row(POOL_W), row(LANES), full(IN_PAD, D_MODEL),
                  row(D_MODEL), row(1), full(1, D_MODEL), row(D_MODEL)],
        out_specs=[row(D_MODEL), full(1, D_MODEL)],
        out_shape=[SDS((s, D_MODEL), F32), SDS((1, D_MODEL), F32)],
        compiler_params=_cparams("arbitrary"),
        name="in_bwd",
    )(dq, dk, dv, du, dfl, w_in_t, x, r1, g1, dx1)


def _tiles(s):
    big = min(512, s)
    return dict(row=big, attn=min(256, s // 2), ff_rows=min(256, s))


def _tie(a, token):
    return a + token[0:1, 0:1].astype(a.dtype)


def _local_step(x, tgt, p, weight, emit, started):
    s = x.shape[0]
    t = _tiles(s)
    tm, tq = t["row"], t["attn"]
    nb = s // LANES
    nqb = s // tq
    g1, g2, gf = p["norm1_g"], p["norm2_g"], p["final_g"].reshape(1, D_MODEL)
    w_pool, pool_scale = p["w_pool"][0], p["pool_scale"]

    h, r1 = _norm1(x, _tie(g1, started), tm=tm)
    w_in_t = weight("w_in", h)
    qkv, u, fl = _in_proj(h, w_in_t, tm=tm)
    fl_t = fl[:, :N_HEADS].T.reshape(N_HEADS * nb, LANES)
    b_rows = jnp.repeat(p["b_forget"].reshape(N_HEADS), nb).reshape(N_HEADS * nb, 1)
    c = _forget_cumsum(fl_t, b_rows).reshape(N_PAIRS, 2, s)
    c_col = c.transpose(0, 2, 1)
    c_rowblk = c.reshape(N_PAIRS, 2, nqb, tq).transpose(0, 2, 1, 3)
    attn_o, lse = _attn_fwd(qkv, c_col, tk=tq)
    lse = lse.reshape(N_PAIRS, nqb // 2, 2, 2, tq).transpose(0, 1, 3, 2, 4).reshape(N_PAIRS, nqb, 2, tq)
    pooled, pool_o = _pool_fwd(u, w_pool, pool_scale, tm=tm)
    w_out = weight("w_out", attn_o)
    x1, h2, r2 = _out_norm2(attn_o, pool_o, w_out, x, g2, tm=tm)
    wg_t, wu_t = weight("w_gate_up", h2)
    gate, up, act = _gate_up(h2, wg_t, wu_t, tm=t["ff_rows"], tn=D_FF)
    wd = weight("w_down", act)
    dx2, loss_row, d_gf = _down_final(act, wd, x1, gf, tgt, tm=tm)

    dgate, dup = _swiglu_bwd(dx2, wd, gate, up, tm=t["ff_rows"], tn=D_FF)
    (d_wd,) = _mm_tn(act, [dx2], ta=D_FF, ts=tm, name="grad_w_down")
    token = emit("w_down", d_wd)
    (d_wg_t,) = _mm_tn(dgate, [h2], ta=D_FF, ts=tm, name="grad_w_gate")
    (d_wu_t,) = _mm_tn(dup, [h2], ta=D_FF, ts=tm, name="grad_w_up")
    token = token + emit("w_gate_up", (d_wg_t, d_wu_t))
    dx1, dmixed, d_g2 = _mlp_in_bwd(dgate, dup, wg_t, wu_t, w_out, x1, r2, _tie(g2, token), dx2, tm=t["ff_rows"])
    du, d_wpool, d_pscale = _pool_bwd(dmixed, pooled, w_pool, pool_scale, tm=tm)
    token = emit("w_out", jnp.concatenate(_mm_tn_shared([attn_o, pool_o], dx1, ts=tm, name="grad_w_out"), axis=0))
    rowb = _tie(c_rowblk - lse, token)
    dq, dk, dv, dck, dcq = _attn_bwd(qkv, attn_o, dmixed, rowb, c_col, tq=tq)
    dc_t = (dcq - dck)[:, :, 0::4, :].transpose(0, 2, 1, 3).reshape(N_HEADS * nb, LANES)
    dfl_t, db = _forget_bwd(dc_t, fl_t, b_rows)
    dfl = jnp.pad(dfl_t.reshape(N_HEADS, s).T, ((0, 0), (0, LANES - N_HEADS))).astype(BF16)
    d_wq, d_wk, d_wv, d_wu_in, d_wf = _mm_tn_shared([dq, dk, dv, du, dfl], h, ts=tm, name="grad_w_in")
    token = emit("w_in", jnp.concatenate([d_wq, d_wk, d_wv, d_wf[:N_HEADS], d_wu_in], axis=0))
    dx, d_g1 = _in_bwd(dq, dk, dv, du, dfl, w_in_t, x, r1, _tie(g1, token), dx1, tm=tm)

    small = dict(norm1_g=d_g1, b_forget=db[:, 0].reshape(1, N_HEADS), w_pool=d_wpool, pool_scale=d_pscale,
                 norm2_g=d_g2, final_g=d_gf)
    return loss_row, dx, small


def _my_index():
    return 4 * lax.axis_index("x") + 2 * lax.axis_index("y") + lax.axis_index("c")


def _peer(k):
    pos = [lax.axis_index(a) for a in ("x", "y", "c")]
    flipped = tuple(1 - p if (k >> b) & 1 else p for p, b in zip(pos, (2, 1, 0)))
    return flipped, 4 * flipped[0] + 2 * flipped[1] + flipped[2]


_HBM = pl.BlockSpec(memory_space=pltpu.HBM)
_SEM = pl.BlockSpec(memory_space=pltpu.SEMAPHORE)
_DATAFLOW = pltpu.SideEffectType.DATAFLOW_SIDE_EFFECTING


def _peer_copies(ins, lands, send_sems, recv_sems, scatter, arrivals):
    me = _my_index()
    copies = []
    for w in range(len(ins)):
        for k in range(1, N_DEV):
            dev, idx = _peer(k)
            copies.append(pltpu.make_async_remote_copy(
                src_ref=ins[w].at[idx] if scatter[w] else ins[w], dst_ref=lands[w].at[idx if arrivals else me],
                send_sem=send_sems[w].at[k - 1], recv_sem=recv_sems[w].at[k - 1], device_id=dev, device_id_type=MESH))
    return copies


def _exchange_start(arrays, scatter, name):
    n = len(arrays)
    land_shapes = [(N_DEV,) + tuple(a.shape[1:] if sc else a.shape) for a, sc in zip(arrays, scatter)]

    def body(*refs):
        ins, lands = refs[:n], refs[n:2 * n]
        send_sems, recv_sems = refs[2 * n:3 * n], refs[3 * n:4 * n]
        token = refs[6 * n]
        for cp in _peer_copies(ins, lands, send_sems, recv_sems, scatter, False):
            cp.start()
        token[...] = jnp.zeros_like(token)

    sem = pltpu.SemaphoreType.DMA((N_DEV - 1,))
    outs = pl.pallas_call(
        body,
        in_specs=[_HBM] * (2 * n),
        out_specs=[_SEM] * (2 * n) + [_HBM] * (2 * n) + [pl.BlockSpec(memory_space=pltpu.VMEM)],
        out_shape=[sem] * (2 * n) + [pltpu.HBM(a.shape, a.dtype) for a in arrays]
        + [pltpu.HBM(sh, a.dtype) for sh, a in zip(land_shapes, arrays)] + [SDS((8, LANES), F32)],
        input_output_aliases={i: 2 * n + i for i in range(2 * n)},
        compiler_params=pltpu.CompilerParams(has_side_effects=_DATAFLOW),
        name=name,
    )(*[pltpu.with_memory_space_constraint(a, pltpu.HBM) for a in arrays],
      *[pltpu.with_memory_space_constraint(lax.empty(sh, a.dtype), pltpu.HBM) for sh, a in zip(land_shapes, arrays)])
    handles = [dict(send=outs[w], recv=outs[n + w], src=outs[2 * n + w], land=outs[3 * n + w], scatter=scatter[w])
               for w in range(n)]
    return handles, outs[4 * n]


def _exchange_wait(handles, after, name):
    n = len(handles)
    scatter = [h["scatter"] for h in handles]

    def body(*refs):
        ins, lands = refs[:n], refs[n:2 * n]
        send_sems, recv_sems = refs[2 * n:3 * n], refs[3 * n:4 * n]
        for cp in _peer_copies(ins, lands, send_sems, recv_sems, scatter, False):
            cp.wait_send()
        for cp in _peer_copies(ins, lands, send_sems, recv_sems, scatter, True):
            cp.wait_recv()

    srcs, lands = [h["src"] for h in handles], [h["land"] for h in handles]
    outs = pl.pallas_call(
        body,
        in_specs=[_HBM] * (2 * n) + [_SEM] * (2 * n) + [pl.BlockSpec(memory_space=pl.ANY)],
        out_specs=[_HBM] * (2 * n),
        out_shape=[pltpu.HBM(a.shape, a.dtype) for a in srcs + lands],
        input_output_aliases={i: i for i in range(2 * n)},
        compiler_params=pltpu.CompilerParams(has_side_effects=_DATAFLOW),
        name=name,
    )(*srcs, *lands, *[h["send"] for h in handles], *[h["recv"] for h in handles], after)
    me = _my_index()
    full = []
    for src, land, sc in zip(outs[:n], outs[n:], scatter):
        own = lax.dynamic_index_in_dim(src, me, 0, keepdims=True) if sc else src[None]
        full.append(lax.dynamic_update_slice(land, own, (me,) + (0,) * (land.ndim - 1)))
    return full


def _adamw(parts, w, m, v, name):
    rows, cols = w.shape
    tr = rows // 4 if rows % 32 == 0 else rows

    def body(p_ref, w_ref, m_ref, v_ref, g_ref, d_ref, mo_ref, vo_ref):
        g = p_ref[0].astype(F32)
        for d in range(1, N_DEV):
            g = g + p_ref[d].astype(F32)
        g_ref[...] = g
        d_ref[...], mo_ref[...], vo_ref[...] = _adam_update(g, w_ref[...], m_ref[...], v_ref[...])

    blk = pl.BlockSpec((tr, cols), lambda i: (i, 0))
    return pl.pallas_call(
        body,
        grid=(rows // tr,),
        in_specs=[pl.BlockSpec((N_DEV, tr, cols), lambda i: (0, i, 0)), blk, blk, blk],
        out_specs=[blk] * 4,
        out_shape=[SDS((rows, cols), F32)] * 4,
        compiler_params=_cparams("arbitrary"),
        name=name,
    )(parts, w, m, v)


_ROW_OF = dict(norm1_g=(0, D_MODEL), norm2_g=(1, D_MODEL), final_g=(2, D_MODEL), pool_scale=(3, POOL_W),
               b_forget=(4, N_HEADS), loss=(5, 1))


def _pack_rows(vals):
    rows = [jnp.pad(vals[n].reshape(1, width).astype(F32), ((0, 0), (0, D_MODEL - width)))
            for n, (_, width) in sorted(_ROW_OF.items(), key=lambda kv: kv[1][0])]
    return jnp.concatenate(rows + [jnp.zeros((8 - len(rows), D_MODEL), F32)], axis=0)


def _adam_update(g, w, m, v):
    m_new = ADAM_B1 * m + (1.0 - ADAM_B1) * g
    v_new = ADAM_B2 * v + (1.0 - ADAM_B2) * (g * g)
    m_hat = m_new / (1.0 - ADAM_B1 ** ADAM_STEP)
    v_hat = v_new / (1.0 - ADAM_B2 ** ADAM_STEP)
    return -ADAM_LR * (m_hat / (jnp.sqrt(v_hat) + ADAM_EPS) + ADAM_WD * w), m_new, v_new


def _adamw_replicated(parts_rows, parts_pool, w, m, v):
    names = ("norm1_g", "norm2_g", "final_g", "pool_scale", "b_forget", "w_pool")
    shapes = {n: ((len(POOL_WINDOWS), POOL_G, POOL_G) if n == "w_pool" else (1, _ROW_OF[n][1])) for n in names}

    def body(rows_ref, pool_ref, *refs):
        ins, outs = refs[:3 * len(names)], refs[3 * len(names):]

        def total(n):
            if n == "w_pool":
                pieces = [pool_ref[d] for d in range(N_DEV)]
            else:
                row, width = _ROW_OF[n]
                pieces = [rows_ref[d, row:row + 1, 0:width] for d in range(N_DEV)]
            g = pieces[0]
            for p in pieces[1:]:
                g = g + p
            return g

        outs[0][...] = total("loss")
        for k, n in enumerate(names):
            g = total(n)
            delta, m_new, v_new = _adam_update(g, ins[3 * k][...], ins[3 * k + 1][...], ins[3 * k + 2][...])
            for o_ref, val in zip(outs[1 + 4 * k:5 + 4 * k], (g, delta, m_new, v_new)):
                o_ref[...] = val

    args = [d[n].reshape(shapes[n]) for n in names for d in (w, m, v)]
    res = pl.pallas_call(
        body,
        out_shape=[SDS((1, 1), F32)] + [SDS(shapes[n], F32) for n in names for _ in range(4)],
        compiler_params=_cparams(),
        name="adamw_replicated",
    )(parts_rows, parts_pool, *args)
    return res[0], {n: [r.reshape(w[n].shape) for r in res[1 + 4 * k:5 + 4 * k]] for k, n in enumerate(names)}


def kernel(x, norm1_g, w_in, b_forget, w_pool, pool_scale, w_out, norm2_g, w_gate, w_up, w_down, final_g, loss_target, m_norm1_g, m_w_in, m_b_forget, m_w_pool, m_pool_scale, m_w_out, m_norm2_g, m_w_gate, m_w_up, m_w_down, m_final_g, v_norm1_g, v_w_in, v_b_forget, v_w_pool, v_pool_scale, v_w_out, v_norm2_g, v_w_gate, v_w_up, v_w_down, v_final_g):
    big = ("w_in", "w_out", "w_gate", "w_up", "w_down")
    order = ("norm1_g", "w_in", "b_forget", "w_pool", "pool_scale", "w_out", "norm2_g", "w_gate", "w_up", "w_down",
             "final_g")
    w = dict(norm1_g=norm1_g, w_in=w_in, b_forget=b_forget, w_pool=w_pool, pool_scale=pool_scale, w_out=w_out,
             norm2_g=norm2_g, w_gate=w_gate, w_up=w_up, w_down=w_down, final_g=final_g)
    m = dict(norm1_g=m_norm1_g, w_in=m_w_in, b_forget=m_b_forget, w_pool=m_w_pool, pool_scale=m_pool_scale,
             w_out=m_w_out, norm2_g=m_norm2_g, w_gate=m_w_gate, w_up=m_w_up, w_down=m_w_down, final_g=m_final_g)
    v = dict(norm1_g=v_norm1_g, w_in=v_w_in, b_forget=v_b_forget, w_pool=v_w_pool, pool_scale=v_pool_scale,
             w_out=v_w_out, norm2_g=v_norm2_g, w_gate=v_w_gate, w_up=v_w_up, w_down=v_w_down, final_g=v_final_g)

    flipped = ("w_in", "w_gate", "w_up")
    shard = lambda d, n: d[n][0].T if n in flipped else d[n][0]
    gather, started = _exchange_start([shard(w, n).astype(BF16) for n in big], [False] * len(big), "gather_start")
    gather = dict(zip(big, gather))

    def gathered(names, after):
        return _exchange_wait([gather[n] for n in names], after, "gather_wait_" + names[0])

    def weight(name, after):
        if name == "w_in":
            full = gathered(["w_in"], after)[0].reshape(IN_W, D_MODEL)
            f0 = QKV_W + N_HEADS
            return jnp.concatenate([full[:QKV_W], full[f0:], full[QKV_W:f0],
                                    jnp.zeros((IN_PAD - IN_W, D_MODEL), BF16)], axis=0)
        if name == "w_out":
            return gathered(["w_out"], after)[0].reshape(D_MODEL, D_MODEL)
        if name == "w_gate_up":
            return [g.reshape(D_FF, D_MODEL) for g in gathered(["w_gate", "w_up"], after)]
        return gathered(["w_down"], after)[0].reshape(D_FF, D_MODEL)

    rows = lambda g: g.reshape(N_DEV, g.shape[0] // N_DEV, g.shape[1])
    sent = {}

    def emit(name, grad):
        if name == "w_gate_up":
            names, slots = ["w_gate", "w_up"], [rows(g) for g in grad]
        else:
            names, slots = [name], [rows(grad).astype(BF16) if name == "w_in" else rows(grad)]
        handles, token = _exchange_start(slots, [True] * len(slots), "grads_start_" + name)
        sent.update(zip(names, handles))
        return token

    loss_row, dx, small_grads = _local_step(x[0], loss_target[0], w, weight, emit, started)

    packed = _pack_rows(dict(small_grads, loss=0.5 / D_MODEL * jnp.sum(loss_row)))
    small_handles, after = _exchange_start([packed, small_grads["w_pool"]], [False, False], "grads_start_replicated")

    outs = {}
    for name in ("w_down", "w_gate", "w_up", "w_out", "w_in"):
        (parts,) = _exchange_wait([sent[name]], after, "grads_wait_" + name)
        outs[name] = _adamw(parts, shard(w, name), shard(m, name), shard(v, name), "adamw_" + name)
        after = outs[name][0]
        outs[name] = [(a.T if name in flipped else a)[None] for a in outs[name]]
    parts_rows, parts_pool = _exchange_wait(small_handles, after, "grads_wait_replicated")
    loss, small = _adamw_replicated(parts_rows, parts_pool, w, m, v)
    outs.update(small)

    return (loss.reshape(()), dx[None]) + tuple(outs[n][k] for k in range(4) for n in order)
```

```python
import functools

import jax
import jax.numpy as jnp
from jax import lax
from jax.experimental import pallas as pl
from jax.experimental.pallas import tpu as pltpu

F32 = jnp.float32
BF16 = jnp.bfloat16
SDS = jax.ShapeDtypeStruct

D_MODEL = 1024
ATTN_W = 512
N_HEADS = 8
HEAD_DIM = 64
N_PAIRS = N_HEADS // 2
POOL_W = 512
POOL_WINDOWS = (2, 4, 8, 16)
POOL_G = 128
HALO = 16
IN_W = 3 * ATTN_W + N_HEADS + POOL_W
QKV_W = 3 * ATTN_W
U_OFF = QKV_W
F_OFF = QKV_W + POOL_W
IN_PAD = F_OFF + 128
D_FF = 2816
EPS = 1e-6
NEG = -1e30
N_DEV = 8
LANES = 128

ADAM_LR = 0.001
ADAM_B1 = 0.9
ADAM_B2 = 0.999
ADAM_EPS = 1e-08
ADAM_WD = 0.01
ADAM_STEP = 10

VMEM_LIMIT_BYTES = 56 * 1024 * 1024
MESH = pl.DeviceIdType.MESH
NT = (((1,), (1,)), ((), ()))
TN = (((0,), (0,)), ((), ()))


def _cparams(*sem):
    return pltpu.CompilerParams(dimension_semantics=sem or None, vmem_limit_bytes=VMEM_LIMIT_BYTES)


def _split3(a):
    hi = a.astype(BF16)
    r1 = a - hi.astype(F32)
    mid = r1.astype(BF16)
    lo = (r1 - mid.astype(F32)).astype(BF16)
    return hi, mid, lo


def _dot_sel(a, sel, dims=None):
    sb = sel.astype(BF16)
    if dims is None:
        return sum(jnp.dot(p, sb, preferred_element_type=F32) for p in _split3(a))
    return sum(lax.dot_general(p, sb, dims, preferred_element_type=F32) for p in _split3(a))


def _sel_dot(sel, a, dims=None):
    sb = sel.astype(BF16)
    if dims is None:
        return sum(jnp.dot(sb, p, preferred_element_type=F32) for p in _split3(a))
    return sum(lax.dot_general(sb, p, dims, preferred_element_type=F32) for p in _split3(a))


def _iota2(shape, dim):
    return lax.broadcasted_iota(jnp.int32, shape, dim)


def _norm1(x, g1, *, tm):
    s = x.shape[0]

    def body(x_ref, g_ref, h_ref, r_ref):
        xv = x_ref[...]
        r = lax.rsqrt(jnp.mean(xv * xv, axis=-1, keepdims=True) + EPS)
        h_ref[...] = (xv * r * g_ref[...]).astype(BF16)
        r_ref[...] = r

    row = lambda w: pl.BlockSpec((tm, w), lambda i: (i, 0))
    return pl.pallas_call(
        body,
        grid=(s // tm,),
        in_specs=[row(D_MODEL), pl.BlockSpec((1, D_MODEL), lambda i: (0, 0))],
        out_specs=[row(D_MODEL), row(1)],
        out_shape=[SDS((s, D_MODEL), BF16), SDS((s, 1), F32)],
        compiler_params=_cparams("arbitrary"),
        name="norm1",
    )(x, g1)


def _in_proj(h, w_in_t, *, tm):
    s = h.shape[0]

    def body(h_ref, w_ref, qkv_ref, u_ref, fl_ref):
        h = h_ref[...]
        qkv_ref[...] = lax.dot_general(h, w_ref[0:QKV_W, :], NT, preferred_element_type=F32).astype(BF16)
        u_ref[...] = lax.dot_general(h, w_ref[U_OFF:F_OFF, :], NT, preferred_element_type=F32)
        fl_ref[...] = lax.dot_general(h, w_ref[F_OFF:IN_PAD, :], NT, preferred_element_type=F32)

    row = lambda w: pl.BlockSpec((tm, w), lambda i: (i, 0))
    return pl.pallas_call(
        body,
        grid=(s // tm,),
        in_specs=[row(D_MODEL), pl.BlockSpec((IN_PAD, D_MODEL), lambda i: (0, 0))],
        out_specs=[row(QKV_W), row(POOL_W), row(LANES)],
        out_shape=[SDS((s, QKV_W), BF16), SDS((s, POOL_W), F32), SDS((s, LANES), F32)],
        compiler_params=_cparams("arbitrary"),
        name="in_proj",
    )(h, w_in_t)


def _head_block_masks(rows, nb):
    shift = nb.bit_length() - 1
    rr, cc = _iota2((rows, rows), 0), _iota2((rows, rows), 1)
    same = lax.shift_right_logical(rr, shift) == lax.shift_right_logical(cc, shift)
    return rr, cc, same


def _forget_cumsum(fl_t, b_rows):
    rows = fl_t.shape[0]
    nb = rows // N_HEADS

    def body(fl_ref, b_ref, c_ref):
        z = fl_ref[...] + b_ref[...]
        lf = jnp.minimum(z, 0.0) - jnp.log1p(jnp.exp(-jnp.abs(z)))
        upper = _iota2((LANES, LANES), 0) <= _iota2((LANES, LANES), 1)
        within = _dot_sel(lf, upper)
        tot = _dot_sel(lf, jnp.ones((LANES, LANES), F32))
        rr, cc, same = _head_block_masks(rows, nb)
        c_ref[...] = within + _sel_dot(same & (cc < rr), tot)

    return pl.pallas_call(body, out_shape=SDS(fl_t.shape, F32), compiler_params=_cparams(), name="forget_cumsum")(
        fl_t, b_rows)


BIAS_LANES = 3


def _augment(t, h, col, col_first):
    n = t.shape[0]
    lane = _iota2((n, LANES), 1)
    own = (lane < HEAD_DIM) if h == 0 else (lane >= HEAD_DIM)
    b0 = HEAD_DIM if h == 0 else 0
    c0, o0 = (b0, b0 + BIAS_LANES) if col_first else (b0 + BIAS_LANES, b0)
    x = jnp.where(own, t, 0.0)
    for off, piece in enumerate(_split3(col)):
        x = jnp.where(lane == c0 + off, piece.astype(F32), x)
    x = jnp.where((lane >= o0) & (lane < o0 + BIAS_LANES), 1.0, x)
    return x.astype(BF16)


def _attn_fwd(qkv, c_col, *, tk):
    s = qkv.shape[0]
    tq = 2 * tk
    nb = s // tk

    def body(q_ref, k_ref, v_ref, cq_ref, ck_ref, o_ref, lse_ref, kp_ref, vt_ref, st_ref):
        i = pl.program_id(1)

        @pl.when(i == 0)
        def _():
            def prep(jb, _):
                st = pl.multiple_of(jb * tk, tk)
                k2 = k_ref[pl.ds(st, tk), :].astype(F32)
                ck = ck_ref[pl.ds(st, tk), :]
                for h in range(2):
                    kp_ref[h * nb + jb] = _augment(k2, h, -ck[:, h:h + 1], True)
                vt_ref[jb] = v_ref[pl.ds(st, tk), :].astype(F32).T.astype(BF16)
                return 0

            lax.fori_loop(0, nb, prep, 0)

        qs = q_ref[...].astype(F32) * 0.125
        cq = cq_ref[...]
        qp = [_augment(qs, h, cq[:, h:h + 1], False) for h in range(2)]

        def logits(j):
            return tuple(lax.dot_general(kp_ref[h * nb + j], qp[h], NT, preferred_element_type=F32) for h in range(2))

        def softmax_pv(j, slot, stats, masked):
            out = []
            for h in range(2):
                m, l, acc = stats[h]
                st = st_ref[2 * slot + h]
                if masked:
                    st = jnp.where(j * tk + _iota2((tk, tq), 0) <= i * tq + _iota2((tk, tq), 1), st, NEG)
                m_new = jnp.maximum(m, jnp.max(st, axis=0, keepdims=True))
                alpha = jnp.exp(m - m_new)
                p = jnp.exp(st - m_new)
                l = alpha * l + jnp.sum(p, axis=0, keepdims=True)
                vt = vt_ref[j, h * HEAD_DIM:(h + 1) * HEAD_DIM, :]
                acc = alpha * acc + jnp.dot(vt, p.astype(BF16), preferred_element_type=F32)
                out.append((m_new, l, acc))
            return tuple(out)

        def put(slot, j):
            for h, st in enumerate(logits(j)):
                st_ref[2 * slot + h] = st

        def pair(t, stats):
            put(1, 2 * t + 1)
            stats = softmax_pv(2 * t, 0, stats, False)
            put(0, 2 * t + 2)
            return softmax_pv(2 * t + 1, 1, stats, False)

        init = tuple((jnp.full((1, tq), NEG, F32), jnp.zeros((1, tq), F32), jnp.zeros((HEAD_DIM, tq), F32))
                     for _ in range(2))
        put(0, 0)
        stats = lax.fori_loop(0, i, pair, init)
        put(1, 2 * i + 1)
        stats = softmax_pv(2 * i, 0, stats, True)
        (ma, la, acca), (mb, lb, accb) = softmax_pv(2 * i + 1, 1, stats, True)
        o_ref[...] = jnp.concatenate([acca / la, accb / lb], axis=0).T.astype(BF16)
        lse_ref[...] = jnp.where(_iota2((2, tq), 0) == 0, ma + jnp.log(la), mb + jnp.log(lb))

    return pl.pallas_call(
        body,
        grid=(N_PAIRS, s // tq),
        in_specs=[
            pl.BlockSpec((tq, LANES), lambda p, i: (i, p)),
            pl.BlockSpec((s, LANES), lambda p, i: (0, N_PAIRS + p)),
            pl.BlockSpec((s, LANES), lambda p, i: (0, 2 * N_PAIRS + p)),
            pl.BlockSpec((None, tq, 2), lambda p, i: (p, i, 0)),
            pl.BlockSpec((None, s, 2), lambda p, i: (p, 0, 0)),
        ],
        out_specs=[
            pl.BlockSpec((tq, LANES), lambda p, i: (i, p)),
            pl.BlockSpec((None, None, 2, tq), lambda p, i: (p, i, 0, 0)),
        ],
        out_shape=[SDS((s, ATTN_W), BF16), SDS((N_PAIRS, s // tq, 2, tq), F32)],
        scratch_shapes=[pltpu.VMEM((2 * nb, tk, LANES), BF16), pltpu.VMEM((nb, LANES, tk), BF16),
                        pltpu.VMEM((4, tk, tq), F32)],
        compiler_params=_cparams("arbitrary", "arbitrary"),
        name="attn_fwd",
    )(qkv, qkv, qkv, c_col, c_col)


def _pool_counts(row0, tm, w):
    t = row0 + _iota2((tm, 1), 0)
    return jnp.minimum(t + 1, w).astype(F32)


def _pool_fwd(u, w_pool, pool_scale, *, tm):
    s = u.shape[0]

    def body(u_ref, w_ref, sc_ref, pooled_ref, po_ref, tail_ref):
        i = pl.program_id(0)

        @pl.when(i == 0)
        def _():
            tail_ref[...] = jnp.zeros_like(tail_ref)

        uv = u_ref[...]
        ext = jnp.concatenate([tail_ref[...], uv], axis=0)
        tail_ref[...] = uv[tm - HALO:, :]
        for g, w in enumerate(POOL_WINDOWS):
            cols = slice(g * POOL_G, (g + 1) * POOL_G)
            acc = ext[:, cols]
            k = 1
            while k < w:
                acc = acc + pltpu.roll(acc, k, axis=0)
                k *= 2
            pooled = (acc[HALO:, :] / _pool_counts(i * tm, tm, w) - uv[:, cols]).astype(BF16)
            pooled_ref[:, cols] = pooled
            mixed = jnp.dot(pooled, w_ref[g].astype(BF16), preferred_element_type=F32)
            po_ref[:, cols] = (mixed * sc_ref[:, cols]).astype(BF16)

    row = pl.BlockSpec((tm, POOL_W), lambda i: (i, 0))
    return pl.pallas_call(
        body,
        grid=(s // tm,),
        in_specs=[row, pl.BlockSpec((len(POOL_WINDOWS), POOL_G, POOL_G), lambda i: (0, 0, 0)),
                  pl.BlockSpec((1, POOL_W), lambda i: (0, 0))],
        out_specs=[row, row],
        out_shape=[SDS((s, POOL_W), BF16), SDS((s, POOL_W), BF16)],
        scratch_shapes=[pltpu.VMEM((HALO, POOL_W), F32)],
        compiler_params=_cparams("arbitrary"),
        name="pool_fwd",
    )(u, w_pool, pool_scale)


def _out_norm2(attn_o, pool_o, w_out, x, g2, *, tm):
    s = x.shape[0]

    def body(a_ref, p_ref, w_ref, x_ref, g_ref, x1_ref, h2_ref, r_ref):
        x1 = (x_ref[...] + jnp.dot(a_ref[...], w_ref[0:ATTN_W, :], preferred_element_type=F32)
              + jnp.dot(p_ref[...], w_ref[ATTN_W:, :], preferred_element_type=F32))
        r = lax.rsqrt(jnp.mean(x1 * x1, axis=-1, keepdims=True) + EPS)
        x1_ref[...] = x1
        r_ref[...] = r
        h2_ref[...] = (x1 * r * g_ref[...]).astype(BF16)

    row = lambda w: pl.BlockSpec((tm, w), lambda i: (i, 0))
    full = lambda a, b: pl.BlockSpec((a, b), lambda i: (0, 0))
    return pl.pallas_call(
        body,
        grid=(s // tm,),
        in_specs=[row(ATTN_W), row(POOL_W), full(D_MODEL, D_MODEL), row(D_MODEL), full(1, D_MODEL)],
        out_specs=[row(D_MODEL), row(D_MODEL), row(1)],
        out_shape=[SDS((s, D_MODEL), F32), SDS((s, D_MODEL), BF16), SDS((s, 1), F32)],
        compiler_params=_cparams("arbitrary"),
        name="out_norm2",
    )(attn_o, pool_o, w_out, x, g2)


def _gate_up(h2, wg_t, wu_t, *, tm, tn):
    s = h2.shape[0]

    def body(h_ref, wg_ref, wu_ref, gate_ref, up_ref, act_ref):
        h = h_ref[...]
        gate = lax.dot_general(h, wg_ref[...], NT, preferred_element_type=F32)
        up = lax.dot_general(h, wu_ref[...], NT, preferred_element_type=F32)
        gate_ref[...] = gate.astype(BF16)
        up_ref[...] = up.astype(BF16)
        act_ref[...] = (gate * jax.nn.sigmoid(gate) * up).astype(BF16)

    wspec = pl.BlockSpec((tn, D_MODEL), lambda c, r: (c, 0))
    ospec = pl.BlockSpec((tm, tn), lambda c, r: (r, c))
    return pl.pallas_call(
        body,
        grid=(D_FF // tn, s // tm),
        in_specs=[pl.BlockSpec((tm, D_MODEL), lambda c, r: (r, 0)), wspec, wspec],
        out_specs=[ospec, ospec, ospec],
        out_shape=[SDS((s, D_FF), BF16), SDS((s, D_FF), BF16), SDS((s, D_FF), BF16)],
        compiler_params=_cparams("arbitrary", "arbitrary"),
        name="gate_up",
    )(h2, wg_t, wu_t)


def _down_final(act, wd, x1, gf, tgt, *, tm):
    s = x1.shape[0]

    def body(a_ref, w_ref, x1_ref, g_ref, t_ref, dx2_ref, loss_ref, dgf_ref):
        @pl.when(pl.program_id(0) == 0)
        def _():
            loss_ref[...] = jnp.zeros_like(loss_ref)
            dgf_ref[...] = jnp.zeros_like(dgf_ref)

        x2 = x1_ref[...] + jnp.dot(a_ref[...], w_ref[...], preferred_element_type=F32)
        r = lax.rsqrt(jnp.mean(x2 * x2, axis=-1, keepdims=True) + EPS)
        xn = x2 * r
        g = g_ref[...]
        diff = xn * g - t_ref[...]
        loss_ref[...] += jnp.sum(diff * diff, axis=0, keepdims=True)
        dy = diff * (1.0 / D_MODEL)
        dgf_ref[...] += jnp.sum(dy * xn, axis=0, keepdims=True)
        dxn = dy * g
        dx2_ref[...] = r * (dxn - xn * jnp.mean(dxn * xn, axis=-1, keepdims=True))

    row = lambda w: pl.BlockSpec((tm, w), lambda i: (i, 0))
    full = lambda a, b: pl.BlockSpec((a, b), lambda i: (0, 0))
    return pl.pallas_call(
        body,
        grid=(s // tm,),
        in_specs=[row(D_FF), full(D_FF, D_MODEL), row(D_MODEL), full(1, D_MODEL), row(D_MODEL)],
        out_specs=[row(D_MODEL), full(1, D_MODEL), full(1, D_MODEL)],
        out_shape=[SDS((s, D_MODEL), F32), SDS((1, D_MODEL), F32), SDS((1, D_MODEL), F32)],
        compiler_params=_cparams("arbitrary"),
        name="down_final",
    )(act, wd, x1, gf, tgt)


def _swiglu_bwd(dx2, wd, gate, up, *, tm, tn):
    s = dx2.shape[0]

    def body(d_ref, w_ref, gate_ref, up_ref, dgate_ref, dup_ref):
        dact = lax.dot_general(d_ref[...].astype(BF16), w_ref[...], NT, preferred_element_type=F32)
        gate = gate_ref[...].astype(F32)
        sg = jax.nn.sigmoid(gate)
        dup_ref[...] = (dact * (gate * sg)).astype(BF16)
        dgate_ref[...] = (dact * up_ref[...].astype(F32) * (sg * (1.0 + gate * (1.0 - sg)))).astype(BF16)

    ospec = pl.BlockSpec((tm, tn), lambda c, r: (r, c))
    return pl.pallas_call(
        body,
        grid=(D_FF // tn, s // tm),
        in_specs=[pl.BlockSpec((tm, D_MODEL), lambda c, r: (r, 0)), pl.BlockSpec((tn, D_MODEL), lambda c, r: (c, 0)),
                  ospec, ospec],
        out_specs=[ospec, ospec],
        out_shape=[SDS((s, D_FF), BF16), SDS((s, D_FF), BF16)],
        compiler_params=_cparams("arbitrary", "arbitrary"),
        name="swiglu_bwd",
    )(dx2, wd, gate, up)


def _mm_tn(a, bs, *, ta, ts, name):
    s, ka = a.shape
    n = len(bs)

    def body(a_ref, *refs):
        b_refs, o_refs = refs[:n], refs[n:]

        @pl.when(pl.program_id(1) == 0)
        def _():
            for o_ref in o_refs:
                o_ref[...] = jnp.zeros_like(o_ref)

        av = a_ref[...].astype(BF16)
        for b_ref, o_ref in zip(b_refs, o_refs):
            o_ref[...] += lax.dot_general(av, b_ref[...].astype(BF16), TN, preferred_element_type=F32)

    return pl.pallas_call(
        body,
        grid=(ka // ta, s // ts),
        in_specs=[pl.BlockSpec((ts, ta), lambda i, k: (k, i))]
        + [pl.BlockSpec((ts, b.shape[1]), lambda i, k: (k, 0)) for b in bs],
        out_specs=[pl.BlockSpec((ta, b.shape[1]), lambda i, k: (i, 0)) for b in bs],
        out_shape=[SDS((ka, b.shape[1]), F32) for b in bs],
        compiler_params=_cparams("arbitrary", "arbitrary"),
        name=name,
    )(a, *bs)


def _mm_tn_stacked(as_, rows, b, *, ts, name):
    s, nb_ = b.shape
    n = len(as_)
    offsets = [sum(rows[:i]) for i in range(n)]

    def body(*refs):
        a_refs, b_ref, o_ref = refs[:n], refs[n], refs[n + 1]

        @pl.when(pl.program_id(0) == 0)
        def _():
            o_ref[...] = jnp.zeros_like(o_ref)

        bv = b_ref[...].astype(BF16)
        for a_ref, off, cnt in zip(a_refs, offsets, rows):
            part = lax.dot_general(a_ref[...].astype(BF16), bv, TN, preferred_element_type=F32)
            o_ref[off:off + cnt, :] += part[0:cnt, :]

    return pl.pallas_call(
        body,
        grid=(s // ts,),
        in_specs=[pl.BlockSpec((ts, a.shape[1]), lambda k: (k, 0)) for a in as_] + [pl.BlockSpec((ts, nb_), lambda k: (k, 0))],
        out_specs=pl.BlockSpec((sum(rows), nb_), lambda k: (0, 0)),
        out_shape=SDS((sum(rows), nb_), F32),
        compiler_params=_cparams("arbitrary"),
        name=name,
    )(*as_, b)


def _norm_bwd(dh, x, r, g, dres):
    xn = x * r
    dxn = dh * g
    dx = dres + r * (dxn - xn * jnp.mean(dxn * xn, axis=-1, keepdims=True))
    return dx, jnp.sum(dh * xn, axis=0, keepdims=True)


def _mlp_in_bwd(dgate, dup, wg_t, wu_t, w_out, x1, r2, g2, dx2, *, tm):
    s = x1.shape[0]

    def body(dg_ref, du_ref, wg_ref, wu_ref, wo_ref, x_ref, r_ref, g_ref, d_ref, dx1_ref, dmix_ref, dg2_ref):
        @pl.when(pl.program_id(0) == 0)
        def _():
            dg2_ref[...] = jnp.zeros_like(dg2_ref)

        dh2 = (jnp.dot(dg_ref[...], wg_ref[...], preferred_element_type=F32)
               + jnp.dot(du_ref[...], wu_ref[...], preferred_element_type=F32))
        dx1, dg2 = _norm_bwd(dh2, x_ref[...], r_ref[...], g_ref[...], d_ref[...])
        dg2_ref[...] += dg2
        dx1_ref[...] = dx1
        dmix_ref[...] = lax.dot_general(dx1.astype(BF16), wo_ref[...], NT, preferred_element_type=F32)

    row = lambda w: pl.BlockSpec((tm, w), lambda i: (i, 0))
    full = lambda a, b: pl.BlockSpec((a, b), lambda i: (0, 0))
    return pl.pallas_call(
        body,
        grid=(s // tm,),
        in_specs=[row(D_FF), row(D_FF), full(D_FF, D_MODEL), full(D_FF, D_MODEL), full(D_MODEL, D_MODEL),
                  row(D_MODEL), row(1), full(1, D_MODEL), row(D_MODEL)],
        out_specs=[row(D_MODEL), row(D_MODEL), full(1, D_MODEL)],
        out_shape=[SDS((s, D_MODEL), F32), SDS((s, D_MODEL), F32), SDS((1, D_MODEL), F32)],
        compiler_params=_cparams("arbitrary"),
        name="mlp_in_bwd",
    )(dgate, dup, wg_t, wu_t, w_out, x1, r2, g2, dx2)


def _pool_bwd(dmixed, pooled, w_pool, pool_scale, *, tm):
    s = pooled.shape[0]
    nt = s // tm
    ng = len(POOL_WINDOWS)

    def body(d_ref, p_ref, w_ref, sc_ref, du_ref, dw_ref, dsc_ref, head_ref):
        i = pl.program_id(0)

        @pl.when(i == 0)
        def _():
            head_ref[...] = jnp.zeros_like(head_ref)
            dw_ref[...] = jnp.zeros_like(dw_ref)
            dsc_ref[...] = jnp.zeros_like(dsc_ref)

        row0 = (nt - 1 - i) * tm
        for g, w in enumerate(POOL_WINDOWS):
            cols = slice(g * POOL_G, (g + 1) * POOL_G)
            wb = w_ref[g].astype(BF16)
            pooled_g = p_ref[:, cols]
            dpo = d_ref[:, cols]
            mixed = jnp.dot(pooled_g, wb, preferred_element_type=F32)
            dsc_ref[:, cols] += jnp.sum(dpo * mixed, axis=0, keepdims=True)
            dmp = (dpo * sc_ref[:, cols]).astype(BF16)
            dw_ref[g] += lax.dot_general(pooled_g, dmp, TN, preferred_element_type=F32)
            dpooled = lax.dot_general(dmp, wb, NT, preferred_element_type=F32)
            a = dpooled / _pool_counts(row0, tm, w)
            acc = jnp.concatenate([a, head_ref[:, cols]], axis=0)
            head_ref[:, cols] = a[0:HALO, :]
            k = 1
            while k < w:
                acc = acc + pltpu.roll(acc, tm + HALO - k, axis=0)
                k *= 2
            du_ref[:, cols] = (acc[0:tm, :] - dpooled).astype(BF16)

    rev = lambda i: (nt - 1 - i, 0)
    return pl.pallas_call(
        body,
        grid=(nt,),
        in_specs=[pl.BlockSpec((tm, POOL_W), lambda i: (nt - 1 - i, 1)), pl.BlockSpec((tm, POOL_W), rev),
                  pl.BlockSpec((ng, POOL_G, POOL_G), lambda i: (0, 0, 0)), pl.BlockSpec((1, POOL_W), lambda i: (0, 0))],
        out_specs=[pl.BlockSpec((tm, POOL_W), rev), pl.BlockSpec((ng, POOL_G, POOL_G), lambda i: (0, 0, 0)),
                   pl.BlockSpec((1, POOL_W), lambda i: (0, 0))],
        out_shape=[SDS((s, POOL_W), BF16), SDS((ng, POOL_G, POOL_G), F32), SDS((1, POOL_W), F32)],
        scratch_shapes=[pltpu.VMEM((HALO, POOL_W), F32)],
        compiler_params=_cparams("arbitrary"),
        name="pool_bwd",
    )(dmixed, pooled, w_pool, pool_scale)


SUM_ROWS = 16


def _heads_t(t):
    n = t.shape[0]
    lane = _iota2((n, LANES), 1)
    tf = t.astype(F32)
    halves = jnp.concatenate([jnp.where(lane < HEAD_DIM, tf, 0.0).T, jnp.where(lane < HEAD_DIM, 0.0, tf).T], axis=1)
    r, c = _iota2((SUM_ROWS, 2 * n), 0), _iota2((SUM_ROWS, 2 * n), 1)
    ones = jnp.where(((r == 0) & (c < n)) | ((r == 4) & (c >= n)), 1.0, 0.0)
    return jnp.concatenate([halves, ones], axis=0).astype(BF16)


def _attn_bwd(qkv, attn_o, dmixed, rowb, ck_col, *, tq):
    s = qkv.shape[0]
    tk = tq
    nb = s // tq
    rows_t = LANES + SUM_ROWS

    def body(q_ref, k_ref, v_ref, o_ref, do_ref, rowb_ref, ck_ref, dq_ref, dk_ref, dv_ref, dck_ref, dcq_ref,
             dqt_ref, delta_ref, kp_ref, qp_ref, dob_ref, qt_ref, kt_ref, dot_ref, front_ref):
        lane = _iota2((tq, LANES), 1)
        lo = lane < HEAD_DIM
        first = _iota2((8, LANES), 1) < HEAD_DIM
        sel = jnp.where(_iota2((8, LANES), 0) < 4, jnp.where(first, 1.0, 0.0), jnp.where(first, 0.0, 1.0))

        def prep(b, _):
            st = pl.multiple_of(b * tq, tq)
            do2 = do_ref[pl.ds(st, tq), :]
            delta_ref[b] = _sel_dot(sel, do2 * o_ref[pl.ds(st, tq), :].astype(F32), NT)
            dob_ref[pl.ds(st, tq), :] = do2.astype(BF16)
            dqt_ref[b] = jnp.zeros((rows_t, tq), F32)
            k2 = k_ref[pl.ds(st, tq), :].astype(F32)
            q2 = q_ref[pl.ds(st, tq), :].astype(F32)
            ck = ck_ref[pl.ds(st, tq), :]
            for h in range(2):
                kp_ref[h * nb + b] = _augment(k2, h, -ck[:, h:h + 1], True)
                qp_ref[h * nb + b] = _augment(q2 * 0.125, h, jnp.zeros((tq, 1), F32), False)
            qt_ref[b] = _heads_t(q2)
            kt_ref[b] = _heads_t(k2)
            dot_ref[b] = _heads_t(do2)[0:LANES, :]
            return 0

        lax.fori_loop(0, nb, prep, 0)

        def split(t):
            z = jnp.zeros_like(t)
            return jnp.where(lo, t, z), jnp.where(lo, z, t)

        def kv_block(j, _):
            st_j = pl.multiple_of(j * tk, tk)
            vs = split(v_ref[pl.ds(st_j, tk), :])
            kt = kt_ref[j]

            def stage(i, slot):
                ic = jnp.minimum(i, nb - 1)
                do2 = dob_ref[pl.ds(pl.multiple_of(ic * tq, tq), tq), :]
                for h in range(2):
                    front_ref[4 * slot + h] = lax.dot_general(kp_ref[h * nb + j], qp_ref[h * nb + ic], NT,
                                                              preferred_element_type=F32)
                    front_ref[4 * slot + 2 + h] = lax.dot_general(vs[h], do2, NT, preferred_element_type=F32)

            def q_block(i, slot, carry, diagonal):
                dkt, dvt = carry
                ic = jnp.minimum(i, nb - 1)
                rb = rowb_ref[ic] + jnp.where(i < nb, 0.0, NEG)
                dl = delta_ref[ic]
                pts, dsts = [], []
                for h in range(2):
                    st = front_ref[4 * slot + h] + rb[h:h + 1, :]
                    if diagonal:
                        st = jnp.where(_iota2((tk, tq), 0) <= _iota2((tk, tq), 1), st, NEG)
                    pt = jnp.exp(st)
                    pts.append(pt.astype(BF16))
                    dsts.append((pt * (front_ref[4 * slot + 2 + h] - dl[4 * h:4 * h + 1, :])).astype(BF16))
                dvt = dvt + lax.dot_general(dot_ref[ic], jnp.concatenate(pts, axis=1), NT, preferred_element_type=F32)
                dkt = dkt + lax.dot_general(qt_ref[ic], jnp.concatenate(dsts, axis=1), NT, preferred_element_type=F32)
                dqt_ref[ic] += jnp.dot(kt, jnp.concatenate(dsts, axis=0), preferred_element_type=F32)
                return dkt, dvt

            def pair(t, carry):
                i0 = j + 1 + 2 * t
                stage(i0 + 1, 0)
                carry = q_block(i0, 1, carry, False)
                stage(i0 + 2, 1)
                return q_block(i0 + 1, 0, carry, False)

            stage(j, 0)
            stage(j + 1, 1)
            carry = q_block(j, 0, (jnp.zeros((rows_t, tk), F32), jnp.zeros((LANES, tk), F32)), True)
            dkt, dvt = lax.fori_loop(0, lax.shift_right_logical(nb - j, 1), pair, carry)
            dk_ref[pl.ds(st_j, tk), :] = (dkt[0:LANES, :].T * 0.125).astype(BF16)
            dv_ref[pl.ds(st_j, tk), :] = dvt.T.astype(BF16)
            dck_ref[j] = dkt[LANES:LANES + 8, :]
            return 0

        lax.fori_loop(0, nb, kv_block, 0)

        def finish(b, _):
            acc = dqt_ref[b]
            dq_ref[pl.ds(pl.multiple_of(b * tq, tq), tq), :] = (acc[0:LANES, :].T * 0.125).astype(BF16)
            dcq_ref[b] = acc[LANES:LANES + 8, :]
            return 0

        lax.fori_loop(0, nb, finish, 0)

    col = lambda off: pl.BlockSpec((s, LANES), lambda p: (0, off + p))
    sums = pl.BlockSpec((None, nb, 8, tq), lambda p: (p, 0, 0, 0))
    return pl.pallas_call(
        body,
        grid=(N_PAIRS,),
        in_specs=[col(0), col(N_PAIRS), col(2 * N_PAIRS), col(0), col(0),
                  pl.BlockSpec((None, nb, 2, tq), lambda p: (p, 0, 0, 0)),
                  pl.BlockSpec((None, s, 2), lambda p: (p, 0, 0))],
        out_specs=[col(0), col(0), col(0), sums, sums],
        out_shape=[SDS((s, ATTN_W), BF16), SDS((s, ATTN_W), BF16), SDS((s, ATTN_W), BF16),
                   SDS((N_PAIRS, nb, 8, tq), F32), SDS((N_PAIRS, nb, 8, tq), F32)],
        scratch_shapes=[pltpu.VMEM((nb, rows_t, tq), F32), pltpu.VMEM((nb, 8, tq), F32),
                        pltpu.VMEM((2 * nb, tk, LANES), BF16), pltpu.VMEM((2 * nb, tq, LANES), BF16),
                        pltpu.VMEM((s, LANES), BF16), pltpu.VMEM((nb, rows_t, 2 * tq), BF16),
                        pltpu.VMEM((nb, rows_t, 2 * tk), BF16), pltpu.VMEM((nb, LANES, 2 * tq), BF16),
                        pltpu.VMEM((8, tk, tq), F32)],
        compiler_params=_cparams("arbitrary"),
        name="attn_bwd",
    )(qkv, qkv, qkv, attn_o, dmixed, rowb, ck_col)


def _forget_bwd(dc_t, fl_t, b_rows):
    rows = fl_t.shape[0]
    nb = rows // N_HEADS

    def body(dc_ref, fl_ref, b_ref, dfl_ref, db_ref):
        dc = dc_ref[...]
        lower = _iota2((LANES, LANES), 0) >= _iota2((LANES, LANES), 1)
        ones = jnp.ones((LANES, LANES), F32)
        rr, cc, same = _head_block_masks(rows, nb)
        dlf = _dot_sel(dc, lower) + _sel_dot(same & (cc > rr), _dot_sel(dc, ones))
        dfl = dlf / (1.0 + jnp.exp(fl_ref[...] + b_ref[...]))
        dfl_ref[...] = dfl
        shift = nb.bit_length() - 1
        hsel = lax.shift_right_logical(_iota2((N_HEADS, rows), 1), shift) == _iota2((N_HEADS, rows), 0)
        db_ref[...] = _sel_dot(hsel, _dot_sel(dfl, ones))

    return pl.pallas_call(body, out_shape=[SDS(fl_t.shape, F32), SDS((N_HEADS, LANES), F32)],
                          compiler_params=_cparams(), name="forget_bwd")(dc_t, fl_t, b_rows)


def _in_bwd(dq, dk, dv, du, dfl, w_in_t, x, r1, g1, dx1, *, tm):
    s = x.shape[0]
    pieces = ((0, ATTN_W), (ATTN_W, 2 * ATTN_W), (2 * ATTN_W, QKV_W), (U_OFF, F_OFF), (F_OFF, IN_PAD))

    def body(dq_ref, dk_ref, dv_ref, du_ref, df_ref, w_ref, x_ref, r_ref, g_ref, d_ref, dx_ref, dg1_ref):
        @pl.when(pl.program_id(0) == 0)
        def _():
            dg1_ref[...] = jnp.zeros_like(dg1_ref)

        dh = None
        for ref, (c0, c1) in zip((dq_ref, dk_ref, dv_ref, du_ref, df_ref), pieces):
            t = jnp.dot(ref[...], w_ref[c0:c1, :], preferred_element_type=F32)
            dh = t if dh is None else dh + t
        dx, dg1 = _norm_bwd(dh, x_ref[...], r_ref[...], g_ref[...], d_ref[...])
        dx_ref[...] = dx
        dg1_ref[...] += dg1

    row = lambda w: pl.BlockSpec((tm, w), lambda i: (i, 0))
    full = lambda a, b: pl.BlockSpec((a, b), lambda i: (0, 0))
    return pl.pallas_call(
        body,
        grid=(s // tm,),
        in_specs=[row(ATTN_W), row(ATTN_W), row(ATTN_W), row(POOL_W), row(LANES), full(IN_PAD, D_MODEL),
                  row(D_MODEL), row(1), full(1, D_MODEL), row(D_MODEL)],
        out_specs=[row(D_MODEL), full(1, D_MODEL)],
        out_shape=[SDS((s, D_MODEL), F32), SDS((1, D_MODEL), F32)],
        compiler_params=_cparams("arbitrary"),
        name="in_bwd",
    )(dq, dk, dv, du, dfl, w_in_t, x, r1, g1, dx1)


def _tiles(s):
    big = min(512, s)
    return dict(row=big, attn=min(256, s // 2), ff_rows=min(256, s))


def _tie(a, token):
    return a + token[0:1, 0:1].astype(a.dtype)


def _local_step(x, tgt, p, weight, emit, started):
    s = x.shape[0]
    t = _tiles(s)
    tm, tq = t["row"], t["attn"]
    nb = s // LANES
    nqb = s // tq
    g1, g2, gf = p["norm1_g"], p["norm2_g"], p["final_g"].reshape(1, D_MODEL)
    w_pool, pool_scale = p["w_pool"][0], p["pool_scale"]

    h, r1 = _norm1(x, _tie(g1, started), tm=tm)
    w_in_t = weight("w_in", h)
    qkv, u, fl = _in_proj(h, w_in_t, tm=tm)
    fl_t = fl[:, :N_HEADS].T.reshape(N_HEADS * nb, LANES)
    b_rows = jnp.repeat(p["b_forget"].reshape(N_HEADS), nb).reshape(N_HEADS * nb, 1)
    c = _forget_cumsum(fl_t, b_rows).reshape(N_PAIRS, 2, s)
    c_col = c.transpose(0, 2, 1)
    c_rowblk = c.reshape(N_PAIRS, 2, nqb, tq).transpose(0, 2, 1, 3)
    attn_o, lse = _attn_fwd(qkv, c_col, tk=tq)
    lse = lse.reshape(N_PAIRS, nqb // 2, 2, 2, tq).transpose(0, 1, 3, 2, 4).reshape(N_PAIRS, nqb, 2, tq)
    pooled, pool_o = _pool_fwd(u, w_pool, pool_scale, tm=tm)
    w_out = weight("w_out", attn_o)
    x1, h2, r2 = _out_norm2(attn_o, pool_o, w_out, x, g2, tm=tm)
    wg_t, wu_t = weight("w_gate_up", h2)
    gate, up, act = _gate_up(h2, wg_t, wu_t, tm=t["ff_rows"], tn=D_FF)
    wd = weight("w_down", act)
    dx2, loss_row, d_gf = _down_final(act, wd, x1, gf, tgt, tm=tm)

    dgate, dup = _swiglu_bwd(dx2, wd, gate, up, tm=t["ff_rows"], tn=D_FF)
    (d_wd,) = _mm_tn(act, [dx2], ta=D_FF, ts=tm, name="grad_w_down")
    (d_wg_t,) = _mm_tn(dgate, [h2], ta=D_FF, ts=tm, name="grad_w_gate")
    (d_wu_t,) = _mm_tn(dup, [h2], ta=D_FF, ts=tm, name="grad_w_up")
    token = emit("ff", (d_wd, d_wg_t, d_wu_t))
    dx1, dmixed, d_g2 = _mlp_in_bwd(dgate, dup, wg_t, wu_t, w_out, x1, r2, _tie(g2, token), dx2, tm=t["ff_rows"])
    du, d_wpool, d_pscale = _pool_bwd(dmixed, pooled, w_pool, pool_scale, tm=tm)
    token = emit("w_out", _mm_tn_stacked([attn_o, pool_o], [ATTN_W, POOL_W], dx1, ts=tm, name="grad_w_out"))
    rowb = _tie(c_rowblk - lse, token)
    dq, dk, dv, dck, dcq = _attn_bwd(qkv, attn_o, dmixed, rowb, c_col, tq=tq)
    dc_t = (dcq - dck)[:, :, 0::4, :].transpose(0, 2, 1, 3).reshape(N_HEADS * nb, LANES)
    dfl_t, db = _forget_bwd(dc_t, fl_t, b_rows)
    dfl = jnp.pad(dfl_t.reshape(N_HEADS, s).T, ((0, 0), (0, LANES - N_HEADS))).astype(BF16)
    d_w_in_t = _mm_tn_stacked([dq, dk, dv, dfl, du], [ATTN_W, ATTN_W, ATTN_W, N_HEADS, POOL_W], h, ts=tm,
                              name="grad_w_in")
    token = emit("w_in", d_w_in_t)
    dx, d_g1 = _in_bwd(dq, dk, dv, du, dfl, w_in_t, x, r1, _tie(g1, token), dx1, tm=tm)

    small = dict(norm1_g=d_g1, b_forget=db[:, 0].reshape(1, N_HEADS), w_pool=d_wpool, pool_scale=d_pscale,
                 norm2_g=d_g2, final_g=d_gf)
    return loss_row, dx, small


def _my_index():
    return 4 * lax.axis_index("x") + 2 * lax.axis_index("y") + lax.axis_index("c")


def _peer(k):
    pos = [lax.axis_index(a) for a in ("x", "y", "c")]
    flipped = tuple(1 - p if (k >> b) & 1 else p for p, b in zip(pos, (2, 1, 0)))
    return flipped, 4 * flipped[0] + 2 * flipped[1] + flipped[2]


_HBM = pl.BlockSpec(memory_space=pltpu.HBM)
_SEM = pl.BlockSpec(memory_space=pltpu.SEMAPHORE)
_DATAFLOW = pltpu.SideEffectType.DATAFLOW_SIDE_EFFECTING


def _peer_copies(ins, lands, send_sems, recv_sems, scatter, arrivals):
    me = _my_index()
    copies = []
    for w in range(len(ins)):
        for k in range(1, N_DEV):
            dev, idx = _peer(k)
            copies.append(pltpu.make_async_remote_copy(
                src_ref=ins[w].at[idx] if scatter[w] else ins[w], dst_ref=lands[w].at[idx if arrivals else me],
                send_sem=send_sems[w].at[k - 1], recv_sem=recv_sems[w].at[k - 1], device_id=dev, device_id_type=MESH))
    return copies


def _own_copies(ins, lands, send_sems, scatter):
    me = _my_index()
    return [pltpu.make_async_copy(ins[w].at[me] if scatter[w] else ins[w], lands[w].at[me], send_sems[w].at[N_DEV - 1])
            for w in range(len(ins))]


def _exchange_start(arrays, scatter, name):
    n = len(arrays)
    land_shapes = [(N_DEV,) + tuple(a.shape[1:] if sc else a.shape) for a, sc in zip(arrays, scatter)]

    def body(*refs):
        ins, lands = refs[:n], refs[n:2 * n]
        send_sems, recv_sems = refs[2 * n:3 * n], refs[3 * n:4 * n]
        token = refs[6 * n]
        for cp in _peer_copies(ins, lands, send_sems, recv_sems, scatter, False):
            cp.start()
        for cp in _own_copies(ins, lands, send_sems, scatter):
            cp.start()
        token[...] = jnp.zeros_like(token)

    sends, recvs = pltpu.SemaphoreType.DMA((N_DEV,)), pltpu.SemaphoreType.DMA((N_DEV - 1,))
    outs = pl.pallas_call(
        body,
        in_specs=[_HBM] * (2 * n),
        out_specs=[_SEM] * (2 * n) + [_HBM] * (2 * n) + [pl.BlockSpec(memory_space=pltpu.VMEM)],
        out_shape=[sends] * n + [recvs] * n + [pltpu.HBM(a.shape, a.dtype) for a in arrays]
        + [pltpu.HBM(sh, a.dtype) for sh, a in zip(land_shapes, arrays)] + [SDS((8, LANES), F32)],
        input_output_aliases={i: 2 * n + i for i in range(2 * n)},
        compiler_params=pltpu.CompilerParams(has_side_effects=_DATAFLOW),
        name=name,
    )(*[pltpu.with_memory_space_constraint(a, pltpu.HBM) for a in arrays],
      *[pltpu.with_memory_space_constraint(lax.empty(sh, a.dtype), pltpu.HBM) for sh, a in zip(land_shapes, arrays)])
    handles = [dict(send=outs[w], recv=outs[n + w], src=outs[2 * n + w], land=outs[3 * n + w], scatter=scatter[w])
               for w in range(n)]
    return handles, outs[4 * n]


def _exchange_wait(handles, after, name):
    n = len(handles)
    scatter = [h["scatter"] for h in handles]

    def body(*refs):
        ins, lands = refs[:n], refs[n:2 * n]
        send_sems, recv_sems = refs[2 * n:3 * n], refs[3 * n:4 * n]
        for cp in _peer_copies(ins, lands, send_sems, recv_sems, scatter, False):
            cp.wait_send()
        for cp in _peer_copies(ins, lands, send_sems, recv_sems, scatter, True):
            cp.wait_recv()
        for cp in _own_copies(ins, lands, send_sems, scatter):
            cp.wait()

    srcs, lands = [h["src"] for h in handles], [h["land"] for h in handles]
    outs = pl.pallas_call(
        body,
        in_specs=[_HBM] * (2 * n) + [_SEM] * (2 * n) + [pl.BlockSpec(memory_space=pl.ANY)],
        out_specs=[_HBM] * (2 * n),
        out_shape=[pltpu.HBM(a.shape, a.dtype) for a in srcs + lands],
        input_output_aliases={i: i for i in range(2 * n)},
        compiler_params=pltpu.CompilerParams(has_side_effects=_DATAFLOW),
        name=name,
    )(*srcs, *lands, *[h["send"] for h in handles], *[h["recv"] for h in handles], after)
    return outs[n:]


def _adamw(parts, w, m, v, name):
    rows, cols = w.shape
    tr = rows // 4 if rows % 32 == 0 else rows

    def body(p_ref, w_ref, m_ref, v_ref, g_ref, d_ref, mo_ref, vo_ref):
        g = p_ref[0].astype(F32)
        for d in range(1, N_DEV):
            g = g + p_ref[d].astype(F32)
        g_ref[...] = g
        d_ref[...], mo_ref[...], vo_ref[...] = _adam_update(g, w_ref[...], m_ref[...], v_ref[...])

    blk = pl.BlockSpec((tr, cols), lambda i: (i, 0))
    return pl.pallas_call(
        body,
        grid=(rows // tr,),
        in_specs=[pl.BlockSpec((N_DEV, tr, cols), lambda i: (0, i, 0)), blk, blk, blk],
        out_specs=[blk] * 4,
        out_shape=[SDS((rows, cols), F32)] * 4,
        compiler_params=_cparams("arbitrary"),
        name=name,
    )(parts, w, m, v)


_ROW_OF = dict(norm1_g=(0, D_MODEL), norm2_g=(1, D_MODEL), final_g=(2, D_MODEL), pool_scale=(3, POOL_W),
               b_forget=(4, N_HEADS), loss=(5, 1))


def _pack_rows(vals):
    rows = [jnp.pad(vals[n].reshape(1, width).astype(F32), ((0, 0), (0, D_MODEL - width)))
            for n, (_, width) in sorted(_ROW_OF.items(), key=lambda kv: kv[1][0])]
    return jnp.concatenate(rows + [jnp.zeros((8 - len(rows), D_MODEL), F32)], axis=0)


def _adam_update(g, w, m, v):
    m_new = ADAM_B1 * m + (1.0 - ADAM_B1) * g
    v_new = ADAM_B2 * v + (1.0 - ADAM_B2) * (g * g)
    m_hat = m_new / (1.0 - ADAM_B1 ** ADAM_STEP)
    v_hat = v_new / (1.0 - ADAM_B2 ** ADAM_STEP)
    return -ADAM_LR * (m_hat / (jnp.sqrt(v_hat) + ADAM_EPS) + ADAM_WD * w), m_new, v_new


def _adamw_replicated(parts_rows, parts_pool, w, m, v):
    names = ("norm1_g", "norm2_g", "final_g", "pool_scale", "b_forget", "w_pool")
    shapes = {n: ((len(POOL_WINDOWS), POOL_G, POOL_G) if n == "w_pool" else (1, _ROW_OF[n][1])) for n in names}

    def body(rows_ref, pool_ref, *refs):
        ins, outs = refs[:3 * len(names)], refs[3 * len(names):]

        def total(n):
            if n == "w_pool":
                pieces = [pool_ref[d] for d in range(N_DEV)]
            else:
                row, width = _ROW_OF[n]
                pieces = [rows_ref[d, row:row + 1, 0:width] for d in range(N_DEV)]
            g = pieces[0]
            for p in pieces[1:]:
                g = g + p
            return g

        outs[0][...] = total("loss")
        for k, n in enumerate(names):
            g = total(n)
            delta, m_new, v_new = _adam_update(g, ins[3 * k][...], ins[3 * k + 1][...], ins[3 * k + 2][...])
            for o_ref, val in zip(outs[1 + 4 * k:5 + 4 * k], (g, delta, m_new, v_new)):
                o_ref[...] = val

    args = [d[n].reshape(shapes[n]) for n in names for d in (w, m, v)]
    res = pl.pallas_call(
        body,
        out_shape=[SDS((1, 1), F32)] + [SDS(shapes[n], F32) for n in names for _ in range(4)],
        compiler_params=_cparams(),
        name="adamw_replicated",
    )(parts_rows, parts_pool, *args)
    return res[0], {n: [r.reshape(w[n].shape) for r in res[1 + 4 * k:5 + 4 * k]] for k, n in enumerate(names)}


def kernel(x, norm1_g, w_in, b_forget, w_pool, pool_scale, w_out, norm2_g, w_gate, w_up, w_down, final_g, loss_target, m_norm1_g, m_w_in, m_b_forget, m_w_pool, m_pool_scale, m_w_out, m_norm2_g, m_w_gate, m_w_up, m_w_down, m_final_g, v_norm1_g, v_w_in, v_b_forget, v_w_pool, v_pool_scale, v_w_out, v_norm2_g, v_w_gate, v_w_up, v_w_down, v_final_g):
    big = ("w_in", "w_out", "w_gate", "w_up", "w_down")
    order = ("norm1_g", "w_in", "b_forget", "w_pool", "pool_scale", "w_out", "norm2_g", "w_gate", "w_up", "w_down",
             "final_g")
    w = dict(norm1_g=norm1_g, w_in=w_in, b_forget=b_forget, w_pool=w_pool, pool_scale=pool_scale, w_out=w_out,
             norm2_g=norm2_g, w_gate=w_gate, w_up=w_up, w_down=w_down, final_g=final_g)
    m = dict(norm1_g=m_norm1_g, w_in=m_w_in, b_forget=m_b_forget, w_pool=m_w_pool, pool_scale=m_pool_scale,
             w_out=m_w_out, norm2_g=m_norm2_g, w_gate=m_w_gate, w_up=m_w_up, w_down=m_w_down, final_g=m_final_g)
    v = dict(norm1_g=v_norm1_g, w_in=v_w_in, b_forget=v_b_forget, w_pool=v_w_pool, pool_scale=v_pool_scale,
             w_out=v_w_out, norm2_g=v_norm2_g, w_gate=v_w_gate, w_up=v_w_up, w_down=v_w_down, final_g=v_final_g)

    flipped = ("w_in", "w_gate", "w_up")
    shard = lambda d, n: d[n][0].T if n in flipped else d[n][0]
    gather, started = _exchange_start([shard(w, n).astype(BF16) for n in big], [False] * len(big), "gather_start")
    gather = dict(zip(big, gather))

    def gathered(names, after):
        return _exchange_wait([gather[n] for n in names], after, "gather_wait_" + names[0])

    def weight(name, after):
        if name == "w_in":
            full = gathered(["w_in"], after)[0].reshape(IN_W, D_MODEL)
            f0 = QKV_W + N_HEADS
            return jnp.concatenate([full[:QKV_W], full[f0:], full[QKV_W:f0],
                                    jnp.zeros((IN_PAD - IN_W, D_MODEL), BF16)], axis=0)
        if name == "w_out":
            return gathered(["w_out"], after)[0].reshape(D_MODEL, D_MODEL)
        if name == "w_gate_up":
            return [g.reshape(D_FF, D_MODEL) for g in gathered(["w_gate", "w_up"], after)]
        return gathered(["w_down"], after)[0].reshape(D_FF, D_MODEL)

    rows = lambda g: g.reshape(N_DEV, g.shape[0] // N_DEV, g.shape[1])
    sent = {}

    def emit(name, grad):
        if name == "ff":
            names, slots = ["w_down", "w_gate", "w_up"], [rows(g) for g in grad]
        else:
            names, slots = [name], [rows(grad).astype(BF16) if name == "w_in" else rows(grad)]
        handles, token = _exchange_start(slots, [True] * len(slots), "grads_start_" + name)
        sent.update(zip(names, handles))
        return token

    loss_row, dx, small_grads = _local_step(x[0], loss_target[0], w, weight, emit, started)

    packed = _pack_rows(dict(small_grads, loss=0.5 / D_MODEL * jnp.sum(loss_row)))
    small_handles, after = _exchange_start([packed, small_grads["w_pool"]], [False, False], "grads_start_replicated")

    outs = {}
    for name in ("w_down", "w_gate", "w_up", "w_out", "w_in"):
        (parts,) = _exchange_wait([sent[name]], after, "grads_wait_" + name)
        outs[name] = _adamw(parts, shard(w, name), shard(m, name), shard(v, name), "adamw_" + name)
        after = outs[name][0]
        outs[name] = [(a.T if name in flipped else a)[None] for a in outs[name]]
    parts_rows, parts_pool = _exchange_wait(small_handles, after, "grads_wait_replicated")
    loss, small = _adamw_replicated(parts_rows, parts_pool, w, m, v)
    outs.update(small)

    return (loss.reshape(()), dx[None]) + tuple(outs[n][k] for k in range(4) for n in order)
```

```python
import functools

import jax
import jax.numpy as jnp
from jax import lax
from jax.experimental import pallas as pl
from jax.experimental.pallas import tpu as pltpu

F32 = jnp.float32
BF16 = jnp.bfloat16
SDS = jax.ShapeDtypeStruct

D_MODEL = 1024
ATTN_W = 512
N_HEADS = 8
HEAD_DIM = 64
N_PAIRS = N_HEADS // 2
POOL_W = 512
POOL_WINDOWS = (2, 4, 8, 16)
POOL_G = 128
HALO = 16
IN_W = 3 * ATTN_W + N_HEADS + POOL_W
QKV_W = 3 * ATTN_W
U_OFF = QKV_W
F_OFF = QKV_W + POOL_W
IN_PAD = F_OFF + 128
D_FF = 2816
EPS = 1e-6
NEG = -1e30
N_DEV = 8
LANES = 128

ADAM_LR = 0.001
ADAM_B1 = 0.9
ADAM_B2 = 0.999
ADAM_EPS = 1e-08
ADAM_WD = 0.01
ADAM_STEP = 10

VMEM_LIMIT_BYTES = 56 * 1024 * 1024
MESH = pl.DeviceIdType.MESH
NT = (((1,), (1,)), ((), ()))
TN = (((0,), (0,)), ((), ()))


def _cparams(*sem):
    return pltpu.CompilerParams(dimension_semantics=sem or None, vmem_limit_bytes=VMEM_LIMIT_BYTES)


def _split3(a):
    hi = a.astype(BF16)
    r1 = a - hi.astype(F32)
    mid = r1.astype(BF16)
    lo = (r1 - mid.astype(F32)).astype(BF16)
    return hi, mid, lo


def _dot_sel(a, sel, dims=None):
    sb = sel.astype(BF16)
    if dims is None:
        return sum(jnp.dot(p, sb, preferred_element_type=F32) for p in _split3(a))
    return sum(lax.dot_general(p, sb, dims, preferred_element_type=F32) for p in _split3(a))


def _sel_dot(sel, a, dims=None):
    sb = sel.astype(BF16)
    if dims is None:
        return sum(jnp.dot(sb, p, preferred_element_type=F32) for p in _split3(a))
    return sum(lax.dot_general(sb, p, dims, preferred_element_type=F32) for p in _split3(a))


def _iota2(shape, dim):
    return lax.broadcasted_iota(jnp.int32, shape, dim)


def _norm1(x, g1, *, tm):
    s = x.shape[0]

    def body(x_ref, g_ref, h_ref, r_ref):
        xv = x_ref[...]
        r = lax.rsqrt(jnp.mean(xv * xv, axis=-1, keepdims=True) + EPS)
        h_ref[...] = (xv * r * g_ref[...]).astype(BF16)
        r_ref[...] = r

    row = lambda w: pl.BlockSpec((tm, w), lambda i: (i, 0))
    return pl.pallas_call(
        body,
        grid=(s // tm,),
        in_specs=[row(D_MODEL), pl.BlockSpec((1, D_MODEL), lambda i: (0, 0))],
        out_specs=[row(D_MODEL), row(1)],
        out_shape=[SDS((s, D_MODEL), BF16), SDS((s, 1), F32)],
        compiler_params=_cparams("arbitrary"),
        name="norm1",
    )(x, g1)


def _in_proj(h, w_in_t, *, tm):
    s = h.shape[0]

    def body(h_ref, w_ref, qkv_ref, u_ref, fl_ref):
        h = h_ref[...]
        qkv_ref[...] = lax.dot_general(h, w_ref[0:QKV_W, :], NT, preferred_element_type=F32).astype(BF16)
        u_ref[...] = lax.dot_general(h, w_ref[U_OFF:F_OFF, :], NT, preferred_element_type=F32)
        fl_ref[...] = lax.dot_general(h, w_ref[F_OFF:IN_PAD, :], NT, preferred_element_type=F32)

    row = lambda w: pl.BlockSpec((tm, w), lambda i: (i, 0))
    return pl.pallas_call(
        body,
        grid=(s // tm,),
        in_specs=[row(D_MODEL), pl.BlockSpec((IN_PAD, D_MODEL), lambda i: (0, 0))],
        out_specs=[row(QKV_W), row(POOL_W), row(LANES)],
        out_shape=[SDS((s, QKV_W), BF16), SDS((s, POOL_W), F32), SDS((s, LANES), F32)],
        compiler_params=_cparams("arbitrary"),
        name="in_proj",
    )(h, w_in_t)


def _head_block_masks(rows, nb):
    shift = nb.bit_length() - 1
    rr, cc = _iota2((rows, rows), 0), _iota2((rows, rows), 1)
    same = lax.shift_right_logical(rr, shift) == lax.shift_right_logical(cc, shift)
    return rr, cc, same


def _forget_cumsum(fl_t, b_rows):
    rows = fl_t.shape[0]
    nb = rows // N_HEADS

    def body(fl_ref, b_ref, c_ref):
        z = fl_ref[...] + b_ref[...]
        lf = jnp.minimum(z, 0.0) - jnp.log1p(jnp.exp(-jnp.abs(z)))
        upper = _iota2((LANES, LANES), 0) <= _iota2((LANES, LANES), 1)
        within = _dot_sel(lf, upper)
        tot = _dot_sel(lf, jnp.ones((LANES, LANES), F32))
        rr, cc, same = _head_block_masks(rows, nb)
        c_ref[...] = within + _sel_dot(same & (cc < rr), tot)

    return pl.pallas_call(body, out_shape=SDS(fl_t.shape, F32), compiler_params=_cparams(), name="forget_cumsum")(
        fl_t, b_rows)


BIAS_LANES = 3


def _augment(t, h, col, col_first):
    n = t.shape[0]
    lane = _iota2((n, LANES), 1)
    own = (lane < HEAD_DIM) if h == 0 else (lane >= HEAD_DIM)
    b0 = HEAD_DIM if h == 0 else 0
    c0, o0 = (b0, b0 + BIAS_LANES) if col_first else (b0 + BIAS_LANES, b0)
    x = jnp.where(own, t, 0.0)
    for off, piece in enumerate(_split3(col)):
        x = jnp.where(lane == c0 + off, piece.astype(F32), x)
    x = jnp.where((lane >= o0) & (lane < o0 + BIAS_LANES), 1.0, x)
    return x.astype(BF16)


def _attn_fwd(qkv, c_col, *, tk):
    s = qkv.shape[0]
    tq = 2 * tk
    nb = s // tk

    def body(q_ref, k_ref, v_ref, cq_ref, ck_ref, o_ref, lse_ref, kp_ref, vt_ref, st_ref):
        i = pl.program_id(1)

        @pl.when(i == 0)
        def _():
            def prep(jb, _):
                st = pl.multiple_of(jb * tk, tk)
                k2 = k_ref[pl.ds(st, tk), :].astype(F32)
                ck = ck_ref[pl.ds(st, tk), :]
                for h in range(2):
                    kp_ref[h * nb + jb] = _augment(k2, h, -ck[:, h:h + 1], True)
                vt_ref[jb] = v_ref[pl.ds(st, tk), :].astype(F32).T.astype(BF16)
                return 0

            lax.fori_loop(0, nb, prep, 0)

        qs = q_ref[...].astype(F32) * 0.125
        cq = cq_ref[...]
        qp = [_augment(qs, h, cq[:, h:h + 1], False) for h in range(2)]

        def logits(j):
            return tuple(lax.dot_general(kp_ref[h * nb + j], qp[h], NT, preferred_element_type=F32) for h in range(2))

        def softmax_pv(j, slot, stats, masked):
            out = []
            for h in range(2):
                m, l, acc = stats[h]
                st = st_ref[2 * slot + h]
                if masked:
                    st = jnp.where(j * tk + _iota2((tk, tq), 0) <= i * tq + _iota2((tk, tq), 1), st, NEG)
                m_new = jnp.maximum(m, jnp.max(st, axis=0, keepdims=True))
                alpha = jnp.exp(m - m_new)
                p = jnp.exp(st - m_new)
                l = alpha * l + jnp.sum(p, axis=0, keepdims=True)
                vt = vt_ref[j, h * HEAD_DIM:(h + 1) * HEAD_DIM, :]
                acc = alpha * acc + jnp.dot(vt, p.astype(BF16), preferred_element_type=F32)
                out.append((m_new, l, acc))
            return tuple(out)

        def put(slot, j):
            for h, st in enumerate(logits(j)):
                st_ref[2 * slot + h] = st

        def pair(t, stats):
            put(1, 2 * t + 1)
            stats = softmax_pv(2 * t, 0, stats, False)
            put(0, 2 * t + 2)
            return softmax_pv(2 * t + 1, 1, stats, False)

        init = tuple((jnp.full((1, tq), NEG, F32), jnp.zeros((1, tq), F32), jnp.zeros((HEAD_DIM, tq), F32))
                     for _ in range(2))
        put(0, 0)
        stats = lax.fori_loop(0, i, pair, init)
        put(1, 2 * i + 1)
        stats = softmax_pv(2 * i, 0, stats, True)
        (ma, la, acca), (mb, lb, accb) = softmax_pv(2 * i + 1, 1, stats, True)
        o_ref[...] = jnp.concatenate([acca / la, accb / lb], axis=0).T.astype(BF16)
        lse_ref[...] = jnp.where(_iota2((2, tq), 0) == 0, ma + jnp.log(la), mb + jnp.log(lb))

    return pl.pallas_call(
        body,
        grid=(N_PAIRS, s // tq),
        in_specs=[
            pl.BlockSpec((tq, LANES), lambda p, i: (i, p)),
            pl.BlockSpec((s, LANES), lambda p, i: (0, N_PAIRS + p)),
            pl.BlockSpec((s, LANES), lambda p, i: (0, 2 * N_PAIRS + p)),
            pl.BlockSpec((None, tq, 2), lambda p, i: (p, i, 0)),
            pl.BlockSpec((None, s, 2), lambda p, i: (p, 0, 0)),
        ],
        out_specs=[
            pl.BlockSpec((tq, LANES), lambda p, i: (i, p)),
            pl.BlockSpec((None, None, 2, tq), lambda p, i: (p, i, 0, 0)),
        ],
        out_shape=[SDS((s, ATTN_W), BF16), SDS((N_PAIRS, s // tq, 2, tq), F32)],
        scratch_shapes=[pltpu.VMEM((2 * nb, tk, LANES), BF16), pltpu.VMEM((nb, LANES, tk), BF16),
                        pltpu.VMEM((4, tk, tq), F32)],
        compiler_params=_cparams("arbitrary", "arbitrary"),
        name="attn_fwd",
    )(qkv, qkv, qkv, c_col, c_col)


def _pool_counts(row0, tm, w):
    t = row0 + _iota2((tm, 1), 0)
    return jnp.minimum(t + 1, w).astype(F32)


def _pool_fwd(u, w_pool, pool_scale, *, tm):
    s = u.shape[0]

    def body(u_ref, w_ref, sc_ref, pooled_ref, po_ref, tail_ref):
        i = pl.program_id(0)

        @pl.when(i == 0)
        def _():
            tail_ref[...] = jnp.zeros_like(tail_ref)

        uv = u_ref[...]
        ext = jnp.concatenate([tail_ref[...], uv], axis=0)
        tail_ref[...] = uv[tm - HALO:, :]
        for g, w in enumerate(POOL_WINDOWS):
            cols = slice(g * POOL_G, (g + 1) * POOL_G)
            acc = ext[:, cols]
            k = 1
            while k < w:
                acc = acc + pltpu.roll(acc, k, axis=0)
                k *= 2
            pooled = (acc[HALO:, :] / _pool_counts(i * tm, tm, w) - uv[:, cols]).astype(BF16)
            pooled_ref[:, cols] = pooled
            mixed = jnp.dot(pooled, w_ref[g].astype(BF16), preferred_element_type=F32)
            po_ref[:, cols] = (mixed * sc_ref[:, cols]).astype(BF16)

    row = pl.BlockSpec((tm, POOL_W), lambda i: (i, 0))
    return pl.pallas_call(
        body,
        grid=(s // tm,),
        in_specs=[row, pl.BlockSpec((len(POOL_WINDOWS), POOL_G, POOL_G), lambda i: (0, 0, 0)),
                  pl.BlockSpec((1, POOL_W), lambda i: (0, 0))],
        out_specs=[row, row],
        out_shape=[SDS((s, POOL_W), BF16), SDS((s, POOL_W), BF16)],
        scratch_shapes=[pltpu.VMEM((HALO, POOL_W), F32)],
        compiler_params=_cparams("arbitrary"),
        name="pool_fwd",
    )(u, w_pool, pool_scale)


def _out_norm2(attn_o, pool_o, w_out, x, g2, *, tm):
    s = x.shape[0]

    def body(a_ref, p_ref, w_ref, x_ref, g_ref, x1_ref, h2_ref, r_ref):
        x1 = (x_ref[...] + jnp.dot(a_ref[...], w_ref[0:ATTN_W, :], preferred_element_type=F32)
              + jnp.dot(p_ref[...], w_ref[ATTN_W:, :], preferred_element_type=F32))
        r = lax.rsqrt(jnp.mean(x1 * x1, axis=-1, keepdims=True) + EPS)
        x1_ref[...] = x1
        r_ref[...] = r
        h2_ref[...] = (x1 * r * g_ref[...]).astype(BF16)

    row = lambda w: pl.BlockSpec((tm, w), lambda i: (i, 0))
    full = lambda a, b: pl.BlockSpec((a, b), lambda i: (0, 0))
    return pl.pallas_call(
        body,
        grid=(s // tm,),
        in_specs=[row(ATTN_W), row(POOL_W), full(D_MODEL, D_MODEL), row(D_MODEL), full(1, D_MODEL)],
        out_specs=[row(D_MODEL), row(D_MODEL), row(1)],
        out_shape=[SDS((s, D_MODEL), F32), SDS((s, D_MODEL), BF16), SDS((s, 1), F32)],
        compiler_params=_cparams("arbitrary"),
        name="out_norm2",
    )(attn_o, pool_o, w_out, x, g2)


def _gate_up(h2, wg_t, wu_t, *, tm, tn):
    s = h2.shape[0]

    def body(h_ref, wg_ref, wu_ref, gate_ref, up_ref, act_ref):
        h = h_ref[...]
        gate = lax.dot_general(h, wg_ref[...], NT, preferred_element_type=F32)
        up = lax.dot_general(h, wu_ref[...], NT, preferred_element_type=F32)
        gate_ref[...] = gate.astype(BF16)
        up_ref[...] = up.astype(BF16)
        act_ref[...] = (gate * jax.nn.sigmoid(gate) * up).astype(BF16)

    wspec = pl.BlockSpec((tn, D_MODEL), lambda c, r: (c, 0))
    ospec = pl.BlockSpec((tm, tn), lambda c, r: (r, c))
    return pl.pallas_call(
        body,
        grid=(D_FF // tn, s // tm),
        in_specs=[pl.BlockSpec((tm, D_MODEL), lambda c, r: (r, 0)), wspec, wspec],
        out_specs=[ospec, ospec, ospec],
        out_shape=[SDS((s, D_FF), BF16), SDS((s, D_FF), BF16), SDS((s, D_FF), BF16)],
        compiler_params=_cparams("arbitrary", "arbitrary"),
        name="gate_up",
    )(h2, wg_t, wu_t)


def _down_final(act, wd, x1, gf, tgt, *, tm):
    s = x1.shape[0]

    def body(a_ref, w_ref, x1_ref, g_ref, t_ref, dx2_ref, loss_ref, dgf_ref):
        @pl.when(pl.program_id(0) == 0)
        def _():
            loss_ref[...] = jnp.zeros_like(loss_ref)
            dgf_ref[...] = jnp.zeros_like(dgf_ref)

        x2 = x1_ref[...] + jnp.dot(a_ref[...], w_ref[...], preferred_element_type=F32)
        r = lax.rsqrt(jnp.mean(x2 * x2, axis=-1, keepdims=True) + EPS)
        xn = x2 * r
        g = g_ref[...]
        diff = xn * g - t_ref[...]
        loss_ref[...] += jnp.sum(diff * diff, axis=0, keepdims=True)
        dy = diff * (1.0 / D_MODEL)
        dgf_ref[...] += jnp.sum(dy * xn, axis=0, keepdims=True)
        dxn = dy * g
        dx2_ref[...] = r * (dxn - xn * jnp.mean(dxn * xn, axis=-1, keepdims=True))

    row = lambda w: pl.BlockSpec((tm, w), lambda i: (i, 0))
    full = lambda a, b: pl.BlockSpec((a, b), lambda i: (0, 0))
    return pl.pallas_call(
        body,
        grid=(s // tm,),
        in_specs=[row(D_FF), full(D_FF, D_MODEL), row(D_MODEL), full(1, D_MODEL), row(D_MODEL)],
        out_specs=[row(D_MODEL), full(1, D_MODEL), full(1, D_MODEL)],
        out_shape=[SDS((s, D_MODEL), F32), SDS((1, D_MODEL), F32), SDS((1, D_MODEL), F32)],
        compiler_params=_cparams("arbitrary"),
        name="down_final",
    )(act, wd, x1, gf, tgt)


def _swiglu_bwd(dx2, wd, gate, up, *, tm, tn):
    s = dx2.shape[0]

    def body(d_ref, w_ref, gate_ref, up_ref, dgate_ref, dup_ref):
        dact = lax.dot_general(d_ref[...].astype(BF16), w_ref[...], NT, preferred_element_type=F32)
        gate = gate_ref[...].astype(F32)
        sg = jax.nn.sigmoid(gate)
        dup_ref[...] = (dact * (gate * sg)).astype(BF16)
        dgate_ref[...] = (dact * up_ref[...].astype(F32) * (sg * (1.0 + gate * (1.0 - sg)))).astype(BF16)

    ospec = pl.BlockSpec((tm, tn), lambda c, r: (r, c))
    return pl.pallas_call(
        body,
        grid=(D_FF // tn, s // tm),
        in_specs=[pl.BlockSpec((tm, D_MODEL), lambda c, r: (r, 0)), pl.BlockSpec((tn, D_MODEL), lambda c, r: (c, 0)),
                  ospec, ospec],
        out_specs=[ospec, ospec],
        out_shape=[SDS((s, D_FF), BF16), SDS((s, D_FF), BF16)],
        compiler_params=_cparams("arbitrary", "arbitrary"),
        name="swiglu_bwd",
    )(dx2, wd, gate, up)


def _mm_tn_stacked(as_, rows, b, *, ts, name):
    s, nb_ = b.shape
    n = len(as_)
    offsets = [sum(rows[:i]) for i in range(n)]

    def body(*refs):
        a_refs, b_ref, o_ref, acc_ref = refs[:n], refs[n], refs[n + 1], refs[n + 2]
        k = pl.program_id(0)

        @pl.when(k == 0)
        def _():
            acc_ref[...] = jnp.zeros_like(acc_ref)

        bv = b_ref[...].astype(BF16)
        for a_ref, off, cnt in zip(a_refs, offsets, rows):
            part = lax.dot_general(a_ref[...].astype(BF16), bv, TN, preferred_element_type=F32)
            acc_ref[off:off + cnt, :] += part[0:cnt, :]

        @pl.when(k == s // ts - 1)
        def _():
            o_ref[...] = acc_ref[...].astype(BF16)

    return pl.pallas_call(
        body,
        grid=(s // ts,),
        in_specs=[pl.BlockSpec((ts, a.shape[1]), lambda k: (k, 0)) for a in as_] + [pl.BlockSpec((ts, nb_), lambda k: (k, 0))],
        out_specs=pl.BlockSpec((sum(rows), nb_), lambda k: (0, 0)),
        out_shape=SDS((sum(rows), nb_), BF16),
        scratch_shapes=[pltpu.VMEM((sum(rows), nb_), F32)],
        compiler_params=_cparams("arbitrary"),
        name=name,
    )(*as_, b)


def _norm_bwd(dh, x, r, g, dres):
    xn = x * r
    dxn = dh * g
    dx = dres + r * (dxn - xn * jnp.mean(dxn * xn, axis=-1, keepdims=True))
    return dx, jnp.sum(dh * xn, axis=0, keepdims=True)


def _mlp_in_bwd(dgate, dup, wg_t, wu_t, w_out, x1, r2, g2, dx2, *, tm):
    s = x1.shape[0]

    def body(dg_ref, du_ref, wg_ref, wu_ref, wo_ref, x_ref, r_ref, g_ref, d_ref, dx1_ref, dmix_ref, dg2_ref):
        @pl.when(pl.program_id(0) == 0)
        def _():
            dg2_ref[...] = jnp.zeros_like(dg2_ref)

        dh2 = (jnp.dot(dg_ref[...], wg_ref[...], preferred_element_type=F32)
               + jnp.dot(du_ref[...], wu_ref[...], preferred_element_type=F32))
        dx1, dg2 = _norm_bwd(dh2, x_ref[...], r_ref[...], g_ref[...], d_ref[...])
        dg2_ref[...] += dg2
        dx1_ref[...] = dx1
        dmix_ref[...] = lax.dot_general(dx1.astype(BF16), wo_ref[...], NT, preferred_element_type=F32)

    row = lambda w: pl.BlockSpec((tm, w), lambda i: (i, 0))
    full = lambda a, b: pl.BlockSpec((a, b), lambda i: (0, 0))
    return pl.pallas_call(
        body,
        grid=(s // tm,),
        in_specs=[row(D_FF), row(D_FF), full(D_FF, D_MODEL), full(D_FF, D_MODEL), full(D_MODEL, D_MODEL),
                  row(D_MODEL), row(1), full(1, D_MODEL), row(D_MODEL)],
        out_specs=[row(D_MODEL), row(D_MODEL), full(1, D_MODEL)],
        out_shape=[SDS((s, D_MODEL), F32), SDS((s, D_MODEL), F32), SDS((1, D_MODEL), F32)],
        compiler_params=_cparams("arbitrary"),
        name="mlp_in_bwd",
    )(dgate, dup, wg_t, wu_t, w_out, x1, r2, g2, dx2)


def _pool_bwd(dmixed, pooled, w_pool, pool_scale, *, tm):
    s = pooled.shape[0]
    nt = s // tm
    ng = len(POOL_WINDOWS)

    def body(d_ref, p_ref, w_ref, sc_ref, du_ref, dw_ref, dsc_ref, head_ref):
        i = pl.program_id(0)

        @pl.when(i == 0)
        def _():
            head_ref[...] = jnp.zeros_like(head_ref)
            dw_ref[...] = jnp.zeros_like(dw_ref)
            dsc_ref[...] = jnp.zeros_like(dsc_ref)

        row0 = (nt - 1 - i) * tm
        for g, w in enumerate(POOL_WINDOWS):
            cols = slice(g * POOL_G, (g + 1) * POOL_G)
            wb = w_ref[g].astype(BF16)
            pooled_g = p_ref[:, cols]
            dpo = d_ref[:, cols]
            mixed = jnp.dot(pooled_g, wb, preferred_element_type=F32)
            dsc_ref[:, cols] += jnp.sum(dpo * mixed, axis=0, keepdims=True)
            dmp = (dpo * sc_ref[:, cols]).astype(BF16)
            dw_ref[g] += lax.dot_general(pooled_g, dmp, TN, preferred_element_type=F32)
            dpooled = lax.dot_general(dmp, wb, NT, preferred_element_type=F32)
            a = dpooled / _pool_counts(row0, tm, w)
            acc = jnp.concatenate([a, head_ref[:, cols]], axis=0)
            head_ref[:, cols] = a[0:HALO, :]
            k = 1
            while k < w:
                acc = acc + pltpu.roll(acc, tm + HALO - k, axis=0)
                k *= 2
            du_ref[:, cols] = (acc[0:tm, :] - dpooled).astype(BF16)

    rev = lambda i: (nt - 1 - i, 0)
    return pl.pallas_call(
        body,
        grid=(nt,),
        in_specs=[pl.BlockSpec((tm, POOL_W), lambda i: (nt - 1 - i, 1)), pl.BlockSpec((tm, POOL_W), rev),
                  pl.BlockSpec((ng, POOL_G, POOL_G), lambda i: (0, 0, 0)), pl.BlockSpec((1, POOL_W), lambda i: (0, 0))],
        out_specs=[pl.BlockSpec((tm, POOL_W), rev), pl.BlockSpec((ng, POOL_G, POOL_G), lambda i: (0, 0, 0)),
                   pl.BlockSpec((1, POOL_W), lambda i: (0, 0))],
        out_shape=[SDS((s, POOL_W), BF16), SDS((ng, POOL_G, POOL_G), F32), SDS((1, POOL_W), F32)],
        scratch_shapes=[pltpu.VMEM((HALO, POOL_W), F32)],
        compiler_params=_cparams("arbitrary"),
        name="pool_bwd",
    )(dmixed, pooled, w_pool, pool_scale)


SUM_ROWS = 16


def _heads_t(t):
    n = t.shape[0]
    lane = _iota2((n, LANES), 1)
    tf = t.astype(F32)
    halves = jnp.concatenate([jnp.where(lane < HEAD_DIM, tf, 0.0).T, jnp.where(lane < HEAD_DIM, 0.0, tf).T], axis=1)
    r, c = _iota2((SUM_ROWS, 2 * n), 0), _iota2((SUM_ROWS, 2 * n), 1)
    ones = jnp.where(((r == 0) & (c < n)) | ((r == 4) & (c >= n)), 1.0, 0.0)
    return jnp.concatenate([halves, ones], axis=0).astype(BF16)


def _attn_bwd(qkv, attn_o, dmixed, rowb, ck_col, *, tq):
    s = qkv.shape[0]
    tk = tq
    nb = s // tq
    rows_t = LANES + SUM_ROWS

    def body(q_ref, k_ref, v_ref, o_ref, do_ref, rowb_ref, ck_ref, dq_ref, dk_ref, dv_ref, dck_ref, dcq_ref,
             dqt_ref, delta_ref, kp_ref, qp_ref, dob_ref, qt_ref, kt_ref, dot_ref, front_ref):
        lane = _iota2((tq, LANES), 1)
        lo = lane < HEAD_DIM
        first = _iota2((8, LANES), 1) < HEAD_DIM
        sel = jnp.where(_iota2((8, LANES), 0) < 4, jnp.where(first, 1.0, 0.0), jnp.where(first, 0.0, 1.0))

        def prep(b, _):
            st = pl.multiple_of(b * tq, tq)
            do2 = do_ref[pl.ds(st, tq), :]
            delta_ref[b] = _sel_dot(sel, do2 * o_ref[pl.ds(st, tq), :].astype(F32), NT)
            dob_ref[pl.ds(st, tq), :] = do2.astype(BF16)
            dqt_ref[b] = jnp.zeros((rows_t, tq), F32)
            k2 = k_ref[pl.ds(st, tq), :].astype(F32)
            q2 = q_ref[pl.ds(st, tq), :].astype(F32)
            ck = ck_ref[pl.ds(st, tq), :]
            for h in range(2):
                kp_ref[h * nb + b] = _augment(k2, h, -ck[:, h:h + 1], True)
                qp_ref[h * nb + b] = _augment(q2 * 0.125, h, jnp.zeros((tq, 1), F32), False)
            qt_ref[b] = _heads_t(q2)
            kt_ref[b] = _heads_t(k2)
            dot_ref[b] = _heads_t(do2)[0:LANES, :]
            return 0

        lax.fori_loop(0, nb, prep, 0)

        def split(t):
            z = jnp.zeros_like(t)
            return jnp.where(lo, t, z), jnp.where(lo, z, t)

        def kv_block(j, _):
            st_j = pl.multiple_of(j * tk, tk)
            vs = split(v_ref[pl.ds(st_j, tk), :])
            kt = kt_ref[j]

            def stage(i, slot):
                ic = jnp.minimum(i, nb - 1)
                do2 = dob_ref[pl.ds(pl.multiple_of(ic * tq, tq), tq), :]
                for h in range(2):
                    front_ref[4 * slot + h] = lax.dot_general(kp_ref[h * nb + j], qp_ref[h * nb + ic], NT,
                                                              preferred_element_type=F32)
                    front_ref[4 * slot + 2 + h] = lax.dot_general(vs[h], do2, NT, preferred_element_type=F32)

            def q_block(i, slot, carry, diagonal):
                dkt, dvt = carry
                ic = jnp.minimum(i, nb - 1)
                rb = rowb_ref[ic] + jnp.where(i < nb, 0.0, NEG)
                dl = delta_ref[ic]
                pts, dsts = [], []
                for h in range(2):
                    st = front_ref[4 * slot + h] + rb[h:h + 1, :]
                    if diagonal:
                        st = jnp.where(_iota2((tk, tq), 0) <= _iota2((tk, tq), 1), st, NEG)
                    pt = jnp.exp(st)
                    pts.append(pt.astype(BF16))
                    dsts.append((pt * (front_ref[4 * slot + 2 + h] - dl[4 * h:4 * h + 1, :])).astype(BF16))
                dvt = dvt + lax.dot_general(dot_ref[ic], jnp.concatenate(pts, axis=1), NT, preferred_element_type=F32)
                dkt = dkt + lax.dot_general(qt_ref[ic], jnp.concatenate(dsts, axis=1), NT, preferred_element_type=F32)
                dqt_ref[ic] += jnp.dot(kt, jnp.concatenate(dsts, axis=0), preferred_element_type=F32)
                return dkt, dvt

            def pair(t, carry):
                i0 = j + 1 + 2 * t
                stage(i0 + 1, 0)
                carry = q_block(i0, 1, carry, False)
                stage(i0 + 2, 1)
                return q_block(i0 + 1, 0, carry, False)

            stage(j, 0)
            stage(j + 1, 1)
            carry = q_block(j, 0, (jnp.zeros((rows_t, tk), F32), jnp.zeros((LANES, tk), F32)), True)
            dkt, dvt = lax.fori_loop(0, lax.shift_right_logical(nb - j, 1), pair, carry)
            dk_ref[pl.ds(st_j, tk), :] = (dkt[0:LANES, :].T * 0.125).astype(BF16)
            dv_ref[pl.ds(st_j, tk), :] = dvt.T.astype(BF16)
            dck_ref[j] = dkt[LANES:LANES + 8, :]
            return 0

        lax.fori_loop(0, nb, kv_block, 0)

        def finish(b, _):
            acc = dqt_ref[b]
            dq_ref[pl.ds(pl.multiple_of(b * tq, tq), tq), :] = (acc[0:LANES, :].T * 0.125).astype(BF16)
            dcq_ref[b] = acc[LANES:LANES + 8, :]
            return 0

        lax.fori_loop(0, nb, finish, 0)

    col = lambda off: pl.BlockSpec((s, LANES), lambda p: (0, off + p))
    sums = pl.BlockSpec((None, nb, 8, tq), lambda p: (p, 0, 0, 0))
    return pl.pallas_call(
        body,
        grid=(N_PAIRS,),
        in_specs=[col(0), col(N_PAIRS), col(2 * N_PAIRS), col(0), col(0),
                  pl.BlockSpec((None, nb, 2, tq), lambda p: (p, 0, 0, 0)),
                  pl.BlockSpec((None, s, 2), lambda p: (p, 0, 0))],
        out_specs=[col(0), col(0), col(0), sums, sums],
        out_shape=[SDS((s, ATTN_W), BF16), SDS((s, ATTN_W), BF16), SDS((s, ATTN_W), BF16),
                   SDS((N_PAIRS, nb, 8, tq), F32), SDS((N_PAIRS, nb, 8, tq), F32)],
        scratch_shapes=[pltpu.VMEM((nb, rows_t, tq), F32), pltpu.VMEM((nb, 8, tq), F32),
                        pltpu.VMEM((2 * nb, tk, LANES), BF16), pltpu.VMEM((2 * nb, tq, LANES), BF16),
                        pltpu.VMEM((s, LANES), BF16), pltpu.VMEM((nb, rows_t, 2 * tq), BF16),
                        pltpu.VMEM((nb, rows_t, 2 * tk), BF16), pltpu.VMEM((nb, LANES, 2 * tq), BF16),
                        pltpu.VMEM((8, tk, tq), F32)],
        compiler_params=_cparams("arbitrary"),
        name="attn_bwd",
    )(qkv, qkv, qkv, attn_o, dmixed, rowb, ck_col)


def _forget_bwd(dc_t, fl_t, b_rows):
    rows = fl_t.shape[0]
    nb = rows // N_HEADS

    def body(dc_ref, fl_ref, b_ref, dfl_ref, db_ref):
        dc = dc_ref[...]
        lower = _iota2((LANES, LANES), 0) >= _iota2((LANES, LANES), 1)
        ones = jnp.ones((LANES, LANES), F32)
        rr, cc, same = _head_block_masks(rows, nb)
        dlf = _dot_sel(dc, lower) + _sel_dot(same & (cc > rr), _dot_sel(dc, ones))
        dfl = dlf / (1.0 + jnp.exp(fl_ref[...] + b_ref[...]))
        dfl_ref[...] = dfl
        shift = nb.bit_length() - 1
        hsel = lax.shift_right_logical(_iota2((N_HEADS, rows), 1), shift) == _iota2((N_HEADS, rows), 0)
        db_ref[...] = _sel_dot(hsel, _dot_sel(dfl, ones))

    return pl.pallas_call(body, out_shape=[SDS(fl_t.shape, F32), SDS((N_HEADS, LANES), F32)],
                          compiler_params=_cparams(), name="forget_bwd")(dc_t, fl_t, b_rows)


def _in_bwd(dq, dk, dv, du, dfl, w_in_t, x, r1, g1, dx1, *, tm):
    s = x.shape[0]
    pieces = ((0, ATTN_W), (ATTN_W, 2 * ATTN_W), (2 * ATTN_W, QKV_W), (U_OFF, F_OFF), (F_OFF, IN_PAD))

    def body(dq_ref, dk_ref, dv_ref, du_ref, df_ref, w_ref, x_ref, r_ref, g_ref, d_ref, dx_ref, dg1_ref):
        @pl.when(pl.program_id(0) == 0)
        def _():
            dg1_ref[...] = jnp.zeros_like(dg1_ref)

        dh = None
        for ref, (c0, c1) in zip((dq_ref, dk_ref, dv_ref, du_ref, df_ref), pieces):
            t = jnp.dot(ref[...], w_ref[c0:c1, :], preferred_element_type=F32)
            dh = t if dh is None else dh + t
        dx, dg1 = _norm_bwd(dh, x_ref[...], r_ref[...], g_ref[...], d_ref[...])
        dx_ref[...] = dx
        dg1_ref[...] += dg1

    row = lambda w: pl.BlockSpec((tm, w), lambda i: (i, 0))
    full = lambda a, b: pl.BlockSpec((a, b), lambda i: (0, 0))
    return pl.pallas_call(
        body,
        grid=(s // tm,),
        in_specs=[row(ATTN_W), row(ATTN_W), row(ATTN_W), row(POOL_W), row(LANES), full(IN_PAD, D_MODEL),
                  row(D_MODEL), row(1), full(1, D_MODEL), row(D_MODEL)],
        out_specs=[row(D_MODEL), full(1, D_MODEL)],
        out_shape=[SDS((s, D_MODEL), F32), SDS((1, D_MODEL), F32)],
        compiler_params=_cparams("arbitrary"),
        name="in_bwd",
    )(dq, dk, dv, du, dfl, w_in_t, x, r1, g1, dx1)


def _tiles(s):
    big = min(512, s)
    return dict(row=big, attn=min(256, s // 2), ff_rows=min(256, s))


def _tie(a, token):
    return a + token[0:1, 0:1].astype(a.dtype)


def _local_step(x, tgt, p, weight, emit, started):
    s = x.shape[0]
    t = _tiles(s)
    tm, tq = t["row"], t["attn"]
    nb = s // LANES
    nqb = s // tq
    g1, g2, gf = p["norm1_g"], p["norm2_g"], p["final_g"].reshape(1, D_MODEL)
    w_pool, pool_scale = p["w_pool"][0], p["pool_scale"]

    h, r1 = _norm1(x, _tie(g1, started), tm=tm)
    w_in_t = weight("w_in", h)
    qkv, u, fl = _in_proj(h, w_in_t, tm=tm)
    fl_t = fl[:, :N_HEADS].T.reshape(N_HEADS * nb, LANES)
    b_rows = jnp.repeat(p["b_forget"].reshape(N_HEADS), nb).reshape(N_HEADS * nb, 1)
    c = _forget_cumsum(fl_t, b_rows).reshape(N_PAIRS, 2, s)
    c_col = c.transpose(0, 2, 1)
    c_rowblk = c.reshape(N_PAIRS, 2, nqb, tq).transpose(0, 2, 1, 3)
    attn_o, lse = _attn_fwd(qkv, c_col, tk=tq)
    lse = lse.reshape(N_PAIRS, nqb // 2, 2, 2, tq).transpose(0, 1, 3, 2, 4).reshape(N_PAIRS, nqb, 2, tq)
    pooled, pool_o = _pool_fwd(u, w_pool, pool_scale, tm=tm)
    w_out = weight("w_out", attn_o)
    x1, h2, r2 = _out_norm2(attn_o, pool_o, w_out, x, g2, tm=tm)
    wg_t, wu_t = weight("w_gate_up", h2)
    gate, up, act = _gate_up(h2, wg_t, wu_t, tm=t["ff_rows"], tn=D_FF)
    wd = weight("w_down", act)
    dx2, loss_row, d_gf = _down_final(act, wd, x1, gf, tgt, tm=tm)

    dgate, dup = _swiglu_bwd(dx2, wd, gate, up, tm=t["ff_rows"], tn=D_FF)
    d_wd = _mm_tn_stacked([act], [D_FF], dx2, ts=tm, name="grad_w_down")
    d_wg_t = _mm_tn_stacked([dgate], [D_FF], h2, ts=tm, name="grad_w_gate")
    d_wu_t = _mm_tn_stacked([dup], [D_FF], h2, ts=tm, name="grad_w_up")
    token = emit("ff", (d_wd, d_wg_t, d_wu_t))
    dx1, dmixed, d_g2 = _mlp_in_bwd(dgate, dup, wg_t, wu_t, w_out, x1, r2, _tie(g2, token), dx2, tm=t["ff_rows"])
    du, d_wpool, d_pscale = _pool_bwd(dmixed, pooled, w_pool, pool_scale, tm=tm)
    token = emit("w_out", _mm_tn_stacked([attn_o, pool_o], [ATTN_W, POOL_W], dx1, ts=tm, name="grad_w_out"))
    rowb = _tie(c_rowblk - lse, token)
    dq, dk, dv, dck, dcq = _attn_bwd(qkv, attn_o, dmixed, rowb, c_col, tq=tq)
    dc_t = (dcq - dck)[:, :, 0::4, :].transpose(0, 2, 1, 3).reshape(N_HEADS * nb, LANES)
    dfl_t, db = _forget_bwd(dc_t, fl_t, b_rows)
    dfl = jnp.pad(dfl_t.reshape(N_HEADS, s).T, ((0, 0), (0, LANES - N_HEADS))).astype(BF16)
    d_w_in_t = _mm_tn_stacked([dq, dk, dv, dfl, du], [ATTN_W, ATTN_W, ATTN_W, N_HEADS, POOL_W], h, ts=tm,
                              name="grad_w_in")
    token = emit("w_in", d_w_in_t)
    dx, d_g1 = _in_bwd(dq, dk, dv, du, dfl, w_in_t, x, r1, _tie(g1, token), dx1, tm=tm)

    small = dict(norm1_g=d_g1, b_forget=db[:, 0].reshape(1, N_HEADS), w_pool=d_wpool, pool_scale=d_pscale,
                 norm2_g=d_g2, final_g=d_gf)
    return loss_row, dx, small


def _my_index():
    return 4 * lax.axis_index("x") + 2 * lax.axis_index("y") + lax.axis_index("c")


def _peer(k):
    pos = [lax.axis_index(a) for a in ("x", "y", "c")]
    flipped = tuple(1 - p if (k >> b) & 1 else p for p, b in zip(pos, (2, 1, 0)))
    return flipped, 4 * flipped[0] + 2 * flipped[1] + flipped[2]


_HBM = pl.BlockSpec(memory_space=pltpu.HBM)
_SEM = pl.BlockSpec(memory_space=pltpu.SEMAPHORE)
_DATAFLOW = pltpu.SideEffectType.DATAFLOW_SIDE_EFFECTING


def _peer_copies(ins, lands, send_sems, recv_sems, scatter, arrivals):
    me = _my_index()
    copies = []
    for w in range(len(ins)):
        for k in range(1, N_DEV):
            dev, idx = _peer(k)
            copies.append(pltpu.make_async_remote_copy(
                src_ref=ins[w].at[idx] if scatter[w] else ins[w], dst_ref=lands[w].at[idx if arrivals else me],
                send_sem=send_sems[w].at[k - 1], recv_sem=recv_sems[w].at[k - 1], device_id=dev, device_id_type=MESH))
    return copies


def _own_copies(ins, lands, send_sems, scatter):
    me = _my_index()
    return [pltpu.make_async_copy(ins[w].at[me] if scatter[w] else ins[w], lands[w].at[me], send_sems[w].at[N_DEV - 1])
            for w in range(len(ins))]


def _exchange_start(arrays, scatter, name):
    n = len(arrays)
    land_shapes = [(N_DEV,) + tuple(a.shape[1:] if sc else a.shape) for a, sc in zip(arrays, scatter)]

    def body(*refs):
        ins, lands = refs[:n], refs[n:2 * n]
        send_sems, recv_sems = refs[2 * n:3 * n], refs[3 * n:4 * n]
        token = refs[6 * n]
        for cp in _peer_copies(ins, lands, send_sems, recv_sems, scatter, False):
            cp.start()
        for cp in _own_copies(ins, lands, send_sems, scatter):
            cp.start()
        token[...] = jnp.zeros_like(token)

    sends, recvs = pltpu.SemaphoreType.DMA((N_DEV,)), pltpu.SemaphoreType.DMA((N_DEV - 1,))
    outs = pl.pallas_call(
        body,
        in_specs=[_HBM] * (2 * n),
        out_specs=[_SEM] * (2 * n) + [_HBM] * (2 * n) + [pl.BlockSpec(memory_space=pltpu.VMEM)],
        out_shape=[sends] * n + [recvs] * n + [pltpu.HBM(a.shape, a.dtype) for a in arrays]
        + [pltpu.HBM(sh, a.dtype) for sh, a in zip(land_shapes, arrays)] + [SDS((8, LANES), F32)],
        input_output_aliases={i: 2 * n + i for i in range(2 * n)},
        compiler_params=pltpu.CompilerParams(has_side_effects=_DATAFLOW),
        name=name,
    )(*[pltpu.with_memory_space_constraint(a, pltpu.HBM) for a in arrays],
      *[pltpu.with_memory_space_constraint(lax.empty(sh, a.dtype), pltpu.HBM) for sh, a in zip(land_shapes, arrays)])
    handles = [dict(send=outs[w], recv=outs[n + w], src=outs[2 * n + w], land=outs[3 * n + w], scatter=scatter[w])
               for w in range(n)]
    return handles, outs[4 * n]


def _exchange_wait(handles, after, name):
    n = len(handles)
    scatter = [h["scatter"] for h in handles]

    def body(*refs):
        ins, lands = refs[:n], refs[n:2 * n]
        send_sems, recv_sems = refs[2 * n:3 * n], refs[3 * n:4 * n]
        for cp in _peer_copies(ins, lands, send_sems, recv_sems, scatter, False):
            cp.wait_send()
        for cp in _peer_copies(ins, lands, send_sems, recv_sems, scatter, True):
            cp.wait_recv()
        for cp in _own_copies(ins, lands, send_sems, scatter):
            cp.wait()

    srcs, lands = [h["src"] for h in handles], [h["land"] for h in handles]
    outs = pl.pallas_call(
        body,
        in_specs=[_HBM] * (2 * n) + [_SEM] * (2 * n) + [pl.BlockSpec(memory_space=pl.ANY)],
        out_specs=[_HBM] * (2 * n),
        out_shape=[pltpu.HBM(a.shape, a.dtype) for a in srcs + lands],
        input_output_aliases={i: i for i in range(2 * n)},
        compiler_params=pltpu.CompilerParams(has_side_effects=_DATAFLOW),
        name=name,
    )(*srcs, *lands, *[h["send"] for h in handles], *[h["recv"] for h in handles], after)
    return outs[n:]


def _adamw(parts, w, m, v, name):
    rows, cols = w.shape
    tr = rows // 4 if rows % 32 == 0 else rows

    def body(p_ref, w_ref, m_ref, v_ref, g_ref, d_ref, mo_ref, vo_ref):
        g = p_ref[0].astype(F32)
        for d in range(1, N_DEV):
            g = g + p_ref[d].astype(F32)
        g_ref[...] = g
        d_ref[...], mo_ref[...], vo_ref[...] = _adam_update(g, w_ref[...], m_ref[...], v_ref[...])

    blk = pl.BlockSpec((tr, cols), lambda i: (i, 0))
    return pl.pallas_call(
        body,
        grid=(rows // tr,),
        in_specs=[pl.BlockSpec((N_DEV, tr, cols), lambda i: (0, i, 0)), blk, blk, blk],
        out_specs=[blk] * 4,
        out_shape=[SDS((rows, cols), F32)] * 4,
        compiler_params=_cparams("arbitrary"),
        name=name,
    )(parts, w, m, v)


_ROW_OF = dict(norm1_g=(0, D_MODEL), norm2_g=(1, D_MODEL), final_g=(2, D_MODEL), pool_scale=(3, POOL_W),
               b_forget=(4, N_HEADS), loss=(5, 1))


def _pack_rows(vals):
    rows = [jnp.pad(vals[n].reshape(1, width).astype(F32), ((0, 0), (0, D_MODEL - width)))
            for n, (_, width) in sorted(_ROW_OF.items(), key=lambda kv: kv[1][0])]
    return jnp.concatenate(rows + [jnp.zeros((8 - len(rows), D_MODEL), F32)], axis=0)


def _adam_update(g, w, m, v):
    m_new = ADAM_B1 * m + (1.0 - ADAM_B1) * g
    v_new = ADAM_B2 * v + (1.0 - ADAM_B2) * (g * g)
    m_hat = m_new / (1.0 - ADAM_B1 ** ADAM_STEP)
    v_hat = v_new / (1.0 - ADAM_B2 ** ADAM_STEP)
    return -ADAM_LR * (m_hat / (jnp.sqrt(v_hat) + ADAM_EPS) + ADAM_WD * w), m_new, v_new


def _adamw_replicated(parts_rows, parts_pool, w, m, v):
    names = ("norm1_g", "norm2_g", "final_g", "pool_scale", "b_forget", "w_pool")
    shapes = {n: ((len(POOL_WINDOWS), POOL_G, POOL_G) if n == "w_pool" else (1, _ROW_OF[n][1])) for n in names}

    def body(rows_ref, pool_ref, *refs):
        ins, outs = refs[:3 * len(names)], refs[3 * len(names):]

        def total(n):
            if n == "w_pool":
                pieces = [pool_ref[d] for d in range(N_DEV)]
            else:
                row, width = _ROW_OF[n]
                pieces = [rows_ref[d, row:row + 1, 0:width] for d in range(N_DEV)]
            g = pieces[0]
            for p in pieces[1:]:
                g = g + p
            return g

        outs[0][...] = total("loss")
        for k, n in enumerate(names):
            g = total(n)
            delta, m_new, v_new = _adam_update(g, ins[3 * k][...], ins[3 * k + 1][...], ins[3 * k + 2][...])
            for o_ref, val in zip(outs[1 + 4 * k:5 + 4 * k], (g, delta, m_new, v_new)):
                o_ref[...] = val

    args = [d[n].reshape(shapes[n]) for n in names for d in (w, m, v)]
    res = pl.pallas_call(
        body,
        out_shape=[SDS((1, 1), F32)] + [SDS(shapes[n], F32) for n in names for _ in range(4)],
        compiler_params=_cparams(),
        name="adamw_replicated",
    )(parts_rows, parts_pool, *args)
    return res[0], {n: [r.reshape(w[n].shape) for r in res[1 + 4 * k:5 + 4 * k]] for k, n in enumerate(names)}


def kernel(x, norm1_g, w_in, b_forget, w_pool, pool_scale, w_out, norm2_g, w_gate, w_up, w_down, final_g, loss_target, m_norm1_g, m_w_in, m_b_forget, m_w_pool, m_pool_scale, m_w_out, m_norm2_g, m_w_gate, m_w_up, m_w_down, m_final_g, v_norm1_g, v_w_in, v_b_forget, v_w_pool, v_pool_scale, v_w_out, v_norm2_g, v_w_gate, v_w_up, v_w_down, v_final_g):
    big = ("w_in", "w_out", "w_gate", "w_up", "w_down")
    order = ("norm1_g", "w_in", "b_forget", "w_pool", "pool_scale", "w_out", "norm2_g", "w_gate", "w_up", "w_down",
             "final_g")
    w = dict(norm1_g=norm1_g, w_in=w_in, b_forget=b_forget, w_pool=w_pool, pool_scale=pool_scale, w_out=w_out,
             norm2_g=norm2_g, w_gate=w_gate, w_up=w_up, w_down=w_down, final_g=final_g)
    m = dict(norm1_g=m_norm1_g, w_in=m_w_in, b_forget=m_b_forget, w_pool=m_w_pool, pool_scale=m_pool_scale,
             w_out=m_w_out, norm2_g=m_norm2_g, w_gate=m_w_gate, w_up=m_w_up, w_down=m_w_down, final_g=m_final_g)
    v = dict(norm1_g=v_norm1_g, w_in=v_w_in, b_forget=v_b_forget, w_pool=v_w_pool, pool_scale=v_pool_scale,
             w_out=v_w_out, norm2_g=v_norm2_g, w_gate=v_w_gate, w_up=v_w_up, w_down=v_w_down, final_g=v_final_g)

    flipped = ("w_in", "w_gate", "w_up")
    shard = lambda d, n: d[n][0].T if n in flipped else d[n][0]
    gather, started = _exchange_start([shard(w, n).astype(BF16) for n in big], [False] * len(big), "gather_start")
    gather = dict(zip(big, gather))

    def gathered(names, after):
        return _exchange_wait([gather[n] for n in names], after, "gather_wait_" + names[0])

    def weight(name, after):
        if name == "w_in":
            full = gathered(["w_in"], after)[0].reshape(IN_W, D_MODEL)
            f0 = QKV_W + N_HEADS
            return jnp.concatenate([full[:QKV_W], full[f0:], full[QKV_W:f0],
                                    jnp.zeros((IN_PAD - IN_W, D_MODEL), BF16)], axis=0)
        if name == "w_out":
            return gathered(["w_out"], after)[0].reshape(D_MODEL, D_MODEL)
        if name == "w_gate_up":
            return [g.reshape(D_FF, D_MODEL) for g in gathered(["w_gate", "w_up"], after)]
        return gathered(["w_down"], after)[0].reshape(D_FF, D_MODEL)

    rows = lambda g: g.reshape(N_DEV, g.shape[0] // N_DEV, g.shape[1])
    sent = {}

    def emit(name, grad):
        if name == "ff":
            names, slots = ["w_down", "w_gate", "w_up"], [rows(g) for g in grad]
        else:
            names, slots = [name], [rows(grad)]
        handles, token = _exchange_start(slots, [True] * len(slots), "grads_start_" + name)
        sent.update(zip(names, handles))
        return token

    loss_row, dx, small_grads = _local_step(x[0], loss_target[0], w, weight, emit, started)

    packed = _pack_rows(dict(small_grads, loss=0.5 / D_MODEL * jnp.sum(loss_row)))
    small_handles, after = _exchange_start([packed, small_grads["w_pool"]], [False, False], "grads_start_replicated")

    outs = {}
    for name in ("w_down", "w_gate", "w_up", "w_out", "w_in"):
        (parts,) = _exchange_wait([sent[name]], after, "grads_wait_" + name)
        outs[name] = _adamw(parts, shard(w, name), shard(m, name), shard(v, name), "adamw_" + name)
        after = outs[name][0]
        outs[name] = [(a.T if name in flipped else a)[None] for a in outs[name]]
    parts_rows, parts_pool = _exchange_wait(small_handles, after, "grads_wait_replicated")
    loss, small = _adamw_replicated(parts_rows, parts_pool, w, m, v)
    outs.update(small)

    return (loss.reshape(()), dx[None]) + tuple(outs[n][k] for k in range(4) for n in order)
```

```python
import functools

import jax
import jax.numpy as jnp
from jax import lax
from jax.experimental import pallas as pl
from jax.experimental.pallas import tpu as pltpu

F32 = jnp.float32
BF16 = jnp.bfloat16
SDS = jax.ShapeDtypeStruct

D_MODEL = 1024
ATTN_W = 512
N_HEADS = 8
HEAD_DIM = 64
N_PAIRS = N_HEADS // 2
POOL_W = 512
POOL_WINDOWS = (2, 4, 8, 16)
POOL_G = 128
HALO = 16
IN_W = 3 * ATTN_W + N_HEADS + POOL_W
QKV_W = 3 * ATTN_W
U_OFF = QKV_W
F_OFF = QKV_W + POOL_W
IN_PAD = F_OFF + 128
D_FF = 2816
EPS = 1e-6
NEG = -1e30
N_DEV = 8
LANES = 128

ADAM_LR = 0.001
ADAM_B1 = 0.9
ADAM_B2 = 0.999
ADAM_EPS = 1e-08
ADAM_WD = 0.01
ADAM_STEP = 10

VMEM_LIMIT_BYTES = 56 * 1024 * 1024
MESH = pl.DeviceIdType.MESH
NT = (((1,), (1,)), ((), ()))
TN = (((0,), (0,)), ((), ()))


def _cparams(*sem):
    return pltpu.CompilerParams(dimension_semantics=sem or None, vmem_limit_bytes=VMEM_LIMIT_BYTES)


def _split3(a):
    hi = a.astype(BF16)
    r1 = a - hi.astype(F32)
    mid = r1.astype(BF16)
    lo = (r1 - mid.astype(F32)).astype(BF16)
    return hi, mid, lo


def _dot_sel(a, sel, dims=None):
    sb = sel.astype(BF16)
    if dims is None:
        return sum(jnp.dot(p, sb, preferred_element_type=F32) for p in _split3(a))
    return sum(lax.dot_general(p, sb, dims, preferred_element_type=F32) for p in _split3(a))


def _sel_dot(sel, a, dims=None):
    sb = sel.astype(BF16)
    if dims is None:
        return sum(jnp.dot(sb, p, preferred_element_type=F32) for p in _split3(a))
    return sum(lax.dot_general(sb, p, dims, preferred_element_type=F32) for p in _split3(a))


def _iota2(shape, dim):
    return lax.broadcasted_iota(jnp.int32, shape, dim)


def _norm1(x, g1, *, tm):
    s = x.shape[0]

    def body(x_ref, g_ref, h_ref, r_ref):
        xv = x_ref[...]
        r = lax.rsqrt(jnp.mean(xv * xv, axis=-1, keepdims=True) + EPS)
        h_ref[...] = (xv * r * g_ref[...]).astype(BF16)
        r_ref[...] = r

    row = lambda w: pl.BlockSpec((tm, w), lambda i: (i, 0))
    return pl.pallas_call(
        body,
        grid=(s // tm,),
        in_specs=[row(D_MODEL), pl.BlockSpec((1, D_MODEL), lambda i: (0, 0))],
        out_specs=[row(D_MODEL), row(1)],
        out_shape=[SDS((s, D_MODEL), BF16), SDS((s, 1), F32)],
        compiler_params=_cparams("arbitrary"),
        name="norm1",
    )(x, g1)


def _in_proj(h, w_in_t, *, tm):
    s = h.shape[0]

    def body(h_ref, w_ref, qkv_ref, u_ref, fl_ref):
        h = h_ref[...]
        qkv_ref[...] = lax.dot_general(h, w_ref[0:QKV_W, :], NT, preferred_element_type=F32).astype(BF16)
        u_ref[...] = lax.dot_general(h, w_ref[U_OFF:F_OFF, :], NT, preferred_element_type=F32)
        fl_ref[...] = lax.dot_general(h, w_ref[F_OFF:IN_PAD, :], NT, preferred_element_type=F32)

    row = lambda w: pl.BlockSpec((tm, w), lambda i: (i, 0))
    return pl.pallas_call(
        body,
        grid=(s // tm,),
        in_specs=[row(D_MODEL), pl.BlockSpec((IN_PAD, D_MODEL), lambda i: (0, 0))],
        out_specs=[row(QKV_W), row(POOL_W), row(LANES)],
        out_shape=[SDS((s, QKV_W), BF16), SDS((s, POOL_W), F32), SDS((s, LANES), F32)],
        compiler_params=_cparams("arbitrary"),
        name="in_proj",
    )(h, w_in_t)


def _head_block_masks(rows, nb):
    shift = nb.bit_length() - 1
    rr, cc = _iota2((rows, rows), 0), _iota2((rows, rows), 1)
    same = lax.shift_right_logical(rr, shift) == lax.shift_right_logical(cc, shift)
    return rr, cc, same


def _forget_cumsum(fl_t, b_rows):
    rows = fl_t.shape[0]
    nb = rows // N_HEADS

    def body(fl_ref, b_ref, c_ref):
        z = fl_ref[...] + b_ref[...]
        lf = jnp.minimum(z, 0.0) - jnp.log1p(jnp.exp(-jnp.abs(z)))
        upper = _iota2((LANES, LANES), 0) <= _iota2((LANES, LANES), 1)
        within = _dot_sel(lf, upper)
        tot = _dot_sel(lf, jnp.ones((LANES, LANES), F32))
        rr, cc, same = _head_block_masks(rows, nb)
        c_ref[...] = within + _sel_dot(same & (cc < rr), tot)

    return pl.pallas_call(body, out_shape=SDS(fl_t.shape, F32), compiler_params=_cparams(), name="forget_cumsum")(
        fl_t, b_rows)


BIAS_LANES = 3


def _augment(t, h, col, col_first):
    n = t.shape[0]
    lane = _iota2((n, LANES), 1)
    own = (lane < HEAD_DIM) if h == 0 else (lane >= HEAD_DIM)
    b0 = HEAD_DIM if h == 0 else 0
    c0, o0 = (b0, b0 + BIAS_LANES) if col_first else (b0 + BIAS_LANES, b0)
    x = jnp.where(own, t, 0.0)
    for off, piece in enumerate(_split3(col)):
        x = jnp.where(lane == c0 + off, piece.astype(F32), x)
    x = jnp.where((lane >= o0) & (lane < o0 + BIAS_LANES), 1.0, x)
    return x.astype(BF16)


def _attn_fwd(qkv, c_col, *, tk):
    s = qkv.shape[0]
    tq = 2 * tk
    nb = s // tk

    def body(q_ref, k_ref, v_ref, cq_ref, ck_ref, o_ref, lse_ref, kp_ref, vt_ref, st_ref):
        i = pl.program_id(1)

        @pl.when(i == 0)
        def _():
            def prep(jb, _):
                st = pl.multiple_of(jb * tk, tk)
                k2 = k_ref[pl.ds(st, tk), :].astype(F32)
                ck = ck_ref[pl.ds(st, tk), :]
                for h in range(2):
                    kp_ref[h * nb + jb] = _augment(k2, h, -ck[:, h:h + 1], True)
                vt_ref[jb] = v_ref[pl.ds(st, tk), :].astype(F32).T.astype(BF16)
                return 0

            lax.fori_loop(0, nb, prep, 0)

        qs = q_ref[...].astype(F32) * 0.125
        cq = cq_ref[...]
        qp = [_augment(qs, h, cq[:, h:h + 1], False) for h in range(2)]

        def logits(j):
            return tuple(lax.dot_general(kp_ref[h * nb + j], qp[h], NT, preferred_element_type=F32) for h in range(2))

        def softmax_pv(j, slot, stats, masked):
            out = []
            for h in range(2):
                m, l, acc = stats[h]
                st = st_ref[2 * slot + h]
                if masked:
                    st = jnp.where(j * tk + _iota2((tk, tq), 0) <= i * tq + _iota2((tk, tq), 1), st, NEG)
                m_new = jnp.maximum(m, jnp.max(st, axis=0, keepdims=True))
                alpha = jnp.exp(m - m_new)
                p = jnp.exp(st - m_new)
                l = alpha * l + jnp.sum(p, axis=0, keepdims=True)
                vt = vt_ref[j, h * HEAD_DIM:(h + 1) * HEAD_DIM, :]
                acc = alpha * acc + jnp.dot(vt, p.astype(BF16), preferred_element_type=F32)
                out.append((m_new, l, acc))
            return tuple(out)

        def put(slot, j):
            for h, st in enumerate(logits(j)):
                st_ref[2 * slot + h] = st

        def pair(t, stats):
            put(1, 2 * t + 1)
            stats = softmax_pv(2 * t, 0, stats, False)
            put(0, 2 * t + 2)
            return softmax_pv(2 * t + 1, 1, stats, False)

        init = tuple((jnp.full((1, tq), NEG, F32), jnp.zeros((1, tq), F32), jnp.zeros((HEAD_DIM, tq), F32))
                     for _ in range(2))
        put(0, 0)
        stats = lax.fori_loop(0, i, pair, init)
        put(1, 2 * i + 1)
        stats = softmax_pv(2 * i, 0, stats, True)
        (ma, la, acca), (mb, lb, accb) = softmax_pv(2 * i + 1, 1, stats, True)
        o_ref[...] = jnp.concatenate([acca / la, accb / lb], axis=0).T.astype(BF16)
        lse_ref[...] = jnp.where(_iota2((2, tq), 0) == 0, ma + jnp.log(la), mb + jnp.log(lb))

    return pl.pallas_call(
        body,
        grid=(N_PAIRS, s // tq),
        in_specs=[
            pl.BlockSpec((tq, LANES), lambda p, i: (i, p)),
            pl.BlockSpec((s, LANES), lambda p, i: (0, N_PAIRS + p)),
            pl.BlockSpec((s, LANES), lambda p, i: (0, 2 * N_PAIRS + p)),
            pl.BlockSpec((None, tq, 2), lambda p, i: (p, i, 0)),
            pl.BlockSpec((None, s, 2), lambda p, i: (p, 0, 0)),
        ],
        out_specs=[
            pl.BlockSpec((tq, LANES), lambda p, i: (i, p)),
            pl.BlockSpec((None, None, 2, tq), lambda p, i: (p, i, 0, 0)),
        ],
        out_shape=[SDS((s, ATTN_W), BF16), SDS((N_PAIRS, s // tq, 2, tq), F32)],
        scratch_shapes=[pltpu.VMEM((2 * nb, tk, LANES), BF16), pltpu.VMEM((nb, LANES, tk), BF16),
                        pltpu.VMEM((4, tk, tq), F32)],
        compiler_params=_cparams("arbitrary", "arbitrary"),
        name="attn_fwd",
    )(qkv, qkv, qkv, c_col, c_col)


def _pool_counts(row0, tm, w):
    t = row0 + _iota2((tm, 1), 0)
    return jnp.minimum(t + 1, w).astype(F32)


def _pool_fwd(u, w_pool, pool_scale, *, tm):
    s = u.shape[0]

    def body(u_ref, w_ref, sc_ref, pooled_ref, po_ref, tail_ref):
        i = pl.program_id(0)

        @pl.when(i == 0)
        def _():
            tail_ref[...] = jnp.zeros_like(tail_ref)

        uv = u_ref[...]
        ext = jnp.concatenate([tail_ref[...], uv], axis=0)
        tail_ref[...] = uv[tm - HALO:, :]
        for g, w in enumerate(POOL_WINDOWS):
            cols = slice(g * POOL_G, (g + 1) * POOL_G)
            acc = ext[:, cols]
            k = 1
            while k < w:
                acc = acc + pltpu.roll(acc, k, axis=0)
                k *= 2
            pooled = (acc[HALO:, :] / _pool_counts(i * tm, tm, w) - uv[:, cols]).astype(BF16)
            pooled_ref[:, cols] = pooled
            mixed = jnp.dot(pooled, w_ref[g].astype(BF16), preferred_element_type=F32)
            po_ref[:, cols] = (mixed * sc_ref[:, cols]).astype(BF16)

    row = pl.BlockSpec((tm, POOL_W), lambda i: (i, 0))
    return pl.pallas_call(
        body,
        grid=(s // tm,),
        in_specs=[row, pl.BlockSpec((len(POOL_WINDOWS), POOL_G, POOL_G), lambda i: (0, 0, 0)),
                  pl.BlockSpec((1, POOL_W), lambda i: (0, 0))],
        out_specs=[row, row],
        out_shape=[SDS((s, POOL_W), BF16), SDS((s, POOL_W), BF16)],
        scratch_shapes=[pltpu.VMEM((HALO, POOL_W), F32)],
        compiler_params=_cparams("arbitrary"),
        name="pool_fwd",
    )(u, w_pool, pool_scale)


def _out_norm2(attn_o, pool_o, w_out, x, g2, *, tm):
    s = x.shape[0]

    def body(a_ref, p_ref, w_ref, x_ref, g_ref, x1_ref, h2_ref, r_ref):
        x1 = (x_ref[...] + jnp.dot(a_ref[...], w_ref[0:ATTN_W, :], preferred_element_type=F32)
              + jnp.dot(p_ref[...], w_ref[ATTN_W:, :], preferred_element_type=F32))
        r = lax.rsqrt(jnp.mean(x1 * x1, axis=-1, keepdims=True) + EPS)
        x1_ref[...] = x1
        r_ref[...] = r
        h2_ref[...] = (x1 * r * g_ref[...]).astype(BF16)

    row = lambda w: pl.BlockSpec((tm, w), lambda i: (i, 0))
    full = lambda a, b: pl.BlockSpec((a, b), lambda i: (0, 0))
    return pl.pallas_call(
        body,
        grid=(s // tm,),
        in_specs=[row(ATTN_W), row(POOL_W), full(D_MODEL, D_MODEL), row(D_MODEL), full(1, D_MODEL)],
        out_specs=[row(D_MODEL), row(D_MODEL), row(1)],
        out_shape=[SDS((s, D_MODEL), F32), SDS((s, D_MODEL), BF16), SDS((s, 1), F32)],
        compiler_params=_cparams("arbitrary"),
        name="out_norm2",
    )(attn_o, pool_o, w_out, x, g2)


def _gate_up(h2, wg_t, wu_t, *, tm, tn):
    s = h2.shape[0]

    def body(h_ref, wg_ref, wu_ref, gate_ref, up_ref, act_ref):
        h = h_ref[...]
        gate = lax.dot_general(h, wg_ref[...], NT, preferred_element_type=F32)
        up = lax.dot_general(h, wu_ref[...], NT, preferred_element_type=F32)
        gate_ref[...] = gate.astype(BF16)
        up_ref[...] = up.astype(BF16)
        act_ref[...] = (gate * jax.nn.sigmoid(gate) * up).astype(BF16)

    wspec = pl.BlockSpec((tn, D_MODEL), lambda c, r: (c, 0))
    ospec = pl.BlockSpec((tm, tn), lambda c, r: (r, c))
    return pl.pallas_call(
        body,
        grid=(D_FF // tn, s // tm),
        in_specs=[pl.BlockSpec((tm, D_MODEL), lambda c, r: (r, 0)), wspec, wspec],
        out_specs=[ospec, ospec, ospec],
        out_shape=[SDS((s, D_FF), BF16), SDS((s, D_FF), BF16), SDS((s, D_FF), BF16)],
        compiler_params=_cparams("arbitrary", "arbitrary"),
        name="gate_up",
    )(h2, wg_t, wu_t)


def _down_final(act, wd, x1, gf, tgt, *, tm):
    s = x1.shape[0]

    def body(a_ref, w_ref, x1_ref, g_ref, t_ref, dx2_ref, loss_ref, dgf_ref):
        @pl.when(pl.program_id(0) == 0)
        def _():
            loss_ref[...] = jnp.zeros_like(loss_ref)
            dgf_ref[...] = jnp.zeros_like(dgf_ref)

        x2 = x1_ref[...] + jnp.dot(a_ref[...], w_ref[...], preferred_element_type=F32)
        r = lax.rsqrt(jnp.mean(x2 * x2, axis=-1, keepdims=True) + EPS)
        xn = x2 * r
        g = g_ref[...]
        diff = xn * g - t_ref[...]
        loss_ref[...] += jnp.sum(diff * diff, axis=0, keepdims=True)
        dy = diff * (1.0 / D_MODEL)
        dgf_ref[...] += jnp.sum(dy * xn, axis=0, keepdims=True)
        dxn = dy * g
        dx2_ref[...] = r * (dxn - xn * jnp.mean(dxn * xn, axis=-1, keepdims=True))

    row = lambda w: pl.BlockSpec((tm, w), lambda i: (i, 0))
    full = lambda a, b: pl.BlockSpec((a, b), lambda i: (0, 0))
    return pl.pallas_call(
        body,
        grid=(s // tm,),
        in_specs=[row(D_FF), full(D_FF, D_MODEL), row(D_MODEL), full(1, D_MODEL), row(D_MODEL)],
        out_specs=[row(D_MODEL), full(1, D_MODEL), full(1, D_MODEL)],
        out_shape=[SDS((s, D_MODEL), F32), SDS((1, D_MODEL), F32), SDS((1, D_MODEL), F32)],
        compiler_params=_cparams("arbitrary"),
        name="down_final",
    )(act, wd, x1, gf, tgt)


def _swiglu_bwd(dx2, wd, gate, up, *, tm, tn):
    s = dx2.shape[0]

    def body(d_ref, w_ref, gate_ref, up_ref, dgate_ref, dup_ref):
        dact = lax.dot_general(d_ref[...].astype(BF16), w_ref[...], NT, preferred_element_type=F32)
        gate = gate_ref[...].astype(F32)
        sg = jax.nn.sigmoid(gate)
        dup_ref[...] = (dact * (gate * sg)).astype(BF16)
        dgate_ref[...] = (dact * up_ref[...].astype(F32) * (sg * (1.0 + gate * (1.0 - sg)))).astype(BF16)

    ospec = pl.BlockSpec((tm, tn), lambda c, r: (r, c))
    return pl.pallas_call(
        body,
        grid=(D_FF // tn, s // tm),
        in_specs=[pl.BlockSpec((tm, D_MODEL), lambda c, r: (r, 0)), pl.BlockSpec((tn, D_MODEL), lambda c, r: (c, 0)),
                  ospec, ospec],
        out_specs=[ospec, ospec],
        out_shape=[SDS((s, D_FF), BF16), SDS((s, D_FF), BF16)],
        compiler_params=_cparams("arbitrary", "arbitrary"),
        name="swiglu_bwd",
    )(dx2, wd, gate, up)


def _mm_tn_stacked(as_, rows, b, *, ts, name):
    s, nb_ = b.shape
    n = len(as_)
    offsets = [sum(rows[:i]) for i in range(n)]

    def body(*refs):
        a_refs, b_ref, o_ref, acc_ref = refs[:n], refs[n], refs[n + 1], refs[n + 2]
        k = pl.program_id(0)

        @pl.when(k == 0)
        def _():
            acc_ref[...] = jnp.zeros_like(acc_ref)

        bv = b_ref[...].astype(BF16)
        for a_ref, off, cnt in zip(a_refs, offsets, rows):
            part = lax.dot_general(a_ref[...].astype(BF16), bv, TN, preferred_element_type=F32)
            acc_ref[off:off + cnt, :] += part[0:cnt, :]

        @pl.when(k == s // ts - 1)
        def _():
            o_ref[...] = acc_ref[...].astype(BF16)

    return pl.pallas_call(
        body,
        grid=(s // ts,),
        in_specs=[pl.BlockSpec((ts, a.shape[1]), lambda k: (k, 0)) for a in as_] + [pl.BlockSpec((ts, nb_), lambda k: (k, 0))],
        out_specs=pl.BlockSpec((sum(rows), nb_), lambda k: (0, 0)),
        out_shape=SDS((sum(rows), nb_), BF16),
        scratch_shapes=[pltpu.VMEM((sum(rows), nb_), F32)],
        compiler_params=_cparams("arbitrary"),
        name=name,
    )(*as_, b)


def _norm_bwd(dh, x, r, g, dres):
    xn = x * r
    dxn = dh * g
    dx = dres + r * (dxn - xn * jnp.mean(dxn * xn, axis=-1, keepdims=True))
    return dx, jnp.sum(dh * xn, axis=0, keepdims=True)


def _mlp_in_bwd(dgate, dup, wg_t, wu_t, w_out, x1, r2, g2, dx2, *, tm):
    s = x1.shape[0]

    def body(dg_ref, du_ref, wg_ref, wu_ref, wo_ref, x_ref, r_ref, g_ref, d_ref, dx1_ref, dmix_ref, dg2_ref):
        @pl.when(pl.program_id(0) == 0)
        def _():
            dg2_ref[...] = jnp.zeros_like(dg2_ref)

        dh2 = (jnp.dot(dg_ref[...], wg_ref[...], preferred_element_type=F32)
               + jnp.dot(du_ref[...], wu_ref[...], preferred_element_type=F32))
        dx1, dg2 = _norm_bwd(dh2, x_ref[...], r_ref[...], g_ref[...], d_ref[...])
        dg2_ref[...] += dg2
        dx1_ref[...] = dx1
        dmix_ref[...] = lax.dot_general(dx1.astype(BF16), wo_ref[...], NT, preferred_element_type=F32)

    row = lambda w: pl.BlockSpec((tm, w), lambda i: (i, 0))
    full = lambda a, b: pl.BlockSpec((a, b), lambda i: (0, 0))
    return pl.pallas_call(
        body,
        grid=(s // tm,),
        in_specs=[row(D_FF), row(D_FF), full(D_FF, D_MODEL), full(D_FF, D_MODEL), full(D_MODEL, D_MODEL),
                  row(D_MODEL), row(1), full(1, D_MODEL), row(D_MODEL)],
        out_specs=[row(D_MODEL), row(D_MODEL), full(1, D_MODEL)],
        out_shape=[SDS((s, D_MODEL), F32), SDS((s, D_MODEL), F32), SDS((1, D_MODEL), F32)],
        compiler_params=_cparams("arbitrary"),
        name="mlp_in_bwd",
    )(dgate, dup, wg_t, wu_t, w_out, x1, r2, g2, dx2)


def _pool_bwd(dmixed, pooled, w_pool, pool_scale, *, tm):
    s = pooled.shape[0]
    nt = s // tm
    ng = len(POOL_WINDOWS)

    def body(d_ref, p_ref, w_ref, sc_ref, du_ref, dw_ref, dsc_ref, head_ref):
        i = pl.program_id(0)

        @pl.when(i == 0)
        def _():
            head_ref[...] = jnp.zeros_like(head_ref)
            dw_ref[...] = jnp.zeros_like(dw_ref)
            dsc_ref[...] = jnp.zeros_like(dsc_ref)

        row0 = (nt - 1 - i) * tm
        for g, w in enumerate(POOL_WINDOWS):
            cols = slice(g * POOL_G, (g + 1) * POOL_G)
            wb = w_ref[g].astype(BF16)
            pooled_g = p_ref[:, cols]
            dpo = d_ref[:, cols]
            mixed = jnp.dot(pooled_g, wb, preferred_element_type=F32)
            dsc_ref[:, cols] += jnp.sum(dpo * mixed, axis=0, keepdims=True)
            dmp = (dpo * sc_ref[:, cols]).astype(BF16)
            dw_ref[g] += lax.dot_general(pooled_g, dmp, TN, preferred_element_type=F32)
            dpooled = lax.dot_general(dmp, wb, NT, preferred_element_type=F32)
            a = dpooled / _pool_counts(row0, tm, w)
            acc = jnp.concatenate([a, head_ref[:, cols]], axis=0)
            head_ref[:, cols] = a[0:HALO, :]
            k = 1
            while k < w:
                acc = acc + pltpu.roll(acc, tm + HALO - k, axis=0)
                k *= 2
            du_ref[:, cols] = (acc[0:tm, :] - dpooled).astype(BF16)

    rev = lambda i: (nt - 1 - i, 0)
    return pl.pallas_call(
        body,
        grid=(nt,),
        in_specs=[pl.BlockSpec((tm, POOL_W), lambda i: (nt - 1 - i, 1)), pl.BlockSpec((tm, POOL_W), rev),
                  pl.BlockSpec((ng, POOL_G, POOL_G), lambda i: (0, 0, 0)), pl.BlockSpec((1, POOL_W), lambda i: (0, 0))],
        out_specs=[pl.BlockSpec((tm, POOL_W), rev), pl.BlockSpec((ng, POOL_G, POOL_G), lambda i: (0, 0, 0)),
                   pl.BlockSpec((1, POOL_W), lambda i: (0, 0))],
        out_shape=[SDS((s, POOL_W), BF16), SDS((ng, POOL_G, POOL_G), F32), SDS((1, POOL_W), F32)],
        scratch_shapes=[pltpu.VMEM((HALO, POOL_W), F32)],
        compiler_params=_cparams("arbitrary"),
        name="pool_bwd",
    )(dmixed, pooled, w_pool, pool_scale)


SUM_ROWS = 16


def _heads_t(t):
    n = t.shape[0]
    lane = _iota2((n, LANES), 1)
    tf = t.astype(F32)
    halves = jnp.concatenate([jnp.where(lane < HEAD_DIM, tf, 0.0).T, jnp.where(lane < HEAD_DIM, 0.0, tf).T], axis=1)
    r, c = _iota2((SUM_ROWS, 2 * n), 0), _iota2((SUM_ROWS, 2 * n), 1)
    ones = jnp.where(((r == 0) & (c < n)) | ((r == 4) & (c >= n)), 1.0, 0.0)
    return jnp.concatenate([halves, ones], axis=0).astype(BF16)


def _attn_bwd(qkv, attn_o, dmixed, rowb, ck_col, *, tq):
    s = qkv.shape[0]
    tk = tq
    nb = s // tq
    rows_t = LANES + SUM_ROWS

    def body(q_ref, k_ref, v_ref, o_ref, do_ref, rowb_ref, ck_ref, dq_ref, dk_ref, dv_ref, dck_ref, dcq_ref,
             dqt_ref, delta_ref, kp_ref, qp_ref, dob_ref, qt_ref, kt_ref, dot_ref, front_ref):
        lane = _iota2((tq, LANES), 1)
        lo = lane < HEAD_DIM
        first = _iota2((8, LANES), 1) < HEAD_DIM
        sel = jnp.where(_iota2((8, LANES), 0) < 4, jnp.where(first, 1.0, 0.0), jnp.where(first, 0.0, 1.0))

        def prep(b, _):
            st = pl.multiple_of(b * tq, tq)
            do2 = do_ref[pl.ds(st, tq), :]
            delta_ref[b] = _sel_dot(sel, do2 * o_ref[pl.ds(st, tq), :].astype(F32), NT)
            dob_ref[pl.ds(st, tq), :] = do2.astype(BF16)
            dqt_ref[b] = jnp.zeros((rows_t, tq), F32)
            k2 = k_ref[pl.ds(st, tq), :].astype(F32)
            q2 = q_ref[pl.ds(st, tq), :].astype(F32)
            ck = ck_ref[pl.ds(st, tq), :]
            for h in range(2):
                kp_ref[h * nb + b] = _augment(k2, h, -ck[:, h:h + 1], True)
                qp_ref[h * nb + b] = _augment(q2 * 0.125, h, jnp.zeros((tq, 1), F32), False)
            qt_ref[b] = _heads_t(q2)
            kt_ref[b] = _heads_t(k2)
            dot_ref[b] = _heads_t(do2)[0:LANES, :]
            return 0

        lax.fori_loop(0, nb, prep, 0)

        def split(t):
            z = jnp.zeros_like(t)
            return jnp.where(lo, t, z), jnp.where(lo, z, t)

        def kv_block(j, _):
            st_j = pl.multiple_of(j * tk, tk)
            vs = split(v_ref[pl.ds(st_j, tk), :])
            kt = kt_ref[j]

            def stage(i, slot):
                ic = jnp.minimum(i, nb - 1)
                do2 = dob_ref[pl.ds(pl.multiple_of(ic * tq, tq), tq), :]
                for h in range(2):
                    front_ref[4 * slot + h] = lax.dot_general(kp_ref[h * nb + j], qp_ref[h * nb + ic], NT,
                                                              preferred_element_type=F32)
                    front_ref[4 * slot + 2 + h] = lax.dot_general(vs[h], do2, NT, preferred_element_type=F32)

            def q_block(i, slot, carry, diagonal):
                dkt, dvt = carry
                ic = jnp.minimum(i, nb - 1)
                rb = rowb_ref[ic] + jnp.where(i < nb, 0.0, NEG)
                dl = delta_ref[ic]
                pts, dsts = [], []
                for h in range(2):
                    st = front_ref[4 * slot + h] + rb[h:h + 1, :]
                    if diagonal:
                        st = jnp.where(_iota2((tk, tq), 0) <= _iota2((tk, tq), 1), st, NEG)
                    pt = jnp.exp(st)
                    pts.append(pt.astype(BF16))
                    dsts.append((pt * (front_ref[4 * slot + 2 + h] - dl[4 * h:4 * h + 1, :])).astype(BF16))
                dvt = dvt + lax.dot_general(dot_ref[ic], jnp.concatenate(pts, axis=1), NT, preferred_element_type=F32)
                dkt = dkt + lax.dot_general(qt_ref[ic], jnp.concatenate(dsts, axis=1), NT, preferred_element_type=F32)
                dqt_ref[ic] += jnp.dot(kt, jnp.concatenate(dsts, axis=0), preferred_element_type=F32)
                return dkt, dvt

            def pair(t, carry):
                i0 = j + 1 + 2 * t
                stage(i0 + 1, 0)
                carry = q_block(i0, 1, carry, False)
                stage(i0 + 2, 1)
                return q_block(i0 + 1, 0, carry, False)

            stage(j, 0)
            stage(j + 1, 1)
            carry = q_block(j, 0, (jnp.zeros((rows_t, tk), F32), jnp.zeros((LANES, tk), F32)), True)
            dkt, dvt = lax.fori_loop(0, lax.shift_right_logical(nb - j, 1), pair, carry)
            dk_ref[pl.ds(st_j, tk), :] = (dkt[0:LANES, :].T * 0.125).astype(BF16)
            dv_ref[pl.ds(st_j, tk), :] = dvt.T.astype(BF16)
            dck_ref[j] = dkt[LANES:LANES + 8, :]
            return 0

        lax.fori_loop(0, nb, kv_block, 0)

        def finish(b, _):
            acc = dqt_ref[b]
            dq_ref[pl.ds(pl.multiple_of(b * tq, tq), tq), :] = (acc[0:LANES, :].T * 0.125).astype(BF16)
            dcq_ref[b] = acc[LANES:LANES + 8, :]
            return 0

        lax.fori_loop(0, nb, finish, 0)

    col = lambda off: pl.BlockSpec((s, LANES), lambda p: (0, off + p))
    sums = pl.BlockSpec((None, nb, 8, tq), lambda p: (p, 0, 0, 0))
    return pl.pallas_call(
        body,
        grid=(N_PAIRS,),
        in_specs=[col(0), col(N_PAIRS), col(2 * N_PAIRS), col(0), col(0),
                  pl.BlockSpec((None, nb, 2, tq), lambda p: (p, 0, 0, 0)),
                  pl.BlockSpec((None, s, 2), lambda p: (p, 0, 0))],
        out_specs=[col(0), col(0), col(0), sums, sums],
        out_shape=[SDS((s, ATTN_W), BF16), SDS((s, ATTN_W), BF16), SDS((s, ATTN_W), BF16),
                   SDS((N_PAIRS, nb, 8, tq), F32), SDS((N_PAIRS, nb, 8, tq), F32)],
        scratch_shapes=[pltpu.VMEM((nb, rows_t, tq), F32), pltpu.VMEM((nb, 8, tq), F32),
                        pltpu.VMEM((2 * nb, tk, LANES), BF16), pltpu.VMEM((2 * nb, tq, LANES), BF16),
                        pltpu.VMEM((s, LANES), BF16), pltpu.VMEM((nb, rows_t, 2 * tq), BF16),
                        pltpu.VMEM((nb, rows_t, 2 * tk), BF16), pltpu.VMEM((nb, LANES, 2 * tq), BF16),
                        pltpu.VMEM((8, tk, tq), F32)],
        compiler_params=_cparams("arbitrary"),
        name="attn_bwd",
    )(qkv, qkv, qkv, attn_o, dmixed, rowb, ck_col)


def _forget_bwd(dc_t, fl_t, b_rows):
    rows = fl_t.shape[0]
    nb = rows // N_HEADS

    def body(dc_ref, fl_ref, b_ref, dfl_ref, db_ref):
        dc = dc_ref[...]
        lower = _iota2((LANES, LANES), 0) >= _iota2((LANES, LANES), 1)
        ones = jnp.ones((LANES, LANES), F32)
        rr, cc, same = _head_block_masks(rows, nb)
        dlf = _dot_sel(dc, lower) + _sel_dot(same & (cc > rr), _dot_sel(dc, ones))
        dfl = dlf / (1.0 + jnp.exp(fl_ref[...] + b_ref[...]))
        dfl_ref[...] = dfl
        shift = nb.bit_length() - 1
        hsel = lax.shift_right_logical(_iota2((N_HEADS, rows), 1), shift) == _iota2((N_HEADS, rows), 0)
        db_ref[...] = _sel_dot(hsel, _dot_sel(dfl, ones))

    return pl.pallas_call(body, out_shape=[SDS(fl_t.shape, F32), SDS((N_HEADS, LANES), F32)],
                          compiler_params=_cparams(), name="forget_bwd")(dc_t, fl_t, b_rows)


def _in_bwd(dq, dk, dv, du, dfl, w_in_t, x, r1, g1, dx1, *, tm):
    s = x.shape[0]
    pieces = ((0, ATTN_W), (ATTN_W, 2 * ATTN_W), (2 * ATTN_W, QKV_W), (U_OFF, F_OFF), (F_OFF, IN_PAD))

    def body(dq_ref, dk_ref, dv_ref, du_ref, df_ref, w_ref, x_ref, r_ref, g_ref, d_ref, dx_ref, dg1_ref):
        @pl.when(pl.program_id(0) == 0)
        def _():
            dg1_ref[...] = jnp.zeros_like(dg1_ref)

        dh = None
        for ref, (c0, c1) in zip((dq_ref, dk_ref, dv_ref, du_ref, df_ref), pieces):
            t = jnp.dot(ref[...], w_ref[c0:c1, :], preferred_element_type=F32)
            dh = t if dh is None else dh + t
        dx, dg1 = _norm_bwd(dh, x_ref[...], r_ref[...], g_ref[...], d_ref[...])
        dx_ref[...] = dx
        dg1_ref[...] += dg1

    row = lambda w: pl.BlockSpec((tm, w), lambda i: (i, 0))
    full = lambda a, b: pl.BlockSpec((a, b), lambda i: (0, 0))
    return pl.pallas_call(
        body,
        grid=(s // tm,),
        in_specs=[row(ATTN_W), row(ATTN_W), row(ATTN_W), row(POOL_W), row(LANES), full(IN_PAD, D_MODEL),
                  row(D_MODEL), row(1), full(1, D_MODEL), row(D_MODEL)],
        out_specs=[row(D_MODEL), full(1, D_MODEL)],
        out_shape=[SDS((s, D_MODEL), F32), SDS((1, D_MODEL), F32)],
        compiler_params=_cparams("arbitrary"),
        name="in_bwd",
    )(dq, dk, dv, du, dfl, w_in_t, x, r1, g1, dx1)


def _tiles(s):
    big = min(512, s)
    return dict(row=big, attn=min(256, s // 2), ff_rows=min(256, s), tall=min(1024, s))


def _tie(a, token):
    return a + token[0:1, 0:1].astype(a.dtype)


def _local_step(x, tgt, p, weight, emit, started):
    s = x.shape[0]
    t = _tiles(s)
    tm, tq = t["row"], t["attn"]
    nb = s // LANES
    nqb = s // tq
    g1, g2, gf = p["norm1_g"], p["norm2_g"], p["final_g"].reshape(1, D_MODEL)
    w_pool, pool_scale = p["w_pool"][0], p["pool_scale"]

    h, r1 = _norm1(x, _tie(g1, started), tm=tm)
    w_in_t = weight("w_in", h)
    qkv, u, fl = _in_proj(h, w_in_t, tm=t["tall"])
    fl_t = fl[:, :N_HEADS].T.reshape(N_HEADS * nb, LANES)
    b_rows = jnp.repeat(p["b_forget"].reshape(N_HEADS), nb).reshape(N_HEADS * nb, 1)
    c = _forget_cumsum(fl_t, b_rows).reshape(N_PAIRS, 2, s)
    c_col = c.transpose(0, 2, 1)
    c_rowblk = c.reshape(N_PAIRS, 2, nqb, tq).transpose(0, 2, 1, 3)
    attn_o, lse = _attn_fwd(qkv, c_col, tk=tq)
    lse = lse.reshape(N_PAIRS, nqb // 2, 2, 2, tq).transpose(0, 1, 3, 2, 4).reshape(N_PAIRS, nqb, 2, tq)
    pooled, pool_o = _pool_fwd(u, w_pool, pool_scale, tm=tm)
    w_out = weight("w_out", attn_o)
    x1, h2, r2 = _out_norm2(attn_o, pool_o, w_out, x, g2, tm=tm)
    wg_t, wu_t = weight("w_gate_up", h2)
    gate, up, act = _gate_up(h2, wg_t, wu_t, tm=t["ff_rows"], tn=D_FF)
    wd = weight("w_down", act)
    dx2, loss_row, d_gf = _down_final(act, wd, x1, gf, tgt, tm=tm)

    dgate, dup = _swiglu_bwd(dx2, wd, gate, up, tm=t["ff_rows"], tn=D_FF)
    d_wd = _mm_tn_stacked([act], [D_FF], dx2, ts=t["tall"], name="grad_w_down")
    d_wg_t = _mm_tn_stacked([dgate], [D_FF], h2, ts=t["tall"], name="grad_w_gate")
    d_wu_t = _mm_tn_stacked([dup], [D_FF], h2, ts=t["tall"], name="grad_w_up")
    token = emit("ff", (d_wd, d_wg_t, d_wu_t))
    dx1, dmixed, d_g2 = _mlp_in_bwd(dgate, dup, wg_t, wu_t, w_out, x1, r2, _tie(g2, token), dx2, tm=t["ff_rows"])
    du, d_wpool, d_pscale = _pool_bwd(dmixed, pooled, w_pool, pool_scale, tm=tm)
    token = emit("w_out", _mm_tn_stacked([attn_o, pool_o], [ATTN_W, POOL_W], dx1, ts=t["tall"], name="grad_w_out"))
    rowb = _tie(c_rowblk - lse, token)
    dq, dk, dv, dck, dcq = _attn_bwd(qkv, attn_o, dmixed, rowb, c_col, tq=tq)
    dc_t = (dcq - dck)[:, :, 0::4, :].transpose(0, 2, 1, 3).reshape(N_HEADS * nb, LANES)
    dfl_t, db = _forget_bwd(dc_t, fl_t, b_rows)
    dfl = jnp.pad(dfl_t.reshape(N_HEADS, s).T, ((0, 0), (0, LANES - N_HEADS))).astype(BF16)
    d_w_in_t = _mm_tn_stacked([dq, dk, dv, dfl, du], [ATTN_W, ATTN_W, ATTN_W, N_HEADS, POOL_W], h, ts=t["tall"],
                              name="grad_w_in")
    token = emit("w_in", d_w_in_t)
    dx, d_g1 = _in_bwd(dq, dk, dv, du, dfl, w_in_t, x, r1, _tie(g1, token), dx1, tm=tm)

    small = dict(norm1_g=d_g1, b_forget=db[:, 0].reshape(1, N_HEADS), w_pool=d_wpool, pool_scale=d_pscale,
                 norm2_g=d_g2, final_g=d_gf)
    return loss_row, dx, small


def _my_index():
    return 4 * lax.axis_index("x") + 2 * lax.axis_index("y") + lax.axis_index("c")


def _peer(k):
    pos = [lax.axis_index(a) for a in ("x", "y", "c")]
    flipped = tuple(1 - p if (k >> b) & 1 else p for p, b in zip(pos, (2, 1, 0)))
    return flipped, 4 * flipped[0] + 2 * flipped[1] + flipped[2]


_HBM = pl.BlockSpec(memory_space=pltpu.HBM)
_SEM = pl.BlockSpec(memory_space=pltpu.SEMAPHORE)
_DATAFLOW = pltpu.SideEffectType.DATAFLOW_SIDE_EFFECTING


def _peer_copies(ins, lands, send_sems, recv_sems, scatter, arrivals):
    me = _my_index()
    copies = []
    for w in range(len(ins)):
        for k in range(1, N_DEV):
            dev, idx = _peer(k)
            copies.append(pltpu.make_async_remote_copy(
                src_ref=ins[w].at[idx] if scatter[w] else ins[w], dst_ref=lands[w].at[idx if arrivals else me],
                send_sem=send_sems[w].at[k - 1], recv_sem=recv_sems[w].at[k - 1], device_id=dev, device_id_type=MESH))
    return copies


def _own_copies(ins, lands, send_sems, scatter):
    me = _my_index()
    return [pltpu.make_async_copy(ins[w].at[me] if scatter[w] else ins[w], lands[w].at[me], send_sems[w].at[N_DEV - 1])
            for w in range(len(ins))]


def _exchange_start(arrays, scatter, name):
    n = len(arrays)
    land_shapes = [(N_DEV,) + tuple(a.shape[1:] if sc else a.shape) for a, sc in zip(arrays, scatter)]

    def body(*refs):
        ins, lands = refs[:n], refs[n:2 * n]
        send_sems, recv_sems = refs[2 * n:3 * n], refs[3 * n:4 * n]
        token = refs[6 * n]
        for cp in _peer_copies(ins, lands, send_sems, recv_sems, scatter, False):
            cp.start()
        for cp in _own_copies(ins, lands, send_sems, scatter):
            cp.start()
        token[...] = jnp.zeros_like(token)

    sends, recvs = pltpu.SemaphoreType.DMA((N_DEV,)), pltpu.SemaphoreType.DMA((N_DEV - 1,))
    outs = pl.pallas_call(
        body,
        in_specs=[_HBM] * (2 * n),
        out_specs=[_SEM] * (2 * n) + [_HBM] * (2 * n) + [pl.BlockSpec(memory_space=pltpu.VMEM)],
        out_shape=[sends] * n + [recvs] * n + [pltpu.HBM(a.shape, a.dtype) for a in arrays]
        + [pltpu.HBM(sh, a.dtype) for sh, a in zip(land_shapes, arrays)] + [SDS((8, LANES), F32)],
        input_output_aliases={i: 2 * n + i for i in range(2 * n)},
        compiler_params=pltpu.CompilerParams(has_side_effects=_DATAFLOW),
        name=name,
    )(*[pltpu.with_memory_space_constraint(a, pltpu.HBM) for a in arrays],
      *[pltpu.with_memory_space_constraint(lax.empty(sh, a.dtype), pltpu.HBM) for sh, a in zip(land_shapes, arrays)])
    handles = [dict(send=outs[w], recv=outs[n + w], src=outs[2 * n + w], land=outs[3 * n + w], scatter=scatter[w])
               for w in range(n)]
    return handles, outs[4 * n]


def _exchange_wait(handles, after, name):
    n = len(handles)
    scatter = [h["scatter"] for h in handles]

    def body(*refs):
        ins, lands = refs[:n], refs[n:2 * n]
        send_sems, recv_sems = refs[2 * n:3 * n], refs[3 * n:4 * n]
        for cp in _peer_copies(ins, lands, send_sems, recv_sems, scatter, False):
            cp.wait_send()
        for cp in _peer_copies(ins, lands, send_sems, recv_sems, scatter, True):
            cp.wait_recv()
        for cp in _own_copies(ins, lands, send_sems, scatter):
            cp.wait()

    srcs, lands = [h["src"] for h in handles], [h["land"] for h in handles]
    outs = pl.pallas_call(
        body,
        in_specs=[_HBM] * (2 * n) + [_SEM] * (2 * n) + [pl.BlockSpec(memory_space=pl.ANY)],
        out_specs=[_HBM] * (2 * n),
        out_shape=[pltpu.HBM(a.shape, a.dtype) for a in srcs + lands],
        input_output_aliases={i: i for i in range(2 * n)},
        compiler_params=pltpu.CompilerParams(has_side_effects=_DATAFLOW),
        name=name,
    )(*srcs, *lands, *[h["send"] for h in handles], *[h["recv"] for h in handles], after)
    return outs[n:]


def _adamw(parts, w, m, v, name):
    rows, cols = w.shape
    tr = rows // 4 if rows % 32 == 0 else rows

    def body(p_ref, w_ref, m_ref, v_ref, g_ref, d_ref, mo_ref, vo_ref):
        g = p_ref[0].astype(F32)
        for d in range(1, N_DEV):
            g = g + p_ref[d].astype(F32)
        g_ref[...] = g
        d_ref[...], mo_ref[...], vo_ref[...] = _adam_update(g, w_ref[...], m_ref[...], v_ref[...])

    blk = pl.BlockSpec((tr, cols), lambda i: (i, 0))
    return pl.pallas_call(
        body,
        grid=(rows // tr,),
        in_specs=[pl.BlockSpec((N_DEV, tr, cols), lambda i: (0, i, 0)), blk, blk, blk],
        out_specs=[blk] * 4,
        out_shape=[SDS((rows, cols), F32)] * 4,
        compiler_params=_cparams("arbitrary"),
        name=name,
    )(parts, w, m, v)


_ROW_OF = dict(norm1_g=(0, D_MODEL), norm2_g=(1, D_MODEL), final_g=(2, D_MODEL), pool_scale=(3, POOL_W),
               b_forget=(4, N_HEADS), loss=(5, 1))


def _pack_rows(vals):
    rows = [jnp.pad(vals[n].reshape(1, width).astype(F32), ((0, 0), (0, D_MODEL - width)))
            for n, (_, width) in sorted(_ROW_OF.items(), key=lambda kv: kv[1][0])]
    return jnp.concatenate(rows + [jnp.zeros((8 - len(rows), D_MODEL), F32)], axis=0)


def _adam_update(g, w, m, v):
    m_new = ADAM_B1 * m + (1.0 - ADAM_B1) * g
    v_new = ADAM_B2 * v + (1.0 - ADAM_B2) * (g * g)
    m_hat = m_new / (1.0 - ADAM_B1 ** ADAM_STEP)
    v_hat = v_new / (1.0 - ADAM_B2 ** ADAM_STEP)
    return -ADAM_LR * (m_hat / (jnp.sqrt(v_hat) + ADAM_EPS) + ADAM_WD * w), m_new, v_new


def _adamw_replicated(parts_rows, parts_pool, w, m, v):
    names = ("norm1_g", "norm2_g", "final_g", "pool_scale", "b_forget", "w_pool")
    shapes = {n: ((len(POOL_WINDOWS), POOL_G, POOL_G) if n == "w_pool" else (1, _ROW_OF[n][1])) for n in names}

    def body(rows_ref, pool_ref, *refs):
        ins, outs = refs[:3 * len(names)], refs[3 * len(names):]

        def total(n):
            if n == "w_pool":
                pieces = [pool_ref[d] for d in range(N_DEV)]
            else:
                row, width = _ROW_OF[n]
                pieces = [rows_ref[d, row:row + 1, 0:width] for d in range(N_DEV)]
            g = pieces[0]
            for p in pieces[1:]:
                g = g + p
            return g

        outs[0][...] = total("loss")
        for k, n in enumerate(names):
            g = total(n)
            delta, m_new, v_new = _adam_update(g, ins[3 * k][...], ins[3 * k + 1][...], ins[3 * k + 2][...])
            for o_ref, val in zip(outs[1 + 4 * k:5 + 4 * k], (g, delta, m_new, v_new)):
                o_ref[...] = val

    args = [d[n].reshape(shapes[n]) for n in names for d in (w, m, v)]
    res = pl.pallas_call(
        body,
        out_shape=[SDS((1, 1), F32)] + [SDS(shapes[n], F32) for n in names for _ in range(4)],
        compiler_params=_cparams(),
        name="adamw_replicated",
    )(parts_rows, parts_pool, *args)
    return res[0], {n: [r.reshape(w[n].shape) for r in res[1 + 4 * k:5 + 4 * k]] for k, n in enumerate(names)}


def kernel(x, norm1_g, w_in, b_forget, w_pool, pool_scale, w_out, norm2_g, w_gate, w_up, w_down, final_g, loss_target, m_norm1_g, m_w_in, m_b_forget, m_w_pool, m_pool_scale, m_w_out, m_norm2_g, m_w_gate, m_w_up, m_w_down, m_final_g, v_norm1_g, v_w_in, v_b_forget, v_w_pool, v_pool_scale, v_w_out, v_norm2_g, v_w_gate, v_w_up, v_w_down, v_final_g):
    big = ("w_in", "w_out", "w_gate", "w_up", "w_down")
    order = ("norm1_g", "w_in", "b_forget", "w_pool", "pool_scale", "w_out", "norm2_g", "w_gate", "w_up", "w_down",
             "final_g")
    w = dict(norm1_g=norm1_g, w_in=w_in, b_forget=b_forget, w_pool=w_pool, pool_scale=pool_scale, w_out=w_out,
             norm2_g=norm2_g, w_gate=w_gate, w_up=w_up, w_down=w_down, final_g=final_g)
    m = dict(norm1_g=m_norm1_g, w_in=m_w_in, b_forget=m_b_forget, w_pool=m_w_pool, pool_scale=m_pool_scale,
             w_out=m_w_out, norm2_g=m_norm2_g, w_gate=m_w_gate, w_up=m_w_up, w_down=m_w_down, final_g=m_final_g)
    v = dict(norm1_g=v_norm1_g, w_in=v_w_in, b_forget=v_b_forget, w_pool=v_w_pool, pool_scale=v_pool_scale,
             w_out=v_w_out, norm2_g=v_norm2_g, w_gate=v_w_gate, w_up=v_w_up, w_down=v_w_down, final_g=v_final_g)

    flipped = ("w_in", "w_gate", "w_up")
    shard = lambda d, n: d[n][0].T if n in flipped else d[n][0]
    gather, started = _exchange_start([shard(w, n).astype(BF16) for n in big], [False] * len(big), "gather_start")
    gather = dict(zip(big, gather))

    def gathered(names, after):
        return _exchange_wait([gather[n] for n in names], after, "gather_wait_" + names[0])

    def weight(name, after):
        if name == "w_in":
            full = gathered(["w_in"], after)[0].reshape(IN_W, D_MODEL)
            f0 = QKV_W + N_HEADS
            return jnp.concatenate([full[:QKV_W], full[f0:], full[QKV_W:f0],
                                    jnp.zeros((IN_PAD - IN_W, D_MODEL), BF16)], axis=0)
        if name == "w_out":
            return gathered(["w_out"], after)[0].reshape(D_MODEL, D_MODEL)
        if name == "w_gate_up":
            return [g.reshape(D_FF, D_MODEL) for g in gathered(["w_gate", "w_up"], after)]
        return gathered(["w_down"], after)[0].reshape(D_FF, D_MODEL)

    rows = lambda g: g.reshape(N_DEV, g.shape[0] // N_DEV, g.shape[1])
    sent = {}

    def emit(name, grad):
        if name == "ff":
            names, slots = ["w_down", "w_gate", "w_up"], [rows(g) for g in grad]
        else:
            names, slots = [name], [rows(grad)]
        handles, token = _exchange_start(slots, [True] * len(slots), "grads_start_" + name)
        sent.update(zip(names, handles))
        return token

    loss_row, dx, small_grads = _local_step(x[0], loss_target[0], w, weight, emit, started)

    packed = _pack_rows(dict(small_grads, loss=0.5 / D_MODEL * jnp.sum(loss_row)))
    small_handles, after = _exchange_start([packed, small_grads["w_pool"]], [False, False], "grads_start_replicated")

    outs = {}
    for name in ("w_down", "w_gate", "w_up", "w_out", "w_in"):
        (parts,) = _exchange_wait([sent[name]], after, "grads_wait_" + name)
        outs[name] = _adamw(parts, shard(w, name), shard(m, name), shard(v, name), "adamw_" + name)
        after = outs[name][0]
        outs[name] = [(a.T if name in flipped else a)[None] for a in outs[name]]
    parts_rows, parts_pool = _exchange_wait(small_handles, after, "grads_wait_replicated")
    loss, small = _adamw_replicated(parts_rows, parts_pool, w, m, v)
    outs.update(small)

    return (loss.reshape(()), dx[None]) + tuple(outs[n][k] for k in range(4) for n in order)
```

```python
import functools

import jax
import jax.numpy as jnp
from jax import lax
from jax.experimental import pallas as pl
from jax.experimental.pallas import tpu as pltpu

F32 = jnp.float32
BF16 = jnp.bfloat16
SDS = jax.ShapeDtypeStruct

D_MODEL = 1024
ATTN_W = 512
N_HEADS = 8
HEAD_DIM = 64
N_PAIRS = N_HEADS // 2
POOL_W = 512
POOL_WINDOWS = (2, 4, 8, 16)
POOL_G = 128
HALO = 16
IN_W = 3 * ATTN_W + N_HEADS + POOL_W
QKV_W = 3 * ATTN_W
U_OFF = QKV_W
F_OFF = QKV_W + POOL_W
IN_PAD = F_OFF + 128
D_FF = 2816
EPS = 1e-6
NEG = -1e30
N_DEV = 8
LANES = 128

ADAM_LR = 0.001
ADAM_B1 = 0.9
ADAM_B2 = 0.999
ADAM_EPS = 1e-08
ADAM_WD = 0.01
ADAM_STEP = 10

VMEM_LIMIT_BYTES = 56 * 1024 * 1024
MESH = pl.DeviceIdType.MESH
NT = (((1,), (1,)), ((), ()))
TN = (((0,), (0,)), ((), ()))


def _cparams(*sem):
    return pltpu.CompilerParams(dimension_semantics=sem or None, vmem_limit_bytes=VMEM_LIMIT_BYTES)


def _split3(a):
    hi = a.astype(BF16)
    r1 = a - hi.astype(F32)
    mid = r1.astype(BF16)
    lo = (r1 - mid.astype(F32)).astype(BF16)
    return hi, mid, lo


def _dot_sel(a, sel, dims=None):
    sb = sel.astype(BF16)
    if dims is None:
        return sum(jnp.dot(p, sb, preferred_element_type=F32) for p in _split3(a))
    return sum(lax.dot_general(p, sb, dims, preferred_element_type=F32) for p in _split3(a))


def _sel_dot(sel, a, dims=None):
    sb = sel.astype(BF16)
    if dims is None:
        return sum(jnp.dot(sb, p, preferred_element_type=F32) for p in _split3(a))
    return sum(lax.dot_general(sb, p, dims, preferred_element_type=F32) for p in _split3(a))


def _iota2(shape, dim):
    return lax.broadcasted_iota(jnp.int32, shape, dim)


def _norm1(x, g1, *, tm):
    s = x.shape[0]

    def body(x_ref, g_ref, h_ref, r_ref):
        xv = x_ref[...]
        r = lax.rsqrt(jnp.mean(xv * xv, axis=-1, keepdims=True) + EPS)
        h_ref[...] = (xv * r * g_ref[...]).astype(BF16)
        r_ref[...] = r

    row = lambda w: pl.BlockSpec((tm, w), lambda i: (i, 0))
    return pl.pallas_call(
        body,
        grid=(s // tm,),
        in_specs=[row(D_MODEL), pl.BlockSpec((1, D_MODEL), lambda i: (0, 0))],
        out_specs=[row(D_MODEL), row(1)],
        out_shape=[SDS((s, D_MODEL), BF16), SDS((s, 1), F32)],
        compiler_params=_cparams("arbitrary"),
        name="norm1",
    )(x, g1)


def _in_proj(h, w_in_t, *, tm):
    s = h.shape[0]

    def body(h_ref, w_ref, qkv_ref, u_ref, fl_ref):
        h = h_ref[...]
        qkv_ref[...] = lax.dot_general(h, w_ref[0:QKV_W, :], NT, preferred_element_type=F32).astype(BF16)
        u_ref[...] = lax.dot_general(h, w_ref[U_OFF:F_OFF, :], NT, preferred_element_type=F32)
        fl_ref[...] = lax.dot_general(h, w_ref[F_OFF:IN_PAD, :], NT, preferred_element_type=F32)

    row = lambda w: pl.BlockSpec((tm, w), lambda i: (i, 0))
    return pl.pallas_call(
        body,
        grid=(s // tm,),
        in_specs=[row(D_MODEL), pl.BlockSpec((IN_PAD, D_MODEL), lambda i: (0, 0))],
        out_specs=[row(QKV_W), row(POOL_W), row(LANES)],
        out_shape=[SDS((s, QKV_W), BF16), SDS((s, POOL_W), F32), SDS((s, LANES), F32)],
        compiler_params=_cparams("arbitrary"),
        name="in_proj",
    )(h, w_in_t)


def _head_block_masks(rows, nb):
    shift = nb.bit_length() - 1
    rr, cc = _iota2((rows, rows), 0), _iota2((rows, rows), 1)
    same = lax.shift_right_logical(rr, shift) == lax.shift_right_logical(cc, shift)
    return rr, cc, same


def _forget_cumsum(fl_t, b_rows):
    rows = fl_t.shape[0]
    nb = rows // N_HEADS

    def body(fl_ref, b_ref, c_ref):
        z = fl_ref[...] + b_ref[...]
        lf = jnp.minimum(z, 0.0) - jnp.log1p(jnp.exp(-jnp.abs(z)))
        upper = _iota2((LANES, LANES), 0) <= _iota2((LANES, LANES), 1)
        within = _dot_sel(lf, upper)
        tot = _dot_sel(lf, jnp.ones((LANES, LANES), F32))
        rr, cc, same = _head_block_masks(rows, nb)
        c_ref[...] = within + _sel_dot(same & (cc < rr), tot)

    return pl.pallas_call(body, out_shape=SDS(fl_t.shape, F32), compiler_params=_cparams(), name="forget_cumsum")(
        fl_t, b_rows)


BIAS_LANES = 3


def _augment(t, h, col, col_first):
    n = t.shape[0]
    lane = _iota2((n, LANES), 1)
    own = (lane < HEAD_DIM) if h == 0 else (lane >= HEAD_DIM)
    b0 = HEAD_DIM if h == 0 else 0
    c0, o0 = (b0, b0 + BIAS_LANES) if col_first else (b0 + BIAS_LANES, b0)
    x = jnp.where(own, t, 0.0)
    for off, piece in enumerate(_split3(col)):
        x = jnp.where(lane == c0 + off, piece.astype(F32), x)
    x = jnp.where((lane >= o0) & (lane < o0 + BIAS_LANES), 1.0, x)
    return x.astype(BF16)


def _attn_fwd(qkv, c_col, *, tk):
    s = qkv.shape[0]
    tq = 2 * tk
    nb = s // tk

    def body(q_ref, k_ref, v_ref, cq_ref, ck_ref, o_ref, lse_ref, kp_ref, vt_ref, st_ref):
        i = pl.program_id(1)

        @pl.when(i == 0)
        def _():
            def prep(jb, _):
                st = pl.multiple_of(jb * tk, tk)
                k2 = k_ref[pl.ds(st, tk), :].astype(F32)
                ck = ck_ref[pl.ds(st, tk), :]
                for h in range(2):
                    kp_ref[h * nb + jb] = _augment(k2, h, -ck[:, h:h + 1], True)
                vt_ref[jb] = v_ref[pl.ds(st, tk), :].astype(F32).T.astype(BF16)
                return 0

            lax.fori_loop(0, nb, prep, 0)

        qs = q_ref[...].astype(F32) * 0.125
        cq = cq_ref[...]
        qp = [_augment(qs, h, cq[:, h:h + 1], False) for h in range(2)]

        def logits(j):
            return tuple(lax.dot_general(kp_ref[h * nb + j], qp[h], NT, preferred_element_type=F32) for h in range(2))

        def softmax_pv(j, slot, stats, masked):
            out = []
            for h in range(2):
                m, l, acc = stats[h]
                st = st_ref[2 * slot + h]
                if masked:
                    st = jnp.where(j * tk + _iota2((tk, tq), 0) <= i * tq + _iota2((tk, tq), 1), st, NEG)
                m_new = jnp.maximum(m, jnp.max(st, axis=0, keepdims=True))
                alpha = jnp.exp(m - m_new)
                p = jnp.exp(st - m_new)
                l = alpha * l + jnp.sum(p, axis=0, keepdims=True)
                vt = vt_ref[j, h * HEAD_DIM:(h + 1) * HEAD_DIM, :]
                acc = alpha * acc + jnp.dot(vt, p.astype(BF16), preferred_element_type=F32)
                out.append((m_new, l, acc))
            return tuple(out)

        def put(slot, j):
            for h, st in enumerate(logits(j)):
                st_ref[2 * slot + h] = st

        def pair(t, stats):
            put(1, 2 * t + 1)
            stats = softmax_pv(2 * t, 0, stats, False)
            put(0, 2 * t + 2)
            return softmax_pv(2 * t + 1, 1, stats, False)

        init = tuple((jnp.full((1, tq), NEG, F32), jnp.zeros((1, tq), F32), jnp.zeros((HEAD_DIM, tq), F32))
                     for _ in range(2))
        put(0, 0)
        stats = lax.fori_loop(0, i, pair, init)
        put(1, 2 * i + 1)
        stats = softmax_pv(2 * i, 0, stats, True)
        (ma, la, acca), (mb, lb, accb) = softmax_pv(2 * i + 1, 1, stats, True)
        o_ref[...] = jnp.concatenate([acca / la, accb / lb], axis=0).T.astype(BF16)
        lse_ref[...] = jnp.where(_iota2((2, tq), 0) == 0, ma + jnp.log(la), mb + jnp.log(lb))

    return pl.pallas_call(
        body,
        grid=(N_PAIRS, s // tq),
        in_specs=[
            pl.BlockSpec((tq, LANES), lambda p, i: (i, p)),
            pl.BlockSpec((s, LANES), lambda p, i: (0, N_PAIRS + p)),
            pl.BlockSpec((s, LANES), lambda p, i: (0, 2 * N_PAIRS + p)),
            pl.BlockSpec((None, tq, 2), lambda p, i: (p, i, 0)),
            pl.BlockSpec((None, s, 2), lambda p, i: (p, 0, 0)),
        ],
        out_specs=[
            pl.BlockSpec((tq, LANES), lambda p, i: (i, p)),
            pl.BlockSpec((None, None, 2, tq), lambda p, i: (p, i, 0, 0)),
        ],
        out_shape=[SDS((s, ATTN_W), BF16), SDS((N_PAIRS, s // tq, 2, tq), F32)],
        scratch_shapes=[pltpu.VMEM((2 * nb, tk, LANES), BF16), pltpu.VMEM((nb, LANES, tk), BF16),
                        pltpu.VMEM((4, tk, tq), F32)],
        compiler_params=_cparams("arbitrary", "arbitrary"),
        name="attn_fwd",
    )(qkv, qkv, qkv, c_col, c_col)


def _pool_counts(row0, tm, w):
    t = row0 + _iota2((tm, 1), 0)
    return jnp.minimum(t + 1, w).astype(F32)


def _pool_fwd(u, w_pool, pool_scale, *, tm):
    s = u.shape[0]

    def body(u_ref, w_ref, sc_ref, pooled_ref, po_ref, tail_ref):
        i = pl.program_id(0)

        @pl.when(i == 0)
        def _():
            tail_ref[...] = jnp.zeros_like(tail_ref)

        uv = u_ref[...]
        ext = jnp.concatenate([tail_ref[...], uv], axis=0)
        tail_ref[...] = uv[tm - HALO:, :]
        for g, w in enumerate(POOL_WINDOWS):
            cols = slice(g * POOL_G, (g + 1) * POOL_G)
            acc = ext[:, cols]
            k = 1
            while k < w:
                acc = acc + pltpu.roll(acc, k, axis=0)
                k *= 2
            pooled = (acc[HALO:, :] / _pool_counts(i * tm, tm, w) - uv[:, cols]).astype(BF16)
            pooled_ref[:, cols] = pooled
            mixed = jnp.dot(pooled, w_ref[g].astype(BF16), preferred_element_type=F32)
            po_ref[:, cols] = (mixed * sc_ref[:, cols]).astype(BF16)

    row = pl.BlockSpec((tm, POOL_W), lambda i: (i, 0))
    return pl.pallas_call(
        body,
        grid=(s // tm,),
        in_specs=[row, pl.BlockSpec((len(POOL_WINDOWS), POOL_G, POOL_G), lambda i: (0, 0, 0)),
                  pl.BlockSpec((1, POOL_W), lambda i: (0, 0))],
        out_specs=[row, row],
        out_shape=[SDS((s, POOL_W), BF16), SDS((s, POOL_W), BF16)],
        scratch_shapes=[pltpu.VMEM((HALO, POOL_W), F32)],
        compiler_params=_cparams("arbitrary"),
        name="pool_fwd",
    )(u, w_pool, pool_scale)


def _out_norm2(attn_o, pool_o, w_out, x, g2, *, tm):
    s = x.shape[0]

    def body(a_ref, p_ref, w_ref, x_ref, g_ref, x1_ref, h2_ref, r_ref):
        x1 = (x_ref[...] + jnp.dot(a_ref[...], w_ref[0:ATTN_W, :], preferred_element_type=F32)
              + jnp.dot(p_ref[...], w_ref[ATTN_W:, :], preferred_element_type=F32))
        r = lax.rsqrt(jnp.mean(x1 * x1, axis=-1, keepdims=True) + EPS)
        x1_ref[...] = x1
        r_ref[...] = r
        h2_ref[...] = (x1 * r * g_ref[...]).astype(BF16)

    row = lambda w: pl.BlockSpec((tm, w), lambda i: (i, 0))
    full = lambda a, b: pl.BlockSpec((a, b), lambda i: (0, 0))
    return pl.pallas_call(
        body,
        grid=(s // tm,),
        in_specs=[row(ATTN_W), row(POOL_W), full(D_MODEL, D_MODEL), row(D_MODEL), full(1, D_MODEL)],
        out_specs=[row(D_MODEL), row(D_MODEL), row(1)],
        out_shape=[SDS((s, D_MODEL), F32), SDS((s, D_MODEL), BF16), SDS((s, 1), F32)],
        compiler_params=_cparams("arbitrary"),
        name="out_norm2",
    )(attn_o, pool_o, w_out, x, g2)


def _gate_up(h2, wg_t, wu_t, *, tm, tn):
    s = h2.shape[0]

    def body(h_ref, wg_ref, wu_ref, gate_ref, up_ref, act_ref):
        h = h_ref[...]
        gate = lax.dot_general(h, wg_ref[...], NT, preferred_element_type=F32)
        up = lax.dot_general(h, wu_ref[...], NT, preferred_element_type=F32)
        gate_ref[...] = gate.astype(BF16)
        up_ref[...] = up.astype(BF16)
        act_ref[...] = (gate * jax.nn.sigmoid(gate) * up).astype(BF16)

    wspec = pl.BlockSpec((tn, D_MODEL), lambda c, r: (c, 0))
    ospec = pl.BlockSpec((tm, tn), lambda c, r: (r, c))
    return pl.pallas_call(
        body,
        grid=(D_FF // tn, s // tm),
        in_specs=[pl.BlockSpec((tm, D_MODEL), lambda c, r: (r, 0)), wspec, wspec],
        out_specs=[ospec, ospec, ospec],
        out_shape=[SDS((s, D_FF), BF16), SDS((s, D_FF), BF16), SDS((s, D_FF), BF16)],
        compiler_params=_cparams("arbitrary", "arbitrary"),
        name="gate_up",
    )(h2, wg_t, wu_t)


def _down_final(act, wd, x1, gf, tgt, *, tm):
    s = x1.shape[0]

    def body(a_ref, w_ref, x1_ref, g_ref, t_ref, dx2_ref, loss_ref, dgf_ref):
        @pl.when(pl.program_id(0) == 0)
        def _():
            loss_ref[...] = jnp.zeros_like(loss_ref)
            dgf_ref[...] = jnp.zeros_like(dgf_ref)

        x2 = x1_ref[...] + jnp.dot(a_ref[...], w_ref[...], preferred_element_type=F32)
        r = lax.rsqrt(jnp.mean(x2 * x2, axis=-1, keepdims=True) + EPS)
        xn = x2 * r
        g = g_ref[...]
        diff = xn * g - t_ref[...]
        loss_ref[...] += jnp.sum(diff * diff, axis=0, keepdims=True)
        dy = diff * (1.0 / D_MODEL)
        dgf_ref[...] += jnp.sum(dy * xn, axis=0, keepdims=True)
        dxn = dy * g
        dx2_ref[...] = r * (dxn - xn * jnp.mean(dxn * xn, axis=-1, keepdims=True))

    row = lambda w: pl.BlockSpec((tm, w), lambda i: (i, 0))
    full = lambda a, b: pl.BlockSpec((a, b), lambda i: (0, 0))
    return pl.pallas_call(
        body,
        grid=(s // tm,),
        in_specs=[row(D_FF), full(D_FF, D_MODEL), row(D_MODEL), full(1, D_MODEL), row(D_MODEL)],
        out_specs=[row(D_MODEL), full(1, D_MODEL), full(1, D_MODEL)],
        out_shape=[SDS((s, D_MODEL), F32), SDS((1, D_MODEL), F32), SDS((1, D_MODEL), F32)],
        compiler_params=_cparams("arbitrary"),
        name="down_final",
    )(act, wd, x1, gf, tgt)


def _swiglu_bwd(dx2, wd, gate, up, *, tm, tn):
    s = dx2.shape[0]

    def body(d_ref, w_ref, gate_ref, up_ref, dgate_ref, dup_ref):
        dact = lax.dot_general(d_ref[...].astype(BF16), w_ref[...], NT, preferred_element_type=F32)
        gate = gate_ref[...].astype(F32)
        sg = jax.nn.sigmoid(gate)
        dup_ref[...] = (dact * (gate * sg)).astype(BF16)
        dgate_ref[...] = (dact * up_ref[...].astype(F32) * (sg * (1.0 + gate * (1.0 - sg)))).astype(BF16)

    ospec = pl.BlockSpec((tm, tn), lambda c, r: (r, c))
    return pl.pallas_call(
        body,
        grid=(D_FF // tn, s // tm),
        in_specs=[pl.BlockSpec((tm, D_MODEL), lambda c, r: (r, 0)), pl.BlockSpec((tn, D_MODEL), lambda c, r: (c, 0)),
                  ospec, ospec],
        out_specs=[ospec, ospec],
        out_shape=[SDS((s, D_FF), BF16), SDS((s, D_FF), BF16)],
        compiler_params=_cparams("arbitrary", "arbitrary"),
        name="swiglu_bwd",
    )(dx2, wd, gate, up)


def _mm_tn_stacked(as_, rows, b, *, ts, name):
    s, nb_ = b.shape
    n = len(as_)
    offsets = [sum(rows[:i]) for i in range(n)]

    def body(*refs):
        a_refs, b_ref, o_ref, acc_ref = refs[:n], refs[n], refs[n + 1], refs[n + 2]
        k = pl.program_id(0)

        @pl.when(k == 0)
        def _():
            acc_ref[...] = jnp.zeros_like(acc_ref)

        bv = b_ref[...].astype(BF16)
        for a_ref, off, cnt in zip(a_refs, offsets, rows):
            part = lax.dot_general(a_ref[...].astype(BF16), bv, TN, preferred_element_type=F32)
            acc_ref[off:off + cnt, :] += part[0:cnt, :]

        @pl.when(k == s // ts - 1)
        def _():
            o_ref[...] = acc_ref[...].astype(BF16)

    return pl.pallas_call(
        body,
        grid=(s // ts,),
        in_specs=[pl.BlockSpec((ts, a.shape[1]), lambda k: (k, 0)) for a in as_] + [pl.BlockSpec((ts, nb_), lambda k: (k, 0))],
        out_specs=pl.BlockSpec((sum(rows), nb_), lambda k: (0, 0)),
        out_shape=SDS((sum(rows), nb_), BF16),
        scratch_shapes=[pltpu.VMEM((sum(rows), nb_), F32)],
        compiler_params=_cparams("arbitrary"),
        name=name,
    )(*as_, b)


def _norm_bwd(dh, x, r, g, dres):
    xn = x * r
    dxn = dh * g
    dx = dres + r * (dxn - xn * jnp.mean(dxn * xn, axis=-1, keepdims=True))
    return dx, jnp.sum(dh * xn, axis=0, keepdims=True)


def _mlp_in_bwd(dgate, dup, wg_t, wu_t, w_out, x1, r2, g2, dx2, *, tm):
    s = x1.shape[0]

    def body(dg_ref, du_ref, wg_ref, wu_ref, wo_ref, x_ref, r_ref, g_ref, d_ref, dx1_ref, dmix_ref, dg2_ref):
        @pl.when(pl.program_id(0) == 0)
        def _():
            dg2_ref[...] = jnp.zeros_like(dg2_ref)

        dh2 = (jnp.dot(dg_ref[...], wg_ref[...], preferred_element_type=F32)
               + jnp.dot(du_ref[...], wu_ref[...], preferred_element_type=F32))
        dx1, dg2 = _norm_bwd(dh2, x_ref[...], r_ref[...], g_ref[...], d_ref[...])
        dg2_ref[...] += dg2
        dx1_ref[...] = dx1
        dmix_ref[...] = lax.dot_general(dx1.astype(BF16), wo_ref[...], NT, preferred_element_type=F32)

    row = lambda w: pl.BlockSpec((tm, w), lambda i: (i, 0))
    full = lambda a, b: pl.BlockSpec((a, b), lambda i: (0, 0))
    return pl.pallas_call(
        body,
        grid=(s // tm,),
        in_specs=[row(D_FF), row(D_FF), full(D_FF, D_MODEL), full(D_FF, D_MODEL), full(D_MODEL, D_MODEL),
                  row(D_MODEL), row(1), full(1, D_MODEL), row(D_MODEL)],
        out_specs=[row(D_MODEL), row(D_MODEL), full(1, D_MODEL)],
        out_shape=[SDS((s, D_MODEL), F32), SDS((s, D_MODEL), F32), SDS((1, D_MODEL), F32)],
        compiler_params=_cparams("arbitrary"),
        name="mlp_in_bwd",
    )(dgate, dup, wg_t, wu_t, w_out, x1, r2, g2, dx2)


def _pool_bwd(dmixed, pooled, w_pool, pool_scale, *, tm):
    s = pooled.shape[0]
    nt = s // tm
    ng = len(POOL_WINDOWS)

    def body(d_ref, p_ref, w_ref, sc_ref, du_ref, dw_ref, dsc_ref, head_ref):
        i = pl.program_id(0)

        @pl.when(i == 0)
        def _():
            head_ref[...] = jnp.zeros_like(head_ref)
            dw_ref[...] = jnp.zeros_like(dw_ref)
            dsc_ref[...] = jnp.zeros_like(dsc_ref)

        row0 = (nt - 1 - i) * tm
        for g, w in enumerate(POOL_WINDOWS):
            cols = slice(g * POOL_G, (g + 1) * POOL_G)
            wb = w_ref[g].astype(BF16)
            pooled_g = p_ref[:, cols]
            dpo = d_ref[:, cols]
            mixed = jnp.dot(pooled_g, wb, preferred_element_type=F32)
            dsc_ref[:, cols] += jnp.sum(dpo * mixed, axis=0, keepdims=True)
            dmp = (dpo * sc_ref[:, cols]).astype(BF16)
            dw_ref[g] += lax.dot_general(pooled_g, dmp, TN, preferred_element_type=F32)
            dpooled = lax.dot_general(dmp, wb, NT, preferred_element_type=F32)
            a = dpooled / _pool_counts(row0, tm, w)
            acc = jnp.concatenate([a, head_ref[:, cols]], axis=0)
            head_ref[:, cols] = a[0:HALO, :]
            k = 1
            while k < w:
                acc = acc + pltpu.roll(acc, tm + HALO - k, axis=0)
                k *= 2
            du_ref[:, cols] = (acc[0:tm, :] - dpooled).astype(BF16)

    rev = lambda i: (nt - 1 - i, 0)
    return pl.pallas_call(
        body,
        grid=(nt,),
        in_specs=[pl.BlockSpec((tm, POOL_W), lambda i: (nt - 1 - i, 1)), pl.BlockSpec((tm, POOL_W), rev),
                  pl.BlockSpec((ng, POOL_G, POOL_G), lambda i: (0, 0, 0)), pl.BlockSpec((1, POOL_W), lambda i: (0, 0))],
        out_specs=[pl.BlockSpec((tm, POOL_W), rev), pl.BlockSpec((ng, POOL_G, POOL_G), lambda i: (0, 0, 0)),
                   pl.BlockSpec((1, POOL_W), lambda i: (0, 0))],
        out_shape=[SDS((s, POOL_W), BF16), SDS((ng, POOL_G, POOL_G), F32), SDS((1, POOL_W), F32)],
        scratch_shapes=[pltpu.VMEM((HALO, POOL_W), F32)],
        compiler_params=_cparams("arbitrary"),
        name="pool_bwd",
    )(dmixed, pooled, w_pool, pool_scale)


SUM_ROWS = 16


def _heads_t(t):
    n = t.shape[0]
    lane = _iota2((n, LANES), 1)
    tf = t.astype(F32)
    halves = jnp.concatenate([jnp.where(lane < HEAD_DIM, tf, 0.0).T, jnp.where(lane < HEAD_DIM, 0.0, tf).T], axis=1)
    r, c = _iota2((SUM_ROWS, 2 * n), 0), _iota2((SUM_ROWS, 2 * n), 1)
    ones = jnp.where(((r == 0) & (c < n)) | ((r == 4) & (c >= n)), 1.0, 0.0)
    return jnp.concatenate([halves, ones], axis=0).astype(BF16)


def _attn_bwd(qkv, attn_o, dmixed, rowb, ck_col, *, tq):
    s = qkv.shape[0]
    tk = tq
    nb = s // tq
    rows_t = LANES + SUM_ROWS

    def body(q_ref, k_ref, v_ref, o_ref, do_ref, rowb_ref, ck_ref, dq_ref, dk_ref, dv_ref, dck_ref, dcq_ref,
             dqt_ref, delta_ref, kp_ref, qp_ref, dob_ref, qt_ref, kt_ref, dot_ref, front_ref):
        lane = _iota2((tq, LANES), 1)
        lo = lane < HEAD_DIM
        first = _iota2((8, LANES), 1) < HEAD_DIM
        sel = jnp.where(_iota2((8, LANES), 0) < 4, jnp.where(first, 1.0, 0.0), jnp.where(first, 0.0, 1.0))

        def prep(b, _):
            st = pl.multiple_of(b * tq, tq)
            do2 = do_ref[pl.ds(st, tq), :]
            delta_ref[b] = _sel_dot(sel, do2 * o_ref[pl.ds(st, tq), :].astype(F32), NT)
            dob_ref[pl.ds(st, tq), :] = do2.astype(BF16)
            dqt_ref[b] = jnp.zeros((rows_t, tq), F32)
            k2 = k_ref[pl.ds(st, tq), :].astype(F32)
            q2 = q_ref[pl.ds(st, tq), :].astype(F32)
            ck = ck_ref[pl.ds(st, tq), :]
            for h in range(2):
                kp_ref[h * nb + b] = _augment(k2, h, -ck[:, h:h + 1], True)
                qp_ref[h * nb + b] = _augment(q2 * 0.125, h, jnp.zeros((tq, 1), F32), False)
            qt_ref[b] = _heads_t(q2)
            kt_ref[b] = _heads_t(k2)
            dot_ref[b] = _heads_t(do2)[0:LANES, :]
            return 0

        lax.fori_loop(0, nb, prep, 0)

        def split(t):
            z = jnp.zeros_like(t)
            return jnp.where(lo, t, z), jnp.where(lo, z, t)

        def kv_block(j, _):
            st_j = pl.multiple_of(j * tk, tk)
            vs = split(v_ref[pl.ds(st_j, tk), :])
            kt = kt_ref[j]

            def stage(i, slot):
                ic = jnp.minimum(i, nb - 1)
                do2 = dob_ref[pl.ds(pl.multiple_of(ic * tq, tq), tq), :]
                for h in range(2):
                    front_ref[4 * slot + h] = lax.dot_general(kp_ref[h * nb + j], qp_ref[h * nb + ic], NT,
                                                              preferred_element_type=F32)
                    front_ref[4 * slot + 2 + h] = lax.dot_general(vs[h], do2, NT, preferred_element_type=F32)

            def q_block(i, slot, carry, diagonal):
                dkt, dvt = carry
                ic = jnp.minimum(i, nb - 1)
                rb = rowb_ref[ic] + jnp.where(i < nb, 0.0, NEG)
                dl = delta_ref[ic]
                pts, dsts = [], []
                for h in range(2):
                    st = front_ref[4 * slot + h] + rb[h:h + 1, :]
                    if diagonal:
                        st = jnp.where(_iota2((tk, tq), 0) <= _iota2((tk, tq), 1), st, NEG)
                    pt = jnp.exp(st)
                    pts.append(pt.astype(BF16))
                    dsts.append((pt * (front_ref[4 * slot + 2 + h] - dl[4 * h:4 * h + 1, :])).astype(BF16))
                dvt = dvt + lax.dot_general(dot_ref[ic], jnp.concatenate(pts, axis=1), NT, preferred_element_type=F32)
                dkt = dkt + lax.dot_general(qt_ref[ic], jnp.concatenate(dsts, axis=1), NT, preferred_element_type=F32)
                dqt_ref[ic] += jnp.dot(kt, jnp.concatenate(dsts, axis=0), preferred_element_type=F32)
                return dkt, dvt

            def pair(t, carry):
                i0 = j + 1 + 2 * t
                stage(i0 + 1, 0)
                carry = q_block(i0, 1, carry, False)
                stage(i0 + 2, 1)
                return q_block(i0 + 1, 0, carry, False)

            stage(j, 0)
            stage(j + 1, 1)
            carry = q_block(j, 0, (jnp.zeros((rows_t, tk), F32), jnp.zeros((LANES, tk), F32)), True)
            dkt, dvt = lax.fori_loop(0, lax.shift_right_logical(nb - j, 1), pair, carry)
            dk_ref[pl.ds(st_j, tk), :] = (dkt[0:LANES, :].T * 0.125).astype(BF16)
            dv_ref[pl.ds(st_j, tk), :] = dvt.T.astype(BF16)
            dck_ref[j] = dkt[LANES:LANES + 8, :]
            return 0

        lax.fori_loop(0, nb, kv_block, 0)

        def finish(b, _):
            acc = dqt_ref[b]
            dq_ref[pl.ds(pl.multiple_of(b * tq, tq), tq), :] = (acc[0:LANES, :].T * 0.125).astype(BF16)
            dcq_ref[b] = acc[LANES:LANES + 8, :]
            return 0

        lax.fori_loop(0, nb, finish, 0)

    col = lambda off: pl.BlockSpec((s, LANES), lambda p: (0, off + p))
    sums = pl.BlockSpec((None, nb, 8, tq), lambda p: (p, 0, 0, 0))
    return pl.pallas_call(
        body,
        grid=(N_PAIRS,),
        in_specs=[col(0), col(N_PAIRS), col(2 * N_PAIRS), col(0), col(0),
                  pl.BlockSpec((None, nb, 2, tq), lambda p: (p, 0, 0, 0)),
                  pl.BlockSpec((None, s, 2), lambda p: (p, 0, 0))],
        out_specs=[col(0), col(0), col(0), sums, sums],
        out_shape=[SDS((s, ATTN_W), BF16), SDS((s, ATTN_W), BF16), SDS((s, ATTN_W), BF16),
                   SDS((N_PAIRS, nb, 8, tq), F32), SDS((N_PAIRS, nb, 8, tq), F32)],
        scratch_shapes=[pltpu.VMEM((nb, rows_t, tq), F32), pltpu.VMEM((nb, 8, tq), F32),
                        pltpu.VMEM((2 * nb, tk, LANES), BF16), pltpu.VMEM((2 * nb, tq, LANES), BF16),
                        pltpu.VMEM((s, LANES), BF16), pltpu.VMEM((nb, rows_t, 2 * tq), BF16),
                        pltpu.VMEM((nb, rows_t, 2 * tk), BF16), pltpu.VMEM((nb, LANES, 2 * tq), BF16),
                        pltpu.VMEM((8, tk, tq), F32)],
        compiler_params=_cparams("arbitrary"),
        name="attn_bwd",
    )(qkv, qkv, qkv, attn_o, dmixed, rowb, ck_col)


def _forget_bwd(dc_t, fl_t, b_rows):
    rows = fl_t.shape[0]
    nb = rows // N_HEADS

    def body(dc_ref, fl_ref, b_ref, dfl_ref, db_ref):
        dc = dc_ref[...]
        lower = _iota2((LANES, LANES), 0) >= _iota2((LANES, LANES), 1)
        ones = jnp.ones((LANES, LANES), F32)
        rr, cc, same = _head_block_masks(rows, nb)
        dlf = _dot_sel(dc, lower) + _sel_dot(same & (cc > rr), _dot_sel(dc, ones))
        dfl = dlf / (1.0 + jnp.exp(fl_ref[...] + b_ref[...]))
        dfl_ref[...] = dfl
        shift = nb.bit_length() - 1
        hsel = lax.shift_right_logical(_iota2((N_HEADS, rows), 1), shift) == _iota2((N_HEADS, rows), 0)
        db_ref[...] = _sel_dot(hsel, _dot_sel(dfl, ones))

    return pl.pallas_call(body, out_shape=[SDS(fl_t.shape, F32), SDS((N_HEADS, LANES), F32)],
                          compiler_params=_cparams(), name="forget_bwd")(dc_t, fl_t, b_rows)


def _in_bwd(dq, dk, dv, du, dfl, w_in_t, x, r1, g1, dx1, *, tm):
    s = x.shape[0]
    pieces = ((0, ATTN_W), (ATTN_W, 2 * ATTN_W), (2 * ATTN_W, QKV_W), (U_OFF, F_OFF), (F_OFF, IN_PAD))

    def body(dq_ref, dk_ref, dv_ref, du_ref, df_ref, w_ref, x_ref, r_ref, g_ref, d_ref, dx_ref, dg1_ref):
        @pl.when(pl.program_id(0) == 0)
        def _():
            dg1_ref[...] = jnp.zeros_like(dg1_ref)

        dh = None
        for ref, (c0, c1) in zip((dq_ref, dk_ref, dv_ref, du_ref, df_ref), pieces):
            t = jnp.dot(ref[...], w_ref[c0:c1, :], preferred_element_type=F32)
            dh = t if dh is None else dh + t
        dx, dg1 = _norm_bwd(dh, x_ref[...], r_ref[...], g_ref[...], d_ref[...])
        dx_ref[...] = dx
        dg1_ref[...] += dg1

    row = lambda w: pl.BlockSpec((tm, w), lambda i: (i, 0))
    full = lambda a, b: pl.BlockSpec((a, b), lambda i: (0, 0))
    return pl.pallas_call(
        body,
        grid=(s // tm,),
        in_specs=[row(ATTN_W), row(ATTN_W), row(ATTN_W), row(POOL_W), row(LANES), full(IN_PAD, D_MODEL),
                  row(D_MODEL), row(1), full(1, D_MODEL), row(D_MODEL)],
        out_specs=[row(D_MODEL), full(1, D_MODEL)],
        out_shape=[SDS((s, D_MODEL), F32), SDS((1, D_MODEL), F32)],
        compiler_params=_cparams("arbitrary"),
        name="in_bwd",
    )(dq, dk, dv, du, dfl, w_in_t, x, r1, g1, dx1)


def _tiles(s):
    big = min(512, s)
    return dict(row=big, attn=min(256, s // 2), ff_rows=min(256, s), tall=min(1024, s))


def _tie(a, token):
    return a + token[0:1, 0:1].astype(a.dtype)


def _local_step(x, tgt, p, weight, emit, started):
    s = x.shape[0]
    t = _tiles(s)
    tm, tq = t["row"], t["attn"]
    nb = s // LANES
    nqb = s // tq
    g1, g2, gf = p["norm1_g"], p["norm2_g"], p["final_g"].reshape(1, D_MODEL)
    w_pool, pool_scale = p["w_pool"][0], p["pool_scale"]

    h, r1 = _norm1(x, _tie(g1, started), tm=tm)
    w_in_t = weight("w_in", h)
    qkv, u, fl = _in_proj(h, w_in_t, tm=t["tall"])
    fl_t = fl[:, :N_HEADS].T.reshape(N_HEADS * nb, LANES)
    b_rows = jnp.repeat(p["b_forget"].reshape(N_HEADS), nb).reshape(N_HEADS * nb, 1)
    c = _forget_cumsum(fl_t, b_rows).reshape(N_PAIRS, 2, s)
    c_col = c.transpose(0, 2, 1)
    c_rowblk = c.reshape(N_PAIRS, 2, nqb, tq).transpose(0, 2, 1, 3)
    attn_o, lse = _attn_fwd(qkv, c_col, tk=tq)
    lse = lse.reshape(N_PAIRS, nqb // 2, 2, 2, tq).transpose(0, 1, 3, 2, 4).reshape(N_PAIRS, nqb, 2, tq)
    pooled, pool_o = _pool_fwd(u, w_pool, pool_scale, tm=tm)
    w_out = weight("w_out", attn_o)
    x1, h2, r2 = _out_norm2(attn_o, pool_o, w_out, x, g2, tm=tm)
    wg_t, wu_t = weight("w_gate_up", h2)
    gate, up, act = _gate_up(h2, wg_t, wu_t, tm=t["ff_rows"], tn=D_FF)
    wd = weight("w_down", act)
    dx2, loss_row, d_gf = _down_final(act, wd, x1, gf, tgt, tm=tm)

    dgate, dup = _swiglu_bwd(dx2, wd, gate, up, tm=t["ff_rows"], tn=D_FF)
    d_wd = _mm_tn_stacked([act], [D_FF], dx2, ts=t["tall"], name="grad_w_down")
    d_wg_t = _mm_tn_stacked([dgate], [D_FF], h2, ts=t["tall"], name="grad_w_gate")
    d_wu_t = _mm_tn_stacked([dup], [D_FF], h2, ts=t["tall"], name="grad_w_up")
    token = emit("ff", (d_wd, d_wg_t, d_wu_t))
    dx1, dmixed, d_g2 = _mlp_in_bwd(dgate, dup, wg_t, wu_t, w_out, x1, r2, _tie(g2, token), dx2, tm=t["ff_rows"])
    du, d_wpool, d_pscale = _pool_bwd(dmixed, pooled, w_pool, pool_scale, tm=tm)
    token = emit("w_out", _mm_tn_stacked([attn_o, pool_o], [ATTN_W, POOL_W], dx1, ts=t["tall"], name="grad_w_out"))
    rowb = _tie(c_rowblk - lse, token)
    dq, dk, dv, dck, dcq = _attn_bwd(qkv, attn_o, dmixed, rowb, c_col, tq=tq)
    dc_t = (dcq - dck)[:, :, 0::4, :].transpose(0, 2, 1, 3).reshape(N_HEADS * nb, LANES)
    dfl_t, db = _forget_bwd(dc_t, fl_t, b_rows)
    dfl = jnp.pad(dfl_t.reshape(N_HEADS, s).T, ((0, 0), (0, LANES - N_HEADS))).astype(BF16)
    d_w_in_t = _mm_tn_stacked([dq, dk, dv, dfl, du], [ATTN_W, ATTN_W, ATTN_W, N_HEADS, POOL_W], h, ts=t["tall"],
                              name="grad_w_in")
    token = emit("w_in", d_w_in_t)
    dx, d_g1 = _in_bwd(dq, dk, dv, du, dfl, w_in_t, x, r1, _tie(g1, token), dx1, tm=tm)

    small = dict(norm1_g=d_g1, b_forget=db[:, 0].reshape(1, N_HEADS), w_pool=d_wpool, pool_scale=d_pscale,
                 norm2_g=d_g2, final_g=d_gf)
    return loss_row, dx, small


def _my_index():
    return 4 * lax.axis_index("x") + 2 * lax.axis_index("y") + lax.axis_index("c")


def _peer(k):
    pos = [lax.axis_index(a) for a in ("x", "y", "c")]
    flipped = tuple(1 - p if (k >> b) & 1 else p for p, b in zip(pos, (2, 1, 0)))
    return flipped, 4 * flipped[0] + 2 * flipped[1] + flipped[2]


_HBM = pl.BlockSpec(memory_space=pltpu.HBM)
_SEM = pl.BlockSpec(memory_space=pltpu.SEMAPHORE)
_DATAFLOW = pltpu.SideEffectType.DATAFLOW_SIDE_EFFECTING


ALL_PEERS = tuple(range(1, N_DEV))
SAME_CORE = (1, 2, 4, 6)


def _peer_copies(ins, lands, send_sems, recv_sems, scatter, peers, arrivals):
    me = _my_index()
    copies = []
    for w in range(len(ins)):
        for k in peers[w]:
            dev, idx = _peer(k)
            copies.append(pltpu.make_async_remote_copy(
                src_ref=ins[w].at[idx] if scatter[w] else ins[w], dst_ref=lands[w].at[idx if arrivals else me],
                send_sem=send_sems[w].at[k - 1], recv_sem=recv_sems[w].at[k - 1], device_id=dev, device_id_type=MESH))
    return copies


def _own_copies(ins, lands, send_sems, scatter):
    me = _my_index()
    return [pltpu.make_async_copy(ins[w].at[me] if scatter[w] else ins[w], lands[w].at[me], send_sems[w].at[N_DEV - 1])
            for w in range(len(ins))]


def _forward_copies(land, send_sems, recv_sems, arrivals):
    sibling, _ = _peer(1)
    copies = []
    for j, k in enumerate(SAME_CORE[1:]):
        src, dst = _peer(k)[1], _peer(k ^ 1 if arrivals else k)[1]
        copies.append(pltpu.make_async_remote_copy(
            src_ref=land.at[src], dst_ref=land.at[dst], send_sem=send_sems.at[j], recv_sem=recv_sems.at[j],
            device_id=sibling, device_id_type=MESH))
    return copies


def _forward_start(land, name):
    def body(land_ref, send_sems, recv_sems, land_thru, token):
        for cp in _forward_copies(land_ref, send_sems, recv_sems, False):
            cp.start()
        token[...] = jnp.zeros_like(token)

    sem = pltpu.SemaphoreType.DMA((len(SAME_CORE) - 1,))
    send, recv, thru, _ = pl.pallas_call(
        body,
        in_specs=[_HBM],
        out_specs=[_SEM, _SEM, _HBM, pl.BlockSpec(memory_space=pltpu.VMEM)],
        out_shape=[sem, sem, pltpu.HBM(land.shape, land.dtype), SDS((8, LANES), F32)],
        input_output_aliases={0: 2},
        compiler_params=pltpu.CompilerParams(has_side_effects=_DATAFLOW),
        name=name,
    )(land)
    return send, recv, thru


def _forward_wait(handle, after, name):
    def body(land_ref, send_sems, recv_sems, after_ref, land_out):
        for cp in _forward_copies(land_ref, send_sems, recv_sems, False):
            cp.wait_send()
        for cp in _forward_copies(land_ref, send_sems, recv_sems, True):
            cp.wait_recv()

    send, recv, land = handle
    return pl.pallas_call(
        body,
        in_specs=[_HBM, _SEM, _SEM, pl.BlockSpec(memory_space=pl.ANY)],
        out_specs=_HBM,
        out_shape=pltpu.HBM(land.shape, land.dtype),
        input_output_aliases={0: 0},
        compiler_params=pltpu.CompilerParams(has_side_effects=_DATAFLOW),
        name=name,
    )(land, send, recv, after)


def _exchange_start(arrays, scatter, name, peers=None):
    n = len(arrays)
    peers = peers or [ALL_PEERS] * n
    land_shapes = [(N_DEV,) + tuple(a.shape[1:] if sc else a.shape) for a, sc in zip(arrays, scatter)]

    def body(*refs):
        ins, lands = refs[:n], refs[n:2 * n]
        send_sems, recv_sems = refs[2 * n:3 * n], refs[3 * n:4 * n]
        token = refs[6 * n]
        for cp in _peer_copies(ins, lands, send_sems, recv_sems, scatter, peers, False):
            cp.start()
        for cp in _own_copies(ins, lands, send_sems, scatter):
            cp.start()
        token[...] = jnp.zeros_like(token)

    sends, recvs = pltpu.SemaphoreType.DMA((N_DEV,)), pltpu.SemaphoreType.DMA((N_DEV - 1,))
    outs = pl.pallas_call(
        body,
        in_specs=[_HBM] * (2 * n),
        out_specs=[_SEM] * (2 * n) + [_HBM] * (2 * n) + [pl.BlockSpec(memory_space=pltpu.VMEM)],
        out_shape=[sends] * n + [recvs] * n + [pltpu.HBM(a.shape, a.dtype) for a in arrays]
        + [pltpu.HBM(sh, a.dtype) for sh, a in zip(land_shapes, arrays)] + [SDS((8, LANES), F32)],
        input_output_aliases={i: 2 * n + i for i in range(2 * n)},
        compiler_params=pltpu.CompilerParams(has_side_effects=_DATAFLOW),
        name=name,
    )(*[pltpu.with_memory_space_constraint(a, pltpu.HBM) for a in arrays],
      *[pltpu.with_memory_space_constraint(lax.empty(sh, a.dtype), pltpu.HBM) for sh, a in zip(land_shapes, arrays)])
    handles = [dict(send=outs[w], recv=outs[n + w], src=outs[2 * n + w], land=outs[3 * n + w], scatter=scatter[w],
                    peers=peers[w]) for w in range(n)]
    return handles, outs[4 * n]


def _exchange_wait(handles, after, name):
    n = len(handles)
    scatter, peers = [h["scatter"] for h in handles], [h["peers"] for h in handles]

    def body(*refs):
        ins, lands = refs[:n], refs[n:2 * n]
        send_sems, recv_sems = refs[2 * n:3 * n], refs[3 * n:4 * n]
        for cp in _peer_copies(ins, lands, send_sems, recv_sems, scatter, peers, False):
            cp.wait_send()
        for cp in _peer_copies(ins, lands, send_sems, recv_sems, scatter, peers, True):
            cp.wait_recv()
        for cp in _own_copies(ins, lands, send_sems, scatter):
            cp.wait()

    srcs, lands = [h["src"] for h in handles], [h["land"] for h in handles]
    outs = pl.pallas_call(
        body,
        in_specs=[_HBM] * (2 * n) + [_SEM] * (2 * n) + [pl.BlockSpec(memory_space=pl.ANY)],
        out_specs=[_HBM] * (2 * n),
        out_shape=[pltpu.HBM(a.shape, a.dtype) for a in srcs + lands],
        input_output_aliases={i: i for i in range(2 * n)},
        compiler_params=pltpu.CompilerParams(has_side_effects=_DATAFLOW),
        name=name,
    )(*srcs, *lands, *[h["send"] for h in handles], *[h["recv"] for h in handles], after)
    return outs[n:]


def _adamw(parts, w, m, v, name):
    rows, cols = w.shape
    tr = rows // 4 if rows % 32 == 0 else rows

    def body(p_ref, w_ref, m_ref, v_ref, g_ref, d_ref, mo_ref, vo_ref):
        g = p_ref[0].astype(F32)
        for d in range(1, N_DEV):
            g = g + p_ref[d].astype(F32)
        g_ref[...] = g
        d_ref[...], mo_ref[...], vo_ref[...] = _adam_update(g, w_ref[...], m_ref[...], v_ref[...])

    blk = pl.BlockSpec((tr, cols), lambda i: (i, 0))
    return pl.pallas_call(
        body,
        grid=(rows // tr,),
        in_specs=[pl.BlockSpec((N_DEV, tr, cols), lambda i: (0, i, 0)), blk, blk, blk],
        out_specs=[blk] * 4,
        out_shape=[SDS((rows, cols), F32)] * 4,
        compiler_params=_cparams("arbitrary"),
        name=name,
    )(parts, w, m, v)


_ROW_OF = dict(norm1_g=(0, D_MODEL), norm2_g=(1, D_MODEL), final_g=(2, D_MODEL), pool_scale=(3, POOL_W),
               b_forget=(4, N_HEADS), loss=(5, 1))


def _pack_rows(vals):
    rows = [jnp.pad(vals[n].reshape(1, width).astype(F32), ((0, 0), (0, D_MODEL - width)))
            for n, (_, width) in sorted(_ROW_OF.items(), key=lambda kv: kv[1][0])]
    return jnp.concatenate(rows + [jnp.zeros((8 - len(rows), D_MODEL), F32)], axis=0)


def _adam_update(g, w, m, v):
    m_new = ADAM_B1 * m + (1.0 - ADAM_B1) * g
    v_new = ADAM_B2 * v + (1.0 - ADAM_B2) * (g * g)
    m_hat = m_new / (1.0 - ADAM_B1 ** ADAM_STEP)
    v_hat = v_new / (1.0 - ADAM_B2 ** ADAM_STEP)
    return -ADAM_LR * (m_hat / (jnp.sqrt(v_hat) + ADAM_EPS) + ADAM_WD * w), m_new, v_new


def _adamw_replicated(parts_rows, parts_pool, w, m, v):
    names = ("norm1_g", "norm2_g", "final_g", "pool_scale", "b_forget", "w_pool")
    shapes = {n: ((len(POOL_WINDOWS), POOL_G, POOL_G) if n == "w_pool" else (1, _ROW_OF[n][1])) for n in names}

    def body(rows_ref, pool_ref, *refs):
        ins, outs = refs[:3 * len(names)], refs[3 * len(names):]

        def total(n):
            if n == "w_pool":
                pieces = [pool_ref[d] for d in range(N_DEV)]
            else:
                row, width = _ROW_OF[n]
                pieces = [rows_ref[d, row:row + 1, 0:width] for d in range(N_DEV)]
            g = pieces[0]
            for p in pieces[1:]:
                g = g + p
            return g

        outs[0][...] = total("loss")
        for k, n in enumerate(names):
            g = total(n)
            delta, m_new, v_new = _adam_update(g, ins[3 * k][...], ins[3 * k + 1][...], ins[3 * k + 2][...])
            for o_ref, val in zip(outs[1 + 4 * k:5 + 4 * k], (g, delta, m_new, v_new)):
                o_ref[...] = val

    args = [d[n].reshape(shapes[n]) for n in names for d in (w, m, v)]
    res = pl.pallas_call(
        body,
        out_shape=[SDS((1, 1), F32)] + [SDS(shapes[n], F32) for n in names for _ in range(4)],
        compiler_params=_cparams(),
        name="adamw_replicated",
    )(parts_rows, parts_pool, *args)
    return res[0], {n: [r.reshape(w[n].shape) for r in res[1 + 4 * k:5 + 4 * k]] for k, n in enumerate(names)}


def kernel(x, norm1_g, w_in, b_forget, w_pool, pool_scale, w_out, norm2_g, w_gate, w_up, w_down, final_g, loss_target, m_norm1_g, m_w_in, m_b_forget, m_w_pool, m_pool_scale, m_w_out, m_norm2_g, m_w_gate, m_w_up, m_w_down, m_final_g, v_norm1_g, v_w_in, v_b_forget, v_w_pool, v_pool_scale, v_w_out, v_norm2_g, v_w_gate, v_w_up, v_w_down, v_final_g):
    big = ("w_in", "w_out", "w_gate", "w_up", "w_down")
    order = ("norm1_g", "w_in", "b_forget", "w_pool", "pool_scale", "w_out", "norm2_g", "w_gate", "w_up", "w_down",
             "final_g")
    w = dict(norm1_g=norm1_g, w_in=w_in, b_forget=b_forget, w_pool=w_pool, pool_scale=pool_scale, w_out=w_out,
             norm2_g=norm2_g, w_gate=w_gate, w_up=w_up, w_down=w_down, final_g=final_g)
    m = dict(norm1_g=m_norm1_g, w_in=m_w_in, b_forget=m_b_forget, w_pool=m_w_pool, pool_scale=m_pool_scale,
             w_out=m_w_out, norm2_g=m_norm2_g, w_gate=m_w_gate, w_up=m_w_up, w_down=m_w_down, final_g=m_final_g)
    v = dict(norm1_g=v_norm1_g, w_in=v_w_in, b_forget=v_b_forget, w_pool=v_w_pool, pool_scale=v_pool_scale,
             w_out=v_w_out, norm2_g=v_norm2_g, w_gate=v_w_gate, w_up=v_w_up, w_down=v_w_down, final_g=v_final_g)

    flipped = ("w_in", "w_gate", "w_up")
    shard = lambda d, n: d[n][0].T if n in flipped else d[n][0]
    gather, started = _exchange_start([shard(w, n).astype(BF16) for n in big], [False] * len(big), "gather_start",
                                      peers=[SAME_CORE if n == "w_in" else ALL_PEERS for n in big])
    gather = dict(zip(big, gather))

    def gathered(names, after):
        return _exchange_wait([gather[n] for n in names], after, "gather_wait_" + names[0])

    def weight(name, after):
        if name == "w_in":
            forward = _forward_start(gathered(["w_in"], after)[0], "gather_forward_start")
            full = _forward_wait(forward, after, "gather_forward_wait").reshape(IN_W, D_MODEL)
            f0 = QKV_W + N_HEADS
            return jnp.concatenate([full[:QKV_W], full[f0:], full[QKV_W:f0],
                                    jnp.zeros((IN_PAD - IN_W, D_MODEL), BF16)], axis=0)
        if name == "w_out":
            return gathered(["w_out"], after)[0].reshape(D_MODEL, D_MODEL)
        if name == "w_gate_up":
            return [g.reshape(D_FF, D_MODEL) for g in gathered(["w_gate", "w_up"], after)]
        return gathered(["w_down"], after)[0].reshape(D_FF, D_MODEL)

    rows = lambda g: g.reshape(N_DEV, g.shape[0] // N_DEV, g.shape[1])
    sent = {}

    def emit(name, grad):
        if name == "ff":
            names, slots = ["w_down", "w_gate", "w_up"], [rows(g) for g in grad]
        else:
            names, slots = [name], [rows(grad)]
        handles, token = _exchange_start(slots, [True] * len(slots), "grads_start_" + name)
        sent.update(zip(names, handles))
        return token

    loss_row, dx, small_grads = _local_step(x[0], loss_target[0], w, weight, emit, started)

    packed = _pack_rows(dict(small_grads, loss=0.5 / D_MODEL * jnp.sum(loss_row)))
    small_handles, after = _exchange_start([packed, small_grads["w_pool"]], [False, False], "grads_start_replicated")

    outs = {}
    for name in ("w_down", "w_gate", "w_up", "w_out", "w_in"):
        (parts,) = _exchange_wait([sent[name]], after, "grads_wait_" + name)
        outs[name] = _adamw(parts, shard(w, name), shard(m, name), shard(v, name), "adamw_" + name)
        after = outs[name][0]
        outs[name] = [(a.T if name in flipped else a)[None] for a in outs[name]]
    parts_rows, parts_pool = _exchange_wait(small_handles, after, "grads_wait_replicated")
    loss, small = _adamw_replicated(parts_rows, parts_pool, w, m, v)
    outs.update(small)

    return (loss.reshape(()), dx[None]) + tuple(outs[n][k] for k in range(4) for n in order)
```

```python
import functools

import jax
import jax.numpy as jnp
from jax import lax
from jax.experimental import pallas as pl
from jax.experimental.pallas import tpu as pltpu

F32 = jnp.float32
BF16 = jnp.bfloat16
SDS = jax.ShapeDtypeStruct

D_MODEL = 1024
ATTN_W = 512
N_HEADS = 8
HEAD_DIM = 64
N_PAIRS = N_HEADS // 2
POOL_W = 512
POOL_WINDOWS = (2, 4, 8, 16)
POOL_G = 128
HALO = 16
IN_W = 3 * ATTN_W + N_HEADS + POOL_W
QKV_W = 3 * ATTN_W
U_OFF = QKV_W
F_OFF = QKV_W + POOL_W
IN_PAD = F_OFF + 128
D_FF = 2816
EPS = 1e-6
NEG = -1e30
N_DEV = 8
LANES = 128

ADAM_LR = 0.001
ADAM_B1 = 0.9
ADAM_B2 = 0.999
ADAM_EPS = 1e-08
ADAM_WD = 0.01
ADAM_STEP = 10

VMEM_LIMIT_BYTES = 56 * 1024 * 1024
MESH = pl.DeviceIdType.MESH
NT = (((1,), (1,)), ((), ()))
TN = (((0,), (0,)), ((), ()))


def _cparams(*sem):
    return pltpu.CompilerParams(dimension_semantics=sem or None, vmem_limit_bytes=VMEM_LIMIT_BYTES)


def _split3(a):
    hi = a.astype(BF16)
    r1 = a - hi.astype(F32)
    mid = r1.astype(BF16)
    lo = (r1 - mid.astype(F32)).astype(BF16)
    return hi, mid, lo


def _dot_sel(a, sel, dims=None):
    sb = sel.astype(BF16)
    if dims is None:
        return sum(jnp.dot(p, sb, preferred_element_type=F32) for p in _split3(a))
    return sum(lax.dot_general(p, sb, dims, preferred_element_type=F32) for p in _split3(a))


def _sel_dot(sel, a, dims=None):
    sb = sel.astype(BF16)
    if dims is None:
        return sum(jnp.dot(sb, p, preferred_element_type=F32) for p in _split3(a))
    return sum(lax.dot_general(sb, p, dims, preferred_element_type=F32) for p in _split3(a))


def _iota2(shape, dim):
    return lax.broadcasted_iota(jnp.int32, shape, dim)


def _norm1(x, g1, *, tm):
    s = x.shape[0]

    def body(x_ref, g_ref, h_ref, r_ref):
        xv = x_ref[...]
        r = lax.rsqrt(jnp.mean(xv * xv, axis=-1, keepdims=True) + EPS)
        h_ref[...] = (xv * r * g_ref[...]).astype(BF16)
        r_ref[...] = r

    row = lambda w: pl.BlockSpec((tm, w), lambda i: (i, 0))
    return pl.pallas_call(
        body,
        grid=(s // tm,),
        in_specs=[row(D_MODEL), pl.BlockSpec((1, D_MODEL), lambda i: (0, 0))],
        out_specs=[row(D_MODEL), row(1)],
        out_shape=[SDS((s, D_MODEL), BF16), SDS((s, 1), F32)],
        compiler_params=_cparams("arbitrary"),
        name="norm1",
    )(x, g1)


def _in_proj(h, w_in_t, *, tm):
    s = h.shape[0]

    def body(h_ref, w_ref, qkv_ref, u_ref, fl_ref):
        h = h_ref[...]
        qkv_ref[...] = lax.dot_general(h, w_ref[0:QKV_W, :], NT, preferred_element_type=F32).astype(BF16)
        u_ref[...] = lax.dot_general(h, w_ref[U_OFF:F_OFF, :], NT, preferred_element_type=F32)
        fl_ref[...] = lax.dot_general(h, w_ref[F_OFF:IN_PAD, :], NT, preferred_element_type=F32)

    row = lambda w: pl.BlockSpec((tm, w), lambda i: (i, 0))
    return pl.pallas_call(
        body,
        grid=(s // tm,),
        in_specs=[row(D_MODEL), pl.BlockSpec((IN_PAD, D_MODEL), lambda i: (0, 0))],
        out_specs=[row(QKV_W), row(POOL_W), row(LANES)],
        out_shape=[SDS((s, QKV_W), BF16), SDS((s, POOL_W), F32), SDS((s, LANES), F32)],
        compiler_params=_cparams("arbitrary"),
        name="in_proj",
    )(h, w_in_t)


def _head_block_masks(rows, nb):
    shift = nb.bit_length() - 1
    rr, cc = _iota2((rows, rows), 0), _iota2((rows, rows), 1)
    same = lax.shift_right_logical(rr, shift) == lax.shift_right_logical(cc, shift)
    return rr, cc, same


def _forget_cumsum(fl_t, b_rows):
    rows = fl_t.shape[0]
    nb = rows // N_HEADS

    def body(fl_ref, b_ref, c_ref):
        z = fl_ref[...] + b_ref[...]
        lf = jnp.minimum(z, 0.0) - jnp.log1p(jnp.exp(-jnp.abs(z)))
        upper = _iota2((LANES, LANES), 0) <= _iota2((LANES, LANES), 1)
        within = _dot_sel(lf, upper)
        tot = _dot_sel(lf, jnp.ones((LANES, LANES), F32))
        rr, cc, same = _head_block_masks(rows, nb)
        c_ref[...] = within + _sel_dot(same & (cc < rr), tot)

    return pl.pallas_call(body, out_shape=SDS(fl_t.shape, F32), compiler_params=_cparams(), name="forget_cumsum")(
        fl_t, b_rows)


BIAS_LANES = 3


def _augment(t, h, col, col_first):
    n = t.shape[0]
    lane = _iota2((n, LANES), 1)
    own = (lane < HEAD_DIM) if h == 0 else (lane >= HEAD_DIM)
    b0 = HEAD_DIM if h == 0 else 0
    c0, o0 = (b0, b0 + BIAS_LANES) if col_first else (b0 + BIAS_LANES, b0)
    x = jnp.where(own, t, 0.0)
    for off, piece in enumerate(_split3(col)):
        x = jnp.where(lane == c0 + off, piece.astype(F32), x)
    x = jnp.where((lane >= o0) & (lane < o0 + BIAS_LANES), 1.0, x)
    return x.astype(BF16)


def _attn_fwd(qkv, c_col, *, tk):
    s = qkv.shape[0]
    tq = 2 * tk
    nb = s // tk

    def body(q_ref, k_ref, v_ref, cq_ref, ck_ref, o_ref, lse_ref, kp_ref, vt_ref, st_ref):
        i = pl.program_id(1)

        @pl.when(i == 0)
        def _():
            def prep(jb, _):
                st = pl.multiple_of(jb * tk, tk)
                k2 = k_ref[pl.ds(st, tk), :].astype(F32)
                ck = ck_ref[pl.ds(st, tk), :]
                for h in range(2):
                    kp_ref[h * nb + jb] = _augment(k2, h, -ck[:, h:h + 1], True)
                vt_ref[jb] = v_ref[pl.ds(st, tk), :].astype(F32).T.astype(BF16)
                return 0

            lax.fori_loop(0, nb, prep, 0)

        qs = q_ref[...].astype(F32) * 0.125
        cq = cq_ref[...]
        qp = [_augment(qs, h, cq[:, h:h + 1], False) for h in range(2)]

        def logits(j):
            return tuple(lax.dot_general(kp_ref[h * nb + j], qp[h], NT, preferred_element_type=F32) for h in range(2))

        def softmax_pv(j, slot, stats, masked):
            out = []
            for h in range(2):
                m, l, acc = stats[h]
                st = st_ref[2 * slot + h]
                if masked:
                    st = jnp.where(j * tk + _iota2((tk, tq), 0) <= i * tq + _iota2((tk, tq), 1), st, NEG)
                m_new = jnp.maximum(m, jnp.max(st, axis=0, keepdims=True))
                alpha = jnp.exp(m - m_new)
                p = jnp.exp(st - m_new)
                l = alpha * l + jnp.sum(p, axis=0, keepdims=True)
                vt = vt_ref[j, h * HEAD_DIM:(h + 1) * HEAD_DIM, :]
                acc = alpha * acc + jnp.dot(vt, p.astype(BF16), preferred_element_type=F32)
                out.append((m_new, l, acc))
            return tuple(out)

        def put(slot, j):
            for h, st in enumerate(logits(j)):
                st_ref[2 * slot + h] = st

        def pair(t, stats):
            put(1, 2 * t + 1)
            stats = softmax_pv(2 * t, 0, stats, False)
            put(0, 2 * t + 2)
            return softmax_pv(2 * t + 1, 1, stats, False)

        init = tuple((jnp.full((1, tq), NEG, F32), jnp.zeros((1, tq), F32), jnp.zeros((HEAD_DIM, tq), F32))
                     for _ in range(2))
        put(0, 0)
        stats = lax.fori_loop(0, i, pair, init)
        put(1, 2 * i + 1)
        stats = softmax_pv(2 * i, 0, stats, True)
        (ma, la, acca), (mb, lb, accb) = softmax_pv(2 * i + 1, 1, stats, True)
        o_ref[...] = jnp.concatenate([acca / la, accb / lb], axis=0).T.astype(BF16)
        lse_ref[...] = jnp.where(_iota2((2, tq), 0) == 0, ma + jnp.log(la), mb + jnp.log(lb))

    return pl.pallas_call(
        body,
        grid=(N_PAIRS, s // tq),
        in_specs=[
            pl.BlockSpec((tq, LANES), lambda p, i: (i, p)),
            pl.BlockSpec((s, LANES), lambda p, i: (0, N_PAIRS + p)),
            pl.BlockSpec((s, LANES), lambda p, i: (0, 2 * N_PAIRS + p)),
            pl.BlockSpec((None, tq, 2), lambda p, i: (p, i, 0)),
            pl.BlockSpec((None, s, 2), lambda p, i: (p, 0, 0)),
        ],
        out_specs=[
            pl.BlockSpec((tq, LANES), lambda p, i: (i, p)),
            pl.BlockSpec((None, None, 2, tq), lambda p, i: (p, i, 0, 0)),
        ],
        out_shape=[SDS((s, ATTN_W), BF16), SDS((N_PAIRS, s // tq, 2, tq), F32)],
        scratch_shapes=[pltpu.VMEM((2 * nb, tk, LANES), BF16), pltpu.VMEM((nb, LANES, tk), BF16),
                        pltpu.VMEM((4, tk, tq), F32)],
        compiler_params=_cparams("arbitrary", "arbitrary"),
        name="attn_fwd",
    )(qkv, qkv, qkv, c_col, c_col)


def _pool_counts(row0, tm, w):
    t = row0 + _iota2((tm, 1), 0)
    return jnp.minimum(t + 1, w).astype(F32)


def _pool_fwd(u, w_pool, pool_scale, *, tm):
    s = u.shape[0]

    def body(u_ref, w_ref, sc_ref, pooled_ref, po_ref, tail_ref):
        i = pl.program_id(0)

        @pl.when(i == 0)
        def _():
            tail_ref[...] = jnp.zeros_like(tail_ref)

        uv = u_ref[...]
        ext = jnp.concatenate([tail_ref[...], uv], axis=0)
        tail_ref[...] = uv[tm - HALO:, :]
        for g, w in enumerate(POOL_WINDOWS):
            cols = slice(g * POOL_G, (g + 1) * POOL_G)
            acc = ext[:, cols]
            k = 1
            while k < w:
                acc = acc + pltpu.roll(acc, k, axis=0)
                k *= 2
            pooled = (acc[HALO:, :] / _pool_counts(i * tm, tm, w) - uv[:, cols]).astype(BF16)
            pooled_ref[:, cols] = pooled
            mixed = jnp.dot(pooled, w_ref[g].astype(BF16), preferred_element_type=F32)
            po_ref[:, cols] = (mixed * sc_ref[:, cols]).astype(BF16)

    row = pl.BlockSpec((tm, POOL_W), lambda i: (i, 0))
    return pl.pallas_call(
        body,
        grid=(s // tm,),
        in_specs=[row, pl.BlockSpec((len(POOL_WINDOWS), POOL_G, POOL_G), lambda i: (0, 0, 0)),
                  pl.BlockSpec((1, POOL_W), lambda i: (0, 0))],
        out_specs=[row, row],
        out_shape=[SDS((s, POOL_W), BF16), SDS((s, POOL_W), BF16)],
        scratch_shapes=[pltpu.VMEM((HALO, POOL_W), F32)],
        compiler_params=_cparams("arbitrary"),
        name="pool_fwd",
    )(u, w_pool, pool_scale)


def _out_norm2(attn_o, pool_o, w_out, x, g2, *, tm):
    s = x.shape[0]

    def body(a_ref, p_ref, w_ref, x_ref, g_ref, x1_ref, h2_ref, r_ref):
        x1 = (x_ref[...] + jnp.dot(a_ref[...], w_ref[0:ATTN_W, :], preferred_element_type=F32)
              + jnp.dot(p_ref[...], w_ref[ATTN_W:, :], preferred_element_type=F32))
        r = lax.rsqrt(jnp.mean(x1 * x1, axis=-1, keepdims=True) + EPS)
        x1_ref[...] = x1
        r_ref[...] = r
        h2_ref[...] = (x1 * r * g_ref[...]).astype(BF16)

    row = lambda w: pl.BlockSpec((tm, w), lambda i: (i, 0))
    full = lambda a, b: pl.BlockSpec((a, b), lambda i: (0, 0))
    return pl.pallas_call(
        body,
        grid=(s // tm,),
        in_specs=[row(ATTN_W), row(POOL_W), full(D_MODEL, D_MODEL), row(D_MODEL), full(1, D_MODEL)],
        out_specs=[row(D_MODEL), row(D_MODEL), row(1)],
        out_shape=[SDS((s, D_MODEL), F32), SDS((s, D_MODEL), BF16), SDS((s, 1), F32)],
        compiler_params=_cparams("arbitrary"),
        name="out_norm2",
    )(attn_o, pool_o, w_out, x, g2)


def _gate_up(h2, wg_t, wu_t, *, tm, tn):
    s = h2.shape[0]

    def body(h_ref, wg_ref, wu_ref, gate_ref, up_ref, act_ref):
        h = h_ref[...]
        gate = lax.dot_general(h, wg_ref[...], NT, preferred_element_type=F32)
        up = lax.dot_general(h, wu_ref[...], NT, preferred_element_type=F32)
        gate_ref[...] = gate.astype(BF16)
        up_ref[...] = up.astype(BF16)
        act_ref[...] = (gate * jax.nn.sigmoid(gate) * up).astype(BF16)

    wspec = pl.BlockSpec((tn, D_MODEL), lambda c, r: (c, 0))
    ospec = pl.BlockSpec((tm, tn), lambda c, r: (r, c))
    return pl.pallas_call(
        body,
        grid=(D_FF // tn, s // tm),
        in_specs=[pl.BlockSpec((tm, D_MODEL), lambda c, r: (r, 0)), wspec, wspec],
        out_specs=[ospec, ospec, ospec],
        out_shape=[SDS((s, D_FF), BF16), SDS((s, D_FF), BF16), SDS((s, D_FF), BF16)],
        compiler_params=_cparams("arbitrary", "arbitrary"),
        name="gate_up",
    )(h2, wg_t, wu_t)


def _staggered(n, start, finish):
    pending = start(0)
    for k in range(n):
        following = start(k + 1) if k + 1 < n else None
        finish(k, pending)
        pending = following


def _down_final(act, wd, x1, gf, tgt, *, tm, sub):
    s = x1.shape[0]

    def body(a_ref, w_ref, x1_ref, g_ref, t_ref, dx2_ref, loss_ref, dgf_ref):
        @pl.when(pl.program_id(0) == 0)
        def _():
            loss_ref[...] = jnp.zeros_like(loss_ref)
            dgf_ref[...] = jnp.zeros_like(dgf_ref)

        g = g_ref[...]

        def matmul(k):
            return jnp.dot(a_ref[k * sub:(k + 1) * sub, :], w_ref[...], preferred_element_type=F32)

        def rest(k, mm):
            rows = slice(k * sub, (k + 1) * sub)
            x2 = x1_ref[rows, :] + mm
            r = lax.rsqrt(jnp.mean(x2 * x2, axis=-1, keepdims=True) + EPS)
            xn = x2 * r
            diff = xn * g - t_ref[rows, :]
            loss_ref[...] += jnp.sum(diff * diff, axis=0, keepdims=True)
            dy = diff * (1.0 / D_MODEL)
            dgf_ref[...] += jnp.sum(dy * xn, axis=0, keepdims=True)
            dxn = dy * g
            dx2_ref[rows, :] = r * (dxn - xn * jnp.mean(dxn * xn, axis=-1, keepdims=True))

        _staggered(tm // sub, matmul, rest)

    row = lambda w: pl.BlockSpec((tm, w), lambda i: (i, 0))
    full = lambda a, b: pl.BlockSpec((a, b), lambda i: (0, 0))
    return pl.pallas_call(
        body,
        grid=(s // tm,),
        in_specs=[row(D_FF), full(D_FF, D_MODEL), row(D_MODEL), full(1, D_MODEL), row(D_MODEL)],
        out_specs=[row(D_MODEL), full(1, D_MODEL), full(1, D_MODEL)],
        out_shape=[SDS((s, D_MODEL), F32), SDS((1, D_MODEL), F32), SDS((1, D_MODEL), F32)],
        compiler_params=_cparams("arbitrary"),
        name="down_final",
    )(act, wd, x1, gf, tgt)


def _swiglu_bwd(dx2, wd, gate, up, *, tm, tn):
    s = dx2.shape[0]

    def body(d_ref, w_ref, gate_ref, up_ref, dgate_ref, dup_ref):
        dact = lax.dot_general(d_ref[...].astype(BF16), w_ref[...], NT, preferred_element_type=F32)
        gate = gate_ref[...].astype(F32)
        sg = jax.nn.sigmoid(gate)
        dup_ref[...] = (dact * (gate * sg)).astype(BF16)
        dgate_ref[...] = (dact * up_ref[...].astype(F32) * (sg * (1.0 + gate * (1.0 - sg)))).astype(BF16)

    ospec = pl.BlockSpec((tm, tn), lambda c, r: (r, c))
    return pl.pallas_call(
        body,
        grid=(D_FF // tn, s // tm),
        in_specs=[pl.BlockSpec((tm, D_MODEL), lambda c, r: (r, 0)), pl.BlockSpec((tn, D_MODEL), lambda c, r: (c, 0)),
                  ospec, ospec],
        out_specs=[ospec, ospec],
        out_shape=[SDS((s, D_FF), BF16), SDS((s, D_FF), BF16)],
        compiler_params=_cparams("arbitrary", "arbitrary"),
        name="swiglu_bwd",
    )(dx2, wd, gate, up)


def _mm_tn_stacked(as_, rows, b, *, ts, name):
    s, nb_ = b.shape
    n = len(as_)
    offsets = [sum(rows[:i]) for i in range(n)]

    def body(*refs):
        a_refs, b_ref, o_ref, acc_ref = refs[:n], refs[n], refs[n + 1], refs[n + 2]
        k = pl.program_id(0)

        @pl.when(k == 0)
        def _():
            acc_ref[...] = jnp.zeros_like(acc_ref)

        bv = b_ref[...].astype(BF16)
        for a_ref, off, cnt in zip(a_refs, offsets, rows):
            part = lax.dot_general(a_ref[...].astype(BF16), bv, TN, preferred_element_type=F32)
            acc_ref[off:off + cnt, :] += part[0:cnt, :]

        @pl.when(k == s // ts - 1)
        def _():
            o_ref[...] = acc_ref[...].astype(BF16)

    return pl.pallas_call(
        body,
        grid=(s // ts,),
        in_specs=[pl.BlockSpec((ts, a.shape[1]), lambda k: (k, 0)) for a in as_] + [pl.BlockSpec((ts, nb_), lambda k: (k, 0))],
        out_specs=pl.BlockSpec((sum(rows), nb_), lambda k: (0, 0)),
        out_shape=SDS((sum(rows), nb_), BF16),
        scratch_shapes=[pltpu.VMEM((sum(rows), nb_), F32)],
        compiler_params=_cparams("arbitrary"),
        name=name,
    )(*as_, b)


def _norm_bwd(dh, x, r, g, dres):
    xn = x * r
    dxn = dh * g
    dx = dres + r * (dxn - xn * jnp.mean(dxn * xn, axis=-1, keepdims=True))
    return dx, jnp.sum(dh * xn, axis=0, keepdims=True)


def _mlp_in_bwd(dgate, dup, wg_t, wu_t, w_out, x1, r2, g2, dx2, *, tm):
    s = x1.shape[0]

    def body(dg_ref, du_ref, wg_ref, wu_ref, wo_ref, x_ref, r_ref, g_ref, d_ref, dx1_ref, dmix_ref, dg2_ref):
        @pl.when(pl.program_id(0) == 0)
        def _():
            dg2_ref[...] = jnp.zeros_like(dg2_ref)

        dh2 = (jnp.dot(dg_ref[...], wg_ref[...], preferred_element_type=F32)
               + jnp.dot(du_ref[...], wu_ref[...], preferred_element_type=F32))
        dx1, dg2 = _norm_bwd(dh2, x_ref[...], r_ref[...], g_ref[...], d_ref[...])
        dg2_ref[...] += dg2
        dx1_ref[...] = dx1
        dmix_ref[...] = lax.dot_general(dx1.astype(BF16), wo_ref[...], NT, preferred_element_type=F32)

    row = lambda w: pl.BlockSpec((tm, w), lambda i: (i, 0))
    full = lambda a, b: pl.BlockSpec((a, b), lambda i: (0, 0))
    return pl.pallas_call(
        body,
        grid=(s // tm,),
        in_specs=[row(D_FF), row(D_FF), full(D_FF, D_MODEL), full(D_FF, D_MODEL), full(D_MODEL, D_MODEL),
                  row(D_MODEL), row(1), full(1, D_MODEL), row(D_MODEL)],
        out_specs=[row(D_MODEL), row(D_MODEL), full(1, D_MODEL)],
        out_shape=[SDS((s, D_MODEL), F32), SDS((s, D_MODEL), F32), SDS((1, D_MODEL), F32)],
        compiler_params=_cparams("arbitrary"),
        name="mlp_in_bwd",
    )(dgate, dup, wg_t, wu_t, w_out, x1, r2, g2, dx2)


def _pool_bwd(dmixed, pooled, w_pool, pool_scale, *, tm):
    s = pooled.shape[0]
    nt = s // tm
    ng = len(POOL_WINDOWS)

    def body(d_ref, p_ref, w_ref, sc_ref, du_ref, dw_ref, dsc_ref, head_ref):
        i = pl.program_id(0)

        @pl.when(i == 0)
        def _():
            head_ref[...] = jnp.zeros_like(head_ref)
            dw_ref[...] = jnp.zeros_like(dw_ref)
            dsc_ref[...] = jnp.zeros_like(dsc_ref)

        row0 = (nt - 1 - i) * tm
        for g, w in enumerate(POOL_WINDOWS):
            cols = slice(g * POOL_G, (g + 1) * POOL_G)
            wb = w_ref[g].astype(BF16)
            pooled_g = p_ref[:, cols]
            dpo = d_ref[:, cols]
            mixed = jnp.dot(pooled_g, wb, preferred_element_type=F32)
            dsc_ref[:, cols] += jnp.sum(dpo * mixed, axis=0, keepdims=True)
            dmp = (dpo * sc_ref[:, cols]).astype(BF16)
            dw_ref[g] += lax.dot_general(pooled_g, dmp, TN, preferred_element_type=F32)
            dpooled = lax.dot_general(dmp, wb, NT, preferred_element_type=F32)
            a = dpooled / _pool_counts(row0, tm, w)
            acc = jnp.concatenate([a, head_ref[:, cols]], axis=0)
            head_ref[:, cols] = a[0:HALO, :]
            k = 1
            while k < w:
                acc = acc + pltpu.roll(acc, tm + HALO - k, axis=0)
                k *= 2
            du_ref[:, cols] = (acc[0:tm, :] - dpooled).astype(BF16)

    rev = lambda i: (nt - 1 - i, 0)
    return pl.pallas_call(
        body,
        grid=(nt,),
        in_specs=[pl.BlockSpec((tm, POOL_W), lambda i: (nt - 1 - i, 1)), pl.BlockSpec((tm, POOL_W), rev),
                  pl.BlockSpec((ng, POOL_G, POOL_G), lambda i: (0, 0, 0)), pl.BlockSpec((1, POOL_W), lambda i: (0, 0))],
        out_specs=[pl.BlockSpec((tm, POOL_W), rev), pl.BlockSpec((ng, POOL_G, POOL_G), lambda i: (0, 0, 0)),
                   pl.BlockSpec((1, POOL_W), lambda i: (0, 0))],
        out_shape=[SDS((s, POOL_W), BF16), SDS((ng, POOL_G, POOL_G), F32), SDS((1, POOL_W), F32)],
        scratch_shapes=[pltpu.VMEM((HALO, POOL_W), F32)],
        compiler_params=_cparams("arbitrary"),
        name="pool_bwd",
    )(dmixed, pooled, w_pool, pool_scale)


SUM_ROWS = 16


def _heads_t(t):
    n = t.shape[0]
    lane = _iota2((n, LANES), 1)
    tf = t.astype(F32)
    halves = jnp.concatenate([jnp.where(lane < HEAD_DIM, tf, 0.0).T, jnp.where(lane < HEAD_DIM, 0.0, tf).T], axis=1)
    r, c = _iota2((SUM_ROWS, 2 * n), 0), _iota2((SUM_ROWS, 2 * n), 1)
    ones = jnp.where(((r == 0) & (c < n)) | ((r == 4) & (c >= n)), 1.0, 0.0)
    return jnp.concatenate([halves, ones], axis=0).astype(BF16)


def _attn_bwd(qkv, attn_o, dmixed, rowb, ck_col, *, tq):
    s = qkv.shape[0]
    tk = tq
    nb = s // tq
    rows_t = LANES + SUM_ROWS

    def body(q_ref, k_ref, v_ref, o_ref, do_ref, rowb_ref, ck_ref, dq_ref, dk_ref, dv_ref, dck_ref, dcq_ref,
             dqt_ref, delta_ref, kp_ref, qp_ref, dob_ref, qt_ref, kt_ref, dot_ref, front_ref):
        lane = _iota2((tq, LANES), 1)
        lo = lane < HEAD_DIM
        first = _iota2((8, LANES), 1) < HEAD_DIM
        sel = jnp.where(_iota2((8, LANES), 0) < 4, jnp.where(first, 1.0, 0.0), jnp.where(first, 0.0, 1.0))

        def prep(b, _):
            st = pl.multiple_of(b * tq, tq)
            do2 = do_ref[pl.ds(st, tq), :]
            delta_ref[b] = _sel_dot(sel, do2 * o_ref[pl.ds(st, tq), :].astype(F32), NT)
            dob_ref[pl.ds(st, tq), :] = do2.astype(BF16)
            dqt_ref[b] = jnp.zeros((rows_t, tq), F32)
            k2 = k_ref[pl.ds(st, tq), :].astype(F32)
            q2 = q_ref[pl.ds(st, tq), :].astype(F32)
            ck = ck_ref[pl.ds(st, tq), :]
            for h in range(2):
                kp_ref[h * nb + b] = _augment(k2, h, -ck[:, h:h + 1], True)
                qp_ref[h * nb + b] = _augment(q2 * 0.125, h, jnp.zeros((tq, 1), F32), False)
            qt_ref[b] = _heads_t(q2)
            kt_ref[b] = _heads_t(k2)
            dot_ref[b] = _heads_t(do2)[0:LANES, :]
            return 0

        lax.fori_loop(0, nb, prep, 0)

        def split(t):
            z = jnp.zeros_like(t)
            return jnp.where(lo, t, z), jnp.where(lo, z, t)

        def kv_block(j, _):
            st_j = pl.multiple_of(j * tk, tk)
            vs = split(v_ref[pl.ds(st_j, tk), :])
            kt = kt_ref[j]

            def stage(i, slot):
                ic = jnp.minimum(i, nb - 1)
                do2 = dob_ref[pl.ds(pl.multiple_of(ic * tq, tq), tq), :]
                for h in range(2):
                    front_ref[4 * slot + h] = lax.dot_general(kp_ref[h * nb + j], qp_ref[h * nb + ic], NT,
                                                              preferred_element_type=F32)
                    front_ref[4 * slot + 2 + h] = lax.dot_general(vs[h], do2, NT, preferred_element_type=F32)

            def q_block(i, slot, carry, diagonal):
                dkt, dvt = carry
                ic = jnp.minimum(i, nb - 1)
                rb = rowb_ref[ic] + jnp.where(i < nb, 0.0, NEG)
                dl = delta_ref[ic]
                pts, dsts = [], []
                for h in range(2):
                    st = front_ref[4 * slot + h] + rb[h:h + 1, :]
                    if diagonal:
                        st = jnp.where(_iota2((tk, tq), 0) <= _iota2((tk, tq), 1), st, NEG)
                    pt = jnp.exp(st)
                    pts.append(pt.astype(BF16))
                    dsts.append((pt * (front_ref[4 * slot + 2 + h] - dl[4 * h:4 * h + 1, :])).astype(BF16))
                dvt = dvt + lax.dot_general(dot_ref[ic], jnp.concatenate(pts, axis=1), NT, preferred_element_type=F32)
                dkt = dkt + lax.dot_general(qt_ref[ic], jnp.concatenate(dsts, axis=1), NT, preferred_element_type=F32)
                dqt_ref[ic] += jnp.dot(kt, jnp.concatenate(dsts, axis=0), preferred_element_type=F32)
                return dkt, dvt

            def pair(t, carry):
                i0 = j + 1 + 2 * t
                stage(i0 + 1, 0)
                carry = q_block(i0, 1, carry, False)
                stage(i0 + 2, 1)
                return q_block(i0 + 1, 0, carry, False)

            stage(j, 0)
            stage(j + 1, 1)
            carry = q_block(j, 0, (jnp.zeros((rows_t, tk), F32), jnp.zeros((LANES, tk), F32)), True)
            dkt, dvt = lax.fori_loop(0, lax.shift_right_logical(nb - j, 1), pair, carry)
            dk_ref[pl.ds(st_j, tk), :] = (dkt[0:LANES, :].T * 0.125).astype(BF16)
            dv_ref[pl.ds(st_j, tk), :] = dvt.T.astype(BF16)
            dck_ref[j] = dkt[LANES:LANES + 8, :]
            return 0

        lax.fori_loop(0, nb, kv_block, 0)

        def finish(b, _):
            acc = dqt_ref[b]
            dq_ref[pl.ds(pl.multiple_of(b * tq, tq), tq), :] = (acc[0:LANES, :].T * 0.125).astype(BF16)
            dcq_ref[b] = acc[LANES:LANES + 8, :]
            return 0

        lax.fori_loop(0, nb, finish, 0)

    col = lambda off: pl.BlockSpec((s, LANES), lambda p: (0, off + p))
    sums = pl.BlockSpec((None, nb, 8, tq), lambda p: (p, 0, 0, 0))
    return pl.pallas_call(
        body,
        grid=(N_PAIRS,),
        in_specs=[col(0), col(N_PAIRS), col(2 * N_PAIRS), col(0), col(0),
                  pl.BlockSpec((None, nb, 2, tq), lambda p: (p, 0, 0, 0)),
                  pl.BlockSpec((None, s, 2), lambda p: (p, 0, 0))],
        out_specs=[col(0), col(0), col(0), sums, sums],
        out_shape=[SDS((s, ATTN_W), BF16), SDS((s, ATTN_W), BF16), SDS((s, ATTN_W), BF16),
                   SDS((N_PAIRS, nb, 8, tq), F32), SDS((N_PAIRS, nb, 8, tq), F32)],
        scratch_shapes=[pltpu.VMEM((nb, rows_t, tq), F32), pltpu.VMEM((nb, 8, tq), F32),
                        pltpu.VMEM((2 * nb, tk, LANES), BF16), pltpu.VMEM((2 * nb, tq, LANES), BF16),
                        pltpu.VMEM((s, LANES), BF16), pltpu.VMEM((nb, rows_t, 2 * tq), BF16),
                        pltpu.VMEM((nb, rows_t, 2 * tk), BF16), pltpu.VMEM((nb, LANES, 2 * tq), BF16),
                        pltpu.VMEM((8, tk, tq), F32)],
        compiler_params=_cparams("arbitrary"),
        name="attn_bwd",
    )(qkv, qkv, qkv, attn_o, dmixed, rowb, ck_col)


def _forget_bwd(dc_t, fl_t, b_rows):
    rows = fl_t.shape[0]
    nb = rows // N_HEADS

    def body(dc_ref, fl_ref, b_ref, dfl_ref, db_ref):
        dc = dc_ref[...]
        lower = _iota2((LANES, LANES), 0) >= _iota2((LANES, LANES), 1)
        ones = jnp.ones((LANES, LANES), F32)
        rr, cc, same = _head_block_masks(rows, nb)
        dlf = _dot_sel(dc, lower) + _sel_dot(same & (cc > rr), _dot_sel(dc, ones))
        dfl = dlf / (1.0 + jnp.exp(fl_ref[...] + b_ref[...]))
        dfl_ref[...] = dfl
        shift = nb.bit_length() - 1
        hsel = lax.shift_right_logical(_iota2((N_HEADS, rows), 1), shift) == _iota2((N_HEADS, rows), 0)
        db_ref[...] = _sel_dot(hsel, _dot_sel(dfl, ones))

    return pl.pallas_call(body, out_shape=[SDS(fl_t.shape, F32), SDS((N_HEADS, LANES), F32)],
                          compiler_params=_cparams(), name="forget_bwd")(dc_t, fl_t, b_rows)


def _in_bwd(dq, dk, dv, du, dfl, w_in_t, x, r1, g1, dx1, *, tm):
    s = x.shape[0]
    pieces = ((0, ATTN_W), (ATTN_W, 2 * ATTN_W), (2 * ATTN_W, QKV_W), (U_OFF, F_OFF), (F_OFF, IN_PAD))

    def body(dq_ref, dk_ref, dv_ref, du_ref, df_ref, w_ref, x_ref, r_ref, g_ref, d_ref, dx_ref, dg1_ref):
        @pl.when(pl.program_id(0) == 0)
        def _():
            dg1_ref[...] = jnp.zeros_like(dg1_ref)

        dh = None
        for ref, (c0, c1) in zip((dq_ref, dk_ref, dv_ref, du_ref, df_ref), pieces):
            t = jnp.dot(ref[...], w_ref[c0:c1, :], preferred_element_type=F32)
            dh = t if dh is None else dh + t
        dx, dg1 = _norm_bwd(dh, x_ref[...], r_ref[...], g_ref[...], d_ref[...])
        dx_ref[...] = dx
        dg1_ref[...] += dg1

    row = lambda w: pl.BlockSpec((tm, w), lambda i: (i, 0))
    full = lambda a, b: pl.BlockSpec((a, b), lambda i: (0, 0))
    return pl.pallas_call(
        body,
        grid=(s // tm,),
        in_specs=[row(ATTN_W), row(ATTN_W), row(ATTN_W), row(POOL_W), row(LANES), full(IN_PAD, D_MODEL),
                  row(D_MODEL), row(1), full(1, D_MODEL), row(D_MODEL)],
        out_specs=[row(D_MODEL), full(1, D_MODEL)],
        out_shape=[SDS((s, D_MODEL), F32), SDS((1, D_MODEL), F32)],
        compiler_params=_cparams("arbitrary"),
        name="in_bwd",
    )(dq, dk, dv, du, dfl, w_in_t, x, r1, g1, dx1)


def _tiles(s):
    big = min(512, s)
    return dict(row=big, attn=min(256, s // 2), ff_rows=min(256, s), tall=min(1024, s))


def _tie(a, token):
    return a + token[0:1, 0:1].astype(a.dtype)


def _local_step(x, tgt, p, weight, emit, started):
    s = x.shape[0]
    t = _tiles(s)
    tm, tq = t["row"], t["attn"]
    nb = s // LANES
    nqb = s // tq
    g1, g2, gf = p["norm1_g"], p["norm2_g"], p["final_g"].reshape(1, D_MODEL)
    w_pool, pool_scale = p["w_pool"][0], p["pool_scale"]

    h, r1 = _norm1(x, _tie(g1, started), tm=tm)
    w_in_t = weight("w_in", h)
    qkv, u, fl = _in_proj(h, w_in_t, tm=t["tall"])
    fl_t = fl[:, :N_HEADS].T.reshape(N_HEADS * nb, LANES)
    b_rows = jnp.repeat(p["b_forget"].reshape(N_HEADS), nb).reshape(N_HEADS * nb, 1)
    c = _forget_cumsum(fl_t, b_rows).reshape(N_PAIRS, 2, s)
    c_col = c.transpose(0, 2, 1)
    c_rowblk = c.reshape(N_PAIRS, 2, nqb, tq).transpose(0, 2, 1, 3)
    attn_o, lse = _attn_fwd(qkv, c_col, tk=tq)
    lse = lse.reshape(N_PAIRS, nqb // 2, 2, 2, tq).transpose(0, 1, 3, 2, 4).reshape(N_PAIRS, nqb, 2, tq)
    pooled, pool_o = _pool_fwd(u, w_pool, pool_scale, tm=tm)
    w_out = weight("w_out", attn_o)
    x1, h2, r2 = _out_norm2(attn_o, pool_o, w_out, x, g2, tm=tm)
    wg_t, wu_t = weight("w_gate_up", h2)
    gate, up, act = _gate_up(h2, wg_t, wu_t, tm=t["ff_rows"], tn=D_FF)
    wd = weight("w_down", act)
    dx2, loss_row, d_gf = _down_final(act, wd, x1, gf, tgt, tm=tm, sub=min(128, tm))

    dgate, dup = _swiglu_bwd(dx2, wd, gate, up, tm=t["ff_rows"], tn=D_FF)
    d_wd = _mm_tn_stacked([act], [D_FF], dx2, ts=t["tall"], name="grad_w_down")
    d_wg_t = _mm_tn_stacked([dgate], [D_FF], h2, ts=t["tall"], name="grad_w_gate")
    d_wu_t = _mm_tn_stacked([dup], [D_FF], h2, ts=t["tall"], name="grad_w_up")
    dx1, dmixed, d_g2 = _mlp_in_bwd(dgate, dup, wg_t, wu_t, w_out, x1, r2, g2, dx2, tm=t["ff_rows"])
    du, d_wpool, d_pscale = _pool_bwd(dmixed, pooled, w_pool, pool_scale, tm=tm)
    d_wo = _mm_tn_stacked([attn_o, pool_o], [ATTN_W, POOL_W], dx1, ts=t["tall"], name="grad_w_out")
    token = emit(("w_down", "w_gate", "w_up", "w_out"), (d_wd, d_wg_t, d_wu_t, d_wo))
    rowb = _tie(c_rowblk - lse, token)
    dq, dk, dv, dck, dcq = _attn_bwd(qkv, attn_o, dmixed, rowb, c_col, tq=tq)
    dc_t = (dcq - dck)[:, :, 0::4, :].transpose(0, 2, 1, 3).reshape(N_HEADS * nb, LANES)
    dfl_t, db = _forget_bwd(dc_t, fl_t, b_rows)
    dfl = jnp.pad(dfl_t.reshape(N_HEADS, s).T, ((0, 0), (0, LANES - N_HEADS))).astype(BF16)
    d_w_in_t = _mm_tn_stacked([dq, dk, dv, dfl, du], [ATTN_W, ATTN_W, ATTN_W, N_HEADS, POOL_W], h, ts=t["tall"],
                              name="grad_w_in")
    token = emit(("w_in",), (d_w_in_t,))
    dx, d_g1 = _in_bwd(dq, dk, dv, du, dfl, w_in_t, x, r1, _tie(g1, token), dx1, tm=tm)

    small = dict(norm1_g=d_g1, b_forget=db[:, 0].reshape(1, N_HEADS), w_pool=d_wpool, pool_scale=d_pscale,
                 norm2_g=d_g2, final_g=d_gf)
    return loss_row, dx, small


def _my_index():
    return 4 * lax.axis_index("x") + 2 * lax.axis_index("y") + lax.axis_index("c")


def _peer(k):
    pos = [lax.axis_index(a) for a in ("x", "y", "c")]
    flipped = tuple(1 - p if (k >> b) & 1 else p for p, b in zip(pos, (2, 1, 0)))
    return flipped, 4 * flipped[0] + 2 * flipped[1] + flipped[2]


_HBM = pl.BlockSpec(memory_space=pltpu.HBM)
_SEM = pl.BlockSpec(memory_space=pltpu.SEMAPHORE)
_DATAFLOW = pltpu.SideEffectType.DATAFLOW_SIDE_EFFECTING


ALL_PEERS = tuple(range(1, N_DEV))
SAME_CORE = (1, 2, 4, 6)


def _peer_copies(ins, lands, send_sems, recv_sems, scatter, peers, arrivals):
    me = _my_index()
    copies = []
    for w in range(len(ins)):
        for k in peers[w]:
            dev, idx = _peer(k)
            copies.append(pltpu.make_async_remote_copy(
                src_ref=ins[w].at[idx] if scatter[w] else ins[w], dst_ref=lands[w].at[idx if arrivals else me],
                send_sem=send_sems[w].at[k - 1], recv_sem=recv_sems[w].at[k - 1], device_id=dev, device_id_type=MESH))
    return copies


def _own_copies(ins, lands, send_sems, scatter):
    me = _my_index()
    return [pltpu.make_async_copy(ins[w].at[me] if scatter[w] else ins[w], lands[w].at[me], send_sems[w].at[N_DEV - 1])
            for w in range(len(ins))]


def _forward_copies(land, send_sems, recv_sems, arrivals):
    sibling, _ = _peer(1)
    copies = []
    for j, k in enumerate(SAME_CORE[1:]):
        src, dst = _peer(k)[1], _peer(k ^ 1 if arrivals else k)[1]
        copies.append(pltpu.make_async_remote_copy(
            src_ref=land.at[src], dst_ref=land.at[dst], send_sem=send_sems.at[j], recv_sem=recv_sems.at[j],
            device_id=sibling, device_id_type=MESH))
    return copies


def _forward_start(land, name):
    def body(land_ref, send_sems, recv_sems, land_thru, token):
        for cp in _forward_copies(land_ref, send_sems, recv_sems, False):
            cp.start()
        token[...] = jnp.zeros_like(token)

    sem = pltpu.SemaphoreType.DMA((len(SAME_CORE) - 1,))
    send, recv, thru, _ = pl.pallas_call(
        body,
        in_specs=[_HBM],
        out_specs=[_SEM, _SEM, _HBM, pl.BlockSpec(memory_space=pltpu.VMEM)],
        out_shape=[sem, sem, pltpu.HBM(land.shape, land.dtype), SDS((8, LANES), F32)],
        input_output_aliases={0: 2},
        compiler_params=pltpu.CompilerParams(has_side_effects=_DATAFLOW),
        name=name,
    )(land)
    return send, recv, thru


def _forward_wait(handle, after, name):
    def body(land_ref, send_sems, recv_sems, after_ref, land_out):
        for cp in _forward_copies(land_ref, send_sems, recv_sems, False):
            cp.wait_send()
        for cp in _forward_copies(land_ref, send_sems, recv_sems, True):
            cp.wait_recv()

    send, recv, land = handle
    return pl.pallas_call(
        body,
        in_specs=[_HBM, _SEM, _SEM, pl.BlockSpec(memory_space=pl.ANY)],
        out_specs=_HBM,
        out_shape=pltpu.HBM(land.shape, land.dtype),
        input_output_aliases={0: 0},
        compiler_params=pltpu.CompilerParams(has_side_effects=_DATAFLOW),
        name=name,
    )(land, send, recv, after)


def _exchange_start(arrays, scatter, name, peers=None):
    n = len(arrays)
    peers = peers or [ALL_PEERS] * n
    land_shapes = [(N_DEV,) + tuple(a.shape[1:] if sc else a.shape) for a, sc in zip(arrays, scatter)]

    def body(*refs):
        ins, lands = refs[:n], refs[n:2 * n]
        send_sems, recv_sems = refs[2 * n:3 * n], refs[3 * n:4 * n]
        token = refs[6 * n]
        for cp in _peer_copies(ins, lands, send_sems, recv_sems, scatter, peers, False):
            cp.start()
        for cp in _own_copies(ins, lands, send_sems, scatter):
            cp.start()
        token[...] = jnp.zeros_like(token)

    sends, recvs = pltpu.SemaphoreType.DMA((N_DEV,)), pltpu.SemaphoreType.DMA((N_DEV - 1,))
    outs = pl.pallas_call(
        body,
        in_specs=[_HBM] * (2 * n),
        out_specs=[_SEM] * (2 * n) + [_HBM] * (2 * n) + [pl.BlockSpec(memory_space=pltpu.VMEM)],
        out_shape=[sends] * n + [recvs] * n + [pltpu.HBM(a.shape, a.dtype) for a in arrays]
        + [pltpu.HBM(sh, a.dtype) for sh, a in zip(land_shapes, arrays)] + [SDS((8, LANES), F32)],
        input_output_aliases={i: 2 * n + i for i in range(2 * n)},
        compiler_params=pltpu.CompilerParams(has_side_effects=_DATAFLOW),
        name=name,
    )(*[pltpu.with_memory_space_constraint(a, pltpu.HBM) for a in arrays],
      *[pltpu.with_memory_space_constraint(lax.empty(sh, a.dtype), pltpu.HBM) for sh, a in zip(land_shapes, arrays)])
    handles = [dict(send=outs[w], recv=outs[n + w], src=outs[2 * n + w], land=outs[3 * n + w], scatter=scatter[w],
                    peers=peers[w]) for w in range(n)]
    return handles, outs[4 * n]


def _exchange_wait(handles, after, name):
    n = len(handles)
    scatter, peers = [h["scatter"] for h in handles], [h["peers"] for h in handles]

    def body(*refs):
        ins, lands = refs[:n], refs[n:2 * n]
        send_sems, recv_sems = refs[2 * n:3 * n], refs[3 * n:4 * n]
        for cp in _peer_copies(ins, lands, send_sems, recv_sems, scatter, peers, False):
            cp.wait_send()
        for cp in _peer_copies(ins, lands, send_sems, recv_sems, scatter, peers, True):
            cp.wait_recv()
        for cp in _own_copies(ins, lands, send_sems, scatter):
            cp.wait()

    srcs, lands = [h["src"] for h in handles], [h["land"] for h in handles]
    outs = pl.pallas_call(
        body,
        in_specs=[_HBM] * (2 * n) + [_SEM] * (2 * n) + [pl.BlockSpec(memory_space=pl.ANY)],
        out_specs=[_HBM] * (2 * n),
        out_shape=[pltpu.HBM(a.shape, a.dtype) for a in srcs + lands],
        input_output_aliases={i: i for i in range(2 * n)},
        compiler_params=pltpu.CompilerParams(has_side_effects=_DATAFLOW),
        name=name,
    )(*srcs, *lands, *[h["send"] for h in handles], *[h["recv"] for h in handles], after)
    return outs[n:]


def _adamw(parts, w, m, v, name):
    rows, cols = w.shape
    tr = rows // 4 if rows % 32 == 0 else rows

    def body(p_ref, w_ref, m_ref, v_ref, g_ref, d_ref, mo_ref, vo_ref):
        g = p_ref[0].astype(F32)
        for d in range(1, N_DEV):
            g = g + p_ref[d].astype(F32)
        g_ref[...] = g
        d_ref[...], mo_ref[...], vo_ref[...] = _adam_update(g, w_ref[...], m_ref[...], v_ref[...])

    blk = pl.BlockSpec((tr, cols), lambda i: (i, 0))
    return pl.pallas_call(
        body,
        grid=(rows // tr,),
        in_specs=[pl.BlockSpec((N_DEV, tr, cols), lambda i: (0, i, 0)), blk, blk, blk],
        out_specs=[blk] * 4,
        out_shape=[SDS((rows, cols), F32)] * 4,
        compiler_params=_cparams("arbitrary"),
        name=name,
    )(parts, w, m, v)


_ROW_OF = dict(norm1_g=(0, D_MODEL), norm2_g=(1, D_MODEL), final_g=(2, D_MODEL), pool_scale=(3, POOL_W),
               b_forget=(4, N_HEADS), loss=(5, 1))


def _pack_rows(vals):
    rows = [jnp.pad(vals[n].reshape(1, width).astype(F32), ((0, 0), (0, D_MODEL - width)))
            for n, (_, width) in sorted(_ROW_OF.items(), key=lambda kv: kv[1][0])]
    return jnp.concatenate(rows + [jnp.zeros((8 - len(rows), D_MODEL), F32)], axis=0)


def _adam_update(g, w, m, v):
    m_new = ADAM_B1 * m + (1.0 - ADAM_B1) * g
    v_new = ADAM_B2 * v + (1.0 - ADAM_B2) * (g * g)
    m_hat = m_new / (1.0 - ADAM_B1 ** ADAM_STEP)
    v_hat = v_new / (1.0 - ADAM_B2 ** ADAM_STEP)
    return -ADAM_LR * (m_hat / (jnp.sqrt(v_hat) + ADAM_EPS) + ADAM_WD * w), m_new, v_new


def _adamw_replicated(parts_rows, parts_pool, w, m, v):
    names = ("norm1_g", "norm2_g", "final_g", "pool_scale", "b_forget", "w_pool")
    shapes = {n: ((len(POOL_WINDOWS), POOL_G, POOL_G) if n == "w_pool" else (1, _ROW_OF[n][1])) for n in names}

    def body(rows_ref, pool_ref, *refs):
        ins, outs = refs[:3 * len(names)], refs[3 * len(names):]

        def total(n):
            if n == "w_pool":
                pieces = [pool_ref[d] for d in range(N_DEV)]
            else:
                row, width = _ROW_OF[n]
                pieces = [rows_ref[d, row:row + 1, 0:width] for d in range(N_DEV)]
            g = pieces[0]
            for p in pieces[1:]:
                g = g + p
            return g

        outs[0][...] = total("loss")
        for k, n in enumerate(names):
            g = total(n)
            delta, m_new, v_new = _adam_update(g, ins[3 * k][...], ins[3 * k + 1][...], ins[3 * k + 2][...])
            for o_ref, val in zip(outs[1 + 4 * k:5 + 4 * k], (g, delta, m_new, v_new)):
                o_ref[...] = val

    args = [d[n].reshape(shapes[n]) for n in names for d in (w, m, v)]
    res = pl.pallas_call(
        body,
        out_shape=[SDS((1, 1), F32)] + [SDS(shapes[n], F32) for n in names for _ in range(4)],
        compiler_params=_cparams(),
        name="adamw_replicated",
    )(parts_rows, parts_pool, *args)
    return res[0], {n: [r.reshape(w[n].shape) for r in res[1 + 4 * k:5 + 4 * k]] for k, n in enumerate(names)}


def kernel(x, norm1_g, w_in, b_forget, w_pool, pool_scale, w_out, norm2_g, w_gate, w_up, w_down, final_g, loss_target, m_norm1_g, m_w_in, m_b_forget, m_w_pool, m_pool_scale, m_w_out, m_norm2_g, m_w_gate, m_w_up, m_w_down, m_final_g, v_norm1_g, v_w_in, v_b_forget, v_w_pool, v_pool_scale, v_w_out, v_norm2_g, v_w_gate, v_w_up, v_w_down, v_final_g):
    big = ("w_in", "w_out", "w_gate", "w_up", "w_down")
    order = ("norm1_g", "w_in", "b_forget", "w_pool", "pool_scale", "w_out", "norm2_g", "w_gate", "w_up", "w_down",
             "final_g")
    w = dict(norm1_g=norm1_g, w_in=w_in, b_forget=b_forget, w_pool=w_pool, pool_scale=pool_scale, w_out=w_out,
             norm2_g=norm2_g, w_gate=w_gate, w_up=w_up, w_down=w_down, final_g=final_g)
    m = dict(norm1_g=m_norm1_g, w_in=m_w_in, b_forget=m_b_forget, w_pool=m_w_pool, pool_scale=m_pool_scale,
             w_out=m_w_out, norm2_g=m_norm2_g, w_gate=m_w_gate, w_up=m_w_up, w_down=m_w_down, final_g=m_final_g)
    v = dict(norm1_g=v_norm1_g, w_in=v_w_in, b_forget=v_b_forget, w_pool=v_w_pool, pool_scale=v_pool_scale,
             w_out=v_w_out, norm2_g=v_norm2_g, w_gate=v_w_gate, w_up=v_w_up, w_down=v_w_down, final_g=v_final_g)

    flipped = ("w_in", "w_gate", "w_up")
    shard = lambda d, n: d[n][0].T if n in flipped else d[n][0]
    gather, started = _exchange_start([shard(w, n).astype(BF16) for n in big], [False] * len(big), "gather_start",
                                      peers=[SAME_CORE if n == "w_in" else ALL_PEERS for n in big])
    gather = dict(zip(big, gather))

    def gathered(names, after):
        return _exchange_wait([gather[n] for n in names], after, "gather_wait_" + names[0])

    def weight(name, after):
        if name == "w_in":
            forward = _forward_start(gathered(["w_in"], after)[0], "gather_forward_start")
            full = _forward_wait(forward, after, "gather_forward_wait").reshape(IN_W, D_MODEL)
            f0 = QKV_W + N_HEADS
            return jnp.concatenate([full[:QKV_W], full[f0:], full[QKV_W:f0],
                                    jnp.zeros((IN_PAD - IN_W, D_MODEL), BF16)], axis=0)
        if name == "w_out":
            return gathered(["w_out"], after)[0].reshape(D_MODEL, D_MODEL)
        if name == "w_gate_up":
            return [g.reshape(D_FF, D_MODEL) for g in gathered(["w_gate", "w_up"], after)]
        return gathered(["w_down"], after)[0].reshape(D_FF, D_MODEL)

    rows = lambda g: g.reshape(N_DEV, g.shape[0] // N_DEV, g.shape[1])
    sent = {}

    def emit(names, grads):
        handles, token = _exchange_start([rows(g) for g in grads], [True] * len(names), "grads_start_" + names[0])
        sent.update(zip(names, handles))
        return token

    loss_row, dx, small_grads = _local_step(x[0], loss_target[0], w, weight, emit, started)

    packed = _pack_rows(dict(small_grads, loss=0.5 / D_MODEL * jnp.sum(loss_row)))
    small_handles, after = _exchange_start([packed, small_grads["w_pool"]], [False, False], "grads_start_replicated")

    outs = {}
    for name in ("w_down", "w_gate", "w_up", "w_out", "w_in"):
        (parts,) = _exchange_wait([sent[name]], after, "grads_wait_" + name)
        outs[name] = _adamw(parts, shard(w, name), shard(m, name), shard(v, name), "adamw_" + name)
        after = outs[name][0]
        outs[name] = [(a.T if name in flipped else a)[None] for a in outs[name]]
    parts_rows, parts_pool = _exchange_wait(small_handles, after, "grads_wait_replicated")
    loss, small = _adamw_replicated(parts_rows, parts_pool, w, m, v)
    outs.update(small)

    return (loss.reshape(()), dx[None]) + tuple(outs[n][k] for k in range(4) for n in order)
```

```python
import jax
import jax.numpy as jnp
from jax import lax
from jax.experimental import pallas as pl
from jax.experimental.pallas import tpu as pltpu

F32 = jnp.float32
BF16 = jnp.bfloat16
SDS = jax.ShapeDtypeStruct

D_MODEL = 1024
ATTN_W = 512
N_HEADS = 8
HEAD_DIM = 64
Q_SCALE = HEAD_DIM ** -0.5
N_PAIRS = N_HEADS // 2
POOL_W = 512
POOL_WINDOWS = (2, 4, 8, 16)
POOL_G = 128
HALO = 16
IN_W = 3 * ATTN_W + N_HEADS + POOL_W
QKV_W = 3 * ATTN_W
U_OFF = QKV_W
F_OFF = QKV_W + POOL_W
IN_PAD = F_OFF + 128
D_FF = 2816
EPS = 1e-6
NEG = -1e30
N_DEV = 8
LANES = 128

ADAM_LR = 0.001
ADAM_B1 = 0.9
ADAM_B2 = 0.999
ADAM_EPS = 1e-08
ADAM_WD = 0.01
ADAM_STEP = 10

VMEM_LIMIT_BYTES = 56 * 1024 * 1024
MESH = pl.DeviceIdType.MESH
NT = (((1,), (1,)), ((), ()))
TN = (((0,), (0,)), ((), ()))


def _cparams(*sem):
    return pltpu.CompilerParams(dimension_semantics=sem or None, vmem_limit_bytes=VMEM_LIMIT_BYTES)


def _split3(a):
    hi = a.astype(BF16)
    r1 = a - hi.astype(F32)
    mid = r1.astype(BF16)
    lo = (r1 - mid.astype(F32)).astype(BF16)
    return hi, mid, lo


def _dot_sel(a, sel, dims=None):
    sb = sel.astype(BF16)
    if dims is None:
        return sum(jnp.dot(p, sb, preferred_element_type=F32) for p in _split3(a))
    return sum(lax.dot_general(p, sb, dims, preferred_element_type=F32) for p in _split3(a))


def _sel_dot(sel, a, dims=None):
    sb = sel.astype(BF16)
    if dims is None:
        return sum(jnp.dot(sb, p, preferred_element_type=F32) for p in _split3(a))
    return sum(lax.dot_general(sb, p, dims, preferred_element_type=F32) for p in _split3(a))


def _iota2(shape, dim):
    return lax.broadcasted_iota(jnp.int32, shape, dim)


def _norm1(x, g1, *, tm):
    s = x.shape[0]

    def body(x_ref, g_ref, h_ref, r_ref):
        xv = x_ref[...]
        r = lax.rsqrt(jnp.mean(xv * xv, axis=-1, keepdims=True) + EPS)
        h_ref[...] = (xv * r * g_ref[...]).astype(BF16)
        r_ref[...] = r

    row = lambda w: pl.BlockSpec((tm, w), lambda i: (i, 0))
    return pl.pallas_call(
        body,
        grid=(s // tm,),
        in_specs=[row(D_MODEL), pl.BlockSpec((1, D_MODEL), lambda i: (0, 0))],
        out_specs=[row(D_MODEL), row(1)],
        out_shape=[SDS((s, D_MODEL), BF16), SDS((s, 1), F32)],
        compiler_params=_cparams("arbitrary"),
        name="norm1",
    )(x, g1)


def _in_proj_pool(h, w_in_t, w_pool, pool_scale, *, tm):
    s = h.shape[0]

    def body(h_ref, w_ref, wp_ref, sc_ref, qkv_ref, fl_ref, pooled_ref, po_ref, tail_ref):
        i = pl.program_id(0)

        @pl.when(i == 0)
        def _():
            tail_ref[...] = jnp.zeros_like(tail_ref)

        hv = h_ref[...]
        uv = lax.dot_general(hv, w_ref[U_OFF:F_OFF, :], NT, preferred_element_type=F32)
        qkv_ref[...] = lax.dot_general(hv, w_ref[0:QKV_W, :], NT, preferred_element_type=F32).astype(BF16)
        fl_ref[...] = lax.dot_general(hv, w_ref[F_OFF:IN_PAD, :], NT, preferred_element_type=F32)
        ext = jnp.concatenate([tail_ref[...], uv], axis=0)
        tail_ref[...] = uv[tm - HALO:, :]
        for g, w in enumerate(POOL_WINDOWS):
            cols = slice(g * POOL_G, (g + 1) * POOL_G)
            acc = ext[:, cols]
            k = 1
            while k < w:
                acc = acc + pltpu.roll(acc, k, axis=0)
                k *= 2
            pooled = (acc[HALO:, :] / _pool_counts(i * tm, tm, w) - uv[:, cols]).astype(BF16)
            pooled_ref[:, cols] = pooled
            mixed = jnp.dot(pooled, wp_ref[g].astype(BF16), preferred_element_type=F32)
            po_ref[:, cols] = (mixed * sc_ref[:, cols]).astype(BF16)

    row = lambda w: pl.BlockSpec((tm, w), lambda i: (i, 0))
    return pl.pallas_call(
        body,
        grid=(s // tm,),
        in_specs=[row(D_MODEL), pl.BlockSpec((IN_PAD, D_MODEL), lambda i: (0, 0)),
                  pl.BlockSpec((len(POOL_WINDOWS), POOL_G, POOL_G), lambda i: (0, 0, 0)),
                  pl.BlockSpec((1, POOL_W), lambda i: (0, 0))],
        out_specs=[row(QKV_W), row(LANES), row(POOL_W), row(POOL_W)],
        out_shape=[SDS((s, QKV_W), BF16), SDS((s, LANES), F32), SDS((s, POOL_W), BF16), SDS((s, POOL_W), BF16)],
        scratch_shapes=[pltpu.VMEM((HALO, POOL_W), F32)],
        compiler_params=_cparams("arbitrary"),
        name="in_proj_pool",
    )(h, w_in_t, w_pool, pool_scale)


def _head_block_masks(rows, nb):
    shift = nb.bit_length() - 1
    rr, cc = _iota2((rows, rows), 0), _iota2((rows, rows), 1)
    same = lax.shift_right_logical(rr, shift) == lax.shift_right_logical(cc, shift)
    return rr, cc, same


def _forget_cumsum(fl_t, b_rows):
    rows = fl_t.shape[0]
    nb = rows // N_HEADS

    def body(fl_ref, b_ref, c_ref):
        z = fl_ref[...] + b_ref[...]
        lf = jnp.minimum(z, 0.0) - jnp.log1p(jnp.exp(-jnp.abs(z)))
        upper = _iota2((LANES, LANES), 0) <= _iota2((LANES, LANES), 1)
        within = _dot_sel(lf, upper)
        tot = _dot_sel(lf, jnp.ones((LANES, LANES), F32))
        rr, cc, same = _head_block_masks(rows, nb)
        c_ref[...] = within + _sel_dot(same & (cc < rr), tot)

    return pl.pallas_call(body, out_shape=SDS(fl_t.shape, F32), compiler_params=_cparams(), name="forget_cumsum")(
        fl_t, b_rows)


BIAS_LANES = 3


def _augment(t, h, col, col_first):
    n = t.shape[0]
    lane = _iota2((n, LANES), 1)
    own = (lane < HEAD_DIM) if h == 0 else (lane >= HEAD_DIM)
    b0 = HEAD_DIM if h == 0 else 0
    c0, o0 = (b0, b0 + BIAS_LANES) if col_first else (b0 + BIAS_LANES, b0)
    x = jnp.where(own, t, 0.0)
    for off, piece in enumerate(_split3(col)):
        x = jnp.where(lane == c0 + off, piece.astype(F32), x)
    x = jnp.where((lane >= o0) & (lane < o0 + BIAS_LANES), 1.0, x)
    return x.astype(BF16)


def _attn_fwd(qkv, c_col, *, tk):
    s = qkv.shape[0]
    tq = 2 * tk
    nb = s // tk

    def body(q_ref, k_ref, v_ref, cq_ref, ck_ref, o_ref, lse_ref, kp_ref, vt_ref, st_ref):
        i = pl.program_id(1)

        @pl.when(i == 0)
        def _():
            def prep(jb, _):
                st = pl.multiple_of(jb * tk, tk)
                k2 = k_ref[pl.ds(st, tk), :].astype(F32)
                ck = ck_ref[pl.ds(st, tk), :]
                for h in range(2):
                    kp_ref[h * nb + jb] = _augment(k2, h, -ck[:, h:h + 1], True)
                vt_ref[jb] = v_ref[pl.ds(st, tk), :].astype(F32).T.astype(BF16)
                return 0

            lax.fori_loop(0, nb, prep, 0)

        qs = q_ref[...].astype(F32) * Q_SCALE
        cq = cq_ref[...]
        qp = [_augment(qs, h, cq[:, h:h + 1], False) for h in range(2)]

        def logits(j):
            return tuple(lax.dot_general(kp_ref[h * nb + j], qp[h], NT, preferred_element_type=F32) for h in range(2))

        def softmax_pv(j, slot, stats, masked):
            out = []
            for h in range(2):
                m, l, acc = stats[h]
                st = st_ref[2 * slot + h]
                if masked:
                    st = jnp.where(j * tk + _iota2((tk, tq), 0) <= i * tq + _iota2((tk, tq), 1), st, NEG)
                m_new = jnp.maximum(m, jnp.max(st, axis=0, keepdims=True))
                alpha = jnp.exp(m - m_new)
                p = jnp.exp(st - m_new)
                l = alpha * l + jnp.sum(p, axis=0, keepdims=True)
                vt = vt_ref[j, h * HEAD_DIM:(h + 1) * HEAD_DIM, :]
                acc = alpha * acc + jnp.dot(vt, p.astype(BF16), preferred_element_type=F32)
                out.append((m_new, l, acc))
            return tuple(out)

        def put(slot, j):
            for h, st in enumerate(logits(j)):
                st_ref[2 * slot + h] = st

        def pair(t, stats):
            put(1, 2 * t + 1)
            stats = softmax_pv(2 * t, 0, stats, False)
            put(0, 2 * t + 2)
            return softmax_pv(2 * t + 1, 1, stats, False)

        init = tuple((jnp.full((1, tq), NEG, F32), jnp.zeros((1, tq), F32), jnp.zeros((HEAD_DIM, tq), F32))
                     for _ in range(2))
        put(0, 0)
        stats = lax.fori_loop(0, i, pair, init)
        put(1, 2 * i + 1)
        stats = softmax_pv(2 * i, 0, stats, True)
        (ma, la, acca), (mb, lb, accb) = softmax_pv(2 * i + 1, 1, stats, True)
        o_ref[...] = jnp.concatenate([acca / la, accb / lb], axis=0).T.astype(BF16)
        lse_ref[...] = jnp.where(_iota2((2, tq), 0) == 0, ma + jnp.log(la), mb + jnp.log(lb))

    return pl.pallas_call(
        body,
        grid=(N_PAIRS, s // tq),
        in_specs=[
            pl.BlockSpec((tq, LANES), lambda p, i: (i, p)),
            pl.BlockSpec((s, LANES), lambda p, i: (0, N_PAIRS + p)),
            pl.BlockSpec((s, LANES), lambda p, i: (0, 2 * N_PAIRS + p)),
            pl.BlockSpec((None, tq, 2), lambda p, i: (p, i, 0)),
            pl.BlockSpec((None, s, 2), lambda p, i: (p, 0, 0)),
        ],
        out_specs=[
            pl.BlockSpec((tq, LANES), lambda p, i: (i, p)),
            pl.BlockSpec((None, None, 2, tq), lambda p, i: (p, i, 0, 0)),
        ],
        out_shape=[SDS((s, ATTN_W), BF16), SDS((N_PAIRS, s // tq, 2, tq), F32)],
        scratch_shapes=[pltpu.VMEM((2 * nb, tk, LANES), BF16), pltpu.VMEM((nb, LANES, tk), BF16),
                        pltpu.VMEM((4, tk, tq), F32)],
        compiler_params=_cparams("arbitrary", "arbitrary"),
        name="attn_fwd",
    )(qkv, qkv, qkv, c_col, c_col)


def _pool_counts(row0, tm, w):
    t = row0 + _iota2((tm, 1), 0)
    return jnp.minimum(t + 1, w).astype(F32)


def _out_norm2(attn_o, pool_o, w_out, x, g2, *, tm):
    s = x.shape[0]

    def body(a_ref, p_ref, w_ref, x_ref, g_ref, x1_ref, h2_ref, r_ref):
        x1 = (x_ref[...] + jnp.dot(a_ref[...], w_ref[0:ATTN_W, :], preferred_element_type=F32)
              + jnp.dot(p_ref[...], w_ref[ATTN_W:, :], preferred_element_type=F32))
        r = lax.rsqrt(jnp.mean(x1 * x1, axis=-1, keepdims=True) + EPS)
        x1_ref[...] = x1
        r_ref[...] = r
        h2_ref[...] = (x1 * r * g_ref[...]).astype(BF16)

    row = lambda w: pl.BlockSpec((tm, w), lambda i: (i, 0))
    full = lambda a, b: pl.BlockSpec((a, b), lambda i: (0, 0))
    return pl.pallas_call(
        body,
        grid=(s // tm,),
        in_specs=[row(ATTN_W), row(POOL_W), full(D_MODEL, D_MODEL), row(D_MODEL), full(1, D_MODEL)],
        out_specs=[row(D_MODEL), row(D_MODEL), row(1)],
        out_shape=[SDS((s, D_MODEL), F32), SDS((s, D_MODEL), BF16), SDS((s, 1), F32)],
        compiler_params=_cparams("arbitrary"),
        name="out_norm2",
    )(attn_o, pool_o, w_out, x, g2)


def _gate_up(h2, wg_t, wu_t, *, tm, tn):
    s = h2.shape[0]

    def body(h_ref, wg_ref, wu_ref, gate_ref, up_ref, act_ref):
        h = h_ref[...]
        gate = lax.dot_general(h, wg_ref[...], NT, preferred_element_type=F32)
        up = lax.dot_general(h, wu_ref[...], NT, preferred_element_type=F32)
        gate_ref[...] = gate.astype(BF16)
        up_ref[...] = up.astype(BF16)
        act_ref[...] = (gate * jax.nn.sigmoid(gate) * up).astype(BF16)

    wspec = pl.BlockSpec((tn, D_MODEL), lambda c, r: (c, 0))
    ospec = pl.BlockSpec((tm, tn), lambda c, r: (r, c))
    return pl.pallas_call(
        body,
        grid=(D_FF // tn, s // tm),
        in_specs=[pl.BlockSpec((tm, D_MODEL), lambda c, r: (r, 0)), wspec, wspec],
        out_specs=[ospec, ospec, ospec],
        out_shape=[SDS((s, D_FF), BF16), SDS((s, D_FF), BF16), SDS((s, D_FF), BF16)],
        compiler_params=_cparams("arbitrary", "arbitrary"),
        name="gate_up",
    )(h2, wg_t, wu_t)


def _staggered(n, start, finish):
    pending = start(0)
    for k in range(n):
        following = start(k + 1) if k + 1 < n else None
        finish(k, pending)
        pending = following


def _down_final(act, wd, x1, gf, tgt, *, tm, sub):
    s = x1.shape[0]

    def body(a_ref, w_ref, x1_ref, g_ref, t_ref, dx2_ref, loss_ref, dgf_ref):
        @pl.when(pl.program_id(0) == 0)
        def _():
            loss_ref[...] = jnp.zeros_like(loss_ref)
            dgf_ref[...] = jnp.zeros_like(dgf_ref)

        g = g_ref[...]

        def matmul(k):
            return jnp.dot(a_ref[k * sub:(k + 1) * sub, :], w_ref[...], preferred_element_type=F32)

        def rest(k, mm):
            rows = slice(k * sub, (k + 1) * sub)
            x2 = x1_ref[rows, :] + mm
            r = lax.rsqrt(jnp.mean(x2 * x2, axis=-1, keepdims=True) + EPS)
            xn = x2 * r
            diff = xn * g - t_ref[rows, :]
            loss_ref[...] += jnp.sum(diff * diff, axis=0, keepdims=True)
            dy = diff * (1.0 / D_MODEL)
            dgf_ref[...] += jnp.sum(dy * xn, axis=0, keepdims=True)
            dxn = dy * g
            dx2_ref[rows, :] = r * (dxn - xn * jnp.mean(dxn * xn, axis=-1, keepdims=True))

        _staggered(tm // sub, matmul, rest)

    row = lambda w: pl.BlockSpec((tm, w), lambda i: (i, 0))
    full = lambda a, b: pl.BlockSpec((a, b), lambda i: (0, 0))
    return pl.pallas_call(
        body,
        grid=(s // tm,),
        in_specs=[row(D_FF), full(D_FF, D_MODEL), row(D_MODEL), full(1, D_MODEL), row(D_MODEL)],
        out_specs=[row(D_MODEL), full(1, D_MODEL), full(1, D_MODEL)],
        out_shape=[SDS((s, D_MODEL), F32), SDS((1, D_MODEL), F32), SDS((1, D_MODEL), F32)],
        compiler_params=_cparams("arbitrary"),
        name="down_final",
    )(act, wd, x1, gf, tgt)


def _swiglu_bwd(dx2, wd, gate, up, *, tm, tn):
    s = dx2.shape[0]

    def body(d_ref, w_ref, gate_ref, up_ref, dgate_ref, dup_ref):
        dact = lax.dot_general(d_ref[...].astype(BF16), w_ref[...], NT, preferred_element_type=F32)
        gate = gate_ref[...].astype(F32)
        sg = jax.nn.sigmoid(gate)
        dup_ref[...] = (dact * (gate * sg)).astype(BF16)
        dgate_ref[...] = (dact * up_ref[...].astype(F32) * (sg * (1.0 + gate * (1.0 - sg)))).astype(BF16)

    ospec = pl.BlockSpec((tm, tn), lambda c, r: (r, c))
    return pl.pallas_call(
        body,
        grid=(D_FF // tn, s // tm),
        in_specs=[pl.BlockSpec((tm, D_MODEL), lambda c, r: (r, 0)), pl.BlockSpec((tn, D_MODEL), lambda c, r: (c, 0)),
                  ospec, ospec],
        out_specs=[ospec, ospec],
        out_shape=[SDS((s, D_FF), BF16), SDS((s, D_FF), BF16)],
        compiler_params=_cparams("arbitrary", "arbitrary"),
        name="swiglu_bwd",
    )(dx2, wd, gate, up)


def _mm_tn_stacked(as_, rows, b, *, ts, name):
    s, nb_ = b.shape
    n = len(as_)
    offsets = [sum(rows[:i]) for i in range(n)]

    def body(*refs):
        a_refs, b_ref, o_ref, acc_ref = refs[:n], refs[n], refs[n + 1], refs[n + 2]
        k = pl.program_id(0)

        @pl.when(k == 0)
        def _():
            acc_ref[...] = jnp.zeros_like(acc_ref)

        bv = b_ref[...].astype(BF16)
        for a_ref, off, cnt in zip(a_refs, offsets, rows):
            part = lax.dot_general(a_ref[...].astype(BF16), bv, TN, preferred_element_type=F32)
            acc_ref[off:off + cnt, :] += part[0:cnt, :]

        @pl.when(k == s // ts - 1)
        def _():
            o_ref[...] = acc_ref[...].astype(BF16)

    return pl.pallas_call(
        body,
        grid=(s // ts,),
        in_specs=[pl.BlockSpec((ts, a.shape[1]), lambda k: (k, 0)) for a in as_] + [pl.BlockSpec((ts, nb_), lambda k: (k, 0))],
        out_specs=pl.BlockSpec((sum(rows), nb_), lambda k: (0, 0)),
        out_shape=SDS((sum(rows), nb_), BF16),
        scratch_shapes=[pltpu.VMEM((sum(rows), nb_), F32)],
        compiler_params=_cparams("arbitrary"),
        name=name,
    )(*as_, b)


def _norm_bwd(dh, x, r, g, dres):
    xn = x * r
    dxn = dh * g
    dx = dres + r * (dxn - xn * jnp.mean(dxn * xn, axis=-1, keepdims=True))
    return dx, jnp.sum(dh * xn, axis=0, keepdims=True)


def _mlp_in_pool_bwd(dgate, dup, wg_t, wu_t, w_out, x1, r2, g2, dx2, pooled, w_pool, pool_scale, *, tm):
    s = x1.shape[0]
    nt = s // tm
    ng = len(POOL_WINDOWS)

    def body(dg_ref, dup_ref, wg_ref, wu_ref, wo_ref, x_ref, r_ref, g_ref, d_ref, p_ref, w_ref, sc_ref,
             dx1_ref, dattn_ref, du_ref, dg2_ref, dw_ref, dsc_ref, head_ref):
        i = pl.program_id(0)

        @pl.when(i == 0)
        def _():
            dg2_ref[...] = jnp.zeros_like(dg2_ref)
            head_ref[...] = jnp.zeros_like(head_ref)
            dw_ref[...] = jnp.zeros_like(dw_ref)
            dsc_ref[...] = jnp.zeros_like(dsc_ref)

        dh2 = (jnp.dot(dg_ref[...], wg_ref[...], preferred_element_type=F32)
               + jnp.dot(dup_ref[...], wu_ref[...], preferred_element_type=F32))
        dx1, dg2 = _norm_bwd(dh2, x_ref[...], r_ref[...], g_ref[...], d_ref[...])
        dg2_ref[...] += dg2
        dx1_ref[...] = dx1
        dmix = lax.dot_general(dx1.astype(BF16), wo_ref[...], NT, preferred_element_type=F32)
        dattn_ref[...] = dmix[:, 0:ATTN_W]
        row0 = (nt - 1 - i) * tm
        for g, w in enumerate(POOL_WINDOWS):
            cols = slice(g * POOL_G, (g + 1) * POOL_G)
            wb = w_ref[g].astype(BF16)
            pooled_g = p_ref[:, cols]
            dpo = dmix[:, ATTN_W + g * POOL_G:ATTN_W + (g + 1) * POOL_G]
            mixed = jnp.dot(pooled_g, wb, preferred_element_type=F32)
            dsc_ref[:, cols] += jnp.sum(dpo * mixed, axis=0, keepdims=True)
            dmp = (dpo * sc_ref[:, cols]).astype(BF16)
            dw_ref[g] += lax.dot_general(pooled_g, dmp, TN, preferred_element_type=F32)
            dpooled = lax.dot_general(dmp, wb, NT, preferred_element_type=F32)
            a = dpooled / _pool_counts(row0, tm, w)
            acc = jnp.concatenate([a, head_ref[:, cols]], axis=0)
            head_ref[:, cols] = a[0:HALO, :]
            k = 1
            while k < w:
                acc = acc + pltpu.roll(acc, tm + HALO - k, axis=0)
                k *= 2
            du_ref[:, cols] = (acc[0:tm, :] - dpooled).astype(BF16)

    row = lambda w: pl.BlockSpec((tm, w), lambda i: (nt - 1 - i, 0))
    full = lambda a, b: pl.BlockSpec((a, b), lambda i: (0, 0))
    pool_w = pl.BlockSpec((ng, POOL_G, POOL_G), lambda i: (0, 0, 0))
    return pl.pallas_call(
        body,
        grid=(nt,),
        in_specs=[row(D_FF), row(D_FF), full(D_FF, D_MODEL), full(D_FF, D_MODEL), full(D_MODEL, D_MODEL),
                  row(D_MODEL), row(1), full(1, D_MODEL), row(D_MODEL), row(POOL_W), pool_w, full(1, POOL_W)],
        out_specs=[row(D_MODEL), row(ATTN_W), row(POOL_W), full(1, D_MODEL), pool_w, full(1, POOL_W)],
        out_shape=[SDS((s, D_MODEL), F32), SDS((s, ATTN_W), F32), SDS((s, POOL_W), BF16), SDS((1, D_MODEL), F32),
                   SDS((ng, POOL_G, POOL_G), F32), SDS((1, POOL_W), F32)],
        scratch_shapes=[pltpu.VMEM((HALO, POOL_W), F32)],
        compiler_params=_cparams("arbitrary"),
        name="mlp_in_pool_bwd",
    )(dgate, dup, wg_t, wu_t, w_out, x1, r2, g2, dx2, pooled, w_pool, pool_scale)


SUM_ROWS = 16


def _heads_t(t):
    n = t.shape[0]
    lane = _iota2((n, LANES), 1)
    tf = t.astype(F32)
    halves = jnp.concatenate([jnp.where(lane < HEAD_DIM, tf, 0.0).T, jnp.where(lane < HEAD_DIM, 0.0, tf).T], axis=1)
    r, c = _iota2((SUM_ROWS, 2 * n), 0), _iota2((SUM_ROWS, 2 * n), 1)
    ones = jnp.where(((r == 0) & (c < n)) | ((r == 4) & (c >= n)), 1.0, 0.0)
    return jnp.concatenate([halves, ones], axis=0).astype(BF16)


def _attn_bwd(qkv, attn_o, d_attn, rowb, ck_col, *, tq):
    s = qkv.shape[0]
    tk = tq
    nb = s // tq
    rows_t = LANES + SUM_ROWS

    def body(q_ref, k_ref, v_ref, o_ref, do_ref, rowb_ref, ck_ref, dq_ref, dk_ref, dv_ref, dck_ref, dcq_ref,
             dqt_ref, delta_ref, kp_ref, qp_ref, dob_ref, qt_ref, kt_ref, dot_ref, front_ref):
        lane = _iota2((tq, LANES), 1)
        lo = lane < HEAD_DIM
        first = _iota2((8, LANES), 1) < HEAD_DIM
        sel = jnp.where(_iota2((8, LANES), 0) < 4, jnp.where(first, 1.0, 0.0), jnp.where(first, 0.0, 1.0))

        def prep(b, _):
            st = pl.multiple_of(b * tq, tq)
            do2 = do_ref[pl.ds(st, tq), :]
            delta_ref[b] = _sel_dot(sel, do2 * o_ref[pl.ds(st, tq), :].astype(F32), NT)
            dob_ref[pl.ds(st, tq), :] = do2.astype(BF16)
            dqt_ref[b] = jnp.zeros((rows_t, tq), F32)
            k2 = k_ref[pl.ds(st, tq), :].astype(F32)
            q2 = q_ref[pl.ds(st, tq), :].astype(F32)
            ck = ck_ref[pl.ds(st, tq), :]
            for h in range(2):
                kp_ref[h * nb + b] = _augment(k2, h, -ck[:, h:h + 1], True)
                qp_ref[h * nb + b] = _augment(q2 * Q_SCALE, h, jnp.zeros((tq, 1), F32), False)
            qt_ref[b] = _heads_t(q2)
            kt_ref[b] = _heads_t(k2)
            dot_ref[b] = _heads_t(do2)[0:LANES, :]
            return 0

        lax.fori_loop(0, nb, prep, 0)

        def split(t):
            z = jnp.zeros_like(t)
            return jnp.where(lo, t, z), jnp.where(lo, z, t)

        def kv_block(j, _):
            st_j = pl.multiple_of(j * tk, tk)
            vs = split(v_ref[pl.ds(st_j, tk), :])
            kt = kt_ref[j]

            def stage(i, slot):
                ic = jnp.minimum(i, nb - 1)
                do2 = dob_ref[pl.ds(pl.multiple_of(ic * tq, tq), tq), :]
                for h in range(2):
                    front_ref[4 * slot + h] = lax.dot_general(kp_ref[h * nb + j], qp_ref[h * nb + ic], NT,
                                                              preferred_element_type=F32)
                    front_ref[4 * slot + 2 + h] = lax.dot_general(vs[h], do2, NT, preferred_element_type=F32)

            def q_block(i, slot, carry, diagonal):
                dkt, dvt = carry
                ic = jnp.minimum(i, nb - 1)
                rb = rowb_ref[ic] + jnp.where(i < nb, 0.0, NEG)
                dl = delta_ref[ic]
                pts, dsts = [], []
                for h in range(2):
                    st = front_ref[4 * slot + h] + rb[h:h + 1, :]
                    if diagonal:
                        st = jnp.where(_iota2((tk, tq), 0) <= _iota2((tk, tq), 1), st, NEG)
                    pt = jnp.exp(st)
                    pts.append(pt.astype(BF16))
                    dsts.append((pt * (front_ref[4 * slot + 2 + h] - dl[4 * h:4 * h + 1, :])).astype(BF16))
                dvt = dvt + lax.dot_general(dot_ref[ic], jnp.concatenate(pts, axis=1), NT, preferred_element_type=F32)
                dkt = dkt + lax.dot_general(qt_ref[ic], jnp.concatenate(dsts, axis=1), NT, preferred_element_type=F32)
                dqt_ref[ic] += jnp.dot(kt, jnp.concatenate(dsts, axis=0), preferred_element_type=F32)
                return dkt, dvt

            def pair(t, carry):
                i0 = j + 1 + 2 * t
                stage(i0 + 1, 0)
                carry = q_block(i0, 1, carry, False)
                stage(i0 + 2, 1)
                return q_block(i0 + 1, 0, carry, False)

            stage(j, 0)
            stage(j + 1, 1)
            carry = q_block(j, 0, (jnp.zeros((rows_t, tk), F32), jnp.zeros((LANES, tk), F32)), True)
            dkt, dvt = lax.fori_loop(0, lax.shift_right_logical(nb - j, 1), pair, carry)
            dk_ref[pl.ds(st_j, tk), :] = (dkt[0:LANES, :].T * Q_SCALE).astype(BF16)
            dv_ref[pl.ds(st_j, tk), :] = dvt.T.astype(BF16)
            dck_ref[j] = dkt[LANES:LANES + 8, :]
            return 0

        lax.fori_loop(0, nb, kv_block, 0)

        def finish(b, _):
            acc = dqt_ref[b]
            dq_ref[pl.ds(pl.multiple_of(b * tq, tq), tq), :] = (acc[0:LANES, :].T * Q_SCALE).astype(BF16)
            dcq_ref[b] = acc[LANES:LANES + 8, :]
            return 0

        lax.fori_loop(0, nb, finish, 0)

    col = lambda off: pl.BlockSpec((s, LANES), lambda p: (0, off + p))
    sums = pl.BlockSpec((None, nb, 8, tq), lambda p: (p, 0, 0, 0))
    return pl.pallas_call(
        body,
        grid=(N_PAIRS,),
        in_specs=[col(0), col(N_PAIRS), col(2 * N_PAIRS), col(0), col(0),
                  pl.BlockSpec((None, nb, 2, tq), lambda p: (p, 0, 0, 0)),
                  pl.BlockSpec((None, s, 2), lambda p: (p, 0, 0))],
        out_specs=[col(0), col(0), col(0), sums, sums],
        out_shape=[SDS((s, ATTN_W), BF16), SDS((s, ATTN_W), BF16), SDS((s, ATTN_W), BF16),
                   SDS((N_PAIRS, nb, 8, tq), F32), SDS((N_PAIRS, nb, 8, tq), F32)],
        scratch_shapes=[pltpu.VMEM((nb, rows_t, tq), F32), pltpu.VMEM((nb, 8, tq), F32),
                        pltpu.VMEM((2 * nb, tk, LANES), BF16), pltpu.VMEM((2 * nb, tq, LANES), BF16),
                        pltpu.VMEM((s, LANES), BF16), pltpu.VMEM((nb, rows_t, 2 * tq), BF16),
                        pltpu.VMEM((nb, rows_t, 2 * tk), BF16), pltpu.VMEM((nb, LANES, 2 * tq), BF16),
                        pltpu.VMEM((8, tk, tq), F32)],
        compiler_params=_cparams("arbitrary"),
        name="attn_bwd",
    )(qkv, qkv, qkv, attn_o, d_attn, rowb, ck_col)


def _forget_bwd(dc_t, fl_t, b_rows):
    rows = fl_t.shape[0]
    nb = rows // N_HEADS

    def body(dc_ref, fl_ref, b_ref, dfl_ref, db_ref):
        dc = dc_ref[...]
        lower = _iota2((LANES, LANES), 0) >= _iota2((LANES, LANES), 1)
        ones = jnp.ones((LANES, LANES), F32)
        rr, cc, same = _head_block_masks(rows, nb)
        dlf = _dot_sel(dc, lower) + _sel_dot(same & (cc > rr), _dot_sel(dc, ones))
        dfl = dlf / (1.0 + jnp.exp(fl_ref[...] + b_ref[...]))
        dfl_ref[...] = dfl
        shift = nb.bit_length() - 1
        hsel = lax.shift_right_logical(_iota2((N_HEADS, rows), 1), shift) == _iota2((N_HEADS, rows), 0)
        db_ref[...] = _sel_dot(hsel, _dot_sel(dfl, ones))

    return pl.pallas_call(body, out_shape=[SDS(fl_t.shape, F32), SDS((N_HEADS, LANES), F32)],
                          compiler_params=_cparams(), name="forget_bwd")(dc_t, fl_t, b_rows)


def _in_bwd(dq, dk, dv, du, dfl, w_in_t, x, r1, g1, dx1, *, tm):
    s = x.shape[0]
    pieces = ((0, ATTN_W), (ATTN_W, 2 * ATTN_W), (2 * ATTN_W, QKV_W), (U_OFF, F_OFF), (F_OFF, IN_PAD))

    def body(dq_ref, dk_ref, dv_ref, du_ref, df_ref, w_ref, x_ref, r_ref, g_ref, d_ref, dx_ref, dg1_ref):
        @pl.when(pl.program_id(0) == 0)
        def _():
            dg1_ref[...] = jnp.zeros_like(dg1_ref)

        dh = None
        for ref, (c0, c1) in zip((dq_ref, dk_ref, dv_ref, du_ref, df_ref), pieces):
            t = jnp.dot(ref[...], w_ref[c0:c1, :], preferred_element_type=F32)
            dh = t if dh is None else dh + t
        dx, dg1 = _norm_bwd(dh, x_ref[...], r_ref[...], g_ref[...], d_ref[...])
        dx_ref[...] = dx
        dg1_ref[...] += dg1

    row = lambda w: pl.BlockSpec((tm, w), lambda i: (i, 0))
    full = lambda a, b: pl.BlockSpec((a, b), lambda i: (0, 0))
    return pl.pallas_call(
        body,
        grid=(s // tm,),
        in_specs=[row(ATTN_W), row(ATTN_W), row(ATTN_W), row(POOL_W), row(LANES), full(IN_PAD, D_MODEL),
                  row(D_MODEL), row(1), full(1, D_MODEL), row(D_MODEL)],
        out_specs=[row(D_MODEL), full(1, D_MODEL)],
        out_shape=[SDS((s, D_MODEL), F32), SDS((1, D_MODEL), F32)],
        compiler_params=_cparams("arbitrary"),
        name="in_bwd",
    )(dq, dk, dv, du, dfl, w_in_t, x, r1, g1, dx1)


def _tiles(s):
    big = min(512, s)
    return dict(row=big, attn=min(256, s // 2), ff_rows=min(256, s), tall=min(1024, s))


def _tie(a, token):
    return a + token[0:1, 0:1].astype(a.dtype)


def _local_step(x, tgt, p, weight, emit, started):
    s = x.shape[0]
    t = _tiles(s)
    tm, tq = t["row"], t["attn"]
    nb = s // LANES
    nqb = s // tq
    g1, g2, gf = p["norm1_g"], p["norm2_g"], p["final_g"].reshape(1, D_MODEL)
    w_pool, pool_scale = p["w_pool"][0], p["pool_scale"]

    h, r1 = _norm1(x, _tie(g1, started), tm=tm)
    w_in_t = weight("w_in", h)
    qkv, fl, pooled, pool_o = _in_proj_pool(h, w_in_t, w_pool, pool_scale, tm=t["tall"])
    fl_t = fl[:, :N_HEADS].T.reshape(N_HEADS * nb, LANES)
    b_rows = jnp.repeat(p["b_forget"].reshape(N_HEADS), nb).reshape(N_HEADS * nb, 1)
    c = _forget_cumsum(fl_t, b_rows).reshape(N_PAIRS, 2, s)
    c_col = c.transpose(0, 2, 1)
    c_rowblk = c.reshape(N_PAIRS, 2, nqb, tq).transpose(0, 2, 1, 3)
    attn_o, lse = _attn_fwd(qkv, c_col, tk=tq)
    lse = lse.reshape(N_PAIRS, nqb // 2, 2, 2, tq).transpose(0, 1, 3, 2, 4).reshape(N_PAIRS, nqb, 2, tq)
    w_out = weight("w_out", attn_o)
    x1, h2, r2 = _out_norm2(attn_o, pool_o, w_out, x, g2, tm=tm)
    wg_t, wu_t = weight("w_gate_up", h2)
    gate, up, act = _gate_up(h2, wg_t, wu_t, tm=t["ff_rows"], tn=D_FF)
    wd = weight("w_down", act)
    dx2, loss_row, d_gf = _down_final(act, wd, x1, gf, tgt, tm=tm, sub=min(128, tm))

    dgate, dup = _swiglu_bwd(dx2, wd, gate, up, tm=t["ff_rows"], tn=D_FF)
    d_wd = _mm_tn_stacked([act], [D_FF], dx2, ts=t["tall"], name="grad_w_down")
    d_wg_t = _mm_tn_stacked([dgate], [D_FF], h2, ts=t["tall"], name="grad_w_gate")
    d_wu_t = _mm_tn_stacked([dup], [D_FF], h2, ts=t["tall"], name="grad_w_up")
    dx1, d_attn, du, d_g2, d_wpool, d_pscale = _mlp_in_pool_bwd(dgate, dup, wg_t, wu_t, w_out, x1, r2, g2, dx2, pooled,
                                                               w_pool, pool_scale, tm=t["ff_rows"])
    d_wo = _mm_tn_stacked([attn_o, pool_o], [ATTN_W, POOL_W], dx1, ts=t["tall"], name="grad_w_out")
    token = emit(("w_down", "w_gate", "w_up", "w_out"), (d_wd, d_wg_t, d_wu_t, d_wo))
    rowb = _tie(c_rowblk - lse, token)
    dq, dk, dv, dck, dcq = _attn_bwd(qkv, attn_o, d_attn, rowb, c_col, tq=tq)
    dc_t = (dcq - dck)[:, :, 0::4, :].transpose(0, 2, 1, 3).reshape(N_HEADS * nb, LANES)
    dfl_t, db = _forget_bwd(dc_t, fl_t, b_rows)
    dfl = jnp.pad(dfl_t.reshape(N_HEADS, s).T, ((0, 0), (0, LANES - N_HEADS))).astype(BF16)
    d_w_in_t = _mm_tn_stacked([dq, dk, dv, dfl, du], [ATTN_W, ATTN_W, ATTN_W, N_HEADS, POOL_W], h, ts=t["tall"],
                              name="grad_w_in")
    token = emit(("w_in",), (d_w_in_t,))
    dx, d_g1 = _in_bwd(dq, dk, dv, du, dfl, w_in_t, x, r1, _tie(g1, token), dx1, tm=tm)

    small = dict(norm1_g=d_g1, b_forget=db[:, 0].reshape(1, N_HEADS), w_pool=d_wpool, pool_scale=d_pscale,
                 norm2_g=d_g2, final_g=d_gf)
    return loss_row, dx, small


def _my_index():
    return 4 * lax.axis_index("x") + 2 * lax.axis_index("y") + lax.axis_index("c")


def _peer(k):
    pos = [lax.axis_index(a) for a in ("x", "y", "c")]
    flipped = tuple(1 - p if (k >> b) & 1 else p for p, b in zip(pos, (2, 1, 0)))
    return flipped, 4 * flipped[0] + 2 * flipped[1] + flipped[2]


_HBM = pl.BlockSpec(memory_space=pltpu.HBM)
_SEM = pl.BlockSpec(memory_space=pltpu.SEMAPHORE)
_DATAFLOW = pltpu.SideEffectType.DATAFLOW_SIDE_EFFECTING


ALL_PEERS = tuple(range(1, N_DEV))
SAME_CORE = (1, 2, 4, 6)


def _peer_copies(ins, lands, send_sems, recv_sems, scatter, peers, arrivals):
    me = _my_index()
    copies = []
    for w in range(len(ins)):
        for k in peers[w]:
            dev, idx = _peer(k)
            copies.append(pltpu.make_async_remote_copy(
                src_ref=ins[w].at[idx] if scatter[w] else ins[w], dst_ref=lands[w].at[idx if arrivals else me],
                send_sem=send_sems[w].at[k - 1], recv_sem=recv_sems[w].at[k - 1], device_id=dev, device_id_type=MESH))
    return copies


def _own_copies(ins, lands, send_sems, scatter):
    me = _my_index()
    return [pltpu.make_async_copy(ins[w].at[me] if scatter[w] else ins[w], lands[w].at[me], send_sems[w].at[N_DEV - 1])
            for w in range(len(ins))]


def _forward_copies(land, send_sems, recv_sems, arrivals):
    sibling, _ = _peer(1)
    copies = []
    for j, k in enumerate(SAME_CORE[1:]):
        src, dst = _peer(k)[1], _peer(k ^ 1 if arrivals else k)[1]
        copies.append(pltpu.make_async_remote_copy(
            src_ref=land.at[src], dst_ref=land.at[dst], send_sem=send_sems.at[j], recv_sem=recv_sems.at[j],
            device_id=sibling, device_id_type=MESH))
    return copies


def _forward_start(land, name):
    def body(land_ref, send_sems, recv_sems, land_thru, token):
        for cp in _forward_copies(land_ref, send_sems, recv_sems, False):
            cp.start()
        token[...] = jnp.zeros_like(token)

    sem = pltpu.SemaphoreType.DMA((len(SAME_CORE) - 1,))
    send, recv, thru, _ = pl.pallas_call(
        body,
        in_specs=[_HBM],
        out_specs=[_SEM, _SEM, _HBM, pl.BlockSpec(memory_space=pltpu.VMEM)],
        out_shape=[sem, sem, pltpu.HBM(land.shape, land.dtype), SDS((8, LANES), F32)],
        input_output_aliases={0: 2},
        compiler_params=pltpu.CompilerParams(has_side_effects=_DATAFLOW),
        name=name,
    )(land)
    return send, recv, thru


def _forward_wait(handle, after, name):
    def body(land_ref, send_sems, recv_sems, after_ref, land_out):
        for cp in _forward_copies(land_ref, send_sems, recv_sems, False):
            cp.wait_send()
        for cp in _forward_copies(land_ref, send_sems, recv_sems, True):
            cp.wait_recv()

    send, recv, land = handle
    return pl.pallas_call(
        body,
        in_specs=[_HBM, _SEM, _SEM, pl.BlockSpec(memory_space=pl.ANY)],
        out_specs=_HBM,
        out_shape=pltpu.HBM(land.shape, land.dtype),
        input_output_aliases={0: 0},
        compiler_params=pltpu.CompilerParams(has_side_effects=_DATAFLOW),
        name=name,
    )(land, send, recv, after)


def _exchange_start(arrays, scatter, name, peers=None):
    n = len(arrays)
    peers = peers or [ALL_PEERS] * n
    land_shapes = [(N_DEV,) + tuple(a.shape[1:] if sc else a.shape) for a, sc in zip(arrays, scatter)]

    def body(*refs):
        ins, lands = refs[:n], refs[n:2 * n]
        send_sems, recv_sems = refs[2 * n:3 * n], refs[3 * n:4 * n]
        token = refs[6 * n]
        for cp in _peer_copies(ins, lands, send_sems, recv_sems, scatter, peers, False):
            cp.start()
        for cp in _own_copies(ins, lands, send_sems, scatter):
            cp.start()
        token[...] = jnp.zeros_like(token)

    sends, recvs = pltpu.SemaphoreType.DMA((N_DEV,)), pltpu.SemaphoreType.DMA((N_DEV - 1,))
    outs = pl.pallas_call(
        body,
        in_specs=[_HBM] * (2 * n),
        out_specs=[_SEM] * (2 * n) + [_HBM] * (2 * n) + [pl.BlockSpec(memory_space=pltpu.VMEM)],
        out_shape=[sends] * n + [recvs] * n + [pltpu.HBM(a.shape, a.dtype) for a in arrays]
        + [pltpu.HBM(sh, a.dtype) for sh, a in zip(land_shapes, arrays)] + [SDS((8, LANES), F32)],
        input_output_aliases={i: 2 * n + i for i in range(2 * n)},
        compiler_params=pltpu.CompilerParams(has_side_effects=_DATAFLOW),
        name=name,
    )(*[pltpu.with_memory_space_constraint(a, pltpu.HBM) for a in arrays],
      *[pltpu.with_memory_space_constraint(lax.empty(sh, a.dtype), pltpu.HBM) for sh, a in zip(land_shapes, arrays)])
    handles = [dict(send=outs[w], recv=outs[n + w], src=outs[2 * n + w], land=outs[3 * n + w], scatter=scatter[w],
                    peers=peers[w]) for w in range(n)]
    return handles, outs[4 * n]


def _exchange_wait(handles, after, name):
    n = len(handles)
    scatter, peers = [h["scatter"] for h in handles], [h["peers"] for h in handles]

    def body(*refs):
        ins, lands = refs[:n], refs[n:2 * n]
        send_sems, recv_sems = refs[2 * n:3 * n], refs[3 * n:4 * n]
        for cp in _peer_copies(ins, lands, send_sems, recv_sems, scatter, peers, False):
            cp.wait_send()
        for cp in _peer_copies(ins, lands, send_sems, recv_sems, scatter, peers, True):
            cp.wait_recv()
        for cp in _own_copies(ins, lands, send_sems, scatter):
            cp.wait()

    srcs, lands = [h["src"] for h in handles], [h["land"] for h in handles]
    outs = pl.pallas_call(
        body,
        in_specs=[_HBM] * (2 * n) + [_SEM] * (2 * n) + [pl.BlockSpec(memory_space=pl.ANY)],
        out_specs=[_HBM] * (2 * n),
        out_shape=[pltpu.HBM(a.shape, a.dtype) for a in srcs + lands],
        input_output_aliases={i: i for i in range(2 * n)},
        compiler_params=pltpu.CompilerParams(has_side_effects=_DATAFLOW),
        name=name,
    )(*srcs, *lands, *[h["send"] for h in handles], *[h["recv"] for h in handles], after)
    return outs[n:]


def _adamw(parts, w, m, v, name):
    rows, cols = w.shape
    tr = rows // 4 if rows % 32 == 0 else rows

    def body(p_ref, w_ref, m_ref, v_ref, g_ref, d_ref, mo_ref, vo_ref):
        g = p_ref[0].astype(F32)
        for d in range(1, N_DEV):
            g = g + p_ref[d].astype(F32)
        g_ref[...] = g
        d_ref[...], mo_ref[...], vo_ref[...] = _adam_update(g, w_ref[...], m_ref[...], v_ref[...])

    blk = pl.BlockSpec((tr, cols), lambda i: (i, 0))
    return pl.pallas_call(
        body,
        grid=(rows // tr,),
        in_specs=[pl.BlockSpec((N_DEV, tr, cols), lambda i: (0, i, 0)), blk, blk, blk],
        out_specs=[blk] * 4,
        out_shape=[SDS((rows, cols), F32)] * 4,
        compiler_params=_cparams("arbitrary"),
        name=name,
    )(parts, w, m, v)


_ROW_OF = dict(norm1_g=(0, D_MODEL), norm2_g=(1, D_MODEL), final_g=(2, D_MODEL), pool_scale=(3, POOL_W),
               b_forget=(4, N_HEADS), loss=(5, 1))


def _pack_rows(vals):
    rows = [jnp.pad(vals[n].reshape(1, width).astype(F32), ((0, 0), (0, D_MODEL - width)))
            for n, (_, width) in sorted(_ROW_OF.items(), key=lambda kv: kv[1][0])]
    return jnp.concatenate(rows + [jnp.zeros((8 - len(rows), D_MODEL), F32)], axis=0)


def _adam_update(g, w, m, v):
    m_new = ADAM_B1 * m + (1.0 - ADAM_B1) * g
    v_new = ADAM_B2 * v + (1.0 - ADAM_B2) * (g * g)
    m_hat = m_new / (1.0 - ADAM_B1 ** ADAM_STEP)
    v_hat = v_new / (1.0 - ADAM_B2 ** ADAM_STEP)
    return -ADAM_LR * (m_hat / (jnp.sqrt(v_hat) + ADAM_EPS) + ADAM_WD * w), m_new, v_new


def _adamw_replicated(parts_rows, parts_pool, w, m, v):
    names = ("norm1_g", "norm2_g", "final_g", "pool_scale", "b_forget", "w_pool")
    shapes = {n: ((len(POOL_WINDOWS), POOL_G, POOL_G) if n == "w_pool" else (1, _ROW_OF[n][1])) for n in names}

    def body(rows_ref, pool_ref, *refs):
        ins, outs = refs[:3 * len(names)], refs[3 * len(names):]

        def total(n):
            if n == "w_pool":
                pieces = [pool_ref[d] for d in range(N_DEV)]
            else:
                row, width = _ROW_OF[n]
                pieces = [rows_ref[d, row:row + 1, 0:width] for d in range(N_DEV)]
            g = pieces[0]
            for p in pieces[1:]:
                g = g + p
            return g

        outs[0][...] = total("loss")
        for k, n in enumerate(names):
            g = total(n)
            delta, m_new, v_new = _adam_update(g, ins[3 * k][...], ins[3 * k + 1][...], ins[3 * k + 2][...])
            for o_ref, val in zip(outs[1 + 4 * k:5 + 4 * k], (g, delta, m_new, v_new)):
                o_ref[...] = val

    args = [d[n].reshape(shapes[n]) for n in names for d in (w, m, v)]
    res = pl.pallas_call(
        body,
        out_shape=[SDS((1, 1), F32)] + [SDS(shapes[n], F32) for n in names for _ in range(4)],
        compiler_params=_cparams(),
        name="adamw_replicated",
    )(parts_rows, parts_pool, *args)
    return res[0], {n: [r.reshape(w[n].shape) for r in res[1 + 4 * k:5 + 4 * k]] for k, n in enumerate(names)}


def kernel(x, norm1_g, w_in, b_forget, w_pool, pool_scale, w_out, norm2_g, w_gate, w_up, w_down, final_g, loss_target, m_norm1_g, m_w_in, m_b_forget, m_w_pool, m_pool_scale, m_w_out, m_norm2_g, m_w_gate, m_w_up, m_w_down, m_final_g, v_norm1_g, v_w_in, v_b_forget, v_w_pool, v_pool_scale, v_w_out, v_norm2_g, v_w_gate, v_w_up, v_w_down, v_final_g):
    big = ("w_in", "w_out", "w_gate", "w_up", "w_down")
    order = ("norm1_g", "w_in", "b_forget", "w_pool", "pool_scale", "w_out", "norm2_g", "w_gate", "w_up", "w_down",
             "final_g")
    w = dict(norm1_g=norm1_g, w_in=w_in, b_forget=b_forget, w_pool=w_pool, pool_scale=pool_scale, w_out=w_out,
             norm2_g=norm2_g, w_gate=w_gate, w_up=w_up, w_down=w_down, final_g=final_g)
    m = dict(norm1_g=m_norm1_g, w_in=m_w_in, b_forget=m_b_forget, w_pool=m_w_pool, pool_scale=m_pool_scale,
             w_out=m_w_out, norm2_g=m_norm2_g, w_gate=m_w_gate, w_up=m_w_up, w_down=m_w_down, final_g=m_final_g)
    v = dict(norm1_g=v_norm1_g, w_in=v_w_in, b_forget=v_b_forget, w_pool=v_w_pool, pool_scale=v_pool_scale,
             w_out=v_w_out, norm2_g=v_norm2_g, w_gate=v_w_gate, w_up=v_w_up, w_down=v_w_down, final_g=v_final_g)

    flipped = ("w_in", "w_gate", "w_up")
    shard = lambda d, n: d[n][0].T if n in flipped else d[n][0]
    gather, started = _exchange_start([shard(w, n).astype(BF16) for n in big], [False] * len(big), "gather_start",
                                      peers=[SAME_CORE if n == "w_in" else ALL_PEERS for n in big])
    gather = dict(zip(big, gather))

    def gathered(names, after):
        return _exchange_wait([gather[n] for n in names], after, "gather_wait_" + names[0])

    def weight(name, after):
        if name == "w_in":
            forward = _forward_start(gathered(["w_in"], after)[0], "gather_forward_start")
            full = _forward_wait(forward, after, "gather_forward_wait").reshape(IN_W, D_MODEL)
            f0 = QKV_W + N_HEADS
            return jnp.concatenate([full[:QKV_W], full[f0:], full[QKV_W:f0],
                                    jnp.zeros((IN_PAD - IN_W, D_MODEL), BF16)], axis=0)
        if name == "w_out":
            return gathered(["w_out"], after)[0].reshape(D_MODEL, D_MODEL)
        if name == "w_gate_up":
            return [g.reshape(D_FF, D_MODEL) for g in gathered(["w_gate", "w_up"], after)]
        return gathered(["w_down"], after)[0].reshape(D_FF, D_MODEL)

    rows = lambda g: g.reshape(N_DEV, g.shape[0] // N_DEV, g.shape[1])
    sent = {}

    def emit(names, grads):
        handles, token = _exchange_start([rows(g) for g in grads], [True] * len(names), "grads_start_" + names[0])
        sent.update(zip(names, handles))
        return token

    loss_row, dx, small_grads = _local_step(x[0], loss_target[0], w, weight, emit, started)

    packed = _pack_rows(dict(small_grads, loss=0.5 / D_MODEL * jnp.sum(loss_row)))
    small_handles, after = _exchange_start([packed, small_grads["w_pool"]], [False, False], "grads_start_replicated")

    outs = {}
    for name in ("w_down", "w_gate", "w_up", "w_out", "w_in"):
        (parts,) = _exchange_wait([sent[name]], after, "grads_wait_" + name)
        outs[name] = _adamw(parts, shard(w, name), shard(m, name), shard(v, name), "adamw_" + name)
        after = outs[name][0]
        outs[name] = [(a.T if name in flipped else a)[None] for a in outs[name]]
    parts_rows, parts_pool = _exchange_wait(small_handles, after, "grads_wait_replicated")
    loss, small = _adamw_replicated(parts_rows, parts_pool, w, m, v)
    outs.update(small)

    return (loss.reshape(()), dx[None]) + tuple(outs[n][k] for k in range(4) for n in order)
```

```python
import jax
import jax.numpy as jnp
from jax import lax
from jax.experimental import pallas as pl
from jax.experimental.pallas import tpu as pltpu

F32 = jnp.float32
BF16 = jnp.bfloat16
SDS = jax.ShapeDtypeStruct

D_MODEL = 1024
ATTN_W = 512
N_HEADS = 8
HEAD_DIM = 64
Q_SCALE = HEAD_DIM ** -0.5
N_PAIRS = N_HEADS // 2
POOL_W = 512
POOL_WINDOWS = (2, 4, 8, 16)
POOL_G = 128
HALO = 16
IN_W = 3 * ATTN_W + N_HEADS + POOL_W
QKV_W = 3 * ATTN_W
U_OFF = QKV_W
F_OFF = QKV_W + POOL_W
IN_PAD = F_OFF + 128
D_FF = 2816
EPS = 1e-6
NEG = -1e30
N_DEV = 8
LANES = 128

ADAM_LR = 0.001
ADAM_B1 = 0.9
ADAM_B2 = 0.999
ADAM_EPS = 1e-08
ADAM_WD = 0.01
ADAM_STEP = 10

VMEM_LIMIT_BYTES = 56 * 1024 * 1024
MESH = pl.DeviceIdType.MESH
NT = (((1,), (1,)), ((), ()))
TN = (((0,), (0,)), ((), ()))


def _cparams(*sem):
    return pltpu.CompilerParams(dimension_semantics=sem or None, vmem_limit_bytes=VMEM_LIMIT_BYTES)


def _split3(a):
    hi = a.astype(BF16)
    r1 = a - hi.astype(F32)
    mid = r1.astype(BF16)
    lo = (r1 - mid.astype(F32)).astype(BF16)
    return hi, mid, lo


def _dot_sel(a, sel, dims=None):
    sb = sel.astype(BF16)
    if dims is None:
        return sum(jnp.dot(p, sb, preferred_element_type=F32) for p in _split3(a))
    return sum(lax.dot_general(p, sb, dims, preferred_element_type=F32) for p in _split3(a))


def _sel_dot(sel, a, dims=None):
    sb = sel.astype(BF16)
    if dims is None:
        return sum(jnp.dot(sb, p, preferred_element_type=F32) for p in _split3(a))
    return sum(lax.dot_general(sb, p, dims, preferred_element_type=F32) for p in _split3(a))


def _iota2(shape, dim):
    return lax.broadcasted_iota(jnp.int32, shape, dim)


def _norm1(x, g1, *, tm):
    s = x.shape[0]

    def body(x_ref, g_ref, h_ref, r_ref):
        xv = x_ref[...]
        r = lax.rsqrt(jnp.mean(xv * xv, axis=-1, keepdims=True) + EPS)
        h_ref[...] = (xv * r * g_ref[...]).astype(BF16)
        r_ref[...] = r

    row = lambda w: pl.BlockSpec((tm, w), lambda i: (i, 0))
    return pl.pallas_call(
        body,
        grid=(s // tm,),
        in_specs=[row(D_MODEL), pl.BlockSpec((1, D_MODEL), lambda i: (0, 0))],
        out_specs=[row(D_MODEL), row(1)],
        out_shape=[SDS((s, D_MODEL), BF16), SDS((s, 1), F32)],
        compiler_params=_cparams("arbitrary"),
        name="norm1",
    )(x, g1)


def _in_proj_pool(h, w_in_t, w_pool, pool_scale, *, tm):
    s = h.shape[0]

    def body(h_ref, w_ref, wp_ref, sc_ref, qkv_ref, fl_ref, pooled_ref, po_ref, tail_ref):
        i = pl.program_id(0)

        @pl.when(i == 0)
        def _():
            tail_ref[...] = jnp.zeros_like(tail_ref)

        hv = h_ref[...]
        uv = lax.dot_general(hv, w_ref[U_OFF:F_OFF, :], NT, preferred_element_type=F32)
        qkv_ref[...] = lax.dot_general(hv, w_ref[0:QKV_W, :], NT, preferred_element_type=F32).astype(BF16)
        fl_ref[...] = lax.dot_general(hv, w_ref[F_OFF:IN_PAD, :], NT, preferred_element_type=F32)
        ext = jnp.concatenate([tail_ref[...], uv], axis=0)
        tail_ref[...] = uv[tm - HALO:, :]
        for g, w in enumerate(POOL_WINDOWS):
            cols = slice(g * POOL_G, (g + 1) * POOL_G)
            acc = ext[:, cols]
            k = 1
            while k < w:
                acc = acc + pltpu.roll(acc, k, axis=0)
                k *= 2
            pooled = (acc[HALO:, :] / _pool_counts(i * tm, tm, w) - uv[:, cols]).astype(BF16)
            pooled_ref[:, cols] = pooled
            mixed = jnp.dot(pooled, wp_ref[g].astype(BF16), preferred_element_type=F32)
            po_ref[:, cols] = (mixed * sc_ref[:, cols]).astype(BF16)

    row = lambda w: pl.BlockSpec((tm, w), lambda i: (i, 0))
    return pl.pallas_call(
        body,
        grid=(s // tm,),
        in_specs=[row(D_MODEL), pl.BlockSpec((IN_PAD, D_MODEL), lambda i: (0, 0)),
                  pl.BlockSpec((len(POOL_WINDOWS), POOL_G, POOL_G), lambda i: (0, 0, 0)),
                  pl.BlockSpec((1, POOL_W), lambda i: (0, 0))],
        out_specs=[row(QKV_W), row(LANES), row(POOL_W), row(POOL_W)],
        out_shape=[SDS((s, QKV_W), BF16), SDS((s, LANES), F32), SDS((s, POOL_W), BF16), SDS((s, POOL_W), BF16)],
        scratch_shapes=[pltpu.VMEM((HALO, POOL_W), F32)],
        compiler_params=_cparams("arbitrary"),
        name="in_proj_pool",
    )(h, w_in_t, w_pool, pool_scale)


def _head_block_masks(rows, nb):
    shift = nb.bit_length() - 1
    rr, cc = _iota2((rows, rows), 0), _iota2((rows, rows), 1)
    same = lax.shift_right_logical(rr, shift) == lax.shift_right_logical(cc, shift)
    return rr, cc, same


def _forget_cumsum(fl_t, b_rows):
    rows = fl_t.shape[0]
    nb = rows // N_HEADS

    def body(fl_ref, b_ref, c_ref):
        z = fl_ref[...] + b_ref[...]
        lf = jnp.minimum(z, 0.0) - jnp.log1p(jnp.exp(-jnp.abs(z)))
        upper = _iota2((LANES, LANES), 0) <= _iota2((LANES, LANES), 1)
        within = _dot_sel(lf, upper)
        tot = _dot_sel(lf, jnp.ones((LANES, LANES), F32))
        rr, cc, same = _head_block_masks(rows, nb)
        c_ref[...] = within + _sel_dot(same & (cc < rr), tot)

    return pl.pallas_call(body, out_shape=SDS(fl_t.shape, F32), compiler_params=_cparams(), name="forget_cumsum")(
        fl_t, b_rows)


BIAS_LANES = 3


def _augment(t, h, col, col_first):
    n = t.shape[0]
    lane = _iota2((n, LANES), 1)
    own = (lane < HEAD_DIM) if h == 0 else (lane >= HEAD_DIM)
    b0 = HEAD_DIM if h == 0 else 0
    c0, o0 = (b0, b0 + BIAS_LANES) if col_first else (b0 + BIAS_LANES, b0)
    x = jnp.where(own, t, 0.0)
    for off, piece in enumerate(_split3(col)):
        x = jnp.where(lane == c0 + off, piece.astype(F32), x)
    x = jnp.where((lane >= o0) & (lane < o0 + BIAS_LANES), 1.0, x)
    return x.astype(BF16)


def _attn_fwd(qkv, c_col, *, tk):
    s = qkv.shape[0]
    tq = 2 * tk
    nb = s // tk

    def body(q_ref, k_ref, v_ref, cq_ref, ck_ref, o_ref, lse_ref, kp_ref, vt_ref, st_ref):
        i = pl.program_id(1)

        @pl.when(i == 0)
        def _():
            def prep(jb, _):
                st = pl.multiple_of(jb * tk, tk)
                k2 = k_ref[pl.ds(st, tk), :].astype(F32)
                ck = ck_ref[pl.ds(st, tk), :]
                for h in range(2):
                    kp_ref[h * nb + jb] = _augment(k2, h, -ck[:, h:h + 1], True)
                vt_ref[jb] = v_ref[pl.ds(st, tk), :].astype(F32).T.astype(BF16)
                return 0

            lax.fori_loop(0, nb, prep, 0)

        qs = q_ref[...].astype(F32) * Q_SCALE
        cq = cq_ref[...]
        qp = [_augment(qs, h, cq[:, h:h + 1], False) for h in range(2)]

        def logits(j):
            return tuple(lax.dot_general(kp_ref[h * nb + j], qp[h], NT, preferred_element_type=F32) for h in range(2))

        def softmax_pv(j, slot, stats, masked):
            out = []
            for h in range(2):
                m, l, acc = stats[h]
                st = st_ref[2 * slot + h]
                if masked:
                    st = jnp.where(j * tk + _iota2((tk, tq), 0) <= i * tq + _iota2((tk, tq), 1), st, NEG)
                m_new = jnp.maximum(m, jnp.max(st, axis=0, keepdims=True))
                alpha = jnp.exp(m - m_new)
                p = jnp.exp(st - m_new)
                l = alpha * l + jnp.sum(p, axis=0, keepdims=True)
                vt = vt_ref[j, h * HEAD_DIM:(h + 1) * HEAD_DIM, :]
                acc = alpha * acc + jnp.dot(vt, p.astype(BF16), preferred_element_type=F32)
                out.append((m_new, l, acc))
            return tuple(out)

        def put(slot, j):
            for h, st in enumerate(logits(j)):
                st_ref[2 * slot + h] = st

        def pair(t, stats):
            put(1, 2 * t + 1)
            stats = softmax_pv(2 * t, 0, stats, False)
            put(0, 2 * t + 2)
            return softmax_pv(2 * t + 1, 1, stats, False)

        init = tuple((jnp.full((1, tq), NEG, F32), jnp.zeros((1, tq), F32), jnp.zeros((HEAD_DIM, tq), F32))
                     for _ in range(2))
        put(0, 0)
        stats = lax.fori_loop(0, i, pair, init)
        put(1, 2 * i + 1)
        stats = softmax_pv(2 * i, 0, stats, True)
        (ma, la, acca), (mb, lb, accb) = softmax_pv(2 * i + 1, 1, stats, True)
        o_ref[...] = jnp.concatenate([acca / la, accb / lb], axis=0).T.astype(BF16)
        lse_ref[...] = jnp.where(_iota2((2, tq), 0) == 0, ma + jnp.log(la), mb + jnp.log(lb))

    return pl.pallas_call(
        body,
        grid=(N_PAIRS, s // tq),
        in_specs=[
            pl.BlockSpec((tq, LANES), lambda p, i: (i, p)),
            pl.BlockSpec((s, LANES), lambda p, i: (0, N_PAIRS + p)),
            pl.BlockSpec((s, LANES), lambda p, i: (0, 2 * N_PAIRS + p)),
            pl.BlockSpec((None, tq, 2), lambda p, i: (p, i, 0)),
            pl.BlockSpec((None, s, 2), lambda p, i: (p, 0, 0)),
        ],
        out_specs=[
            pl.BlockSpec((tq, LANES), lambda p, i: (i, p)),
            pl.BlockSpec((None, None, 2, tq), lambda p, i: (p, i, 0, 0)),
        ],
        out_shape=[SDS((s, ATTN_W), BF16), SDS((N_PAIRS, s // tq, 2, tq), F32)],
        scratch_shapes=[pltpu.VMEM((2 * nb, tk, LANES), BF16), pltpu.VMEM((nb, LANES, tk), BF16),
                        pltpu.VMEM((4, tk, tq), F32)],
        compiler_params=_cparams("arbitrary", "arbitrary"),
        name="attn_fwd",
    )(qkv, qkv, qkv, c_col, c_col)


def _pool_counts(row0, tm, w):
    t = row0 + _iota2((tm, 1), 0)
    return jnp.minimum(t + 1, w).astype(F32)


def _out_gate_up(attn_o, pool_o, w_out, x, g2, wg_t, wu_t, *, tm):
    s = x.shape[0]

    def body(a_ref, p_ref, wo_ref, x_ref, g_ref, wg_ref, wu_ref, x1_ref, h2_ref, r_ref, gate_ref, up_ref, act_ref):
        x1 = (x_ref[...] + jnp.dot(a_ref[...], wo_ref[0:ATTN_W, :], preferred_element_type=F32)
              + jnp.dot(p_ref[...], wo_ref[ATTN_W:, :], preferred_element_type=F32))
        r = lax.rsqrt(jnp.mean(x1 * x1, axis=-1, keepdims=True) + EPS)
        x1_ref[...] = x1
        r_ref[...] = r
        h2 = (x1 * r * g_ref[...]).astype(BF16)
        h2_ref[...] = h2
        gate = lax.dot_general(h2, wg_ref[...], NT, preferred_element_type=F32)
        up = lax.dot_general(h2, wu_ref[...], NT, preferred_element_type=F32)
        gate_ref[...] = gate.astype(BF16)
        up_ref[...] = up.astype(BF16)
        act_ref[...] = (gate * jax.nn.sigmoid(gate) * up).astype(BF16)

    row = lambda w: pl.BlockSpec((tm, w), lambda i: (i, 0))
    full = lambda a, b: pl.BlockSpec((a, b), lambda i: (0, 0))
    return pl.pallas_call(
        body,
        grid=(s // tm,),
        in_specs=[row(ATTN_W), row(POOL_W), full(D_MODEL, D_MODEL), row(D_MODEL), full(1, D_MODEL),
                  full(D_FF, D_MODEL), full(D_FF, D_MODEL)],
        out_specs=[row(D_MODEL), row(D_MODEL), row(1), row(D_FF), row(D_FF), row(D_FF)],
        out_shape=[SDS((s, D_MODEL), F32), SDS((s, D_MODEL), BF16), SDS((s, 1), F32), SDS((s, D_FF), BF16),
                   SDS((s, D_FF), BF16), SDS((s, D_FF), BF16)],
        compiler_params=_cparams("arbitrary"),
        name="out_gate_up",
    )(attn_o, pool_o, w_out, x, g2, wg_t, wu_t)


def _staggered(n, start, finish):
    pending = start(0)
    for k in range(n):
        following = start(k + 1) if k + 1 < n else None
        finish(k, pending)
        pending = following


def _down_final(act, wd, x1, gf, tgt, *, tm, sub):
    s = x1.shape[0]

    def body(a_ref, w_ref, x1_ref, g_ref, t_ref, dx2_ref, loss_ref, dgf_ref):
        @pl.when(pl.program_id(0) == 0)
        def _():
            loss_ref[...] = jnp.zeros_like(loss_ref)
            dgf_ref[...] = jnp.zeros_like(dgf_ref)

        g = g_ref[...]

        def matmul(k):
            return jnp.dot(a_ref[k * sub:(k + 1) * sub, :], w_ref[...], preferred_element_type=F32)

        def rest(k, mm):
            rows = slice(k * sub, (k + 1) * sub)
            x2 = x1_ref[rows, :] + mm
            r = lax.rsqrt(jnp.mean(x2 * x2, axis=-1, keepdims=True) + EPS)
            xn = x2 * r
            diff = xn * g - t_ref[rows, :]
            loss_ref[...] += jnp.sum(diff * diff, axis=0, keepdims=True)
            dy = diff * (1.0 / D_MODEL)
            dgf_ref[...] += jnp.sum(dy * xn, axis=0, keepdims=True)
            dxn = dy * g
            dx2_ref[rows, :] = r * (dxn - xn * jnp.mean(dxn * xn, axis=-1, keepdims=True))

        _staggered(tm // sub, matmul, rest)

    row = lambda w: pl.BlockSpec((tm, w), lambda i: (i, 0))
    full = lambda a, b: pl.BlockSpec((a, b), lambda i: (0, 0))
    return pl.pallas_call(
        body,
        grid=(s // tm,),
        in_specs=[row(D_FF), full(D_FF, D_MODEL), row(D_MODEL), full(1, D_MODEL), row(D_MODEL)],
        out_specs=[row(D_MODEL), full(1, D_MODEL), full(1, D_MODEL)],
        out_shape=[SDS((s, D_MODEL), F32), SDS((1, D_MODEL), F32), SDS((1, D_MODEL), F32)],
        compiler_params=_cparams("arbitrary"),
        name="down_final",
    )(act, wd, x1, gf, tgt)


def _swiglu_bwd(dx2, wd, gate, up, *, tm, tn):
    s = dx2.shape[0]

    def body(d_ref, w_ref, gate_ref, up_ref, dgate_ref, dup_ref):
        dact = lax.dot_general(d_ref[...].astype(BF16), w_ref[...], NT, preferred_element_type=F32)
        gate = gate_ref[...].astype(F32)
        sg = jax.nn.sigmoid(gate)
        dup_ref[...] = (dact * (gate * sg)).astype(BF16)
        dgate_ref[...] = (dact * up_ref[...].astype(F32) * (sg * (1.0 + gate * (1.0 - sg)))).astype(BF16)

    ospec = pl.BlockSpec((tm, tn), lambda c, r: (r, c))
    return pl.pallas_call(
        body,
        grid=(D_FF // tn, s // tm),
        in_specs=[pl.BlockSpec((tm, D_MODEL), lambda c, r: (r, 0)), pl.BlockSpec((tn, D_MODEL), lambda c, r: (c, 0)),
                  ospec, ospec],
        out_specs=[ospec, ospec],
        out_shape=[SDS((s, D_FF), BF16), SDS((s, D_FF), BF16)],
        compiler_params=_cparams("arbitrary", "arbitrary"),
        name="swiglu_bwd",
    )(dx2, wd, gate, up)


def _mm_tn_stacked(as_, rows, b, *, ts, name):
    s, nb_ = b.shape
    n = len(as_)
    offsets = [sum(rows[:i]) for i in range(n)]

    def body(*refs):
        a_refs, b_ref, o_ref, acc_ref = refs[:n], refs[n], refs[n + 1], refs[n + 2]
        k = pl.program_id(0)

        @pl.when(k == 0)
        def _():
            acc_ref[...] = jnp.zeros_like(acc_ref)

        bv = b_ref[...].astype(BF16)
        for a_ref, off, cnt in zip(a_refs, offsets, rows):
            part = lax.dot_general(a_ref[...].astype(BF16), bv, TN, preferred_element_type=F32)
            acc_ref[off:off + cnt, :] += part[0:cnt, :]

        @pl.when(k == s // ts - 1)
        def _():
            o_ref[...] = acc_ref[...].astype(BF16)

    return pl.pallas_call(
        body,
        grid=(s // ts,),
        in_specs=[pl.BlockSpec((ts, a.shape[1]), lambda k: (k, 0)) for a in as_] + [pl.BlockSpec((ts, nb_), lambda k: (k, 0))],
        out_specs=pl.BlockSpec((sum(rows), nb_), lambda k: (0, 0)),
        out_shape=SDS((sum(rows), nb_), BF16),
        scratch_shapes=[pltpu.VMEM((sum(rows), nb_), F32)],
        compiler_params=_cparams("arbitrary"),
        name=name,
    )(*as_, b)


def _norm_bwd(dh, x, r, g, dres):
    xn = x * r
    dxn = dh * g
    dx = dres + r * (dxn - xn * jnp.mean(dxn * xn, axis=-1, keepdims=True))
    return dx, jnp.sum(dh * xn, axis=0, keepdims=True)


def _mlp_in_pool_bwd(dgate, dup, wg_t, wu_t, w_out, x1, r2, g2, dx2, pooled, w_pool, pool_scale, *, tm):
    s = x1.shape[0]
    nt = s // tm
    ng = len(POOL_WINDOWS)

    def body(dg_ref, dup_ref, wg_ref, wu_ref, wo_ref, x_ref, r_ref, g_ref, d_ref, p_ref, w_ref, sc_ref,
             dx1_ref, dattn_ref, du_ref, dg2_ref, dw_ref, dsc_ref, head_ref):
        i = pl.program_id(0)

        @pl.when(i == 0)
        def _():
            dg2_ref[...] = jnp.zeros_like(dg2_ref)
            head_ref[...] = jnp.zeros_like(head_ref)
            dw_ref[...] = jnp.zeros_like(dw_ref)
            dsc_ref[...] = jnp.zeros_like(dsc_ref)

        dh2 = (jnp.dot(dg_ref[...], wg_ref[...], preferred_element_type=F32)
               + jnp.dot(dup_ref[...], wu_ref[...], preferred_element_type=F32))
        dx1, dg2 = _norm_bwd(dh2, x_ref[...], r_ref[...], g_ref[...], d_ref[...])
        dg2_ref[...] += dg2
        dx1_ref[...] = dx1
        dmix = lax.dot_general(dx1.astype(BF16), wo_ref[...], NT, preferred_element_type=F32)
        dattn_ref[...] = dmix[:, 0:ATTN_W]
        row0 = (nt - 1 - i) * tm
        for g, w in enumerate(POOL_WINDOWS):
            cols = slice(g * POOL_G, (g + 1) * POOL_G)
            wb = w_ref[g].astype(BF16)
            pooled_g = p_ref[:, cols]
            dpo = dmix[:, ATTN_W + g * POOL_G:ATTN_W + (g + 1) * POOL_G]
            mixed = jnp.dot(pooled_g, wb, preferred_element_type=F32)
            dsc_ref[:, cols] += jnp.sum(dpo * mixed, axis=0, keepdims=True)
            dmp = (dpo * sc_ref[:, cols]).astype(BF16)
            dw_ref[g] += lax.dot_general(pooled_g, dmp, TN, preferred_element_type=F32)
            dpooled = lax.dot_general(dmp, wb, NT, preferred_element_type=F32)
            a = dpooled / _pool_counts(row0, tm, w)
            acc = jnp.concatenate([a, head_ref[:, cols]], axis=0)
            head_ref[:, cols] = a[0:HALO, :]
            k = 1
            while k < w:
                acc = acc + pltpu.roll(acc, tm + HALO - k, axis=0)
                k *= 2
            du_ref[:, cols] = (acc[0:tm, :] - dpooled).astype(BF16)

    row = lambda w: pl.BlockSpec((tm, w), lambda i: (nt - 1 - i, 0))
    full = lambda a, b: pl.BlockSpec((a, b), lambda i: (0, 0))
    pool_w = pl.BlockSpec((ng, POOL_G, POOL_G), lambda i: (0, 0, 0))
    return pl.pallas_call(
        body,
        grid=(nt,),
        in_specs=[row(D_FF), row(D_FF), full(D_FF, D_MODEL), full(D_FF, D_MODEL), full(D_MODEL, D_MODEL),
                  row(D_MODEL), row(1), full(1, D_MODEL), row(D_MODEL), row(POOL_W), pool_w, full(1, POOL_W)],
        out_specs=[row(D_MODEL), row(ATTN_W), row(POOL_W), full(1, D_MODEL), pool_w, full(1, POOL_W)],
        out_shape=[SDS((s, D_MODEL), F32), SDS((s, ATTN_W), F32), SDS((s, POOL_W), BF16), SDS((1, D_MODEL), F32),
                   SDS((ng, POOL_G, POOL_G), F32), SDS((1, POOL_W), F32)],
        scratch_shapes=[pltpu.VMEM((HALO, POOL_W), F32)],
        compiler_params=_cparams("arbitrary"),
        name="mlp_in_pool_bwd",
    )(dgate, dup, wg_t, wu_t, w_out, x1, r2, g2, dx2, pooled, w_pool, pool_scale)


SUM_ROWS = 16


def _heads_t(t):
    n = t.shape[0]
    lane = _iota2((n, LANES), 1)
    tf = t.astype(F32)
    halves = jnp.concatenate([jnp.where(lane < HEAD_DIM, tf, 0.0).T, jnp.where(lane < HEAD_DIM, 0.0, tf).T], axis=1)
    r, c = _iota2((SUM_ROWS, 2 * n), 0), _iota2((SUM_ROWS, 2 * n), 1)
    ones = jnp.where(((r == 0) & (c < n)) | ((r == 4) & (c >= n)), 1.0, 0.0)
    return jnp.concatenate([halves, ones], axis=0).astype(BF16)


def _attn_bwd(qkv, attn_o, d_attn, rowb, ck_col, *, tq):
    s = qkv.shape[0]
    tk = tq
    nb = s // tq
    rows_t = LANES + SUM_ROWS

    def body(q_ref, k_ref, v_ref, o_ref, do_ref, rowb_ref, ck_ref, dq_ref, dk_ref, dv_ref, dck_ref, dcq_ref,
             dqt_ref, delta_ref, kp_ref, qp_ref, dob_ref, qt_ref, kt_ref, dot_ref, front_ref):
        lane = _iota2((tq, LANES), 1)
        lo = lane < HEAD_DIM
        first = _iota2((8, LANES), 1) < HEAD_DIM
        sel = jnp.where(_iota2((8, LANES), 0) < 4, jnp.where(first, 1.0, 0.0), jnp.where(first, 0.0, 1.0))

        def prep(b, _):
            st = pl.multiple_of(b * tq, tq)
            do2 = do_ref[pl.ds(st, tq), :]
            delta_ref[b] = _sel_dot(sel, do2 * o_ref[pl.ds(st, tq), :].astype(F32), NT)
            dob_ref[pl.ds(st, tq), :] = do2.astype(BF16)
            dqt_ref[b] = jnp.zeros((rows_t, tq), F32)
            k2 = k_ref[pl.ds(st, tq), :].astype(F32)
            q2 = q_ref[pl.ds(st, tq), :].astype(F32)
            ck = ck_ref[pl.ds(st, tq), :]
            for h in range(2):
                kp_ref[h * nb + b] = _augment(k2, h, -ck[:, h:h + 1], True)
                qp_ref[h * nb + b] = _augment(q2 * Q_SCALE, h, jnp.zeros((tq, 1), F32), False)
            qt_ref[b] = _heads_t(q2)
            kt_ref[b] = _heads_t(k2)
            dot_ref[b] = _heads_t(do2)[0:LANES, :]
            return 0

        lax.fori_loop(0, nb, prep, 0)

        def split(t):
            z = jnp.zeros_like(t)
            return jnp.where(lo, t, z), jnp.where(lo, z, t)

        def kv_block(j, _):
            st_j = pl.multiple_of(j * tk, tk)
            vs = split(v_ref[pl.ds(st_j, tk), :])
            kt = kt_ref[j]

            def stage(i, slot):
                ic = jnp.minimum(i, nb - 1)
                do2 = dob_ref[pl.ds(pl.multiple_of(ic * tq, tq), tq), :]
                for h in range(2):
                    front_ref[4 * slot + h] = lax.dot_general(kp_ref[h * nb + j], qp_ref[h * nb + ic], NT,
                                                              preferred_element_type=F32)
                    front_ref[4 * slot + 2 + h] = lax.dot_general(vs[h], do2, NT, preferred_element_type=F32)

            def q_block(i, slot, carry, diagonal):
                dkt, dvt = carry
                ic = jnp.minimum(i, nb - 1)
                rb = rowb_ref[ic] + jnp.where(i < nb, 0.0, NEG)
                dl = delta_ref[ic]
                pts, dsts = [], []
                for h in range(2):
                    st = front_ref[4 * slot + h] + rb[h:h + 1, :]
                    if diagonal:
                        st = jnp.where(_iota2((tk, tq), 0) <= _iota2((tk, tq), 1), st, NEG)
                    pt = jnp.exp(st)
                    pts.append(pt.astype(BF16))
                    dsts.append((pt * (front_ref[4 * slot + 2 + h] - dl[4 * h:4 * h + 1, :])).astype(BF16))
                dvt = dvt + lax.dot_general(dot_ref[ic], jnp.concatenate(pts, axis=1), NT, preferred_element_type=F32)
                dkt = dkt + lax.dot_general(qt_ref[ic], jnp.concatenate(dsts, axis=1), NT, preferred_element_type=F32)
                dqt_ref[ic] += jnp.dot(kt, jnp.concatenate(dsts, axis=0), preferred_element_type=F32)
                return dkt, dvt

            def pair(t, carry):
                i0 = j + 1 + 2 * t
                stage(i0 + 1, 0)
                carry = q_block(i0, 1, carry, False)
                stage(i0 + 2, 1)
                return q_block(i0 + 1, 0, carry, False)

            stage(j, 0)
            stage(j + 1, 1)
            carry = q_block(j, 0, (jnp.zeros((rows_t, tk), F32), jnp.zeros((LANES, tk), F32)), True)
            dkt, dvt = lax.fori_loop(0, lax.shift_right_logical(nb - j, 1), pair, carry)
            dk_ref[pl.ds(st_j, tk), :] = (dkt[0:LANES, :].T * Q_SCALE).astype(BF16)
            dv_ref[pl.ds(st_j, tk), :] = dvt.T.astype(BF16)
            dck_ref[j] = dkt[LANES:LANES + 8, :]
            return 0

        lax.fori_loop(0, nb, kv_block, 0)

        def finish(b, _):
            acc = dqt_ref[b]
            dq_ref[pl.ds(pl.multiple_of(b * tq, tq), tq), :] = (acc[0:LANES, :].T * Q_SCALE).astype(BF16)
            dcq_ref[b] = acc[LANES:LANES + 8, :]
            return 0

        lax.fori_loop(0, nb, finish, 0)

    col = lambda off: pl.BlockSpec((s, LANES), lambda p: (0, off + p))
    sums = pl.BlockSpec((None, nb, 8, tq), lambda p: (p, 0, 0, 0))
    return pl.pallas_call(
        body,
        grid=(N_PAIRS,),
        in_specs=[col(0), col(N_PAIRS), col(2 * N_PAIRS), col(0), col(0),
                  pl.BlockSpec((None, nb, 2, tq), lambda p: (p, 0, 0, 0)),
                  pl.BlockSpec((None, s, 2), lambda p: (p, 0, 0))],
        out_specs=[col(0), col(0), col(0), sums, sums],
        out_shape=[SDS((s, ATTN_W), BF16), SDS((s, ATTN_W), BF16), SDS((s, ATTN_W), BF16),
                   SDS((N_PAIRS, nb, 8, tq), F32), SDS((N_PAIRS, nb, 8, tq), F32)],
        scratch_shapes=[pltpu.VMEM((nb, rows_t, tq), F32), pltpu.VMEM((nb, 8, tq), F32),
                        pltpu.VMEM((2 * nb, tk, LANES), BF16), pltpu.VMEM((2 * nb, tq, LANES), BF16),
                        pltpu.VMEM((s, LANES), BF16), pltpu.VMEM((nb, rows_t, 2 * tq), BF16),
                        pltpu.VMEM((nb, rows_t, 2 * tk), BF16), pltpu.VMEM((nb, LANES, 2 * tq), BF16),
                        pltpu.VMEM((8, tk, tq), F32)],
        compiler_params=_cparams("arbitrary"),
        name="attn_bwd",
    )(qkv, qkv, qkv, attn_o, d_attn, rowb, ck_col)


def _forget_bwd(dc_t, fl_t, b_rows):
    rows = fl_t.shape[0]
    nb = rows // N_HEADS

    def body(dc_ref, fl_ref, b_ref, dfl_ref, db_ref):
        dc = dc_ref[...]
        lower = _iota2((LANES, LANES), 0) >= _iota2((LANES, LANES), 1)
        ones = jnp.ones((LANES, LANES), F32)
        rr, cc, same = _head_block_masks(rows, nb)
        dlf = _dot_sel(dc, lower) + _sel_dot(same & (cc > rr), _dot_sel(dc, ones))
        dfl = dlf / (1.0 + jnp.exp(fl_ref[...] + b_ref[...]))
        dfl_ref[...] = dfl
        shift = nb.bit_length() - 1
        hsel = lax.shift_right_logical(_iota2((N_HEADS, rows), 1), shift) == _iota2((N_HEADS, rows), 0)
        db_ref[...] = _sel_dot(hsel, _dot_sel(dfl, ones))

    return pl.pallas_call(body, out_shape=[SDS(fl_t.shape, F32), SDS((N_HEADS, LANES), F32)],
                          compiler_params=_cparams(), name="forget_bwd")(dc_t, fl_t, b_rows)


def _in_bwd(dq, dk, dv, du, dfl, w_in_t, x, r1, g1, dx1, *, tm):
    s = x.shape[0]
    pieces = ((0, ATTN_W), (ATTN_W, 2 * ATTN_W), (2 * ATTN_W, QKV_W), (U_OFF, F_OFF), (F_OFF, IN_PAD))

    def body(dq_ref, dk_ref, dv_ref, du_ref, df_ref, w_ref, x_ref, r_ref, g_ref, d_ref, dx_ref, dg1_ref):
        @pl.when(pl.program_id(0) == 0)
        def _():
            dg1_ref[...] = jnp.zeros_like(dg1_ref)

        dh = None
        for ref, (c0, c1) in zip((dq_ref, dk_ref, dv_ref, du_ref, df_ref), pieces):
            t = jnp.dot(ref[...], w_ref[c0:c1, :], preferred_element_type=F32)
            dh = t if dh is None else dh + t
        dx, dg1 = _norm_bwd(dh, x_ref[...], r_ref[...], g_ref[...], d_ref[...])
        dx_ref[...] = dx
        dg1_ref[...] += dg1

    row = lambda w: pl.BlockSpec((tm, w), lambda i: (i, 0))
    full = lambda a, b: pl.BlockSpec((a, b), lambda i: (0, 0))
    return pl.pallas_call(
        body,
        grid=(s // tm,),
        in_specs=[row(ATTN_W), row(ATTN_W), row(ATTN_W), row(POOL_W), row(LANES), full(IN_PAD, D_MODEL),
                  row(D_MODEL), row(1), full(1, D_MODEL), row(D_MODEL)],
        out_specs=[row(D_MODEL), full(1, D_MODEL)],
        out_shape=[SDS((s, D_MODEL), F32), SDS((1, D_MODEL), F32)],
        compiler_params=_cparams("arbitrary"),
        name="in_bwd",
    )(dq, dk, dv, du, dfl, w_in_t, x, r1, g1, dx1)


def _tiles(s):
    big = min(512, s)
    return dict(row=big, attn=min(256, s // 2), ff_rows=min(256, s), tall=min(1024, s))


def _tie(a, token):
    return a + token[0:1, 0:1].astype(a.dtype)


def _local_step(x, tgt, p, weight, emit, started):
    s = x.shape[0]
    t = _tiles(s)
    tm, tq = t["row"], t["attn"]
    nb = s // LANES
    nqb = s // tq
    g1, g2, gf = p["norm1_g"], p["norm2_g"], p["final_g"].reshape(1, D_MODEL)
    w_pool, pool_scale = p["w_pool"][0], p["pool_scale"]

    h, r1 = _norm1(x, _tie(g1, started), tm=tm)
    w_in_t = weight("w_in", h)
    qkv, fl, pooled, pool_o = _in_proj_pool(h, w_in_t, w_pool, pool_scale, tm=t["tall"])
    fl_t = fl[:, :N_HEADS].T.reshape(N_HEADS * nb, LANES)
    b_rows = jnp.repeat(p["b_forget"].reshape(N_HEADS), nb).reshape(N_HEADS * nb, 1)
    c = _forget_cumsum(fl_t, b_rows).reshape(N_PAIRS, 2, s)
    c_col = c.transpose(0, 2, 1)
    c_rowblk = c.reshape(N_PAIRS, 2, nqb, tq).transpose(0, 2, 1, 3)
    attn_o, lse = _attn_fwd(qkv, c_col, tk=tq)
    lse = lse.reshape(N_PAIRS, nqb // 2, 2, 2, tq).transpose(0, 1, 3, 2, 4).reshape(N_PAIRS, nqb, 2, tq)
    w_out = weight("w_out", attn_o)
    wg_t, wu_t = weight("w_gate_up", attn_o)
    x1, h2, r2, gate, up, act = _out_gate_up(attn_o, pool_o, w_out, x, g2, wg_t, wu_t, tm=t["ff_rows"])
    wd = weight("w_down", act)
    dx2, loss_row, d_gf = _down_final(act, wd, x1, gf, tgt, tm=tm, sub=min(128, tm))

    dgate, dup = _swiglu_bwd(dx2, wd, gate, up, tm=t["ff_rows"], tn=D_FF)
    d_wd = _mm_tn_stacked([act], [D_FF], dx2, ts=t["tall"], name="grad_w_down")
    d_wg_t = _mm_tn_stacked([dgate], [D_FF], h2, ts=t["tall"], name="grad_w_gate")
    d_wu_t = _mm_tn_stacked([dup], [D_FF], h2, ts=t["tall"], name="grad_w_up")
    dx1, d_attn, du, d_g2, d_wpool, d_pscale = _mlp_in_pool_bwd(dgate, dup, wg_t, wu_t, w_out, x1, r2, g2, dx2, pooled,
                                                               w_pool, pool_scale, tm=t["ff_rows"])
    d_wo = _mm_tn_stacked([attn_o, pool_o], [ATTN_W, POOL_W], dx1, ts=t["tall"], name="grad_w_out")
    token = emit(("w_down", "w_gate", "w_up", "w_out"), (d_wd, d_wg_t, d_wu_t, d_wo))
    rowb = _tie(c_rowblk - lse, token)
    dq, dk, dv, dck, dcq = _attn_bwd(qkv, attn_o, d_attn, rowb, c_col, tq=tq)
    dc_t = (dcq - dck)[:, :, 0::4, :].transpose(0, 2, 1, 3).reshape(N_HEADS * nb, LANES)
    dfl_t, db = _forget_bwd(dc_t, fl_t, b_rows)
    dfl = jnp.pad(dfl_t.reshape(N_HEADS, s).T, ((0, 0), (0, LANES - N_HEADS))).astype(BF16)
    d_w_in_t = _mm_tn_stacked([dq, dk, dv, dfl, du], [ATTN_W, ATTN_W, ATTN_W, N_HEADS, POOL_W], h, ts=t["tall"],
                              name="grad_w_in")
    token = emit(("w_in",), (d_w_in_t,))
    dx, d_g1 = _in_bwd(dq, dk, dv, du, dfl, w_in_t, x, r1, _tie(g1, token), dx1, tm=tm)

    small = dict(norm1_g=d_g1, b_forget=db[:, 0].reshape(1, N_HEADS), w_pool=d_wpool, pool_scale=d_pscale,
                 norm2_g=d_g2, final_g=d_gf)
    return loss_row, dx, small


def _my_index():
    return 4 * lax.axis_index("x") + 2 * lax.axis_index("y") + lax.axis_index("c")


def _peer(k):
    pos = [lax.axis_index(a) for a in ("x", "y", "c")]
    flipped = tuple(1 - p if (k >> b) & 1 else p for p, b in zip(pos, (2, 1, 0)))
    return flipped, 4 * flipped[0] + 2 * flipped[1] + flipped[2]


_HBM = pl.BlockSpec(memory_space=pltpu.HBM)
_SEM = pl.BlockSpec(memory_space=pltpu.SEMAPHORE)
_DATAFLOW = pltpu.SideEffectType.DATAFLOW_SIDE_EFFECTING


ALL_PEERS = tuple(range(1, N_DEV))
SAME_CORE = (1, 2, 4, 6)


def _peer_copies(ins, lands, send_sems, recv_sems, scatter, peers, arrivals):
    me = _my_index()
    copies = []
    for w in range(len(ins)):
        for k in peers[w]:
            dev, idx = _peer(k)
            copies.append(pltpu.make_async_remote_copy(
                src_ref=ins[w].at[idx] if scatter[w] else ins[w], dst_ref=lands[w].at[idx if arrivals else me],
                send_sem=send_sems[w].at[k - 1], recv_sem=recv_sems[w].at[k - 1], device_id=dev, device_id_type=MESH))
    return copies


def _own_copies(ins, lands, send_sems, scatter):
    me = _my_index()
    return [pltpu.make_async_copy(ins[w].at[me] if scatter[w] else ins[w], lands[w].at[me], send_sems[w].at[N_DEV - 1])
            for w in range(len(ins))]


def _forward_copies(land, send_sems, recv_sems, arrivals):
    sibling, _ = _peer(1)
    copies = []
    for j, k in enumerate(SAME_CORE[1:]):
        src, dst = _peer(k)[1], _peer(k ^ 1 if arrivals else k)[1]
        copies.append(pltpu.make_async_remote_copy(
            src_ref=land.at[src], dst_ref=land.at[dst], send_sem=send_sems.at[j], recv_sem=recv_sems.at[j],
            device_id=sibling, device_id_type=MESH))
    return copies


def _forward_start(land, name):
    def body(land_ref, send_sems, recv_sems, land_thru, token):
        for cp in _forward_copies(land_ref, send_sems, recv_sems, False):
            cp.start()
        token[...] = jnp.zeros_like(token)

    sem = pltpu.SemaphoreType.DMA((len(SAME_CORE) - 1,))
    send, recv, thru, _ = pl.pallas_call(
        body,
        in_specs=[_HBM],
        out_specs=[_SEM, _SEM, _HBM, pl.BlockSpec(memory_space=pltpu.VMEM)],
        out_shape=[sem, sem, pltpu.HBM(land.shape, land.dtype), SDS((8, LANES), F32)],
        input_output_aliases={0: 2},
        compiler_params=pltpu.CompilerParams(has_side_effects=_DATAFLOW),
        name=name,
    )(land)
    return send, recv, thru


def _forward_wait(handle, after, name):
    def body(land_ref, send_sems, recv_sems, after_ref, land_out):
        for cp in _forward_copies(land_ref, send_sems, recv_sems, False):
            cp.wait_send()
        for cp in _forward_copies(land_ref, send_sems, recv_sems, True):
            cp.wait_recv()

    send, recv, land = handle
    return pl.pallas_call(
        body,
        in_specs=[_HBM, _SEM, _SEM, pl.BlockSpec(memory_space=pl.ANY)],
        out_specs=_HBM,
        out_shape=pltpu.HBM(land.shape, land.dtype),
        input_output_aliases={0: 0},
        compiler_params=pltpu.CompilerParams(has_side_effects=_DATAFLOW),
        name=name,
    )(land, send, recv, after)


def _exchange_start(arrays, scatter, name, peers=None):
    n = len(arrays)
    peers = peers or [ALL_PEERS] * n
    land_shapes = [(N_DEV,) + tuple(a.shape[1:] if sc else a.shape) for a, sc in zip(arrays, scatter)]

    def body(*refs):
        ins, lands = refs[:n], refs[n:2 * n]
        send_sems, recv_sems = refs[2 * n:3 * n], refs[3 * n:4 * n]
        token = refs[6 * n]
        for cp in _peer_copies(ins, lands, send_sems, recv_sems, scatter, peers, False):
            cp.start()
        for cp in _own_copies(ins, lands, send_sems, scatter):
            cp.start()
        token[...] = jnp.zeros_like(token)

    sends, recvs = pltpu.SemaphoreType.DMA((N_DEV,)), pltpu.SemaphoreType.DMA((N_DEV - 1,))
    outs = pl.pallas_call(
        body,
        in_specs=[_HBM] * (2 * n),
        out_specs=[_SEM] * (2 * n) + [_HBM] * (2 * n) + [pl.BlockSpec(memory_space=pltpu.VMEM)],
        out_shape=[sends] * n + [recvs] * n + [pltpu.HBM(a.shape, a.dtype) for a in arrays]
        + [pltpu.HBM(sh, a.dtype) for sh, a in zip(land_shapes, arrays)] + [SDS((8, LANES), F32)],
        input_output_aliases={i: 2 * n + i for i in range(2 * n)},
        compiler_params=pltpu.CompilerParams(has_side_effects=_DATAFLOW),
        name=name,
    )(*[pltpu.with_memory_space_constraint(a, pltpu.HBM) for a in arrays],
      *[pltpu.with_memory_space_constraint(lax.empty(sh, a.dtype), pltpu.HBM) for sh, a in zip(land_shapes, arrays)])
    handles = [dict(send=outs[w], recv=outs[n + w], src=outs[2 * n + w], land=outs[3 * n + w], scatter=scatter[w],
                    peers=peers[w]) for w in range(n)]
    return handles, outs[4 * n]


def _exchange_wait(handles, after, name):
    n = len(handles)
    scatter, peers = [h["scatter"] for h in handles], [h["peers"] for h in handles]

    def body(*refs):
        ins, lands = refs[:n], refs[n:2 * n]
        send_sems, recv_sems = refs[2 * n:3 * n], refs[3 * n:4 * n]
        for cp in _peer_copies(ins, lands, send_sems, recv_sems, scatter, peers, False):
            cp.wait_send()
        for cp in _peer_copies(ins, lands, send_sems, recv_sems, scatter, peers, True):
            cp.wait_recv()
        for cp in _own_copies(ins, lands, send_sems, scatter):
            cp.wait()

    srcs, lands = [h["src"] for h in handles], [h["land"] for h in handles]
    outs = pl.pallas_call(
        body,
        in_specs=[_HBM] * (2 * n) + [_SEM] * (2 * n) + [pl.BlockSpec(memory_space=pl.ANY)],
        out_specs=[_HBM] * (2 * n),
        out_shape=[pltpu.HBM(a.shape, a.dtype) for a in srcs + lands],
        input_output_aliases={i: i for i in range(2 * n)},
        compiler_params=pltpu.CompilerParams(has_side_effects=_DATAFLOW),
        name=name,
    )(*srcs, *lands, *[h["send"] for h in handles], *[h["recv"] for h in handles], after)
    return outs[n:]


def _adamw(parts, w, m, v, name):
    rows, cols = w.shape
    tr = rows // 4 if rows % 32 == 0 else rows

    def body(p_ref, w_ref, m_ref, v_ref, g_ref, d_ref, mo_ref, vo_ref):
        g = p_ref[0].astype(F32)
        for d in range(1, N_DEV):
            g = g + p_ref[d].astype(F32)
        g_ref[...] = g
        d_ref[...], mo_ref[...], vo_ref[...] = _adam_update(g, w_ref[...], m_ref[...], v_ref[...])

    blk = pl.BlockSpec((tr, cols), lambda i: (i, 0))
    return pl.pallas_call(
        body,
        grid=(rows // tr,),
        in_specs=[pl.BlockSpec((N_DEV, tr, cols), lambda i: (0, i, 0)), blk, blk, blk],
        out_specs=[blk] * 4,
        out_shape=[SDS((rows, cols), F32)] * 4,
        compiler_params=_cparams("arbitrary"),
        name=name,
    )(parts, w, m, v)


_ROW_OF = dict(norm1_g=(0, D_MODEL), norm2_g=(1, D_MODEL), final_g=(2, D_MODEL), pool_scale=(3, POOL_W),
               b_forget=(4, N_HEADS), loss=(5, 1))


def _pack_rows(vals):
    rows = [jnp.pad(vals[n].reshape(1, width).astype(F32), ((0, 0), (0, D_MODEL - width)))
            for n, (_, width) in sorted(_ROW_OF.items(), key=lambda kv: kv[1][0])]
    return jnp.concatenate(rows + [jnp.zeros((8 - len(rows), D_MODEL), F32)], axis=0)


def _adam_update(g, w, m, v):
    m_new = ADAM_B1 * m + (1.0 - ADAM_B1) * g
    v_new = ADAM_B2 * v + (1.0 - ADAM_B2) * (g * g)
    m_hat = m_new / (1.0 - ADAM_B1 ** ADAM_STEP)
    v_hat = v_new / (1.0 - ADAM_B2 ** ADAM_STEP)
    return -ADAM_LR * (m_hat / (jnp.sqrt(v_hat) + ADAM_EPS) + ADAM_WD * w), m_new, v_new


def _adamw_replicated(parts_rows, parts_pool, w, m, v):
    names = ("norm1_g", "norm2_g", "final_g", "pool_scale", "b_forget", "w_pool")
    shapes = {n: ((len(POOL_WINDOWS), POOL_G, POOL_G) if n == "w_pool" else (1, _ROW_OF[n][1])) for n in names}

    def body(rows_ref, pool_ref, *refs):
        ins, outs = refs[:3 * len(names)], refs[3 * len(names):]

        def total(n):
            if n == "w_pool":
                pieces = [pool_ref[d] for d in range(N_DEV)]
            else:
                row, width = _ROW_OF[n]
                pieces = [rows_ref[d, row:row + 1, 0:width] for d in range(N_DEV)]
            g = pieces[0]
            for p in pieces[1:]:
                g = g + p
            return g

        outs[0][...] = total("loss")
        for k, n in enumerate(names):
            g = total(n)
            delta, m_new, v_new = _adam_update(g, ins[3 * k][...], ins[3 * k + 1][...], ins[3 * k + 2][...])
            for o_ref, val in zip(outs[1 + 4 * k:5 + 4 * k], (g, delta, m_new, v_new)):
                o_ref[...] = val

    args = [d[n].reshape(shapes[n]) for n in names for d in (w, m, v)]
    res = pl.pallas_call(
        body,
        out_shape=[SDS((1, 1), F32)] + [SDS(shapes[n], F32) for n in names for _ in range(4)],
        compiler_params=_cparams(),
        name="adamw_replicated",
    )(parts_rows, parts_pool, *args)
    return res[0], {n: [r.reshape(w[n].shape) for r in res[1 + 4 * k:5 + 4 * k]] for k, n in enumerate(names)}


def kernel(x, norm1_g, w_in, b_forget, w_pool, pool_scale, w_out, norm2_g, w_gate, w_up, w_down, final_g, loss_target, m_norm1_g, m_w_in, m_b_forget, m_w_pool, m_pool_scale, m_w_out, m_norm2_g, m_w_gate, m_w_up, m_w_down, m_final_g, v_norm1_g, v_w_in, v_b_forget, v_w_pool, v_pool_scale, v_w_out, v_norm2_g, v_w_gate, v_w_up, v_w_down, v_final_g):
    big = ("w_in", "w_out", "w_gate", "w_up", "w_down")
    order = ("norm1_g", "w_in", "b_forget", "w_pool", "pool_scale", "w_out", "norm2_g", "w_gate", "w_up", "w_down",
             "final_g")
    w = dict(norm1_g=norm1_g, w_in=w_in, b_forget=b_forget, w_pool=w_pool, pool_scale=pool_scale, w_out=w_out,
             norm2_g=norm2_g, w_gate=w_gate, w_up=w_up, w_down=w_down, final_g=final_g)
    m = dict(norm1_g=m_norm1_g, w_in=m_w_in, b_forget=m_b_forget, w_pool=m_w_pool, pool_scale=m_pool_scale,
             w_out=m_w_out, norm2_g=m_norm2_g, w_gate=m_w_gate, w_up=m_w_up, w_down=m_w_down, final_g=m_final_g)
    v = dict(norm1_g=v_norm1_g, w_in=v_w_in, b_forget=v_b_forget, w_pool=v_w_pool, pool_scale=v_pool_scale,
             w_out=v_w_out, norm2_g=v_norm2_g, w_gate=v_w_gate, w_up=v_w_up, w_down=v_w_down, final_g=v_final_g)

    flipped = ("w_in", "w_gate", "w_up")
    shard = lambda d, n: d[n][0].T if n in flipped else d[n][0]
    gather, started = _exchange_start([shard(w, n).astype(BF16) for n in big], [False] * len(big), "gather_start",
                                      peers=[SAME_CORE if n == "w_in" else ALL_PEERS for n in big])
    gather = dict(zip(big, gather))

    def gathered(names, after):
        return _exchange_wait([gather[n] for n in names], after, "gather_wait_" + names[0])

    def weight(name, after):
        if name == "w_in":
            forward = _forward_start(gathered(["w_in"], after)[0], "gather_forward_start")
            full = _forward_wait(forward, after, "gather_forward_wait").reshape(IN_W, D_MODEL)
            f0 = QKV_W + N_HEADS
            return jnp.concatenate([full[:QKV_W], full[f0:], full[QKV_W:f0],
                                    jnp.zeros((IN_PAD - IN_W, D_MODEL), BF16)], axis=0)
        if name == "w_out":
            return gathered(["w_out"], after)[0].reshape(D_MODEL, D_MODEL)
        if name == "w_gate_up":
            return [g.reshape(D_FF, D_MODEL) for g in gathered(["w_gate", "w_up"], after)]
        return gathered(["w_down"], after)[0].reshape(D_FF, D_MODEL)

    rows = lambda g: g.reshape(N_DEV, g.shape[0] // N_DEV, g.shape[1])
    sent = {}

    def emit(names, grads):
        handles, token = _exchange_start([rows(g) for g in grads], [True] * len(names), "grads_start_" + names[0])
        sent.update(zip(names, handles))
        return token

    loss_row, dx, small_grads = _local_step(x[0], loss_target[0], w, weight, emit, started)

    packed = _pack_rows(dict(small_grads, loss=0.5 / D_MODEL * jnp.sum(loss_row)))
    small_handles, after = _exchange_start([packed, small_grads["w_pool"]], [False, False], "grads_start_replicated")

    outs = {}
    for name in ("w_down", "w_gate", "w_up", "w_out", "w_in"):
        (parts,) = _exchange_wait([sent[name]], after, "grads_wait_" + name)
        outs[name] = _adamw(parts, shard(w, name), shard(m, name), shard(v, name), "adamw_" + name)
        after = outs[name][0]
        outs[name] = [(a.T if name in flipped else a)[None] for a in outs[name]]
    parts_rows, parts_pool = _exchange_wait(small_handles, after, "grads_wait_replicated")
    loss, small = _adamw_replicated(parts_rows, parts_pool, w, m, v)
    outs.update(small)

    return (loss.reshape(()), dx[None]) + tuple(outs[n][k] for k in range(4) for n in order)
```

```python
import jax
import jax.numpy as jnp
from jax import lax
from jax.experimental import pallas as pl
from jax.experimental.pallas import tpu as pltpu

F32 = jnp.float32
BF16 = jnp.bfloat16
SDS = jax.ShapeDtypeStruct

D_MODEL = 1024
ATTN_W = 512
N_HEADS = 8
HEAD_DIM = 64
Q_SCALE = HEAD_DIM ** -0.5
N_PAIRS = N_HEADS // 2
POOL_W = 512
POOL_WINDOWS = (2, 4, 8, 16)
POOL_G = 128
HALO = 16
IN_W = 3 * ATTN_W + N_HEADS + POOL_W
QKV_W = 3 * ATTN_W
U_OFF = QKV_W
F_OFF = QKV_W + POOL_W
IN_PAD = F_OFF + 128
D_FF = 2816
EPS = 1e-6
NEG = -1e30
N_DEV = 8
LANES = 128

ADAM_LR = 0.001
ADAM_B1 = 0.9
ADAM_B2 = 0.999
ADAM_EPS = 1e-08
ADAM_WD = 0.01
ADAM_STEP = 10

VMEM_LIMIT_BYTES = 56 * 1024 * 1024
MESH = pl.DeviceIdType.MESH
NT = (((1,), (1,)), ((), ()))
TN = (((0,), (0,)), ((), ()))


def _cparams(*sem):
    return pltpu.CompilerParams(dimension_semantics=sem or None, vmem_limit_bytes=VMEM_LIMIT_BYTES)


def _split3(a):
    hi = a.astype(BF16)
    r1 = a - hi.astype(F32)
    mid = r1.astype(BF16)
    lo = (r1 - mid.astype(F32)).astype(BF16)
    return hi, mid, lo


def _dot_sel(a, sel, dims=None):
    sb = sel.astype(BF16)
    if dims is None:
        return sum(jnp.dot(p, sb, preferred_element_type=F32) for p in _split3(a))
    return sum(lax.dot_general(p, sb, dims, preferred_element_type=F32) for p in _split3(a))


def _sel_dot(sel, a, dims=None):
    sb = sel.astype(BF16)
    if dims is None:
        return sum(jnp.dot(sb, p, preferred_element_type=F32) for p in _split3(a))
    return sum(lax.dot_general(sb, p, dims, preferred_element_type=F32) for p in _split3(a))


def _iota2(shape, dim):
    return lax.broadcasted_iota(jnp.int32, shape, dim)


def _norm1(x, g1, *, tm):
    s = x.shape[0]

    def body(x_ref, g_ref, h_ref, r_ref):
        xv = x_ref[...]
        r = lax.rsqrt(jnp.mean(xv * xv, axis=-1, keepdims=True) + EPS)
        h_ref[...] = (xv * r * g_ref[...]).astype(BF16)
        r_ref[...] = r

    row = lambda w: pl.BlockSpec((tm, w), lambda i: (i, 0))
    return pl.pallas_call(
        body,
        grid=(s // tm,),
        in_specs=[row(D_MODEL), pl.BlockSpec((1, D_MODEL), lambda i: (0, 0))],
        out_specs=[row(D_MODEL), row(1)],
        out_shape=[SDS((s, D_MODEL), BF16), SDS((s, 1), F32)],
        compiler_params=_cparams("arbitrary"),
        name="norm1",
    )(x, g1)


def _in_proj_pool(h, w_in_t, w_pool, pool_scale, *, tm):
    s = h.shape[0]

    def body(h_ref, w_ref, wp_ref, sc_ref, qkv_ref, fl_ref, pooled_ref, po_ref, tail_ref):
        i = pl.program_id(0)

        @pl.when(i == 0)
        def _():
            tail_ref[...] = jnp.zeros_like(tail_ref)

        hv = h_ref[...]
        uv = lax.dot_general(hv, w_ref[U_OFF:F_OFF, :], NT, preferred_element_type=F32)
        qkv_ref[...] = lax.dot_general(hv, w_ref[0:QKV_W, :], NT, preferred_element_type=F32).astype(BF16)
        fl_ref[...] = lax.dot_general(hv, w_ref[F_OFF:IN_PAD, :], NT, preferred_element_type=F32)
        ext = jnp.concatenate([tail_ref[...], uv], axis=0)
        tail_ref[...] = uv[tm - HALO:, :]
        for g, w in enumerate(POOL_WINDOWS):
            cols = slice(g * POOL_G, (g + 1) * POOL_G)
            acc = ext[:, cols]
            k = 1
            while k < w:
                acc = acc + pltpu.roll(acc, k, axis=0)
                k *= 2
            pooled = (acc[HALO:, :] / _pool_counts(i * tm, tm, w) - uv[:, cols]).astype(BF16)
            pooled_ref[:, cols] = pooled
            mixed = jnp.dot(pooled, wp_ref[g].astype(BF16), preferred_element_type=F32)
            po_ref[:, cols] = (mixed * sc_ref[:, cols]).astype(BF16)

    row = lambda w: pl.BlockSpec((tm, w), lambda i: (i, 0))
    return pl.pallas_call(
        body,
        grid=(s // tm,),
        in_specs=[row(D_MODEL), pl.BlockSpec((IN_PAD, D_MODEL), lambda i: (0, 0)),
                  pl.BlockSpec((len(POOL_WINDOWS), POOL_G, POOL_G), lambda i: (0, 0, 0)),
                  pl.BlockSpec((1, POOL_W), lambda i: (0, 0))],
        out_specs=[row(QKV_W), row(LANES), row(POOL_W), row(POOL_W)],
        out_shape=[SDS((s, QKV_W), BF16), SDS((s, LANES), F32), SDS((s, POOL_W), BF16), SDS((s, POOL_W), BF16)],
        scratch_shapes=[pltpu.VMEM((HALO, POOL_W), F32)],
        compiler_params=_cparams("arbitrary"),
        name="in_proj_pool",
    )(h, w_in_t, w_pool, pool_scale)


def _head_block_masks(rows, nb):
    shift = nb.bit_length() - 1
    rr, cc = _iota2((rows, rows), 0), _iota2((rows, rows), 1)
    same = lax.shift_right_logical(rr, shift) == lax.shift_right_logical(cc, shift)
    return rr, cc, same


def _forget_cumsum(fl_t, b_rows):
    rows = fl_t.shape[0]
    nb = rows // N_HEADS

    def body(fl_ref, b_ref, c_ref):
        z = fl_ref[...] + b_ref[...]
        lf = jnp.minimum(z, 0.0) - jnp.log1p(jnp.exp(-jnp.abs(z)))
        upper = _iota2((LANES, LANES), 0) <= _iota2((LANES, LANES), 1)
        within = _dot_sel(lf, upper)
        tot = _dot_sel(lf, jnp.ones((LANES, LANES), F32))
        rr, cc, same = _head_block_masks(rows, nb)
        c_ref[...] = within + _sel_dot(same & (cc < rr), tot)

    return pl.pallas_call(body, out_shape=SDS(fl_t.shape, F32), compiler_params=_cparams(), name="forget_cumsum")(
        fl_t, b_rows)


BIAS_LANES = 3


def _augment(t, h, col, col_first):
    n = t.shape[0]
    lane = _iota2((n, LANES), 1)
    own = (lane < HEAD_DIM) if h == 0 else (lane >= HEAD_DIM)
    b0 = HEAD_DIM if h == 0 else 0
    c0, o0 = (b0, b0 + BIAS_LANES) if col_first else (b0 + BIAS_LANES, b0)
    x = jnp.where(own, t, 0.0)
    for off, piece in enumerate(_split3(col)):
        x = jnp.where(lane == c0 + off, piece.astype(F32), x)
    x = jnp.where((lane >= o0) & (lane < o0 + BIAS_LANES), 1.0, x)
    return x.astype(BF16)


def _attn_fwd(qkv, c_col, *, tk):
    s = qkv.shape[0]
    tq = 2 * tk
    nb = s // tk

    def body(q_ref, k_ref, v_ref, cq_ref, ck_ref, o_ref, lse_ref, kp_ref, vt_ref, st_ref):
        i = pl.program_id(1)

        @pl.when(i == 0)
        def _():
            def prep(jb, _):
                st = pl.multiple_of(jb * tk, tk)
                k2 = k_ref[pl.ds(st, tk), :].astype(F32)
                ck = ck_ref[pl.ds(st, tk), :]
                for h in range(2):
                    kp_ref[h * nb + jb] = _augment(k2, h, -ck[:, h:h + 1], True)
                vt_ref[jb] = v_ref[pl.ds(st, tk), :].astype(F32).T.astype(BF16)
                return 0

            lax.fori_loop(0, nb, prep, 0)

        qs = q_ref[...].astype(F32) * Q_SCALE
        cq = cq_ref[...]
        qp = [_augment(qs, h, cq[:, h:h + 1], False) for h in range(2)]

        def logits(j):
            return tuple(lax.dot_general(kp_ref[h * nb + j], qp[h], NT, preferred_element_type=F32) for h in range(2))

        def softmax_pv(j, slot, stats, masked):
            out = []
            for h in range(2):
                m, l, acc = stats[h]
                st = st_ref[2 * slot + h]
                if masked:
                    st = jnp.where(j * tk + _iota2((tk, tq), 0) <= i * tq + _iota2((tk, tq), 1), st, NEG)
                m_new = jnp.maximum(m, jnp.max(st, axis=0, keepdims=True))
                alpha = jnp.exp(m - m_new)
                p = jnp.exp(st - m_new)
                l = alpha * l + jnp.sum(p, axis=0, keepdims=True)
                vt = vt_ref[j, h * HEAD_DIM:(h + 1) * HEAD_DIM, :]
                acc = alpha * acc + jnp.dot(vt, p.astype(BF16), preferred_element_type=F32)
                out.append((m_new, l, acc))
            return tuple(out)

        def put(slot, j):
            for h, st in enumerate(logits(j)):
                st_ref[2 * slot + h] = st

        def run(j0, steps, stats):
            for d in range(steps):
                put(1 - d % 2, j0 + d + 1)
                stats = softmax_pv(j0 + d, d % 2, stats, False)
            return stats

        init = tuple((jnp.full((1, tq), NEG, F32), jnp.zeros((1, tq), F32), jnp.zeros((HEAD_DIM, tq), F32))
                     for _ in range(2))
        put(0, 0)
        fours = lax.shift_right_logical(i, 1)
        stats = lax.fori_loop(0, fours, lambda t, st: run(4 * t, 4, st), init)
        stats = lax.fori_loop(0, i - 2 * fours, lambda t, st: run(4 * fours, 2, st), stats)
        put(1, 2 * i + 1)
        stats = softmax_pv(2 * i, 0, stats, True)
        (ma, la, acca), (mb, lb, accb) = softmax_pv(2 * i + 1, 1, stats, True)
        o_ref[...] = jnp.concatenate([acca / la, accb / lb], axis=0).T.astype(BF16)
        lse_ref[...] = jnp.where(_iota2((2, tq), 0) == 0, ma + jnp.log(la), mb + jnp.log(lb))

    return pl.pallas_call(
        body,
        grid=(N_PAIRS, s // tq),
        in_specs=[
            pl.BlockSpec((tq, LANES), lambda p, i: (i, p)),
            pl.BlockSpec((s, LANES), lambda p, i: (0, N_PAIRS + p)),
            pl.BlockSpec((s, LANES), lambda p, i: (0, 2 * N_PAIRS + p)),
            pl.BlockSpec((None, tq, 2), lambda p, i: (p, i, 0)),
            pl.BlockSpec((None, s, 2), lambda p, i: (p, 0, 0)),
        ],
        out_specs=[
            pl.BlockSpec((tq, LANES), lambda p, i: (i, p)),
            pl.BlockSpec((None, None, 2, tq), lambda p, i: (p, i, 0, 0)),
        ],
        out_shape=[SDS((s, ATTN_W), BF16), SDS((N_PAIRS, s // tq, 2, tq), F32)],
        scratch_shapes=[pltpu.VMEM((2 * nb, tk, LANES), BF16), pltpu.VMEM((nb, LANES, tk), BF16),
                        pltpu.VMEM((4, tk, tq), F32)],
        compiler_params=_cparams("arbitrary", "arbitrary"),
        name="attn_fwd",
    )(qkv, qkv, qkv, c_col, c_col)


def _pool_counts(row0, tm, w):
    t = row0 + _iota2((tm, 1), 0)
    return jnp.minimum(t + 1, w).astype(F32)


def _out_gate_up(attn_o, pool_o, w_out, x, g2, wg_t, wu_t, *, tm):
    s = x.shape[0]

    def body(a_ref, p_ref, wo_ref, x_ref, g_ref, wg_ref, wu_ref, x1_ref, h2_ref, r_ref, gate_ref, up_ref, act_ref):
        x1 = (x_ref[...] + jnp.dot(a_ref[...], wo_ref[0:ATTN_W, :], preferred_element_type=F32)
              + jnp.dot(p_ref[...], wo_ref[ATTN_W:, :], preferred_element_type=F32))
        r = lax.rsqrt(jnp.mean(x1 * x1, axis=-1, keepdims=True) + EPS)
        x1_ref[...] = x1
        r_ref[...] = r
        h2 = (x1 * r * g_ref[...]).astype(BF16)
        h2_ref[...] = h2
        gate = lax.dot_general(h2, wg_ref[...], NT, preferred_element_type=F32)
        up = lax.dot_general(h2, wu_ref[...], NT, preferred_element_type=F32)
        gate_ref[...] = gate.astype(BF16)
        up_ref[...] = up.astype(BF16)
        act_ref[...] = (gate * jax.nn.sigmoid(gate) * up).astype(BF16)

    row = lambda w: pl.BlockSpec((tm, w), lambda i: (i, 0))
    full = lambda a, b: pl.BlockSpec((a, b), lambda i: (0, 0))
    return pl.pallas_call(
        body,
        grid=(s // tm,),
        in_specs=[row(ATTN_W), row(POOL_W), full(D_MODEL, D_MODEL), row(D_MODEL), full(1, D_MODEL),
                  full(D_FF, D_MODEL), full(D_FF, D_MODEL)],
        out_specs=[row(D_MODEL), row(D_MODEL), row(1), row(D_FF), row(D_FF), row(D_FF)],
        out_shape=[SDS((s, D_MODEL), F32), SDS((s, D_MODEL), BF16), SDS((s, 1), F32), SDS((s, D_FF), BF16),
                   SDS((s, D_FF), BF16), SDS((s, D_FF), BF16)],
        compiler_params=_cparams("arbitrary"),
        name="out_gate_up",
    )(attn_o, pool_o, w_out, x, g2, wg_t, wu_t)


def _staggered(n, start, finish):
    pending = start(0)
    for k in range(n):
        following = start(k + 1) if k + 1 < n else None
        finish(k, pending)
        pending = following


def _down_final(act, wd, x1, gf, tgt, *, tm, sub):
    s = x1.shape[0]

    def body(a_ref, w_ref, x1_ref, g_ref, t_ref, dx2_ref, loss_ref, dgf_ref):
        @pl.when(pl.program_id(0) == 0)
        def _():
            loss_ref[...] = jnp.zeros_like(loss_ref)
            dgf_ref[...] = jnp.zeros_like(dgf_ref)

        g = g_ref[...]

        def matmul(k):
            return jnp.dot(a_ref[k * sub:(k + 1) * sub, :], w_ref[...], preferred_element_type=F32)

        def rest(k, mm):
            rows = slice(k * sub, (k + 1) * sub)
            x2 = x1_ref[rows, :] + mm
            r = lax.rsqrt(jnp.mean(x2 * x2, axis=-1, keepdims=True) + EPS)
            xn = x2 * r
            diff = xn * g - t_ref[rows, :]
            loss_ref[...] += jnp.sum(diff * diff, axis=0, keepdims=True)
            dy = diff * (1.0 / D_MODEL)
            dgf_ref[...] += jnp.sum(dy * xn, axis=0, keepdims=True)
            dxn = dy * g
            dx2_ref[rows, :] = r * (dxn - xn * jnp.mean(dxn * xn, axis=-1, keepdims=True))

        _staggered(tm // sub, matmul, rest)

    row = lambda w: pl.BlockSpec((tm, w), lambda i: (i, 0))
    full = lambda a, b: pl.BlockSpec((a, b), lambda i: (0, 0))
    return pl.pallas_call(
        body,
        grid=(s // tm,),
        in_specs=[row(D_FF), full(D_FF, D_MODEL), row(D_MODEL), full(1, D_MODEL), row(D_MODEL)],
        out_specs=[row(D_MODEL), full(1, D_MODEL), full(1, D_MODEL)],
        out_shape=[SDS((s, D_MODEL), F32), SDS((1, D_MODEL), F32), SDS((1, D_MODEL), F32)],
        compiler_params=_cparams("arbitrary"),
        name="down_final",
    )(act, wd, x1, gf, tgt)


def _swiglu_bwd(dx2, wd, gate, up, *, tm, tn):
    s = dx2.shape[0]

    def body(d_ref, w_ref, gate_ref, up_ref, dgate_ref, dup_ref):
        dact = lax.dot_general(d_ref[...].astype(BF16), w_ref[...], NT, preferred_element_type=F32)
        gate = gate_ref[...].astype(F32)
        sg = jax.nn.sigmoid(gate)
        dup_ref[...] = (dact * (gate * sg)).astype(BF16)
        dgate_ref[...] = (dact * up_ref[...].astype(F32) * (sg * (1.0 + gate * (1.0 - sg)))).astype(BF16)

    ospec = pl.BlockSpec((tm, tn), lambda c, r: (r, c))
    return pl.pallas_call(
        body,
        grid=(D_FF // tn, s // tm),
        in_specs=[pl.BlockSpec((tm, D_MODEL), lambda c, r: (r, 0)), pl.BlockSpec((tn, D_MODEL), lambda c, r: (c, 0)),
                  ospec, ospec],
        out_specs=[ospec, ospec],
        out_shape=[SDS((s, D_FF), BF16), SDS((s, D_FF), BF16)],
        compiler_params=_cparams("arbitrary", "arbitrary"),
        name="swiglu_bwd",
    )(dx2, wd, gate, up)


def _mm_tn_stacked(as_, rows, b, *, ts, name):
    s, nb_ = b.shape
    n = len(as_)
    offsets = [sum(rows[:i]) for i in range(n)]

    def body(*refs):
        a_refs, b_ref, o_ref, acc_ref = refs[:n], refs[n], refs[n + 1], refs[n + 2]
        k = pl.program_id(0)

        @pl.when(k == 0)
        def _():
            acc_ref[...] = jnp.zeros_like(acc_ref)

        bv = b_ref[...].astype(BF16)
        for a_ref, off, cnt in zip(a_refs, offsets, rows):
            part = lax.dot_general(a_ref[...].astype(BF16), bv, TN, preferred_element_type=F32)
            acc_ref[off:off + cnt, :] += part[0:cnt, :]

        @pl.when(k == s // ts - 1)
        def _():
            o_ref[...] = acc_ref[...].astype(BF16)

    return pl.pallas_call(
        body,
        grid=(s // ts,),
        in_specs=[pl.BlockSpec((ts, a.shape[1]), lambda k: (k, 0)) for a in as_] + [pl.BlockSpec((ts, nb_), lambda k: (k, 0))],
        out_specs=pl.BlockSpec((sum(rows), nb_), lambda k: (0, 0)),
        out_shape=SDS((sum(rows), nb_), BF16),
        scratch_shapes=[pltpu.VMEM((sum(rows), nb_), F32)],
        compiler_params=_cparams("arbitrary"),
        name=name,
    )(*as_, b)


def _norm_bwd(dh, x, r, g, dres):
    xn = x * r
    dxn = dh * g
    dx = dres + r * (dxn - xn * jnp.mean(dxn * xn, axis=-1, keepdims=True))
    return dx, jnp.sum(dh * xn, axis=0, keepdims=True)


def _mlp_in_pool_bwd(dgate, dup, wg_t, wu_t, w_out, x1, r2, g2, dx2, pooled, w_pool, pool_scale, *, tm):
    s = x1.shape[0]
    nt = s // tm
    ng = len(POOL_WINDOWS)

    def body(dg_ref, dup_ref, wg_ref, wu_ref, wo_ref, x_ref, r_ref, g_ref, d_ref, p_ref, w_ref, sc_ref,
             dx1_ref, dattn_ref, du_ref, dg2_ref, dw_ref, dsc_ref, head_ref):
        i = pl.program_id(0)

        @pl.when(i == 0)
        def _():
            dg2_ref[...] = jnp.zeros_like(dg2_ref)
            head_ref[...] = jnp.zeros_like(head_ref)
            dw_ref[...] = jnp.zeros_like(dw_ref)
            dsc_ref[...] = jnp.zeros_like(dsc_ref)

        dh2 = (jnp.dot(dg_ref[...], wg_ref[...], preferred_element_type=F32)
               + jnp.dot(dup_ref[...], wu_ref[...], preferred_element_type=F32))
        dx1, dg2 = _norm_bwd(dh2, x_ref[...], r_ref[...], g_ref[...], d_ref[...])
        dg2_ref[...] += dg2
        dx1_ref[...] = dx1
        dmix = lax.dot_general(dx1.astype(BF16), wo_ref[...], NT, preferred_element_type=F32)
        dattn_ref[...] = dmix[:, 0:ATTN_W]
        row0 = (nt - 1 - i) * tm
        for g, w in enumerate(POOL_WINDOWS):
            cols = slice(g * POOL_G, (g + 1) * POOL_G)
            wb = w_ref[g].astype(BF16)
            pooled_g = p_ref[:, cols]
            dpo = dmix[:, ATTN_W + g * POOL_G:ATTN_W + (g + 1) * POOL_G]
            mixed = jnp.dot(pooled_g, wb, preferred_element_type=F32)
            dsc_ref[:, cols] += jnp.sum(dpo * mixed, axis=0, keepdims=True)
            dmp = (dpo * sc_ref[:, cols]).astype(BF16)
            dw_ref[g] += lax.dot_general(pooled_g, dmp, TN, preferred_element_type=F32)
            dpooled = lax.dot_general(dmp, wb, NT, preferred_element_type=F32)
            a = dpooled / _pool_counts(row0, tm, w)
            acc = jnp.concatenate([a, head_ref[:, cols]], axis=0)
            head_ref[:, cols] = a[0:HALO, :]
            k = 1
            while k < w:
                acc = acc + pltpu.roll(acc, tm + HALO - k, axis=0)
                k *= 2
            du_ref[:, cols] = (acc[0:tm, :] - dpooled).astype(BF16)

    row = lambda w: pl.BlockSpec((tm, w), lambda i: (nt - 1 - i, 0))
    full = lambda a, b: pl.BlockSpec((a, b), lambda i: (0, 0))
    pool_w = pl.BlockSpec((ng, POOL_G, POOL_G), lambda i: (0, 0, 0))
    return pl.pallas_call(
        body,
        grid=(nt,),
        in_specs=[row(D_FF), row(D_FF), full(D_FF, D_MODEL), full(D_FF, D_MODEL), full(D_MODEL, D_MODEL),
                  row(D_MODEL), row(1), full(1, D_MODEL), row(D_MODEL), row(POOL_W), pool_w, full(1, POOL_W)],
        out_specs=[row(D_MODEL), row(ATTN_W), row(POOL_W), full(1, D_MODEL), pool_w, full(1, POOL_W)],
        out_shape=[SDS((s, D_MODEL), F32), SDS((s, ATTN_W), F32), SDS((s, POOL_W), BF16), SDS((1, D_MODEL), F32),
                   SDS((ng, POOL_G, POOL_G), F32), SDS((1, POOL_W), F32)],
        scratch_shapes=[pltpu.VMEM((HALO, POOL_W), F32)],
        compiler_params=_cparams("arbitrary"),
        name="mlp_in_pool_bwd",
    )(dgate, dup, wg_t, wu_t, w_out, x1, r2, g2, dx2, pooled, w_pool, pool_scale)


SUM_ROWS = 16


def _heads_t(t):
    n = t.shape[0]
    lane = _iota2((n, LANES), 1)
    tf = t.astype(F32)
    halves = jnp.concatenate([jnp.where(lane < HEAD_DIM, tf, 0.0).T, jnp.where(lane < HEAD_DIM, 0.0, tf).T], axis=1)
    r, c = _iota2((SUM_ROWS, 2 * n), 0), _iota2((SUM_ROWS, 2 * n), 1)
    ones = jnp.where(((r == 0) & (c < n)) | ((r == 4) & (c >= n)), 1.0, 0.0)
    return jnp.concatenate([halves, ones], axis=0).astype(BF16)


def _attn_bwd(qkv, attn_o, d_attn, rowb, ck_col, *, tq):
    s = qkv.shape[0]
    tk = tq
    nb = s // tq
    rows_t = LANES + SUM_ROWS

    def body(q_ref, k_ref, v_ref, o_ref, do_ref, rowb_ref, ck_ref, dq_ref, dk_ref, dv_ref, dck_ref, dcq_ref,
             dqt_ref, delta_ref, kp_ref, qp_ref, dob_ref, qt_ref, kt_ref, dot_ref, front_ref):
        lane = _iota2((tq, LANES), 1)
        lo = lane < HEAD_DIM
        first = _iota2((8, LANES), 1) < HEAD_DIM
        sel = jnp.where(_iota2((8, LANES), 0) < 4, jnp.where(first, 1.0, 0.0), jnp.where(first, 0.0, 1.0))

        def prep(b, _):
            st = pl.multiple_of(b * tq, tq)
            do2 = do_ref[pl.ds(st, tq), :]
            delta_ref[b] = _sel_dot(sel, do2 * o_ref[pl.ds(st, tq), :].astype(F32), NT)
            dob_ref[pl.ds(st, tq), :] = do2.astype(BF16)
            dqt_ref[b] = jnp.zeros((rows_t, tq), F32)
            k2 = k_ref[pl.ds(st, tq), :].astype(F32)
            q2 = q_ref[pl.ds(st, tq), :].astype(F32)
            ck = ck_ref[pl.ds(st, tq), :]
            for h in range(2):
                kp_ref[h * nb + b] = _augment(k2, h, -ck[:, h:h + 1], True)
                qp_ref[h * nb + b] = _augment(q2 * Q_SCALE, h, jnp.zeros((tq, 1), F32), False)
            qt_ref[b] = _heads_t(q2)
            kt_ref[b] = _heads_t(k2)
            dot_ref[b] = _heads_t(do2)[0:LANES, :]
            return 0

        lax.fori_loop(0, nb, prep, 0)

        def split(t):
            z = jnp.zeros_like(t)
            return jnp.where(lo, t, z), jnp.where(lo, z, t)

        def kv_block(j, _):
            st_j = pl.multiple_of(j * tk, tk)
            vs = split(v_ref[pl.ds(st_j, tk), :])
            kt = kt_ref[j]

            def stage(i, slot):
                ic = jnp.minimum(i, nb - 1)
                do2 = dob_ref[pl.ds(pl.multiple_of(ic * tq, tq), tq), :]
                for h in range(2):
                    front_ref[4 * slot + h] = lax.dot_general(kp_ref[h * nb + j], qp_ref[h * nb + ic], NT,
                                                              preferred_element_type=F32)
                    front_ref[4 * slot + 2 + h] = lax.dot_general(vs[h], do2, NT, preferred_element_type=F32)

            def q_block(i, slot, carry, diagonal):
                dkt, dvt = carry
                ic = jnp.minimum(i, nb - 1)
                rb = rowb_ref[ic] + jnp.where(i < nb, 0.0, NEG)
                dl = delta_ref[ic]
                pts, dsts = [], []
                for h in range(2):
                    st = front_ref[4 * slot + h] + rb[h:h + 1, :]
                    if diagonal:
                        st = jnp.where(_iota2((tk, tq), 0) <= _iota2((tk, tq), 1), st, NEG)
                    pt = jnp.exp(st)
                    pts.append(pt.astype(BF16))
                    dsts.append((pt * (front_ref[4 * slot + 2 + h] - dl[4 * h:4 * h + 1, :])).astype(BF16))
                dvt = dvt + lax.dot_general(dot_ref[ic], jnp.concatenate(pts, axis=1), NT, preferred_element_type=F32)
                dkt = dkt + lax.dot_general(qt_ref[ic], jnp.concatenate(dsts, axis=1), NT, preferred_element_type=F32)
                dqt_ref[ic] += jnp.dot(kt, jnp.concatenate(dsts, axis=0), preferred_element_type=F32)
                return dkt, dvt

            def run(i0, steps, carry):
                for d in range(steps):
                    stage(i0 + d + 1, d % 2)
                    carry = q_block(i0 + d, 1 - d % 2, carry, False)
                return carry

            stage(j, 0)
            stage(j + 1, 1)
            carry = q_block(j, 0, (jnp.zeros((rows_t, tk), F32), jnp.zeros((LANES, tk), F32)), True)
            left = nb - 1 - j
            fours = lax.shift_right_logical(left, 2)
            carry = lax.fori_loop(0, fours, lambda t, c: run(j + 1 + 4 * t, 4, c), carry)
            twos = lax.shift_right_logical(left - 4 * fours + 1, 1)
            dkt, dvt = lax.fori_loop(0, twos, lambda t, c: run(j + 1 + 4 * fours + 2 * t, 2, c), carry)
            dk_ref[pl.ds(st_j, tk), :] = (dkt[0:LANES, :].T * Q_SCALE).astype(BF16)
            dv_ref[pl.ds(st_j, tk), :] = dvt.T.astype(BF16)
            dck_ref[j] = dkt[LANES:LANES + 8, :]
            return 0

        lax.fori_loop(0, nb, kv_block, 0)

        def finish(b, _):
            acc = dqt_ref[b]
            dq_ref[pl.ds(pl.multiple_of(b * tq, tq), tq), :] = (acc[0:LANES, :].T * Q_SCALE).astype(BF16)
            dcq_ref[b] = acc[LANES:LANES + 8, :]
            return 0

        lax.fori_loop(0, nb, finish, 0)

    col = lambda off: pl.BlockSpec((s, LANES), lambda p: (0, off + p))
    sums = pl.BlockSpec((None, nb, 8, tq), lambda p: (p, 0, 0, 0))
    return pl.pallas_call(
        body,
        grid=(N_PAIRS,),
        in_specs=[col(0), col(N_PAIRS), col(2 * N_PAIRS), col(0), col(0),
                  pl.BlockSpec((None, nb, 2, tq), lambda p: (p, 0, 0, 0)),
                  pl.BlockSpec((None, s, 2), lambda p: (p, 0, 0))],
        out_specs=[col(0), col(0), col(0), sums, sums],
        out_shape=[SDS((s, ATTN_W), BF16), SDS((s, ATTN_W), BF16), SDS((s, ATTN_W), BF16),
                   SDS((N_PAIRS, nb, 8, tq), F32), SDS((N_PAIRS, nb, 8, tq), F32)],
        scratch_shapes=[pltpu.VMEM((nb, rows_t, tq), F32), pltpu.VMEM((nb, 8, tq), F32),
                        pltpu.VMEM((2 * nb, tk, LANES), BF16), pltpu.VMEM((2 * nb, tq, LANES), BF16),
                        pltpu.VMEM((s, LANES), BF16), pltpu.VMEM((nb, rows_t, 2 * tq), BF16),
                        pltpu.VMEM((nb, rows_t, 2 * tk), BF16), pltpu.VMEM((nb, LANES, 2 * tq), BF16),
                        pltpu.VMEM((8, tk, tq), F32)],
        compiler_params=_cparams("arbitrary"),
        name="attn_bwd",
    )(qkv, qkv, qkv, attn_o, d_attn, rowb, ck_col)


def _forget_bwd(dc_t, fl_t, b_rows):
    rows = fl_t.shape[0]
    nb = rows // N_HEADS

    def body(dc_ref, fl_ref, b_ref, dfl_ref, db_ref):
        dc = dc_ref[...]
        lower = _iota2((LANES, LANES), 0) >= _iota2((LANES, LANES), 1)
        ones = jnp.ones((LANES, LANES), F32)
        rr, cc, same = _head_block_masks(rows, nb)
        dlf = _dot_sel(dc, lower) + _sel_dot(same & (cc > rr), _dot_sel(dc, ones))
        dfl = dlf / (1.0 + jnp.exp(fl_ref[...] + b_ref[...]))
        dfl_ref[...] = dfl
        shift = nb.bit_length() - 1
        hsel = lax.shift_right_logical(_iota2((N_HEADS, rows), 1), shift) == _iota2((N_HEADS, rows), 0)
        db_ref[...] = _sel_dot(hsel, _dot_sel(dfl, ones))

    return pl.pallas_call(body, out_shape=[SDS(fl_t.shape, F32), SDS((N_HEADS, LANES), F32)],
                          compiler_params=_cparams(), name="forget_bwd")(dc_t, fl_t, b_rows)


def _in_bwd(dq, dk, dv, du, dfl, w_in_t, x, r1, g1, dx1, *, tm):
    s = x.shape[0]
    pieces = ((0, ATTN_W), (ATTN_W, 2 * ATTN_W), (2 * ATTN_W, QKV_W), (U_OFF, F_OFF), (F_OFF, IN_PAD))

    def body(dq_ref, dk_ref, dv_ref, du_ref, df_ref, w_ref, x_ref, r_ref, g_ref, d_ref, dx_ref, dg1_ref):
        @pl.when(pl.program_id(0) == 0)
        def _():
            dg1_ref[...] = jnp.zeros_like(dg1_ref)

        dh = None
        for ref, (c0, c1) in zip((dq_ref, dk_ref, dv_ref, du_ref, df_ref), pieces):
            t = jnp.dot(ref[...], w_ref[c0:c1, :], preferred_element_type=F32)
            dh = t if dh is None else dh + t
        dx, dg1 = _norm_bwd(dh, x_ref[...], r_ref[...], g_ref[...], d_ref[...])
        dx_ref[...] = dx
        dg1_ref[...] += dg1

    row = lambda w: pl.BlockSpec((tm, w), lambda i: (i, 0))
    full = lambda a, b: pl.BlockSpec((a, b), lambda i: (0, 0))
    return pl.pallas_call(
        body,
        grid=(s // tm,),
        in_specs=[row(ATTN_W), row(ATTN_W), row(ATTN_W), row(POOL_W), row(LANES), full(IN_PAD, D_MODEL),
                  row(D_MODEL), row(1), full(1, D_MODEL), row(D_MODEL)],
        out_specs=[row(D_MODEL), full(1, D_MODEL)],
        out_shape=[SDS((s, D_MODEL), F32), SDS((1, D_MODEL), F32)],
        compiler_params=_cparams("arbitrary"),
        name="in_bwd",
    )(dq, dk, dv, du, dfl, w_in_t, x, r1, g1, dx1)


def _tiles(s):
    big = min(512, s)
    return dict(row=big, attn=min(256, s // 2), ff_rows=min(256, s), tall=min(1024, s))


def _tie(a, token):
    return a + token[0:1, 0:1].astype(a.dtype)


def _local_step(x, tgt, p, weight, emit, started):
    s = x.shape[0]
    t = _tiles(s)
    tm, tq = t["row"], t["attn"]
    nb = s // LANES
    nqb = s // tq
    g1, g2, gf = p["norm1_g"], p["norm2_g"], p["final_g"].reshape(1, D_MODEL)
    w_pool, pool_scale = p["w_pool"][0], p["pool_scale"]

    h, r1 = _norm1(x, _tie(g1, started), tm=tm)
    w_in_t = weight("w_in", h)
    qkv, fl, pooled, pool_o = _in_proj_pool(h, w_in_t, w_pool, pool_scale, tm=t["tall"])
    fl_t = fl[:, :N_HEADS].T.reshape(N_HEADS * nb, LANES)
    b_rows = jnp.repeat(p["b_forget"].reshape(N_HEADS), nb).reshape(N_HEADS * nb, 1)
    c = _forget_cumsum(fl_t, b_rows).reshape(N_PAIRS, 2, s)
    c_col = c.transpose(0, 2, 1)
    c_rowblk = c.reshape(N_PAIRS, 2, nqb, tq).transpose(0, 2, 1, 3)
    attn_o, lse = _attn_fwd(qkv, c_col, tk=tq)
    lse = lse.reshape(N_PAIRS, nqb // 2, 2, 2, tq).transpose(0, 1, 3, 2, 4).reshape(N_PAIRS, nqb, 2, tq)
    w_out = weight("w_out", attn_o)
    wg_t, wu_t = weight("w_gate_up", attn_o)
    x1, h2, r2, gate, up, act = _out_gate_up(attn_o, pool_o, w_out, x, g2, wg_t, wu_t, tm=t["ff_rows"])
    wd = weight("w_down", act)
    dx2, loss_row, d_gf = _down_final(act, wd, x1, gf, tgt, tm=tm, sub=min(128, tm))

    dgate, dup = _swiglu_bwd(dx2, wd, gate, up, tm=t["ff_rows"], tn=D_FF)
    d_wd = _mm_tn_stacked([act], [D_FF], dx2, ts=t["tall"], name="grad_w_down")
    d_wg_t = _mm_tn_stacked([dgate], [D_FF], h2, ts=t["tall"], name="grad_w_gate")
    d_wu_t = _mm_tn_stacked([dup], [D_FF], h2, ts=t["tall"], name="grad_w_up")
    dx1, d_attn, du, d_g2, d_wpool, d_pscale = _mlp_in_pool_bwd(dgate, dup, wg_t, wu_t, w_out, x1, r2, g2, dx2, pooled,
                                                               w_pool, pool_scale, tm=t["ff_rows"])
    d_wo = _mm_tn_stacked([attn_o, pool_o], [ATTN_W, POOL_W], dx1, ts=t["tall"], name="grad_w_out")
    token = emit(("w_down", "w_gate", "w_up", "w_out"), (d_wd, d_wg_t, d_wu_t, d_wo))
    rowb = _tie(c_rowblk - lse, token)
    dq, dk, dv, dck, dcq = _attn_bwd(qkv, attn_o, d_attn, rowb, c_col, tq=tq)
    dc_t = (dcq - dck)[:, :, 0::4, :].transpose(0, 2, 1, 3).reshape(N_HEADS * nb, LANES)
    dfl_t, db = _forget_bwd(dc_t, fl_t, b_rows)
    dfl = jnp.pad(dfl_t.reshape(N_HEADS, s).T, ((0, 0), (0, LANES - N_HEADS))).astype(BF16)
    d_w_in_t = _mm_tn_stacked([dq, dk, dv, dfl, du], [ATTN_W, ATTN_W, ATTN_W, N_HEADS, POOL_W], h, ts=t["tall"],
                              name="grad_w_in")
    token = emit(("w_in",), (d_w_in_t,))
    dx, d_g1 = _in_bwd(dq, dk, dv, du, dfl, w_in_t, x, r1, _tie(g1, token), dx1, tm=tm)

    small = dict(norm1_g=d_g1, b_forget=db[:, 0].reshape(1, N_HEADS), w_pool=d_wpool, pool_scale=d_pscale,
                 norm2_g=d_g2, final_g=d_gf)
    return loss_row, dx, small


def _my_index():
    return 4 * lax.axis_index("x") + 2 * lax.axis_index("y") + lax.axis_index("c")


def _peer(k):
    pos = [lax.axis_index(a) for a in ("x", "y", "c")]
    flipped = tuple(1 - p if (k >> b) & 1 else p for p, b in zip(pos, (2, 1, 0)))
    return flipped, 4 * flipped[0] + 2 * flipped[1] + flipped[2]


_HBM = pl.BlockSpec(memory_space=pltpu.HBM)
_SEM = pl.BlockSpec(memory_space=pltpu.SEMAPHORE)
_DATAFLOW = pltpu.SideEffectType.DATAFLOW_SIDE_EFFECTING


ALL_PEERS = tuple(range(1, N_DEV))
SAME_CORE = (1, 2, 4, 6)


def _peer_copies(ins, lands, send_sems, recv_sems, scatter, peers, arrivals):
    me = _my_index()
    copies = []
    for w in range(len(ins)):
        for k in peers[w]:
            dev, idx = _peer(k)
            copies.append(pltpu.make_async_remote_copy(
                src_ref=ins[w].at[idx] if scatter[w] else ins[w], dst_ref=lands[w].at[idx if arrivals else me],
                send_sem=send_sems[w].at[k - 1], recv_sem=recv_sems[w].at[k - 1], device_id=dev, device_id_type=MESH))
    return copies


def _own_copies(ins, lands, send_sems, scatter):
    me = _my_index()
    return [pltpu.make_async_copy(ins[w].at[me] if scatter[w] else ins[w], lands[w].at[me], send_sems[w].at[N_DEV - 1])
            for w in range(len(ins))]


def _forward_copies(land, send_sems, recv_sems, arrivals):
    sibling, _ = _peer(1)
    copies = []
    for j, k in enumerate(SAME_CORE[1:]):
        src, dst = _peer(k)[1], _peer(k ^ 1 if arrivals else k)[1]
        copies.append(pltpu.make_async_remote_copy(
            src_ref=land.at[src], dst_ref=land.at[dst], send_sem=send_sems.at[j], recv_sem=recv_sems.at[j],
            device_id=sibling, device_id_type=MESH))
    return copies


def _forward_start(land, name):
    def body(land_ref, send_sems, recv_sems, land_thru, token):
        for cp in _forward_copies(land_ref, send_sems, recv_sems, False):
            cp.start()
        token[...] = jnp.zeros_like(token)

    sem = pltpu.SemaphoreType.DMA((len(SAME_CORE) - 1,))
    send, recv, thru, _ = pl.pallas_call(
        body,
        in_specs=[_HBM],
        out_specs=[_SEM, _SEM, _HBM, pl.BlockSpec(memory_space=pltpu.VMEM)],
        out_shape=[sem, sem, pltpu.HBM(land.shape, land.dtype), SDS((8, LANES), F32)],
        input_output_aliases={0: 2},
        compiler_params=pltpu.CompilerParams(has_side_effects=_DATAFLOW),
        name=name,
    )(land)
    return send, recv, thru


def _forward_wait(handle, after, name):
    def body(land_ref, send_sems, recv_sems, after_ref, land_out):
        for cp in _forward_copies(land_ref, send_sems, recv_sems, False):
            cp.wait_send()
        for cp in _forward_copies(land_ref, send_sems, recv_sems, True):
            cp.wait_recv()

    send, recv, land = handle
    return pl.pallas_call(
        body,
        in_specs=[_HBM, _SEM, _SEM, pl.BlockSpec(memory_space=pl.ANY)],
        out_specs=_HBM,
        out_shape=pltpu.HBM(land.shape, land.dtype),
        input_output_aliases={0: 0},
        compiler_params=pltpu.CompilerParams(has_side_effects=_DATAFLOW),
        name=name,
    )(land, send, recv, after)


def _exchange_start(arrays, scatter, name, peers=None):
    n = len(arrays)
    peers = peers or [ALL_PEERS] * n
    land_shapes = [(N_DEV,) + tuple(a.shape[1:] if sc else a.shape) for a, sc in zip(arrays, scatter)]

    def body(*refs):
        ins, lands = refs[:n], refs[n:2 * n]
        send_sems, recv_sems = refs[2 * n:3 * n], refs[3 * n:4 * n]
        token = refs[6 * n]
        for cp in _peer_copies(ins, lands, send_sems, recv_sems, scatter, peers, False):
            cp.start()
        for cp in _own_copies(ins, lands, send_sems, scatter):
            cp.start()
        token[...] = jnp.zeros_like(token)

    sends, recvs = pltpu.SemaphoreType.DMA((N_DEV,)), pltpu.SemaphoreType.DMA((N_DEV - 1,))
    outs = pl.pallas_call(
        body,
        in_specs=[_HBM] * (2 * n),
        out_specs=[_SEM] * (2 * n) + [_HBM] * (2 * n) + [pl.BlockSpec(memory_space=pltpu.VMEM)],
        out_shape=[sends] * n + [recvs] * n + [pltpu.HBM(a.shape, a.dtype) for a in arrays]
        + [pltpu.HBM(sh, a.dtype) for sh, a in zip(land_shapes, arrays)] + [SDS((8, LANES), F32)],
        input_output_aliases={i: 2 * n + i for i in range(2 * n)},
        compiler_params=pltpu.CompilerParams(has_side_effects=_DATAFLOW),
        name=name,
    )(*[pltpu.with_memory_space_constraint(a, pltpu.HBM) for a in arrays],
      *[pltpu.with_memory_space_constraint(lax.empty(sh, a.dtype), pltpu.HBM) for sh, a in zip(land_shapes, arrays)])
    handles = [dict(send=outs[w], recv=outs[n + w], src=outs[2 * n + w], land=outs[3 * n + w], scatter=scatter[w],
                    peers=peers[w]) for w in range(n)]
    return handles, outs[4 * n]


def _exchange_wait(handles, after, name):
    n = len(handles)
    scatter, peers = [h["scatter"] for h in handles], [h["peers"] for h in handles]

    def body(*refs):
        ins, lands = refs[:n], refs[n:2 * n]
        send_sems, recv_sems = refs[2 * n:3 * n], refs[3 * n:4 * n]
        for cp in _peer_copies(ins, lands, send_sems, recv_sems, scatter, peers, False):
            cp.wait_send()
        for cp in _peer_copies(ins, lands, send_sems, recv_sems, scatter, peers, True):
            cp.wait_recv()
        for cp in _own_copies(ins, lands, send_sems, scatter):
            cp.wait()

    srcs, lands = [h["src"] for h in handles], [h["land"] for h in handles]
    outs = pl.pallas_call(
        body,
        in_specs=[_HBM] * (2 * n) + [_SEM] * (2 * n) + [pl.BlockSpec(memory_space=pl.ANY)],
        out_specs=[_HBM] * (2 * n),
        out_shape=[pltpu.HBM(a.shape, a.dtype) for a in srcs + lands],
        input_output_aliases={i: i for i in range(2 * n)},
        compiler_params=pltpu.CompilerParams(has_side_effects=_DATAFLOW),
        name=name,
    )(*srcs, *lands, *[h["send"] for h in handles], *[h["recv"] for h in handles], after)
    return outs[n:]


def _adamw(parts, w, m, v, name):
    rows, cols = w.shape
    tr = rows // 4 if rows % 32 == 0 else rows

    def body(p_ref, w_ref, m_ref, v_ref, g_ref, d_ref, mo_ref, vo_ref):
        g = p_ref[0].astype(F32)
        for d in range(1, N_DEV):
            g = g + p_ref[d].astype(F32)
        g_ref[...] = g
        d_ref[...], mo_ref[...], vo_ref[...] = _adam_update(g, w_ref[...], m_ref[...], v_ref[...])

    blk = pl.BlockSpec((tr, cols), lambda i: (i, 0))
    return pl.pallas_call(
        body,
        grid=(rows // tr,),
        in_specs=[pl.BlockSpec((N_DEV, tr, cols), lambda i: (0, i, 0)), blk, blk, blk],
        out_specs=[blk] * 4,
        out_shape=[SDS((rows, cols), F32)] * 4,
        compiler_params=_cparams("arbitrary"),
        name=name,
    )(parts, w, m, v)


_ROW_OF = dict(norm1_g=(0, D_MODEL), norm2_g=(1, D_MODEL), final_g=(2, D_MODEL), pool_scale=(3, POOL_W),
               b_forget=(4, N_HEADS), loss=(5, 1))


def _pack_rows(vals):
    rows = [jnp.pad(vals[n].reshape(1, width).astype(F32), ((0, 0), (0, D_MODEL - width)))
            for n, (_, width) in sorted(_ROW_OF.items(), key=lambda kv: kv[1][0])]
    return jnp.concatenate(rows + [jnp.zeros((8 - len(rows), D_MODEL), F32)], axis=0)


def _adam_update(g, w, m, v):
    m_new = ADAM_B1 * m + (1.0 - ADAM_B1) * g
    v_new = ADAM_B2 * v + (1.0 - ADAM_B2) * (g * g)
    m_hat = m_new / (1.0 - ADAM_B1 ** ADAM_STEP)
    v_hat = v_new / (1.0 - ADAM_B2 ** ADAM_STEP)
    return -ADAM_LR * (m_hat / (jnp.sqrt(v_hat) + ADAM_EPS) + ADAM_WD * w), m_new, v_new


def _adamw_replicated(parts_rows, parts_pool, w, m, v):
    names = ("norm1_g", "norm2_g", "final_g", "pool_scale", "b_forget", "w_pool")
    shapes = {n: ((len(POOL_WINDOWS), POOL_G, POOL_G) if n == "w_pool" else (1, _ROW_OF[n][1])) for n in names}

    def body(rows_ref, pool_ref, *refs):
        ins, outs = refs[:3 * len(names)], refs[3 * len(names):]

        def total(n):
            if n == "w_pool":
                pieces = [pool_ref[d] for d in range(N_DEV)]
            else:
                row, width = _ROW_OF[n]
                pieces = [rows_ref[d, row:row + 1, 0:width] for d in range(N_DEV)]
            g = pieces[0]
            for p in pieces[1:]:
                g = g + p
            return g

        outs[0][...] = total("loss")
        for k, n in enumerate(names):
            g = total(n)
            delta, m_new, v_new = _adam_update(g, ins[3 * k][...], ins[3 * k + 1][...], ins[3 * k + 2][...])
            for o_ref, val in zip(outs[1 + 4 * k:5 + 4 * k], (g, delta, m_new, v_new)):
                o_ref[...] = val

    args = [d[n].reshape(shapes[n]) for n in names for d in (w, m, v)]
    res = pl.pallas_call(
        body,
        out_shape=[SDS((1, 1), F32)] + [SDS(shapes[n], F32) for n in names for _ in range(4)],
        compiler_params=_cparams(),
        name="adamw_replicated",
    )(parts_rows, parts_pool, *args)
    return res[0], {n: [r.reshape(w[n].shape) for r in res[1 + 4 * k:5 + 4 * k]] for k, n in enumerate(names)}


def kernel(x, norm1_g, w_in, b_forget, w_pool, pool_scale, w_out, norm2_g, w_gate, w_up, w_down, final_g, loss_target, m_norm1_g, m_w_in, m_b_forget, m_w_pool, m_pool_scale, m_w_out, m_norm2_g, m_w_gate, m_w_up, m_w_down, m_final_g, v_norm1_g, v_w_in, v_b_forget, v_w_pool, v_pool_scale, v_w_out, v_norm2_g, v_w_gate, v_w_up, v_w_down, v_final_g):
    big = ("w_in", "w_out", "w_gate", "w_up", "w_down")
    order = ("norm1_g", "w_in", "b_forget", "w_pool", "pool_scale", "w_out", "norm2_g", "w_gate", "w_up", "w_down",
             "final_g")
    w = dict(norm1_g=norm1_g, w_in=w_in, b_forget=b_forget, w_pool=w_pool, pool_scale=pool_scale, w_out=w_out,
             norm2_g=norm2_g, w_gate=w_gate, w_up=w_up, w_down=w_down, final_g=final_g)
    m = dict(norm1_g=m_norm1_g, w_in=m_w_in, b_forget=m_b_forget, w_pool=m_w_pool, pool_scale=m_pool_scale,
             w_out=m_w_out, norm2_g=m_norm2_g, w_gate=m_w_gate, w_up=m_w_up, w_down=m_w_down, final_g=m_final_g)
    v = dict(norm1_g=v_norm1_g, w_in=v_w_in, b_forget=v_b_forget, w_pool=v_w_pool, pool_scale=v_pool_scale,
             w_out=v_w_out, norm2_g=v_norm2_g, w_gate=v_w_gate, w_up=v_w_up, w_down=v_w_down, final_g=v_final_g)

    flipped = ("w_in", "w_gate", "w_up")
    shard = lambda d, n: d[n][0].T if n in flipped else d[n][0]
    gather, started = _exchange_start([shard(w, n).astype(BF16) for n in big], [False] * len(big), "gather_start",
                                      peers=[SAME_CORE if n == "w_in" else ALL_PEERS for n in big])
    gather = dict(zip(big, gather))

    def gathered(names, after):
        return _exchange_wait([gather[n] for n in names], after, "gather_wait_" + names[0])

    def weight(name, after):
        if name == "w_in":
            forward = _forward_start(gathered(["w_in"], after)[0], "gather_forward_start")
            full = _forward_wait(forward, after, "gather_forward_wait").reshape(IN_W, D_MODEL)
            f0 = QKV_W + N_HEADS
            return jnp.concatenate([full[:QKV_W], full[f0:], full[QKV_W:f0],
                                    jnp.zeros((IN_PAD - IN_W, D_MODEL), BF16)], axis=0)
        if name == "w_out":
            return gathered(["w_out"], after)[0].reshape(D_MODEL, D_MODEL)
        if name == "w_gate_up":
            return [g.reshape(D_FF, D_MODEL) for g in gathered(["w_gate", "w_up"], after)]
        return gathered(["w_down"], after)[0].reshape(D_FF, D_MODEL)

    rows = lambda g: g.reshape(N_DEV, g.shape[0] // N_DEV, g.shape[1])
    sent = {}

    def emit(names, grads):
        handles, token = _exchange_start([rows(g) for g in grads], [True] * len(names), "grads_start_" + names[0])
        sent.update(zip(names, handles))
        return token

    loss_row, dx, small_grads = _local_step(x[0], loss_target[0], w, weight, emit, started)

    packed = _pack_rows(dict(small_grads, loss=0.5 / D_MODEL * jnp.sum(loss_row)))
    small_handles, after = _exchange_start([packed, small_grads["w_pool"]], [False, False], "grads_start_replicated")

    outs = {}
    for name in ("w_down", "w_gate", "w_up", "w_out", "w_in"):
        (parts,) = _exchange_wait([sent[name]], after, "grads_wait_" + name)
        outs[name] = _adamw(parts, shard(w, name), shard(m, name), shard(v, name), "adamw_" + name)
        after = outs[name][0]
        outs[name] = [(a.T if name in flipped else a)[None] for a in outs[name]]
    parts_rows, parts_pool = _exchange_wait(small_handles, after, "grads_wait_replicated")
    loss, small = _adamw_replicated(parts_rows, parts_pool, w, m, v)
    outs.update(small)

    return (loss.reshape(()), dx[None]) + tuple(outs[n][k] for k in range(4) for n in order)
```

```python
import jax
import jax.numpy as jnp
from jax import lax
from jax.experimental import pallas as pl
from jax.experimental.pallas import tpu as pltpu

F32 = jnp.float32
BF16 = jnp.bfloat16
SDS = jax.ShapeDtypeStruct

D_MODEL = 1024
ATTN_W = 512
N_HEADS = 8
HEAD_DIM = 64
Q_SCALE = HEAD_DIM ** -0.5
N_PAIRS = N_HEADS // 2
POOL_W = 512
POOL_WINDOWS = (2, 4, 8, 16)
POOL_G = 128
HALO = 16
IN_W = 3 * ATTN_W + N_HEADS + POOL_W
QKV_W = 3 * ATTN_W
U_OFF = QKV_W
F_OFF = QKV_W + POOL_W
IN_PAD = F_OFF + 128
D_FF = 2816
EPS = 1e-6
NEG = -1e30
N_DEV = 8
LANES = 128

ADAM_LR = 0.001
ADAM_B1 = 0.9
ADAM_B2 = 0.999
ADAM_EPS = 1e-08
ADAM_WD = 0.01
ADAM_STEP = 10

VMEM_LIMIT_BYTES = 56 * 1024 * 1024
MESH = pl.DeviceIdType.MESH
NT = (((1,), (1,)), ((), ()))
TN = (((0,), (0,)), ((), ()))


def _cparams(*sem):
    return pltpu.CompilerParams(dimension_semantics=sem or None, vmem_limit_bytes=VMEM_LIMIT_BYTES)


def _split3(a):
    hi = a.astype(BF16)
    r1 = a - hi.astype(F32)
    mid = r1.astype(BF16)
    lo = (r1 - mid.astype(F32)).astype(BF16)
    return hi, mid, lo


def _dot_sel(a, sel, dims=None):
    sb = sel.astype(BF16)
    if dims is None:
        return sum(jnp.dot(p, sb, preferred_element_type=F32) for p in _split3(a))
    return sum(lax.dot_general(p, sb, dims, preferred_element_type=F32) for p in _split3(a))


def _sel_dot(sel, a, dims=None):
    sb = sel.astype(BF16)
    if dims is None:
        return sum(jnp.dot(sb, p, preferred_element_type=F32) for p in _split3(a))
    return sum(lax.dot_general(sb, p, dims, preferred_element_type=F32) for p in _split3(a))


def _iota2(shape, dim):
    return lax.broadcasted_iota(jnp.int32, shape, dim)


UNROLLS = (8, 4, 2)


def _shift_div(x, n):
    return lax.shift_right_logical(x, n.bit_length() - 1)


def _norm1(x, g1, *, tm):
    s = x.shape[0]

    def body(x_ref, g_ref, h_ref, r_ref):
        xv = x_ref[...]
        r = lax.rsqrt(jnp.mean(xv * xv, axis=-1, keepdims=True) + EPS)
        h_ref[...] = (xv * r * g_ref[...]).astype(BF16)
        r_ref[...] = r

    row = lambda w: pl.BlockSpec((tm, w), lambda i: (i, 0))
    return pl.pallas_call(
        body,
        grid=(s // tm,),
        in_specs=[row(D_MODEL), pl.BlockSpec((1, D_MODEL), lambda i: (0, 0))],
        out_specs=[row(D_MODEL), row(1)],
        out_shape=[SDS((s, D_MODEL), BF16), SDS((s, 1), F32)],
        compiler_params=_cparams("arbitrary"),
        name="norm1",
    )(x, g1)


def _in_proj_pool(h, w_in_t, w_pool, pool_scale, *, tm):
    s = h.shape[0]

    def body(h_ref, w_ref, wp_ref, sc_ref, qkv_ref, fl_ref, pooled_ref, po_ref, tail_ref):
        i = pl.program_id(0)

        @pl.when(i == 0)
        def _():
            tail_ref[...] = jnp.zeros_like(tail_ref)

        hv = h_ref[...]
        uv = lax.dot_general(hv, w_ref[U_OFF:F_OFF, :], NT, preferred_element_type=F32)
        qkv_ref[...] = lax.dot_general(hv, w_ref[0:QKV_W, :], NT, preferred_element_type=F32).astype(BF16)
        fl_ref[...] = lax.dot_general(hv, w_ref[F_OFF:IN_PAD, :], NT, preferred_element_type=F32)
        ext = jnp.concatenate([tail_ref[...], uv], axis=0)
        tail_ref[...] = uv[tm - HALO:, :]
        for g, w in enumerate(POOL_WINDOWS):
            cols = slice(g * POOL_G, (g + 1) * POOL_G)
            acc = ext[:, cols]
            k = 1
            while k < w:
                acc = acc + pltpu.roll(acc, k, axis=0)
                k *= 2
            pooled = (acc[HALO:, :] / _pool_counts(i * tm, tm, w) - uv[:, cols]).astype(BF16)
            pooled_ref[:, cols] = pooled
            mixed = jnp.dot(pooled, wp_ref[g].astype(BF16), preferred_element_type=F32)
            po_ref[:, cols] = (mixed * sc_ref[:, cols]).astype(BF16)

    row = lambda w: pl.BlockSpec((tm, w), lambda i: (i, 0))
    return pl.pallas_call(
        body,
        grid=(s // tm,),
        in_specs=[row(D_MODEL), pl.BlockSpec((IN_PAD, D_MODEL), lambda i: (0, 0)),
                  pl.BlockSpec((len(POOL_WINDOWS), POOL_G, POOL_G), lambda i: (0, 0, 0)),
                  pl.BlockSpec((1, POOL_W), lambda i: (0, 0))],
        out_specs=[row(QKV_W), row(LANES), row(POOL_W), row(POOL_W)],
        out_shape=[SDS((s, QKV_W), BF16), SDS((s, LANES), F32), SDS((s, POOL_W), BF16), SDS((s, POOL_W), BF16)],
        scratch_shapes=[pltpu.VMEM((HALO, POOL_W), F32)],
        compiler_params=_cparams("arbitrary"),
        name="in_proj_pool",
    )(h, w_in_t, w_pool, pool_scale)


def _head_block_masks(rows, nb):
    shift = nb.bit_length() - 1
    rr, cc = _iota2((rows, rows), 0), _iota2((rows, rows), 1)
    same = lax.shift_right_logical(rr, shift) == lax.shift_right_logical(cc, shift)
    return rr, cc, same


def _forget_cumsum(fl_t, b_rows):
    rows = fl_t.shape[0]
    nb = rows // N_HEADS

    def body(fl_ref, b_ref, c_ref):
        z = fl_ref[...] + b_ref[...]
        lf = jnp.minimum(z, 0.0) - jnp.log1p(jnp.exp(-jnp.abs(z)))
        upper = _iota2((LANES, LANES), 0) <= _iota2((LANES, LANES), 1)
        within = _dot_sel(lf, upper)
        tot = _dot_sel(lf, jnp.ones((LANES, LANES), F32))
        rr, cc, same = _head_block_masks(rows, nb)
        c_ref[...] = within + _sel_dot(same & (cc < rr), tot)

    return pl.pallas_call(body, out_shape=SDS(fl_t.shape, F32), compiler_params=_cparams(), name="forget_cumsum")(
        fl_t, b_rows)


BIAS_LANES = 3


def _augment(t, h, col, col_first):
    n = t.shape[0]
    lane = _iota2((n, LANES), 1)
    own = (lane < HEAD_DIM) if h == 0 else (lane >= HEAD_DIM)
    b0 = HEAD_DIM if h == 0 else 0
    c0, o0 = (b0, b0 + BIAS_LANES) if col_first else (b0 + BIAS_LANES, b0)
    x = jnp.where(own, t, 0.0)
    for off, piece in enumerate(_split3(col)):
        x = jnp.where(lane == c0 + off, piece.astype(F32), x)
    x = jnp.where((lane >= o0) & (lane < o0 + BIAS_LANES), 1.0, x)
    return x.astype(BF16)


def _attn_fwd(qkv, c_col, *, tk):
    s = qkv.shape[0]
    tq = 2 * tk
    nb = s // tk

    def body(q_ref, k_ref, v_ref, cq_ref, ck_ref, o_ref, lse_ref, kp_ref, vt_ref, st_ref):
        i = pl.program_id(1)

        @pl.when(i == 0)
        def _():
            def prep(jb, _):
                st = pl.multiple_of(jb * tk, tk)
                k2 = k_ref[pl.ds(st, tk), :].astype(F32)
                ck = ck_ref[pl.ds(st, tk), :]
                for h in range(2):
                    kp_ref[h * nb + jb] = _augment(k2, h, -ck[:, h:h + 1], True)
                vt_ref[jb] = v_ref[pl.ds(st, tk), :].astype(F32).T.astype(BF16)
                return 0

            lax.fori_loop(0, nb, prep, 0)

        qs = q_ref[...].astype(F32) * Q_SCALE
        cq = cq_ref[...]
        qp = [_augment(qs, h, cq[:, h:h + 1], False) for h in range(2)]

        def logits(j):
            return tuple(lax.dot_general(kp_ref[h * nb + j], qp[h], NT, preferred_element_type=F32) for h in range(2))

        def softmax_pv(j, slot, stats, masked):
            out = []
            for h in range(2):
                m, l, acc = stats[h]
                st = st_ref[2 * slot + h]
                if masked:
                    st = jnp.where(j * tk + _iota2((tk, tq), 0) <= i * tq + _iota2((tk, tq), 1), st, NEG)
                m_new = jnp.maximum(m, jnp.max(st, axis=0, keepdims=True))
                alpha = jnp.exp(m - m_new)
                p = jnp.exp(st - m_new)
                l = alpha * l + jnp.sum(p, axis=0, keepdims=True)
                vt = vt_ref[j, h * HEAD_DIM:(h + 1) * HEAD_DIM, :]
                acc = alpha * acc + jnp.dot(vt, p.astype(BF16), preferred_element_type=F32)
                out.append((m_new, l, acc))
            return tuple(out)

        def put(slot, j):
            for h, st in enumerate(logits(j)):
                st_ref[2 * slot + h] = st

        def run(j0, steps, stats):
            for d in range(steps):
                put(1 - d % 2, j0 + d + 1)
                stats = softmax_pv(j0 + d, d % 2, stats, False)
            return stats

        init = tuple((jnp.full((1, tq), NEG, F32), jnp.zeros((1, tq), F32), jnp.zeros((HEAD_DIM, tq), F32))
                     for _ in range(2))
        put(0, 0)
        first, left, stats = 0, 2 * i, init
        for size in UNROLLS:
            trips = _shift_div(left, size)
            stats = lax.fori_loop(0, trips, lambda t, st, j0=first, n=size: run(j0 + n * t, n, st), stats)
            first, left = first + size * trips, left - size * trips
        put(1, 2 * i + 1)
        stats = softmax_pv(2 * i, 0, stats, True)
        (ma, la, acca), (mb, lb, accb) = softmax_pv(2 * i + 1, 1, stats, True)
        o_ref[...] = jnp.concatenate([acca / la, accb / lb], axis=0).T.astype(BF16)
        lse_ref[...] = jnp.where(_iota2((2, tq), 0) == 0, ma + jnp.log(la), mb + jnp.log(lb))

    return pl.pallas_call(
        body,
        grid=(N_PAIRS, s // tq),
        in_specs=[
            pl.BlockSpec((tq, LANES), lambda p, i: (i, p)),
            pl.BlockSpec((s, LANES), lambda p, i: (0, N_PAIRS + p)),
            pl.BlockSpec((s, LANES), lambda p, i: (0, 2 * N_PAIRS + p)),
            pl.BlockSpec((None, tq, 2), lambda p, i: (p, i, 0)),
            pl.BlockSpec((None, s, 2), lambda p, i: (p, 0, 0)),
        ],
        out_specs=[
            pl.BlockSpec((tq, LANES), lambda p, i: (i, p)),
            pl.BlockSpec((None, None, 2, tq), lambda p, i: (p, i, 0, 0)),
        ],
        out_shape=[SDS((s, ATTN_W), BF16), SDS((N_PAIRS, s // tq, 2, tq), F32)],
        scratch_shapes=[pltpu.VMEM((2 * nb, tk, LANES), BF16), pltpu.VMEM((nb, LANES, tk), BF16),
                        pltpu.VMEM((4, tk, tq), F32)],
        compiler_params=_cparams("arbitrary", "arbitrary"),
        name="attn_fwd",
    )(qkv, qkv, qkv, c_col, c_col)


def _pool_counts(row0, tm, w):
    t = row0 + _iota2((tm, 1), 0)
    return jnp.minimum(t + 1, w).astype(F32)


def _out_gate_up(attn_o, pool_o, w_out, x, g2, wg_t, wu_t, *, tm):
    s = x.shape[0]

    def body(a_ref, p_ref, wo_ref, x_ref, g_ref, wg_ref, wu_ref, x1_ref, h2_ref, r_ref, gate_ref, up_ref, act_ref):
        x1 = (x_ref[...] + jnp.dot(a_ref[...], wo_ref[0:ATTN_W, :], preferred_element_type=F32)
              + jnp.dot(p_ref[...], wo_ref[ATTN_W:, :], preferred_element_type=F32))
        r = lax.rsqrt(jnp.mean(x1 * x1, axis=-1, keepdims=True) + EPS)
        x1_ref[...] = x1
        r_ref[...] = r
        h2 = (x1 * r * g_ref[...]).astype(BF16)
        h2_ref[...] = h2
        gate = lax.dot_general(h2, wg_ref[...], NT, preferred_element_type=F32)
        up = lax.dot_general(h2, wu_ref[...], NT, preferred_element_type=F32)
        gate_ref[...] = gate.astype(BF16)
        up_ref[...] = up.astype(BF16)
        act_ref[...] = (gate * jax.nn.sigmoid(gate) * up).astype(BF16)

    row = lambda w: pl.BlockSpec((tm, w), lambda i: (i, 0))
    full = lambda a, b: pl.BlockSpec((a, b), lambda i: (0, 0))
    return pl.pallas_call(
        body,
        grid=(s // tm,),
        in_specs=[row(ATTN_W), row(POOL_W), full(D_MODEL, D_MODEL), row(D_MODEL), full(1, D_MODEL),
                  full(D_FF, D_MODEL), full(D_FF, D_MODEL)],
        out_specs=[row(D_MODEL), row(D_MODEL), row(1), row(D_FF), row(D_FF), row(D_FF)],
        out_shape=[SDS((s, D_MODEL), F32), SDS((s, D_MODEL), BF16), SDS((s, 1), F32), SDS((s, D_FF), BF16),
                   SDS((s, D_FF), BF16), SDS((s, D_FF), BF16)],
        compiler_params=_cparams("arbitrary"),
        name="out_gate_up",
    )(attn_o, pool_o, w_out, x, g2, wg_t, wu_t)


def _staggered(n, start, finish):
    pending = start(0)
    for k in range(n):
        following = start(k + 1) if k + 1 < n else None
        finish(k, pending)
        pending = following


def _down_final(act, wd, x1, gf, tgt, *, tm, sub):
    s = x1.shape[0]

    def body(a_ref, w_ref, x1_ref, g_ref, t_ref, dx2_ref, loss_ref, dgf_ref):
        @pl.when(pl.program_id(0) == 0)
        def _():
            loss_ref[...] = jnp.zeros_like(loss_ref)
            dgf_ref[...] = jnp.zeros_like(dgf_ref)

        g = g_ref[...]

        def matmul(k):
            return jnp.dot(a_ref[k * sub:(k + 1) * sub, :], w_ref[...], preferred_element_type=F32)

        def rest(k, mm):
            rows = slice(k * sub, (k + 1) * sub)
            x2 = x1_ref[rows, :] + mm
            r = lax.rsqrt(jnp.mean(x2 * x2, axis=-1, keepdims=True) + EPS)
            xn = x2 * r
            diff = xn * g - t_ref[rows, :]
            loss_ref[...] += jnp.sum(diff * diff, axis=0, keepdims=True)
            dy = diff * (1.0 / D_MODEL)
            dgf_ref[...] += jnp.sum(dy * xn, axis=0, keepdims=True)
            dxn = dy * g
            dx2_ref[rows, :] = r * (dxn - xn * jnp.mean(dxn * xn, axis=-1, keepdims=True))

        _staggered(tm // sub, matmul, rest)

    row = lambda w: pl.BlockSpec((tm, w), lambda i: (i, 0))
    full = lambda a, b: pl.BlockSpec((a, b), lambda i: (0, 0))
    return pl.pallas_call(
        body,
        grid=(s // tm,),
        in_specs=[row(D_FF), full(D_FF, D_MODEL), row(D_MODEL), full(1, D_MODEL), row(D_MODEL)],
        out_specs=[row(D_MODEL), full(1, D_MODEL), full(1, D_MODEL)],
        out_shape=[SDS((s, D_MODEL), F32), SDS((1, D_MODEL), F32), SDS((1, D_MODEL), F32)],
        compiler_params=_cparams("arbitrary"),
        name="down_final",
    )(act, wd, x1, gf, tgt)


def _swiglu_bwd(dx2, wd, gate, up, *, tm, tn):
    s = dx2.shape[0]

    def body(d_ref, w_ref, gate_ref, up_ref, dgate_ref, dup_ref):
        dact = lax.dot_general(d_ref[...].astype(BF16), w_ref[...], NT, preferred_element_type=F32)
        gate = gate_ref[...].astype(F32)
        sg = jax.nn.sigmoid(gate)
        dup_ref[...] = (dact * (gate * sg)).astype(BF16)
        dgate_ref[...] = (dact * up_ref[...].astype(F32) * (sg * (1.0 + gate * (1.0 - sg)))).astype(BF16)

    ospec = pl.BlockSpec((tm, tn), lambda c, r: (r, c))
    return pl.pallas_call(
        body,
        grid=(D_FF // tn, s // tm),
        in_specs=[pl.BlockSpec((tm, D_MODEL), lambda c, r: (r, 0)), pl.BlockSpec((tn, D_MODEL), lambda c, r: (c, 0)),
                  ospec, ospec],
        out_specs=[ospec, ospec],
        out_shape=[SDS((s, D_FF), BF16), SDS((s, D_FF), BF16)],
        compiler_params=_cparams("arbitrary", "arbitrary"),
        name="swiglu_bwd",
    )(dx2, wd, gate, up)


def _mm_tn_stacked(as_, rows, b, *, ts, name):
    s, nb_ = b.shape
    n = len(as_)
    offsets = [sum(rows[:i]) for i in range(n)]

    def body(*refs):
        a_refs, b_ref, o_ref, acc_ref = refs[:n], refs[n], refs[n + 1], refs[n + 2]
        k = pl.program_id(0)

        @pl.when(k == 0)
        def _():
            acc_ref[...] = jnp.zeros_like(acc_ref)

        bv = b_ref[...].astype(BF16)
        for a_ref, off, cnt in zip(a_refs, offsets, rows):
            part = lax.dot_general(a_ref[...].astype(BF16), bv, TN, preferred_element_type=F32)
            acc_ref[off:off + cnt, :] += part[0:cnt, :]

        @pl.when(k == s // ts - 1)
        def _():
            o_ref[...] = acc_ref[...].astype(BF16)

    return pl.pallas_call(
        body,
        grid=(s // ts,),
        in_specs=[pl.BlockSpec((ts, a.shape[1]), lambda k: (k, 0)) for a in as_] + [pl.BlockSpec((ts, nb_), lambda k: (k, 0))],
        out_specs=pl.BlockSpec((sum(rows), nb_), lambda k: (0, 0)),
        out_shape=SDS((sum(rows), nb_), BF16),
        scratch_shapes=[pltpu.VMEM((sum(rows), nb_), F32)],
        compiler_params=_cparams("arbitrary"),
        name=name,
    )(*as_, b)


def _norm_bwd(dh, x, r, g, dres):
    xn = x * r
    dxn = dh * g
    dx = dres + r * (dxn - xn * jnp.mean(dxn * xn, axis=-1, keepdims=True))
    return dx, jnp.sum(dh * xn, axis=0, keepdims=True)


def _mlp_in_pool_bwd(dgate, dup, wg_t, wu_t, w_out, x1, r2, g2, dx2, pooled, w_pool, pool_scale, *, tm):
    s = x1.shape[0]
    nt = s // tm
    ng = len(POOL_WINDOWS)

    def body(dg_ref, dup_ref, wg_ref, wu_ref, wo_ref, x_ref, r_ref, g_ref, d_ref, p_ref, w_ref, sc_ref,
             dx1_ref, dattn_ref, du_ref, dg2_ref, dw_ref, dsc_ref, head_ref):
        i = pl.program_id(0)

        @pl.when(i == 0)
        def _():
            dg2_ref[...] = jnp.zeros_like(dg2_ref)
            head_ref[...] = jnp.zeros_like(head_ref)
            dw_ref[...] = jnp.zeros_like(dw_ref)
            dsc_ref[...] = jnp.zeros_like(dsc_ref)

        dh2 = (jnp.dot(dg_ref[...], wg_ref[...], preferred_element_type=F32)
               + jnp.dot(dup_ref[...], wu_ref[...], preferred_element_type=F32))
        dx1, dg2 = _norm_bwd(dh2, x_ref[...], r_ref[...], g_ref[...], d_ref[...])
        dg2_ref[...] += dg2
        dx1_ref[...] = dx1
        dmix = lax.dot_general(dx1.astype(BF16), wo_ref[...], NT, preferred_element_type=F32)
        dattn_ref[...] = dmix[:, 0:ATTN_W]
        row0 = (nt - 1 - i) * tm
        for g, w in enumerate(POOL_WINDOWS):
            cols = slice(g * POOL_G, (g + 1) * POOL_G)
            wb = w_ref[g].astype(BF16)
            pooled_g = p_ref[:, cols]
            dpo = dmix[:, ATTN_W + g * POOL_G:ATTN_W + (g + 1) * POOL_G]
            mixed = jnp.dot(pooled_g, wb, preferred_element_type=F32)
            dsc_ref[:, cols] += jnp.sum(dpo * mixed, axis=0, keepdims=True)
            dmp = (dpo * sc_ref[:, cols]).astype(BF16)
            dw_ref[g] += lax.dot_general(pooled_g, dmp, TN, preferred_element_type=F32)
            dpooled = lax.dot_general(dmp, wb, NT, preferred_element_type=F32)
            a = dpooled / _pool_counts(row0, tm, w)
            acc = jnp.concatenate([a, head_ref[:, cols]], axis=0)
            head_ref[:, cols] = a[0:HALO, :]
            k = 1
            while k < w:
                acc = acc + pltpu.roll(acc, tm + HALO - k, axis=0)
                k *= 2
            du_ref[:, cols] = (acc[0:tm, :] - dpooled).astype(BF16)

    row = lambda w: pl.BlockSpec((tm, w), lambda i: (nt - 1 - i, 0))
    full = lambda a, b: pl.BlockSpec((a, b), lambda i: (0, 0))
    pool_w = pl.BlockSpec((ng, POOL_G, POOL_G), lambda i: (0, 0, 0))
    return pl.pallas_call(
        body,
        grid=(nt,),
        in_specs=[row(D_FF), row(D_FF), full(D_FF, D_MODEL), full(D_FF, D_MODEL), full(D_MODEL, D_MODEL),
                  row(D_MODEL), row(1), full(1, D_MODEL), row(D_MODEL), row(POOL_W), pool_w, full(1, POOL_W)],
        out_specs=[row(D_MODEL), row(ATTN_W), row(POOL_W), full(1, D_MODEL), pool_w, full(1, POOL_W)],
        out_shape=[SDS((s, D_MODEL), F32), SDS((s, ATTN_W), F32), SDS((s, POOL_W), BF16), SDS((1, D_MODEL), F32),
                   SDS((ng, POOL_G, POOL_G), F32), SDS((1, POOL_W), F32)],
        scratch_shapes=[pltpu.VMEM((HALO, POOL_W), F32)],
        compiler_params=_cparams("arbitrary"),
        name="mlp_in_pool_bwd",
    )(dgate, dup, wg_t, wu_t, w_out, x1, r2, g2, dx2, pooled, w_pool, pool_scale)


SUM_ROWS = 16


def _heads_t(t):
    n = t.shape[0]
    lane = _iota2((n, LANES), 1)
    tf = t.astype(F32)
    halves = jnp.concatenate([jnp.where(lane < HEAD_DIM, tf, 0.0).T, jnp.where(lane < HEAD_DIM, 0.0, tf).T], axis=1)
    r, c = _iota2((SUM_ROWS, 2 * n), 0), _iota2((SUM_ROWS, 2 * n), 1)
    ones = jnp.where(((r == 0) & (c < n)) | ((r == 4) & (c >= n)), 1.0, 0.0)
    return jnp.concatenate([halves, ones], axis=0).astype(BF16)


def _attn_bwd(qkv, attn_o, d_attn, rowb, ck_col, *, tq):
    s = qkv.shape[0]
    tk = tq
    nb = s // tq
    rows_t = LANES + SUM_ROWS

    def body(q_ref, k_ref, v_ref, o_ref, do_ref, rowb_ref, ck_ref, dq_ref, dk_ref, dv_ref, dck_ref, dcq_ref,
             dqt_ref, delta_ref, kp_ref, qp_ref, dob_ref, qt_ref, kt_ref, dot_ref, front_ref):
        lane = _iota2((tq, LANES), 1)
        lo = lane < HEAD_DIM
        first = _iota2((8, LANES), 1) < HEAD_DIM
        sel = jnp.where(_iota2((8, LANES), 0) < 4, jnp.where(first, 1.0, 0.0), jnp.where(first, 0.0, 1.0))

        def prep(b, _):
            st = pl.multiple_of(b * tq, tq)
            do2 = do_ref[pl.ds(st, tq), :]
            delta_ref[b] = _sel_dot(sel, do2 * o_ref[pl.ds(st, tq), :].astype(F32), NT)
            dob_ref[pl.ds(st, tq), :] = do2.astype(BF16)
            dqt_ref[b] = jnp.zeros((rows_t, tq), F32)
            k2 = k_ref[pl.ds(st, tq), :].astype(F32)
            q2 = q_ref[pl.ds(st, tq), :].astype(F32)
            ck = ck_ref[pl.ds(st, tq), :]
            for h in range(2):
                kp_ref[h * nb + b] = _augment(k2, h, -ck[:, h:h + 1], True)
                qp_ref[h * nb + b] = _augment(q2 * Q_SCALE, h, jnp.zeros((tq, 1), F32), False)
            qt_ref[b] = _heads_t(q2)
            kt_ref[b] = _heads_t(k2)
            dot_ref[b] = _heads_t(do2)[0:LANES, :]
            return 0

        lax.fori_loop(0, nb, prep, 0)

        def split(t):
            z = jnp.zeros_like(t)
            return jnp.where(lo, t, z), jnp.where(lo, z, t)

        def kv_block(j, _):
            st_j = pl.multiple_of(j * tk, tk)
            vs = split(v_ref[pl.ds(st_j, tk), :])
            kt = kt_ref[j]

            def stage(i, slot):
                ic = jnp.minimum(i, nb - 1)
                do2 = dob_ref[pl.ds(pl.multiple_of(ic * tq, tq), tq), :]
                for h in range(2):
                    front_ref[4 * slot + h] = lax.dot_general(kp_ref[h * nb + j], qp_ref[h * nb + ic], NT,
                                                              preferred_element_type=F32)
                    front_ref[4 * slot + 2 + h] = lax.dot_general(vs[h], do2, NT, preferred_element_type=F32)

            def q_block(i, slot, carry, diagonal):
                dkt, dvt = carry
                ic = jnp.minimum(i, nb - 1)
                rb = rowb_ref[ic] + jnp.where(i < nb, 0.0, NEG)
                dl = delta_ref[ic]
                pts, dsts = [], []
                for h in range(2):
                    st = front_ref[4 * slot + h] + rb[h:h + 1, :]
                    if diagonal:
                        st = jnp.where(_iota2((tk, tq), 0) <= _iota2((tk, tq), 1), st, NEG)
                    pt = jnp.exp(st)
                    pts.append(pt.astype(BF16))
                    dsts.append((pt * (front_ref[4 * slot + 2 + h] - dl[4 * h:4 * h + 1, :])).astype(BF16))
                dvt = dvt + lax.dot_general(dot_ref[ic], jnp.concatenate(pts, axis=1), NT, preferred_element_type=F32)
                dkt = dkt + lax.dot_general(qt_ref[ic], jnp.concatenate(dsts, axis=1), NT, preferred_element_type=F32)
                dqt_ref[ic] += jnp.dot(kt, jnp.concatenate(dsts, axis=0), preferred_element_type=F32)
                return dkt, dvt

            def run(i0, steps, carry):
                for d in range(steps):
                    stage(i0 + d + 1, d % 2)
                    carry = q_block(i0 + d, 1 - d % 2, carry, False)
                return carry

            stage(j, 0)
            stage(j + 1, 1)
            carry = q_block(j, 0, (jnp.zeros((rows_t, tk), F32), jnp.zeros((LANES, tk), F32)), True)
            first, left = j + 1, nb - 1 - j
            for size in UNROLLS:
                trips = _shift_div(left + 1 if size == UNROLLS[-1] else left, size)
                carry = lax.fori_loop(0, trips, lambda t, c, i0=first, n=size: run(i0 + n * t, n, c), carry)
                first, left = first + size * trips, left - size * trips
            dkt, dvt = carry
            dk_ref[pl.ds(st_j, tk), :] = (dkt[0:LANES, :].T * Q_SCALE).astype(BF16)
            dv_ref[pl.ds(st_j, tk), :] = dvt.T.astype(BF16)
            dck_ref[j] = dkt[LANES:LANES + 8, :]
            return 0

        lax.fori_loop(0, nb, kv_block, 0)

        def finish(b, _):
            acc = dqt_ref[b]
            dq_ref[pl.ds(pl.multiple_of(b * tq, tq), tq), :] = (acc[0:LANES, :].T * Q_SCALE).astype(BF16)
            dcq_ref[b] = acc[LANES:LANES + 8, :]
            return 0

        lax.fori_loop(0, nb, finish, 0)

    col = lambda off: pl.BlockSpec((s, LANES), lambda p: (0, off + p))
    sums = pl.BlockSpec((None, nb, 8, tq), lambda p: (p, 0, 0, 0))
    return pl.pallas_call(
        body,
        grid=(N_PAIRS,),
        in_specs=[col(0), col(N_PAIRS), col(2 * N_PAIRS), col(0), col(0),
                  pl.BlockSpec((None, nb, 2, tq), lambda p: (p, 0, 0, 0)),
                  pl.BlockSpec((None, s, 2), lambda p: (p, 0, 0))],
        out_specs=[col(0), col(0), col(0), sums, sums],
        out_shape=[SDS((s, ATTN_W), BF16), SDS((s, ATTN_W), BF16), SDS((s, ATTN_W), BF16),
                   SDS((N_PAIRS, nb, 8, tq), F32), SDS((N_PAIRS, nb, 8, tq), F32)],
        scratch_shapes=[pltpu.VMEM((nb, rows_t, tq), F32), pltpu.VMEM((nb, 8, tq), F32),
                        pltpu.VMEM((2 * nb, tk, LANES), BF16), pltpu.VMEM((2 * nb, tq, LANES), BF16),
                        pltpu.VMEM((s, LANES), BF16), pltpu.VMEM((nb, rows_t, 2 * tq), BF16),
                        pltpu.VMEM((nb, rows_t, 2 * tk), BF16), pltpu.VMEM((nb, LANES, 2 * tq), BF16),
                        pltpu.VMEM((8, tk, tq), F32)],
        compiler_params=_cparams("arbitrary"),
        name="attn_bwd",
    )(qkv, qkv, qkv, attn_o, d_attn, rowb, ck_col)


def _forget_bwd(dc_t, fl_t, b_rows):
    rows = fl_t.shape[0]
    nb = rows // N_HEADS

    def body(dc_ref, fl_ref, b_ref, dfl_ref, db_ref):
        dc = dc_ref[...]
        lower = _iota2((LANES, LANES), 0) >= _iota2((LANES, LANES), 1)
        ones = jnp.ones((LANES, LANES), F32)
        rr, cc, same = _head_block_masks(rows, nb)
        dlf = _dot_sel(dc, lower) + _sel_dot(same & (cc > rr), _dot_sel(dc, ones))
        dfl = dlf / (1.0 + jnp.exp(fl_ref[...] + b_ref[...]))
        dfl_ref[...] = dfl
        shift = nb.bit_length() - 1
        hsel = lax.shift_right_logical(_iota2((N_HEADS, rows), 1), shift) == _iota2((N_HEADS, rows), 0)
        db_ref[...] = _sel_dot(hsel, _dot_sel(dfl, ones))

    return pl.pallas_call(body, out_shape=[SDS(fl_t.shape, F32), SDS((N_HEADS, LANES), F32)],
                          compiler_params=_cparams(), name="forget_bwd")(dc_t, fl_t, b_rows)


def _in_bwd(dq, dk, dv, du, dfl, w_in_t, x, r1, g1, dx1, *, tm):
    s = x.shape[0]
    pieces = ((0, ATTN_W), (ATTN_W, 2 * ATTN_W), (2 * ATTN_W, QKV_W), (U_OFF, F_OFF), (F_OFF, IN_PAD))

    def body(dq_ref, dk_ref, dv_ref, du_ref, df_ref, w_ref, x_ref, r_ref, g_ref, d_ref, dx_ref, dg1_ref):
        @pl.when(pl.program_id(0) == 0)
        def _():
            dg1_ref[...] = jnp.zeros_like(dg1_ref)

        dh = None
        for ref, (c0, c1) in zip((dq_ref, dk_ref, dv_ref, du_ref, df_ref), pieces):
            t = jnp.dot(ref[...], w_ref[c0:c1, :], preferred_element_type=F32)
            dh = t if dh is None else dh + t
        dx, dg1 = _norm_bwd(dh, x_ref[...], r_ref[...], g_ref[...], d_ref[...])
        dx_ref[...] = dx
        dg1_ref[...] += dg1

    row = lambda w: pl.BlockSpec((tm, w), lambda i: (i, 0))
    full = lambda a, b: pl.BlockSpec((a, b), lambda i: (0, 0))
    return pl.pallas_call(
        body,
        grid=(s // tm,),
        in_specs=[row(ATTN_W), row(ATTN_W), row(ATTN_W), row(POOL_W), row(LANES), full(IN_PAD, D_MODEL),
                  row(D_MODEL), row(1), full(1, D_MODEL), row(D_MODEL)],
        out_specs=[row(D_MODEL), full(1, D_MODEL)],
        out_shape=[SDS((s, D_MODEL), F32), SDS((1, D_MODEL), F32)],
        compiler_params=_cparams("arbitrary"),
        name="in_bwd",
    )(dq, dk, dv, du, dfl, w_in_t, x, r1, g1, dx1)


def _tiles(s):
    big = min(512, s)
    return dict(row=big, attn=min(256, s // 2), ff_rows=min(256, s), tall=min(1024, s))


def _tie(a, token):
    return a + token[0:1, 0:1].astype(a.dtype)


def _local_step(x, tgt, p, weight, emit, started):
    s = x.shape[0]
    t = _tiles(s)
    tm, tq = t["row"], t["attn"]
    nb = s // LANES
    nqb = s // tq
    g1, g2, gf = p["norm1_g"], p["norm2_g"], p["final_g"].reshape(1, D_MODEL)
    w_pool, pool_scale = p["w_pool"][0], p["pool_scale"]

    h, r1 = _norm1(x, _tie(g1, started), tm=tm)
    w_in_t = weight("w_in", h)
    qkv, fl, pooled, pool_o = _in_proj_pool(h, w_in_t, w_pool, pool_scale, tm=t["tall"])
    fl_t = fl[:, :N_HEADS].T.reshape(N_HEADS * nb, LANES)
    b_rows = jnp.repeat(p["b_forget"].reshape(N_HEADS), nb).reshape(N_HEADS * nb, 1)
    c = _forget_cumsum(fl_t, b_rows).reshape(N_PAIRS, 2, s)
    c_col = c.transpose(0, 2, 1)
    c_rowblk = c.reshape(N_PAIRS, 2, nqb, tq).transpose(0, 2, 1, 3)
    attn_o, lse = _attn_fwd(qkv, c_col, tk=tq)
    lse = lse.reshape(N_PAIRS, nqb // 2, 2, 2, tq).transpose(0, 1, 3, 2, 4).reshape(N_PAIRS, nqb, 2, tq)
    w_out = weight("w_out", attn_o)
    wg_t, wu_t = weight("w_gate_up", attn_o)
    x1, h2, r2, gate, up, act = _out_gate_up(attn_o, pool_o, w_out, x, g2, wg_t, wu_t, tm=t["ff_rows"])
    wd = weight("w_down", act)
    dx2, loss_row, d_gf = _down_final(act, wd, x1, gf, tgt, tm=tm, sub=min(128, tm))

    dgate, dup = _swiglu_bwd(dx2, wd, gate, up, tm=t["ff_rows"], tn=D_FF)
    d_wd = _mm_tn_stacked([act], [D_FF], dx2, ts=t["tall"], name="grad_w_down")
    d_wg_t = _mm_tn_stacked([dgate], [D_FF], h2, ts=t["tall"], name="grad_w_gate")
    d_wu_t = _mm_tn_stacked([dup], [D_FF], h2, ts=t["tall"], name="grad_w_up")
    dx1, d_attn, du, d_g2, d_wpool, d_pscale = _mlp_in_pool_bwd(dgate, dup, wg_t, wu_t, w_out, x1, r2, g2, dx2, pooled,
                                                               w_pool, pool_scale, tm=t["ff_rows"])
    d_wo = _mm_tn_stacked([attn_o, pool_o], [ATTN_W, POOL_W], dx1, ts=t["tall"], name="grad_w_out")
    token = emit(("w_down", "w_gate", "w_up", "w_out"), (d_wd, d_wg_t, d_wu_t, d_wo))
    rowb = _tie(c_rowblk - lse, token)
    dq, dk, dv, dck, dcq = _attn_bwd(qkv, attn_o, d_attn, rowb, c_col, tq=tq)
    dc_t = (dcq - dck)[:, :, 0::4, :].transpose(0, 2, 1, 3).reshape(N_HEADS * nb, LANES)
    dfl_t, db = _forget_bwd(dc_t, fl_t, b_rows)
    dfl = jnp.pad(dfl_t.reshape(N_HEADS, s).T, ((0, 0), (0, LANES - N_HEADS))).astype(BF16)
    d_w_in_t = _mm_tn_stacked([dq, dk, dv, dfl, du], [ATTN_W, ATTN_W, ATTN_W, N_HEADS, POOL_W], h, ts=t["tall"],
                              name="grad_w_in")
    token = emit(("w_in",), (d_w_in_t,))
    dx, d_g1 = _in_bwd(dq, dk, dv, du, dfl, w_in_t, x, r1, _tie(g1, token), dx1, tm=tm)

    small = dict(norm1_g=d_g1, b_forget=db[:, 0].reshape(1, N_HEADS), w_pool=d_wpool, pool_scale=d_pscale,
                 norm2_g=d_g2, final_g=d_gf)
    return loss_row, dx, small


def _my_index():
    return 4 * lax.axis_index("x") + 2 * lax.axis_index("y") + lax.axis_index("c")


def _peer(k):
    pos = [lax.axis_index(a) for a in ("x", "y", "c")]
    flipped = tuple(1 - p if (k >> b) & 1 else p for p, b in zip(pos, (2, 1, 0)))
    return flipped, 4 * flipped[0] + 2 * flipped[1] + flipped[2]


_HBM = pl.BlockSpec(memory_space=pltpu.HBM)
_SEM = pl.BlockSpec(memory_space=pltpu.SEMAPHORE)
_DATAFLOW = pltpu.SideEffectType.DATAFLOW_SIDE_EFFECTING


ALL_PEERS = tuple(range(1, N_DEV))
SAME_CORE = (1, 2, 4, 6)


def _peer_copies(ins, lands, send_sems, recv_sems, scatter, peers, arrivals):
    me = _my_index()
    copies = []
    for w in range(len(ins)):
        for k in peers[w]:
            dev, idx = _peer(k)
            copies.append(pltpu.make_async_remote_copy(
                src_ref=ins[w].at[idx] if scatter[w] else ins[w], dst_ref=lands[w].at[idx if arrivals else me],
                send_sem=send_sems[w].at[k - 1], recv_sem=recv_sems[w].at[k - 1], device_id=dev, device_id_type=MESH))
    return copies


def _own_copies(ins, lands, send_sems, scatter):
    me = _my_index()
    return [pltpu.make_async_copy(ins[w].at[me] if scatter[w] else ins[w], lands[w].at[me], send_sems[w].at[N_DEV - 1])
            for w in range(len(ins))]


def _forward_copies(land, send_sems, recv_sems, arrivals):
    sibling, _ = _peer(1)
    copies = []
    for j, k in enumerate(SAME_CORE[1:]):
        src, dst = _peer(k)[1], _peer(k ^ 1 if arrivals else k)[1]
        copies.append(pltpu.make_async_remote_copy(
            src_ref=land.at[src], dst_ref=land.at[dst], send_sem=send_sems.at[j], recv_sem=recv_sems.at[j],
            device_id=sibling, device_id_type=MESH))
    return copies


def _forward_start(land, name):
    def body(land_ref, send_sems, recv_sems, land_thru, token):
        for cp in _forward_copies(land_ref, send_sems, recv_sems, False):
            cp.start()
        token[...] = jnp.zeros_like(token)

    sem = pltpu.SemaphoreType.DMA((len(SAME_CORE) - 1,))
    send, recv, thru, _ = pl.pallas_call(
        body,
        in_specs=[_HBM],
        out_specs=[_SEM, _SEM, _HBM, pl.BlockSpec(memory_space=pltpu.VMEM)],
        out_shape=[sem, sem, pltpu.HBM(land.shape, land.dtype), SDS((8, LANES), F32)],
        input_output_aliases={0: 2},
        compiler_params=pltpu.CompilerParams(has_side_effects=_DATAFLOW),
        name=name,
    )(land)
    return send, recv, thru


def _forward_wait(handle, after, name):
    def body(land_ref, send_sems, recv_sems, after_ref, land_out):
        for cp in _forward_copies(land_ref, send_sems, recv_sems, False):
            cp.wait_send()
        for cp in _forward_copies(land_ref, send_sems, recv_sems, True):
            cp.wait_recv()

    send, recv, land = handle
    return pl.pallas_call(
        body,
        in_specs=[_HBM, _SEM, _SEM, pl.BlockSpec(memory_space=pl.ANY)],
        out_specs=_HBM,
        out_shape=pltpu.HBM(land.shape, land.dtype),
        input_output_aliases={0: 0},
        compiler_params=pltpu.CompilerParams(has_side_effects=_DATAFLOW),
        name=name,
    )(land, send, recv, after)


def _exchange_start(arrays, scatter, name, peers=None):
    n = len(arrays)
    peers = peers or [ALL_PEERS] * n
    land_shapes = [(N_DEV,) + tuple(a.shape[1:] if sc else a.shape) for a, sc in zip(arrays, scatter)]

    def body(*refs):
        ins, lands = refs[:n], refs[n:2 * n]
        send_sems, recv_sems = refs[2 * n:3 * n], refs[3 * n:4 * n]
        token = refs[6 * n]
        for cp in _peer_copies(ins, lands, send_sems, recv_sems, scatter, peers, False):
            cp.start()
        for cp in _own_copies(ins, lands, send_sems, scatter):
            cp.start()
        token[...] = jnp.zeros_like(token)

    sends, recvs = pltpu.SemaphoreType.DMA((N_DEV,)), pltpu.SemaphoreType.DMA((N_DEV - 1,))
    outs = pl.pallas_call(
        body,
        in_specs=[_HBM] * (2 * n),
        out_specs=[_SEM] * (2 * n) + [_HBM] * (2 * n) + [pl.BlockSpec(memory_space=pltpu.VMEM)],
        out_shape=[sends] * n + [recvs] * n + [pltpu.HBM(a.shape, a.dtype) for a in arrays]
        + [pltpu.HBM(sh, a.dtype) for sh, a in zip(land_shapes, arrays)] + [SDS((8, LANES), F32)],
        input_output_aliases={i: 2 * n + i for i in range(2 * n)},
        compiler_params=pltpu.CompilerParams(has_side_effects=_DATAFLOW),
        name=name,
    )(*[pltpu.with_memory_space_constraint(a, pltpu.HBM) for a in arrays],
      *[pltpu.with_memory_space_constraint(lax.empty(sh, a.dtype), pltpu.HBM) for sh, a in zip(land_shapes, arrays)])
    handles = [dict(send=outs[w], recv=outs[n + w], src=outs[2 * n + w], land=outs[3 * n + w], scatter=scatter[w],
                    peers=peers[w]) for w in range(n)]
    return handles, outs[4 * n]


def _exchange_wait(handles, after, name):
    n = len(handles)
    scatter, peers = [h["scatter"] for h in handles], [h["peers"] for h in handles]

    def body(*refs):
        ins, lands = refs[:n], refs[n:2 * n]
        send_sems, recv_sems = refs[2 * n:3 * n], refs[3 * n:4 * n]
        for cp in _peer_copies(ins, lands, send_sems, recv_sems, scatter, peers, False):
            cp.wait_send()
        for cp in _peer_copies(ins, lands, send_sems, recv_sems, scatter, peers, True):
            cp.wait_recv()
        for cp in _own_copies(ins, lands, send_sems, scatter):
            cp.wait()

    srcs, lands = [h["src"] for h in handles], [h["land"] for h in handles]
    outs = pl.pallas_call(
        body,
        in_specs=[_HBM] * (2 * n) + [_SEM] * (2 * n) + [pl.BlockSpec(memory_space=pl.ANY)],
        out_specs=[_HBM] * (2 * n),
        out_shape=[pltpu.HBM(a.shape, a.dtype) for a in srcs + lands],
        input_output_aliases={i: i for i in range(2 * n)},
        compiler_params=pltpu.CompilerParams(has_side_effects=_DATAFLOW),
        name=name,
    )(*srcs, *lands, *[h["send"] for h in handles], *[h["recv"] for h in handles], after)
    return outs[n:]


def _adamw(parts, w, m, v, name):
    rows, cols = w.shape
    tr = rows // 4 if rows % 32 == 0 else rows

    def body(p_ref, w_ref, m_ref, v_ref, g_ref, d_ref, mo_ref, vo_ref):
        g = p_ref[0].astype(F32)
        for d in range(1, N_DEV):
            g = g + p_ref[d].astype(F32)
        g_ref[...] = g
        d_ref[...], mo_ref[...], vo_ref[...] = _adam_update(g, w_ref[...], m_ref[...], v_ref[...])

    blk = pl.BlockSpec((tr, cols), lambda i: (i, 0))
    return pl.pallas_call(
        body,
        grid=(rows // tr,),
        in_specs=[pl.BlockSpec((N_DEV, tr, cols), lambda i: (0, i, 0)), blk, blk, blk],
        out_specs=[blk] * 4,
        out_shape=[SDS((rows, cols), F32)] * 4,
        compiler_params=_cparams("arbitrary"),
        name=name,
    )(parts, w, m, v)


_ROW_OF = dict(norm1_g=(0, D_MODEL), norm2_g=(1, D_MODEL), final_g=(2, D_MODEL), pool_scale=(3, POOL_W),
               b_forget=(4, N_HEADS), loss=(5, 1))


def _pack_rows(vals):
    rows = [jnp.pad(vals[n].reshape(1, width).astype(F32), ((0, 0), (0, D_MODEL - width)))
            for n, (_, width) in sorted(_ROW_OF.items(), key=lambda kv: kv[1][0])]
    return jnp.concatenate(rows + [jnp.zeros((8 - len(rows), D_MODEL), F32)], axis=0)


def _adam_update(g, w, m, v):
    m_new = ADAM_B1 * m + (1.0 - ADAM_B1) * g
    v_new = ADAM_B2 * v + (1.0 - ADAM_B2) * (g * g)
    m_hat = m_new / (1.0 - ADAM_B1 ** ADAM_STEP)
    v_hat = v_new / (1.0 - ADAM_B2 ** ADAM_STEP)
    return -ADAM_LR * (m_hat / (jnp.sqrt(v_hat) + ADAM_EPS) + ADAM_WD * w), m_new, v_new


def _adamw_replicated(parts_rows, parts_pool, w, m, v):
    names = ("norm1_g", "norm2_g", "final_g", "pool_scale", "b_forget", "w_pool")
    shapes = {n: ((len(POOL_WINDOWS), POOL_G, POOL_G) if n == "w_pool" else (1, _ROW_OF[n][1])) for n in names}

    def body(rows_ref, pool_ref, *refs):
        ins, outs = refs[:3 * len(names)], refs[3 * len(names):]

        def total(n):
            if n == "w_pool":
                pieces = [pool_ref[d] for d in range(N_DEV)]
            else:
                row, width = _ROW_OF[n]
                pieces = [rows_ref[d, row:row + 1, 0:width] for d in range(N_DEV)]
            g = pieces[0]
            for p in pieces[1:]:
                g = g + p
            return g

        outs[0][...] = total("loss")
        for k, n in enumerate(names):
            g = total(n)
            delta, m_new, v_new = _adam_update(g, ins[3 * k][...], ins[3 * k + 1][...], ins[3 * k + 2][...])
            for o_ref, val in zip(outs[1 + 4 * k:5 + 4 * k], (g, delta, m_new, v_new)):
                o_ref[...] = val

    args = [d[n].reshape(shapes[n]) for n in names for d in (w, m, v)]
    res = pl.pallas_call(
        body,
        out_shape=[SDS((1, 1), F32)] + [SDS(shapes[n], F32) for n in names for _ in range(4)],
        compiler_params=_cparams(),
        name="adamw_replicated",
    )(parts_rows, parts_pool, *args)
    return res[0], {n: [r.reshape(w[n].shape) for r in res[1 + 4 * k:5 + 4 * k]] for k, n in enumerate(names)}


def kernel(x, norm1_g, w_in, b_forget, w_pool, pool_scale, w_out, norm2_g, w_gate, w_up, w_down, final_g, loss_target, m_norm1_g, m_w_in, m_b_forget, m_w_pool, m_pool_scale, m_w_out, m_norm2_g, m_w_gate, m_w_up, m_w_down, m_final_g, v_norm1_g, v_w_in, v_b_forget, v_w_pool, v_pool_scale, v_w_out, v_norm2_g, v_w_gate, v_w_up, v_w_down, v_final_g):
    big = ("w_in", "w_out", "w_gate", "w_up", "w_down")
    order = ("norm1_g", "w_in", "b_forget", "w_pool", "pool_scale", "w_out", "norm2_g", "w_gate", "w_up", "w_down",
             "final_g")
    w = dict(norm1_g=norm1_g, w_in=w_in, b_forget=b_forget, w_pool=w_pool, pool_scale=pool_scale, w_out=w_out,
             norm2_g=norm2_g, w_gate=w_gate, w_up=w_up, w_down=w_down, final_g=final_g)
    m = dict(norm1_g=m_norm1_g, w_in=m_w_in, b_forget=m_b_forget, w_pool=m_w_pool, pool_scale=m_pool_scale,
             w_out=m_w_out, norm2_g=m_norm2_g, w_gate=m_w_gate, w_up=m_w_up, w_down=m_w_down, final_g=m_final_g)
    v = dict(norm1_g=v_norm1_g, w_in=v_w_in, b_forget=v_b_forget, w_pool=v_w_pool, pool_scale=v_pool_scale,
             w_out=v_w_out, norm2_g=v_norm2_g, w_gate=v_w_gate, w_up=v_w_up, w_down=v_w_down, final_g=v_final_g)

    flipped = ("w_in", "w_gate", "w_up")
    shard = lambda d, n: d[n][0].T if n in flipped else d[n][0]
    gather, started = _exchange_start([shard(w, n).astype(BF16) for n in big], [False] * len(big), "gather_start",
                                      peers=[SAME_CORE if n == "w_in" else ALL_PEERS for n in big])
    gather = dict(zip(big, gather))

    def gathered(names, after):
        return _exchange_wait([gather[n] for n in names], after, "gather_wait_" + names[0])

    def weight(name, after):
        if name == "w_in":
            forward = _forward_start(gathered(["w_in"], after)[0], "gather_forward_start")
            full = _forward_wait(forward, after, "gather_forward_wait").reshape(IN_W, D_MODEL)
            f0 = QKV_W + N_HEADS
            return jnp.concatenate([full[:QKV_W], full[f0:], full[QKV_W:f0],
                                    jnp.zeros((IN_PAD - IN_W, D_MODEL), BF16)], axis=0)
        if name == "w_out":
            return gathered(["w_out"], after)[0].reshape(D_MODEL, D_MODEL)
        if name == "w_gate_up":
            return [g.reshape(D_FF, D_MODEL) for g in gathered(["w_gate", "w_up"], after)]
        return gathered(["w_down"], after)[0].reshape(D_FF, D_MODEL)

    rows = lambda g: g.reshape(N_DEV, g.shape[0] // N_DEV, g.shape[1])
    sent = {}

    def emit(names, grads):
        handles, token = _exchange_start([rows(g) for g in grads], [True] * len(names), "grads_start_" + names[0])
        sent.update(zip(names, handles))
        return token

    loss_row, dx, small_grads = _local_step(x[0], loss_target[0], w, weight, emit, started)

    packed = _pack_rows(dict(small_grads, loss=0.5 / D_MODEL * jnp.sum(loss_row)))
    small_handles, after = _exchange_start([packed, small_grads["w_pool"]], [False, False], "grads_start_replicated")

    outs = {}
    for name in ("w_down", "w_gate", "w_up", "w_out", "w_in"):
        (parts,) = _exchange_wait([sent[name]], after, "grads_wait_" + name)
        outs[name] = _adamw(parts, shard(w, name), shard(m, name), shard(v, name), "adamw_" + name)
        after = outs[name][0]
        outs[name] = [(a.T if name in flipped else a)[None] for a in outs[name]]
    parts_rows, parts_pool = _exchange_wait(small_handles, after, "grads_wait_replicated")
    loss, small = _adamw_replicated(parts_rows, parts_pool, w, m, v)
    outs.update(small)

    return (loss.reshape(()), dx[None]) + tuple(outs[n][k] for k in range(4) for n in order)
```

```python
import jax
import jax.numpy as jnp
from jax import lax
from jax.experimental import pallas as pl
from jax.experimental.pallas import tpu as pltpu

F32 = jnp.float32
BF16 = jnp.bfloat16
SDS = jax.ShapeDtypeStruct

D_MODEL = 1024
ATTN_W = 512
N_HEADS = 8
HEAD_DIM = 64
Q_SCALE = HEAD_DIM ** -0.5
N_PAIRS = N_HEADS // 2
POOL_W = 512
POOL_WINDOWS = (2, 4, 8, 16)
POOL_G = 128
HALO = 16
IN_W = 3 * ATTN_W + N_HEADS + POOL_W
QKV_W = 3 * ATTN_W
U_OFF = QKV_W
F_OFF = QKV_W + POOL_W
IN_PAD = F_OFF + 128
D_FF = 2816
EPS = 1e-6
NEG = -1e30
N_DEV = 8
LANES = 128

ADAM_LR = 0.001
ADAM_B1 = 0.9
ADAM_B2 = 0.999
ADAM_EPS = 1e-08
ADAM_WD = 0.01
ADAM_STEP = 10

VMEM_LIMIT_BYTES = 56 * 1024 * 1024
MESH = pl.DeviceIdType.MESH
NT = (((1,), (1,)), ((), ()))
TN = (((0,), (0,)), ((), ()))


def _cparams(*sem):
    return pltpu.CompilerParams(dimension_semantics=sem or None, vmem_limit_bytes=VMEM_LIMIT_BYTES)


def _split3(a):
    hi = a.astype(BF16)
    r1 = a - hi.astype(F32)
    mid = r1.astype(BF16)
    lo = (r1 - mid.astype(F32)).astype(BF16)
    return hi, mid, lo


def _dot_sel(a, sel, dims=None):
    sb = sel.astype(BF16)
    if dims is None:
        return sum(jnp.dot(p, sb, preferred_element_type=F32) for p in _split3(a))
    return sum(lax.dot_general(p, sb, dims, preferred_element_type=F32) for p in _split3(a))


def _sel_dot(sel, a, dims=None):
    sb = sel.astype(BF16)
    if dims is None:
        return sum(jnp.dot(sb, p, preferred_element_type=F32) for p in _split3(a))
    return sum(lax.dot_general(sb, p, dims, preferred_element_type=F32) for p in _split3(a))


def _iota2(shape, dim):
    return lax.broadcasted_iota(jnp.int32, shape, dim)


UNROLLS = (8, 4, 2)


def _shift_div(x, n):
    return lax.shift_right_logical(x, n.bit_length() - 1)


def _norm1(x, g1, *, tm):
    s = x.shape[0]

    def body(x_ref, g_ref, h_ref, r_ref):
        xv = x_ref[...]
        r = lax.rsqrt(jnp.mean(xv * xv, axis=-1, keepdims=True) + EPS)
        h_ref[...] = (xv * r * g_ref[...]).astype(BF16)
        r_ref[...] = r

    row = lambda w: pl.BlockSpec((tm, w), lambda i: (i, 0))
    return pl.pallas_call(
        body,
        grid=(s // tm,),
        in_specs=[row(D_MODEL), pl.BlockSpec((1, D_MODEL), lambda i: (0, 0))],
        out_specs=[row(D_MODEL), row(1)],
        out_shape=[SDS((s, D_MODEL), BF16), SDS((s, 1), F32)],
        compiler_params=_cparams("arbitrary"),
        name="norm1",
    )(x, g1)


def _in_proj_pool(h, w_in_t, w_pool, pool_scale, *, tm):
    s = h.shape[0]

    def body(h_ref, w_ref, wp_ref, sc_ref, qkv_ref, fl_ref, pooled_ref, po_ref, tail_ref):
        i = pl.program_id(0)

        @pl.when(i == 0)
        def _():
            tail_ref[...] = jnp.zeros_like(tail_ref)

        hv = h_ref[...]
        uv = lax.dot_general(hv, w_ref[U_OFF:F_OFF, :], NT, preferred_element_type=F32)
        qkv_ref[...] = lax.dot_general(hv, w_ref[0:QKV_W, :], NT, preferred_element_type=F32).astype(BF16)
        fl_ref[...] = lax.dot_general(hv, w_ref[F_OFF:IN_PAD, :], NT, preferred_element_type=F32)
        ext = jnp.concatenate([tail_ref[...], uv], axis=0)
        tail_ref[...] = uv[tm - HALO:, :]
        for g, w in enumerate(POOL_WINDOWS):
            cols = slice(g * POOL_G, (g + 1) * POOL_G)
            acc = ext[:, cols]
            k = 1
            while k < w:
                acc = acc + pltpu.roll(acc, k, axis=0)
                k *= 2
            pooled = (acc[HALO:, :] / _pool_counts(i * tm, tm, w) - uv[:, cols]).astype(BF16)
            pooled_ref[:, cols] = pooled
            mixed = jnp.dot(pooled, wp_ref[g].astype(BF16), preferred_element_type=F32)
            po_ref[:, cols] = (mixed * sc_ref[:, cols]).astype(BF16)

    row = lambda w: pl.BlockSpec((tm, w), lambda i: (i, 0))
    return pl.pallas_call(
        body,
        grid=(s // tm,),
        in_specs=[row(D_MODEL), pl.BlockSpec((IN_PAD, D_MODEL), lambda i: (0, 0)),
                  pl.BlockSpec((len(POOL_WINDOWS), POOL_G, POOL_G), lambda i: (0, 0, 0)),
                  pl.BlockSpec((1, POOL_W), lambda i: (0, 0))],
        out_specs=[row(QKV_W), row(LANES), row(POOL_W), row(POOL_W)],
        out_shape=[SDS((s, QKV_W), BF16), SDS((s, LANES), F32), SDS((s, POOL_W), BF16), SDS((s, POOL_W), BF16)],
        scratch_shapes=[pltpu.VMEM((HALO, POOL_W), F32)],
        compiler_params=_cparams("arbitrary"),
        name="in_proj_pool",
    )(h, w_in_t, w_pool, pool_scale)


def _head_block_masks(rows, nb):
    shift = nb.bit_length() - 1
    rr, cc = _iota2((rows, rows), 0), _iota2((rows, rows), 1)
    same = lax.shift_right_logical(rr, shift) == lax.shift_right_logical(cc, shift)
    return rr, cc, same


def _forget_cumsum(fl_t, b_rows):
    rows = fl_t.shape[0]
    nb = rows // N_HEADS

    def body(fl_ref, b_ref, c_ref):
        z = fl_ref[...] + b_ref[...]
        lf = jnp.minimum(z, 0.0) - jnp.log1p(jnp.exp(-jnp.abs(z)))
        upper = _iota2((LANES, LANES), 0) <= _iota2((LANES, LANES), 1)
        within = _dot_sel(lf, upper)
        tot = _dot_sel(lf, jnp.ones((LANES, LANES), F32))
        rr, cc, same = _head_block_masks(rows, nb)
        c_ref[...] = within + _sel_dot(same & (cc < rr), tot)

    return pl.pallas_call(body, out_shape=SDS(fl_t.shape, F32), compiler_params=_cparams(), name="forget_cumsum")(
        fl_t, b_rows)


BIAS_LANES = 3


def _augment(t, h, col, col_first):
    n = t.shape[0]
    lane = _iota2((n, LANES), 1)
    own = (lane < HEAD_DIM) if h == 0 else (lane >= HEAD_DIM)
    b0 = HEAD_DIM if h == 0 else 0
    c0, o0 = (b0, b0 + BIAS_LANES) if col_first else (b0 + BIAS_LANES, b0)
    x = jnp.where(own, t, 0.0)
    for off, piece in enumerate(_split3(col)):
        x = jnp.where(lane == c0 + off, piece.astype(F32), x)
    x = jnp.where((lane >= o0) & (lane < o0 + BIAS_LANES), 1.0, x)
    return x.astype(BF16)


def _attn_fwd(qkv, c_col, *, tk):
    s = qkv.shape[0]
    tq = 2 * tk
    nb = s // tk

    def body(q_ref, k_ref, v_ref, cq_ref, ck_ref, o_ref, lse_ref, kp_ref, vt_ref, st_ref):
        i = pl.program_id(1)

        @pl.when(i == 0)
        def _():
            def prep(jb, _):
                st = pl.multiple_of(jb * tk, tk)
                k2 = k_ref[pl.ds(st, tk), :].astype(F32)
                ck = ck_ref[pl.ds(st, tk), :]
                for h in range(2):
                    kp_ref[h * nb + jb] = _augment(k2, h, -ck[:, h:h + 1], True)
                vt_ref[jb] = v_ref[pl.ds(st, tk), :].astype(F32).T.astype(BF16)
                return 0

            lax.fori_loop(0, nb, prep, 0)

        qs = q_ref[...].astype(F32) * Q_SCALE
        cq = cq_ref[...]
        qp = [_augment(qs, h, cq[:, h:h + 1], False) for h in range(2)]

        def logits(j):
            return tuple(lax.dot_general(kp_ref[h * nb + j], qp[h], NT, preferred_element_type=F32) for h in range(2))

        def softmax_pv(j, slot, stats, masked):
            out = []
            for h in range(2):
                m, l, acc = stats[h]
                st = st_ref[2 * slot + h]
                if masked:
                    st = jnp.where(j * tk + _iota2((tk, tq), 0) <= i * tq + _iota2((tk, tq), 1), st, NEG)
                m_new = jnp.maximum(m, jnp.max(st, axis=0, keepdims=True))
                alpha = jnp.exp(m - m_new)
                p = jnp.exp(st - m_new)
                l = alpha * l + jnp.sum(p, axis=0, keepdims=True)
                vt = vt_ref[j, h * HEAD_DIM:(h + 1) * HEAD_DIM, :]
                acc = alpha * acc + jnp.dot(vt, p.astype(BF16), preferred_element_type=F32)
                out.append((m_new, l, acc))
            return tuple(out)

        def put(slot, j):
            for h, st in enumerate(logits(j)):
                st_ref[2 * slot + h] = st

        def run(j0, steps, stats):
            for d in range(steps):
                put(1 - d % 2, j0 + d + 1)
                stats = softmax_pv(j0 + d, d % 2, stats, False)
            return stats

        init = tuple((jnp.full((1, tq), NEG, F32), jnp.zeros((1, tq), F32), jnp.zeros((HEAD_DIM, tq), F32))
                     for _ in range(2))
        put(0, 0)
        first, left, stats = 0, 2 * i, init
        for size in UNROLLS:
            trips = _shift_div(left, size)
            stats = lax.fori_loop(0, trips, lambda t, st, j0=first, n=size: run(j0 + n * t, n, st), stats)
            first, left = first + size * trips, left - size * trips
        put(1, 2 * i + 1)
        stats = softmax_pv(2 * i, 0, stats, True)
        (ma, la, acca), (mb, lb, accb) = softmax_pv(2 * i + 1, 1, stats, True)
        o_ref[...] = jnp.concatenate([acca / la, accb / lb], axis=0).T.astype(BF16)
        lse_ref[...] = jnp.where(_iota2((2, tq), 0) == 0, ma + jnp.log(la), mb + jnp.log(lb))

    return pl.pallas_call(
        body,
        grid=(N_PAIRS, s // tq),
        in_specs=[
            pl.BlockSpec((tq, LANES), lambda p, i: (i, p)),
            pl.BlockSpec((s, LANES), lambda p, i: (0, N_PAIRS + p)),
            pl.BlockSpec((s, LANES), lambda p, i: (0, 2 * N_PAIRS + p)),
            pl.BlockSpec((None, tq, 2), lambda p, i: (p, i, 0)),
            pl.BlockSpec((None, s, 2), lambda p, i: (p, 0, 0)),
        ],
        out_specs=[
            pl.BlockSpec((tq, LANES), lambda p, i: (i, p)),
            pl.BlockSpec((None, None, 2, tq), lambda p, i: (p, i, 0, 0)),
        ],
        out_shape=[SDS((s, ATTN_W), BF16), SDS((N_PAIRS, s // tq, 2, tq), F32)],
        scratch_shapes=[pltpu.VMEM((2 * nb, tk, LANES), BF16), pltpu.VMEM((nb, LANES, tk), BF16),
                        pltpu.VMEM((4, tk, tq), F32)],
        compiler_params=_cparams("arbitrary", "arbitrary"),
        name="attn_fwd",
    )(qkv, qkv, qkv, c_col, c_col)


def _pool_counts(row0, tm, w):
    t = row0 + _iota2((tm, 1), 0)
    return jnp.minimum(t + 1, w).astype(F32)


def _out_gate_up(attn_o, pool_o, w_out, x, g2, wg_t, wu_t, *, tm):
    s = x.shape[0]

    def body(a_ref, p_ref, wo_ref, x_ref, g_ref, wg_ref, wu_ref, x1_ref, h2_ref, r_ref, gate_ref, up_ref, act_ref):
        x1 = (x_ref[...] + jnp.dot(a_ref[...], wo_ref[0:ATTN_W, :], preferred_element_type=F32)
              + jnp.dot(p_ref[...], wo_ref[ATTN_W:, :], preferred_element_type=F32))
        r = lax.rsqrt(jnp.mean(x1 * x1, axis=-1, keepdims=True) + EPS)
        x1_ref[...] = x1
        r_ref[...] = r
        h2 = (x1 * r * g_ref[...]).astype(BF16)
        h2_ref[...] = h2
        gate = lax.dot_general(h2, wg_ref[...], NT, preferred_element_type=F32)
        up = lax.dot_general(h2, wu_ref[...], NT, preferred_element_type=F32)
        gate_ref[...] = gate.astype(BF16)
        up_ref[...] = up.astype(BF16)
        act_ref[...] = (gate * jax.nn.sigmoid(gate) * up).astype(BF16)

    row = lambda w: pl.BlockSpec((tm, w), lambda i: (i, 0))
    full = lambda a, b: pl.BlockSpec((a, b), lambda i: (0, 0))
    return pl.pallas_call(
        body,
        grid=(s // tm,),
        in_specs=[row(ATTN_W), row(POOL_W), full(D_MODEL, D_MODEL), row(D_MODEL), full(1, D_MODEL),
                  full(D_FF, D_MODEL), full(D_FF, D_MODEL)],
        out_specs=[row(D_MODEL), row(D_MODEL), row(1), row(D_FF), row(D_FF), row(D_FF)],
        out_shape=[SDS((s, D_MODEL), F32), SDS((s, D_MODEL), BF16), SDS((s, 1), F32), SDS((s, D_FF), BF16),
                   SDS((s, D_FF), BF16), SDS((s, D_FF), BF16)],
        compiler_params=_cparams("arbitrary"),
        name="out_gate_up",
    )(attn_o, pool_o, w_out, x, g2, wg_t, wu_t)


def _staggered(n, start, finish):
    pending = start(0)
    for k in range(n):
        following = start(k + 1) if k + 1 < n else None
        finish(k, pending)
        pending = following


def _down_final(act, wd, x1, gf, tgt, *, tm, sub):
    s = x1.shape[0]

    def body(a_ref, w_ref, x1_ref, g_ref, t_ref, dx2_ref, loss_ref, dgf_ref):
        @pl.when(pl.program_id(0) == 0)
        def _():
            loss_ref[...] = jnp.zeros_like(loss_ref)
            dgf_ref[...] = jnp.zeros_like(dgf_ref)

        g = g_ref[...]

        def matmul(k):
            return jnp.dot(a_ref[k * sub:(k + 1) * sub, :], w_ref[...], preferred_element_type=F32)

        def rest(k, mm):
            rows = slice(k * sub, (k + 1) * sub)
            x2 = x1_ref[rows, :] + mm
            r = lax.rsqrt(jnp.mean(x2 * x2, axis=-1, keepdims=True) + EPS)
            xn = x2 * r
            diff = xn * g - t_ref[rows, :]
            loss_ref[...] += jnp.sum(diff * diff, axis=0, keepdims=True)
            dy = diff * (1.0 / D_MODEL)
            dgf_ref[...] += jnp.sum(dy * xn, axis=0, keepdims=True)
            dxn = dy * g
            dx2_ref[rows, :] = r * (dxn - xn * jnp.mean(dxn * xn, axis=-1, keepdims=True))

        _staggered(tm // sub, matmul, rest)

    row = lambda w: pl.BlockSpec((tm, w), lambda i: (i, 0))
    full = lambda a, b: pl.BlockSpec((a, b), lambda i: (0, 0))
    return pl.pallas_call(
        body,
        grid=(s // tm,),
        in_specs=[row(D_FF), full(D_FF, D_MODEL), row(D_MODEL), full(1, D_MODEL), row(D_MODEL)],
        out_specs=[row(D_MODEL), full(1, D_MODEL), full(1, D_MODEL)],
        out_shape=[SDS((s, D_MODEL), F32), SDS((1, D_MODEL), F32), SDS((1, D_MODEL), F32)],
        compiler_params=_cparams("arbitrary"),
        name="down_final",
    )(act, wd, x1, gf, tgt)


def _swiglu_bwd(dx2, wd, gate, up, *, tm, tn):
    s = dx2.shape[0]

    def body(d_ref, w_ref, gate_ref, up_ref, dgate_ref, dup_ref):
        dact = lax.dot_general(d_ref[...].astype(BF16), w_ref[...], NT, preferred_element_type=F32)
        gate = gate_ref[...].astype(F32)
        sg = jax.nn.sigmoid(gate)
        dup_ref[...] = (dact * (gate * sg)).astype(BF16)
        dgate_ref[...] = (dact * up_ref[...].astype(F32) * (sg * (1.0 + gate * (1.0 - sg)))).astype(BF16)

    ospec = pl.BlockSpec((tm, tn), lambda c, r: (r, c))
    return pl.pallas_call(
        body,
        grid=(D_FF // tn, s // tm),
        in_specs=[pl.BlockSpec((tm, D_MODEL), lambda c, r: (r, 0)), pl.BlockSpec((tn, D_MODEL), lambda c, r: (c, 0)),
                  ospec, ospec],
        out_specs=[ospec, ospec],
        out_shape=[SDS((s, D_FF), BF16), SDS((s, D_FF), BF16)],
        compiler_params=_cparams("arbitrary", "arbitrary"),
        name="swiglu_bwd",
    )(dx2, wd, gate, up)


def _mm_tn_stacked(as_, rows, b, *, ts, name):
    s, nb_ = b.shape
    n = len(as_)
    offsets = [sum(rows[:i]) for i in range(n)]

    def body(*refs):
        a_refs, b_ref, o_ref, acc_ref = refs[:n], refs[n], refs[n + 1], refs[n + 2]
        k = pl.program_id(0)

        @pl.when(k == 0)
        def _():
            acc_ref[...] = jnp.zeros_like(acc_ref)

        bv = b_ref[...].astype(BF16)
        for a_ref, off, cnt in zip(a_refs, offsets, rows):
            part = lax.dot_general(a_ref[...].astype(BF16), bv, TN, preferred_element_type=F32)
            acc_ref[off:off + cnt, :] += part[0:cnt, :]

        @pl.when(k == s // ts - 1)
        def _():
            o_ref[...] = acc_ref[...].astype(BF16)

    return pl.pallas_call(
        body,
        grid=(s // ts,),
        in_specs=[pl.BlockSpec((ts, a.shape[1]), lambda k: (k, 0)) for a in as_] + [pl.BlockSpec((ts, nb_), lambda k: (k, 0))],
        out_specs=pl.BlockSpec((sum(rows), nb_), lambda k: (0, 0)),
        out_shape=SDS((sum(rows), nb_), BF16),
        scratch_shapes=[pltpu.VMEM((sum(rows), nb_), F32)],
        compiler_params=_cparams("arbitrary"),
        name=name,
    )(*as_, b)


def _norm_bwd(dh, x, r, g, dres):
    xn = x * r
    dxn = dh * g
    dx = dres + r * (dxn - xn * jnp.mean(dxn * xn, axis=-1, keepdims=True))
    return dx, jnp.sum(dh * xn, axis=0, keepdims=True)


def _mlp_in_pool_bwd(dgate, dup, wg_t, wu_t, w_out, x1, r2, g2, dx2, pooled, w_pool, pool_scale, *, tm):
    s = x1.shape[0]
    nt = s // tm
    ng = len(POOL_WINDOWS)

    def body(dg_ref, dup_ref, wg_ref, wu_ref, wo_ref, x_ref, r_ref, g_ref, d_ref, p_ref, w_ref, sc_ref,
             dx1_ref, dattn_ref, du_ref, dg2_ref, dw_ref, dsc_ref, head_ref):
        i = pl.program_id(0)

        @pl.when(i == 0)
        def _():
            dg2_ref[...] = jnp.zeros_like(dg2_ref)
            head_ref[...] = jnp.zeros_like(head_ref)
            dw_ref[...] = jnp.zeros_like(dw_ref)
            dsc_ref[...] = jnp.zeros_like(dsc_ref)

        dh2 = (jnp.dot(dg_ref[...], wg_ref[...], preferred_element_type=F32)
               + jnp.dot(dup_ref[...], wu_ref[...], preferred_element_type=F32))
        dx1, dg2 = _norm_bwd(dh2, x_ref[...], r_ref[...], g_ref[...], d_ref[...])
        dg2_ref[...] += dg2
        dx1_ref[...] = dx1
        dmix = lax.dot_general(dx1.astype(BF16), wo_ref[...], NT, preferred_element_type=F32)
        dattn_ref[...] = dmix[:, 0:ATTN_W]
        row0 = (nt - 1 - i) * tm
        for g, w in enumerate(POOL_WINDOWS):
            cols = slice(g * POOL_G, (g + 1) * POOL_G)
            wb = w_ref[g].astype(BF16)
            pooled_g = p_ref[:, cols]
            dpo = dmix[:, ATTN_W + g * POOL_G:ATTN_W + (g + 1) * POOL_G]
            mixed = jnp.dot(pooled_g, wb, preferred_element_type=F32)
            dsc_ref[:, cols] += jnp.sum(dpo * mixed, axis=0, keepdims=True)
            dmp = (dpo * sc_ref[:, cols]).astype(BF16)
            dw_ref[g] += lax.dot_general(pooled_g, dmp, TN, preferred_element_type=F32)
            dpooled = lax.dot_general(dmp, wb, NT, preferred_element_type=F32)
            a = dpooled / _pool_counts(row0, tm, w)
            acc = jnp.concatenate([a, head_ref[:, cols]], axis=0)
            head_ref[:, cols] = a[0:HALO, :]
            k = 1
            while k < w:
                acc = acc + pltpu.roll(acc, tm + HALO - k, axis=0)
                k *= 2
            du_ref[:, cols] = (acc[0:tm, :] - dpooled).astype(BF16)

    row = lambda w: pl.BlockSpec((tm, w), lambda i: (nt - 1 - i, 0))
    full = lambda a, b: pl.BlockSpec((a, b), lambda i: (0, 0))
    pool_w = pl.BlockSpec((ng, POOL_G, POOL_G), lambda i: (0, 0, 0))
    return pl.pallas_call(
        body,
        grid=(nt,),
        in_specs=[row(D_FF), row(D_FF), full(D_FF, D_MODEL), full(D_FF, D_MODEL), full(D_MODEL, D_MODEL),
                  row(D_MODEL), row(1), full(1, D_MODEL), row(D_MODEL), row(POOL_W), pool_w, full(1, POOL_W)],
        out_specs=[row(D_MODEL), row(ATTN_W), row(POOL_W), full(1, D_MODEL), pool_w, full(1, POOL_W)],
        out_shape=[SDS((s, D_MODEL), F32), SDS((s, ATTN_W), F32), SDS((s, POOL_W), BF16), SDS((1, D_MODEL), F32),
                   SDS((ng, POOL_G, POOL_G), F32), SDS((1, POOL_W), F32)],
        scratch_shapes=[pltpu.VMEM((HALO, POOL_W), F32)],
        compiler_params=_cparams("arbitrary"),
        name="mlp_in_pool_bwd",
    )(dgate, dup, wg_t, wu_t, w_out, x1, r2, g2, dx2, pooled, w_pool, pool_scale)


SUM_ROWS = 16


def _heads_t(t):
    n = t.shape[0]
    lane = _iota2((n, LANES), 1)
    tf = t.astype(F32)
    halves = jnp.concatenate([jnp.where(lane < HEAD_DIM, tf, 0.0).T, jnp.where(lane < HEAD_DIM, 0.0, tf).T], axis=1)
    r, c = _iota2((SUM_ROWS, 2 * n), 0), _iota2((SUM_ROWS, 2 * n), 1)
    ones = jnp.where(((r == 0) & (c < n)) | ((r == 4) & (c >= n)), 1.0, 0.0)
    return jnp.concatenate([halves, ones], axis=0).astype(BF16)


def _attn_bwd(qkv, attn_o, d_attn, rowb, ck_col, *, tq):
    s = qkv.shape[0]
    tk = tq
    nb = s // tq
    rows_t = LANES + SUM_ROWS

    def body(q_ref, k_ref, v_ref, o_ref, do_ref, rowb_ref, ck_ref, dq_ref, dk_ref, dv_ref, dck_ref, dcq_ref,
             dqt_ref, delta_ref, kp_ref, qp_ref, dob_ref, qt_ref, kt_ref, dot_ref, front_ref):
        lane = _iota2((tq, LANES), 1)
        lo = lane < HEAD_DIM
        first = _iota2((8, LANES), 1) < HEAD_DIM
        sel = jnp.where(_iota2((8, LANES), 0) < 4, jnp.where(first, 1.0, 0.0), jnp.where(first, 0.0, 1.0))

        def prep(b, _):
            st = pl.multiple_of(b * tq, tq)
            do2 = do_ref[pl.ds(st, tq), :]
            delta_ref[b] = _sel_dot(sel, do2 * o_ref[pl.ds(st, tq), :].astype(F32), NT)
            dob_ref[pl.ds(st, tq), :] = do2.astype(BF16)
            dqt_ref[b] = jnp.zeros((rows_t, tq), F32)
            k2 = k_ref[pl.ds(st, tq), :].astype(F32)
            q2 = q_ref[pl.ds(st, tq), :].astype(F32)
            ck = ck_ref[pl.ds(st, tq), :]
            for h in range(2):
                kp_ref[h * nb + b] = _augment(k2, h, -ck[:, h:h + 1], True)
                qp_ref[h * nb + b] = _augment(q2 * Q_SCALE, h, jnp.zeros((tq, 1), F32), False)
            qt_ref[b] = _heads_t(q2)
            kt_ref[b] = _heads_t(k2)
            dot_ref[b] = _heads_t(do2)[0:LANES, :]
            return 0

        lax.fori_loop(0, nb, prep, 0)

        def split(t):
            z = jnp.zeros_like(t)
            return jnp.where(lo, t, z), jnp.where(lo, z, t)

        def kv_block(j, _):
            st_j = pl.multiple_of(j * tk, tk)
            vs = split(v_ref[pl.ds(st_j, tk), :])
            kt = kt_ref[j]

            def stage(i, slot):
                ic = jnp.minimum(i, nb - 1)
                do2 = dob_ref[pl.ds(pl.multiple_of(ic * tq, tq), tq), :]
                for h in range(2):
                    front_ref[4 * slot + h] = lax.dot_general(kp_ref[h * nb + j], qp_ref[h * nb + ic], NT,
                                                              preferred_element_type=F32)
                    front_ref[4 * slot + 2 + h] = lax.dot_general(vs[h], do2, NT, preferred_element_type=F32)

            def q_block(i, slot, carry, diagonal):
                dkt, dvt = carry
                ic = jnp.minimum(i, nb - 1)
                rb = rowb_ref[ic] + jnp.where(i < nb, 0.0, NEG)
                dl = delta_ref[ic]
                pts, dsts = [], []
                for h in range(2):
                    st = front_ref[4 * slot + h] + rb[h:h + 1, :]
                    if diagonal:
                        st = jnp.where(_iota2((tk, tq), 0) <= _iota2((tk, tq), 1), st, NEG)
                    pt = jnp.exp(st)
                    pts.append(pt.astype(BF16))
                    dsts.append((pt * (front_ref[4 * slot + 2 + h] - dl[4 * h:4 * h + 1, :])).astype(BF16))
                dvt = dvt + lax.dot_general(dot_ref[ic], jnp.concatenate(pts, axis=1), NT, preferred_element_type=F32)
                dkt = dkt + lax.dot_general(qt_ref[ic], jnp.concatenate(dsts, axis=1), NT, preferred_element_type=F32)
                dqt_ref[ic] += jnp.dot(kt, jnp.concatenate(dsts, axis=0), preferred_element_type=F32)
                return dkt, dvt

            def run(i0, steps, carry):
                for d in range(steps):
                    stage(i0 + d + 1, d % 2)
                    carry = q_block(i0 + d, 1 - d % 2, carry, False)
                return carry

            stage(j, 0)
            stage(j + 1, 1)
            carry = q_block(j, 0, (jnp.zeros((rows_t, tk), F32), jnp.zeros((LANES, tk), F32)), True)
            first, left = j + 1, nb - 1 - j
            for size in UNROLLS:
                trips = _shift_div(left + 1 if size == UNROLLS[-1] else left, size)
                carry = lax.fori_loop(0, trips, lambda t, c, i0=first, n=size: run(i0 + n * t, n, c), carry)
                first, left = first + size * trips, left - size * trips
            dkt, dvt = carry
            dk_ref[pl.ds(st_j, tk), :] = (dkt[0:LANES, :].T * Q_SCALE).astype(BF16)
            dv_ref[pl.ds(st_j, tk), :] = dvt.T.astype(BF16)
            dck_ref[j] = dkt[LANES:LANES + 8, :]
            return 0

        lax.fori_loop(0, nb, kv_block, 0)

        def finish(b, _):
            acc = dqt_ref[b]
            dq_ref[pl.ds(pl.multiple_of(b * tq, tq), tq), :] = (acc[0:LANES, :].T * Q_SCALE).astype(BF16)
            dcq_ref[b] = acc[LANES:LANES + 8, :]
            return 0

        lax.fori_loop(0, nb, finish, 0)

    col = lambda off: pl.BlockSpec((s, LANES), lambda p: (0, off + p))
    sums = pl.BlockSpec((None, nb, 8, tq), lambda p: (p, 0, 0, 0))
    return pl.pallas_call(
        body,
        grid=(N_PAIRS,),
        in_specs=[col(0), col(N_PAIRS), col(2 * N_PAIRS), col(0), col(0),
                  pl.BlockSpec((None, nb, 2, tq), lambda p: (p, 0, 0, 0)),
                  pl.BlockSpec((None, s, 2), lambda p: (p, 0, 0))],
        out_specs=[col(0), col(0), col(0), sums, sums],
        out_shape=[SDS((s, ATTN_W), BF16), SDS((s, ATTN_W), BF16), SDS((s, ATTN_W), BF16),
                   SDS((N_PAIRS, nb, 8, tq), F32), SDS((N_PAIRS, nb, 8, tq), F32)],
        scratch_shapes=[pltpu.VMEM((nb, rows_t, tq), F32), pltpu.VMEM((nb, 8, tq), F32),
                        pltpu.VMEM((2 * nb, tk, LANES), BF16), pltpu.VMEM((2 * nb, tq, LANES), BF16),
                        pltpu.VMEM((s, LANES), BF16), pltpu.VMEM((nb, rows_t, 2 * tq), BF16),
                        pltpu.VMEM((nb, rows_t, 2 * tk), BF16), pltpu.VMEM((nb, LANES, 2 * tq), BF16),
                        pltpu.VMEM((8, tk, tq), F32)],
        compiler_params=_cparams("arbitrary"),
        name="attn_bwd",
    )(qkv, qkv, qkv, attn_o, d_attn, rowb, ck_col)


def _forget_bwd(dc_t, fl_t, b_rows):
    rows = fl_t.shape[0]
    nb = rows // N_HEADS

    def body(dc_ref, fl_ref, b_ref, dfl_ref, db_ref):
        dc = dc_ref[...]
        lower = _iota2((LANES, LANES), 0) >= _iota2((LANES, LANES), 1)
        ones = jnp.ones((LANES, LANES), F32)
        rr, cc, same = _head_block_masks(rows, nb)
        dlf = _dot_sel(dc, lower) + _sel_dot(same & (cc > rr), _dot_sel(dc, ones))
        dfl = dlf / (1.0 + jnp.exp(fl_ref[...] + b_ref[...]))
        dfl_ref[...] = dfl
        shift = nb.bit_length() - 1
        hsel = lax.shift_right_logical(_iota2((N_HEADS, rows), 1), shift) == _iota2((N_HEADS, rows), 0)
        db_ref[...] = _sel_dot(hsel, _dot_sel(dfl, ones))

    return pl.pallas_call(body, out_shape=[SDS(fl_t.shape, F32), SDS((N_HEADS, LANES), F32)],
                          compiler_params=_cparams(), name="forget_bwd")(dc_t, fl_t, b_rows)


def _in_bwd(dq, dk, dv, du, dfl, w_in_t, x, r1, g1, dx1, *, tm):
    s = x.shape[0]
    pieces = ((0, ATTN_W), (ATTN_W, 2 * ATTN_W), (2 * ATTN_W, QKV_W), (U_OFF, F_OFF), (F_OFF, IN_PAD))

    def body(dq_ref, dk_ref, dv_ref, du_ref, df_ref, w_ref, x_ref, r_ref, g_ref, d_ref, dx_ref, dg1_ref):
        @pl.when(pl.program_id(0) == 0)
        def _():
            dg1_ref[...] = jnp.zeros_like(dg1_ref)

        dh = None
        for ref, (c0, c1) in zip((dq_ref, dk_ref, dv_ref, du_ref, df_ref), pieces):
            t = jnp.dot(ref[...], w_ref[c0:c1, :], preferred_element_type=F32)
            dh = t if dh is None else dh + t
        dx, dg1 = _norm_bwd(dh, x_ref[...], r_ref[...], g_ref[...], d_ref[...])
        dx_ref[...] = dx
        dg1_ref[...] += dg1

    row = lambda w: pl.BlockSpec((tm, w), lambda i: (i, 0))
    full = lambda a, b: pl.BlockSpec((a, b), lambda i: (0, 0))
    return pl.pallas_call(
        body,
        grid=(s // tm,),
        in_specs=[row(ATTN_W), row(ATTN_W), row(ATTN_W), row(POOL_W), row(LANES), full(IN_PAD, D_MODEL),
                  row(D_MODEL), row(1), full(1, D_MODEL), row(D_MODEL)],
        out_specs=[row(D_MODEL), full(1, D_MODEL)],
        out_shape=[SDS((s, D_MODEL), F32), SDS((1, D_MODEL), F32)],
        compiler_params=_cparams("arbitrary"),
        name="in_bwd",
    )(dq, dk, dv, du, dfl, w_in_t, x, r1, g1, dx1)


def _tiles(s):
    big = min(512, s)
    return dict(row=big, attn=min(256, s // 2), ff_rows=min(256, s), tall=min(1024, s))


def _tie(a, token):
    return a + token[0:1, 0:1].astype(a.dtype)


def _local_step(x, tgt, p, weight, emit, started):
    s = x.shape[0]
    t = _tiles(s)
    tm, tq = t["row"], t["attn"]
    nb = s // LANES
    nqb = s // tq
    g1, g2, gf = p["norm1_g"], p["norm2_g"], p["final_g"].reshape(1, D_MODEL)
    w_pool, pool_scale = p["w_pool"][0], p["pool_scale"]

    h, r1 = _norm1(x, _tie(g1, started), tm=tm)
    w_in_t = weight("w_in", h)
    qkv, fl, pooled, pool_o = _in_proj_pool(h, w_in_t, w_pool, pool_scale, tm=t["tall"])
    fl_t = fl[:, :N_HEADS].T.reshape(N_HEADS * nb, LANES)
    b_rows = jnp.repeat(p["b_forget"].reshape(N_HEADS), nb).reshape(N_HEADS * nb, 1)
    c = _forget_cumsum(fl_t, b_rows).reshape(N_PAIRS, 2, s)
    c_col = c.transpose(0, 2, 1)
    c_rowblk = c.reshape(N_PAIRS, 2, nqb, tq).transpose(0, 2, 1, 3)
    attn_o, lse = _attn_fwd(qkv, c_col, tk=tq)
    lse = lse.reshape(N_PAIRS, nqb // 2, 2, 2, tq).transpose(0, 1, 3, 2, 4).reshape(N_PAIRS, nqb, 2, tq)
    w_out = weight("w_out", attn_o)
    wg_t, wu_t = weight("w_gate_up", attn_o)
    x1, h2, r2, gate, up, act = _out_gate_up(attn_o, pool_o, w_out, x, g2, wg_t, wu_t, tm=t["ff_rows"])
    wd = weight("w_down", act)
    dx2, loss_row, d_gf = _down_final(act, wd, x1, gf, tgt, tm=tm, sub=min(128, tm))

    dgate, dup = _swiglu_bwd(dx2, wd, gate, up, tm=t["ff_rows"], tn=D_FF)
    d_wd = _mm_tn_stacked([act], [D_FF], dx2, ts=t["tall"], name="grad_w_down")
    d_wg_t = _mm_tn_stacked([dgate], [D_FF], h2, ts=t["tall"], name="grad_w_gate")
    d_wu_t = _mm_tn_stacked([dup], [D_FF], h2, ts=t["tall"], name="grad_w_up")
    dx1, d_attn, du, d_g2, d_wpool, d_pscale = _mlp_in_pool_bwd(dgate, dup, wg_t, wu_t, w_out, x1, r2, g2, dx2, pooled,
                                                               w_pool, pool_scale, tm=t["ff_rows"])
    d_wo = _mm_tn_stacked([attn_o, pool_o], [ATTN_W, POOL_W], dx1, ts=t["tall"], name="grad_w_out")
    token = emit(("w_down", "w_gate", "w_up", "w_out"), (d_wd, d_wg_t, d_wu_t, d_wo))
    rowb = _tie(c_rowblk - lse, token)
    dq, dk, dv, dck, dcq = _attn_bwd(qkv, attn_o, d_attn, rowb, c_col, tq=tq)
    dc_t = (dcq - dck)[:, :, 0::4, :].transpose(0, 2, 1, 3).reshape(N_HEADS * nb, LANES)
    dfl_t, db = _forget_bwd(dc_t, fl_t, b_rows)
    dfl = jnp.pad(dfl_t.reshape(N_HEADS, s).T, ((0, 0), (0, LANES - N_HEADS))).astype(BF16)
    d_w_in_t = _mm_tn_stacked([dq, dk, dv, dfl, du], [ATTN_W, ATTN_W, ATTN_W, N_HEADS, POOL_W], h, ts=t["tall"],
                              name="grad_w_in")
    token = emit(("w_in",), (d_w_in_t,))
    dx, d_g1 = _in_bwd(dq, dk, dv, du, dfl, w_in_t, x, r1, _tie(g1, token), dx1, tm=tm)

    small = dict(norm1_g=d_g1, b_forget=db[:, 0].reshape(1, N_HEADS), w_pool=d_wpool, pool_scale=d_pscale,
                 norm2_g=d_g2, final_g=d_gf)
    return loss_row, dx, small


def _my_index():
    return 4 * lax.axis_index("x") + 2 * lax.axis_index("y") + lax.axis_index("c")


def _peer(k):
    pos = [lax.axis_index(a) for a in ("x", "y", "c")]
    flipped = tuple(1 - p if (k >> b) & 1 else p for p, b in zip(pos, (2, 1, 0)))
    return flipped, 4 * flipped[0] + 2 * flipped[1] + flipped[2]


_HBM = pl.BlockSpec(memory_space=pltpu.HBM)
_SEM = pl.BlockSpec(memory_space=pltpu.SEMAPHORE)
_DATAFLOW = pltpu.SideEffectType.DATAFLOW_SIDE_EFFECTING


ALL_PEERS = tuple(range(1, N_DEV))
SAME_CORE = (1, 2, 4, 6)


def _peer_copies(ins, lands, send_sems, recv_sems, scatter, peers, arrivals):
    me = _my_index()
    copies = []
    for w in range(len(ins)):
        for k in peers[w]:
            dev, idx = _peer(k)
            copies.append(pltpu.make_async_remote_copy(
                src_ref=ins[w].at[idx] if scatter[w] else ins[w], dst_ref=lands[w].at[idx if arrivals else me],
                send_sem=send_sems[w].at[k - 1], recv_sem=recv_sems[w].at[k - 1], device_id=dev, device_id_type=MESH))
    return copies


def _own_copies(ins, lands, send_sems, scatter):
    me = _my_index()
    return [pltpu.make_async_copy(ins[w].at[me] if scatter[w] else ins[w], lands[w].at[me], send_sems[w].at[N_DEV - 1])
            for w in range(len(ins))]


def _forward_copies(land, send_sems, recv_sems, arrivals):
    sibling, _ = _peer(1)
    copies = []
    for j, k in enumerate(SAME_CORE[1:]):
        src, dst = _peer(k)[1], _peer(k ^ 1 if arrivals else k)[1]
        copies.append(pltpu.make_async_remote_copy(
            src_ref=land.at[src], dst_ref=land.at[dst], send_sem=send_sems.at[j], recv_sem=recv_sems.at[j],
            device_id=sibling, device_id_type=MESH))
    return copies


def _forward_start(land, name):
    def body(land_ref, send_sems, recv_sems, land_thru, token):
        for cp in _forward_copies(land_ref, send_sems, recv_sems, False):
            cp.start()
        token[...] = jnp.zeros_like(token)

    sem = pltpu.SemaphoreType.DMA((len(SAME_CORE) - 1,))
    send, recv, thru, _ = pl.pallas_call(
        body,
        in_specs=[_HBM],
        out_specs=[_SEM, _SEM, _HBM, pl.BlockSpec(memory_space=pltpu.VMEM)],
        out_shape=[sem, sem, pltpu.HBM(land.shape, land.dtype), SDS((8, LANES), F32)],
        input_output_aliases={0: 2},
        compiler_params=pltpu.CompilerParams(has_side_effects=_DATAFLOW),
        name=name,
    )(land)
    return send, recv, thru


def _forward_wait(handle, after, name):
    def body(land_ref, send_sems, recv_sems, after_ref, land_out):
        for cp in _forward_copies(land_ref, send_sems, recv_sems, False):
            cp.wait_send()
        for cp in _forward_copies(land_ref, send_sems, recv_sems, True):
            cp.wait_recv()

    send, recv, land = handle
    return pl.pallas_call(
        body,
        in_specs=[_HBM, _SEM, _SEM, pl.BlockSpec(memory_space=pl.ANY)],
        out_specs=_HBM,
        out_shape=pltpu.HBM(land.shape, land.dtype),
        input_output_aliases={0: 0},
        compiler_params=pltpu.CompilerParams(has_side_effects=_DATAFLOW),
        name=name,
    )(land, send, recv, after)


def _exchange_start(arrays, scatter, name, peers=None):
    n = len(arrays)
    peers = peers or [ALL_PEERS] * n
    land_shapes = [(N_DEV,) + tuple(a.shape[1:] if sc else a.shape) for a, sc in zip(arrays, scatter)]

    def body(*refs):
        ins, lands = refs[:n], refs[n:2 * n]
        send_sems, recv_sems = refs[2 * n:3 * n], refs[3 * n:4 * n]
        token = refs[6 * n]
        for cp in _peer_copies(ins, lands, send_sems, recv_sems, scatter, peers, False):
            cp.start()
        for cp in _own_copies(ins, lands, send_sems, scatter):
            cp.start()
        token[...] = jnp.zeros_like(token)

    sends, recvs = pltpu.SemaphoreType.DMA((N_DEV,)), pltpu.SemaphoreType.DMA((N_DEV - 1,))
    outs = pl.pallas_call(
        body,
        in_specs=[_HBM] * (2 * n),
        out_specs=[_SEM] * (2 * n) + [_HBM] * (2 * n) + [pl.BlockSpec(memory_space=pltpu.VMEM)],
        out_shape=[sends] * n + [recvs] * n + [pltpu.HBM(a.shape, a.dtype) for a in arrays]
        + [pltpu.HBM(sh, a.dtype) for sh, a in zip(land_shapes, arrays)] + [SDS((8, LANES), F32)],
        input_output_aliases={i: 2 * n + i for i in range(2 * n)},
        compiler_params=pltpu.CompilerParams(has_side_effects=_DATAFLOW),
        name=name,
    )(*[pltpu.with_memory_space_constraint(a, pltpu.HBM) for a in arrays],
      *[pltpu.with_memory_space_constraint(lax.empty(sh, a.dtype), pltpu.HBM) for sh, a in zip(land_shapes, arrays)])
    handles = [dict(send=outs[w], recv=outs[n + w], src=outs[2 * n + w], land=outs[3 * n + w], scatter=scatter[w],
                    peers=peers[w]) for w in range(n)]
    return handles, outs[4 * n]


def _exchange_wait(handles, after, name):
    n = len(handles)
    scatter, peers = [h["scatter"] for h in handles], [h["peers"] for h in handles]

    def body(*refs):
        ins, lands = refs[:n], refs[n:2 * n]
        send_sems, recv_sems = refs[2 * n:3 * n], refs[3 * n:4 * n]
        for cp in _peer_copies(ins, lands, send_sems, recv_sems, scatter, peers, False):
            cp.wait_send()
        for cp in _peer_copies(ins, lands, send_sems, recv_sems, scatter, peers, True):
            cp.wait_recv()
        for cp in _own_copies(ins, lands, send_sems, scatter):
            cp.wait()

    srcs, lands = [h["src"] for h in handles], [h["land"] for h in handles]
    outs = pl.pallas_call(
        body,
        in_specs=[_HBM] * (2 * n) + [_SEM] * (2 * n) + [pl.BlockSpec(memory_space=pl.ANY)],
        out_specs=[_HBM] * (2 * n),
        out_shape=[pltpu.HBM(a.shape, a.dtype) for a in srcs + lands],
        input_output_aliases={i: i for i in range(2 * n)},
        compiler_params=pltpu.CompilerParams(has_side_effects=_DATAFLOW),
        name=name,
    )(*srcs, *lands, *[h["send"] for h in handles], *[h["recv"] for h in handles], after)
    return outs[n:]


def _adamw(parts, w, m, v, name):
    rows, cols = w.shape
    tr = rows // 4 if rows % 32 == 0 else rows

    def body(p_ref, w_ref, m_ref, v_ref, g_ref, d_ref, mo_ref, vo_ref):
        g = p_ref[0].astype(F32)
        for d in range(1, N_DEV):
            g = g + p_ref[d].astype(F32)
        g_ref[...] = g
        d_ref[...], mo_ref[...], vo_ref[...] = _adam_update(g, w_ref[...], m_ref[...], v_ref[...])

    blk = pl.BlockSpec((tr, cols), lambda i: (i, 0))
    return pl.pallas_call(
        body,
        grid=(rows // tr,),
        in_specs=[pl.BlockSpec((N_DEV, tr, cols), lambda i: (0, i, 0)), blk, blk, blk],
        out_specs=[blk] * 4,
        out_shape=[SDS((rows, cols), F32)] * 4,
        compiler_params=_cparams("arbitrary"),
        name=name,
    )(parts, w, m, v)


def _adamw_dense(parts, w, m, v, name):
    _, rows, cols = parts.shape
    per_row = cols // LANES

    def body(p_ref, w_ref, m_ref, v_ref, g_ref, d_ref, mo_ref, vo_ref):
        g = p_ref[0].astype(F32)
        for d in range(1, N_DEV):
            g = g + p_ref[d].astype(F32)
        for c in range(per_row):
            at = (pl.ds(c, rows, stride=per_row), slice(None))
            gc = g[:, c * LANES:(c + 1) * LANES]
            g_ref[at] = gc
            d_ref[at], mo_ref[at], vo_ref[at] = _adam_update(gc, w_ref[at], m_ref[at], v_ref[at])

    return pl.pallas_call(
        body,
        out_shape=[SDS(w.shape, F32)] * 4,
        compiler_params=_cparams(),
        name=name,
    )(parts, w, m, v)


_ROW_OF = dict(norm1_g=(0, D_MODEL), norm2_g=(1, D_MODEL), final_g=(2, D_MODEL), pool_scale=(3, POOL_W),
               b_forget=(4, N_HEADS), loss=(5, 1))


def _pack_rows(vals):
    rows = [jnp.pad(vals[n].reshape(1, width).astype(F32), ((0, 0), (0, D_MODEL - width)))
            for n, (_, width) in sorted(_ROW_OF.items(), key=lambda kv: kv[1][0])]
    return jnp.concatenate(rows + [jnp.zeros((8 - len(rows), D_MODEL), F32)], axis=0)


def _adam_update(g, w, m, v):
    m_new = ADAM_B1 * m + (1.0 - ADAM_B1) * g
    v_new = ADAM_B2 * v + (1.0 - ADAM_B2) * (g * g)
    m_hat = m_new / (1.0 - ADAM_B1 ** ADAM_STEP)
    v_hat = v_new / (1.0 - ADAM_B2 ** ADAM_STEP)
    return -ADAM_LR * (m_hat / (jnp.sqrt(v_hat) + ADAM_EPS) + ADAM_WD * w), m_new, v_new


def _adamw_replicated(parts_rows, parts_pool, w, m, v):
    names = ("norm1_g", "norm2_g", "final_g", "pool_scale", "b_forget", "w_pool")
    shapes = {n: ((len(POOL_WINDOWS), POOL_G, POOL_G) if n == "w_pool" else (1, _ROW_OF[n][1])) for n in names}

    def body(rows_ref, pool_ref, *refs):
        ins, outs = refs[:3 * len(names)], refs[3 * len(names):]

        def total(n):
            if n == "w_pool":
                pieces = [pool_ref[d] for d in range(N_DEV)]
            else:
                row, width = _ROW_OF[n]
                pieces = [rows_ref[d, row:row + 1, 0:width] for d in range(N_DEV)]
            g = pieces[0]
            for p in pieces[1:]:
                g = g + p
            return g

        outs[0][...] = total("loss")
        for k, n in enumerate(names):
            g = total(n)
            delta, m_new, v_new = _adam_update(g, ins[3 * k][...], ins[3 * k + 1][...], ins[3 * k + 2][...])
            for o_ref, val in zip(outs[1 + 4 * k:5 + 4 * k], (g, delta, m_new, v_new)):
                o_ref[...] = val

    args = [d[n].reshape(shapes[n]) for n in names for d in (w, m, v)]
    res = pl.pallas_call(
        body,
        out_shape=[SDS((1, 1), F32)] + [SDS(shapes[n], F32) for n in names for _ in range(4)],
        compiler_params=_cparams(),
        name="adamw_replicated",
    )(parts_rows, parts_pool, *args)
    return res[0], {n: [r.reshape(w[n].shape) for r in res[1 + 4 * k:5 + 4 * k]] for k, n in enumerate(names)}


def kernel(x, norm1_g, w_in, b_forget, w_pool, pool_scale, w_out, norm2_g, w_gate, w_up, w_down, final_g, loss_target, m_norm1_g, m_w_in, m_b_forget, m_w_pool, m_pool_scale, m_w_out, m_norm2_g, m_w_gate, m_w_up, m_w_down, m_final_g, v_norm1_g, v_w_in, v_b_forget, v_w_pool, v_pool_scale, v_w_out, v_norm2_g, v_w_gate, v_w_up, v_w_down, v_final_g):
    big = ("w_in", "w_out", "w_gate", "w_up", "w_down")
    order = ("norm1_g", "w_in", "b_forget", "w_pool", "pool_scale", "w_out", "norm2_g", "w_gate", "w_up", "w_down",
             "final_g")
    w = dict(norm1_g=norm1_g, w_in=w_in, b_forget=b_forget, w_pool=w_pool, pool_scale=pool_scale, w_out=w_out,
             norm2_g=norm2_g, w_gate=w_gate, w_up=w_up, w_down=w_down, final_g=final_g)
    m = dict(norm1_g=m_norm1_g, w_in=m_w_in, b_forget=m_b_forget, w_pool=m_w_pool, pool_scale=m_pool_scale,
             w_out=m_w_out, norm2_g=m_norm2_g, w_gate=m_w_gate, w_up=m_w_up, w_down=m_w_down, final_g=m_final_g)
    v = dict(norm1_g=v_norm1_g, w_in=v_w_in, b_forget=v_b_forget, w_pool=v_w_pool, pool_scale=v_pool_scale,
             w_out=v_w_out, norm2_g=v_norm2_g, w_gate=v_w_gate, w_up=v_w_up, w_down=v_w_down, final_g=v_final_g)

    flipped = ("w_in", "w_gate", "w_up")
    shard = lambda d, n: d[n][0].T if n in flipped else d[n][0]
    gather, started = _exchange_start([shard(w, n).astype(BF16) for n in big], [False] * len(big), "gather_start",
                                      peers=[SAME_CORE if n == "w_in" else ALL_PEERS for n in big])
    gather = dict(zip(big, gather))

    def gathered(names, after):
        return _exchange_wait([gather[n] for n in names], after, "gather_wait_" + names[0])

    def weight(name, after):
        if name == "w_in":
            forward = _forward_start(gathered(["w_in"], after)[0], "gather_forward_start")
            full = _forward_wait(forward, after, "gather_forward_wait").reshape(IN_W, D_MODEL)
            f0 = QKV_W + N_HEADS
            return jnp.concatenate([full[:QKV_W], full[f0:], full[QKV_W:f0],
                                    jnp.zeros((IN_PAD - IN_W, D_MODEL), BF16)], axis=0)
        if name == "w_out":
            return gathered(["w_out"], after)[0].reshape(D_MODEL, D_MODEL)
        if name == "w_gate_up":
            return [g.reshape(D_FF, D_MODEL) for g in gathered(["w_gate", "w_up"], after)]
        return gathered(["w_down"], after)[0].reshape(D_FF, D_MODEL)

    rows = lambda g: g.reshape(N_DEV, g.shape[0] // N_DEV, g.shape[1])
    sent = {}

    def emit(names, grads):
        handles, token = _exchange_start([rows(g) for g in grads], [True] * len(names), "grads_start_" + names[0])
        sent.update(zip(names, handles))
        return token

    loss_row, dx, small_grads = _local_step(x[0], loss_target[0], w, weight, emit, started)

    packed = _pack_rows(dict(small_grads, loss=0.5 / D_MODEL * jnp.sum(loss_row)))
    small_handles, after = _exchange_start([packed, small_grads["w_pool"]], [False, False], "grads_start_replicated")

    outs = {}
    for name in ("w_down", "w_gate", "w_up", "w_out", "w_in"):
        (parts,) = _exchange_wait([sent[name]], after, "grads_wait_" + name)
        if name == "w_in":
            dense = lambda d: d[name].transpose(2, 0, 1).reshape(-1, LANES)
            outs[name] = _adamw_dense(parts, dense(w), dense(m), dense(v), "adamw_" + name)
            after = outs[name][0]
            outs[name] = [a.reshape(-1, D_MODEL // LANES, LANES).transpose(1, 2, 0).reshape(1, D_MODEL, -1)
                          for a in outs[name]]
            continue
        outs[name] = _adamw(parts, shard(w, name), shard(m, name), shard(v, name), "adamw_" + name)
        after = outs[name][0]
        outs[name] = [(a.T if name in flipped else a)[None] for a in outs[name]]
    parts_rows, parts_pool = _exchange_wait(small_handles, after, "grads_wait_replicated")
    loss, small = _adamw_replicated(parts_rows, parts_pool, w, m, v)
    outs.update(small)

    return (loss.reshape(()), dx[None]) + tuple(outs[n][k] for k in range(4) for n in order)
```

```python
import jax
import jax.numpy as jnp
from jax import lax
from jax.experimental import pallas as pl
from jax.experimental.pallas import tpu as pltpu

F32 = jnp.float32
BF16 = jnp.bfloat16
SDS = jax.ShapeDtypeStruct

D_MODEL = 1024
ATTN_W = 512
N_HEADS = 8
HEAD_DIM = 64
Q_SCALE = HEAD_DIM ** -0.5
N_PAIRS = N_HEADS // 2
POOL_W = 512
POOL_WINDOWS = (2, 4, 8, 16)
POOL_G = 128
HALO = 16
IN_W = 3 * ATTN_W + N_HEADS + POOL_W
QKV_W = 3 * ATTN_W
U_OFF = QKV_W
F_OFF = QKV_W + POOL_W
IN_PAD = F_OFF + 128
D_FF = 2816
EPS = 1e-6
NEG = -1e30
N_DEV = 8
LANES = 128
BF16_ROWS = 16

IN_SHARD = IN_W // N_DEV
IN_STEP = IN_SHARD // BF16_ROWS * BF16_ROWS
IN_SHIFT = IN_SHARD - IN_STEP
IN_WINDOW = -(-((N_DEV - 1) * IN_SHIFT + IN_SHARD) // BF16_ROWS) * BF16_ROWS

ADAM_LR = 0.001
ADAM_B1 = 0.9
ADAM_B2 = 0.999
ADAM_EPS = 1e-08
ADAM_WD = 0.01
ADAM_STEP = 10

VMEM_LIMIT_BYTES = 56 * 1024 * 1024
MESH = pl.DeviceIdType.MESH
NT = (((1,), (1,)), ((), ()))
TN = (((0,), (0,)), ((), ()))


def _cparams(*sem):
    return pltpu.CompilerParams(dimension_semantics=sem or None, vmem_limit_bytes=VMEM_LIMIT_BYTES)


def _split3(a):
    hi = a.astype(BF16)
    r1 = a - hi.astype(F32)
    mid = r1.astype(BF16)
    lo = (r1 - mid.astype(F32)).astype(BF16)
    return hi, mid, lo


def _dot_sel(a, sel, dims=None):
    sb = sel.astype(BF16)
    if dims is None:
        return sum(jnp.dot(p, sb, preferred_element_type=F32) for p in _split3(a))
    return sum(lax.dot_general(p, sb, dims, preferred_element_type=F32) for p in _split3(a))


def _sel_dot(sel, a, dims=None):
    sb = sel.astype(BF16)
    if dims is None:
        return sum(jnp.dot(sb, p, preferred_element_type=F32) for p in _split3(a))
    return sum(lax.dot_general(sb, p, dims, preferred_element_type=F32) for p in _split3(a))


def _iota2(shape, dim):
    return lax.broadcasted_iota(jnp.int32, shape, dim)


UNROLLS = (8, 4, 2)


def _shift_div(x, n):
    return lax.shift_right_logical(x, n.bit_length() - 1)


def _norm1(x, g1, *, tm):
    s = x.shape[0]

    def body(x_ref, g_ref, h_ref, r_ref):
        xv = x_ref[...]
        r = lax.rsqrt(jnp.mean(xv * xv, axis=-1, keepdims=True) + EPS)
        h_ref[...] = (xv * r * g_ref[...]).astype(BF16)
        r_ref[...] = r

    row = lambda w: pl.BlockSpec((tm, w), lambda i: (i, 0))
    return pl.pallas_call(
        body,
        grid=(s // tm,),
        in_specs=[row(D_MODEL), pl.BlockSpec((1, D_MODEL), lambda i: (0, 0))],
        out_specs=[row(D_MODEL), row(1)],
        out_shape=[SDS((s, D_MODEL), BF16), SDS((s, 1), F32)],
        compiler_params=_cparams("arbitrary"),
        name="norm1",
    )(x, g1)


def _in_proj_pool(h, w_in_t, w_pool, pool_scale, *, tm):
    s = h.shape[0]

    def body(h_ref, w_ref, wp_ref, sc_ref, qkv_ref, fl_ref, pooled_ref, po_ref, tail_ref):
        i = pl.program_id(0)

        @pl.when(i == 0)
        def _():
            tail_ref[...] = jnp.zeros_like(tail_ref)

        hv = h_ref[...]
        uv = lax.dot_general(hv, w_ref[U_OFF:F_OFF, :], NT, preferred_element_type=F32)
        qkv_ref[...] = lax.dot_general(hv, w_ref[0:QKV_W, :], NT, preferred_element_type=F32).astype(BF16)
        fl_ref[...] = lax.dot_general(hv, w_ref[F_OFF:IN_PAD, :], NT, preferred_element_type=F32)
        ext = jnp.concatenate([tail_ref[...], uv], axis=0)
        tail_ref[...] = uv[tm - HALO:, :]
        for g, w in enumerate(POOL_WINDOWS):
            cols = slice(g * POOL_G, (g + 1) * POOL_G)
            acc = ext[:, cols]
            k = 1
            while k < w:
                acc = acc + pltpu.roll(acc, k, axis=0)
                k *= 2
            pooled = (acc[HALO:, :] / _pool_counts(i * tm, tm, w) - uv[:, cols]).astype(BF16)
            pooled_ref[:, cols] = pooled
            mixed = jnp.dot(pooled, wp_ref[g].astype(BF16), preferred_element_type=F32)
            po_ref[:, cols] = (mixed * sc_ref[:, cols]).astype(BF16)

    row = lambda w: pl.BlockSpec((tm, w), lambda i: (i, 0))
    return pl.pallas_call(
        body,
        grid=(s // tm,),
        in_specs=[row(D_MODEL), pl.BlockSpec((IN_PAD, D_MODEL), lambda i: (0, 0)),
                  pl.BlockSpec((len(POOL_WINDOWS), POOL_G, POOL_G), lambda i: (0, 0, 0)),
                  pl.BlockSpec((1, POOL_W), lambda i: (0, 0))],
        out_specs=[row(QKV_W), row(LANES), row(POOL_W), row(POOL_W)],
        out_shape=[SDS((s, QKV_W), BF16), SDS((s, LANES), F32), SDS((s, POOL_W), BF16), SDS((s, POOL_W), BF16)],
        scratch_shapes=[pltpu.VMEM((HALO, POOL_W), F32)],
        compiler_params=_cparams("arbitrary"),
        name="in_proj_pool",
    )(h, w_in_t, w_pool, pool_scale)


def _head_block_masks(rows, nb):
    shift = nb.bit_length() - 1
    rr, cc = _iota2((rows, rows), 0), _iota2((rows, rows), 1)
    same = lax.shift_right_logical(rr, shift) == lax.shift_right_logical(cc, shift)
    return rr, cc, same


def _forget_cumsum(fl_t, b_rows):
    rows = fl_t.shape[0]
    nb = rows // N_HEADS

    def body(fl_ref, b_ref, c_ref):
        z = fl_ref[...] + b_ref[...]
        lf = jnp.minimum(z, 0.0) - jnp.log1p(jnp.exp(-jnp.abs(z)))
        upper = _iota2((LANES, LANES), 0) <= _iota2((LANES, LANES), 1)
        within = _dot_sel(lf, upper)
        tot = _dot_sel(lf, jnp.ones((LANES, LANES), F32))
        rr, cc, same = _head_block_masks(rows, nb)
        c_ref[...] = within + _sel_dot(same & (cc < rr), tot)

    return pl.pallas_call(body, out_shape=SDS(fl_t.shape, F32), compiler_params=_cparams(), name="forget_cumsum")(
        fl_t, b_rows)


BIAS_LANES = 3


def _augment(t, h, col, col_first):
    n = t.shape[0]
    lane = _iota2((n, LANES), 1)
    own = (lane < HEAD_DIM) if h == 0 else (lane >= HEAD_DIM)
    b0 = HEAD_DIM if h == 0 else 0
    c0, o0 = (b0, b0 + BIAS_LANES) if col_first else (b0 + BIAS_LANES, b0)
    x = jnp.where(own, t, 0.0)
    for off, piece in enumerate(_split3(col)):
        x = jnp.where(lane == c0 + off, piece.astype(F32), x)
    x = jnp.where((lane >= o0) & (lane < o0 + BIAS_LANES), 1.0, x)
    return x.astype(BF16)


def _attn_fwd(qkv, c_col, *, tk):
    s = qkv.shape[0]
    tq = 2 * tk
    nb = s // tk

    def body(q_ref, k_ref, v_ref, cq_ref, ck_ref, o_ref, lse_ref, kp_ref, vt_ref, st_ref):
        i = pl.program_id(1)

        @pl.when(i == 0)
        def _():
            def prep(jb, _):
                st = pl.multiple_of(jb * tk, tk)
                k2 = k_ref[pl.ds(st, tk), :].astype(F32)
                ck = ck_ref[pl.ds(st, tk), :]
                for h in range(2):
                    kp_ref[h * nb + jb] = _augment(k2, h, -ck[:, h:h + 1], True)
                vt_ref[jb] = v_ref[pl.ds(st, tk), :].astype(F32).T.astype(BF16)
                return 0

            lax.fori_loop(0, nb, prep, 0)

        qs = q_ref[...].astype(F32) * Q_SCALE
        cq = cq_ref[...]
        qp = [_augment(qs, h, cq[:, h:h + 1], False) for h in range(2)]

        def logits(j):
            return tuple(lax.dot_general(kp_ref[h * nb + j], qp[h], NT, preferred_element_type=F32) for h in range(2))

        def softmax_pv(j, slot, stats, masked):
            out = []
            for h in range(2):
                m, l, acc = stats[h]
                st = st_ref[2 * slot + h]
                if masked:
                    st = jnp.where(j * tk + _iota2((tk, tq), 0) <= i * tq + _iota2((tk, tq), 1), st, NEG)
                m_new = jnp.maximum(m, jnp.max(st, axis=0, keepdims=True))
                alpha = jnp.exp(m - m_new)
                p = jnp.exp(st - m_new)
                l = alpha * l + jnp.sum(p, axis=0, keepdims=True)
                vt = vt_ref[j, h * HEAD_DIM:(h + 1) * HEAD_DIM, :]
                acc = alpha * acc + jnp.dot(vt, p.astype(BF16), preferred_element_type=F32)
                out.append((m_new, l, acc))
            return tuple(out)

        def put(slot, j):
            for h, st in enumerate(logits(j)):
                st_ref[2 * slot + h] = st

        def run(j0, steps, stats):
            for d in range(steps):
                put(1 - d % 2, j0 + d + 1)
                stats = softmax_pv(j0 + d, d % 2, stats, False)
            return stats

        init = tuple((jnp.full((1, tq), NEG, F32), jnp.zeros((1, tq), F32), jnp.zeros((HEAD_DIM, tq), F32))
                     for _ in range(2))
        put(0, 0)
        first, left, stats = 0, 2 * i, init
        for size in UNROLLS:
            trips = _shift_div(left, size)
            stats = lax.fori_loop(0, trips, lambda t, st, j0=first, n=size: run(j0 + n * t, n, st), stats)
            first, left = first + size * trips, left - size * trips
        put(1, 2 * i + 1)
        stats = softmax_pv(2 * i, 0, stats, True)
        (ma, la, acca), (mb, lb, accb) = softmax_pv(2 * i + 1, 1, stats, True)
        o_ref[...] = jnp.concatenate([acca / la, accb / lb], axis=0).T.astype(BF16)
        lse_ref[...] = jnp.where(_iota2((2, tq), 0) == 0, ma + jnp.log(la), mb + jnp.log(lb))

    return pl.pallas_call(
        body,
        grid=(N_PAIRS, s // tq),
        in_specs=[
            pl.BlockSpec((tq, LANES), lambda p, i: (i, p)),
            pl.BlockSpec((s, LANES), lambda p, i: (0, N_PAIRS + p)),
            pl.BlockSpec((s, LANES), lambda p, i: (0, 2 * N_PAIRS + p)),
            pl.BlockSpec((None, tq, 2), lambda p, i: (p, i, 0)),
            pl.BlockSpec((None, s, 2), lambda p, i: (p, 0, 0)),
        ],
        out_specs=[
            pl.BlockSpec((tq, LANES), lambda p, i: (i, p)),
            pl.BlockSpec((None, None, 2, tq), lambda p, i: (p, i, 0, 0)),
        ],
        out_shape=[SDS((s, ATTN_W), BF16), SDS((N_PAIRS, s // tq, 2, tq), F32)],
        scratch_shapes=[pltpu.VMEM((2 * nb, tk, LANES), BF16), pltpu.VMEM((nb, LANES, tk), BF16),
                        pltpu.VMEM((4, tk, tq), F32)],
        compiler_params=_cparams("arbitrary", "arbitrary"),
        name="attn_fwd",
    )(qkv, qkv, qkv, c_col, c_col)


def _pool_counts(row0, tm, w):
    t = row0 + _iota2((tm, 1), 0)
    return jnp.minimum(t + 1, w).astype(F32)


def _out_gate_up(attn_o, pool_o, w_out, x, g2, wg_t, wu_t, *, tm):
    s = x.shape[0]

    def body(a_ref, p_ref, wo_ref, x_ref, g_ref, wg_ref, wu_ref, x1_ref, h2_ref, r_ref, gate_ref, up_ref, act_ref):
        x1 = (x_ref[...] + jnp.dot(a_ref[...], wo_ref[0:ATTN_W, :], preferred_element_type=F32)
              + jnp.dot(p_ref[...], wo_ref[ATTN_W:, :], preferred_element_type=F32))
        r = lax.rsqrt(jnp.mean(x1 * x1, axis=-1, keepdims=True) + EPS)
        x1_ref[...] = x1
        r_ref[...] = r
        h2 = (x1 * r * g_ref[...]).astype(BF16)
        h2_ref[...] = h2
        gate = lax.dot_general(h2, wg_ref[...], NT, preferred_element_type=F32)
        up = lax.dot_general(h2, wu_ref[...], NT, preferred_element_type=F32)
        gate_ref[...] = gate.astype(BF16)
        up_ref[...] = up.astype(BF16)
        act_ref[...] = (gate * jax.nn.sigmoid(gate) * up).astype(BF16)

    row = lambda w: pl.BlockSpec((tm, w), lambda i: (i, 0))
    full = lambda a, b: pl.BlockSpec((a, b), lambda i: (0, 0))
    return pl.pallas_call(
        body,
        grid=(s // tm,),
        in_specs=[row(ATTN_W), row(POOL_W), full(D_MODEL, D_MODEL), row(D_MODEL), full(1, D_MODEL),
                  full(D_FF, D_MODEL), full(D_FF, D_MODEL)],
        out_specs=[row(D_MODEL), row(D_MODEL), row(1), row(D_FF), row(D_FF), row(D_FF)],
        out_shape=[SDS((s, D_MODEL), F32), SDS((s, D_MODEL), BF16), SDS((s, 1), F32), SDS((s, D_FF), BF16),
                   SDS((s, D_FF), BF16), SDS((s, D_FF), BF16)],
        compiler_params=_cparams("arbitrary"),
        name="out_gate_up",
    )(attn_o, pool_o, w_out, x, g2, wg_t, wu_t)


def _staggered(n, start, finish):
    pending = start(0)
    for k in range(n):
        following = start(k + 1) if k + 1 < n else None
        finish(k, pending)
        pending = following


def _down_final(act, wd, x1, gf, tgt, *, tm, sub):
    s = x1.shape[0]

    def body(a_ref, w_ref, x1_ref, g_ref, t_ref, dx2_ref, loss_ref, dgf_ref):
        @pl.when(pl.program_id(0) == 0)
        def _():
            loss_ref[...] = jnp.zeros_like(loss_ref)
            dgf_ref[...] = jnp.zeros_like(dgf_ref)

        g = g_ref[...]

        def matmul(k):
            return jnp.dot(a_ref[k * sub:(k + 1) * sub, :], w_ref[...], preferred_element_type=F32)

        def rest(k, mm):
            rows = slice(k * sub, (k + 1) * sub)
            x2 = x1_ref[rows, :] + mm
            r = lax.rsqrt(jnp.mean(x2 * x2, axis=-1, keepdims=True) + EPS)
            xn = x2 * r
            diff = xn * g - t_ref[rows, :]
            loss_ref[...] += jnp.sum(diff * diff, axis=0, keepdims=True)
            dy = diff * (1.0 / D_MODEL)
            dgf_ref[...] += jnp.sum(dy * xn, axis=0, keepdims=True)
            dxn = dy * g
            dx2_ref[rows, :] = r * (dxn - xn * jnp.mean(dxn * xn, axis=-1, keepdims=True))

        _staggered(tm // sub, matmul, rest)

    row = lambda w: pl.BlockSpec((tm, w), lambda i: (i, 0))
    full = lambda a, b: pl.BlockSpec((a, b), lambda i: (0, 0))
    return pl.pallas_call(
        body,
        grid=(s // tm,),
        in_specs=[row(D_FF), full(D_FF, D_MODEL), row(D_MODEL), full(1, D_MODEL), row(D_MODEL)],
        out_specs=[row(D_MODEL), full(1, D_MODEL), full(1, D_MODEL)],
        out_shape=[SDS((s, D_MODEL), F32), SDS((1, D_MODEL), F32), SDS((1, D_MODEL), F32)],
        compiler_params=_cparams("arbitrary"),
        name="down_final",
    )(act, wd, x1, gf, tgt)


def _swiglu_bwd(dx2, wd, gate, up, *, tm, tn):
    s = dx2.shape[0]

    def body(d_ref, w_ref, gate_ref, up_ref, dgate_ref, dup_ref):
        dact = lax.dot_general(d_ref[...].astype(BF16), w_ref[...], NT, preferred_element_type=F32)
        gate = gate_ref[...].astype(F32)
        sg = jax.nn.sigmoid(gate)
        dup_ref[...] = (dact * (gate * sg)).astype(BF16)
        dgate_ref[...] = (dact * up_ref[...].astype(F32) * (sg * (1.0 + gate * (1.0 - sg)))).astype(BF16)

    ospec = pl.BlockSpec((tm, tn), lambda c, r: (r, c))
    return pl.pallas_call(
        body,
        grid=(D_FF // tn, s // tm),
        in_specs=[pl.BlockSpec((tm, D_MODEL), lambda c, r: (r, 0)), pl.BlockSpec((tn, D_MODEL), lambda c, r: (c, 0)),
                  ospec, ospec],
        out_specs=[ospec, ospec],
        out_shape=[SDS((s, D_FF), BF16), SDS((s, D_FF), BF16)],
        compiler_params=_cparams("arbitrary", "arbitrary"),
        name="swiglu_bwd",
    )(dx2, wd, gate, up)


def _mm_tn_stacked(as_, rows, b, *, ts, name, windows=None):
    s, nb_ = b.shape
    n = len(as_)
    offsets = [sum(rows[:i]) for i in range(n)]
    total = sum(rows)
    if windows is None:
        acc_rows, out_shape = total, (total, nb_)
    else:
        count, step, size = windows
        acc_rows, out_shape = max(total, (count - 1) * step + size), (count, size, nb_)

    def body(*refs):
        a_refs, b_ref, o_ref, acc_ref = refs[:n], refs[n], refs[n + 1], refs[n + 2]
        k = pl.program_id(0)

        @pl.when(k == 0)
        def _():
            acc_ref[...] = jnp.zeros_like(acc_ref)

        bv = b_ref[...].astype(BF16)
        for a_ref, off, cnt in zip(a_refs, offsets, rows):
            part = lax.dot_general(a_ref[...].astype(BF16), bv, TN, preferred_element_type=F32)
            acc_ref[off:off + cnt, :] += part[0:cnt, :]

        @pl.when(k == s // ts - 1)
        def _():
            if windows is None:
                o_ref[...] = acc_ref[...].astype(BF16)
            else:
                for d in range(count):
                    o_ref[d] = acc_ref[d * step:d * step + size, :].astype(BF16)

    return pl.pallas_call(
        body,
        grid=(s // ts,),
        in_specs=[pl.BlockSpec((ts, a.shape[1]), lambda k: (k, 0)) for a in as_] + [pl.BlockSpec((ts, nb_), lambda k: (k, 0))],
        out_specs=pl.BlockSpec(out_shape, lambda k: (0,) * len(out_shape)),
        out_shape=SDS(out_shape, BF16),
        scratch_shapes=[pltpu.VMEM((acc_rows, nb_), F32)],
        compiler_params=_cparams("arbitrary"),
        name=name,
    )(*as_, b)


def _norm_bwd(dh, x, r, g, dres):
    xn = x * r
    dxn = dh * g
    dx = dres + r * (dxn - xn * jnp.mean(dxn * xn, axis=-1, keepdims=True))
    return dx, jnp.sum(dh * xn, axis=0, keepdims=True)


def _mlp_in_pool_bwd(dgate, dup, wg_t, wu_t, w_out, x1, r2, g2, dx2, pooled, w_pool, pool_scale, *, tm):
    s = x1.shape[0]
    nt = s // tm
    ng = len(POOL_WINDOWS)

    def body(dg_ref, dup_ref, wg_ref, wu_ref, wo_ref, x_ref, r_ref, g_ref, d_ref, p_ref, w_ref, sc_ref,
             dx1_ref, dattn_ref, du_ref, dg2_ref, dw_ref, dsc_ref, head_ref):
        i = pl.program_id(0)

        @pl.when(i == 0)
        def _():
            dg2_ref[...] = jnp.zeros_like(dg2_ref)
            head_ref[...] = jnp.zeros_like(head_ref)
            dw_ref[...] = jnp.zeros_like(dw_ref)
            dsc_ref[...] = jnp.zeros_like(dsc_ref)

        dh2 = (jnp.dot(dg_ref[...], wg_ref[...], preferred_element_type=F32)
               + jnp.dot(dup_ref[...], wu_ref[...], preferred_element_type=F32))
        dx1, dg2 = _norm_bwd(dh2, x_ref[...], r_ref[...], g_ref[...], d_ref[...])
        dg2_ref[...] += dg2
        dx1_ref[...] = dx1
        dmix = lax.dot_general(dx1.astype(BF16), wo_ref[...], NT, preferred_element_type=F32)
        dattn_ref[...] = dmix[:, 0:ATTN_W]
        row0 = (nt - 1 - i) * tm
        for g, w in enumerate(POOL_WINDOWS):
            cols = slice(g * POOL_G, (g + 1) * POOL_G)
            wb = w_ref[g].astype(BF16)
            pooled_g = p_ref[:, cols]
            dpo = dmix[:, ATTN_W + g * POOL_G:ATTN_W + (g + 1) * POOL_G]
            mixed = jnp.dot(pooled_g, wb, preferred_element_type=F32)
            dsc_ref[:, cols] += jnp.sum(dpo * mixed, axis=0, keepdims=True)
            dmp = (dpo * sc_ref[:, cols]).astype(BF16)
            dw_ref[g] += lax.dot_general(pooled_g, dmp, TN, preferred_element_type=F32)
            dpooled = lax.dot_general(dmp, wb, NT, preferred_element_type=F32)
            a = dpooled / _pool_counts(row0, tm, w)
            acc = jnp.concatenate([a, head_ref[:, cols]], axis=0)
            head_ref[:, cols] = a[0:HALO, :]
            k = 1
            while k < w:
                acc = acc + pltpu.roll(acc, tm + HALO - k, axis=0)
                k *= 2
            du_ref[:, cols] = (acc[0:tm, :] - dpooled).astype(BF16)

    row = lambda w: pl.BlockSpec((tm, w), lambda i: (nt - 1 - i, 0))
    full = lambda a, b: pl.BlockSpec((a, b), lambda i: (0, 0))
    pool_w = pl.BlockSpec((ng, POOL_G, POOL_G), lambda i: (0, 0, 0))
    return pl.pallas_call(
        body,
        grid=(nt,),
        in_specs=[row(D_FF), row(D_FF), full(D_FF, D_MODEL), full(D_FF, D_MODEL), full(D_MODEL, D_MODEL),
                  row(D_MODEL), row(1), full(1, D_MODEL), row(D_MODEL), row(POOL_W), pool_w, full(1, POOL_W)],
        out_specs=[row(D_MODEL), row(ATTN_W), row(POOL_W), full(1, D_MODEL), pool_w, full(1, POOL_W)],
        out_shape=[SDS((s, D_MODEL), F32), SDS((s, ATTN_W), F32), SDS((s, POOL_W), BF16), SDS((1, D_MODEL), F32),
                   SDS((ng, POOL_G, POOL_G), F32), SDS((1, POOL_W), F32)],
        scratch_shapes=[pltpu.VMEM((HALO, POOL_W), F32)],
        compiler_params=_cparams("arbitrary"),
        name="mlp_in_pool_bwd",
    )(dgate, dup, wg_t, wu_t, w_out, x1, r2, g2, dx2, pooled, w_pool, pool_scale)


SUM_ROWS = 16


def _heads_t(t):
    n = t.shape[0]
    lane = _iota2((n, LANES), 1)
    tf = t.astype(F32)
    halves = jnp.concatenate([jnp.where(lane < HEAD_DIM, tf, 0.0).T, jnp.where(lane < HEAD_DIM, 0.0, tf).T], axis=1)
    r, c = _iota2((SUM_ROWS, 2 * n), 0), _iota2((SUM_ROWS, 2 * n), 1)
    ones = jnp.where(((r == 0) & (c < n)) | ((r == 4) & (c >= n)), 1.0, 0.0)
    return jnp.concatenate([halves, ones], axis=0).astype(BF16)


def _attn_bwd(qkv, attn_o, d_attn, rowb, ck_col, *, tq):
    s = qkv.shape[0]
    tk = tq
    nb = s // tq
    rows_t = LANES + SUM_ROWS

    def body(q_ref, k_ref, v_ref, o_ref, do_ref, rowb_ref, ck_ref, dq_ref, dk_ref, dv_ref, dck_ref, dcq_ref,
             dqt_ref, delta_ref, kp_ref, qp_ref, dob_ref, qt_ref, kt_ref, dot_ref, front_ref):
        lane = _iota2((tq, LANES), 1)
        lo = lane < HEAD_DIM
        first = _iota2((8, LANES), 1) < HEAD_DIM
        sel = jnp.where(_iota2((8, LANES), 0) < 4, jnp.where(first, 1.0, 0.0), jnp.where(first, 0.0, 1.0))

        def prep(b, _):
            st = pl.multiple_of(b * tq, tq)
            do2 = do_ref[pl.ds(st, tq), :]
            delta_ref[b] = _sel_dot(sel, do2 * o_ref[pl.ds(st, tq), :].astype(F32), NT)
            dob_ref[pl.ds(st, tq), :] = do2.astype(BF16)
            dqt_ref[b] = jnp.zeros((rows_t, tq), F32)
            k2 = k_ref[pl.ds(st, tq), :].astype(F32)
            q2 = q_ref[pl.ds(st, tq), :].astype(F32)
            ck = ck_ref[pl.ds(st, tq), :]
            for h in range(2):
                kp_ref[h * nb + b] = _augment(k2, h, -ck[:, h:h + 1], True)
                qp_ref[h * nb + b] = _augment(q2 * Q_SCALE, h, jnp.zeros((tq, 1), F32), False)
            qt_ref[b] = _heads_t(q2)
            kt_ref[b] = _heads_t(k2)
            dot_ref[b] = _heads_t(do2)[0:LANES, :]
            return 0

        lax.fori_loop(0, nb, prep, 0)

        def split(t):
            z = jnp.zeros_like(t)
            return jnp.where(lo, t, z), jnp.where(lo, z, t)

        def kv_block(j, _):
            st_j = pl.multiple_of(j * tk, tk)
            vs = split(v_ref[pl.ds(st_j, tk), :])
            kt = kt_ref[j]

            def stage(i, slot):
                ic = jnp.minimum(i, nb - 1)
                do2 = dob_ref[pl.ds(pl.multiple_of(ic * tq, tq), tq), :]
                for h in range(2):
                    front_ref[4 * slot + h] = lax.dot_general(kp_ref[h * nb + j], qp_ref[h * nb + ic], NT,
                                                              preferred_element_type=F32)
                    front_ref[4 * slot + 2 + h] = lax.dot_general(vs[h], do2, NT, preferred_element_type=F32)

            def q_block(i, slot, carry, diagonal):
                dkt, dvt = carry
                ic = jnp.minimum(i, nb - 1)
                rb = rowb_ref[ic] + jnp.where(i < nb, 0.0, NEG)
                dl = delta_ref[ic]
                pts, dsts = [], []
                for h in range(2):
                    st = front_ref[4 * slot + h] + rb[h:h + 1, :]
                    if diagonal:
                        st = jnp.where(_iota2((tk, tq), 0) <= _iota2((tk, tq), 1), st, NEG)
                    pt = jnp.exp(st)
                    pts.append(pt.astype(BF16))
                    dsts.append((pt * (front_ref[4 * slot + 2 + h] - dl[4 * h:4 * h + 1, :])).astype(BF16))
                dvt = dvt + lax.dot_general(dot_ref[ic], jnp.concatenate(pts, axis=1), NT, preferred_element_type=F32)
                dkt = dkt + lax.dot_general(qt_ref[ic], jnp.concatenate(dsts, axis=1), NT, preferred_element_type=F32)
                dqt_ref[ic] += jnp.dot(kt, jnp.concatenate(dsts, axis=0), preferred_element_type=F32)
                return dkt, dvt

            def run(i0, steps, carry):
                for d in range(steps):
                    stage(i0 + d + 1, d % 2)
                    carry = q_block(i0 + d, 1 - d % 2, carry, False)
                return carry

            stage(j, 0)
            stage(j + 1, 1)
            carry = q_block(j, 0, (jnp.zeros((rows_t, tk), F32), jnp.zeros((LANES, tk), F32)), True)
            first, left = j + 1, nb - 1 - j
            for size in UNROLLS:
                trips = _shift_div(left + 1 if size == UNROLLS[-1] else left, size)
                carry = lax.fori_loop(0, trips, lambda t, c, i0=first, n=size: run(i0 + n * t, n, c), carry)
                first, left = first + size * trips, left - size * trips
            dkt, dvt = carry
            dk_ref[pl.ds(st_j, tk), :] = (dkt[0:LANES, :].T * Q_SCALE).astype(BF16)
            dv_ref[pl.ds(st_j, tk), :] = dvt.T.astype(BF16)
            dck_ref[j] = dkt[LANES:LANES + 8, :]
            return 0

        lax.fori_loop(0, nb, kv_block, 0)

        def finish(b, _):
            acc = dqt_ref[b]
            dq_ref[pl.ds(pl.multiple_of(b * tq, tq), tq), :] = (acc[0:LANES, :].T * Q_SCALE).astype(BF16)
            dcq_ref[b] = acc[LANES:LANES + 8, :]
            return 0

        lax.fori_loop(0, nb, finish, 0)

    col = lambda off: pl.BlockSpec((s, LANES), lambda p: (0, off + p))
    sums = pl.BlockSpec((None, nb, 8, tq), lambda p: (p, 0, 0, 0))
    return pl.pallas_call(
        body,
        grid=(N_PAIRS,),
        in_specs=[col(0), col(N_PAIRS), col(2 * N_PAIRS), col(0), col(0),
                  pl.BlockSpec((None, nb, 2, tq), lambda p: (p, 0, 0, 0)),
                  pl.BlockSpec((None, s, 2), lambda p: (p, 0, 0))],
        out_specs=[col(0), col(0), col(0), sums, sums],
        out_shape=[SDS((s, ATTN_W), BF16), SDS((s, ATTN_W), BF16), SDS((s, ATTN_W), BF16),
                   SDS((N_PAIRS, nb, 8, tq), F32), SDS((N_PAIRS, nb, 8, tq), F32)],
        scratch_shapes=[pltpu.VMEM((nb, rows_t, tq), F32), pltpu.VMEM((nb, 8, tq), F32),
                        pltpu.VMEM((2 * nb, tk, LANES), BF16), pltpu.VMEM((2 * nb, tq, LANES), BF16),
                        pltpu.VMEM((s, LANES), BF16), pltpu.VMEM((nb, rows_t, 2 * tq), BF16),
                        pltpu.VMEM((nb, rows_t, 2 * tk), BF16), pltpu.VMEM((nb, LANES, 2 * tq), BF16),
                        pltpu.VMEM((8, tk, tq), F32)],
        compiler_params=_cparams("arbitrary"),
        name="attn_bwd",
    )(qkv, qkv, qkv, attn_o, d_attn, rowb, ck_col)


def _forget_bwd(dc_t, fl_t, b_rows):
    rows = fl_t.shape[0]
    nb = rows // N_HEADS

    def body(dc_ref, fl_ref, b_ref, dfl_ref, db_ref):
        dc = dc_ref[...]
        lower = _iota2((LANES, LANES), 0) >= _iota2((LANES, LANES), 1)
        ones = jnp.ones((LANES, LANES), F32)
        rr, cc, same = _head_block_masks(rows, nb)
        dlf = _dot_sel(dc, lower) + _sel_dot(same & (cc > rr), _dot_sel(dc, ones))
        dfl = dlf / (1.0 + jnp.exp(fl_ref[...] + b_ref[...]))
        dfl_ref[...] = dfl
        shift = nb.bit_length() - 1
        hsel = lax.shift_right_logical(_iota2((N_HEADS, rows), 1), shift) == _iota2((N_HEADS, rows), 0)
        db_ref[...] = _sel_dot(hsel, _dot_sel(dfl, ones))

    return pl.pallas_call(body, out_shape=[SDS(fl_t.shape, F32), SDS((N_HEADS, LANES), F32)],
                          compiler_params=_cparams(), name="forget_bwd")(dc_t, fl_t, b_rows)


def _in_bwd(dq, dk, dv, du, dfl, w_in_t, x, r1, g1, dx1, *, tm):
    s = x.shape[0]
    pieces = ((0, ATTN_W), (ATTN_W, 2 * ATTN_W), (2 * ATTN_W, QKV_W), (U_OFF, F_OFF), (F_OFF, IN_PAD))

    def body(dq_ref, dk_ref, dv_ref, du_ref, df_ref, w_ref, x_ref, r_ref, g_ref, d_ref, dx_ref, dg1_ref):
        @pl.when(pl.program_id(0) == 0)
        def _():
            dg1_ref[...] = jnp.zeros_like(dg1_ref)

        dh = None
        for ref, (c0, c1) in zip((dq_ref, dk_ref, dv_ref, du_ref, df_ref), pieces):
            t = jnp.dot(ref[...], w_ref[c0:c1, :], preferred_element_type=F32)
            dh = t if dh is None else dh + t
        dx, dg1 = _norm_bwd(dh, x_ref[...], r_ref[...], g_ref[...], d_ref[...])
        dx_ref[...] = dx
        dg1_ref[...] += dg1

    row = lambda w: pl.BlockSpec((tm, w), lambda i: (i, 0))
    full = lambda a, b: pl.BlockSpec((a, b), lambda i: (0, 0))
    return pl.pallas_call(
        body,
        grid=(s // tm,),
        in_specs=[row(ATTN_W), row(ATTN_W), row(ATTN_W), row(POOL_W), row(LANES), full(IN_PAD, D_MODEL),
                  row(D_MODEL), row(1), full(1, D_MODEL), row(D_MODEL)],
        out_specs=[row(D_MODEL), full(1, D_MODEL)],
        out_shape=[SDS((s, D_MODEL), F32), SDS((1, D_MODEL), F32)],
        compiler_params=_cparams("arbitrary"),
        name="in_bwd",
    )(dq, dk, dv, du, dfl, w_in_t, x, r1, g1, dx1)


def _tiles(s):
    big = min(512, s)
    return dict(row=big, attn=min(256, s // 2), ff_rows=min(256, s), tall=min(1024, s))


def _tie(a, token):
    return a + token[0:1, 0:1].astype(a.dtype)


def _local_step(x, tgt, p, weight, emit, started):
    s = x.shape[0]
    t = _tiles(s)
    tm, tq = t["row"], t["attn"]
    nb = s // LANES
    nqb = s // tq
    g1, g2, gf = p["norm1_g"], p["norm2_g"], p["final_g"].reshape(1, D_MODEL)
    w_pool, pool_scale = p["w_pool"][0], p["pool_scale"]

    h, r1 = _norm1(x, _tie(g1, started), tm=tm)
    w_in_t = weight("w_in", h)
    qkv, fl, pooled, pool_o = _in_proj_pool(h, w_in_t, w_pool, pool_scale, tm=t["tall"])
    fl_t = fl[:, :N_HEADS].T.reshape(N_HEADS * nb, LANES)
    b_rows = jnp.repeat(p["b_forget"].reshape(N_HEADS), nb).reshape(N_HEADS * nb, 1)
    c = _forget_cumsum(fl_t, b_rows).reshape(N_PAIRS, 2, s)
    c_col = c.transpose(0, 2, 1)
    c_rowblk = c.reshape(N_PAIRS, 2, nqb, tq).transpose(0, 2, 1, 3)
    attn_o, lse = _attn_fwd(qkv, c_col, tk=tq)
    lse = lse.reshape(N_PAIRS, nqb // 2, 2, 2, tq).transpose(0, 1, 3, 2, 4).reshape(N_PAIRS, nqb, 2, tq)
    w_out = weight("w_out", attn_o)
    wg_t, wu_t = weight("w_gate_up", attn_o)
    x1, h2, r2, gate, up, act = _out_gate_up(attn_o, pool_o, w_out, x, g2, wg_t, wu_t, tm=t["ff_rows"])
    wd = weight("w_down", act)
    dx2, loss_row, d_gf = _down_final(act, wd, x1, gf, tgt, tm=tm, sub=min(128, tm))

    dgate, dup = _swiglu_bwd(dx2, wd, gate, up, tm=t["ff_rows"], tn=D_FF)
    d_wd = _mm_tn_stacked([act], [D_FF], dx2, ts=t["tall"], name="grad_w_down")
    d_wg_t = _mm_tn_stacked([dgate], [D_FF], h2, ts=t["tall"], name="grad_w_gate")
    d_wu_t = _mm_tn_stacked([dup], [D_FF], h2, ts=t["tall"], name="grad_w_up")
    dx1, d_attn, du, d_g2, d_wpool, d_pscale = _mlp_in_pool_bwd(dgate, dup, wg_t, wu_t, w_out, x1, r2, g2, dx2, pooled,
                                                               w_pool, pool_scale, tm=t["ff_rows"])
    d_wo = _mm_tn_stacked([attn_o, pool_o], [ATTN_W, POOL_W], dx1, ts=t["tall"], name="grad_w_out")
    token = emit(("w_down", "w_gate", "w_up", "w_out"), (d_wd, d_wg_t, d_wu_t, d_wo))
    rowb = _tie(c_rowblk - lse, token)
    dq, dk, dv, dck, dcq = _attn_bwd(qkv, attn_o, d_attn, rowb, c_col, tq=tq)
    dc_t = (dcq - dck)[:, :, 0::4, :].transpose(0, 2, 1, 3).reshape(N_HEADS * nb, LANES)
    dfl_t, db = _forget_bwd(dc_t, fl_t, b_rows)
    dfl = jnp.pad(dfl_t.reshape(N_HEADS, s).T, ((0, 0), (0, LANES - N_HEADS))).astype(BF16)
    d_w_in_t = _mm_tn_stacked([dq, dk, dv, dfl, du], [ATTN_W, ATTN_W, ATTN_W, N_HEADS, POOL_W], h, ts=t["tall"],
                              name="grad_w_in",
                              windows=(N_DEV, IN_STEP, IN_WINDOW))
    token = emit(("w_in",), (d_w_in_t,))
    dx, d_g1 = _in_bwd(dq, dk, dv, du, dfl, w_in_t, x, r1, _tie(g1, token), dx1, tm=tm)

    small = dict(norm1_g=d_g1, b_forget=db[:, 0].reshape(1, N_HEADS), w_pool=d_wpool, pool_scale=d_pscale,
                 norm2_g=d_g2, final_g=d_gf)
    return loss_row, dx, small


def _my_index():
    return 4 * lax.axis_index("x") + 2 * lax.axis_index("y") + lax.axis_index("c")


def _peer(k):
    pos = [lax.axis_index(a) for a in ("x", "y", "c")]
    flipped = tuple(1 - p if (k >> b) & 1 else p for p, b in zip(pos, (2, 1, 0)))
    return flipped, 4 * flipped[0] + 2 * flipped[1] + flipped[2]


_HBM = pl.BlockSpec(memory_space=pltpu.HBM)
_SEM = pl.BlockSpec(memory_space=pltpu.SEMAPHORE)
_DATAFLOW = pltpu.SideEffectType.DATAFLOW_SIDE_EFFECTING


ALL_PEERS = tuple(range(1, N_DEV))
SAME_CORE = (1, 2, 4, 6)


def _peer_copies(ins, lands, send_sems, recv_sems, scatter, peers, arrivals):
    me = _my_index()
    copies = []
    for w in range(len(ins)):
        for k in peers[w]:
            dev, idx = _peer(k)
            copies.append(pltpu.make_async_remote_copy(
                src_ref=ins[w].at[idx] if scatter[w] else ins[w], dst_ref=lands[w].at[idx if arrivals else me],
                send_sem=send_sems[w].at[k - 1], recv_sem=recv_sems[w].at[k - 1], device_id=dev, device_id_type=MESH))
    return copies


def _own_copies(ins, lands, send_sems, scatter):
    me = _my_index()
    return [pltpu.make_async_copy(ins[w].at[me] if scatter[w] else ins[w], lands[w].at[me], send_sems[w].at[N_DEV - 1])
            for w in range(len(ins))]


def _forward_copies(land, send_sems, recv_sems, arrivals):
    sibling, _ = _peer(1)
    copies = []
    for j, k in enumerate(SAME_CORE[1:]):
        src, dst = _peer(k)[1], _peer(k ^ 1 if arrivals else k)[1]
        copies.append(pltpu.make_async_remote_copy(
            src_ref=land.at[src], dst_ref=land.at[dst], send_sem=send_sems.at[j], recv_sem=recv_sems.at[j],
            device_id=sibling, device_id_type=MESH))
    return copies


def _forward_start(land, name):
    def body(land_ref, send_sems, recv_sems, land_thru, token):
        for cp in _forward_copies(land_ref, send_sems, recv_sems, False):
            cp.start()
        token[...] = jnp.zeros_like(token)

    sem = pltpu.SemaphoreType.DMA((len(SAME_CORE) - 1,))
    send, recv, thru, _ = pl.pallas_call(
        body,
        in_specs=[_HBM],
        out_specs=[_SEM, _SEM, _HBM, pl.BlockSpec(memory_space=pltpu.VMEM)],
        out_shape=[sem, sem, pltpu.HBM(land.shape, land.dtype), SDS((8, LANES), F32)],
        input_output_aliases={0: 2},
        compiler_params=pltpu.CompilerParams(has_side_effects=_DATAFLOW),
        name=name,
    )(land)
    return send, recv, thru


def _forward_wait(handle, after, name):
    def body(land_ref, send_sems, recv_sems, after_ref, land_out):
        for cp in _forward_copies(land_ref, send_sems, recv_sems, False):
            cp.wait_send()
        for cp in _forward_copies(land_ref, send_sems, recv_sems, True):
            cp.wait_recv()

    send, recv, land = handle
    return pl.pallas_call(
        body,
        in_specs=[_HBM, _SEM, _SEM, pl.BlockSpec(memory_space=pl.ANY)],
        out_specs=_HBM,
        out_shape=pltpu.HBM(land.shape, land.dtype),
        input_output_aliases={0: 0},
        compiler_params=pltpu.CompilerParams(has_side_effects=_DATAFLOW),
        name=name,
    )(land, send, recv, after)


def _exchange_start(arrays, scatter, name, peers=None):
    n = len(arrays)
    peers = peers or [ALL_PEERS] * n
    land_shapes = [(N_DEV,) + tuple(a.shape[1:] if sc else a.shape) for a, sc in zip(arrays, scatter)]

    def body(*refs):
        ins, lands = refs[:n], refs[n:2 * n]
        send_sems, recv_sems = refs[2 * n:3 * n], refs[3 * n:4 * n]
        token = refs[6 * n]
        for cp in _peer_copies(ins, lands, send_sems, recv_sems, scatter, peers, False):
            cp.start()
        for cp in _own_copies(ins, lands, send_sems, scatter):
            cp.start()
        token[...] = jnp.zeros_like(token)

    sends, recvs = pltpu.SemaphoreType.DMA((N_DEV,)), pltpu.SemaphoreType.DMA((N_DEV - 1,))
    outs = pl.pallas_call(
        body,
        in_specs=[_HBM] * (2 * n),
        out_specs=[_SEM] * (2 * n) + [_HBM] * (2 * n) + [pl.BlockSpec(memory_space=pltpu.VMEM)],
        out_shape=[sends] * n + [recvs] * n + [pltpu.HBM(a.shape, a.dtype) for a in arrays]
        + [pltpu.HBM(sh, a.dtype) for sh, a in zip(land_shapes, arrays)] + [SDS((8, LANES), F32)],
        input_output_aliases={i: 2 * n + i for i in range(2 * n)},
        compiler_params=pltpu.CompilerParams(has_side_effects=_DATAFLOW),
        name=name,
    )(*[pltpu.with_memory_space_constraint(a, pltpu.HBM) for a in arrays],
      *[pltpu.with_memory_space_constraint(lax.empty(sh, a.dtype), pltpu.HBM) for sh, a in zip(land_shapes, arrays)])
    handles = [dict(send=outs[w], recv=outs[n + w], src=outs[2 * n + w], land=outs[3 * n + w], scatter=scatter[w],
                    peers=peers[w]) for w in range(n)]
    return handles, outs[4 * n]


def _exchange_wait(handles, after, name):
    n = len(handles)
    scatter, peers = [h["scatter"] for h in handles], [h["peers"] for h in handles]

    def body(*refs):
        ins, lands = refs[:n], refs[n:2 * n]
        send_sems, recv_sems = refs[2 * n:3 * n], refs[3 * n:4 * n]
        for cp in _peer_copies(ins, lands, send_sems, recv_sems, scatter, peers, False):
            cp.wait_send()
        for cp in _peer_copies(ins, lands, send_sems, recv_sems, scatter, peers, True):
            cp.wait_recv()
        for cp in _own_copies(ins, lands, send_sems, scatter):
            cp.wait()

    srcs, lands = [h["src"] for h in handles], [h["land"] for h in handles]
    outs = pl.pallas_call(
        body,
        in_specs=[_HBM] * (2 * n) + [_SEM] * (2 * n) + [pl.BlockSpec(memory_space=pl.ANY)],
        out_specs=[_HBM] * (2 * n),
        out_shape=[pltpu.HBM(a.shape, a.dtype) for a in srcs + lands],
        input_output_aliases={i: i for i in range(2 * n)},
        compiler_params=pltpu.CompilerParams(has_side_effects=_DATAFLOW),
        name=name,
    )(*srcs, *lands, *[h["send"] for h in handles], *[h["recv"] for h in handles], after)
    return outs[n:]


def _adamw(parts, w, m, v, name):
    rows, cols = w.shape
    tr = rows // 4 if rows % 32 == 0 else rows

    def body(p_ref, w_ref, m_ref, v_ref, g_ref, d_ref, mo_ref, vo_ref):
        g = p_ref[0].astype(F32)
        for d in range(1, N_DEV):
            g = g + p_ref[d].astype(F32)
        g_ref[...] = g
        d_ref[...], mo_ref[...], vo_ref[...] = _adam_update(g, w_ref[...], m_ref[...], v_ref[...])

    blk = pl.BlockSpec((tr, cols), lambda i: (i, 0))
    return pl.pallas_call(
        body,
        grid=(rows // tr,),
        in_specs=[pl.BlockSpec((N_DEV, tr, cols), lambda i: (0, i, 0)), blk, blk, blk],
        out_specs=[blk] * 4,
        out_shape=[SDS((rows, cols), F32)] * 4,
        compiler_params=_cparams("arbitrary"),
        name=name,
    )(parts, w, m, v)


def _adamw_dense(parts, w, m, v, name, *, rows, shift):
    _, window, cols = parts.shape
    per_row = cols // LANES

    def body(p_ref, w_ref, m_ref, v_ref, g_ref, d_ref, mo_ref, vo_ref, sum_ref):
        g = p_ref[0].astype(F32)
        for d in range(1, N_DEV):
            g = g + p_ref[d].astype(F32)
        sum_ref[...] = g
        me = _my_index()
        for j in range(N_DEV):
            @pl.when(me == j)
            def _(j=j):
                for c in range(per_row):
                    at = (pl.ds(c, rows, stride=per_row), slice(None))
                    gc = sum_ref[j * shift:j * shift + rows, c * LANES:(c + 1) * LANES]
                    g_ref[at] = gc
                    d_ref[at], mo_ref[at], vo_ref[at] = _adam_update(gc, w_ref[at], m_ref[at], v_ref[at])

    return pl.pallas_call(
        body,
        out_shape=[SDS(w.shape, F32)] * 4,
        scratch_shapes=[pltpu.VMEM((window, cols), F32)],
        compiler_params=_cparams(),
        name=name,
    )(parts, w, m, v)


_ROW_OF = dict(norm1_g=(0, D_MODEL), norm2_g=(1, D_MODEL), final_g=(2, D_MODEL), pool_scale=(3, POOL_W),
               b_forget=(4, N_HEADS), loss=(5, 1))


def _pack_rows(vals):
    rows = [jnp.pad(vals[n].reshape(1, width).astype(F32), ((0, 0), (0, D_MODEL - width)))
            for n, (_, width) in sorted(_ROW_OF.items(), key=lambda kv: kv[1][0])]
    return jnp.concatenate(rows + [jnp.zeros((8 - len(rows), D_MODEL), F32)], axis=0)


def _adam_update(g, w, m, v):
    m_new = ADAM_B1 * m + (1.0 - ADAM_B1) * g
    v_new = ADAM_B2 * v + (1.0 - ADAM_B2) * (g * g)
    m_hat = m_new / (1.0 - ADAM_B1 ** ADAM_STEP)
    v_hat = v_new / (1.0 - ADAM_B2 ** ADAM_STEP)
    return -ADAM_LR * (m_hat / (jnp.sqrt(v_hat) + ADAM_EPS) + ADAM_WD * w), m_new, v_new


def _adamw_replicated(parts_rows, parts_pool, w, m, v):
    names = ("norm1_g", "norm2_g", "final_g", "pool_scale", "b_forget", "w_pool")
    shapes = {n: ((len(POOL_WINDOWS), POOL_G, POOL_G) if n == "w_pool" else (1, _ROW_OF[n][1])) for n in names}

    def body(rows_ref, pool_ref, *refs):
        ins, outs = refs[:3 * len(names)], refs[3 * len(names):]

        def total(n):
            if n == "w_pool":
                pieces = [pool_ref[d] for d in range(N_DEV)]
            else:
                row, width = _ROW_OF[n]
                pieces = [rows_ref[d, row:row + 1, 0:width] for d in range(N_DEV)]
            g = pieces[0]
            for p in pieces[1:]:
                g = g + p
            return g

        outs[0][...] = total("loss")
        for k, n in enumerate(names):
            g = total(n)
            delta, m_new, v_new = _adam_update(g, ins[3 * k][...], ins[3 * k + 1][...], ins[3 * k + 2][...])
            for o_ref, val in zip(outs[1 + 4 * k:5 + 4 * k], (g, delta, m_new, v_new)):
                o_ref[...] = val

    args = [d[n].reshape(shapes[n]) for n in names for d in (w, m, v)]
    res = pl.pallas_call(
        body,
        out_shape=[SDS((1, 1), F32)] + [SDS(shapes[n], F32) for n in names for _ in range(4)],
        compiler_params=_cparams(),
        name="adamw_replicated",
    )(parts_rows, parts_pool, *args)
    return res[0], {n: [r.reshape(w[n].shape) for r in res[1 + 4 * k:5 + 4 * k]] for k, n in enumerate(names)}


def kernel(x, norm1_g, w_in, b_forget, w_pool, pool_scale, w_out, norm2_g, w_gate, w_up, w_down, final_g, loss_target, m_norm1_g, m_w_in, m_b_forget, m_w_pool, m_pool_scale, m_w_out, m_norm2_g, m_w_gate, m_w_up, m_w_down, m_final_g, v_norm1_g, v_w_in, v_b_forget, v_w_pool, v_pool_scale, v_w_out, v_norm2_g, v_w_gate, v_w_up, v_w_down, v_final_g):
    big = ("w_in", "w_out", "w_gate", "w_up", "w_down")
    order = ("norm1_g", "w_in", "b_forget", "w_pool", "pool_scale", "w_out", "norm2_g", "w_gate", "w_up", "w_down",
             "final_g")
    w = dict(norm1_g=norm1_g, w_in=w_in, b_forget=b_forget, w_pool=w_pool, pool_scale=pool_scale, w_out=w_out,
             norm2_g=norm2_g, w_gate=w_gate, w_up=w_up, w_down=w_down, final_g=final_g)
    m = dict(norm1_g=m_norm1_g, w_in=m_w_in, b_forget=m_b_forget, w_pool=m_w_pool, pool_scale=m_pool_scale,
             w_out=m_w_out, norm2_g=m_norm2_g, w_gate=m_w_gate, w_up=m_w_up, w_down=m_w_down, final_g=m_final_g)
    v = dict(norm1_g=v_norm1_g, w_in=v_w_in, b_forget=v_b_forget, w_pool=v_w_pool, pool_scale=v_pool_scale,
             w_out=v_w_out, norm2_g=v_norm2_g, w_gate=v_w_gate, w_up=v_w_up, w_down=v_w_down, final_g=v_final_g)

    flipped = ("w_in", "w_gate", "w_up")
    shard = lambda d, n: d[n][0].T if n in flipped else d[n][0]
    gather, started = _exchange_start([shard(w, n).astype(BF16) for n in big], [False] * len(big), "gather_start",
                                      peers=[SAME_CORE if n == "w_in" else ALL_PEERS for n in big])
    gather = dict(zip(big, gather))

    def gathered(names, after):
        return _exchange_wait([gather[n] for n in names], after, "gather_wait_" + names[0])

    def weight(name, after):
        if name == "w_in":
            forward = _forward_start(gathered(["w_in"], after)[0], "gather_forward_start")
            full = _forward_wait(forward, after, "gather_forward_wait").reshape(IN_W, D_MODEL)
            f0 = QKV_W + N_HEADS
            return jnp.concatenate([full[:QKV_W], full[f0:], full[QKV_W:f0],
                                    jnp.zeros((IN_PAD - IN_W, D_MODEL), BF16)], axis=0)
        if name == "w_out":
            return gathered(["w_out"], after)[0].reshape(D_MODEL, D_MODEL)
        if name == "w_gate_up":
            return [g.reshape(D_FF, D_MODEL) for g in gathered(["w_gate", "w_up"], after)]
        return gathered(["w_down"], after)[0].reshape(D_FF, D_MODEL)

    rows = lambda g: g if g.ndim == 3 else g.reshape(N_DEV, g.shape[0] // N_DEV, g.shape[1])
    sent = {}

    def emit(names, grads):
        handles, token = _exchange_start([rows(g) for g in grads], [True] * len(names), "grads_start_" + names[0])
        sent.update(zip(names, handles))
        return token

    loss_row, dx, small_grads = _local_step(x[0], loss_target[0], w, weight, emit, started)

    packed = _pack_rows(dict(small_grads, loss=0.5 / D_MODEL * jnp.sum(loss_row)))
    small_handles, after = _exchange_start([packed, small_grads["w_pool"]], [False, False], "grads_start_replicated")

    outs = {}
    for name in ("w_down", "w_gate", "w_up", "w_out", "w_in"):
        (parts,) = _exchange_wait([sent[name]], after, "grads_wait_" + name)
        if name == "w_in":
            dense = lambda d: d[name].transpose(2, 0, 1).reshape(-1, LANES)
            outs[name] = _adamw_dense(parts, dense(w), dense(m), dense(v), "adamw_" + name, rows=IN_SHARD, shift=IN_SHIFT)
            after = outs[name][0]
            outs[name] = [a.reshape(-1, D_MODEL // LANES, LANES).transpose(1, 2, 0).reshape(1, D_MODEL, -1)
                          for a in outs[name]]
            continue
        outs[name] = _adamw(parts, shard(w, name), shard(m, name), shard(v, name), "adamw_" + name)
        after = outs[name][0]
        outs[name] = [(a.T if name in flipped else a)[None] for a in outs[name]]
    parts_rows, parts_pool = _exchange_wait(small_handles, after, "grads_wait_replicated")
    loss, small = _adamw_replicated(parts_rows, parts_pool, w, m, v)
    outs.update(small)

    return (loss.reshape(()), dx[None]) + tuple(outs[n][k] for k in range(4) for n in order)
```

```python
import jax
import jax.numpy as jnp
from jax import lax
from jax.experimental import pallas as pl
from jax.experimental.pallas import tpu as pltpu

F32 = jnp.float32
BF16 = jnp.bfloat16
SDS = jax.ShapeDtypeStruct

D_MODEL = 1024
ATTN_W = 512
N_HEADS = 8
HEAD_DIM = 64
Q_SCALE = HEAD_DIM ** -0.5
N_PAIRS = N_HEADS // 2
POOL_W = 512
POOL_WINDOWS = (2, 4, 8, 16)
POOL_G = 128
HALO = 16
IN_W = 3 * ATTN_W + N_HEADS + POOL_W
QKV_W = 3 * ATTN_W
U_OFF = QKV_W
F_OFF = QKV_W + POOL_W
IN_PAD = F_OFF + 128
D_FF = 2816
EPS = 1e-6
NEG = -1e30
N_DEV = 8
LANES = 128
BF16_ROWS = 16

IN_SHARD = IN_W // N_DEV
IN_STEP = IN_SHARD // BF16_ROWS * BF16_ROWS
IN_SHIFT = IN_SHARD - IN_STEP
IN_WINDOW = -(-((N_DEV - 1) * IN_SHIFT + IN_SHARD) // BF16_ROWS) * BF16_ROWS

ADAM_LR = 0.001
ADAM_B1 = 0.9
ADAM_B2 = 0.999
ADAM_EPS = 1e-08
ADAM_WD = 0.01
ADAM_STEP = 10

VMEM_LIMIT_BYTES = 56 * 1024 * 1024
MESH = pl.DeviceIdType.MESH
NT = (((1,), (1,)), ((), ()))
TN = (((0,), (0,)), ((), ()))


_UNREAD = pl.BlockSpec(memory_space=pl.ANY)


def _cparams(*sem):
    return pltpu.CompilerParams(dimension_semantics=sem or None, vmem_limit_bytes=VMEM_LIMIT_BYTES)


def _split3(a):
    hi = a.astype(BF16)
    r1 = a - hi.astype(F32)
    mid = r1.astype(BF16)
    lo = (r1 - mid.astype(F32)).astype(BF16)
    return hi, mid, lo


def _dot_sel(a, sel, dims=None):
    sb = sel.astype(BF16)
    if dims is None:
        return sum(jnp.dot(p, sb, preferred_element_type=F32) for p in _split3(a))
    return sum(lax.dot_general(p, sb, dims, preferred_element_type=F32) for p in _split3(a))


def _sel_dot(sel, a, dims=None):
    sb = sel.astype(BF16)
    if dims is None:
        return sum(jnp.dot(sb, p, preferred_element_type=F32) for p in _split3(a))
    return sum(lax.dot_general(sb, p, dims, preferred_element_type=F32) for p in _split3(a))


def _iota2(shape, dim):
    return lax.broadcasted_iota(jnp.int32, shape, dim)


UNROLLS = (8, 4, 2)


def _shift_div(x, n):
    return lax.shift_right_logical(x, n.bit_length() - 1)


def _norm1(x, g1, after, *, tm):
    s = x.shape[0]

    def body(x_ref, g_ref, _, h_ref, r_ref):
        xv = x_ref[...]
        r = lax.rsqrt(jnp.mean(xv * xv, axis=-1, keepdims=True) + EPS)
        h_ref[...] = (xv * r * g_ref[...]).astype(BF16)
        r_ref[...] = r

    row = lambda w: pl.BlockSpec((tm, w), lambda i: (i, 0))
    return pl.pallas_call(
        body,
        grid=(s // tm,),
        in_specs=[row(D_MODEL), pl.BlockSpec((1, D_MODEL), lambda i: (0, 0)), _UNREAD],
        out_specs=[row(D_MODEL), row(1)],
        out_shape=[SDS((s, D_MODEL), BF16), SDS((s, 1), F32)],
        compiler_params=_cparams("arbitrary"),
        name="norm1",
    )(x, g1, after)


def _in_proj_pool(h, w_in_t, w_pool, pool_scale, *, tm):
    s = h.shape[0]

    def body(h_ref, w_ref, wp_ref, sc_ref, qkv_ref, fl_ref, pooled_ref, po_ref, tail_ref):
        i = pl.program_id(0)

        @pl.when(i == 0)
        def _():
            tail_ref[...] = jnp.zeros_like(tail_ref)

        hv = h_ref[...]
        uv = lax.dot_general(hv, w_ref[U_OFF:F_OFF, :], NT, preferred_element_type=F32)
        qkv_ref[...] = lax.dot_general(hv, w_ref[0:QKV_W, :], NT, preferred_element_type=F32).astype(BF16)
        fl_ref[...] = lax.dot_general(hv, w_ref[F_OFF:IN_PAD, :], NT, preferred_element_type=F32)
        ext = jnp.concatenate([tail_ref[...], uv], axis=0)
        tail_ref[...] = uv[tm - HALO:, :]
        for g, w in enumerate(POOL_WINDOWS):
            cols = slice(g * POOL_G, (g + 1) * POOL_G)
            acc = ext[:, cols]
            k = 1
            while k < w:
                acc = acc + pltpu.roll(acc, k, axis=0)
                k *= 2
            pooled = (acc[HALO:, :] / _pool_counts(i * tm, tm, w) - uv[:, cols]).astype(BF16)
            pooled_ref[:, cols] = pooled
            mixed = jnp.dot(pooled, wp_ref[g].astype(BF16), preferred_element_type=F32)
            po_ref[:, cols] = (mixed * sc_ref[:, cols]).astype(BF16)

    row = lambda w: pl.BlockSpec((tm, w), lambda i: (i, 0))
    return pl.pallas_call(
        body,
        grid=(s // tm,),
        in_specs=[row(D_MODEL), pl.BlockSpec((IN_PAD, D_MODEL), lambda i: (0, 0)),
                  pl.BlockSpec((len(POOL_WINDOWS), POOL_G, POOL_G), lambda i: (0, 0, 0)),
                  pl.BlockSpec((1, POOL_W), lambda i: (0, 0))],
        out_specs=[row(QKV_W), row(LANES), row(POOL_W), row(POOL_W)],
        out_shape=[SDS((s, QKV_W), BF16), SDS((s, LANES), F32), SDS((s, POOL_W), BF16), SDS((s, POOL_W), BF16)],
        scratch_shapes=[pltpu.VMEM((HALO, POOL_W), F32)],
        compiler_params=_cparams("arbitrary"),
        name="in_proj_pool",
    )(h, w_in_t, w_pool, pool_scale)


def _head_block_masks(rows, nb):
    shift = nb.bit_length() - 1
    rr, cc = _iota2((rows, rows), 0), _iota2((rows, rows), 1)
    same = lax.shift_right_logical(rr, shift) == lax.shift_right_logical(cc, shift)
    return rr, cc, same


def _forget_cumsum(fl_t, b_rows):
    rows = fl_t.shape[0]
    nb = rows // N_HEADS

    def body(fl_ref, b_ref, c_ref):
        z = fl_ref[...] + b_ref[...]
        lf = jnp.minimum(z, 0.0) - jnp.log1p(jnp.exp(-jnp.abs(z)))
        upper = _iota2((LANES, LANES), 0) <= _iota2((LANES, LANES), 1)
        within = _dot_sel(lf, upper)
        tot = _dot_sel(lf, jnp.ones((LANES, LANES), F32))
        rr, cc, same = _head_block_masks(rows, nb)
        c_ref[...] = within + _sel_dot(same & (cc < rr), tot)

    return pl.pallas_call(body, out_shape=SDS(fl_t.shape, F32), compiler_params=_cparams(), name="forget_cumsum")(
        fl_t, b_rows)


BIAS_LANES = 3


def _augment(t, h, col, col_first):
    n = t.shape[0]
    lane = _iota2((n, LANES), 1)
    own = (lane < HEAD_DIM) if h == 0 else (lane >= HEAD_DIM)
    b0 = HEAD_DIM if h == 0 else 0
    c0, o0 = (b0, b0 + BIAS_LANES) if col_first else (b0 + BIAS_LANES, b0)
    x = jnp.where(own, t, 0.0)
    for off, piece in enumerate(_split3(col)):
        x = jnp.where(lane == c0 + off, piece.astype(F32), x)
    x = jnp.where((lane >= o0) & (lane < o0 + BIAS_LANES), 1.0, x)
    return x.astype(BF16)


def _attn_fwd(qkv, c_col, *, tk):
    s = qkv.shape[0]
    tq = 2 * tk
    nb = s // tk

    def body(q_ref, k_ref, v_ref, cq_ref, ck_ref, o_ref, lse_ref, kp_ref, vt_ref, st_ref):
        i = pl.program_id(1)

        @pl.when(i == 0)
        def _():
            def prep(jb, _):
                st = pl.multiple_of(jb * tk, tk)
                k2 = k_ref[pl.ds(st, tk), :].astype(F32)
                ck = ck_ref[pl.ds(st, tk), :]
                for h in range(2):
                    kp_ref[h * nb + jb] = _augment(k2, h, -ck[:, h:h + 1], True)
                vt_ref[jb] = v_ref[pl.ds(st, tk), :].astype(F32).T.astype(BF16)
                return 0

            lax.fori_loop(0, nb, prep, 0)

        qs = q_ref[...].astype(F32) * Q_SCALE
        cq = cq_ref[...]
        qp = [_augment(qs, h, cq[:, h:h + 1], False) for h in range(2)]

        def logits(j):
            return tuple(lax.dot_general(kp_ref[h * nb + j], qp[h], NT, preferred_element_type=F32) for h in range(2))

        def softmax_pv(j, slot, stats, masked):
            out = []
            for h in range(2):
                m, l, acc = stats[h]
                st = st_ref[2 * slot + h]
                if masked:
                    st = jnp.where(j * tk + _iota2((tk, tq), 0) <= i * tq + _iota2((tk, tq), 1), st, NEG)
                m_new = jnp.maximum(m, jnp.max(st, axis=0, keepdims=True))
                alpha = jnp.exp(m - m_new)
                p = jnp.exp(st - m_new)
                l = alpha * l + jnp.sum(p, axis=0, keepdims=True)
                vt = vt_ref[j, h * HEAD_DIM:(h + 1) * HEAD_DIM, :]
                acc = alpha * acc + jnp.dot(vt, p.astype(BF16), preferred_element_type=F32)
                out.append((m_new, l, acc))
            return tuple(out)

        def put(slot, j):
            for h, st in enumerate(logits(j)):
                st_ref[2 * slot + h] = st

        def run(j0, steps, stats):
            for d in range(steps):
                put(1 - d % 2, j0 + d + 1)
                stats = softmax_pv(j0 + d, d % 2, stats, False)
            return stats

        init = tuple((jnp.full((1, tq), NEG, F32), jnp.zeros((1, tq), F32), jnp.zeros((HEAD_DIM, tq), F32))
                     for _ in range(2))
        put(0, 0)
        first, left, stats = 0, 2 * i, init
        for size in UNROLLS:
            trips = _shift_div(left, size)
            stats = lax.fori_loop(0, trips, lambda t, st, j0=first, n=size: run(j0 + n * t, n, st), stats)
            first, left = first + size * trips, left - size * trips
        put(1, 2 * i + 1)
        stats = softmax_pv(2 * i, 0, stats, True)
        (ma, la, acca), (mb, lb, accb) = softmax_pv(2 * i + 1, 1, stats, True)
        o_ref[...] = jnp.concatenate([acca / la, accb / lb], axis=0).T.astype(BF16)
        lse_ref[...] = jnp.where(_iota2((2, tq), 0) == 0, ma + jnp.log(la), mb + jnp.log(lb))

    return pl.pallas_call(
        body,
        grid=(N_PAIRS, s // tq),
        in_specs=[
            pl.BlockSpec((tq, LANES), lambda p, i: (i, p)),
            pl.BlockSpec((s, LANES), lambda p, i: (0, N_PAIRS + p)),
            pl.BlockSpec((s, LANES), lambda p, i: (0, 2 * N_PAIRS + p)),
            pl.BlockSpec((None, tq, 2), lambda p, i: (p, i, 0)),
            pl.BlockSpec((None, s, 2), lambda p, i: (p, 0, 0)),
        ],
        out_specs=[
            pl.BlockSpec((tq, LANES), lambda p, i: (i, p)),
            pl.BlockSpec((None, None, 2, tq), lambda p, i: (p, i, 0, 0)),
        ],
        out_shape=[SDS((s, ATTN_W), BF16), SDS((N_PAIRS, s // tq, 2, tq), F32)],
        scratch_shapes=[pltpu.VMEM((2 * nb, tk, LANES), BF16), pltpu.VMEM((nb, LANES, tk), BF16),
                        pltpu.VMEM((4, tk, tq), F32)],
        compiler_params=_cparams("arbitrary", "arbitrary"),
        name="attn_fwd",
    )(qkv, qkv, qkv, c_col, c_col)


def _pool_counts(row0, tm, w):
    t = row0 + _iota2((tm, 1), 0)
    return jnp.minimum(t + 1, w).astype(F32)


def _out_gate_up(attn_o, pool_o, w_out, x, g2, wg_t, wu_t, *, tm):
    s = x.shape[0]

    def body(a_ref, p_ref, wo_ref, x_ref, g_ref, wg_ref, wu_ref, x1_ref, h2_ref, r_ref, gate_ref, up_ref, act_ref):
        x1 = (x_ref[...] + jnp.dot(a_ref[...], wo_ref[0:ATTN_W, :], preferred_element_type=F32)
              + jnp.dot(p_ref[...], wo_ref[ATTN_W:, :], preferred_element_type=F32))
        r = lax.rsqrt(jnp.mean(x1 * x1, axis=-1, keepdims=True) + EPS)
        x1_ref[...] = x1
        r_ref[...] = r
        h2 = (x1 * r * g_ref[...]).astype(BF16)
        h2_ref[...] = h2
        gate = lax.dot_general(h2, wg_ref[...], NT, preferred_element_type=F32)
        up = lax.dot_general(h2, wu_ref[...], NT, preferred_element_type=F32)
        gate_ref[...] = gate.astype(BF16)
        up_ref[...] = up.astype(BF16)
        act_ref[...] = (gate * jax.nn.sigmoid(gate) * up).astype(BF16)

    row = lambda w: pl.BlockSpec((tm, w), lambda i: (i, 0))
    full = lambda a, b: pl.BlockSpec((a, b), lambda i: (0, 0))
    return pl.pallas_call(
        body,
        grid=(s // tm,),
        in_specs=[row(ATTN_W), row(POOL_W), full(D_MODEL, D_MODEL), row(D_MODEL), full(1, D_MODEL),
                  full(D_FF, D_MODEL), full(D_FF, D_MODEL)],
        out_specs=[row(D_MODEL), row(D_MODEL), row(1), row(D_FF), row(D_FF), row(D_FF)],
        out_shape=[SDS((s, D_MODEL), F32), SDS((s, D_MODEL), BF16), SDS((s, 1), F32), SDS((s, D_FF), BF16),
                   SDS((s, D_FF), BF16), SDS((s, D_FF), BF16)],
        compiler_params=_cparams("arbitrary"),
        name="out_gate_up",
    )(attn_o, pool_o, w_out, x, g2, wg_t, wu_t)


def _staggered(n, start, finish):
    pending = start(0)
    for k in range(n):
        following = start(k + 1) if k + 1 < n else None
        finish(k, pending)
        pending = following


def _down_final(act, wd, x1, gf, tgt, *, tm, sub):
    s = x1.shape[0]

    def body(a_ref, w_ref, x1_ref, g_ref, t_ref, dx2_ref, loss_ref, dgf_ref):
        @pl.when(pl.program_id(0) == 0)
        def _():
            loss_ref[...] = jnp.zeros_like(loss_ref)
            dgf_ref[...] = jnp.zeros_like(dgf_ref)

        g = g_ref[...]

        def matmul(k):
            return jnp.dot(a_ref[k * sub:(k + 1) * sub, :], w_ref[...], preferred_element_type=F32)

        def rest(k, mm):
            rows = slice(k * sub, (k + 1) * sub)
            x2 = x1_ref[rows, :] + mm
            r = lax.rsqrt(jnp.mean(x2 * x2, axis=-1, keepdims=True) + EPS)
            xn = x2 * r
            diff = xn * g - t_ref[rows, :]
            loss_ref[...] += jnp.sum(diff * diff, axis=0, keepdims=True)
            dy = diff * (1.0 / D_MODEL)
            dgf_ref[...] += jnp.sum(dy * xn, axis=0, keepdims=True)
            dxn = dy * g
            dx2_ref[rows, :] = r * (dxn - xn * jnp.mean(dxn * xn, axis=-1, keepdims=True))

        _staggered(tm // sub, matmul, rest)

    row = lambda w: pl.BlockSpec((tm, w), lambda i: (i, 0))
    full = lambda a, b: pl.BlockSpec((a, b), lambda i: (0, 0))
    return pl.pallas_call(
        body,
        grid=(s // tm,),
        in_specs=[row(D_FF), full(D_FF, D_MODEL), row(D_MODEL), full(1, D_MODEL), row(D_MODEL)],
        out_specs=[row(D_MODEL), full(1, D_MODEL), full(1, D_MODEL)],
        out_shape=[SDS((s, D_MODEL), F32), SDS((1, D_MODEL), F32), SDS((1, D_MODEL), F32)],
        compiler_params=_cparams("arbitrary"),
        name="down_final",
    )(act, wd, x1, gf, tgt)


def _swiglu_bwd(dx2, wd, gate, up, *, tm, tn):
    s = dx2.shape[0]

    def body(d_ref, w_ref, gate_ref, up_ref, dgate_ref, dup_ref):
        dact = lax.dot_general(d_ref[...].astype(BF16), w_ref[...], NT, preferred_element_type=F32)
        gate = gate_ref[...].astype(F32)
        sg = jax.nn.sigmoid(gate)
        dup_ref[...] = (dact * (gate * sg)).astype(BF16)
        dgate_ref[...] = (dact * up_ref[...].astype(F32) * (sg * (1.0 + gate * (1.0 - sg)))).astype(BF16)

    ospec = pl.BlockSpec((tm, tn), lambda c, r: (r, c))
    return pl.pallas_call(
        body,
        grid=(D_FF // tn, s // tm),
        in_specs=[pl.BlockSpec((tm, D_MODEL), lambda c, r: (r, 0)), pl.BlockSpec((tn, D_MODEL), lambda c, r: (c, 0)),
                  ospec, ospec],
        out_specs=[ospec, ospec],
        out_shape=[SDS((s, D_FF), BF16), SDS((s, D_FF), BF16)],
        compiler_params=_cparams("arbitrary", "arbitrary"),
        name="swiglu_bwd",
    )(dx2, wd, gate, up)


def _mm_tn_stacked(as_, rows, b, *, ts, name, windows=None):
    s, nb_ = b.shape
    n = len(as_)
    offsets = [sum(rows[:i]) for i in range(n)]
    total = sum(rows)
    if windows is None:
        acc_rows, out_shape = total, (total, nb_)
    else:
        count, step, size = windows
        acc_rows, out_shape = max(total, (count - 1) * step + size), (count, size, nb_)

    def body(*refs):
        a_refs, b_ref, o_ref, acc_ref = refs[:n], refs[n], refs[n + 1], refs[n + 2]
        k = pl.program_id(0)

        @pl.when(k == 0)
        def _():
            acc_ref[...] = jnp.zeros_like(acc_ref)

        bv = b_ref[...].astype(BF16)
        for a_ref, off, cnt in zip(a_refs, offsets, rows):
            part = lax.dot_general(a_ref[...].astype(BF16), bv, TN, preferred_element_type=F32)
            acc_ref[off:off + cnt, :] += part[0:cnt, :]

        @pl.when(k == s // ts - 1)
        def _():
            if windows is None:
                o_ref[...] = acc_ref[...].astype(BF16)
            else:
                for d in range(count):
                    o_ref[d] = acc_ref[d * step:d * step + size, :].astype(BF16)

    return pl.pallas_call(
        body,
        grid=(s // ts,),
        in_specs=[pl.BlockSpec((ts, a.shape[1]), lambda k: (k, 0)) for a in as_] + [pl.BlockSpec((ts, nb_), lambda k: (k, 0))],
        out_specs=pl.BlockSpec(out_shape, lambda k: (0,) * len(out_shape)),
        out_shape=SDS(out_shape, BF16),
        scratch_shapes=[pltpu.VMEM((acc_rows, nb_), F32)],
        compiler_params=_cparams("arbitrary"),
        name=name,
    )(*as_, b)


def _norm_bwd(dh, x, r, g, dres):
    xn = x * r
    dxn = dh * g
    dx = dres + r * (dxn - xn * jnp.mean(dxn * xn, axis=-1, keepdims=True))
    return dx, jnp.sum(dh * xn, axis=0, keepdims=True)


def _mlp_in_pool_bwd(dgate, dup, wg_t, wu_t, w_out, x1, r2, g2, dx2, pooled, w_pool, pool_scale, *, tm):
    s = x1.shape[0]
    nt = s // tm
    ng = len(POOL_WINDOWS)

    def body(dg_ref, dup_ref, wg_ref, wu_ref, wo_ref, x_ref, r_ref, g_ref, d_ref, p_ref, w_ref, sc_ref,
             dx1_ref, dattn_ref, du_ref, dg2_ref, dw_ref, dsc_ref, head_ref):
        i = pl.program_id(0)

        @pl.when(i == 0)
        def _():
            dg2_ref[...] = jnp.zeros_like(dg2_ref)
            head_ref[...] = jnp.zeros_like(head_ref)
            dw_ref[...] = jnp.zeros_like(dw_ref)
            dsc_ref[...] = jnp.zeros_like(dsc_ref)

        dh2 = (jnp.dot(dg_ref[...], wg_ref[...], preferred_element_type=F32)
               + jnp.dot(dup_ref[...], wu_ref[...], preferred_element_type=F32))
        dx1, dg2 = _norm_bwd(dh2, x_ref[...], r_ref[...], g_ref[...], d_ref[...])
        dg2_ref[...] += dg2
        dx1_ref[...] = dx1
        dmix = lax.dot_general(dx1.astype(BF16), wo_ref[...], NT, preferred_element_type=F32)
        dattn_ref[...] = dmix[:, 0:ATTN_W]
        row0 = (nt - 1 - i) * tm
        for g, w in enumerate(POOL_WINDOWS):
            cols = slice(g * POOL_G, (g + 1) * POOL_G)
            wb = w_ref[g].astype(BF16)
            pooled_g = p_ref[:, cols]
            dpo = dmix[:, ATTN_W + g * POOL_G:ATTN_W + (g + 1) * POOL_G]
            mixed = jnp.dot(pooled_g, wb, preferred_element_type=F32)
            dsc_ref[:, cols] += jnp.sum(dpo * mixed, axis=0, keepdims=True)
            dmp = (dpo * sc_ref[:, cols]).astype(BF16)
            dw_ref[g] += lax.dot_general(pooled_g, dmp, TN, preferred_element_type=F32)
            dpooled = lax.dot_general(dmp, wb, NT, preferred_element_type=F32)
            a = dpooled / _pool_counts(row0, tm, w)
            acc = jnp.concatenate([a, head_ref[:, cols]], axis=0)
            head_ref[:, cols] = a[0:HALO, :]
            k = 1
            while k < w:
                acc = acc + pltpu.roll(acc, tm + HALO - k, axis=0)
                k *= 2
            du_ref[:, cols] = (acc[0:tm, :] - dpooled).astype(BF16)

    row = lambda w: pl.BlockSpec((tm, w), lambda i: (nt - 1 - i, 0))
    full = lambda a, b: pl.BlockSpec((a, b), lambda i: (0, 0))
    pool_w = pl.BlockSpec((ng, POOL_G, POOL_G), lambda i: (0, 0, 0))
    return pl.pallas_call(
        body,
        grid=(nt,),
        in_specs=[row(D_FF), row(D_FF), full(D_FF, D_MODEL), full(D_FF, D_MODEL), full(D_MODEL, D_MODEL),
                  row(D_MODEL), row(1), full(1, D_MODEL), row(D_MODEL), row(POOL_W), pool_w, full(1, POOL_W)],
        out_specs=[row(D_MODEL), row(ATTN_W), row(POOL_W), full(1, D_MODEL), pool_w, full(1, POOL_W)],
        out_shape=[SDS((s, D_MODEL), F32), SDS((s, ATTN_W), F32), SDS((s, POOL_W), BF16), SDS((1, D_MODEL), F32),
                   SDS((ng, POOL_G, POOL_G), F32), SDS((1, POOL_W), F32)],
        scratch_shapes=[pltpu.VMEM((HALO, POOL_W), F32)],
        compiler_params=_cparams("arbitrary"),
        name="mlp_in_pool_bwd",
    )(dgate, dup, wg_t, wu_t, w_out, x1, r2, g2, dx2, pooled, w_pool, pool_scale)


SUM_ROWS = 16


def _heads_t(t):
    n = t.shape[0]
    lane = _iota2((n, LANES), 1)
    tf = t.astype(F32)
    halves = jnp.concatenate([jnp.where(lane < HEAD_DIM, tf, 0.0).T, jnp.where(lane < HEAD_DIM, 0.0, tf).T], axis=1)
    r, c = _iota2((SUM_ROWS, 2 * n), 0), _iota2((SUM_ROWS, 2 * n), 1)
    ones = jnp.where(((r == 0) & (c < n)) | ((r == 4) & (c >= n)), 1.0, 0.0)
    return jnp.concatenate([halves, ones], axis=0).astype(BF16)


def _attn_bwd(qkv, attn_o, d_attn, rowb, ck_col, after, *, tq):
    s = qkv.shape[0]
    tk = tq
    nb = s // tq
    rows_t = LANES + SUM_ROWS

    def body(q_ref, k_ref, v_ref, o_ref, do_ref, rowb_ref, ck_ref, _, dq_ref, dk_ref, dv_ref, dck_ref, dcq_ref,
             dqt_ref, delta_ref, kp_ref, qp_ref, dob_ref, qt_ref, kt_ref, dot_ref, front_ref):
        lane = _iota2((tq, LANES), 1)
        lo = lane < HEAD_DIM
        first = _iota2((8, LANES), 1) < HEAD_DIM
        sel = jnp.where(_iota2((8, LANES), 0) < 4, jnp.where(first, 1.0, 0.0), jnp.where(first, 0.0, 1.0))

        def prep(b, _):
            st = pl.multiple_of(b * tq, tq)
            do2 = do_ref[pl.ds(st, tq), :]
            delta_ref[b] = _sel_dot(sel, do2 * o_ref[pl.ds(st, tq), :].astype(F32), NT)
            dob_ref[pl.ds(st, tq), :] = do2.astype(BF16)
            dqt_ref[b] = jnp.zeros((rows_t, tq), F32)
            k2 = k_ref[pl.ds(st, tq), :].astype(F32)
            q2 = q_ref[pl.ds(st, tq), :].astype(F32)
            ck = ck_ref[pl.ds(st, tq), :]
            for h in range(2):
                kp_ref[h * nb + b] = _augment(k2, h, -ck[:, h:h + 1], True)
                qp_ref[h * nb + b] = _augment(q2 * Q_SCALE, h, jnp.zeros((tq, 1), F32), False)
            qt_ref[b] = _heads_t(q2)
            kt_ref[b] = _heads_t(k2)
            dot_ref[b] = _heads_t(do2)[0:LANES, :]
            return 0

        lax.fori_loop(0, nb, prep, 0)

        def split(t):
            z = jnp.zeros_like(t)
            return jnp.where(lo, t, z), jnp.where(lo, z, t)

        def kv_block(j, _):
            st_j = pl.multiple_of(j * tk, tk)
            vs = split(v_ref[pl.ds(st_j, tk), :])
            kt = kt_ref[j]

            def stage(i, slot):
                ic = jnp.minimum(i, nb - 1)
                do2 = dob_ref[pl.ds(pl.multiple_of(ic * tq, tq), tq), :]
                for h in range(2):
                    front_ref[4 * slot + h] = lax.dot_general(kp_ref[h * nb + j], qp_ref[h * nb + ic], NT,
                                                              preferred_element_type=F32)
                    front_ref[4 * slot + 2 + h] = lax.dot_general(vs[h], do2, NT, preferred_element_type=F32)

            def q_block(i, slot, carry, diagonal):
                dkt, dvt = carry
                ic = jnp.minimum(i, nb - 1)
                rb = rowb_ref[ic] + jnp.where(i < nb, 0.0, NEG)
                dl = delta_ref[ic]
                pts, dsts = [], []
                for h in range(2):
                    st = front_ref[4 * slot + h] + rb[h:h + 1, :]
                    if diagonal:
                        st = jnp.where(_iota2((tk, tq), 0) <= _iota2((tk, tq), 1), st, NEG)
                    pt = jnp.exp(st)
                    pts.append(pt.astype(BF16))
                    dsts.append((pt * (front_ref[4 * slot + 2 + h] - dl[4 * h:4 * h + 1, :])).astype(BF16))
                dvt = dvt + lax.dot_general(dot_ref[ic], jnp.concatenate(pts, axis=1), NT, preferred_element_type=F32)
                dkt = dkt + lax.dot_general(qt_ref[ic], jnp.concatenate(dsts, axis=1), NT, preferred_element_type=F32)
                dqt_ref[ic] += jnp.dot(kt, jnp.concatenate(dsts, axis=0), preferred_element_type=F32)
                return dkt, dvt

            def run(i0, steps, carry):
                for d in range(steps):
                    stage(i0 + d + 1, d % 2)
                    carry = q_block(i0 + d, 1 - d % 2, carry, False)
                return carry

            stage(j, 0)
            stage(j + 1, 1)
            carry = q_block(j, 0, (jnp.zeros((rows_t, tk), F32), jnp.zeros((LANES, tk), F32)), True)
            first, left = j + 1, nb - 1 - j
            for size in UNROLLS:
                trips = _shift_div(left + 1 if size == UNROLLS[-1] else left, size)
                carry = lax.fori_loop(0, trips, lambda t, c, i0=first, n=size: run(i0 + n * t, n, c), carry)
                first, left = first + size * trips, left - size * trips
            dkt, dvt = carry
            dk_ref[pl.ds(st_j, tk), :] = (dkt[0:LANES, :].T * Q_SCALE).astype(BF16)
            dv_ref[pl.ds(st_j, tk), :] = dvt.T.astype(BF16)
            dck_ref[j] = dkt[LANES:LANES + 8, :]
            return 0

        lax.fori_loop(0, nb, kv_block, 0)

        def finish(b, _):
            acc = dqt_ref[b]
            dq_ref[pl.ds(pl.multiple_of(b * tq, tq), tq), :] = (acc[0:LANES, :].T * Q_SCALE).astype(BF16)
            dcq_ref[b] = acc[LANES:LANES + 8, :]
            return 0

        lax.fori_loop(0, nb, finish, 0)

    col = lambda off: pl.BlockSpec((s, LANES), lambda p: (0, off + p))
    sums = pl.BlockSpec((None, nb, 8, tq), lambda p: (p, 0, 0, 0))
    return pl.pallas_call(
        body,
        grid=(N_PAIRS,),
        in_specs=[col(0), col(N_PAIRS), col(2 * N_PAIRS), col(0), col(0),
                  pl.BlockSpec((None, nb, 2, tq), lambda p: (p, 0, 0, 0)),
                  pl.BlockSpec((None, s, 2), lambda p: (p, 0, 0)), _UNREAD],
        out_specs=[col(0), col(0), col(0), sums, sums],
        out_shape=[SDS((s, ATTN_W), BF16), SDS((s, ATTN_W), BF16), SDS((s, ATTN_W), BF16),
                   SDS((N_PAIRS, nb, 8, tq), F32), SDS((N_PAIRS, nb, 8, tq), F32)],
        scratch_shapes=[pltpu.VMEM((nb, rows_t, tq), F32), pltpu.VMEM((nb, 8, tq), F32),
                        pltpu.VMEM((2 * nb, tk, LANES), BF16), pltpu.VMEM((2 * nb, tq, LANES), BF16),
                        pltpu.VMEM((s, LANES), BF16), pltpu.VMEM((nb, rows_t, 2 * tq), BF16),
                        pltpu.VMEM((nb, rows_t, 2 * tk), BF16), pltpu.VMEM((nb, LANES, 2 * tq), BF16),
                        pltpu.VMEM((8, tk, tq), F32)],
        compiler_params=_cparams("arbitrary"),
        name="attn_bwd",
    )(qkv, qkv, qkv, attn_o, d_attn, rowb, ck_col, after)


def _forget_bwd(dc_t, fl_t, b_rows):
    rows = fl_t.shape[0]
    nb = rows // N_HEADS

    def body(dc_ref, fl_ref, b_ref, dfl_ref, db_ref):
        dc = dc_ref[...]
        lower = _iota2((LANES, LANES), 0) >= _iota2((LANES, LANES), 1)
        ones = jnp.ones((LANES, LANES), F32)
        rr, cc, same = _head_block_masks(rows, nb)
        dlf = _dot_sel(dc, lower) + _sel_dot(same & (cc > rr), _dot_sel(dc, ones))
        dfl = dlf / (1.0 + jnp.exp(fl_ref[...] + b_ref[...]))
        dfl_ref[...] = dfl
        shift = nb.bit_length() - 1
        hsel = lax.shift_right_logical(_iota2((N_HEADS, rows), 1), shift) == _iota2((N_HEADS, rows), 0)
        db_ref[...] = _sel_dot(hsel, _dot_sel(dfl, ones))

    return pl.pallas_call(body, out_shape=[SDS(fl_t.shape, F32), SDS((N_HEADS, LANES), F32)],
                          compiler_params=_cparams(), name="forget_bwd")(dc_t, fl_t, b_rows)


def _in_bwd(dq, dk, dv, du, dfl, w_in_t, x, r1, g1, dx1, after, *, tm):
    s = x.shape[0]
    pieces = ((0, ATTN_W), (ATTN_W, 2 * ATTN_W), (2 * ATTN_W, QKV_W), (U_OFF, F_OFF), (F_OFF, IN_PAD))

    def body(dq_ref, dk_ref, dv_ref, du_ref, df_ref, w_ref, x_ref, r_ref, g_ref, d_ref, _, dx_ref, dg1_ref):
        @pl.when(pl.program_id(0) == 0)
        def _():
            dg1_ref[...] = jnp.zeros_like(dg1_ref)

        dh = None
        for ref, (c0, c1) in zip((dq_ref, dk_ref, dv_ref, du_ref, df_ref), pieces):
            t = jnp.dot(ref[...], w_ref[c0:c1, :], preferred_element_type=F32)
            dh = t if dh is None else dh + t
        dx, dg1 = _norm_bwd(dh, x_ref[...], r_ref[...], g_ref[...], d_ref[...])
        dx_ref[...] = dx
        dg1_ref[...] += dg1

    row = lambda w: pl.BlockSpec((tm, w), lambda i: (i, 0))
    full = lambda a, b: pl.BlockSpec((a, b), lambda i: (0, 0))
    return pl.pallas_call(
        body,
        grid=(s // tm,),
        in_specs=[row(ATTN_W), row(ATTN_W), row(ATTN_W), row(POOL_W), row(LANES), full(IN_PAD, D_MODEL),
                  row(D_MODEL), row(1), full(1, D_MODEL), row(D_MODEL), _UNREAD],
        out_specs=[row(D_MODEL), full(1, D_MODEL)],
        out_shape=[SDS((s, D_MODEL), F32), SDS((1, D_MODEL), F32)],
        compiler_params=_cparams("arbitrary"),
        name="in_bwd",
    )(dq, dk, dv, du, dfl, w_in_t, x, r1, g1, dx1, after)


def _tiles(s):
    big = min(512, s)
    return dict(row=big, attn=min(256, s // 2), ff_rows=min(256, s), tall=min(1024, s))


def _local_step(x, tgt, p, weight, emit, started):
    s = x.shape[0]
    t = _tiles(s)
    tm, tq = t["row"], t["attn"]
    nb = s // LANES
    nqb = s // tq
    g1, g2, gf = p["norm1_g"], p["norm2_g"], p["final_g"].reshape(1, D_MODEL)
    w_pool, pool_scale = p["w_pool"][0], p["pool_scale"]

    h, r1 = _norm1(x, g1, started, tm=tm)
    w_in_t = weight("w_in", h)
    qkv, fl, pooled, pool_o = _in_proj_pool(h, w_in_t, w_pool, pool_scale, tm=t["tall"])
    fl_t = fl[:, :N_HEADS].T.reshape(N_HEADS * nb, LANES)
    b_rows = jnp.repeat(p["b_forget"].reshape(N_HEADS), nb).reshape(N_HEADS * nb, 1)
    c = _forget_cumsum(fl_t, b_rows).reshape(N_PAIRS, 2, s)
    c_col = c.transpose(0, 2, 1)
    c_rowblk = c.reshape(N_PAIRS, 2, nqb, tq).transpose(0, 2, 1, 3)
    attn_o, lse = _attn_fwd(qkv, c_col, tk=tq)
    lse = lse.reshape(N_PAIRS, nqb // 2, 2, 2, tq).transpose(0, 1, 3, 2, 4).reshape(N_PAIRS, nqb, 2, tq)
    w_out = weight("w_out", attn_o)
    wg_t, wu_t = weight("w_gate_up", attn_o)
    x1, h2, r2, gate, up, act = _out_gate_up(attn_o, pool_o, w_out, x, g2, wg_t, wu_t, tm=t["ff_rows"])
    wd = weight("w_down", act)
    dx2, loss_row, d_gf = _down_final(act, wd, x1, gf, tgt, tm=tm, sub=min(128, tm))

    dgate, dup = _swiglu_bwd(dx2, wd, gate, up, tm=t["ff_rows"], tn=D_FF)
    d_wd = _mm_tn_stacked([act], [D_FF], dx2, ts=t["tall"], name="grad_w_down")
    d_wg_t = _mm_tn_stacked([dgate], [D_FF], h2, ts=t["tall"], name="grad_w_gate")
    d_wu_t = _mm_tn_stacked([dup], [D_FF], h2, ts=t["tall"], name="grad_w_up")
    dx1, d_attn, du, d_g2, d_wpool, d_pscale = _mlp_in_pool_bwd(dgate, dup, wg_t, wu_t, w_out, x1, r2, g2, dx2, pooled,
                                                               w_pool, pool_scale, tm=t["ff_rows"])
    d_wo = _mm_tn_stacked([attn_o, pool_o], [ATTN_W, POOL_W], dx1, ts=t["tall"], name="grad_w_out")
    token = emit(("w_down", "w_gate", "w_up", "w_out"), (d_wd, d_wg_t, d_wu_t, d_wo))
    dq, dk, dv, dck, dcq = _attn_bwd(qkv, attn_o, d_attn, c_rowblk - lse, c_col, token, tq=tq)
    dc_t = (dcq - dck)[:, :, 0::4, :].transpose(0, 2, 1, 3).reshape(N_HEADS * nb, LANES)
    dfl_t, db = _forget_bwd(dc_t, fl_t, b_rows)
    dfl = jnp.pad(dfl_t.reshape(N_HEADS, s).T, ((0, 0), (0, LANES - N_HEADS))).astype(BF16)
    d_w_in_t = _mm_tn_stacked([dq, dk, dv, dfl, du], [ATTN_W, ATTN_W, ATTN_W, N_HEADS, POOL_W], h, ts=t["tall"],
                              name="grad_w_in",
                              windows=(N_DEV, IN_STEP, IN_WINDOW))
    token = emit(("w_in",), (d_w_in_t,))
    dx, d_g1 = _in_bwd(dq, dk, dv, du, dfl, w_in_t, x, r1, g1, dx1, token, tm=tm)

    small = dict(norm1_g=d_g1, b_forget=db[:, 0].reshape(1, N_HEADS), w_pool=d_wpool, pool_scale=d_pscale,
                 norm2_g=d_g2, final_g=d_gf)
    return loss_row, dx, small


def _my_index():
    return 4 * lax.axis_index("x") + 2 * lax.axis_index("y") + lax.axis_index("c")


def _peer(k):
    pos = [lax.axis_index(a) for a in ("x", "y", "c")]
    flipped = tuple(1 - p if (k >> b) & 1 else p for p, b in zip(pos, (2, 1, 0)))
    return flipped, 4 * flipped[0] + 2 * flipped[1] + flipped[2]


_HBM = pl.BlockSpec(memory_space=pltpu.HBM)
_SEM = pl.BlockSpec(memory_space=pltpu.SEMAPHORE)
_DATAFLOW = pltpu.SideEffectType.DATAFLOW_SIDE_EFFECTING


ALL_PEERS = tuple(range(1, N_DEV))
SAME_CORE = (1, 2, 4, 6)


def _peer_copies(ins, lands, send_sems, recv_sems, scatter, peers, arrivals):
    me = _my_index()
    copies = []
    for w in range(len(ins)):
        for k in peers[w]:
            dev, idx = _peer(k)
            copies.append(pltpu.make_async_remote_copy(
                src_ref=ins[w].at[idx] if scatter[w] else ins[w], dst_ref=lands[w].at[idx if arrivals else me],
                send_sem=send_sems[w].at[k - 1], recv_sem=recv_sems[w].at[k - 1], device_id=dev, device_id_type=MESH))
    return copies


def _own_copies(ins, lands, send_sems, scatter):
    me = _my_index()
    return [pltpu.make_async_copy(ins[w].at[me] if scatter[w] else ins[w], lands[w].at[me], send_sems[w].at[N_DEV - 1])
            for w in range(len(ins))]


def _forward_copies(land, send_sems, recv_sems, arrivals):
    sibling, _ = _peer(1)
    copies = []
    for j, k in enumerate(SAME_CORE[1:]):
        src, dst = _peer(k)[1], _peer(k ^ 1 if arrivals else k)[1]
        copies.append(pltpu.make_async_remote_copy(
            src_ref=land.at[src], dst_ref=land.at[dst], send_sem=send_sems.at[j], recv_sem=recv_sems.at[j],
            device_id=sibling, device_id_type=MESH))
    return copies


def _forward_start(land, name):
    def body(land_ref, send_sems, recv_sems, land_thru, token):
        for cp in _forward_copies(land_ref, send_sems, recv_sems, False):
            cp.start()
        token[...] = jnp.zeros_like(token)

    sem = pltpu.SemaphoreType.DMA((len(SAME_CORE) - 1,))
    send, recv, thru, _ = pl.pallas_call(
        body,
        in_specs=[_HBM],
        out_specs=[_SEM, _SEM, _HBM, pl.BlockSpec(memory_space=pltpu.VMEM)],
        out_shape=[sem, sem, pltpu.HBM(land.shape, land.dtype), SDS((8, LANES), F32)],
        input_output_aliases={0: 2},
        compiler_params=pltpu.CompilerParams(has_side_effects=_DATAFLOW),
        name=name,
    )(land)
    return send, recv, thru


def _forward_wait(handle, after, name):
    def body(land_ref, send_sems, recv_sems, after_ref, land_out):
        for cp in _forward_copies(land_ref, send_sems, recv_sems, False):
            cp.wait_send()
        for cp in _forward_copies(land_ref, send_sems, recv_sems, True):
            cp.wait_recv()

    send, recv, land = handle
    return pl.pallas_call(
        body,
        in_specs=[_HBM, _SEM, _SEM, pl.BlockSpec(memory_space=pl.ANY)],
        out_specs=_HBM,
        out_shape=pltpu.HBM(land.shape, land.dtype),
        input_output_aliases={0: 0},
        compiler_params=pltpu.CompilerParams(has_side_effects=_DATAFLOW),
        name=name,
    )(land, send, recv, after)


def _exchange_start(arrays, scatter, name, peers=None):
    n = len(arrays)
    peers = peers or [ALL_PEERS] * n
    land_shapes = [(N_DEV,) + tuple(a.shape[1:] if sc else a.shape) for a, sc in zip(arrays, scatter)]

    def body(*refs):
        ins, lands = refs[:n], refs[n:2 * n]
        send_sems, recv_sems = refs[2 * n:3 * n], refs[3 * n:4 * n]
        token = refs[6 * n]
        for cp in _peer_copies(ins, lands, send_sems, recv_sems, scatter, peers, False):
            cp.start()
        for cp in _own_copies(ins, lands, send_sems, scatter):
            cp.start()
        token[...] = jnp.zeros_like(token)

    sends, recvs = pltpu.SemaphoreType.DMA((N_DEV,)), pltpu.SemaphoreType.DMA((N_DEV - 1,))
    outs = pl.pallas_call(
        body,
        in_specs=[_HBM] * (2 * n),
        out_specs=[_SEM] * (2 * n) + [_HBM] * (2 * n) + [pl.BlockSpec(memory_space=pltpu.VMEM)],
        out_shape=[sends] * n + [recvs] * n + [pltpu.HBM(a.shape, a.dtype) for a in arrays]
        + [pltpu.HBM(sh, a.dtype) for sh, a in zip(land_shapes, arrays)] + [SDS((8, LANES), F32)],
        input_output_aliases={i: 2 * n + i for i in range(2 * n)},
        compiler_params=pltpu.CompilerParams(has_side_effects=_DATAFLOW),
        name=name,
    )(*[pltpu.with_memory_space_constraint(a, pltpu.HBM) for a in arrays],
      *[pltpu.with_memory_space_constraint(lax.empty(sh, a.dtype), pltpu.HBM) for sh, a in zip(land_shapes, arrays)])
    handles = [dict(send=outs[w], recv=outs[n + w], src=outs[2 * n + w], land=outs[3 * n + w], scatter=scatter[w],
                    peers=peers[w]) for w in range(n)]
    return handles, outs[4 * n]


def _exchange_wait(handles, after, name):
    n = len(handles)
    scatter, peers = [h["scatter"] for h in handles], [h["peers"] for h in handles]

    def body(*refs):
        ins, lands = refs[:n], refs[n:2 * n]
        send_sems, recv_sems = refs[2 * n:3 * n], refs[3 * n:4 * n]
        for cp in _peer_copies(ins, lands, send_sems, recv_sems, scatter, peers, False):
            cp.wait_send()
        for cp in _peer_copies(ins, lands, send_sems, recv_sems, scatter, peers, True):
            cp.wait_recv()
        for cp in _own_copies(ins, lands, send_sems, scatter):
            cp.wait()

    srcs, lands = [h["src"] for h in handles], [h["land"] for h in handles]
    outs = pl.pallas_call(
        body,
        in_specs=[_HBM] * (2 * n) + [_SEM] * (2 * n) + [pl.BlockSpec(memory_space=pl.ANY)],
        out_specs=[_HBM] * (2 * n),
        out_shape=[pltpu.HBM(a.shape, a.dtype) for a in srcs + lands],
        input_output_aliases={i: i for i in range(2 * n)},
        compiler_params=pltpu.CompilerParams(has_side_effects=_DATAFLOW),
        name=name,
    )(*srcs, *lands, *[h["send"] for h in handles], *[h["recv"] for h in handles], after)
    return outs[n:]


def _adamw(parts, w, m, v, name):
    rows, cols = w.shape
    tr = rows // 4 if rows % 32 == 0 else rows

    def body(p_ref, w_ref, m_ref, v_ref, g_ref, d_ref, mo_ref, vo_ref):
        g = p_ref[0].astype(F32)
        for d in range(1, N_DEV):
            g = g + p_ref[d].astype(F32)
        g_ref[...] = g
        d_ref[...], mo_ref[...], vo_ref[...] = _adam_update(g, w_ref[...], m_ref[...], v_ref[...])

    blk = pl.BlockSpec((tr, cols), lambda i: (i, 0))
    return pl.pallas_call(
        body,
        grid=(rows // tr,),
        in_specs=[pl.BlockSpec((N_DEV, tr, cols), lambda i: (0, i, 0)), blk, blk, blk],
        out_specs=[blk] * 4,
        out_shape=[SDS((rows, cols), F32)] * 4,
        compiler_params=_cparams("arbitrary"),
        name=name,
    )(parts, w, m, v)


def _adamw_dense(parts, w, m, v, name, *, rows, shift):
    _, window, cols = parts.shape
    per_row = cols // LANES

    def body(p_ref, w_ref, m_ref, v_ref, g_ref, d_ref, mo_ref, vo_ref, sum_ref):
        g = p_ref[0].astype(F32)
        for d in range(1, N_DEV):
            g = g + p_ref[d].astype(F32)
        sum_ref[...] = g
        me = _my_index()
        for j in range(N_DEV):
            @pl.when(me == j)
            def _(j=j):
                for c in range(per_row):
                    at = (pl.ds(c, rows, stride=per_row), slice(None))
                    gc = sum_ref[j * shift:j * shift + rows, c * LANES:(c + 1) * LANES]
                    g_ref[at] = gc
                    d_ref[at], mo_ref[at], vo_ref[at] = _adam_update(gc, w_ref[at], m_ref[at], v_ref[at])

    return pl.pallas_call(
        body,
        out_shape=[SDS(w.shape, F32)] * 4,
        scratch_shapes=[pltpu.VMEM((window, cols), F32)],
        compiler_params=_cparams(),
        name=name,
    )(parts, w, m, v)


_ROW_OF = dict(norm1_g=(0, D_MODEL), norm2_g=(1, D_MODEL), final_g=(2, D_MODEL), pool_scale=(3, POOL_W),
               b_forget=(4, N_HEADS), loss=(5, 1))


def _pack_rows(vals):
    rows = [jnp.pad(vals[n].reshape(1, width).astype(F32), ((0, 0), (0, D_MODEL - width)))
            for n, (_, width) in sorted(_ROW_OF.items(), key=lambda kv: kv[1][0])]
    return jnp.concatenate(rows + [jnp.zeros((8 - len(rows), D_MODEL), F32)], axis=0)


def _adam_update(g, w, m, v):
    m_new = ADAM_B1 * m + (1.0 - ADAM_B1) * g
    v_new = ADAM_B2 * v + (1.0 - ADAM_B2) * (g * g)
    m_hat = m_new / (1.0 - ADAM_B1 ** ADAM_STEP)
    v_hat = v_new / (1.0 - ADAM_B2 ** ADAM_STEP)
    return -ADAM_LR * (m_hat / (jnp.sqrt(v_hat) + ADAM_EPS) + ADAM_WD * w), m_new, v_new


def _adamw_replicated(parts_rows, parts_pool, w, m, v):
    names = ("norm1_g", "norm2_g", "final_g", "pool_scale", "b_forget", "w_pool")
    shapes = {n: ((len(POOL_WINDOWS), POOL_G, POOL_G) if n == "w_pool" else (1, _ROW_OF[n][1])) for n in names}

    def body(rows_ref, pool_ref, *refs):
        ins, outs = refs[:3 * len(names)], refs[3 * len(names):]

        def total(n):
            if n == "w_pool":
                pieces = [pool_ref[d] for d in range(N_DEV)]
            else:
                row, width = _ROW_OF[n]
                pieces = [rows_ref[d, row:row + 1, 0:width] for d in range(N_DEV)]
            g = pieces[0]
            for p in pieces[1:]:
                g = g + p
            return g

        outs[0][...] = total("loss")
        for k, n in enumerate(names):
            g = total(n)
            delta, m_new, v_new = _adam_update(g, ins[3 * k][...], ins[3 * k + 1][...], ins[3 * k + 2][...])
            for o_ref, val in zip(outs[1 + 4 * k:5 + 4 * k], (g, delta, m_new, v_new)):
                o_ref[...] = val

    args = [d[n].reshape(shapes[n]) for n in names for d in (w, m, v)]
    res = pl.pallas_call(
        body,
        out_shape=[SDS((1, 1), F32)] + [SDS(shapes[n], F32) for n in names for _ in range(4)],
        compiler_params=_cparams(),
        name="adamw_replicated",
    )(parts_rows, parts_pool, *args)
    return res[0], {n: [r.reshape(w[n].shape) for r in res[1 + 4 * k:5 + 4 * k]] for k, n in enumerate(names)}


def kernel(x, norm1_g, w_in, b_forget, w_pool, pool_scale, w_out, norm2_g, w_gate, w_up, w_down, final_g, loss_target, m_norm1_g, m_w_in, m_b_forget, m_w_pool, m_pool_scale, m_w_out, m_norm2_g, m_w_gate, m_w_up, m_w_down, m_final_g, v_norm1_g, v_w_in, v_b_forget, v_w_pool, v_pool_scale, v_w_out, v_norm2_g, v_w_gate, v_w_up, v_w_down, v_final_g):
    big = ("w_in", "w_out", "w_gate", "w_up", "w_down")
    order = ("norm1_g", "w_in", "b_forget", "w_pool", "pool_scale", "w_out", "norm2_g", "w_gate", "w_up", "w_down",
             "final_g")
    w = dict(norm1_g=norm1_g, w_in=w_in, b_forget=b_forget, w_pool=w_pool, pool_scale=pool_scale, w_out=w_out,
             norm2_g=norm2_g, w_gate=w_gate, w_up=w_up, w_down=w_down, final_g=final_g)
    m = dict(norm1_g=m_norm1_g, w_in=m_w_in, b_forget=m_b_forget, w_pool=m_w_pool, pool_scale=m_pool_scale,
             w_out=m_w_out, norm2_g=m_norm2_g, w_gate=m_w_gate, w_up=m_w_up, w_down=m_w_down, final_g=m_final_g)
    v = dict(norm1_g=v_norm1_g, w_in=v_w_in, b_forget=v_b_forget, w_pool=v_w_pool, pool_scale=v_pool_scale,
             w_out=v_w_out, norm2_g=v_norm2_g, w_gate=v_w_gate, w_up=v_w_up, w_down=v_w_down, final_g=v_final_g)

    flipped = ("w_in", "w_gate", "w_up")
    shard = lambda d, n: d[n][0].T if n in flipped else d[n][0]
    gather, started = _exchange_start([shard(w, n).astype(BF16) for n in big], [False] * len(big), "gather_start",
                                      peers=[SAME_CORE if n == "w_in" else ALL_PEERS for n in big])
    gather = dict(zip(big, gather))

    def gathered(names, after):
        return _exchange_wait([gather[n] for n in names], after, "gather_wait_" + names[0])

    def weight(name, after):
        if name == "w_in":
            forward = _forward_start(gathered(["w_in"], after)[0], "gather_forward_start")
            full = _forward_wait(forward, after, "gather_forward_wait").reshape(IN_W, D_MODEL)
            f0 = QKV_W + N_HEADS
            return jnp.concatenate([full[:QKV_W], full[f0:], full[QKV_W:f0],
                                    jnp.zeros((IN_PAD - IN_W, D_MODEL), BF16)], axis=0)
        if name == "w_out":
            return gathered(["w_out"], after)[0].reshape(D_MODEL, D_MODEL)
        if name == "w_gate_up":
            return [g.reshape(D_FF, D_MODEL) for g in gathered(["w_gate", "w_up"], after)]
        return gathered(["w_down"], after)[0].reshape(D_FF, D_MODEL)

    rows = lambda g: g if g.ndim == 3 else g.reshape(N_DEV, g.shape[0] // N_DEV, g.shape[1])
    sent = {}

    def emit(names, grads):
        handles, token = _exchange_start([rows(g) for g in grads], [True] * len(names), "grads_start_" + names[0])
        sent.update(zip(names, handles))
        return token

    loss_row, dx, small_grads = _local_step(x[0], loss_target[0], w, weight, emit, started)

    packed = _pack_rows(dict(small_grads, loss=0.5 / D_MODEL * jnp.sum(loss_row)))
    small_handles, after = _exchange_start([packed, small_grads["w_pool"]], [False, False], "grads_start_replicated")

    outs = {}
    for name in ("w_down", "w_gate", "w_up", "w_out", "w_in"):
        (parts,) = _exchange_wait([sent[name]], after, "grads_wait_" + name)
        if name == "w_in":
            dense = lambda d: d[name].transpose(2, 0, 1).reshape(-1, LANES)
            outs[name] = _adamw_dense(parts, dense(w), dense(m), dense(v), "adamw_" + name, rows=IN_SHARD, shift=IN_SHIFT)
            after = outs[name][0]
            outs[name] = [a.reshape(-1, D_MODEL // LANES, LANES).transpose(1, 2, 0).reshape(1, D_MODEL, -1)
                          for a in outs[name]]
            continue
        outs[name] = _adamw(parts, shard(w, name), shard(m, name), shard(v, name), "adamw_" + name)
        after = outs[name][0]
        outs[name] = [(a.T if name in flipped else a)[None] for a in outs[name]]
    parts_rows, parts_pool = _exchange_wait(small_handles, after, "grads_wait_replicated")
    loss, small = _adamw_replicated(parts_rows, parts_pool, w, m, v)
    outs.update(small)

    return (loss.reshape(()), dx[None]) + tuple(outs[n][k] for k in range(4) for n in order)
```

```python
import jax
import jax.numpy as jnp
from jax import lax
from jax.experimental import pallas as pl
from jax.experimental.pallas import tpu as pltpu

F32 = jnp.float32
BF16 = jnp.bfloat16
SDS = jax.ShapeDtypeStruct

D_MODEL = 1024
ATTN_W = 512
N_HEADS = 8
HEAD_DIM = 64
Q_SCALE = HEAD_DIM ** -0.5
N_PAIRS = N_HEADS // 2
POOL_W = 512
POOL_WINDOWS = (2, 4, 8, 16)
POOL_G = 128
HALO = 16
IN_W = 3 * ATTN_W + N_HEADS + POOL_W
QKV_W = 3 * ATTN_W
U_OFF = QKV_W
F_OFF = QKV_W + POOL_W
IN_PAD = F_OFF + 128
D_FF = 2816
EPS = 1e-6
NEG = -1e30
N_DEV = 8
LANES = 128
BF16_ROWS = 16

IN_SHARD = IN_W // N_DEV
IN_STEP = IN_SHARD // BF16_ROWS * BF16_ROWS
IN_SHIFT = IN_SHARD - IN_STEP
IN_WINDOW = -(-((N_DEV - 1) * IN_SHIFT + IN_SHARD) // BF16_ROWS) * BF16_ROWS

ADAM_LR = 0.001
ADAM_B1 = 0.9
ADAM_B2 = 0.999
ADAM_EPS = 1e-08
ADAM_WD = 0.01
ADAM_STEP = 10

VMEM_LIMIT_BYTES = 56 * 1024 * 1024
MESH = pl.DeviceIdType.MESH
NT = (((1,), (1,)), ((), ()))
TN = (((0,), (0,)), ((), ()))


_UNREAD = pl.BlockSpec(memory_space=pl.ANY)


def _cparams(*sem):
    return pltpu.CompilerParams(dimension_semantics=sem or None, vmem_limit_bytes=VMEM_LIMIT_BYTES)


def _split3(a):
    hi = a.astype(BF16)
    r1 = a - hi.astype(F32)
    mid = r1.astype(BF16)
    lo = (r1 - mid.astype(F32)).astype(BF16)
    return hi, mid, lo


def _dot_sel(a, sel, dims=None):
    sb = sel.astype(BF16)
    if dims is None:
        return sum(jnp.dot(p, sb, preferred_element_type=F32) for p in _split3(a))
    return sum(lax.dot_general(p, sb, dims, preferred_element_type=F32) for p in _split3(a))


def _sel_dot(sel, a, dims=None):
    sb = sel.astype(BF16)
    if dims is None:
        return sum(jnp.dot(sb, p, preferred_element_type=F32) for p in _split3(a))
    return sum(lax.dot_general(sb, p, dims, preferred_element_type=F32) for p in _split3(a))


def _iota2(shape, dim):
    return lax.broadcasted_iota(jnp.int32, shape, dim)


UNROLLS = (8, 4, 2)


def _shift_div(x, n):
    return lax.shift_right_logical(x, n.bit_length() - 1)


def _norm1(x, g1, after, *, tm):
    s = x.shape[0]

    def body(x_ref, g_ref, _, h_ref, r_ref):
        xv = x_ref[...]
        r = lax.rsqrt(jnp.mean(xv * xv, axis=-1, keepdims=True) + EPS)
        h_ref[...] = (xv * r * g_ref[...]).astype(BF16)
        r_ref[...] = r

    row = lambda w: pl.BlockSpec((tm, w), lambda i: (i, 0))
    return pl.pallas_call(
        body,
        grid=(s // tm,),
        in_specs=[row(D_MODEL), pl.BlockSpec((1, D_MODEL), lambda i: (0, 0)), _UNREAD],
        out_specs=[row(D_MODEL), row(1)],
        out_shape=[SDS((s, D_MODEL), BF16), SDS((s, 1), F32)],
        compiler_params=_cparams("arbitrary"),
        name="norm1",
    )(x, g1, after)


def _in_proj_pool(h, w_in_t, w_pool, pool_scale, *, tm):
    s = h.shape[0]

    def body(h_ref, w_ref, wp_ref, sc_ref, qkv_ref, fl_ref, pooled_ref, po_ref, tail_ref):
        i = pl.program_id(0)

        @pl.when(i == 0)
        def _():
            tail_ref[...] = jnp.zeros_like(tail_ref)

        hv = h_ref[...]
        uv = lax.dot_general(hv, w_ref[U_OFF:F_OFF, :], NT, preferred_element_type=F32)
        qkv_ref[...] = lax.dot_general(hv, w_ref[0:QKV_W, :], NT, preferred_element_type=F32).astype(BF16)
        fl_ref[...] = lax.dot_general(hv, w_ref[F_OFF:IN_PAD, :], NT, preferred_element_type=F32)
        ext = jnp.concatenate([tail_ref[...], uv], axis=0)
        tail_ref[...] = uv[tm - HALO:, :]
        for g, w in enumerate(POOL_WINDOWS):
            cols = slice(g * POOL_G, (g + 1) * POOL_G)
            acc = ext[:, cols]
            k = 1
            while k < w:
                acc = acc + pltpu.roll(acc, k, axis=0)
                k *= 2
            pooled = (acc[HALO:, :] / _pool_counts(i * tm, tm, w) - uv[:, cols]).astype(BF16)
            pooled_ref[:, cols] = pooled
            mixed = jnp.dot(pooled, wp_ref[g].astype(BF16), preferred_element_type=F32)
            po_ref[:, cols] = (mixed * sc_ref[:, cols]).astype(BF16)

    row = lambda w: pl.BlockSpec((tm, w), lambda i: (i, 0))
    return pl.pallas_call(
        body,
        grid=(s // tm,),
        in_specs=[row(D_MODEL), pl.BlockSpec((IN_PAD, D_MODEL), lambda i: (0, 0)),
                  pl.BlockSpec((len(POOL_WINDOWS), POOL_G, POOL_G), lambda i: (0, 0, 0)),
                  pl.BlockSpec((1, POOL_W), lambda i: (0, 0))],
        out_specs=[row(QKV_W), row(LANES), row(POOL_W), row(POOL_W)],
        out_shape=[SDS((s, QKV_W), BF16), SDS((s, LANES), F32), SDS((s, POOL_W), BF16), SDS((s, POOL_W), BF16)],
        scratch_shapes=[pltpu.VMEM((HALO, POOL_W), F32)],
        compiler_params=_cparams("arbitrary"),
        name="in_proj_pool",
    )(h, w_in_t, w_pool, pool_scale)


def _head_block_masks(rows, nb):
    shift = nb.bit_length() - 1
    rr, cc = _iota2((rows, rows), 0), _iota2((rows, rows), 1)
    same = lax.shift_right_logical(rr, shift) == lax.shift_right_logical(cc, shift)
    return rr, cc, same


def _forget_cumsum(fl_t, b_rows):
    rows = fl_t.shape[0]
    nb = rows // N_HEADS

    def body(fl_ref, b_ref, c_ref):
        z = fl_ref[...] + b_ref[...]
        lf = jnp.minimum(z, 0.0) - jnp.log1p(jnp.exp(-jnp.abs(z)))
        upper = _iota2((LANES, LANES), 0) <= _iota2((LANES, LANES), 1)
        within = _dot_sel(lf, upper)
        tot = _dot_sel(lf, jnp.ones((LANES, LANES), F32))
        rr, cc, same = _head_block_masks(rows, nb)
        c_ref[...] = within + _sel_dot(same & (cc < rr), tot)

    return pl.pallas_call(body, out_shape=SDS(fl_t.shape, F32), compiler_params=_cparams(), name="forget_cumsum")(
        fl_t, b_rows)


BIAS_LANES = 3


def _augment(t, h, col, col_first):
    n = t.shape[0]
    lane = _iota2((n, LANES), 1)
    own = (lane < HEAD_DIM) if h == 0 else (lane >= HEAD_DIM)
    b0 = HEAD_DIM if h == 0 else 0
    c0, o0 = (b0, b0 + BIAS_LANES) if col_first else (b0 + BIAS_LANES, b0)
    x = jnp.where(own, t, 0.0)
    for off, piece in enumerate(_split3(col)):
        x = jnp.where(lane == c0 + off, piece.astype(F32), x)
    x = jnp.where((lane >= o0) & (lane < o0 + BIAS_LANES), 1.0, x)
    return x.astype(BF16)


def _attn_fwd(qkv, c_col, *, tk):
    s = qkv.shape[0]
    tq = 2 * tk
    nb = s // tk

    def body(q_ref, k_ref, v_ref, cq_ref, ck_ref, o_ref, lse_ref, kp_ref, vt_ref, st_ref):
        i = pl.program_id(1)

        @pl.when(i == 0)
        def _():
            def prep(jb, _):
                st = pl.multiple_of(jb * tk, tk)
                k2 = k_ref[pl.ds(st, tk), :].astype(F32)
                ck = ck_ref[pl.ds(st, tk), :]
                for h in range(2):
                    kp_ref[h * nb + jb] = _augment(k2, h, -ck[:, h:h + 1], True)
                vt_ref[jb] = v_ref[pl.ds(st, tk), :].astype(F32).T.astype(BF16)
                return 0

            lax.fori_loop(0, nb, prep, 0)

        qs = q_ref[...].astype(F32) * Q_SCALE
        cq = cq_ref[...]
        qp = [_augment(qs, h, cq[:, h:h + 1], False) for h in range(2)]

        def logits(j):
            return tuple(lax.dot_general(kp_ref[h * nb + j], qp[h], NT, preferred_element_type=F32) for h in range(2))

        def softmax_pv(j, slot, stats, masked):
            out = []
            for h in range(2):
                m, l, acc = stats[h]
                st = st_ref[2 * slot + h]
                if masked:
                    st = jnp.where(j * tk + _iota2((tk, tq), 0) <= i * tq + _iota2((tk, tq), 1), st, NEG)
                m_new = jnp.maximum(m, jnp.max(st, axis=0, keepdims=True))
                alpha = jnp.exp(m - m_new)
                p = jnp.exp(st - m_new)
                l = alpha * l + jnp.sum(p, axis=0, keepdims=True)
                vt = vt_ref[j, h * HEAD_DIM:(h + 1) * HEAD_DIM, :]
                acc = alpha * acc + jnp.dot(vt, p.astype(BF16), preferred_element_type=F32)
                out.append((m_new, l, acc))
            return tuple(out)

        def put(slot, j):
            for h, st in enumerate(logits(j)):
                st_ref[2 * slot + h] = st

        def run(j0, steps, stats):
            for d in range(steps):
                put(1 - d % 2, j0 + d + 1)
                stats = softmax_pv(j0 + d, d % 2, stats, False)
            return stats

        init = tuple((jnp.full((1, tq), NEG, F32), jnp.zeros((1, tq), F32), jnp.zeros((HEAD_DIM, tq), F32))
                     for _ in range(2))
        put(0, 0)
        first, left, stats = 0, 2 * i, init
        for size in UNROLLS:
            trips = _shift_div(left, size)
            stats = lax.fori_loop(0, trips, lambda t, st, j0=first, n=size: run(j0 + n * t, n, st), stats)
            first, left = first + size * trips, left - size * trips
        put(1, 2 * i + 1)
        stats = softmax_pv(2 * i, 0, stats, True)
        (ma, la, acca), (mb, lb, accb) = softmax_pv(2 * i + 1, 1, stats, True)
        o_ref[...] = jnp.concatenate([acca / la, accb / lb], axis=0).T.astype(BF16)
        lse_ref[...] = jnp.where(_iota2((2, tq), 0) == 0, ma + jnp.log(la), mb + jnp.log(lb))

    return pl.pallas_call(
        body,
        grid=(N_PAIRS, s // tq),
        in_specs=[
            pl.BlockSpec((tq, LANES), lambda p, i: (i, p)),
            pl.BlockSpec((s, LANES), lambda p, i: (0, N_PAIRS + p)),
            pl.BlockSpec((s, LANES), lambda p, i: (0, 2 * N_PAIRS + p)),
            pl.BlockSpec((None, tq, 2), lambda p, i: (p, i, 0)),
            pl.BlockSpec((None, s, 2), lambda p, i: (p, 0, 0)),
        ],
        out_specs=[
            pl.BlockSpec((tq, LANES), lambda p, i: (i, p)),
            pl.BlockSpec((None, None, 2, tq), lambda p, i: (p, i, 0, 0)),
        ],
        out_shape=[SDS((s, ATTN_W), BF16), SDS((N_PAIRS, s // tq, 2, tq), F32)],
        scratch_shapes=[pltpu.VMEM((2 * nb, tk, LANES), BF16), pltpu.VMEM((nb, LANES, tk), BF16),
                        pltpu.VMEM((4, tk, tq), F32)],
        compiler_params=_cparams("arbitrary", "arbitrary"),
        name="attn_fwd",
    )(qkv, qkv, qkv, c_col, c_col)


def _pool_counts(row0, tm, w):
    t = row0 + _iota2((tm, 1), 0)
    return jnp.minimum(t + 1, w).astype(F32)


def _out_gate_up(attn_o, pool_o, w_out, x, g2, wg_t, wu_t, *, tm):
    s = x.shape[0]

    def body(a_ref, p_ref, wo_ref, x_ref, g_ref, wg_ref, wu_ref, x1_ref, h2_ref, r_ref, gate_ref, up_ref, act_ref):
        x1 = (x_ref[...] + jnp.dot(a_ref[...], wo_ref[0:ATTN_W, :], preferred_element_type=F32)
              + jnp.dot(p_ref[...], wo_ref[ATTN_W:, :], preferred_element_type=F32))
        r = lax.rsqrt(jnp.mean(x1 * x1, axis=-1, keepdims=True) + EPS)
        x1_ref[...] = x1
        r_ref[...] = r
        h2 = (x1 * r * g_ref[...]).astype(BF16)
        h2_ref[...] = h2
        gate = lax.dot_general(h2, wg_ref[...], NT, preferred_element_type=F32)
        up = lax.dot_general(h2, wu_ref[...], NT, preferred_element_type=F32)
        gate_ref[...] = gate.astype(BF16)
        up_ref[...] = up.astype(BF16)
        act_ref[...] = (gate * jax.nn.sigmoid(gate) * up).astype(BF16)

    row = lambda w: pl.BlockSpec((tm, w), lambda i: (i, 0))
    full = lambda a, b: pl.BlockSpec((a, b), lambda i: (0, 0))
    return pl.pallas_call(
        body,
        grid=(s // tm,),
        in_specs=[row(ATTN_W), row(POOL_W), full(D_MODEL, D_MODEL), row(D_MODEL), full(1, D_MODEL),
                  full(D_FF, D_MODEL), full(D_FF, D_MODEL)],
        out_specs=[row(D_MODEL), row(D_MODEL), row(1), row(D_FF), row(D_FF), row(D_FF)],
        out_shape=[SDS((s, D_MODEL), F32), SDS((s, D_MODEL), BF16), SDS((s, 1), F32), SDS((s, D_FF), BF16),
                   SDS((s, D_FF), BF16), SDS((s, D_FF), BF16)],
        compiler_params=_cparams("arbitrary"),
        name="out_gate_up",
    )(attn_o, pool_o, w_out, x, g2, wg_t, wu_t)


def _staggered(n, start, finish):
    pending = start(0)
    for k in range(n):
        following = start(k + 1) if k + 1 < n else None
        finish(k, pending)
        pending = following


def _down_final(act, wd, x1, gf, tgt, *, tm, sub):
    s = x1.shape[0]

    def body(a_ref, w_ref, x1_ref, g_ref, t_ref, dx2_ref, loss_ref, dgf_ref):
        @pl.when(pl.program_id(0) == 0)
        def _():
            loss_ref[...] = jnp.zeros_like(loss_ref)
            dgf_ref[...] = jnp.zeros_like(dgf_ref)

        g = g_ref[...]

        def matmul(k):
            return jnp.dot(a_ref[k * sub:(k + 1) * sub, :], w_ref[...], preferred_element_type=F32)

        def rest(k, mm):
            rows = slice(k * sub, (k + 1) * sub)
            x2 = x1_ref[rows, :] + mm
            r = lax.rsqrt(jnp.mean(x2 * x2, axis=-1, keepdims=True) + EPS)
            xn = x2 * r
            diff = xn * g - t_ref[rows, :]
            loss_ref[...] += jnp.sum(diff * diff, axis=0, keepdims=True)
            dy = diff * (1.0 / D_MODEL)
            dgf_ref[...] += jnp.sum(dy * xn, axis=0, keepdims=True)
            dxn = dy * g
            dx2_ref[rows, :] = r * (dxn - xn * jnp.mean(dxn * xn, axis=-1, keepdims=True))

        _staggered(tm // sub, matmul, rest)

    row = lambda w: pl.BlockSpec((tm, w), lambda i: (i, 0))
    full = lambda a, b: pl.BlockSpec((a, b), lambda i: (0, 0))
    return pl.pallas_call(
        body,
        grid=(s // tm,),
        in_specs=[row(D_FF), full(D_FF, D_MODEL), row(D_MODEL), full(1, D_MODEL), row(D_MODEL)],
        out_specs=[row(D_MODEL), full(1, D_MODEL), full(1, D_MODEL)],
        out_shape=[SDS((s, D_MODEL), F32), SDS((1, D_MODEL), F32), SDS((1, D_MODEL), F32)],
        compiler_params=_cparams("arbitrary"),
        name="down_final",
    )(act, wd, x1, gf, tgt)


def _swiglu_bwd(dx2, wd, gate, up, *, tm, tn):
    s = dx2.shape[0]

    def body(d_ref, w_ref, gate_ref, up_ref, dgate_ref, dup_ref):
        dact = lax.dot_general(d_ref[...].astype(BF16), w_ref[...], NT, preferred_element_type=F32)
        gate = gate_ref[...].astype(F32)
        sg = jax.nn.sigmoid(gate)
        dup_ref[...] = (dact * (gate * sg)).astype(BF16)
        dgate_ref[...] = (dact * up_ref[...].astype(F32) * (sg * (1.0 + gate * (1.0 - sg)))).astype(BF16)

    ospec = pl.BlockSpec((tm, tn), lambda c, r: (r, c))
    return pl.pallas_call(
        body,
        grid=(D_FF // tn, s // tm),
        in_specs=[pl.BlockSpec((tm, D_MODEL), lambda c, r: (r, 0)), pl.BlockSpec((tn, D_MODEL), lambda c, r: (c, 0)),
                  ospec, ospec],
        out_specs=[ospec, ospec],
        out_shape=[SDS((s, D_FF), BF16), SDS((s, D_FF), BF16)],
        compiler_params=_cparams("arbitrary", "arbitrary"),
        name="swiglu_bwd",
    )(dx2, wd, gate, up)


def _mm_tn_stacked(as_, rows, b, *, ts, name, windows=None):
    s, nb_ = b.shape
    n = len(as_)
    offsets = [sum(rows[:i]) for i in range(n)]
    total = sum(rows)
    if windows is None:
        acc_rows, out_shape = total, (total, nb_)
    else:
        count, step, size = windows
        acc_rows, out_shape = max(total, (count - 1) * step + size), (count, size, nb_)

    def body(*refs):
        a_refs, b_ref, o_ref, acc_ref = refs[:n], refs[n], refs[n + 1], refs[n + 2]
        k = pl.program_id(0)

        @pl.when(k == 0)
        def _():
            acc_ref[...] = jnp.zeros_like(acc_ref)

        bv = b_ref[...].astype(BF16)
        for a_ref, off, cnt in zip(a_refs, offsets, rows):
            part = lax.dot_general(a_ref[...].astype(BF16), bv, TN, preferred_element_type=F32)
            acc_ref[off:off + cnt, :] += part[0:cnt, :]

        @pl.when(k == s // ts - 1)
        def _():
            if windows is None:
                o_ref[...] = acc_ref[...].astype(BF16)
            else:
                for d in range(count):
                    o_ref[d] = acc_ref[d * step:d * step + size, :].astype(BF16)

    return pl.pallas_call(
        body,
        grid=(s // ts,),
        in_specs=[pl.BlockSpec((ts, a.shape[1]), lambda k: (k, 0)) for a in as_] + [pl.BlockSpec((ts, nb_), lambda k: (k, 0))],
        out_specs=pl.BlockSpec(out_shape, lambda k: (0,) * len(out_shape)),
        out_shape=SDS(out_shape, BF16),
        scratch_shapes=[pltpu.VMEM((acc_rows, nb_), F32)],
        compiler_params=_cparams("arbitrary"),
        name=name,
    )(*as_, b)


def _norm_bwd(dh, x, r, g, dres):
    xn = x * r
    dxn = dh * g
    dx = dres + r * (dxn - xn * jnp.mean(dxn * xn, axis=-1, keepdims=True))
    return dx, jnp.sum(dh * xn, axis=0, keepdims=True)


def _mlp_in_pool_bwd(dgate, dup, wg_t, wu_t, w_out, x1, r2, g2, dx2, pooled, w_pool, pool_scale, *, tm):
    s = x1.shape[0]
    nt = s // tm
    ng = len(POOL_WINDOWS)

    def body(dg_ref, dup_ref, wg_ref, wu_ref, wo_ref, x_ref, r_ref, g_ref, d_ref, p_ref, w_ref, sc_ref,
             dx1_ref, dattn_ref, du_ref, dg2_ref, dw_ref, dsc_ref, head_ref):
        i = pl.program_id(0)

        @pl.when(i == 0)
        def _():
            dg2_ref[...] = jnp.zeros_like(dg2_ref)
            head_ref[...] = jnp.zeros_like(head_ref)
            dw_ref[...] = jnp.zeros_like(dw_ref)
            dsc_ref[...] = jnp.zeros_like(dsc_ref)

        dh2 = (jnp.dot(dg_ref[...], wg_ref[...], preferred_element_type=F32)
               + jnp.dot(dup_ref[...], wu_ref[...], preferred_element_type=F32))
        dx1, dg2 = _norm_bwd(dh2, x_ref[...], r_ref[...], g_ref[...], d_ref[...])
        dg2_ref[...] += dg2
        dx1_ref[...] = dx1
        dmix = lax.dot_general(dx1.astype(BF16), wo_ref[...], NT, preferred_element_type=F32)
        dattn_ref[...] = dmix[:, 0:ATTN_W]
        row0 = (nt - 1 - i) * tm
        for g, w in enumerate(POOL_WINDOWS):
            cols = slice(g * POOL_G, (g + 1) * POOL_G)
            wb = w_ref[g].astype(BF16)
            pooled_g = p_ref[:, cols]
            dpo = dmix[:, ATTN_W + g * POOL_G:ATTN_W + (g + 1) * POOL_G]
            mixed = jnp.dot(pooled_g, wb, preferred_element_type=F32)
            dsc_ref[:, cols] += jnp.sum(dpo * mixed, axis=0, keepdims=True)
            dmp = (dpo * sc_ref[:, cols]).astype(BF16)
            dw_ref[g] += lax.dot_general(pooled_g, dmp, TN, preferred_element_type=F32)
            dpooled = lax.dot_general(dmp, wb, NT, preferred_element_type=F32)
            a = dpooled / _pool_counts(row0, tm, w)
            acc = jnp.concatenate([a, head_ref[:, cols]], axis=0)
            head_ref[:, cols] = a[0:HALO, :]
            k = 1
            while k < w:
                acc = acc + pltpu.roll(acc, tm + HALO - k, axis=0)
                k *= 2
            du_ref[:, cols] = (acc[0:tm, :] - dpooled).astype(BF16)

    row = lambda w: pl.BlockSpec((tm, w), lambda i: (nt - 1 - i, 0))
    full = lambda a, b: pl.BlockSpec((a, b), lambda i: (0, 0))
    pool_w = pl.BlockSpec((ng, POOL_G, POOL_G), lambda i: (0, 0, 0))
    return pl.pallas_call(
        body,
        grid=(nt,),
        in_specs=[row(D_FF), row(D_FF), full(D_FF, D_MODEL), full(D_FF, D_MODEL), full(D_MODEL, D_MODEL),
                  row(D_MODEL), row(1), full(1, D_MODEL), row(D_MODEL), row(POOL_W), pool_w, full(1, POOL_W)],
        out_specs=[row(D_MODEL), row(ATTN_W), row(POOL_W), full(1, D_MODEL), pool_w, full(1, POOL_W)],
        out_shape=[SDS((s, D_MODEL), F32), SDS((s, ATTN_W), F32), SDS((s, POOL_W), BF16), SDS((1, D_MODEL), F32),
                   SDS((ng, POOL_G, POOL_G), F32), SDS((1, POOL_W), F32)],
        scratch_shapes=[pltpu.VMEM((HALO, POOL_W), F32)],
        compiler_params=_cparams("arbitrary"),
        name="mlp_in_pool_bwd",
    )(dgate, dup, wg_t, wu_t, w_out, x1, r2, g2, dx2, pooled, w_pool, pool_scale)


SUM_ROWS = 16


def _heads_t(t):
    n = t.shape[0]
    lane = _iota2((n, LANES), 1)
    tf = t.astype(F32)
    halves = jnp.concatenate([jnp.where(lane < HEAD_DIM, tf, 0.0).T, jnp.where(lane < HEAD_DIM, 0.0, tf).T], axis=1)
    r, c = _iota2((SUM_ROWS, 2 * n), 0), _iota2((SUM_ROWS, 2 * n), 1)
    ones = jnp.where(((r == 0) & (c < n)) | ((r == 4) & (c >= n)), 1.0, 0.0)
    return jnp.concatenate([halves, ones], axis=0).astype(BF16)


def _attn_bwd(qkv, attn_o, d_attn, rowb, ck_col, after, *, tq):
    s = qkv.shape[0]
    tk = tq
    nb = s // tq
    rows_t = LANES + SUM_ROWS

    def body(q_ref, k_ref, v_ref, o_ref, do_ref, rowb_ref, ck_ref, _, dq_ref, dk_ref, dv_ref, dck_ref, dcq_ref,
             dqt_ref, delta_ref, kp_ref, qp_ref, dob_ref, qt_ref, kt_ref, dot_ref, front_ref):
        lane = _iota2((tq, LANES), 1)
        lo = lane < HEAD_DIM
        first = _iota2((8, LANES), 1) < HEAD_DIM
        sel = jnp.where(_iota2((8, LANES), 0) < 4, jnp.where(first, 1.0, 0.0), jnp.where(first, 0.0, 1.0))

        def prep(b, _):
            st = pl.multiple_of(b * tq, tq)
            do2 = do_ref[pl.ds(st, tq), :]
            delta_ref[b] = _sel_dot(sel, do2 * o_ref[pl.ds(st, tq), :].astype(F32), NT)
            dob_ref[pl.ds(st, tq), :] = do2.astype(BF16)
            dqt_ref[b] = jnp.zeros((rows_t, tq), F32)
            k2 = k_ref[pl.ds(st, tq), :].astype(F32)
            q2 = q_ref[pl.ds(st, tq), :].astype(F32)
            ck = ck_ref[pl.ds(st, tq), :]
            for h in range(2):
                kp_ref[h * nb + b] = _augment(k2, h, -ck[:, h:h + 1], True)
                qp_ref[h * nb + b] = _augment(q2 * Q_SCALE, h, jnp.zeros((tq, 1), F32), False)
            qt_ref[b] = _heads_t(q2)
            kt_ref[b] = _heads_t(k2)
            dot_ref[b] = _heads_t(do2)[0:LANES, :]
            return 0

        lax.fori_loop(0, nb, prep, 0)

        def split(t):
            z = jnp.zeros_like(t)
            return jnp.where(lo, t, z), jnp.where(lo, z, t)

        def kv_block(j, _):
            st_j = pl.multiple_of(j * tk, tk)
            vs = split(v_ref[pl.ds(st_j, tk), :])
            kt = kt_ref[j]

            def stage(i, slot):
                ic = jnp.minimum(i, nb - 1)
                do2 = dob_ref[pl.ds(pl.multiple_of(ic * tq, tq), tq), :]
                for h in range(2):
                    front_ref[4 * slot + h] = lax.dot_general(kp_ref[h * nb + j], qp_ref[h * nb + ic], NT,
                                                              preferred_element_type=F32)
                    front_ref[4 * slot + 2 + h] = lax.dot_general(vs[h], do2, NT, preferred_element_type=F32)

            def q_block(i, slot, carry, diagonal):
                dkt, dvt = carry
                ic = jnp.minimum(i, nb - 1)
                rb = rowb_ref[ic] + jnp.where(i < nb, 0.0, NEG)
                dl = delta_ref[ic]
                pts, dsts = [], []
                for h in range(2):
                    st = front_ref[4 * slot + h] + rb[h:h + 1, :]
                    if diagonal:
                        st = jnp.where(_iota2((tk, tq), 0) <= _iota2((tk, tq), 1), st, NEG)
                    pt = jnp.exp(st)
                    pts.append(pt.astype(BF16))
                    dsts.append((pt * (front_ref[4 * slot + 2 + h] - dl[4 * h:4 * h + 1, :])).astype(BF16))
                dvt = dvt + lax.dot_general(dot_ref[ic], jnp.concatenate(pts, axis=1), NT, preferred_element_type=F32)
                dkt = dkt + lax.dot_general(qt_ref[ic], jnp.concatenate(dsts, axis=1), NT, preferred_element_type=F32)
                dqt_ref[ic] += jnp.dot(kt, jnp.concatenate(dsts, axis=0), preferred_element_type=F32)
                return dkt, dvt

            def run(i0, steps, carry):
                for d in range(steps):
                    stage(i0 + d + 1, d % 2)
                    carry = q_block(i0 + d, 1 - d % 2, carry, False)
                return carry

            stage(j, 0)
            stage(j + 1, 1)
            carry = q_block(j, 0, (jnp.zeros((rows_t, tk), F32), jnp.zeros((LANES, tk), F32)), True)
            first, left = j + 1, nb - 1 - j
            for size in UNROLLS:
                trips = _shift_div(left + 1 if size == UNROLLS[-1] else left, size)
                carry = lax.fori_loop(0, trips, lambda t, c, i0=first, n=size: run(i0 + n * t, n, c), carry)
                first, left = first + size * trips, left - size * trips
            dkt, dvt = carry
            dk_ref[pl.ds(st_j, tk), :] = (dkt[0:LANES, :].T * Q_SCALE).astype(BF16)
            dv_ref[pl.ds(st_j, tk), :] = dvt.T.astype(BF16)
            dck_ref[j] = dkt[LANES:LANES + 8, :]
            return 0

        lax.fori_loop(0, nb, kv_block, 0)

        def finish(b, _):
            acc = dqt_ref[b]
            dq_ref[pl.ds(pl.multiple_of(b * tq, tq), tq), :] = (acc[0:LANES, :].T * Q_SCALE).astype(BF16)
            dcq_ref[b] = acc[LANES:LANES + 8, :]
            return 0

        lax.fori_loop(0, nb, finish, 0)

    col = lambda off: pl.BlockSpec((s, LANES), lambda p: (0, off + p))
    sums = pl.BlockSpec((None, nb, 8, tq), lambda p: (p, 0, 0, 0))
    return pl.pallas_call(
        body,
        grid=(N_PAIRS,),
        in_specs=[col(0), col(N_PAIRS), col(2 * N_PAIRS), col(0), col(0),
                  pl.BlockSpec((None, nb, 2, tq), lambda p: (p, 0, 0, 0)),
                  pl.BlockSpec((None, s, 2), lambda p: (p, 0, 0)), _UNREAD],
        out_specs=[col(0), col(0), col(0), sums, sums],
        out_shape=[SDS((s, ATTN_W), BF16), SDS((s, ATTN_W), BF16), SDS((s, ATTN_W), BF16),
                   SDS((N_PAIRS, nb, 8, tq), F32), SDS((N_PAIRS, nb, 8, tq), F32)],
        scratch_shapes=[pltpu.VMEM((nb, rows_t, tq), F32), pltpu.VMEM((nb, 8, tq), F32),
                        pltpu.VMEM((2 * nb, tk, LANES), BF16), pltpu.VMEM((2 * nb, tq, LANES), BF16),
                        pltpu.VMEM((s, LANES), BF16), pltpu.VMEM((nb, rows_t, 2 * tq), BF16),
                        pltpu.VMEM((nb, rows_t, 2 * tk), BF16), pltpu.VMEM((nb, LANES, 2 * tq), BF16),
                        pltpu.VMEM((8, tk, tq), F32)],
        compiler_params=_cparams("arbitrary"),
        name="attn_bwd",
    )(qkv, qkv, qkv, attn_o, d_attn, rowb, ck_col, after)


def _forget_bwd(dc_t, fl_t, b_rows):
    rows = fl_t.shape[0]
    nb = rows // N_HEADS

    def body(dc_ref, fl_ref, b_ref, dfl_ref, db_ref):
        dc = dc_ref[...]
        lower = _iota2((LANES, LANES), 0) >= _iota2((LANES, LANES), 1)
        ones = jnp.ones((LANES, LANES), F32)
        rr, cc, same = _head_block_masks(rows, nb)
        dlf = _dot_sel(dc, lower) + _sel_dot(same & (cc > rr), _dot_sel(dc, ones))
        dfl = dlf / (1.0 + jnp.exp(fl_ref[...] + b_ref[...]))
        dfl_ref[...] = dfl
        shift = nb.bit_length() - 1
        hsel = lax.shift_right_logical(_iota2((N_HEADS, rows), 1), shift) == _iota2((N_HEADS, rows), 0)
        db_ref[...] = _sel_dot(hsel, _dot_sel(dfl, ones))

    return pl.pallas_call(body, out_shape=[SDS(fl_t.shape, F32), SDS((N_HEADS, LANES), F32)],
                          compiler_params=_cparams(), name="forget_bwd")(dc_t, fl_t, b_rows)


def _in_bwd(dq, dk, dv, du, dfl, w_in_t, x, r1, g1, dx1, after, *, tm):
    s = x.shape[0]
    pieces = ((0, ATTN_W), (ATTN_W, 2 * ATTN_W), (2 * ATTN_W, QKV_W), (U_OFF, F_OFF), (F_OFF, IN_PAD))

    def body(dq_ref, dk_ref, dv_ref, du_ref, df_ref, w_ref, x_ref, r_ref, g_ref, d_ref, _, dx_ref, dg1_ref):
        @pl.when(pl.program_id(0) == 0)
        def _():
            dg1_ref[...] = jnp.zeros_like(dg1_ref)

        dh = None
        for ref, (c0, c1) in zip((dq_ref, dk_ref, dv_ref, du_ref, df_ref), pieces):
            t = jnp.dot(ref[...], w_ref[c0:c1, :], preferred_element_type=F32)
            dh = t if dh is None else dh + t
        dx, dg1 = _norm_bwd(dh, x_ref[...], r_ref[...], g_ref[...], d_ref[...])
        dx_ref[...] = dx
        dg1_ref[...] += dg1

    row = lambda w: pl.BlockSpec((tm, w), lambda i: (i, 0))
    full = lambda a, b: pl.BlockSpec((a, b), lambda i: (0, 0))
    return pl.pallas_call(
        body,
        grid=(s // tm,),
        in_specs=[row(ATTN_W), row(ATTN_W), row(ATTN_W), row(POOL_W), row(LANES), full(IN_PAD, D_MODEL),
                  row(D_MODEL), row(1), full(1, D_MODEL), row(D_MODEL), _UNREAD],
        out_specs=[row(D_MODEL), full(1, D_MODEL)],
        out_shape=[SDS((s, D_MODEL), F32), SDS((1, D_MODEL), F32)],
        compiler_params=_cparams("arbitrary"),
        name="in_bwd",
    )(dq, dk, dv, du, dfl, w_in_t, x, r1, g1, dx1, after)


def _tiles(s):
    big = min(512, s)
    return dict(row=big, attn=min(256, s // 2), ff_rows=min(256, s), tall=min(1024, s))


def _local_step(x, tgt, p, weight, emit, started):
    s = x.shape[0]
    t = _tiles(s)
    tm, tq = t["row"], t["attn"]
    nb = s // LANES
    nqb = s // tq
    g1, g2, gf = p["norm1_g"], p["norm2_g"], p["final_g"].reshape(1, D_MODEL)
    w_pool, pool_scale = p["w_pool"][0], p["pool_scale"]

    h, r1 = _norm1(x, g1, started, tm=tm)
    w_in_t = weight("w_in", h)
    qkv, fl, pooled, pool_o = _in_proj_pool(h, w_in_t, w_pool, pool_scale, tm=t["tall"])
    fl_t = fl[:, :N_HEADS].T.reshape(N_HEADS * nb, LANES)
    b_rows = jnp.repeat(p["b_forget"].reshape(N_HEADS), nb).reshape(N_HEADS * nb, 1)
    c = _forget_cumsum(fl_t, b_rows).reshape(N_PAIRS, 2, s)
    c_col = c.transpose(0, 2, 1)
    c_rowblk = c.reshape(N_PAIRS, 2, nqb, tq).transpose(0, 2, 1, 3)
    attn_o, lse = _attn_fwd(qkv, c_col, tk=tq)
    lse = lse.reshape(N_PAIRS, nqb // 2, 2, 2, tq).transpose(0, 1, 3, 2, 4).reshape(N_PAIRS, nqb, 2, tq)
    w_out = weight("w_out", attn_o)
    wg_t, wu_t = weight("w_gate_up", attn_o)
    x1, h2, r2, gate, up, act = _out_gate_up(attn_o, pool_o, w_out, x, g2, wg_t, wu_t, tm=t["ff_rows"])
    wd = weight("w_down", act)
    dx2, loss_row, d_gf = _down_final(act, wd, x1, gf, tgt, tm=tm, sub=min(128, tm))

    dgate, dup = _swiglu_bwd(dx2, wd, gate, up, tm=t["ff_rows"], tn=D_FF)
    d_wd = _mm_tn_stacked([act], [D_FF], dx2, ts=t["tall"], name="grad_w_down")
    d_wg_t = _mm_tn_stacked([dgate], [D_FF], h2, ts=t["tall"], name="grad_w_gate")
    d_wu_t = _mm_tn_stacked([dup], [D_FF], h2, ts=t["tall"], name="grad_w_up")
    dx1, d_attn, du, d_g2, d_wpool, d_pscale = _mlp_in_pool_bwd(dgate, dup, wg_t, wu_t, w_out, x1, r2, g2, dx2, pooled,
                                                               w_pool, pool_scale, tm=t["ff_rows"])
    d_wo = _mm_tn_stacked([attn_o, pool_o], [ATTN_W, POOL_W], dx1, ts=t["tall"], name="grad_w_out")
    token = emit(("w_down", "w_gate", "w_up", "w_out"), (d_wd, d_wg_t, d_wu_t, d_wo))
    dq, dk, dv, dck, dcq = _attn_bwd(qkv, attn_o, d_attn, c_rowblk - lse, c_col, token, tq=tq)
    dc_t = (dcq - dck)[:, :, 0::4, :].transpose(0, 2, 1, 3).reshape(N_HEADS * nb, LANES)
    dfl_t, db = _forget_bwd(dc_t, fl_t, b_rows)
    dfl = jnp.pad(dfl_t.reshape(N_HEADS, s).T, ((0, 0), (0, LANES - N_HEADS))).astype(BF16)
    d_w_in_t = _mm_tn_stacked([dq, dk, dv, dfl, du], [ATTN_W, ATTN_W, ATTN_W, N_HEADS, POOL_W], h, ts=t["tall"],
                              name="grad_w_in",
                              windows=(N_DEV, IN_STEP, IN_WINDOW))
    token = emit(("w_in",), (d_w_in_t,))
    dx, d_g1 = _in_bwd(dq, dk, dv, du, dfl, w_in_t, x, r1, g1, dx1, token, tm=tm)

    small = dict(norm1_g=d_g1, b_forget=db[:, 0].reshape(1, N_HEADS), w_pool=d_wpool, pool_scale=d_pscale,
                 norm2_g=d_g2, final_g=d_gf)
    return loss_row, dx, small


def _my_index():
    return 4 * lax.axis_index("x") + 2 * lax.axis_index("y") + lax.axis_index("c")


def _peer(k):
    pos = [lax.axis_index(a) for a in ("x", "y", "c")]
    flipped = tuple(1 - p if (k >> b) & 1 else p for p, b in zip(pos, (2, 1, 0)))
    return flipped, 4 * flipped[0] + 2 * flipped[1] + flipped[2]


_HBM = pl.BlockSpec(memory_space=pltpu.HBM)
_SEM = pl.BlockSpec(memory_space=pltpu.SEMAPHORE)
_DATAFLOW = pltpu.SideEffectType.DATAFLOW_SIDE_EFFECTING


ALL_PEERS = tuple(range(1, N_DEV))
SAME_CORE = (1, 2, 4, 6)


def _peer_copies(ins, lands, send_sems, recv_sems, scatter, peers, arrivals):
    me = _my_index()
    copies = []
    for w in range(len(ins)):
        for k in peers[w]:
            dev, idx = _peer(k)
            copies.append(pltpu.make_async_remote_copy(
                src_ref=ins[w].at[idx] if scatter[w] else ins[w], dst_ref=lands[w].at[idx if arrivals else me],
                send_sem=send_sems[w].at[k - 1], recv_sem=recv_sems[w].at[k - 1], device_id=dev, device_id_type=MESH))
    return copies


def _own_copies(ins, lands, send_sems, scatter):
    me = _my_index()
    return [pltpu.make_async_copy(ins[w].at[me] if scatter[w] else ins[w], lands[w].at[me], send_sems[w].at[N_DEV - 1])
            for w in range(len(ins))]


def _forward_copies(land, send_sems, recv_sems, arrivals):
    sibling, _ = _peer(1)
    copies = []
    for j, k in enumerate(SAME_CORE[1:]):
        src, dst = _peer(k)[1], _peer(k ^ 1 if arrivals else k)[1]
        copies.append(pltpu.make_async_remote_copy(
            src_ref=land.at[src], dst_ref=land.at[dst], send_sem=send_sems.at[j], recv_sem=recv_sems.at[j],
            device_id=sibling, device_id_type=MESH))
    return copies


def _forward_start(land, name):
    def body(land_ref, send_sems, recv_sems, land_thru, token):
        for cp in _forward_copies(land_ref, send_sems, recv_sems, False):
            cp.start()
        token[...] = jnp.zeros_like(token)

    sem = pltpu.SemaphoreType.DMA((len(SAME_CORE) - 1,))
    send, recv, thru, _ = pl.pallas_call(
        body,
        in_specs=[_HBM],
        out_specs=[_SEM, _SEM, _HBM, pl.BlockSpec(memory_space=pltpu.VMEM)],
        out_shape=[sem, sem, pltpu.HBM(land.shape, land.dtype), SDS((8, LANES), F32)],
        input_output_aliases={0: 2},
        compiler_params=pltpu.CompilerParams(has_side_effects=_DATAFLOW),
        name=name,
    )(land)
    return send, recv, thru


def _forward_wait(handle, after, name):
    def body(land_ref, send_sems, recv_sems, after_ref, land_out):
        for cp in _forward_copies(land_ref, send_sems, recv_sems, False):
            cp.wait_send()
        for cp in _forward_copies(land_ref, send_sems, recv_sems, True):
            cp.wait_recv()

    send, recv, land = handle
    return pl.pallas_call(
        body,
        in_specs=[_HBM, _SEM, _SEM, pl.BlockSpec(memory_space=pl.ANY)],
        out_specs=_HBM,
        out_shape=pltpu.HBM(land.shape, land.dtype),
        input_output_aliases={0: 0},
        compiler_params=pltpu.CompilerParams(has_side_effects=_DATAFLOW),
        name=name,
    )(land, send, recv, after)


def _exchange_start(arrays, scatter, name, peers=None, after=None):
    n = len(arrays)
    peers = peers or [ALL_PEERS] * n
    order = [] if after is None else [after]
    land_shapes = [(N_DEV,) + tuple(a.shape[1:] if sc else a.shape) for a, sc in zip(arrays, scatter)]

    def body(*refs):
        ins, lands = refs[:n], refs[n:2 * n]
        outs = refs[2 * n + len(order):]
        send_sems, recv_sems, token = outs[:n], outs[n:2 * n], outs[4 * n]
        for cp in _peer_copies(ins, lands, send_sems, recv_sems, scatter, peers, False):
            cp.start()
        for cp in _own_copies(ins, lands, send_sems, scatter):
            cp.start()
        token[...] = jnp.zeros_like(token)

    sends, recvs = pltpu.SemaphoreType.DMA((N_DEV,)), pltpu.SemaphoreType.DMA((N_DEV - 1,))
    outs = pl.pallas_call(
        body,
        in_specs=[_HBM] * (2 * n) + [_UNREAD] * len(order),
        out_specs=[_SEM] * (2 * n) + [_HBM] * (2 * n) + [pl.BlockSpec(memory_space=pltpu.VMEM)],
        out_shape=[sends] * n + [recvs] * n + [pltpu.HBM(a.shape, a.dtype) for a in arrays]
        + [pltpu.HBM(sh, a.dtype) for sh, a in zip(land_shapes, arrays)] + [SDS((8, LANES), F32)],
        input_output_aliases={i: 2 * n + i for i in range(2 * n)},
        compiler_params=pltpu.CompilerParams(has_side_effects=_DATAFLOW),
        name=name,
    )(*[pltpu.with_memory_space_constraint(a, pltpu.HBM) for a in arrays],
      *[pltpu.with_memory_space_constraint(lax.empty(sh, a.dtype), pltpu.HBM) for sh, a in zip(land_shapes, arrays)],
      *order)
    handles = [dict(send=outs[w], recv=outs[n + w], src=outs[2 * n + w], land=outs[3 * n + w], scatter=scatter[w],
                    peers=peers[w]) for w in range(n)]
    return handles, outs[4 * n]


def _exchange_wait(handles, after, name):
    n = len(handles)
    scatter, peers = [h["scatter"] for h in handles], [h["peers"] for h in handles]

    def body(*refs):
        ins, lands = refs[:n], refs[n:2 * n]
        send_sems, recv_sems = refs[2 * n:3 * n], refs[3 * n:4 * n]
        for cp in _peer_copies(ins, lands, send_sems, recv_sems, scatter, peers, False):
            cp.wait_send()
        for cp in _peer_copies(ins, lands, send_sems, recv_sems, scatter, peers, True):
            cp.wait_recv()
        for cp in _own_copies(ins, lands, send_sems, scatter):
            cp.wait()

    srcs, lands = [h["src"] for h in handles], [h["land"] for h in handles]
    outs = pl.pallas_call(
        body,
        in_specs=[_HBM] * (2 * n) + [_SEM] * (2 * n) + [pl.BlockSpec(memory_space=pl.ANY)],
        out_specs=[_HBM] * (2 * n),
        out_shape=[pltpu.HBM(a.shape, a.dtype) for a in srcs + lands],
        input_output_aliases={i: i for i in range(2 * n)},
        compiler_params=pltpu.CompilerParams(has_side_effects=_DATAFLOW),
        name=name,
    )(*srcs, *lands, *[h["send"] for h in handles], *[h["recv"] for h in handles], after)
    return outs[n:]


def _adamw(parts, w, m, v, name):
    rows, cols = w.shape
    tr = rows // 4 if rows % 32 == 0 else rows

    def body(p_ref, w_ref, m_ref, v_ref, g_ref, d_ref, mo_ref, vo_ref):
        g = p_ref[0].astype(F32)
        for d in range(1, N_DEV):
            g = g + p_ref[d].astype(F32)
        g_ref[...] = g
        d_ref[...], mo_ref[...], vo_ref[...] = _adam_update(g, w_ref[...], m_ref[...], v_ref[...])

    blk = pl.BlockSpec((tr, cols), lambda i: (i, 0))
    return pl.pallas_call(
        body,
        grid=(rows // tr,),
        in_specs=[pl.BlockSpec((N_DEV, tr, cols), lambda i: (0, i, 0)), blk, blk, blk],
        out_specs=[blk] * 4,
        out_shape=[SDS((rows, cols), F32)] * 4,
        compiler_params=_cparams("arbitrary"),
        name=name,
    )(parts, w, m, v)


def _adamw_dense(parts, w, m, v, name, *, rows, shift):
    _, window, cols = parts.shape
    per_row = cols // LANES

    def body(p_ref, w_ref, m_ref, v_ref, g_ref, d_ref, mo_ref, vo_ref, sum_ref):
        g = p_ref[0].astype(F32)
        for d in range(1, N_DEV):
            g = g + p_ref[d].astype(F32)
        sum_ref[...] = g
        me = _my_index()
        for j in range(N_DEV):
            @pl.when(me == j)
            def _(j=j):
                for c in range(per_row):
                    at = (pl.ds(c, rows, stride=per_row), slice(None))
                    gc = sum_ref[j * shift:j * shift + rows, c * LANES:(c + 1) * LANES]
                    g_ref[at] = gc
                    d_ref[at], mo_ref[at], vo_ref[at] = _adam_update(gc, w_ref[at], m_ref[at], v_ref[at])

    return pl.pallas_call(
        body,
        out_shape=[SDS(w.shape, F32)] * 4,
        scratch_shapes=[pltpu.VMEM((window, cols), F32)],
        compiler_params=_cparams(),
        name=name,
    )(parts, w, m, v)


_ROW_OF = dict(norm1_g=(0, D_MODEL), norm2_g=(1, D_MODEL), final_g=(2, D_MODEL), pool_scale=(3, POOL_W),
               b_forget=(4, N_HEADS), loss=(5, 1))


def _pack_rows(vals):
    rows = [jnp.pad(vals[n].reshape(1, width).astype(F32), ((0, 0), (0, D_MODEL - width)))
            for n, (_, width) in sorted(_ROW_OF.items(), key=lambda kv: kv[1][0])]
    return jnp.concatenate(rows + [jnp.zeros((8 - len(rows), D_MODEL), F32)], axis=0)


def _adam_update(g, w, m, v):
    m_new = ADAM_B1 * m + (1.0 - ADAM_B1) * g
    v_new = ADAM_B2 * v + (1.0 - ADAM_B2) * (g * g)
    m_hat = m_new / (1.0 - ADAM_B1 ** ADAM_STEP)
    v_hat = v_new / (1.0 - ADAM_B2 ** ADAM_STEP)
    return -ADAM_LR * (m_hat / (jnp.sqrt(v_hat) + ADAM_EPS) + ADAM_WD * w), m_new, v_new


def _adamw_replicated(parts_rows, parts_pool, w, m, v):
    names = ("norm1_g", "norm2_g", "final_g", "pool_scale", "b_forget", "w_pool")
    shapes = {n: ((len(POOL_WINDOWS), POOL_G, POOL_G) if n == "w_pool" else (1, _ROW_OF[n][1])) for n in names}

    def body(rows_ref, pool_ref, *refs):
        ins, outs = refs[:3 * len(names)], refs[3 * len(names):]

        def total(n):
            if n == "w_pool":
                pieces = [pool_ref[d] for d in range(N_DEV)]
            else:
                row, width = _ROW_OF[n]
                pieces = [rows_ref[d, row:row + 1, 0:width] for d in range(N_DEV)]
            g = pieces[0]
            for p in pieces[1:]:
                g = g + p
            return g

        outs[0][...] = total("loss")
        for k, n in enumerate(names):
            g = total(n)
            delta, m_new, v_new = _adam_update(g, ins[3 * k][...], ins[3 * k + 1][...], ins[3 * k + 2][...])
            for o_ref, val in zip(outs[1 + 4 * k:5 + 4 * k], (g, delta, m_new, v_new)):
                o_ref[...] = val

    args = [d[n].reshape(shapes[n]) for n in names for d in (w, m, v)]
    res = pl.pallas_call(
        body,
        out_shape=[SDS((1, 1), F32)] + [SDS(shapes[n], F32) for n in names for _ in range(4)],
        compiler_params=_cparams(),
        name="adamw_replicated",
    )(parts_rows, parts_pool, *args)
    return res[0], {n: [r.reshape(w[n].shape) for r in res[1 + 4 * k:5 + 4 * k]] for k, n in enumerate(names)}


def kernel(x, norm1_g, w_in, b_forget, w_pool, pool_scale, w_out, norm2_g, w_gate, w_up, w_down, final_g, loss_target, m_norm1_g, m_w_in, m_b_forget, m_w_pool, m_pool_scale, m_w_out, m_norm2_g, m_w_gate, m_w_up, m_w_down, m_final_g, v_norm1_g, v_w_in, v_b_forget, v_w_pool, v_pool_scale, v_w_out, v_norm2_g, v_w_gate, v_w_up, v_w_down, v_final_g):
    big = ("w_in", "w_out", "w_gate", "w_up", "w_down")
    order = ("norm1_g", "w_in", "b_forget", "w_pool", "pool_scale", "w_out", "norm2_g", "w_gate", "w_up", "w_down",
             "final_g")
    w = dict(norm1_g=norm1_g, w_in=w_in, b_forget=b_forget, w_pool=w_pool, pool_scale=pool_scale, w_out=w_out,
             norm2_g=norm2_g, w_gate=w_gate, w_up=w_up, w_down=w_down, final_g=final_g)
    m = dict(norm1_g=m_norm1_g, w_in=m_w_in, b_forget=m_b_forget, w_pool=m_w_pool, pool_scale=m_pool_scale,
             w_out=m_w_out, norm2_g=m_norm2_g, w_gate=m_w_gate, w_up=m_w_up, w_down=m_w_down, final_g=m_final_g)
    v = dict(norm1_g=v_norm1_g, w_in=v_w_in, b_forget=v_b_forget, w_pool=v_w_pool, pool_scale=v_pool_scale,
             w_out=v_w_out, norm2_g=v_norm2_g, w_gate=v_w_gate, w_up=v_w_up, w_down=v_w_down, final_g=v_final_g)

    flipped = ("w_in", "w_gate", "w_up")
    shard = lambda d, n: d[n][0].T if n in flipped else d[n][0]
    cast = lambda n: shard(w, n).astype(BF16)
    (first,), started = _exchange_start([cast("w_in")], [False], "gather_start_w_in", peers=[SAME_CORE])
    gather = dict(w_in=first)

    def gathered(names, after):
        return _exchange_wait([gather[n] for n in names], after, "gather_wait_" + names[0])

    def weight(name, after):
        if name == "w_in":
            handles, token = _exchange_start([cast(n) for n in big[1:]], [False] * len(big[1:]), "gather_start", after=after)
            gather.update(zip(big[1:], handles))
            forward = _forward_start(gathered(["w_in"], token)[0], "gather_forward_start")
            full = _forward_wait(forward, after, "gather_forward_wait").reshape(IN_W, D_MODEL)
            f0 = QKV_W + N_HEADS
            return jnp.concatenate([full[:QKV_W], full[f0:], full[QKV_W:f0],
                                    jnp.zeros((IN_PAD - IN_W, D_MODEL), BF16)], axis=0)
        if name == "w_out":
            return gathered(["w_out"], after)[0].reshape(D_MODEL, D_MODEL)
        if name == "w_gate_up":
            return [g.reshape(D_FF, D_MODEL) for g in gathered(["w_gate", "w_up"], after)]
        return gathered(["w_down"], after)[0].reshape(D_FF, D_MODEL)

    rows = lambda g: g if g.ndim == 3 else g.reshape(N_DEV, g.shape[0] // N_DEV, g.shape[1])
    sent = {}

    def emit(names, grads):
        handles, token = _exchange_start([rows(g) for g in grads], [True] * len(names), "grads_start_" + names[0])
        sent.update(zip(names, handles))
        return token

    loss_row, dx, small_grads = _local_step(x[0], loss_target[0], w, weight, emit, started)

    packed = _pack_rows(dict(small_grads, loss=0.5 / D_MODEL * jnp.sum(loss_row)))
    small_handles, after = _exchange_start([packed, small_grads["w_pool"]], [False, False], "grads_start_replicated")

    outs = {}
    for name in ("w_down", "w_gate", "w_up", "w_out", "w_in"):
        (parts,) = _exchange_wait([sent[name]], after, "grads_wait_" + name)
        if name == "w_in":
            dense = lambda d: d[name].transpose(2, 0, 1).reshape(-1, LANES)
            outs[name] = _adamw_dense(parts, dense(w), dense(m), dense(v), "adamw_" + name, rows=IN_SHARD, shift=IN_SHIFT)
            after = outs[name][0]
            outs[name] = [a.reshape(-1, D_MODEL // LANES, LANES).transpose(1, 2, 0).reshape(1, D_MODEL, -1)
                          for a in outs[name]]
            continue
        outs[name] = _adamw(parts, shard(w, name), shard(m, name), shard(v, name), "adamw_" + name)
        after = outs[name][0]
        outs[name] = [(a.T if name in flipped else a)[None] for a in outs[name]]
    parts_rows, parts_pool = _exchange_wait(small_handles, after, "grads_wait_replicated")
    loss, small = _adamw_replicated(parts_rows, parts_pool, w, m, v)
    outs.update(small)

    return (loss.reshape(()), dx[None]) + tuple(outs[n][k] for k in range(4) for n in order)
```

```python
import jax
import jax.numpy as jnp
from jax import lax
from jax.experimental import pallas as pl
from jax.experimental.pallas import tpu as pltpu

F32 = jnp.float32
BF16 = jnp.bfloat16
SDS = jax.ShapeDtypeStruct

D_MODEL = 1024
ATTN_W = 512
N_HEADS = 8
HEAD_DIM = 64
Q_SCALE = HEAD_DIM ** -0.5
N_PAIRS = N_HEADS // 2
POOL_W = 512
POOL_WINDOWS = (2, 4, 8, 16)
POOL_G = 128
HALO = 16
IN_W = 3 * ATTN_W + N_HEADS + POOL_W
QKV_W = 3 * ATTN_W
U_OFF = QKV_W
F_OFF = QKV_W + POOL_W
IN_PAD = F_OFF + 128
D_FF = 2816
EPS = 1e-6
NEG = -1e30
N_DEV = 8
LANES = 128
BF16_ROWS = 16

IN_SHARD = IN_W // N_DEV
IN_STEP = IN_SHARD // BF16_ROWS * BF16_ROWS
IN_SHIFT = IN_SHARD - IN_STEP
IN_WINDOW = -(-((N_DEV - 1) * IN_SHIFT + IN_SHARD) // BF16_ROWS) * BF16_ROWS

ADAM_LR = 0.001
ADAM_B1 = 0.9
ADAM_B2 = 0.999
ADAM_EPS = 1e-08
ADAM_WD = 0.01
ADAM_STEP = 10

VMEM_LIMIT_BYTES = 56 * 1024 * 1024
MESH = pl.DeviceIdType.MESH
NT = (((1,), (1,)), ((), ()))
TN = (((0,), (0,)), ((), ()))


_UNREAD = pl.BlockSpec(memory_space=pl.ANY)


def _cparams(*sem):
    return pltpu.CompilerParams(dimension_semantics=sem or None, vmem_limit_bytes=VMEM_LIMIT_BYTES)


def _split3(a):
    hi = a.astype(BF16)
    r1 = a - hi.astype(F32)
    mid = r1.astype(BF16)
    lo = (r1 - mid.astype(F32)).astype(BF16)
    return hi, mid, lo


def _dot_sel(a, sel, dims=None):
    sb = sel.astype(BF16)
    if dims is None:
        return sum(jnp.dot(p, sb, preferred_element_type=F32) for p in _split3(a))
    return sum(lax.dot_general(p, sb, dims, preferred_element_type=F32) for p in _split3(a))


def _sel_dot(sel, a, dims=None):
    sb = sel.astype(BF16)
    if dims is None:
        return sum(jnp.dot(sb, p, preferred_element_type=F32) for p in _split3(a))
    return sum(lax.dot_general(sb, p, dims, preferred_element_type=F32) for p in _split3(a))


def _iota2(shape, dim):
    return lax.broadcasted_iota(jnp.int32, shape, dim)


UNROLLS = (8, 4, 2)


def _shift_div(x, n):
    return lax.shift_right_logical(x, n.bit_length() - 1)


def _norm1(x, g1, after, *, tm):
    s = x.shape[0]

    def body(x_ref, g_ref, _, h_ref, r_ref):
        xv = x_ref[...]
        r = lax.rsqrt(jnp.mean(xv * xv, axis=-1, keepdims=True) + EPS)
        h_ref[...] = (xv * r * g_ref[...]).astype(BF16)
        r_ref[...] = r

    row = lambda w: pl.BlockSpec((tm, w), lambda i: (i, 0))
    return pl.pallas_call(
        body,
        grid=(s // tm,),
        in_specs=[row(D_MODEL), pl.BlockSpec((1, D_MODEL), lambda i: (0, 0)), _UNREAD],
        out_specs=[row(D_MODEL), row(1)],
        out_shape=[SDS((s, D_MODEL), BF16), SDS((s, 1), F32)],
        compiler_params=_cparams("arbitrary"),
        name="norm1",
    )(x, g1, after)


def _in_proj_pool(h, w_in_t, w_pool, pool_scale, *, tm):
    s = h.shape[0]

    def body(h_ref, w_ref, wp_ref, sc_ref, qkv_ref, fl_ref, pooled_ref, po_ref, tail_ref):
        i = pl.program_id(0)

        @pl.when(i == 0)
        def _():
            tail_ref[...] = jnp.zeros_like(tail_ref)

        hv = h_ref[...]
        uv = lax.dot_general(hv, w_ref[U_OFF:F_OFF, :], NT, preferred_element_type=F32)
        qkv_ref[...] = lax.dot_general(hv, w_ref[0:QKV_W, :], NT, preferred_element_type=F32).astype(BF16)
        fl_ref[...] = lax.dot_general(hv, w_ref[F_OFF:IN_PAD, :], NT, preferred_element_type=F32)
        ext = jnp.concatenate([tail_ref[...], uv], axis=0)
        tail_ref[...] = uv[tm - HALO:, :]
        for g, w in enumerate(POOL_WINDOWS):
            cols = slice(g * POOL_G, (g + 1) * POOL_G)
            acc = ext[:, cols]
            k = 1
            while k < w:
                acc = acc + pltpu.roll(acc, k, axis=0)
                k *= 2
            pooled = (acc[HALO:, :] / _pool_counts(i * tm, tm, w) - uv[:, cols]).astype(BF16)
            pooled_ref[:, cols] = pooled
            mixed = jnp.dot(pooled, wp_ref[g].astype(BF16), preferred_element_type=F32)
            po_ref[:, cols] = (mixed * sc_ref[:, cols]).astype(BF16)

    row = lambda w: pl.BlockSpec((tm, w), lambda i: (i, 0))
    return pl.pallas_call(
        body,
        grid=(s // tm,),
        in_specs=[row(D_MODEL), pl.BlockSpec((IN_PAD, D_MODEL), lambda i: (0, 0)),
                  pl.BlockSpec((len(POOL_WINDOWS), POOL_G, POOL_G), lambda i: (0, 0, 0)),
                  pl.BlockSpec((1, POOL_W), lambda i: (0, 0))],
        out_specs=[row(QKV_W), row(LANES), row(POOL_W), row(POOL_W)],
        out_shape=[SDS((s, QKV_W), BF16), SDS((s, LANES), F32), SDS((s, POOL_W), BF16), SDS((s, POOL_W), BF16)],
        scratch_shapes=[pltpu.VMEM((HALO, POOL_W), F32)],
        compiler_params=_cparams("arbitrary"),
        name="in_proj_pool",
    )(h, w_in_t, w_pool, pool_scale)


def _head_block_masks(rows, nb):
    shift = nb.bit_length() - 1
    rr, cc = _iota2((rows, rows), 0), _iota2((rows, rows), 1)
    same = lax.shift_right_logical(rr, shift) == lax.shift_right_logical(cc, shift)
    return rr, cc, same


def _forget_cumsum(fl_t, b_rows):
    rows = fl_t.shape[0]
    nb = rows // N_HEADS

    def body(fl_ref, b_ref, c_ref):
        z = fl_ref[...] + b_ref[...]
        lf = jnp.minimum(z, 0.0) - jnp.log1p(jnp.exp(-jnp.abs(z)))
        upper = _iota2((LANES, LANES), 0) <= _iota2((LANES, LANES), 1)
        within = _dot_sel(lf, upper)
        tot = _dot_sel(lf, jnp.ones((LANES, LANES), F32))
        rr, cc, same = _head_block_masks(rows, nb)
        c_ref[...] = within + _sel_dot(same & (cc < rr), tot)

    return pl.pallas_call(body, out_shape=SDS(fl_t.shape, F32), compiler_params=_cparams(), name="forget_cumsum")(
        fl_t, b_rows)


BIAS_LANES = 3


def _augment(t, h, bias, col_first):
    n = t.shape[0]
    lane = _iota2((n, LANES), 1)
    own = (lane < HEAD_DIM) if h == 0 else (lane >= HEAD_DIM)
    b0 = HEAD_DIM if h == 0 else 0
    c0, o0 = (b0, b0 + BIAS_LANES) if col_first else (b0 + BIAS_LANES, b0)
    x = 0.0
    if bias is not None:
        row = _iota2((BF16_ROWS, n), 0)
        pieces = jnp.zeros((BF16_ROWS, n), F32)
        for off, piece in enumerate(_split3(bias)):
            pieces = jnp.where(row == off, piece.astype(F32), pieces)
        r, ln = _iota2((BF16_ROWS, LANES), 0), _iota2((BF16_ROWS, LANES), 1)
        place = jnp.where(r < BIAS_LANES, jnp.where(ln == c0 + r, 1.0, 0.0), 0.0).astype(BF16)
        x = lax.dot_general(pieces.astype(BF16), place, TN, preferred_element_type=F32)
    x = jnp.where(own, t, x)
    x = jnp.where((lane >= o0) & (lane < o0 + BIAS_LANES), 1.0, x)
    return x.astype(BF16)


def _attn_fwd(qkv, c_rows, *, tk):
    s = qkv.shape[0]
    tq = 2 * tk
    nb = s // tk

    def body(q_ref, k_ref, v_ref, cq_ref, ck_ref, o_ref, lse_ref, kp_ref, vt_ref, st_ref):
        i = pl.program_id(1)

        @pl.when(i == 0)
        def _():
            def prep(jb, _):
                st = pl.multiple_of(jb * tk, tk)
                k2 = k_ref[pl.ds(st, tk), :].astype(F32)
                ck = ck_ref[:, pl.ds(st, tk)]
                for h in range(2):
                    kp_ref[h * nb + jb] = _augment(k2, h, -ck[h:h + 1, :], True)
                vt_ref[jb] = v_ref[pl.ds(st, tk), :].astype(F32).T.astype(BF16)
                return 0

            lax.fori_loop(0, nb, prep, 0)

        qs = q_ref[...].astype(F32) * Q_SCALE
        cq = cq_ref[...]
        qp = [_augment(qs, h, cq[h:h + 1, :], False) for h in range(2)]

        def logits(j):
            return tuple(lax.dot_general(kp_ref[h * nb + j], qp[h], NT, preferred_element_type=F32) for h in range(2))

        def softmax_pv(j, slot, stats, masked):
            out = []
            for h in range(2):
                m, l, acc = stats[h]
                st = st_ref[2 * slot + h]
                if masked:
                    st = jnp.where(j * tk + _iota2((tk, tq), 0) <= i * tq + _iota2((tk, tq), 1), st, NEG)
                m_new = jnp.maximum(m, jnp.max(st, axis=0, keepdims=True))
                alpha = jnp.exp(m - m_new)
                p = jnp.exp(st - m_new)
                l = alpha * l + jnp.sum(p, axis=0, keepdims=True)
                vt = vt_ref[j, h * HEAD_DIM:(h + 1) * HEAD_DIM, :]
                acc = alpha * acc + jnp.dot(vt, p.astype(BF16), preferred_element_type=F32)
                out.append((m_new, l, acc))
            return tuple(out)

        def put(slot, j):
            for h, st in enumerate(logits(j)):
                st_ref[2 * slot + h] = st

        def run(j0, steps, stats):
            for d in range(steps):
                put(1 - d % 2, j0 + d + 1)
                stats = softmax_pv(j0 + d, d % 2, stats, False)
            return stats

        init = tuple((jnp.full((1, tq), NEG, F32), jnp.zeros((1, tq), F32), jnp.zeros((HEAD_DIM, tq), F32))
                     for _ in range(2))
        put(0, 0)
        first, left, stats = 0, 2 * i, init
        for size in UNROLLS:
            trips = _shift_div(left, size)
            stats = lax.fori_loop(0, trips, lambda t, st, j0=first, n=size: run(j0 + n * t, n, st), stats)
            first, left = first + size * trips, left - size * trips
        put(1, 2 * i + 1)
        stats = softmax_pv(2 * i, 0, stats, True)
        (ma, la, acca), (mb, lb, accb) = softmax_pv(2 * i + 1, 1, stats, True)
        o_ref[...] = jnp.concatenate([acca / la, accb / lb], axis=0).T.astype(BF16)
        lse_ref[...] = jnp.where(_iota2((2, tq), 0) == 0, ma + jnp.log(la), mb + jnp.log(lb))

    return pl.pallas_call(
        body,
        grid=(N_PAIRS, s // tq),
        in_specs=[
            pl.BlockSpec((tq, LANES), lambda p, i: (i, p)),
            pl.BlockSpec((s, LANES), lambda p, i: (0, N_PAIRS + p)),
            pl.BlockSpec((s, LANES), lambda p, i: (0, 2 * N_PAIRS + p)),
            pl.BlockSpec((None, 2, tq), lambda p, i: (p, 0, i)),
            pl.BlockSpec((None, 2, s), lambda p, i: (p, 0, 0)),
        ],
        out_specs=[
            pl.BlockSpec((tq, LANES), lambda p, i: (i, p)),
            pl.BlockSpec((None, None, 2, tq), lambda p, i: (p, i, 0, 0)),
        ],
        out_shape=[SDS((s, ATTN_W), BF16), SDS((N_PAIRS, s // tq, 2, tq), F32)],
        scratch_shapes=[pltpu.VMEM((2 * nb, tk, LANES), BF16), pltpu.VMEM((nb, LANES, tk), BF16),
                        pltpu.VMEM((4, tk, tq), F32)],
        compiler_params=_cparams("arbitrary", "arbitrary"),
        name="attn_fwd",
    )(qkv, qkv, qkv, c_rows, c_rows)


def _pool_counts(row0, tm, w):
    t = row0 + _iota2((tm, 1), 0)
    return jnp.minimum(t + 1, w).astype(F32)


def _out_gate_up(attn_o, pool_o, w_out, x, g2, wg_t, wu_t, *, tm):
    s = x.shape[0]

    def body(a_ref, p_ref, wo_ref, x_ref, g_ref, wg_ref, wu_ref, x1_ref, h2_ref, r_ref, gate_ref, up_ref, act_ref):
        x1 = (x_ref[...] + jnp.dot(a_ref[...], wo_ref[0:ATTN_W, :], preferred_element_type=F32)
              + jnp.dot(p_ref[...], wo_ref[ATTN_W:, :], preferred_element_type=F32))
        r = lax.rsqrt(jnp.mean(x1 * x1, axis=-1, keepdims=True) + EPS)
        x1_ref[...] = x1
        r_ref[...] = r
        h2 = (x1 * r * g_ref[...]).astype(BF16)
        h2_ref[...] = h2
        gate = lax.dot_general(h2, wg_ref[...], NT, preferred_element_type=F32)
        up = lax.dot_general(h2, wu_ref[...], NT, preferred_element_type=F32)
        gate_ref[...] = gate.astype(BF16)
        up_ref[...] = up.astype(BF16)
        act_ref[...] = (gate * jax.nn.sigmoid(gate) * up).astype(BF16)

    row = lambda w: pl.BlockSpec((tm, w), lambda i: (i, 0))
    full = lambda a, b: pl.BlockSpec((a, b), lambda i: (0, 0))
    return pl.pallas_call(
        body,
        grid=(s // tm,),
        in_specs=[row(ATTN_W), row(POOL_W), full(D_MODEL, D_MODEL), row(D_MODEL), full(1, D_MODEL),
                  full(D_FF, D_MODEL), full(D_FF, D_MODEL)],
        out_specs=[row(D_MODEL), row(D_MODEL), row(1), row(D_FF), row(D_FF), row(D_FF)],
        out_shape=[SDS((s, D_MODEL), F32), SDS((s, D_MODEL), BF16), SDS((s, 1), F32), SDS((s, D_FF), BF16),
                   SDS((s, D_FF), BF16), SDS((s, D_FF), BF16)],
        compiler_params=_cparams("arbitrary"),
        name="out_gate_up",
    )(attn_o, pool_o, w_out, x, g2, wg_t, wu_t)


def _staggered(n, start, finish):
    pending = start(0)
    for k in range(n):
        following = start(k + 1) if k + 1 < n else None
        finish(k, pending)
        pending = following


def _down_final(act, wd, x1, gf, tgt, *, tm, sub):
    s = x1.shape[0]

    def body(a_ref, w_ref, x1_ref, g_ref, t_ref, dx2_ref, loss_ref, dgf_ref):
        @pl.when(pl.program_id(0) == 0)
        def _():
            loss_ref[...] = jnp.zeros_like(loss_ref)
            dgf_ref[...] = jnp.zeros_like(dgf_ref)

        g = g_ref[...]

        def matmul(k):
            return jnp.dot(a_ref[k * sub:(k + 1) * sub, :], w_ref[...], preferred_element_type=F32)

        def rest(k, mm):
            rows = slice(k * sub, (k + 1) * sub)
            x2 = x1_ref[rows, :] + mm
            r = lax.rsqrt(jnp.mean(x2 * x2, axis=-1, keepdims=True) + EPS)
            xn = x2 * r
            diff = xn * g - t_ref[rows, :]
            loss_ref[...] += jnp.sum(diff * diff, axis=0, keepdims=True)
            dy = diff * (1.0 / D_MODEL)
            dgf_ref[...] += jnp.sum(dy * xn, axis=0, keepdims=True)
            dxn = dy * g
            dx2_ref[rows, :] = r * (dxn - xn * jnp.mean(dxn * xn, axis=-1, keepdims=True))

        _staggered(tm // sub, matmul, rest)

    row = lambda w: pl.BlockSpec((tm, w), lambda i: (i, 0))
    full = lambda a, b: pl.BlockSpec((a, b), lambda i: (0, 0))
    return pl.pallas_call(
        body,
        grid=(s // tm,),
        in_specs=[row(D_FF), full(D_FF, D_MODEL), row(D_MODEL), full(1, D_MODEL), row(D_MODEL)],
        out_specs=[row(D_MODEL), full(1, D_MODEL), full(1, D_MODEL)],
        out_shape=[SDS((s, D_MODEL), F32), SDS((1, D_MODEL), F32), SDS((1, D_MODEL), F32)],
        compiler_params=_cparams("arbitrary"),
        name="down_final",
    )(act, wd, x1, gf, tgt)


def _swiglu_bwd(dx2, wd, gate, up, *, tm, tn):
    s = dx2.shape[0]

    def body(d_ref, w_ref, gate_ref, up_ref, dgate_ref, dup_ref):
        dact = lax.dot_general(d_ref[...].astype(BF16), w_ref[...], NT, preferred_element_type=F32)
        gate = gate_ref[...].astype(F32)
        sg = jax.nn.sigmoid(gate)
        dup_ref[...] = (dact * (gate * sg)).astype(BF16)
        dgate_ref[...] = (dact * up_ref[...].astype(F32) * (sg * (1.0 + gate * (1.0 - sg)))).astype(BF16)

    ospec = pl.BlockSpec((tm, tn), lambda c, r: (r, c))
    return pl.pallas_call(
        body,
        grid=(D_FF // tn, s // tm),
        in_specs=[pl.BlockSpec((tm, D_MODEL), lambda c, r: (r, 0)), pl.BlockSpec((tn, D_MODEL), lambda c, r: (c, 0)),
                  ospec, ospec],
        out_specs=[ospec, ospec],
        out_shape=[SDS((s, D_FF), BF16), SDS((s, D_FF), BF16)],
        compiler_params=_cparams("arbitrary", "arbitrary"),
        name="swiglu_bwd",
    )(dx2, wd, gate, up)


def _mm_tn_stacked(as_, rows, b, *, ts, name, windows=None):
    s, nb_ = b.shape
    n = len(as_)
    offsets = [sum(rows[:i]) for i in range(n)]
    total = sum(rows)
    if windows is None:
        acc_rows, out_shape = total, (total, nb_)
    else:
        count, step, size = windows
        acc_rows, out_shape = max(total, (count - 1) * step + size), (count, size, nb_)

    def body(*refs):
        a_refs, b_ref, o_ref, acc_ref = refs[:n], refs[n], refs[n + 1], refs[n + 2]
        k = pl.program_id(0)

        @pl.when(k == 0)
        def _():
            acc_ref[...] = jnp.zeros_like(acc_ref)

        bv = b_ref[...].astype(BF16)
        for a_ref, off, cnt in zip(a_refs, offsets, rows):
            part = lax.dot_general(a_ref[...].astype(BF16), bv, TN, preferred_element_type=F32)
            acc_ref[off:off + cnt, :] += part[0:cnt, :]

        @pl.when(k == s // ts - 1)
        def _():
            if windows is None:
                o_ref[...] = acc_ref[...].astype(BF16)
            else:
                for d in range(count):
                    o_ref[d] = acc_ref[d * step:d * step + size, :].astype(BF16)

    return pl.pallas_call(
        body,
        grid=(s // ts,),
        in_specs=[pl.BlockSpec((ts, a.shape[1]), lambda k: (k, 0)) for a in as_] + [pl.BlockSpec((ts, nb_), lambda k: (k, 0))],
        out_specs=pl.BlockSpec(out_shape, lambda k: (0,) * len(out_shape)),
        out_shape=SDS(out_shape, BF16),
        scratch_shapes=[pltpu.VMEM((acc_rows, nb_), F32)],
        compiler_params=_cparams("arbitrary"),
        name=name,
    )(*as_, b)


def _norm_bwd(dh, x, r, g, dres):
    xn = x * r
    dxn = dh * g
    dx = dres + r * (dxn - xn * jnp.mean(dxn * xn, axis=-1, keepdims=True))
    return dx, jnp.sum(dh * xn, axis=0, keepdims=True)


def _mlp_in_pool_bwd(dgate, dup, wg_t, wu_t, w_out, x1, r2, g2, dx2, pooled, w_pool, pool_scale, *, tm):
    s = x1.shape[0]
    nt = s // tm
    ng = len(POOL_WINDOWS)

    def body(dg_ref, dup_ref, wg_ref, wu_ref, wo_ref, x_ref, r_ref, g_ref, d_ref, p_ref, w_ref, sc_ref,
             dx1_ref, dattn_ref, du_ref, dg2_ref, dw_ref, dsc_ref, head_ref):
        i = pl.program_id(0)

        @pl.when(i == 0)
        def _():
            dg2_ref[...] = jnp.zeros_like(dg2_ref)
            head_ref[...] = jnp.zeros_like(head_ref)
            dw_ref[...] = jnp.zeros_like(dw_ref)
            dsc_ref[...] = jnp.zeros_like(dsc_ref)

        dh2 = (jnp.dot(dg_ref[...], wg_ref[...], preferred_element_type=F32)
               + jnp.dot(dup_ref[...], wu_ref[...], preferred_element_type=F32))
        dx1, dg2 = _norm_bwd(dh2, x_ref[...], r_ref[...], g_ref[...], d_ref[...])
        dg2_ref[...] += dg2
        dx1_ref[...] = dx1
        dmix = lax.dot_general(dx1.astype(BF16), wo_ref[...], NT, preferred_element_type=F32)
        dattn_ref[...] = dmix[:, 0:ATTN_W]
        row0 = (nt - 1 - i) * tm
        for g, w in enumerate(POOL_WINDOWS):
            cols = slice(g * POOL_G, (g + 1) * POOL_G)
            wb = w_ref[g].astype(BF16)
            pooled_g = p_ref[:, cols]
            dpo = dmix[:, ATTN_W + g * POOL_G:ATTN_W + (g + 1) * POOL_G]
            mixed = jnp.dot(pooled_g, wb, preferred_element_type=F32)
            dsc_ref[:, cols] += jnp.sum(dpo * mixed, axis=0, keepdims=True)
            dmp = (dpo * sc_ref[:, cols]).astype(BF16)
            dw_ref[g] += lax.dot_general(pooled_g, dmp, TN, preferred_element_type=F32)
            dpooled = lax.dot_general(dmp, wb, NT, preferred_element_type=F32)
            a = dpooled / _pool_counts(row0, tm, w)
            acc = jnp.concatenate([a, head_ref[:, cols]], axis=0)
            head_ref[:, cols] = a[0:HALO, :]
            k = 1
            while k < w:
                acc = acc + pltpu.roll(acc, tm + HALO - k, axis=0)
                k *= 2
            du_ref[:, cols] = (acc[0:tm, :] - dpooled).astype(BF16)

    row = lambda w: pl.BlockSpec((tm, w), lambda i: (nt - 1 - i, 0))
    full = lambda a, b: pl.BlockSpec((a, b), lambda i: (0, 0))
    pool_w = pl.BlockSpec((ng, POOL_G, POOL_G), lambda i: (0, 0, 0))
    return pl.pallas_call(
        body,
        grid=(nt,),
        in_specs=[row(D_FF), row(D_FF), full(D_FF, D_MODEL), full(D_FF, D_MODEL), full(D_MODEL, D_MODEL),
                  row(D_MODEL), row(1), full(1, D_MODEL), row(D_MODEL), row(POOL_W), pool_w, full(1, POOL_W)],
        out_specs=[row(D_MODEL), row(ATTN_W), row(POOL_W), full(1, D_MODEL), pool_w, full(1, POOL_W)],
        out_shape=[SDS((s, D_MODEL), F32), SDS((s, ATTN_W), F32), SDS((s, POOL_W), BF16), SDS((1, D_MODEL), F32),
                   SDS((ng, POOL_G, POOL_G), F32), SDS((1, POOL_W), F32)],
        scratch_shapes=[pltpu.VMEM((HALO, POOL_W), F32)],
        compiler_params=_cparams("arbitrary"),
        name="mlp_in_pool_bwd",
    )(dgate, dup, wg_t, wu_t, w_out, x1, r2, g2, dx2, pooled, w_pool, pool_scale)


SUM_ROWS = 16


def _heads_t(t):
    n = t.shape[0]
    lane = _iota2((n, LANES), 1)
    tf = t.astype(F32)
    halves = jnp.concatenate([jnp.where(lane < HEAD_DIM, tf, 0.0).T, jnp.where(lane < HEAD_DIM, 0.0, tf).T], axis=1)
    r, c = _iota2((SUM_ROWS, 2 * n), 0), _iota2((SUM_ROWS, 2 * n), 1)
    ones = jnp.where(((r == 0) & (c < n)) | ((r == 4) & (c >= n)), 1.0, 0.0)
    return jnp.concatenate([halves, ones], axis=0).astype(BF16)


def _attn_bwd(qkv, attn_o, d_attn, rowb, c_rows, after, *, tq):
    s = qkv.shape[0]
    tk = tq
    nb = s // tq
    rows_t = LANES + SUM_ROWS

    def body(q_ref, k_ref, v_ref, o_ref, do_ref, rowb_ref, ck_ref, _, dq_ref, dk_ref, dv_ref, dck_ref, dcq_ref,
             dqt_ref, delta_ref, kp_ref, qp_ref, dob_ref, qt_ref, kt_ref, dot_ref, front_ref):
        lane = _iota2((tq, LANES), 1)
        lo = lane < HEAD_DIM
        first = _iota2((8, LANES), 1) < HEAD_DIM
        sel = jnp.where(_iota2((8, LANES), 0) < 4, jnp.where(first, 1.0, 0.0), jnp.where(first, 0.0, 1.0))

        def prep(b, _):
            st = pl.multiple_of(b * tq, tq)
            do2 = do_ref[pl.ds(st, tq), :]
            delta_ref[b] = _sel_dot(sel, do2 * o_ref[pl.ds(st, tq), :].astype(F32), NT)
            dob_ref[pl.ds(st, tq), :] = do2.astype(BF16)
            dqt_ref[b] = jnp.zeros((rows_t, tq), F32)
            k2 = k_ref[pl.ds(st, tq), :].astype(F32)
            q2 = q_ref[pl.ds(st, tq), :].astype(F32)
            ck = ck_ref[:, pl.ds(st, tq)]
            for h in range(2):
                kp_ref[h * nb + b] = _augment(k2, h, -ck[h:h + 1, :], True)
                qp_ref[h * nb + b] = _augment(q2 * Q_SCALE, h, None, False)
            qt_ref[b] = _heads_t(q2)
            kt_ref[b] = _heads_t(k2)
            dot_ref[b] = _heads_t(do2)[0:LANES, :]
            return 0

        lax.fori_loop(0, nb, prep, 0)

        def split(t):
            z = jnp.zeros_like(t)
            return jnp.where(lo, t, z), jnp.where(lo, z, t)

        def kv_block(j, _):
            st_j = pl.multiple_of(j * tk, tk)
            vs = split(v_ref[pl.ds(st_j, tk), :])
            kt = kt_ref[j]

            def stage(i, slot):
                ic = jnp.minimum(i, nb - 1)
                do2 = dob_ref[pl.ds(pl.multiple_of(ic * tq, tq), tq), :]
                for h in range(2):
                    front_ref[4 * slot + h] = lax.dot_general(kp_ref[h * nb + j], qp_ref[h * nb + ic], NT,
                                                              preferred_element_type=F32)
                    front_ref[4 * slot + 2 + h] = lax.dot_general(vs[h], do2, NT, preferred_element_type=F32)

            def q_block(i, slot, carry, diagonal):
                dkt, dvt = carry
                ic = jnp.minimum(i, nb - 1)
                rb = rowb_ref[ic] + jnp.where(i < nb, 0.0, NEG)
                dl = delta_ref[ic]
                pts, dsts = [], []
                for h in range(2):
                    st = front_ref[4 * slot + h] + rb[h:h + 1, :]
                    if diagonal:
                        st = jnp.where(_iota2((tk, tq), 0) <= _iota2((tk, tq), 1), st, NEG)
                    pt = jnp.exp(st)
                    pts.append(pt.astype(BF16))
                    dsts.append((pt * (front_ref[4 * slot + 2 + h] - dl[4 * h:4 * h + 1, :])).astype(BF16))
                dvt = dvt + lax.dot_general(dot_ref[ic], jnp.concatenate(pts, axis=1), NT, preferred_element_type=F32)
                dkt = dkt + lax.dot_general(qt_ref[ic], jnp.concatenate(dsts, axis=1), NT, preferred_element_type=F32)
                dqt_ref[ic] += jnp.dot(kt, jnp.concatenate(dsts, axis=0), preferred_element_type=F32)
                return dkt, dvt

            def run(i0, steps, carry):
                for d in range(steps):
                    stage(i0 + d + 1, d % 2)
                    carry = q_block(i0 + d, 1 - d % 2, carry, False)
                return carry

            stage(j, 0)
            stage(j + 1, 1)
            carry = q_block(j, 0, (jnp.zeros((rows_t, tk), F32), jnp.zeros((LANES, tk), F32)), True)
            first, left = j + 1, nb - 1 - j
            for size in UNROLLS:
                trips = _shift_div(left + 1 if size == UNROLLS[-1] else left, size)
                carry = lax.fori_loop(0, trips, lambda t, c, i0=first, n=size: run(i0 + n * t, n, c), carry)
                first, left = first + size * trips, left - size * trips
            dkt, dvt = carry
            dk_ref[pl.ds(st_j, tk), :] = (dkt[0:LANES, :].T * Q_SCALE).astype(BF16)
            dv_ref[pl.ds(st_j, tk), :] = dvt.T.astype(BF16)
            dck_ref[j] = dkt[LANES:LANES + 8, :]
            return 0

        lax.fori_loop(0, nb, kv_block, 0)

        def finish(b, _):
            acc = dqt_ref[b]
            dq_ref[pl.ds(pl.multiple_of(b * tq, tq), tq), :] = (acc[0:LANES, :].T * Q_SCALE).astype(BF16)
            dcq_ref[b] = acc[LANES:LANES + 8, :]
            return 0

        lax.fori_loop(0, nb, finish, 0)

    col = lambda off: pl.BlockSpec((s, LANES), lambda p: (0, off + p))
    sums = pl.BlockSpec((None, nb, 8, tq), lambda p: (p, 0, 0, 0))
    return pl.pallas_call(
        body,
        grid=(N_PAIRS,),
        in_specs=[col(0), col(N_PAIRS), col(2 * N_PAIRS), col(0), col(0),
                  pl.BlockSpec((None, nb, 2, tq), lambda p: (p, 0, 0, 0)),
                  pl.BlockSpec((None, 2, s), lambda p: (p, 0, 0)), _UNREAD],
        out_specs=[col(0), col(0), col(0), sums, sums],
        out_shape=[SDS((s, ATTN_W), BF16), SDS((s, ATTN_W), BF16), SDS((s, ATTN_W), BF16),
                   SDS((N_PAIRS, nb, 8, tq), F32), SDS((N_PAIRS, nb, 8, tq), F32)],
        scratch_shapes=[pltpu.VMEM((nb, rows_t, tq), F32), pltpu.VMEM((nb, 8, tq), F32),
                        pltpu.VMEM((2 * nb, tk, LANES), BF16), pltpu.VMEM((2 * nb, tq, LANES), BF16),
                        pltpu.VMEM((s, LANES), BF16), pltpu.VMEM((nb, rows_t, 2 * tq), BF16),
                        pltpu.VMEM((nb, rows_t, 2 * tk), BF16), pltpu.VMEM((nb, LANES, 2 * tq), BF16),
                        pltpu.VMEM((8, tk, tq), F32)],
        compiler_params=_cparams("arbitrary"),
        name="attn_bwd",
    )(qkv, qkv, qkv, attn_o, d_attn, rowb, c_rows, after)


def _forget_bwd(dc_t, fl_t, b_rows):
    rows = fl_t.shape[0]
    nb = rows // N_HEADS

    def body(dc_ref, fl_ref, b_ref, dfl_ref, db_ref):
        dc = dc_ref[...]
        lower = _iota2((LANES, LANES), 0) >= _iota2((LANES, LANES), 1)
        ones = jnp.ones((LANES, LANES), F32)
        rr, cc, same = _head_block_masks(rows, nb)
        dlf = _dot_sel(dc, lower) + _sel_dot(same & (cc > rr), _dot_sel(dc, ones))
        dfl = dlf / (1.0 + jnp.exp(fl_ref[...] + b_ref[...]))
        dfl_ref[...] = dfl
        shift = nb.bit_length() - 1
        hsel = lax.shift_right_logical(_iota2((N_HEADS, rows), 1), shift) == _iota2((N_HEADS, rows), 0)
        db_ref[...] = _sel_dot(hsel, _dot_sel(dfl, ones))

    return pl.pallas_call(body, out_shape=[SDS(fl_t.shape, F32), SDS((N_HEADS, LANES), F32)],
                          compiler_params=_cparams(), name="forget_bwd")(dc_t, fl_t, b_rows)


def _in_bwd(dq, dk, dv, du, dfl, w_in_t, x, r1, g1, dx1, after, *, tm):
    s = x.shape[0]
    pieces = ((0, ATTN_W), (ATTN_W, 2 * ATTN_W), (2 * ATTN_W, QKV_W), (U_OFF, F_OFF), (F_OFF, IN_PAD))

    def body(dq_ref, dk_ref, dv_ref, du_ref, df_ref, w_ref, x_ref, r_ref, g_ref, d_ref, _, dx_ref, dg1_ref):
        @pl.when(pl.program_id(0) == 0)
        def _():
            dg1_ref[...] = jnp.zeros_like(dg1_ref)

        dh = None
        for ref, (c0, c1) in zip((dq_ref, dk_ref, dv_ref, du_ref, df_ref), pieces):
            t = jnp.dot(ref[...], w_ref[c0:c1, :], preferred_element_type=F32)
            dh = t if dh is None else dh + t
        dx, dg1 = _norm_bwd(dh, x_ref[...], r_ref[...], g_ref[...], d_ref[...])
        dx_ref[...] = dx
        dg1_ref[...] += dg1

    row = lambda w: pl.BlockSpec((tm, w), lambda i: (i, 0))
    full = lambda a, b: pl.BlockSpec((a, b), lambda i: (0, 0))
    return pl.pallas_call(
        body,
        grid=(s // tm,),
        in_specs=[row(ATTN_W), row(ATTN_W), row(ATTN_W), row(POOL_W), row(LANES), full(IN_PAD, D_MODEL),
                  row(D_MODEL), row(1), full(1, D_MODEL), row(D_MODEL), _UNREAD],
        out_specs=[row(D_MODEL), full(1, D_MODEL)],
        out_shape=[SDS((s, D_MODEL), F32), SDS((1, D_MODEL), F32)],
        compiler_params=_cparams("arbitrary"),
        name="in_bwd",
    )(dq, dk, dv, du, dfl, w_in_t, x, r1, g1, dx1, after)


def _tiles(s):
    big = min(512, s)
    return dict(row=big, attn=min(256, s // 2), ff_rows=min(256, s), tall=min(1024, s))


def _local_step(x, tgt, p, weight, emit, started):
    s = x.shape[0]
    t = _tiles(s)
    tm, tq = t["row"], t["attn"]
    nb = s // LANES
    nqb = s // tq
    g1, g2, gf = p["norm1_g"], p["norm2_g"], p["final_g"].reshape(1, D_MODEL)
    w_pool, pool_scale = p["w_pool"][0], p["pool_scale"]

    h, r1 = _norm1(x, g1, started, tm=tm)
    w_in_t = weight("w_in", h)
    qkv, fl, pooled, pool_o = _in_proj_pool(h, w_in_t, w_pool, pool_scale, tm=t["tall"])
    fl_t = fl[:, :N_HEADS].T.reshape(N_HEADS * nb, LANES)
    b_rows = jnp.repeat(p["b_forget"].reshape(N_HEADS), nb).reshape(N_HEADS * nb, 1)
    c = _forget_cumsum(fl_t, b_rows).reshape(N_PAIRS, 2, s)
    c_rowblk = c.reshape(N_PAIRS, 2, nqb, tq).transpose(0, 2, 1, 3)
    attn_o, lse = _attn_fwd(qkv, c, tk=tq)
    lse = lse.reshape(N_PAIRS, nqb // 2, 2, 2, tq).transpose(0, 1, 3, 2, 4).reshape(N_PAIRS, nqb, 2, tq)
    w_out = weight("w_out", attn_o)
    wg_t, wu_t = weight("w_gate_up", attn_o)
    x1, h2, r2, gate, up, act = _out_gate_up(attn_o, pool_o, w_out, x, g2, wg_t, wu_t, tm=t["ff_rows"])
    wd = weight("w_down", act)
    dx2, loss_row, d_gf = _down_final(act, wd, x1, gf, tgt, tm=tm, sub=min(128, tm))

    dgate, dup = _swiglu_bwd(dx2, wd, gate, up, tm=t["ff_rows"], tn=D_FF)
    d_wd = _mm_tn_stacked([act], [D_FF], dx2, ts=t["tall"], name="grad_w_down")
    d_wg_t = _mm_tn_stacked([dgate], [D_FF], h2, ts=t["tall"], name="grad_w_gate")
    d_wu_t = _mm_tn_stacked([dup], [D_FF], h2, ts=t["tall"], name="grad_w_up")
    dx1, d_attn, du, d_g2, d_wpool, d_pscale = _mlp_in_pool_bwd(dgate, dup, wg_t, wu_t, w_out, x1, r2, g2, dx2, pooled,
                                                               w_pool, pool_scale, tm=t["ff_rows"])
    d_wo = _mm_tn_stacked([attn_o, pool_o], [ATTN_W, POOL_W], dx1, ts=t["tall"], name="grad_w_out")
    token = emit(("w_down", "w_gate", "w_up", "w_out"), (d_wd, d_wg_t, d_wu_t, d_wo))
    dq, dk, dv, dck, dcq = _attn_bwd(qkv, attn_o, d_attn, c_rowblk - lse, c, token, tq=tq)
    dc_t = (dcq - dck)[:, :, 0::4, :].transpose(0, 2, 1, 3).reshape(N_HEADS * nb, LANES)
    dfl_t, db = _forget_bwd(dc_t, fl_t, b_rows)
    dfl = jnp.pad(dfl_t.reshape(N_HEADS, s).T, ((0, 0), (0, LANES - N_HEADS))).astype(BF16)
    d_w_in_t = _mm_tn_stacked([dq, dk, dv, dfl, du], [ATTN_W, ATTN_W, ATTN_W, N_HEADS, POOL_W], h, ts=t["tall"],
                              name="grad_w_in",
                              windows=(N_DEV, IN_STEP, IN_WINDOW))
    token = emit(("w_in",), (d_w_in_t,))
    dx, d_g1 = _in_bwd(dq, dk, dv, du, dfl, w_in_t, x, r1, g1, dx1, token, tm=tm)

    small = dict(norm1_g=d_g1, b_forget=db[:, 0].reshape(1, N_HEADS), w_pool=d_wpool, pool_scale=d_pscale,
                 norm2_g=d_g2, final_g=d_gf)
    return loss_row, dx, small


def _my_index():
    return 4 * lax.axis_index("x") + 2 * lax.axis_index("y") + lax.axis_index("c")


def _peer(k):
    pos = [lax.axis_index(a) for a in ("x", "y", "c")]
    flipped = tuple(1 - p if (k >> b) & 1 else p for p, b in zip(pos, (2, 1, 0)))
    return flipped, 4 * flipped[0] + 2 * flipped[1] + flipped[2]


_HBM = pl.BlockSpec(memory_space=pltpu.HBM)
_SEM = pl.BlockSpec(memory_space=pltpu.SEMAPHORE)
_DATAFLOW = pltpu.SideEffectType.DATAFLOW_SIDE_EFFECTING


ALL_PEERS = tuple(range(1, N_DEV))
SAME_CORE = (1, 2, 4, 6)


def _peer_copies(ins, lands, send_sems, recv_sems, scatter, peers, arrivals):
    me = _my_index()
    copies = []
    for w in range(len(ins)):
        for k in peers[w]:
            dev, idx = _peer(k)
            copies.append(pltpu.make_async_remote_copy(
                src_ref=ins[w].at[idx] if scatter[w] else ins[w], dst_ref=lands[w].at[idx if arrivals else me],
                send_sem=send_sems[w].at[k - 1], recv_sem=recv_sems[w].at[k - 1], device_id=dev, device_id_type=MESH))
    return copies


def _own_copies(ins, lands, send_sems, scatter):
    me = _my_index()
    return [pltpu.make_async_copy(ins[w].at[me] if scatter[w] else ins[w], lands[w].at[me], send_sems[w].at[N_DEV - 1])
            for w in range(len(ins))]


def _forward_copies(land, send_sems, recv_sems, arrivals):
    sibling, _ = _peer(1)
    copies = []
    for j, k in enumerate(SAME_CORE[1:]):
        src, dst = _peer(k)[1], _peer(k ^ 1 if arrivals else k)[1]
        copies.append(pltpu.make_async_remote_copy(
            src_ref=land.at[src], dst_ref=land.at[dst], send_sem=send_sems.at[j], recv_sem=recv_sems.at[j],
            device_id=sibling, device_id_type=MESH))
    return copies


def _forward_start(land, name):
    def body(land_ref, send_sems, recv_sems, land_thru, token):
        for cp in _forward_copies(land_ref, send_sems, recv_sems, False):
            cp.start()
        token[...] = jnp.zeros_like(token)

    sem = pltpu.SemaphoreType.DMA((len(SAME_CORE) - 1,))
    send, recv, thru, _ = pl.pallas_call(
        body,
        in_specs=[_HBM],
        out_specs=[_SEM, _SEM, _HBM, pl.BlockSpec(memory_space=pltpu.VMEM)],
        out_shape=[sem, sem, pltpu.HBM(land.shape, land.dtype), SDS((8, LANES), F32)],
        input_output_aliases={0: 2},
        compiler_params=pltpu.CompilerParams(has_side_effects=_DATAFLOW),
        name=name,
    )(land)
    return send, recv, thru


def _forward_wait(handle, after, name):
    def body(land_ref, send_sems, recv_sems, after_ref, land_out):
        for cp in _forward_copies(land_ref, send_sems, recv_sems, False):
            cp.wait_send()
        for cp in _forward_copies(land_ref, send_sems, recv_sems, True):
            cp.wait_recv()

    send, recv, land = handle
    return pl.pallas_call(
        body,
        in_specs=[_HBM, _SEM, _SEM, pl.BlockSpec(memory_space=pl.ANY)],
        out_specs=_HBM,
        out_shape=pltpu.HBM(land.shape, land.dtype),
        input_output_aliases={0: 0},
        compiler_params=pltpu.CompilerParams(has_side_effects=_DATAFLOW),
        name=name,
    )(land, send, recv, after)


def _exchange_start(arrays, scatter, name, peers=None, after=None):
    n = len(arrays)
    peers = peers or [ALL_PEERS] * n
    order = [] if after is None else [after]
    land_shapes = [(N_DEV,) + tuple(a.shape[1:] if sc else a.shape) for a, sc in zip(arrays, scatter)]

    def body(*refs):
        ins, lands = refs[:n], refs[n:2 * n]
        outs = refs[2 * n + len(order):]
        send_sems, recv_sems, token = outs[:n], outs[n:2 * n], outs[4 * n]
        for cp in _peer_copies(ins, lands, send_sems, recv_sems, scatter, peers, False):
            cp.start()
        for cp in _own_copies(ins, lands, send_sems, scatter):
            cp.start()
        token[...] = jnp.zeros_like(token)

    sends, recvs = pltpu.SemaphoreType.DMA((N_DEV,)), pltpu.SemaphoreType.DMA((N_DEV - 1,))
    outs = pl.pallas_call(
        body,
        in_specs=[_HBM] * (2 * n) + [_UNREAD] * len(order),
        out_specs=[_SEM] * (2 * n) + [_HBM] * (2 * n) + [pl.BlockSpec(memory_space=pltpu.VMEM)],
        out_shape=[sends] * n + [recvs] * n + [pltpu.HBM(a.shape, a.dtype) for a in arrays]
        + [pltpu.HBM(sh, a.dtype) for sh, a in zip(land_shapes, arrays)] + [SDS((8, LANES), F32)],
        input_output_aliases={i: 2 * n + i for i in range(2 * n)},
        compiler_params=pltpu.CompilerParams(has_side_effects=_DATAFLOW),
        name=name,
    )(*[pltpu.with_memory_space_constraint(a, pltpu.HBM) for a in arrays],
      *[pltpu.with_memory_space_constraint(lax.empty(sh, a.dtype), pltpu.HBM) for sh, a in zip(land_shapes, arrays)],
      *order)
    handles = [dict(send=outs[w], recv=outs[n + w], src=outs[2 * n + w], land=outs[3 * n + w], scatter=scatter[w],
                    peers=peers[w]) for w in range(n)]
    return handles, outs[4 * n]


def _exchange_wait(handles, after, name):
    n = len(handles)
    scatter, peers = [h["scatter"] for h in handles], [h["peers"] for h in handles]

    def body(*refs):
        ins, lands = refs[:n], refs[n:2 * n]
        send_sems, recv_sems = refs[2 * n:3 * n], refs[3 * n:4 * n]
        for cp in _peer_copies(ins, lands, send_sems, recv_sems, scatter, peers, False):
            cp.wait_send()
        for cp in _peer_copies(ins, lands, send_sems, recv_sems, scatter, peers, True):
            cp.wait_recv()
        for cp in _own_copies(ins, lands, send_sems, scatter):
            cp.wait()

    srcs, lands = [h["src"] for h in handles], [h["land"] for h in handles]
    outs = pl.pallas_call(
        body,
        in_specs=[_HBM] * (2 * n) + [_SEM] * (2 * n) + [pl.BlockSpec(memory_space=pl.ANY)],
        out_specs=[_HBM] * (2 * n),
        out_shape=[pltpu.HBM(a.shape, a.dtype) for a in srcs + lands],
        input_output_aliases={i: i for i in range(2 * n)},
        compiler_params=pltpu.CompilerParams(has_side_effects=_DATAFLOW),
        name=name,
    )(*srcs, *lands, *[h["send"] for h in handles], *[h["recv"] for h in handles], after)
    return outs[n:]


def _adamw(parts, w, m, v, name):
    rows, cols = w.shape
    tr = rows // 4 if rows % 32 == 0 else rows

    def body(p_ref, w_ref, m_ref, v_ref, g_ref, d_ref, mo_ref, vo_ref):
        g = p_ref[0].astype(F32)
        for d in range(1, N_DEV):
            g = g + p_ref[d].astype(F32)
        g_ref[...] = g
        d_ref[...], mo_ref[...], vo_ref[...] = _adam_update(g, w_ref[...], m_ref[...], v_ref[...])

    blk = pl.BlockSpec((tr, cols), lambda i: (i, 0))
    return pl.pallas_call(
        body,
        grid=(rows // tr,),
        in_specs=[pl.BlockSpec((N_DEV, tr, cols), lambda i: (0, i, 0)), blk, blk, blk],
        out_specs=[blk] * 4,
        out_shape=[SDS((rows, cols), F32)] * 4,
        compiler_params=_cparams("arbitrary"),
        name=name,
    )(parts, w, m, v)


def _adamw_dense(parts, w, m, v, name, *, rows, shift):
    _, window, cols = parts.shape
    per_row = cols // LANES

    def body(p_ref, w_ref, m_ref, v_ref, g_ref, d_ref, mo_ref, vo_ref, sum_ref):
        g = p_ref[0].astype(F32)
        for d in range(1, N_DEV):
            g = g + p_ref[d].astype(F32)
        sum_ref[...] = g
        me = _my_index()
        for j in range(N_DEV):
            @pl.when(me == j)
            def _(j=j):
                for c in range(per_row):
                    at = (pl.ds(c, rows, stride=per_row), slice(None))
                    gc = sum_ref[j * shift:j * shift + rows, c * LANES:(c + 1) * LANES]
                    g_ref[at] = gc
                    d_ref[at], mo_ref[at], vo_ref[at] = _adam_update(gc, w_ref[at], m_ref[at], v_ref[at])

    return pl.pallas_call(
        body,
        out_shape=[SDS(w.shape, F32)] * 4,
        scratch_shapes=[pltpu.VMEM((window, cols), F32)],
        compiler_params=_cparams(),
        name=name,
    )(parts, w, m, v)


_ROW_OF = dict(norm1_g=(0, D_MODEL), norm2_g=(1, D_MODEL), final_g=(2, D_MODEL), pool_scale=(3, POOL_W),
               b_forget=(4, N_HEADS), loss=(5, 1))


def _pack_rows(vals):
    rows = [jnp.pad(vals[n].reshape(1, width).astype(F32), ((0, 0), (0, D_MODEL - width)))
            for n, (_, width) in sorted(_ROW_OF.items(), key=lambda kv: kv[1][0])]
    return jnp.concatenate(rows + [jnp.zeros((8 - len(rows), D_MODEL), F32)], axis=0)


def _adam_update(g, w, m, v):
    m_new = ADAM_B1 * m + (1.0 - ADAM_B1) * g
    v_new = ADAM_B2 * v + (1.0 - ADAM_B2) * (g * g)
    m_hat = m_new / (1.0 - ADAM_B1 ** ADAM_STEP)
    v_hat = v_new / (1.0 - ADAM_B2 ** ADAM_STEP)
    return -ADAM_LR * (m_hat / (jnp.sqrt(v_hat) + ADAM_EPS) + ADAM_WD * w), m_new, v_new


def _adamw_replicated(parts_rows, parts_pool, w, m, v):
    names = ("norm1_g", "norm2_g", "final_g", "pool_scale", "b_forget", "w_pool")
    shapes = {n: ((len(POOL_WINDOWS), POOL_G, POOL_G) if n == "w_pool" else (1, _ROW_OF[n][1])) for n in names}

    def body(rows_ref, pool_ref, *refs):
        ins, outs = refs[:3 * len(names)], refs[3 * len(names):]

        def total(n):
            if n == "w_pool":
                pieces = [pool_ref[d] for d in range(N_DEV)]
            else:
                row, width = _ROW_OF[n]
                pieces = [rows_ref[d, row:row + 1, 0:width] for d in range(N_DEV)]
            g = pieces[0]
            for p in pieces[1:]:
                g = g + p
            return g

        outs[0][...] = total("loss")
        for k, n in enumerate(names):
            g = total(n)
            delta, m_new, v_new = _adam_update(g, ins[3 * k][...], ins[3 * k + 1][...], ins[3 * k + 2][...])
            for o_ref, val in zip(outs[1 + 4 * k:5 + 4 * k], (g, delta, m_new, v_new)):
                o_ref[...] = val

    args = [d[n].reshape(shapes[n]) for n in names for d in (w, m, v)]
    res = pl.pallas_call(
        body,
        out_shape=[SDS((1, 1), F32)] + [SDS(shapes[n], F32) for n in names for _ in range(4)],
        compiler_params=_cparams(),
        name="adamw_replicated",
    )(parts_rows, parts_pool, *args)
    return res[0], {n: [r.reshape(w[n].shape) for r in res[1 + 4 * k:5 + 4 * k]] for k, n in enumerate(names)}


def kernel(x, norm1_g, w_in, b_forget, w_pool, pool_scale, w_out, norm2_g, w_gate, w_up, w_down, final_g, loss_target, m_norm1_g, m_w_in, m_b_forget, m_w_pool, m_pool_scale, m_w_out, m_norm2_g, m_w_gate, m_w_up, m_w_down, m_final_g, v_norm1_g, v_w_in, v_b_forget, v_w_pool, v_pool_scale, v_w_out, v_norm2_g, v_w_gate, v_w_up, v_w_down, v_final_g):
    big = ("w_in", "w_out", "w_gate", "w_up", "w_down")
    order = ("norm1_g", "w_in", "b_forget", "w_pool", "pool_scale", "w_out", "norm2_g", "w_gate", "w_up", "w_down",
             "final_g")
    w = dict(norm1_g=norm1_g, w_in=w_in, b_forget=b_forget, w_pool=w_pool, pool_scale=pool_scale, w_out=w_out,
             norm2_g=norm2_g, w_gate=w_gate, w_up=w_up, w_down=w_down, final_g=final_g)
    m = dict(norm1_g=m_norm1_g, w_in=m_w_in, b_forget=m_b_forget, w_pool=m_w_pool, pool_scale=m_pool_scale,
             w_out=m_w_out, norm2_g=m_norm2_g, w_gate=m_w_gate, w_up=m_w_up, w_down=m_w_down, final_g=m_final_g)
    v = dict(norm1_g=v_norm1_g, w_in=v_w_in, b_forget=v_b_forget, w_pool=v_w_pool, pool_scale=v_pool_scale,
             w_out=v_w_out, norm2_g=v_norm2_g, w_gate=v_w_gate, w_up=v_w_up, w_down=v_w_down, final_g=v_final_g)

    flipped = ("w_in", "w_gate", "w_up")
    shard = lambda d, n: d[n][0].T if n in flipped else d[n][0]
    cast = lambda n: shard(w, n).astype(BF16)
    (first,), started = _exchange_start([cast("w_in")], [False], "gather_start_w_in", peers=[SAME_CORE])
    gather = dict(w_in=first)

    def gathered(names, after):
        return _exchange_wait([gather[n] for n in names], after, "gather_wait_" + names[0])

    def weight(name, after):
        if name == "w_in":
            handles, token = _exchange_start([cast(n) for n in big[1:]], [False] * len(big[1:]), "gather_start", after=after)
            gather.update(zip(big[1:], handles))
            forward = _forward_start(gathered(["w_in"], token)[0], "gather_forward_start")
            full = _forward_wait(forward, after, "gather_forward_wait").reshape(IN_W, D_MODEL)
            f0 = QKV_W + N_HEADS
            return jnp.concatenate([full[:QKV_W], full[f0:], full[QKV_W:f0],
                                    jnp.zeros((IN_PAD - IN_W, D_MODEL), BF16)], axis=0)
        if name == "w_out":
            return gathered(["w_out"], after)[0].reshape(D_MODEL, D_MODEL)
        if name == "w_gate_up":
            return [g.reshape(D_FF, D_MODEL) for g in gathered(["w_gate", "w_up"], after)]
        return gathered(["w_down"], after)[0].reshape(D_FF, D_MODEL)

    rows = lambda g: g if g.ndim == 3 else g.reshape(N_DEV, g.shape[0] // N_DEV, g.shape[1])
    sent = {}

    def emit(names, grads):
        handles, token = _exchange_start([rows(g) for g in grads], [True] * len(names), "grads_start_" + names[0])
        sent.update(zip(names, handles))
        return token

    loss_row, dx, small_grads = _local_step(x[0], loss_target[0], w, weight, emit, started)

    packed = _pack_rows(dict(small_grads, loss=0.5 / D_MODEL * jnp.sum(loss_row)))
    small_handles, after = _exchange_start([packed, small_grads["w_pool"]], [False, False], "grads_start_replicated")

    outs = {}
    for name in ("w_down", "w_gate", "w_up", "w_out", "w_in"):
        (parts,) = _exchange_wait([sent[name]], after, "grads_wait_" + name)
        if name == "w_in":
            dense = lambda d: d[name].transpose(2, 0, 1).reshape(-1, LANES)
            outs[name] = _adamw_dense(parts, dense(w), dense(m), dense(v), "adamw_" + name, rows=IN_SHARD, shift=IN_SHIFT)
            after = outs[name][0]
            outs[name] = [a.reshape(-1, D_MODEL // LANES, LANES).transpose(1, 2, 0).reshape(1, D_MODEL, -1)
                          for a in outs[name]]
            continue
        outs[name] = _adamw(parts, shard(w, name), shard(m, name), shard(v, name), "adamw_" + name)
        after = outs[name][0]
        outs[name] = [(a.T if name in flipped else a)[None] for a in outs[name]]
    parts_rows, parts_pool = _exchange_wait(small_handles, after, "grads_wait_replicated")
    loss, small = _adamw_replicated(parts_rows, parts_pool, w, m, v)
    outs.update(small)

    return (loss.reshape(()), dx[None]) + tuple(outs[n][k] for k in range(4) for n in order)
```

```python
import jax
import jax.numpy as jnp
from jax import lax
from jax.experimental import pallas as pl
from jax.experimental.pallas import tpu as pltpu

F32 = jnp.float32
BF16 = jnp.bfloat16
SDS = jax.ShapeDtypeStruct

D_MODEL = 1024
ATTN_W = 512
N_HEADS = 8
HEAD_DIM = 64
Q_SCALE = HEAD_DIM ** -0.5
N_PAIRS = N_HEADS // 2
POOL_W = 512
POOL_WINDOWS = (2, 4, 8, 16)
POOL_G = 128
HALO = 16
IN_W = 3 * ATTN_W + N_HEADS + POOL_W
QKV_W = 3 * ATTN_W
U_OFF = QKV_W
F_OFF = QKV_W + POOL_W
IN_PAD = F_OFF + 128
D_FF = 2816
EPS = 1e-6
NEG = -1e30
N_DEV = 8
LANES = 128
BF16_ROWS = 16

IN_SHARD = IN_W // N_DEV
IN_STEP = IN_SHARD // BF16_ROWS * BF16_ROWS
IN_SHIFT = IN_SHARD - IN_STEP
IN_WINDOW = -(-((N_DEV - 1) * IN_SHIFT + IN_SHARD) // BF16_ROWS) * BF16_ROWS

ADAM_LR = 0.001
ADAM_B1 = 0.9
ADAM_B2 = 0.999
ADAM_EPS = 1e-08
ADAM_WD = 0.01
ADAM_STEP = 10

VMEM_LIMIT_BYTES = 56 * 1024 * 1024
MESH = pl.DeviceIdType.MESH
NT = (((1,), (1,)), ((), ()))
TN = (((0,), (0,)), ((), ()))


_UNREAD = pl.BlockSpec(memory_space=pl.ANY)


def _cparams(*sem):
    return pltpu.CompilerParams(dimension_semantics=sem or None, vmem_limit_bytes=VMEM_LIMIT_BYTES)


def _split3(a):
    hi = a.astype(BF16)
    r1 = a - hi.astype(F32)
    mid = r1.astype(BF16)
    lo = (r1 - mid.astype(F32)).astype(BF16)
    return hi, mid, lo


def _dot_sel(a, sel, dims=None):
    sb = sel.astype(BF16)
    if dims is None:
        return sum(jnp.dot(p, sb, preferred_element_type=F32) for p in _split3(a))
    return sum(lax.dot_general(p, sb, dims, preferred_element_type=F32) for p in _split3(a))


def _sel_dot(sel, a, dims=None):
    sb = sel.astype(BF16)
    if dims is None:
        return sum(jnp.dot(sb, p, preferred_element_type=F32) for p in _split3(a))
    return sum(lax.dot_general(sb, p, dims, preferred_element_type=F32) for p in _split3(a))


def _iota2(shape, dim):
    return lax.broadcasted_iota(jnp.int32, shape, dim)


UNROLLS = (8, 4, 2)


def _shift_div(x, n):
    return lax.shift_right_logical(x, n.bit_length() - 1)


def _norm1(x, g1, after, *, tm):
    s = x.shape[0]

    def body(x_ref, g_ref, _, h_ref, r_ref):
        xv = x_ref[...]
        r = lax.rsqrt(jnp.mean(xv * xv, axis=-1, keepdims=True) + EPS)
        h_ref[...] = (xv * r * g_ref[...]).astype(BF16)
        r_ref[...] = r

    row = lambda w: pl.BlockSpec((tm, w), lambda i: (i, 0))
    return pl.pallas_call(
        body,
        grid=(s // tm,),
        in_specs=[row(D_MODEL), pl.BlockSpec((1, D_MODEL), lambda i: (0, 0)), _UNREAD],
        out_specs=[row(D_MODEL), row(1)],
        out_shape=[SDS((s, D_MODEL), BF16), SDS((s, 1), F32)],
        compiler_params=_cparams("arbitrary"),
        name="norm1",
    )(x, g1, after)


def _in_proj_pool(h, w_in_t, w_pool, pool_scale, *, tm):
    s = h.shape[0]

    def body(h_ref, w_ref, wp_ref, sc_ref, qkv_ref, fl_ref, pooled_ref, po_ref, tail_ref):
        i = pl.program_id(0)

        @pl.when(i == 0)
        def _():
            tail_ref[...] = jnp.zeros_like(tail_ref)

        hv = h_ref[...]
        uv = lax.dot_general(hv, w_ref[U_OFF:F_OFF, :], NT, preferred_element_type=F32)
        qkv_ref[...] = lax.dot_general(hv, w_ref[0:QKV_W, :], NT, preferred_element_type=F32).astype(BF16)
        fl_ref[...] = lax.dot_general(hv, w_ref[F_OFF:IN_PAD, :], NT, preferred_element_type=F32)
        ext = jnp.concatenate([tail_ref[...], uv], axis=0)
        tail_ref[...] = uv[tm - HALO:, :]
        for g, w in enumerate(POOL_WINDOWS):
            cols = slice(g * POOL_G, (g + 1) * POOL_G)
            acc = ext[:, cols]
            k = 1
            while k < w:
                acc = acc + pltpu.roll(acc, k, axis=0)
                k *= 2
            pooled = (acc[HALO:, :] / _pool_counts(i * tm, tm, w) - uv[:, cols]).astype(BF16)
            pooled_ref[:, cols] = pooled
            mixed = jnp.dot(pooled, wp_ref[g].astype(BF16), preferred_element_type=F32)
            po_ref[:, cols] = (mixed * sc_ref[:, cols]).astype(BF16)

    row = lambda w: pl.BlockSpec((tm, w), lambda i: (i, 0))
    return pl.pallas_call(
        body,
        grid=(s // tm,),
        in_specs=[row(D_MODEL), pl.BlockSpec((IN_PAD, D_MODEL), lambda i: (0, 0)),
                  pl.BlockSpec((len(POOL_WINDOWS), POOL_G, POOL_G), lambda i: (0, 0, 0)),
                  pl.BlockSpec((1, POOL_W), lambda i: (0, 0))],
        out_specs=[row(QKV_W), row(LANES), row(POOL_W), row(POOL_W)],
        out_shape=[SDS((s, QKV_W), BF16), SDS((s, LANES), F32), SDS((s, POOL_W), BF16), SDS((s, POOL_W), BF16)],
        scratch_shapes=[pltpu.VMEM((HALO, POOL_W), F32)],
        compiler_params=_cparams("arbitrary"),
        name="in_proj_pool",
    )(h, w_in_t, w_pool, pool_scale)


def _head_block_masks(rows, nb):
    shift = nb.bit_length() - 1
    rr, cc = _iota2((rows, rows), 0), _iota2((rows, rows), 1)
    same = lax.shift_right_logical(rr, shift) == lax.shift_right_logical(cc, shift)
    return rr, cc, same


def _forget_cumsum(fl_t, b_rows):
    rows = fl_t.shape[0]
    nb = rows // N_HEADS

    def body(fl_ref, b_ref, c_ref):
        z = fl_ref[...] + b_ref[...]
        lf = jnp.minimum(z, 0.0) - jnp.log1p(jnp.exp(-jnp.abs(z)))
        upper = _iota2((LANES, LANES), 0) <= _iota2((LANES, LANES), 1)
        within = _dot_sel(lf, upper)
        tot = _dot_sel(lf, jnp.ones((LANES, LANES), F32))
        rr, cc, same = _head_block_masks(rows, nb)
        c_ref[...] = within + _sel_dot(same & (cc < rr), tot)

    return pl.pallas_call(body, out_shape=SDS(fl_t.shape, F32), compiler_params=_cparams(), name="forget_cumsum")(
        fl_t, b_rows)


BIAS_LANES = 3


def _augment(t, h, bias, col_first):
    n = t.shape[0]
    lane = _iota2((n, LANES), 1)
    own = (lane < HEAD_DIM) if h == 0 else (lane >= HEAD_DIM)
    b0 = HEAD_DIM if h == 0 else 0
    c0, o0 = (b0, b0 + BIAS_LANES) if col_first else (b0 + BIAS_LANES, b0)
    x = 0.0
    if bias is not None:
        row = _iota2((BF16_ROWS, n), 0)
        pieces = jnp.zeros((BF16_ROWS, n), F32)
        for off, piece in enumerate(_split3(bias)):
            pieces = jnp.where(row == off, piece.astype(F32), pieces)
        r, ln = _iota2((BF16_ROWS, LANES), 0), _iota2((BF16_ROWS, LANES), 1)
        place = jnp.where(r < BIAS_LANES, jnp.where(ln == c0 + r, 1.0, 0.0), 0.0).astype(BF16)
        x = lax.dot_general(pieces.astype(BF16), place, TN, preferred_element_type=F32)
    x = jnp.where(own, t, x)
    x = jnp.where((lane >= o0) & (lane < o0 + BIAS_LANES), 1.0, x)
    return x.astype(BF16)


def _attn_fwd(qkv, c_rows, *, tk):
    s = qkv.shape[0]
    tq = 2 * tk
    nb = s // tk

    def body(q_ref, k_ref, v_ref, cq_ref, ck_ref, o_ref, lse_ref, kp_ref, vt_ref, st_ref):
        i = pl.program_id(1)

        @pl.when(i == 0)
        def _():
            def prep(jb, _):
                st = pl.multiple_of(jb * tk, tk)
                k2 = k_ref[pl.ds(st, tk), :].astype(F32)
                ck = ck_ref[:, pl.ds(st, tk)]
                for h in range(2):
                    kp_ref[h * nb + jb] = _augment(k2, h, -ck[h:h + 1, :], True)
                vt_ref[jb] = v_ref[pl.ds(st, tk), :].astype(F32).T.astype(BF16)
                return 0

            lax.fori_loop(0, nb, prep, 0)

        qs = q_ref[...].astype(F32) * Q_SCALE
        cq = cq_ref[...]
        qp = [_augment(qs, h, cq[h:h + 1, :], False) for h in range(2)]

        def logits(j):
            return tuple(lax.dot_general(kp_ref[h * nb + j], qp[h], NT, preferred_element_type=F32) for h in range(2))

        def softmax_pv(j, slot, stats, masked):
            out = []
            for h in range(2):
                m, l, acc = stats[h]
                st = st_ref[2 * slot + h]
                if masked:
                    st = jnp.where(j * tk + _iota2((tk, tq), 0) <= i * tq + _iota2((tk, tq), 1), st, NEG)
                m_new = jnp.maximum(m, jnp.max(st, axis=0, keepdims=True))
                alpha = jnp.exp(m - m_new)
                p = jnp.exp(st - m_new)
                l = alpha * l + jnp.sum(p, axis=0, keepdims=True)
                vt = vt_ref[j, h * HEAD_DIM:(h + 1) * HEAD_DIM, :]
                acc = alpha * acc + jnp.dot(vt, p.astype(BF16), preferred_element_type=F32)
                out.append((m_new, l, acc))
            return tuple(out)

        def put(slot, j):
            for h, st in enumerate(logits(j)):
                st_ref[2 * slot + h] = st

        def run(j0, steps, stats):
            for d in range(steps):
                put(1 - d % 2, j0 + d + 1)
                stats = softmax_pv(j0 + d, d % 2, stats, False)
            return stats

        init = tuple((jnp.full((1, tq), NEG, F32), jnp.zeros((1, tq), F32), jnp.zeros((HEAD_DIM, tq), F32))
                     for _ in range(2))
        put(0, 0)
        first, left, stats = 0, 2 * i, init
        for size in UNROLLS:
            trips = _shift_div(left, size)
            stats = lax.fori_loop(0, trips, lambda t, st, j0=first, n=size: run(j0 + n * t, n, st), stats)
            first, left = first + size * trips, left - size * trips
        put(1, 2 * i + 1)
        stats = softmax_pv(2 * i, 0, stats, True)
        (ma, la, acca), (mb, lb, accb) = softmax_pv(2 * i + 1, 1, stats, True)
        o_ref[...] = jnp.concatenate([acca / la, accb / lb], axis=0).T.astype(BF16)
        lse_ref[...] = jnp.where(_iota2((2, tq), 0) == 0, ma + jnp.log(la), mb + jnp.log(lb))

    return pl.pallas_call(
        body,
        grid=(N_PAIRS, s // tq),
        in_specs=[
            pl.BlockSpec((tq, LANES), lambda p, i: (i, p)),
            pl.BlockSpec((s, LANES), lambda p, i: (0, N_PAIRS + p)),
            pl.BlockSpec((s, LANES), lambda p, i: (0, 2 * N_PAIRS + p)),
            pl.BlockSpec((None, 2, tq), lambda p, i: (p, 0, i)),
            pl.BlockSpec((None, 2, s), lambda p, i: (p, 0, 0)),
        ],
        out_specs=[
            pl.BlockSpec((tq, LANES), lambda p, i: (i, p)),
            pl.BlockSpec((None, None, 2, tq), lambda p, i: (p, i, 0, 0)),
        ],
        out_shape=[SDS((s, ATTN_W), BF16), SDS((N_PAIRS, s // tq, 2, tq), F32)],
        scratch_shapes=[pltpu.VMEM((2 * nb, tk, LANES), BF16), pltpu.VMEM((nb, LANES, tk), BF16),
                        pltpu.VMEM((4, tk, tq), F32)],
        compiler_params=_cparams("arbitrary", "arbitrary"),
        name="attn_fwd",
    )(qkv, qkv, qkv, c_rows, c_rows)


def _pool_counts(row0, tm, w):
    t = row0 + _iota2((tm, 1), 0)
    return jnp.minimum(t + 1, w).astype(F32)


def _out_gate_up(attn_o, pool_o, w_out, x, g2, wg_t, wu_t, *, tm):
    s = x.shape[0]

    def body(a_ref, p_ref, wo_ref, x_ref, g_ref, wg_ref, wu_ref, x1_ref, h2_ref, r_ref, gate_ref, up_ref, act_ref):
        x1 = (x_ref[...] + jnp.dot(a_ref[...], wo_ref[0:ATTN_W, :], preferred_element_type=F32)
              + jnp.dot(p_ref[...], wo_ref[ATTN_W:, :], preferred_element_type=F32))
        r = lax.rsqrt(jnp.mean(x1 * x1, axis=-1, keepdims=True) + EPS)
        x1_ref[...] = x1
        r_ref[...] = r
        h2 = (x1 * r * g_ref[...]).astype(BF16)
        h2_ref[...] = h2
        gate = lax.dot_general(h2, wg_ref[...], NT, preferred_element_type=F32)
        up = lax.dot_general(h2, wu_ref[...], NT, preferred_element_type=F32)
        gate_ref[...] = gate.astype(BF16)
        up_ref[...] = up.astype(BF16)
        act_ref[...] = (gate * jax.nn.sigmoid(gate) * up).astype(BF16)

    row = lambda w: pl.BlockSpec((tm, w), lambda i: (i, 0))
    full = lambda a, b: pl.BlockSpec((a, b), lambda i: (0, 0))
    return pl.pallas_call(
        body,
        grid=(s // tm,),
        in_specs=[row(ATTN_W), row(POOL_W), full(D_MODEL, D_MODEL), row(D_MODEL), full(1, D_MODEL),
                  full(D_FF, D_MODEL), full(D_FF, D_MODEL)],
        out_specs=[row(D_MODEL), row(D_MODEL), row(1), row(D_FF), row(D_FF), row(D_FF)],
        out_shape=[SDS((s, D_MODEL), F32), SDS((s, D_MODEL), BF16), SDS((s, 1), F32), SDS((s, D_FF), BF16),
                   SDS((s, D_FF), BF16), SDS((s, D_FF), BF16)],
        compiler_params=_cparams("arbitrary"),
        name="out_gate_up",
    )(attn_o, pool_o, w_out, x, g2, wg_t, wu_t)


def _staggered(n, start, finish):
    pending = start(0)
    for k in range(n):
        following = start(k + 1) if k + 1 < n else None
        finish(k, pending)
        pending = following


def _down_final(act, wd, x1, gf, tgt, *, tm, sub):
    s = x1.shape[0]

    def body(a_ref, w_ref, x1_ref, g_ref, t_ref, dx2_ref, loss_ref, dgf_ref):
        @pl.when(pl.program_id(0) == 0)
        def _():
            loss_ref[...] = jnp.zeros_like(loss_ref)
            dgf_ref[...] = jnp.zeros_like(dgf_ref)

        g = g_ref[...]

        def matmul(k):
            return jnp.dot(a_ref[k * sub:(k + 1) * sub, :], w_ref[...], preferred_element_type=F32)

        def rest(k, mm):
            rows = slice(k * sub, (k + 1) * sub)
            x2 = x1_ref[rows, :] + mm
            r = lax.rsqrt(jnp.mean(x2 * x2, axis=-1, keepdims=True) + EPS)
            xn = x2 * r
            diff = xn * g - t_ref[rows, :]
            loss_ref[...] += jnp.sum(diff * diff, axis=0, keepdims=True)
            dy = diff * (1.0 / D_MODEL)
            dgf_ref[...] += jnp.sum(dy * xn, axis=0, keepdims=True)
            dxn = dy * g
            dx2_ref[rows, :] = r * (dxn - xn * jnp.mean(dxn * xn, axis=-1, keepdims=True))

        _staggered(tm // sub, matmul, rest)

    row = lambda w: pl.BlockSpec((tm, w), lambda i: (i, 0))
    full = lambda a, b: pl.BlockSpec((a, b), lambda i: (0, 0))
    return pl.pallas_call(
        body,
        grid=(s // tm,),
        in_specs=[row(D_FF), full(D_FF, D_MODEL), row(D_MODEL), full(1, D_MODEL), row(D_MODEL)],
        out_specs=[row(D_MODEL), full(1, D_MODEL), full(1, D_MODEL)],
        out_shape=[SDS((s, D_MODEL), F32), SDS((1, D_MODEL), F32), SDS((1, D_MODEL), F32)],
        compiler_params=_cparams("arbitrary"),
        name="down_final",
    )(act, wd, x1, gf, tgt)


def _swiglu_bwd(dx2, wd, gate, up, *, tm, tn):
    s = dx2.shape[0]

    def body(d_ref, w_ref, gate_ref, up_ref, dgate_ref, dup_ref):
        dact = lax.dot_general(d_ref[...].astype(BF16), w_ref[...], NT, preferred_element_type=F32)
        gate = gate_ref[...].astype(F32)
        sg = jax.nn.sigmoid(gate)
        dup_ref[...] = (dact * (gate * sg)).astype(BF16)
        dgate_ref[...] = (dact * up_ref[...].astype(F32) * (sg * (1.0 + gate * (1.0 - sg)))).astype(BF16)

    ospec = pl.BlockSpec((tm, tn), lambda c, r: (r, c))
    return pl.pallas_call(
        body,
        grid=(D_FF // tn, s // tm),
        in_specs=[pl.BlockSpec((tm, D_MODEL), lambda c, r: (r, 0)), pl.BlockSpec((tn, D_MODEL), lambda c, r: (c, 0)),
                  ospec, ospec],
        out_specs=[ospec, ospec],
        out_shape=[SDS((s, D_FF), BF16), SDS((s, D_FF), BF16)],
        compiler_params=_cparams("arbitrary", "arbitrary"),
        name="swiglu_bwd",
    )(dx2, wd, gate, up)


def _mm_tn_stacked(as_, rows, b, *, ts, name, windows=None):
    s, nb_ = b.shape
    n = len(as_)
    offsets = [sum(rows[:i]) for i in range(n)]
    total = sum(rows)
    if windows is None:
        acc_rows, out_shape = total, (total, nb_)
    else:
        count, step, size = windows
        acc_rows, out_shape = max(total, (count - 1) * step + size), (count, size, nb_)

    def body(*refs):
        a_refs, b_ref, o_ref, acc_ref = refs[:n], refs[n], refs[n + 1], refs[n + 2]
        k = pl.program_id(0)

        @pl.when(k == 0)
        def _():
            acc_ref[...] = jnp.zeros_like(acc_ref)

        bv = b_ref[...].astype(BF16)
        for a_ref, off, cnt in zip(a_refs, offsets, rows):
            part = lax.dot_general(a_ref[...].astype(BF16), bv, TN, preferred_element_type=F32)
            acc_ref[off:off + cnt, :] += part[0:cnt, :]

        @pl.when(k == s // ts - 1)
        def _():
            if windows is None:
                o_ref[...] = acc_ref[...].astype(BF16)
            else:
                for d in range(count):
                    o_ref[d] = acc_ref[d * step:d * step + size, :].astype(BF16)

    return pl.pallas_call(
        body,
        grid=(s // ts,),
        in_specs=[pl.BlockSpec((ts, a.shape[1]), lambda k: (k, 0)) for a in as_] + [pl.BlockSpec((ts, nb_), lambda k: (k, 0))],
        out_specs=pl.BlockSpec(out_shape, lambda k: (0,) * len(out_shape)),
        out_shape=SDS(out_shape, BF16),
        scratch_shapes=[pltpu.VMEM((acc_rows, nb_), F32)],
        compiler_params=_cparams("arbitrary"),
        name=name,
    )(*as_, b)


def _norm_bwd(dh, x, r, g, dres):
    xn = x * r
    dxn = dh * g
    dx = dres + r * (dxn - xn * jnp.mean(dxn * xn, axis=-1, keepdims=True))
    return dx, jnp.sum(dh * xn, axis=0, keepdims=True)


def _mlp_in_pool_bwd(dgate, dup, wg_t, wu_t, w_out, x1, r2, g2, dx2, pooled, w_pool, pool_scale, *, tm):
    s = x1.shape[0]
    nt = s // tm
    ng = len(POOL_WINDOWS)

    def body(dg_ref, dup_ref, wg_ref, wu_ref, wo_ref, x_ref, r_ref, g_ref, d_ref, p_ref, w_ref, sc_ref,
             dx1_ref, dattn_ref, du_ref, dg2_ref, dw_ref, dsc_ref, head_ref):
        i = pl.program_id(0)

        @pl.when(i == 0)
        def _():
            dg2_ref[...] = jnp.zeros_like(dg2_ref)
            head_ref[...] = jnp.zeros_like(head_ref)
            dw_ref[...] = jnp.zeros_like(dw_ref)
            dsc_ref[...] = jnp.zeros_like(dsc_ref)

        dh2 = (jnp.dot(dg_ref[...], wg_ref[...], preferred_element_type=F32)
               + jnp.dot(dup_ref[...], wu_ref[...], preferred_element_type=F32))
        dx1, dg2 = _norm_bwd(dh2, x_ref[...], r_ref[...], g_ref[...], d_ref[...])
        dg2_ref[...] += dg2
        dx1_ref[...] = dx1
        dmix = lax.dot_general(dx1.astype(BF16), wo_ref[...], NT, preferred_element_type=F32)
        dattn_ref[...] = dmix[:, 0:ATTN_W]
        row0 = (nt - 1 - i) * tm
        for g, w in enumerate(POOL_WINDOWS):
            cols = slice(g * POOL_G, (g + 1) * POOL_G)
            wb = w_ref[g].astype(BF16)
            pooled_g = p_ref[:, cols]
            dpo = dmix[:, ATTN_W + g * POOL_G:ATTN_W + (g + 1) * POOL_G]
            mixed = jnp.dot(pooled_g, wb, preferred_element_type=F32)
            dsc_ref[:, cols] += jnp.sum(dpo * mixed, axis=0, keepdims=True)
            dmp = (dpo * sc_ref[:, cols]).astype(BF16)
            dw_ref[g] += lax.dot_general(pooled_g, dmp, TN, preferred_element_type=F32)
            dpooled = lax.dot_general(dmp, wb, NT, preferred_element_type=F32)
            a = dpooled / _pool_counts(row0, tm, w)
            acc = jnp.concatenate([a, head_ref[:, cols]], axis=0)
            head_ref[:, cols] = a[0:HALO, :]
            k = 1
            while k < w:
                acc = acc + pltpu.roll(acc, tm + HALO - k, axis=0)
                k *= 2
            du_ref[:, cols] = (acc[0:tm, :] - dpooled).astype(BF16)

    row = lambda w: pl.BlockSpec((tm, w), lambda i: (nt - 1 - i, 0))
    full = lambda a, b: pl.BlockSpec((a, b), lambda i: (0, 0))
    pool_w = pl.BlockSpec((ng, POOL_G, POOL_G), lambda i: (0, 0, 0))
    return pl.pallas_call(
        body,
        grid=(nt,),
        in_specs=[row(D_FF), row(D_FF), full(D_FF, D_MODEL), full(D_FF, D_MODEL), full(D_MODEL, D_MODEL),
                  row(D_MODEL), row(1), full(1, D_MODEL), row(D_MODEL), row(POOL_W), pool_w, full(1, POOL_W)],
        out_specs=[row(D_MODEL), row(ATTN_W), row(POOL_W), full(1, D_MODEL), pool_w, full(1, POOL_W)],
        out_shape=[SDS((s, D_MODEL), F32), SDS((s, ATTN_W), F32), SDS((s, POOL_W), BF16), SDS((1, D_MODEL), F32),
                   SDS((ng, POOL_G, POOL_G), F32), SDS((1, POOL_W), F32)],
        scratch_shapes=[pltpu.VMEM((HALO, POOL_W), F32)],
        compiler_params=_cparams("arbitrary"),
        name="mlp_in_pool_bwd",
    )(dgate, dup, wg_t, wu_t, w_out, x1, r2, g2, dx2, pooled, w_pool, pool_scale)


SUM_ROWS = 16


def _heads_t(t):
    n = t.shape[0]
    lane = _iota2((n, LANES), 1)
    tf = t.astype(F32)
    halves = jnp.concatenate([jnp.where(lane < HEAD_DIM, tf, 0.0).T, jnp.where(lane < HEAD_DIM, 0.0, tf).T], axis=1)
    r, c = _iota2((SUM_ROWS, 2 * n), 0), _iota2((SUM_ROWS, 2 * n), 1)
    ones = jnp.where(((r == 0) & (c < n)) | ((r == 4) & (c >= n)), 1.0, 0.0)
    return jnp.concatenate([halves, ones], axis=0).astype(BF16)


def _attn_bwd(qkv, attn_o, d_attn, rowb, c_rows, after, *, tq):
    s = qkv.shape[0]
    tk = tq
    nb = s // tq
    rows_t = LANES + SUM_ROWS

    def body(q_ref, k_ref, v_ref, o_ref, do_ref, rowb_ref, ck_ref, _, dq_ref, dk_ref, dv_ref, dck_ref, dcq_ref,
             dqt_ref, delta_ref, kp_ref, qp_ref, dob_ref, qt_ref, kt_ref, dot_ref, front_ref):
        lane = _iota2((tq, LANES), 1)
        lo = lane < HEAD_DIM
        first = _iota2((8, LANES), 1) < HEAD_DIM
        sel = jnp.where(_iota2((8, LANES), 0) < 4, jnp.where(first, 1.0, 0.0), jnp.where(first, 0.0, 1.0))

        def prep(b, _):
            st = pl.multiple_of(b * tq, tq)
            do2 = do_ref[pl.ds(st, tq), :]
            delta_ref[b] = _sel_dot(sel, do2 * o_ref[pl.ds(st, tq), :].astype(F32), NT)
            dob_ref[pl.ds(st, tq), :] = do2.astype(BF16)
            dqt_ref[b] = jnp.zeros((rows_t, tq), F32)
            k2 = k_ref[pl.ds(st, tq), :].astype(F32)
            q2 = q_ref[pl.ds(st, tq), :].astype(F32)
            ck = ck_ref[:, pl.ds(st, tq)]
            for h in range(2):
                kp_ref[h * nb + b] = _augment(k2, h, -ck[h:h + 1, :], True)
                qp_ref[h * nb + b] = _augment(q2 * Q_SCALE, h, None, False)
            qt_ref[b] = _heads_t(q2)
            kt_ref[b] = _heads_t(k2)
            dot_ref[b] = _heads_t(do2)[0:LANES, :]
            return 0

        lax.fori_loop(0, nb, prep, 0)

        def split(t):
            z = jnp.zeros_like(t)
            return jnp.where(lo, t, z), jnp.where(lo, z, t)

        def kv_block(j, _):
            st_j = pl.multiple_of(j * tk, tk)
            vs = split(v_ref[pl.ds(st_j, tk), :])
            kt = kt_ref[j]

            def stage(i, slot):
                ic = jnp.minimum(i, nb - 1)
                do2 = dob_ref[pl.ds(pl.multiple_of(ic * tq, tq), tq), :]
                for h in range(2):
                    front_ref[4 * slot + h] = lax.dot_general(kp_ref[h * nb + j], qp_ref[h * nb + ic], NT,
                                                              preferred_element_type=F32)
                    front_ref[4 * slot + 2 + h] = lax.dot_general(vs[h], do2, NT, preferred_element_type=F32)

            def q_block(i, slot, carry, diagonal):
                dkt, dvt = carry
                ic = jnp.minimum(i, nb - 1)
                rb = rowb_ref[ic] + jnp.where(i < nb, 0.0, NEG)
                dl = delta_ref[ic]
                pts, dsts = [], []
                for h in range(2):
                    st = front_ref[4 * slot + h] + rb[h:h + 1, :]
                    if diagonal:
                        st = jnp.where(_iota2((tk, tq), 0) <= _iota2((tk, tq), 1), st, NEG)
                    pt = jnp.exp(st)
                    pts.append(pt.astype(BF16))
                    dsts.append((pt * (front_ref[4 * slot + 2 + h] - dl[4 * h:4 * h + 1, :])).astype(BF16))
                dvt = dvt + lax.dot_general(dot_ref[ic], jnp.concatenate(pts, axis=1), NT, preferred_element_type=F32)
                dkt = dkt + lax.dot_general(qt_ref[ic], jnp.concatenate(dsts, axis=1), NT, preferred_element_type=F32)
                dqt_ref[ic] += jnp.dot(kt, jnp.concatenate(dsts, axis=0), preferred_element_type=F32)
                return dkt, dvt

            def run(i0, steps, carry):
                for d in range(steps):
                    stage(i0 + d + 1, d % 2)
                    carry = q_block(i0 + d, 1 - d % 2, carry, False)
                return carry

            stage(j, 0)
            stage(j + 1, 1)
            carry = q_block(j, 0, (jnp.zeros((rows_t, tk), F32), jnp.zeros((LANES, tk), F32)), True)
            first, left = j + 1, nb - 1 - j
            for size in UNROLLS:
                trips = _shift_div(left + 1 if size == UNROLLS[-1] else left, size)
                carry = lax.fori_loop(0, trips, lambda t, c, i0=first, n=size: run(i0 + n * t, n, c), carry)
                first, left = first + size * trips, left - size * trips
            dkt, dvt = carry
            dk_ref[pl.ds(st_j, tk), :] = (dkt[0:LANES, :].T * Q_SCALE).astype(BF16)
            dv_ref[pl.ds(st_j, tk), :] = dvt.T.astype(BF16)
            dck_ref[j] = dkt[LANES:LANES + 8, :]
            return 0

        lax.fori_loop(0, nb, kv_block, 0)

        def finish(b, _):
            acc = dqt_ref[b]
            dq_ref[pl.ds(pl.multiple_of(b * tq, tq), tq), :] = (acc[0:LANES, :].T * Q_SCALE).astype(BF16)
            dcq_ref[b] = acc[LANES:LANES + 8, :]
            return 0

        lax.fori_loop(0, nb, finish, 0)

    col = lambda off: pl.BlockSpec((s, LANES), lambda p: (0, off + p))
    sums = pl.BlockSpec((None, nb, 8, tq), lambda p: (p, 0, 0, 0))
    return pl.pallas_call(
        body,
        grid=(N_PAIRS,),
        in_specs=[col(0), col(N_PAIRS), col(2 * N_PAIRS), col(0), col(0),
                  pl.BlockSpec((None, nb, 2, tq), lambda p: (p, 0, 0, 0)),
                  pl.BlockSpec((None, 2, s), lambda p: (p, 0, 0)), _UNREAD],
        out_specs=[col(0), col(0), col(0), sums, sums],
        out_shape=[SDS((s, ATTN_W), BF16), SDS((s, ATTN_W), BF16), SDS((s, ATTN_W), BF16),
                   SDS((N_PAIRS, nb, 8, tq), F32), SDS((N_PAIRS, nb, 8, tq), F32)],
        scratch_shapes=[pltpu.VMEM((nb, rows_t, tq), F32), pltpu.VMEM((nb, 8, tq), F32),
                        pltpu.VMEM((2 * nb, tk, LANES), BF16), pltpu.VMEM((2 * nb, tq, LANES), BF16),
                        pltpu.VMEM((s, LANES), BF16), pltpu.VMEM((nb, rows_t, 2 * tq), BF16),
                        pltpu.VMEM((nb, rows_t, 2 * tk), BF16), pltpu.VMEM((nb, LANES, 2 * tq), BF16),
                        pltpu.VMEM((8, tk, tq), F32)],
        compiler_params=_cparams("arbitrary"),
        name="attn_bwd",
    )(qkv, qkv, qkv, attn_o, d_attn, rowb, c_rows, after)


def _forget_bwd(dc_t, fl_t, b_rows):
    rows = fl_t.shape[0]
    nb = rows // N_HEADS

    def body(dc_ref, fl_ref, b_ref, dfl_ref, db_ref):
        dc = dc_ref[...]
        lower = _iota2((LANES, LANES), 0) >= _iota2((LANES, LANES), 1)
        ones = jnp.ones((LANES, LANES), F32)
        rr, cc, same = _head_block_masks(rows, nb)
        dlf = _dot_sel(dc, lower) + _sel_dot(same & (cc > rr), _dot_sel(dc, ones))
        dfl = dlf / (1.0 + jnp.exp(fl_ref[...] + b_ref[...]))
        dfl_ref[...] = dfl
        shift = nb.bit_length() - 1
        hsel = lax.shift_right_logical(_iota2((N_HEADS, rows), 1), shift) == _iota2((N_HEADS, rows), 0)
        db_ref[...] = _sel_dot(hsel, _dot_sel(dfl, ones))

    return pl.pallas_call(body, out_shape=[SDS(fl_t.shape, F32), SDS((N_HEADS, LANES), F32)],
                          compiler_params=_cparams(), name="forget_bwd")(dc_t, fl_t, b_rows)


def _in_bwd(dq, dk, dv, du, dfl, w_in_t, x, r1, g1, dx1, after, *, tm):
    s = x.shape[0]
    pieces = ((0, ATTN_W), (ATTN_W, 2 * ATTN_W), (2 * ATTN_W, QKV_W), (U_OFF, F_OFF), (F_OFF, IN_PAD))

    def body(dq_ref, dk_ref, dv_ref, du_ref, df_ref, w_ref, x_ref, r_ref, g_ref, d_ref, _, dx_ref, dg1_ref):
        @pl.when(pl.program_id(0) == 0)
        def _():
            dg1_ref[...] = jnp.zeros_like(dg1_ref)

        dh = None
        for ref, (c0, c1) in zip((dq_ref, dk_ref, dv_ref, du_ref, df_ref), pieces):
            t = jnp.dot(ref[...], w_ref[c0:c1, :], preferred_element_type=F32)
            dh = t if dh is None else dh + t
        dx, dg1 = _norm_bwd(dh, x_ref[...], r_ref[...], g_ref[...], d_ref[...])
        dx_ref[...] = dx
        dg1_ref[...] += dg1

    row = lambda w: pl.BlockSpec((tm, w), lambda i: (i, 0))
    full = lambda a, b: pl.BlockSpec((a, b), lambda i: (0, 0))
    return pl.pallas_call(
        body,
        grid=(s // tm,),
        in_specs=[row(ATTN_W), row(ATTN_W), row(ATTN_W), row(POOL_W), row(LANES), full(IN_PAD, D_MODEL),
                  row(D_MODEL), row(1), full(1, D_MODEL), row(D_MODEL), _UNREAD],
        out_specs=[row(D_MODEL), full(1, D_MODEL)],
        out_shape=[SDS((s, D_MODEL), F32), SDS((1, D_MODEL), F32)],
        compiler_params=_cparams("arbitrary"),
        name="in_bwd",
    )(dq, dk, dv, du, dfl, w_in_t, x, r1, g1, dx1, after)


def _tiles(s):
    big = min(512, s)
    return dict(row=big, attn=min(256, s // 2), ff_rows=min(256, s), tall=min(1024, s))


def _local_step(x, tgt, p, weight, emit, started):
    s = x.shape[0]
    t = _tiles(s)
    tm, tq = t["row"], t["attn"]
    nb = s // LANES
    nqb = s // tq
    g1, g2, gf = p["norm1_g"], p["norm2_g"], p["final_g"].reshape(1, D_MODEL)
    w_pool, pool_scale = p["w_pool"][0], p["pool_scale"]

    h, r1 = _norm1(x, g1, started, tm=tm)
    w_in_t = weight("w_in", h)
    qkv, fl, pooled, pool_o = _in_proj_pool(h, w_in_t, w_pool, pool_scale, tm=t["tall"])
    fl_t = fl[:, :N_HEADS].T.reshape(N_HEADS * nb, LANES)
    b_rows = jnp.repeat(p["b_forget"].reshape(N_HEADS), nb).reshape(N_HEADS * nb, 1)
    c = _forget_cumsum(fl_t, b_rows).reshape(N_PAIRS, 2, s)
    c_rowblk = c.reshape(N_PAIRS, 2, nqb, tq).transpose(0, 2, 1, 3)
    attn_o, lse = _attn_fwd(qkv, c, tk=tq)
    lse = lse.reshape(N_PAIRS, nqb // 2, 2, 2, tq).transpose(0, 1, 3, 2, 4).reshape(N_PAIRS, nqb, 2, tq)
    w_out = weight("w_out", attn_o)
    wg_t, wu_t = weight("w_gate_up", attn_o)
    x1, h2, r2, gate, up, act = _out_gate_up(attn_o, pool_o, w_out, x, g2, wg_t, wu_t, tm=t["ff_rows"])
    wd = weight("w_down", act)
    dx2, loss_row, d_gf = _down_final(act, wd, x1, gf, tgt, tm=tm, sub=min(128, tm))

    dgate, dup = _swiglu_bwd(dx2, wd, gate, up, tm=t["ff_rows"], tn=D_FF)
    d_wd = _mm_tn_stacked([act], [D_FF], dx2, ts=t["tall"], name="grad_w_down")
    d_wg_t = _mm_tn_stacked([dgate], [D_FF], h2, ts=t["tall"], name="grad_w_gate")
    d_wu_t = _mm_tn_stacked([dup], [D_FF], h2, ts=t["tall"], name="grad_w_up")
    dx1, d_attn, du, d_g2, d_wpool, d_pscale = _mlp_in_pool_bwd(dgate, dup, wg_t, wu_t, w_out, x1, r2, g2, dx2, pooled,
                                                               w_pool, pool_scale, tm=t["ff_rows"])
    d_wo = _mm_tn_stacked([attn_o, pool_o], [ATTN_W, POOL_W], dx1, ts=t["tall"], name="grad_w_out")
    token = emit(("w_down", "w_gate", "w_up", "w_out"), (d_wd, d_wg_t, d_wu_t, d_wo))
    dq, dk, dv, dck, dcq = _attn_bwd(qkv, attn_o, d_attn, c_rowblk - lse, c, token, tq=tq)
    dc_t = (dcq - dck)[:, :, 0::4, :].transpose(0, 2, 1, 3).reshape(N_HEADS * nb, LANES)
    dfl_t, db = _forget_bwd(dc_t, fl_t, b_rows)
    dfl = jnp.pad(dfl_t.reshape(N_HEADS, s).T, ((0, 0), (0, LANES - N_HEADS))).astype(BF16)
    d_w_in_t = _mm_tn_stacked([dq, dk, dv, dfl, du], [ATTN_W, ATTN_W, ATTN_W, N_HEADS, POOL_W], h, ts=t["tall"],
                              name="grad_w_in",
                              windows=(N_DEV, IN_STEP, IN_WINDOW))
    dx, d_g1 = _in_bwd(dq, dk, dv, du, dfl, w_in_t, x, r1, g1, dx1, token, tm=tm)

    small = dict(norm1_g=d_g1, b_forget=db[:, 0].reshape(1, N_HEADS), w_pool=d_wpool, pool_scale=d_pscale,
                 norm2_g=d_g2, final_g=d_gf, w_in=d_w_in_t)
    return loss_row, dx, small


def _my_index():
    return 4 * lax.axis_index("x") + 2 * lax.axis_index("y") + lax.axis_index("c")


def _peer(k):
    pos = [lax.axis_index(a) for a in ("x", "y", "c")]
    flipped = tuple(1 - p if (k >> b) & 1 else p for p, b in zip(pos, (2, 1, 0)))
    return flipped, 4 * flipped[0] + 2 * flipped[1] + flipped[2]


_HBM = pl.BlockSpec(memory_space=pltpu.HBM)
_SEM = pl.BlockSpec(memory_space=pltpu.SEMAPHORE)
_DATAFLOW = pltpu.SideEffectType.DATAFLOW_SIDE_EFFECTING


ALL_PEERS = tuple(range(1, N_DEV))
SAME_CORE = (1, 2, 4, 6)


def _peer_copies(ins, lands, send_sems, recv_sems, scatter, peers, arrivals):
    me = _my_index()
    copies = []
    for w in range(len(ins)):
        for k in peers[w]:
            dev, idx = _peer(k)
            copies.append(pltpu.make_async_remote_copy(
                src_ref=ins[w].at[idx] if scatter[w] else ins[w], dst_ref=lands[w].at[idx if arrivals else me],
                send_sem=send_sems[w].at[k - 1], recv_sem=recv_sems[w].at[k - 1], device_id=dev, device_id_type=MESH))
    return copies


def _own_copies(ins, lands, send_sems, scatter):
    me = _my_index()
    return [pltpu.make_async_copy(ins[w].at[me] if scatter[w] else ins[w], lands[w].at[me], send_sems[w].at[N_DEV - 1])
            for w in range(len(ins))]


def _forward_copies(land, send_sems, recv_sems, arrivals):
    sibling, _ = _peer(1)
    copies = []
    for j, k in enumerate(SAME_CORE[1:]):
        src, dst = _peer(k)[1], _peer(k ^ 1 if arrivals else k)[1]
        copies.append(pltpu.make_async_remote_copy(
            src_ref=land.at[src], dst_ref=land.at[dst], send_sem=send_sems.at[j], recv_sem=recv_sems.at[j],
            device_id=sibling, device_id_type=MESH))
    return copies


def _forward_start(land, name):
    def body(land_ref, send_sems, recv_sems, land_thru, token):
        for cp in _forward_copies(land_ref, send_sems, recv_sems, False):
            cp.start()
        token[...] = jnp.zeros_like(token)

    sem = pltpu.SemaphoreType.DMA((len(SAME_CORE) - 1,))
    send, recv, thru, _ = pl.pallas_call(
        body,
        in_specs=[_HBM],
        out_specs=[_SEM, _SEM, _HBM, pl.BlockSpec(memory_space=pltpu.VMEM)],
        out_shape=[sem, sem, pltpu.HBM(land.shape, land.dtype), SDS((8, LANES), F32)],
        input_output_aliases={0: 2},
        compiler_params=pltpu.CompilerParams(has_side_effects=_DATAFLOW),
        name=name,
    )(land)
    return send, recv, thru


def _forward_wait(handle, after, name):
    def body(land_ref, send_sems, recv_sems, after_ref, land_out):
        for cp in _forward_copies(land_ref, send_sems, recv_sems, False):
            cp.wait_send()
        for cp in _forward_copies(land_ref, send_sems, recv_sems, True):
            cp.wait_recv()

    send, recv, land = handle
    return pl.pallas_call(
        body,
        in_specs=[_HBM, _SEM, _SEM, pl.BlockSpec(memory_space=pl.ANY)],
        out_specs=_HBM,
        out_shape=pltpu.HBM(land.shape, land.dtype),
        input_output_aliases={0: 0},
        compiler_params=pltpu.CompilerParams(has_side_effects=_DATAFLOW),
        name=name,
    )(land, send, recv, after)


def _exchange_start(arrays, scatter, name, peers=None, after=None):
    n = len(arrays)
    peers = peers or [ALL_PEERS] * n
    order = [] if after is None else [after]
    land_shapes = [(N_DEV,) + tuple(a.shape[1:] if sc else a.shape) for a, sc in zip(arrays, scatter)]

    def body(*refs):
        ins, lands = refs[:n], refs[n:2 * n]
        outs = refs[2 * n + len(order):]
        send_sems, recv_sems, token = outs[:n], outs[n:2 * n], outs[4 * n]
        for cp in _peer_copies(ins, lands, send_sems, recv_sems, scatter, peers, False):
            cp.start()
        for cp in _own_copies(ins, lands, send_sems, scatter):
            cp.start()
        token[...] = jnp.zeros_like(token)

    sends, recvs = pltpu.SemaphoreType.DMA((N_DEV,)), pltpu.SemaphoreType.DMA((N_DEV - 1,))
    outs = pl.pallas_call(
        body,
        in_specs=[_HBM] * (2 * n) + [_UNREAD] * len(order),
        out_specs=[_SEM] * (2 * n) + [_HBM] * (2 * n) + [pl.BlockSpec(memory_space=pltpu.VMEM)],
        out_shape=[sends] * n + [recvs] * n + [pltpu.HBM(a.shape, a.dtype) for a in arrays]
        + [pltpu.HBM(sh, a.dtype) for sh, a in zip(land_shapes, arrays)] + [SDS((8, LANES), F32)],
        input_output_aliases={i: 2 * n + i for i in range(2 * n)},
        compiler_params=pltpu.CompilerParams(has_side_effects=_DATAFLOW),
        name=name,
    )(*[pltpu.with_memory_space_constraint(a, pltpu.HBM) for a in arrays],
      *[pltpu.with_memory_space_constraint(lax.empty(sh, a.dtype), pltpu.HBM) for sh, a in zip(land_shapes, arrays)],
      *order)
    handles = [dict(send=outs[w], recv=outs[n + w], src=outs[2 * n + w], land=outs[3 * n + w], scatter=scatter[w],
                    peers=peers[w]) for w in range(n)]
    return handles, outs[4 * n]


def _exchange_wait(handles, after, name):
    n = len(handles)
    scatter, peers = [h["scatter"] for h in handles], [h["peers"] for h in handles]

    def body(*refs):
        ins, lands = refs[:n], refs[n:2 * n]
        send_sems, recv_sems = refs[2 * n:3 * n], refs[3 * n:4 * n]
        for cp in _peer_copies(ins, lands, send_sems, recv_sems, scatter, peers, False):
            cp.wait_send()
        for cp in _peer_copies(ins, lands, send_sems, recv_sems, scatter, peers, True):
            cp.wait_recv()
        for cp in _own_copies(ins, lands, send_sems, scatter):
            cp.wait()

    srcs, lands = [h["src"] for h in handles], [h["land"] for h in handles]
    outs = pl.pallas_call(
        body,
        in_specs=[_HBM] * (2 * n) + [_SEM] * (2 * n) + [pl.BlockSpec(memory_space=pl.ANY)],
        out_specs=[_HBM] * (2 * n),
        out_shape=[pltpu.HBM(a.shape, a.dtype) for a in srcs + lands],
        input_output_aliases={i: i for i in range(2 * n)},
        compiler_params=pltpu.CompilerParams(has_side_effects=_DATAFLOW),
        name=name,
    )(*srcs, *lands, *[h["send"] for h in handles], *[h["recv"] for h in handles], after)
    return outs[n:]


def _adamw(parts, w, m, v, name):
    rows, cols = w.shape
    tr = rows // 4 if rows % 32 == 0 else rows

    def body(p_ref, w_ref, m_ref, v_ref, g_ref, d_ref, mo_ref, vo_ref):
        g = p_ref[0].astype(F32)
        for d in range(1, N_DEV):
            g = g + p_ref[d].astype(F32)
        g_ref[...] = g
        d_ref[...], mo_ref[...], vo_ref[...] = _adam_update(g, w_ref[...], m_ref[...], v_ref[...])

    blk = pl.BlockSpec((tr, cols), lambda i: (i, 0))
    return pl.pallas_call(
        body,
        grid=(rows // tr,),
        in_specs=[pl.BlockSpec((N_DEV, tr, cols), lambda i: (0, i, 0)), blk, blk, blk],
        out_specs=[blk] * 4,
        out_shape=[SDS((rows, cols), F32)] * 4,
        compiler_params=_cparams("arbitrary"),
        name=name,
    )(parts, w, m, v)


def _adamw_dense(parts, w, m, v, name, *, rows, shift):
    _, window, cols = parts.shape
    per_row = cols // LANES

    def body(p_ref, w_ref, m_ref, v_ref, g_ref, d_ref, mo_ref, vo_ref, sum_ref):
        g = p_ref[0].astype(F32)
        for d in range(1, N_DEV):
            g = g + p_ref[d].astype(F32)
        sum_ref[...] = g
        me = _my_index()
        for j in range(N_DEV):
            @pl.when(me == j)
            def _(j=j):
                for c in range(per_row):
                    at = (pl.ds(c, rows, stride=per_row), slice(None))
                    gc = sum_ref[j * shift:j * shift + rows, c * LANES:(c + 1) * LANES]
                    g_ref[at] = gc
                    d_ref[at], mo_ref[at], vo_ref[at] = _adam_update(gc, w_ref[at], m_ref[at], v_ref[at])

    return pl.pallas_call(
        body,
        out_shape=[SDS(w.shape, F32)] * 4,
        scratch_shapes=[pltpu.VMEM((window, cols), F32)],
        compiler_params=_cparams(),
        name=name,
    )(parts, w, m, v)


_ROW_OF = dict(norm1_g=(0, D_MODEL), norm2_g=(1, D_MODEL), final_g=(2, D_MODEL), pool_scale=(3, POOL_W),
               b_forget=(4, N_HEADS), loss=(5, 1))


def _pack_rows(vals):
    rows = [jnp.pad(vals[n].reshape(1, width).astype(F32), ((0, 0), (0, D_MODEL - width)))
            for n, (_, width) in sorted(_ROW_OF.items(), key=lambda kv: kv[1][0])]
    return jnp.concatenate(rows + [jnp.zeros((8 - len(rows), D_MODEL), F32)], axis=0)


def _adam_update(g, w, m, v):
    m_new = ADAM_B1 * m + (1.0 - ADAM_B1) * g
    v_new = ADAM_B2 * v + (1.0 - ADAM_B2) * (g * g)
    m_hat = m_new / (1.0 - ADAM_B1 ** ADAM_STEP)
    v_hat = v_new / (1.0 - ADAM_B2 ** ADAM_STEP)
    return -ADAM_LR * (m_hat / (jnp.sqrt(v_hat) + ADAM_EPS) + ADAM_WD * w), m_new, v_new


def _adamw_replicated(parts_rows, parts_pool, w, m, v):
    names = ("norm1_g", "norm2_g", "final_g", "pool_scale", "b_forget", "w_pool")
    shapes = {n: ((len(POOL_WINDOWS), POOL_G, POOL_G) if n == "w_pool" else (1, _ROW_OF[n][1])) for n in names}

    def body(rows_ref, pool_ref, *refs):
        ins, outs = refs[:3 * len(names)], refs[3 * len(names):]

        def total(n):
            if n == "w_pool":
                pieces = [pool_ref[d] for d in range(N_DEV)]
            else:
                row, width = _ROW_OF[n]
                pieces = [rows_ref[d, row:row + 1, 0:width] for d in range(N_DEV)]
            g = pieces[0]
            for p in pieces[1:]:
                g = g + p
            return g

        outs[0][...] = total("loss")
        for k, n in enumerate(names):
            g = total(n)
            delta, m_new, v_new = _adam_update(g, ins[3 * k][...], ins[3 * k + 1][...], ins[3 * k + 2][...])
            for o_ref, val in zip(outs[1 + 4 * k:5 + 4 * k], (g, delta, m_new, v_new)):
                o_ref[...] = val

    args = [d[n].reshape(shapes[n]) for n in names for d in (w, m, v)]
    res = pl.pallas_call(
        body,
        out_shape=[SDS((1, 1), F32)] + [SDS(shapes[n], F32) for n in names for _ in range(4)],
        compiler_params=_cparams(),
        name="adamw_replicated",
    )(parts_rows, parts_pool, *args)
    return res[0], {n: [r.reshape(w[n].shape) for r in res[1 + 4 * k:5 + 4 * k]] for k, n in enumerate(names)}


def kernel(x, norm1_g, w_in, b_forget, w_pool, pool_scale, w_out, norm2_g, w_gate, w_up, w_down, final_g, loss_target, m_norm1_g, m_w_in, m_b_forget, m_w_pool, m_pool_scale, m_w_out, m_norm2_g, m_w_gate, m_w_up, m_w_down, m_final_g, v_norm1_g, v_w_in, v_b_forget, v_w_pool, v_pool_scale, v_w_out, v_norm2_g, v_w_gate, v_w_up, v_w_down, v_final_g):
    big = ("w_in", "w_out", "w_gate", "w_up", "w_down")
    order = ("norm1_g", "w_in", "b_forget", "w_pool", "pool_scale", "w_out", "norm2_g", "w_gate", "w_up", "w_down",
             "final_g")
    w = dict(norm1_g=norm1_g, w_in=w_in, b_forget=b_forget, w_pool=w_pool, pool_scale=pool_scale, w_out=w_out,
             norm2_g=norm2_g, w_gate=w_gate, w_up=w_up, w_down=w_down, final_g=final_g)
    m = dict(norm1_g=m_norm1_g, w_in=m_w_in, b_forget=m_b_forget, w_pool=m_w_pool, pool_scale=m_pool_scale,
             w_out=m_w_out, norm2_g=m_norm2_g, w_gate=m_w_gate, w_up=m_w_up, w_down=m_w_down, final_g=m_final_g)
    v = dict(norm1_g=v_norm1_g, w_in=v_w_in, b_forget=v_b_forget, w_pool=v_w_pool, pool_scale=v_pool_scale,
             w_out=v_w_out, norm2_g=v_norm2_g, w_gate=v_w_gate, w_up=v_w_up, w_down=v_w_down, final_g=v_final_g)

    flipped = ("w_in", "w_gate", "w_up")
    shard = lambda d, n: d[n][0].T if n in flipped else d[n][0]
    cast = lambda n: shard(w, n).astype(BF16)
    (first,), started = _exchange_start([cast("w_in")], [False], "gather_start_w_in", peers=[SAME_CORE])
    gather = dict(w_in=first)

    def gathered(names, after):
        return _exchange_wait([gather[n] for n in names], after, "gather_wait_" + names[0])

    def weight(name, after):
        if name == "w_in":
            handles, token = _exchange_start([cast(n) for n in big[1:]], [False] * len(big[1:]), "gather_start", after=after)
            gather.update(zip(big[1:], handles))
            forward = _forward_start(gathered(["w_in"], token)[0], "gather_forward_start")
            full = _forward_wait(forward, after, "gather_forward_wait").reshape(IN_W, D_MODEL)
            f0 = QKV_W + N_HEADS
            return jnp.concatenate([full[:QKV_W], full[f0:], full[QKV_W:f0],
                                    jnp.zeros((IN_PAD - IN_W, D_MODEL), BF16)], axis=0)
        if name == "w_out":
            return gathered(["w_out"], after)[0].reshape(D_MODEL, D_MODEL)
        if name == "w_gate_up":
            return [g.reshape(D_FF, D_MODEL) for g in gathered(["w_gate", "w_up"], after)]
        return gathered(["w_down"], after)[0].reshape(D_FF, D_MODEL)

    rows = lambda g: g if g.ndim == 3 else g.reshape(N_DEV, g.shape[0] // N_DEV, g.shape[1])
    sent = {}

    def emit(names, grads):
        handles, token = _exchange_start([rows(g) for g in grads], [True] * len(names), "grads_start_" + names[0])
        sent.update(zip(names, handles))
        return token

    loss_row, dx, small_grads = _local_step(x[0], loss_target[0], w, weight, emit, started)

    packed = _pack_rows(dict(small_grads, loss=0.5 / D_MODEL * jnp.sum(loss_row)))
    small_handles, after = _exchange_start([packed, small_grads["w_pool"], small_grads["w_in"]], [False, False, True],
                                           "grads_start_replicated")
    sent["w_in"] = small_handles.pop()

    outs = {}
    for name in ("w_down", "w_gate", "w_up", "w_out", "w_in"):
        (parts,) = _exchange_wait([sent[name]], after, "grads_wait_" + name)
        if name == "w_in":
            dense = lambda d: d[name].transpose(2, 0, 1).reshape(-1, LANES)
            outs[name] = _adamw_dense(parts, dense(w), dense(m), dense(v), "adamw_" + name, rows=IN_SHARD, shift=IN_SHIFT)
            after = outs[name][0]
            outs[name] = [a.reshape(-1, D_MODEL // LANES, LANES).transpose(1, 2, 0).reshape(1, D_MODEL, -1)
                          for a in outs[name]]
            continue
        outs[name] = _adamw(parts, shard(w, name), shard(m, name), shard(v, name), "adamw_" + name)
        after = outs[name][0]
        outs[name] = [(a.T if name in flipped else a)[None] for a in outs[name]]
    parts_rows, parts_pool = _exchange_wait(small_handles, after, "grads_wait_replicated")
    loss, small = _adamw_replicated(parts_rows, parts_pool, w, m, v)
    outs.update(small)

    return (loss.reshape(()), dx[None]) + tuple(outs[n][k] for k in range(4) for n in order)
```

```python
import jax
import jax.numpy as jnp
from jax import lax
from jax.experimental import pallas as pl
from jax.experimental.pallas import tpu as pltpu

F32 = jnp.float32
BF16 = jnp.bfloat16
SDS = jax.ShapeDtypeStruct

D_MODEL = 1024
ATTN_W = 512
N_HEADS = 8
HEAD_DIM = 64
Q_SCALE = HEAD_DIM ** -0.5
N_PAIRS = N_HEADS // 2
POOL_W = 512
POOL_WINDOWS = (2, 4, 8, 16)
POOL_G = 128
HALO = 16
IN_W = 3 * ATTN_W + N_HEADS + POOL_W
QKV_W = 3 * ATTN_W
U_OFF = QKV_W
F_OFF = QKV_W + POOL_W
IN_PAD = F_OFF + 128
D_FF = 2816
EPS = 1e-6
NEG = -1e30
N_DEV = 8
LANES = 128
BF16_ROWS = 16

IN_SHARD = IN_W // N_DEV
IN_STEP = IN_SHARD // BF16_ROWS * BF16_ROWS
IN_SHIFT = IN_SHARD - IN_STEP
IN_WINDOW = -(-((N_DEV - 1) * IN_SHIFT + IN_SHARD) // BF16_ROWS) * BF16_ROWS

ADAM_LR = 0.001
ADAM_B1 = 0.9
ADAM_B2 = 0.999
ADAM_EPS = 1e-08
ADAM_WD = 0.01
ADAM_STEP = 10

VMEM_LIMIT_BYTES = 56 * 1024 * 1024
MESH = pl.DeviceIdType.MESH
NT = (((1,), (1,)), ((), ()))
TN = (((0,), (0,)), ((), ()))


_UNREAD = pl.BlockSpec(memory_space=pl.ANY)


def _cparams(*sem):
    return pltpu.CompilerParams(dimension_semantics=sem or None, vmem_limit_bytes=VMEM_LIMIT_BYTES)


def _split3(a):
    hi = a.astype(BF16)
    r1 = a - hi.astype(F32)
    mid = r1.astype(BF16)
    lo = (r1 - mid.astype(F32)).astype(BF16)
    return hi, mid, lo


def _dot_sel(a, sel, dims=None):
    sb = sel.astype(BF16)
    if dims is None:
        return sum(jnp.dot(p, sb, preferred_element_type=F32) for p in _split3(a))
    return sum(lax.dot_general(p, sb, dims, preferred_element_type=F32) for p in _split3(a))


def _sel_dot(sel, a, dims=None):
    sb = sel.astype(BF16)
    if dims is None:
        return sum(jnp.dot(sb, p, preferred_element_type=F32) for p in _split3(a))
    return sum(lax.dot_general(sb, p, dims, preferred_element_type=F32) for p in _split3(a))


def _iota2(shape, dim):
    return lax.broadcasted_iota(jnp.int32, shape, dim)


UNROLLS = (8, 4, 2)
UNROLLS_FWD = (16,) + UNROLLS


def _shift_div(x, n):
    return lax.shift_right_logical(x, n.bit_length() - 1)


def _norm1(x, g1, after, *, tm):
    s = x.shape[0]

    def body(x_ref, g_ref, _, h_ref, r_ref):
        xv = x_ref[...]
        r = lax.rsqrt(jnp.mean(xv * xv, axis=-1, keepdims=True) + EPS)
        h_ref[...] = (xv * r * g_ref[...]).astype(BF16)
        r_ref[...] = r

    row = lambda w: pl.BlockSpec((tm, w), lambda i: (i, 0))
    return pl.pallas_call(
        body,
        grid=(s // tm,),
        in_specs=[row(D_MODEL), pl.BlockSpec((1, D_MODEL), lambda i: (0, 0)), _UNREAD],
        out_specs=[row(D_MODEL), row(1)],
        out_shape=[SDS((s, D_MODEL), BF16), SDS((s, 1), F32)],
        compiler_params=_cparams("arbitrary"),
        name="norm1",
    )(x, g1, after)


def _in_proj_pool(h, w_in_t, w_pool, pool_scale, *, tm):
    s = h.shape[0]

    def body(h_ref, w_ref, wp_ref, sc_ref, qkv_ref, fl_ref, pooled_ref, po_ref, tail_ref):
        i = pl.program_id(0)

        @pl.when(i == 0)
        def _():
            tail_ref[...] = jnp.zeros_like(tail_ref)

        hv = h_ref[...]
        uv = lax.dot_general(hv, w_ref[U_OFF:F_OFF, :], NT, preferred_element_type=F32)
        qkv_ref[...] = lax.dot_general(hv, w_ref[0:QKV_W, :], NT, preferred_element_type=F32).astype(BF16)
        fl_ref[...] = lax.dot_general(hv, w_ref[F_OFF:IN_PAD, :], NT, preferred_element_type=F32)
        ext = jnp.concatenate([tail_ref[...], uv], axis=0)
        tail_ref[...] = uv[tm - HALO:, :]
        for g, w in enumerate(POOL_WINDOWS):
            cols = slice(g * POOL_G, (g + 1) * POOL_G)
            acc = ext[:, cols]
            k = 1
            while k < w:
                acc = acc + pltpu.roll(acc, k, axis=0)
                k *= 2
            pooled = (acc[HALO:, :] / _pool_counts(i * tm, tm, w) - uv[:, cols]).astype(BF16)
            pooled_ref[:, cols] = pooled
            mixed = jnp.dot(pooled, wp_ref[g].astype(BF16), preferred_element_type=F32)
            po_ref[:, cols] = (mixed * sc_ref[:, cols]).astype(BF16)

    row = lambda w: pl.BlockSpec((tm, w), lambda i: (i, 0))
    return pl.pallas_call(
        body,
        grid=(s // tm,),
        in_specs=[row(D_MODEL), pl.BlockSpec((IN_PAD, D_MODEL), lambda i: (0, 0)),
                  pl.BlockSpec((len(POOL_WINDOWS), POOL_G, POOL_G), lambda i: (0, 0, 0)),
                  pl.BlockSpec((1, POOL_W), lambda i: (0, 0))],
        out_specs=[row(QKV_W), row(LANES), row(POOL_W), row(POOL_W)],
        out_shape=[SDS((s, QKV_W), BF16), SDS((s, LANES), F32), SDS((s, POOL_W), BF16), SDS((s, POOL_W), BF16)],
        scratch_shapes=[pltpu.VMEM((HALO, POOL_W), F32)],
        compiler_params=_cparams("arbitrary"),
        name="in_proj_pool",
    )(h, w_in_t, w_pool, pool_scale)


def _head_block_masks(rows, nb):
    shift = nb.bit_length() - 1
    rr, cc = _iota2((rows, rows), 0), _iota2((rows, rows), 1)
    same = lax.shift_right_logical(rr, shift) == lax.shift_right_logical(cc, shift)
    return rr, cc, same


def _forget_cumsum(fl_t, b_rows):
    rows = fl_t.shape[0]
    nb = rows // N_HEADS

    def body(fl_ref, b_ref, c_ref):
        z = fl_ref[...] + b_ref[...]
        lf = jnp.minimum(z, 0.0) - jnp.log1p(jnp.exp(-jnp.abs(z)))
        upper = _iota2((LANES, LANES), 0) <= _iota2((LANES, LANES), 1)
        within = _dot_sel(lf, upper)
        tot = _dot_sel(lf, jnp.ones((LANES, LANES), F32))
        rr, cc, same = _head_block_masks(rows, nb)
        c_ref[...] = within + _sel_dot(same & (cc < rr), tot)

    return pl.pallas_call(body, out_shape=SDS(fl_t.shape, F32), compiler_params=_cparams(), name="forget_cumsum")(
        fl_t, b_rows)


BIAS_LANES = 3


def _augment(t, h, bias, col_first):
    n = t.shape[0]
    lane = _iota2((n, LANES), 1)
    own = (lane < HEAD_DIM) if h == 0 else (lane >= HEAD_DIM)
    b0 = HEAD_DIM if h == 0 else 0
    c0, o0 = (b0, b0 + BIAS_LANES) if col_first else (b0 + BIAS_LANES, b0)
    x = 0.0
    if bias is not None:
        row = _iota2((BF16_ROWS, n), 0)
        pieces = jnp.zeros((BF16_ROWS, n), F32)
        for off, piece in enumerate(_split3(bias)):
            pieces = jnp.where(row == off, piece.astype(F32), pieces)
        r, ln = _iota2((BF16_ROWS, LANES), 0), _iota2((BF16_ROWS, LANES), 1)
        place = jnp.where(r < BIAS_LANES, jnp.where(ln == c0 + r, 1.0, 0.0), 0.0).astype(BF16)
        x = lax.dot_general(pieces.astype(BF16), place, TN, preferred_element_type=F32)
    x = jnp.where(own, t, x)
    x = jnp.where((lane >= o0) & (lane < o0 + BIAS_LANES), 1.0, x)
    return x.astype(BF16)


def _attn_fwd(qkv, c_rows, *, tk):
    s = qkv.shape[0]
    tq = 2 * tk
    nb = s // tk

    def body(q_ref, k_ref, v_ref, cq_ref, ck_ref, o_ref, lse_ref, kp_ref, vt_ref, st_ref):
        i = pl.program_id(1)

        @pl.when(i == 0)
        def _():
            def prep(jb, _):
                st = pl.multiple_of(jb * tk, tk)
                k2 = k_ref[pl.ds(st, tk), :].astype(F32)
                ck = ck_ref[:, pl.ds(st, tk)]
                for h in range(2):
                    kp_ref[h * nb + jb] = _augment(k2, h, -ck[h:h + 1, :], True)
                vt_ref[jb] = v_ref[pl.ds(st, tk), :].astype(F32).T.astype(BF16)
                return 0

            lax.fori_loop(0, nb, prep, 0)

        qs = q_ref[...].astype(F32) * Q_SCALE
        cq = cq_ref[...]
        qp = [_augment(qs, h, cq[h:h + 1, :], False) for h in range(2)]

        def logits(j):
            return tuple(lax.dot_general(kp_ref[h * nb + j], qp[h], NT, preferred_element_type=F32) for h in range(2))

        def softmax_pv(j, slot, stats, masked):
            out = []
            for h in range(2):
                m, l, acc = stats[h]
                st = st_ref[2 * slot + h]
                if masked:
                    st = jnp.where(j * tk + _iota2((tk, tq), 0) <= i * tq + _iota2((tk, tq), 1), st, NEG)
                m_new = jnp.maximum(m, jnp.max(st, axis=0, keepdims=True))
                alpha = jnp.exp(m - m_new)
                p = jnp.exp(st - m_new)
                l = alpha * l + jnp.sum(p, axis=0, keepdims=True)
                vt = vt_ref[j, h * HEAD_DIM:(h + 1) * HEAD_DIM, :]
                acc = alpha * acc + jnp.dot(vt, p.astype(BF16), preferred_element_type=F32)
                out.append((m_new, l, acc))
            return tuple(out)

        def put(slot, j):
            for h, st in enumerate(logits(j)):
                st_ref[2 * slot + h] = st

        def run(j0, steps, stats):
            for d in range(steps):
                put(1 - d % 2, j0 + d + 1)
                stats = softmax_pv(j0 + d, d % 2, stats, False)
            return stats

        init = tuple((jnp.full((1, tq), NEG, F32), jnp.zeros((1, tq), F32), jnp.zeros((HEAD_DIM, tq), F32))
                     for _ in range(2))
        put(0, 0)
        first, left, stats = 0, 2 * i, init
        for size in UNROLLS_FWD:
            trips = _shift_div(left, size)
            stats = lax.fori_loop(0, trips, lambda t, st, j0=first, n=size: run(j0 + n * t, n, st), stats)
            first, left = first + size * trips, left - size * trips
        put(1, 2 * i + 1)
        stats = softmax_pv(2 * i, 0, stats, True)
        (ma, la, acca), (mb, lb, accb) = softmax_pv(2 * i + 1, 1, stats, True)
        o_ref[...] = jnp.concatenate([acca / la, accb / lb], axis=0).T.astype(BF16)
        lse_ref[...] = jnp.where(_iota2((2, tq), 0) == 0, ma + jnp.log(la), mb + jnp.log(lb))

    return pl.pallas_call(
        body,
        grid=(N_PAIRS, s // tq),
        in_specs=[
            pl.BlockSpec((tq, LANES), lambda p, i: (i, p)),
            pl.BlockSpec((s, LANES), lambda p, i: (0, N_PAIRS + p)),
            pl.BlockSpec((s, LANES), lambda p, i: (0, 2 * N_PAIRS + p)),
            pl.BlockSpec((None, 2, tq), lambda p, i: (p, 0, i)),
            pl.BlockSpec((None, 2, s), lambda p, i: (p, 0, 0)),
        ],
        out_specs=[
            pl.BlockSpec((tq, LANES), lambda p, i: (i, p)),
            pl.BlockSpec((None, None, 2, tq), lambda p, i: (p, i, 0, 0)),
        ],
        out_shape=[SDS((s, ATTN_W), BF16), SDS((N_PAIRS, s // tq, 2, tq), F32)],
        scratch_shapes=[pltpu.VMEM((2 * nb, tk, LANES), BF16), pltpu.VMEM((nb, LANES, tk), BF16),
                        pltpu.VMEM((4, tk, tq), F32)],
        compiler_params=_cparams("arbitrary", "arbitrary"),
        name="attn_fwd",
    )(qkv, qkv, qkv, c_rows, c_rows)


def _pool_counts(row0, tm, w):
    t = row0 + _iota2((tm, 1), 0)
    return jnp.minimum(t + 1, w).astype(F32)


def _out_gate_up(attn_o, pool_o, w_out, x, g2, wg_t, wu_t, *, tm):
    s = x.shape[0]

    def body(a_ref, p_ref, wo_ref, x_ref, g_ref, wg_ref, wu_ref, x1_ref, h2_ref, r_ref, gate_ref, up_ref, act_ref):
        x1 = (x_ref[...] + jnp.dot(a_ref[...], wo_ref[0:ATTN_W, :], preferred_element_type=F32)
              + jnp.dot(p_ref[...], wo_ref[ATTN_W:, :], preferred_element_type=F32))
        r = lax.rsqrt(jnp.mean(x1 * x1, axis=-1, keepdims=True) + EPS)
        x1_ref[...] = x1
        r_ref[...] = r
        h2 = (x1 * r * g_ref[...]).astype(BF16)
        h2_ref[...] = h2
        gate = lax.dot_general(h2, wg_ref[...], NT, preferred_element_type=F32)
        up = lax.dot_general(h2, wu_ref[...], NT, preferred_element_type=F32)
        gate_ref[...] = gate.astype(BF16)
        up_ref[...] = up.astype(BF16)
        act_ref[...] = (gate * jax.nn.sigmoid(gate) * up).astype(BF16)

    row = lambda w: pl.BlockSpec((tm, w), lambda i: (i, 0))
    full = lambda a, b: pl.BlockSpec((a, b), lambda i: (0, 0))
    return pl.pallas_call(
        body,
        grid=(s // tm,),
        in_specs=[row(ATTN_W), row(POOL_W), full(D_MODEL, D_MODEL), row(D_MODEL), full(1, D_MODEL),
                  full(D_FF, D_MODEL), full(D_FF, D_MODEL)],
        out_specs=[row(D_MODEL), row(D_MODEL), row(1), row(D_FF), row(D_FF), row(D_FF)],
        out_shape=[SDS((s, D_MODEL), F32), SDS((s, D_MODEL), BF16), SDS((s, 1), F32), SDS((s, D_FF), BF16),
                   SDS((s, D_FF), BF16), SDS((s, D_FF), BF16)],
        compiler_params=_cparams("arbitrary"),
        name="out_gate_up",
    )(attn_o, pool_o, w_out, x, g2, wg_t, wu_t)


def _staggered(n, start, finish):
    pending = start(0)
    for k in range(n):
        following = start(k + 1) if k + 1 < n else None
        finish(k, pending)
        pending = following


def _down_final(act, wd, x1, gf, tgt, *, tm, sub):
    s = x1.shape[0]

    def body(a_ref, w_ref, x1_ref, g_ref, t_ref, dx2_ref, loss_ref, dgf_ref):
        @pl.when(pl.program_id(0) == 0)
        def _():
            loss_ref[...] = jnp.zeros_like(loss_ref)
            dgf_ref[...] = jnp.zeros_like(dgf_ref)

        g = g_ref[...]

        def matmul(k):
            return jnp.dot(a_ref[k * sub:(k + 1) * sub, :], w_ref[...], preferred_element_type=F32)

        def rest(k, mm):
            rows = slice(k * sub, (k + 1) * sub)
            x2 = x1_ref[rows, :] + mm
            r = lax.rsqrt(jnp.mean(x2 * x2, axis=-1, keepdims=True) + EPS)
            xn = x2 * r
            diff = xn * g - t_ref[rows, :]
            loss_ref[...] += jnp.sum(diff * diff, axis=0, keepdims=True)
            dy = diff * (1.0 / D_MODEL)
            dgf_ref[...] += jnp.sum(dy * xn, axis=0, keepdims=True)
            dxn = dy * g
            dx2_ref[rows, :] = r * (dxn - xn * jnp.mean(dxn * xn, axis=-1, keepdims=True))

        _staggered(tm // sub, matmul, rest)

    row = lambda w: pl.BlockSpec((tm, w), lambda i: (i, 0))
    full = lambda a, b: pl.BlockSpec((a, b), lambda i: (0, 0))
    return pl.pallas_call(
        body,
        grid=(s // tm,),
        in_specs=[row(D_FF), full(D_FF, D_MODEL), row(D_MODEL), full(1, D_MODEL), row(D_MODEL)],
        out_specs=[row(D_MODEL), full(1, D_MODEL), full(1, D_MODEL)],
        out_shape=[SDS((s, D_MODEL), F32), SDS((1, D_MODEL), F32), SDS((1, D_MODEL), F32)],
        compiler_params=_cparams("arbitrary"),
        name="down_final",
    )(act, wd, x1, gf, tgt)


def _swiglu_bwd(dx2, wd, gate, up, *, tm, tn):
    s = dx2.shape[0]

    def body(d_ref, w_ref, gate_ref, up_ref, dgate_ref, dup_ref):
        dact = lax.dot_general(d_ref[...].astype(BF16), w_ref[...], NT, preferred_element_type=F32)
        gate = gate_ref[...].astype(F32)
        sg = jax.nn.sigmoid(gate)
        dup_ref[...] = (dact * (gate * sg)).astype(BF16)
        dgate_ref[...] = (dact * up_ref[...].astype(F32) * (sg * (1.0 + gate * (1.0 - sg)))).astype(BF16)

    ospec = pl.BlockSpec((tm, tn), lambda c, r: (r, c))
    return pl.pallas_call(
        body,
        grid=(D_FF // tn, s // tm),
        in_specs=[pl.BlockSpec((tm, D_MODEL), lambda c, r: (r, 0)), pl.BlockSpec((tn, D_MODEL), lambda c, r: (c, 0)),
                  ospec, ospec],
        out_specs=[ospec, ospec],
        out_shape=[SDS((s, D_FF), BF16), SDS((s, D_FF), BF16)],
        compiler_params=_cparams("arbitrary", "arbitrary"),
        name="swiglu_bwd",
    )(dx2, wd, gate, up)


def _mm_tn_stacked(as_, rows, b, *, ts, name, windows=None):
    s, nb_ = b.shape
    n = len(as_)
    offsets = [sum(rows[:i]) for i in range(n)]
    total = sum(rows)
    if windows is None:
        acc_rows, out_shape = total, (total, nb_)
    else:
        count, step, size = windows
        acc_rows, out_shape = max(total, (count - 1) * step + size), (count, size, nb_)

    def body(*refs):
        a_refs, b_ref, o_ref, acc_ref = refs[:n], refs[n], refs[n + 1], refs[n + 2]
        k = pl.program_id(0)

        @pl.when(k == 0)
        def _():
            acc_ref[...] = jnp.zeros_like(acc_ref)

        bv = b_ref[...].astype(BF16)
        for a_ref, off, cnt in zip(a_refs, offsets, rows):
            part = lax.dot_general(a_ref[...].astype(BF16), bv, TN, preferred_element_type=F32)
            acc_ref[off:off + cnt, :] += part[0:cnt, :]

        @pl.when(k == s // ts - 1)
        def _():
            if windows is None:
                o_ref[...] = acc_ref[...].astype(BF16)
            else:
                for d in range(count):
                    o_ref[d] = acc_ref[d * step:d * step + size, :].astype(BF16)

    return pl.pallas_call(
        body,
        grid=(s // ts,),
        in_specs=[pl.BlockSpec((ts, a.shape[1]), lambda k: (k, 0)) for a in as_] + [pl.BlockSpec((ts, nb_), lambda k: (k, 0))],
        out_specs=pl.BlockSpec(out_shape, lambda k: (0,) * len(out_shape)),
        out_shape=SDS(out_shape, BF16),
        scratch_shapes=[pltpu.VMEM((acc_rows, nb_), F32)],
        compiler_params=_cparams("arbitrary"),
        name=name,
    )(*as_, b)


def _norm_bwd(dh, x, r, g, dres):
    xn = x * r
    dxn = dh * g
    dx = dres + r * (dxn - xn * jnp.mean(dxn * xn, axis=-1, keepdims=True))
    return dx, jnp.sum(dh * xn, axis=0, keepdims=True)


def _mlp_in_pool_bwd(dgate, dup, wg_t, wu_t, w_out, x1, r2, g2, dx2, pooled, w_pool, pool_scale, *, tm):
    s = x1.shape[0]
    nt = s // tm
    ng = len(POOL_WINDOWS)

    def body(dg_ref, dup_ref, wg_ref, wu_ref, wo_ref, x_ref, r_ref, g_ref, d_ref, p_ref, w_ref, sc_ref,
             dx1_ref, dattn_ref, du_ref, dg2_ref, dw_ref, dsc_ref, head_ref):
        i = pl.program_id(0)

        @pl.when(i == 0)
        def _():
            dg2_ref[...] = jnp.zeros_like(dg2_ref)
            head_ref[...] = jnp.zeros_like(head_ref)
            dw_ref[...] = jnp.zeros_like(dw_ref)
            dsc_ref[...] = jnp.zeros_like(dsc_ref)

        dh2 = (jnp.dot(dg_ref[...], wg_ref[...], preferred_element_type=F32)
               + jnp.dot(dup_ref[...], wu_ref[...], preferred_element_type=F32))
        dx1, dg2 = _norm_bwd(dh2, x_ref[...], r_ref[...], g_ref[...], d_ref[...])
        dg2_ref[...] += dg2
        dx1_ref[...] = dx1
        dmix = lax.dot_general(dx1.astype(BF16), wo_ref[...], NT, preferred_element_type=F32)
        dattn_ref[...] = dmix[:, 0:ATTN_W]
        row0 = (nt - 1 - i) * tm
        for g, w in enumerate(POOL_WINDOWS):
            cols = slice(g * POOL_G, (g + 1) * POOL_G)
            wb = w_ref[g].astype(BF16)
            pooled_g = p_ref[:, cols]
            dpo = dmix[:, ATTN_W + g * POOL_G:ATTN_W + (g + 1) * POOL_G]
            mixed = jnp.dot(pooled_g, wb, preferred_element_type=F32)
            dsc_ref[:, cols] += jnp.sum(dpo * mixed, axis=0, keepdims=True)
            dmp = (dpo * sc_ref[:, cols]).astype(BF16)
            dw_ref[g] += lax.dot_general(pooled_g, dmp, TN, preferred_element_type=F32)
            dpooled = lax.dot_general(dmp, wb, NT, preferred_element_type=F32)
            a = dpooled / _pool_counts(row0, tm, w)
            acc = jnp.concatenate([a, head_ref[:, cols]], axis=0)
            head_ref[:, cols] = a[0:HALO, :]
            k = 1
            while k < w:
                acc = acc + pltpu.roll(acc, tm + HALO - k, axis=0)
                k *= 2
            du_ref[:, cols] = (acc[0:tm, :] - dpooled).astype(BF16)

    row = lambda w: pl.BlockSpec((tm, w), lambda i: (nt - 1 - i, 0))
    full = lambda a, b: pl.BlockSpec((a, b), lambda i: (0, 0))
    pool_w = pl.BlockSpec((ng, POOL_G, POOL_G), lambda i: (0, 0, 0))
    return pl.pallas_call(
        body,
        grid=(nt,),
        in_specs=[row(D_FF), row(D_FF), full(D_FF, D_MODEL), full(D_FF, D_MODEL), full(D_MODEL, D_MODEL),
                  row(D_MODEL), row(1), full(1, D_MODEL), row(D_MODEL), row(POOL_W), pool_w, full(1, POOL_W)],
        out_specs=[row(D_MODEL), row(ATTN_W), row(POOL_W), full(1, D_MODEL), pool_w, full(1, POOL_W)],
        out_shape=[SDS((s, D_MODEL), F32), SDS((s, ATTN_W), F32), SDS((s, POOL_W), BF16), SDS((1, D_MODEL), F32),
                   SDS((ng, POOL_G, POOL_G), F32), SDS((1, POOL_W), F32)],
        scratch_shapes=[pltpu.VMEM((HALO, POOL_W), F32)],
        compiler_params=_cparams("arbitrary"),
        name="mlp_in_pool_bwd",
    )(dgate, dup, wg_t, wu_t, w_out, x1, r2, g2, dx2, pooled, w_pool, pool_scale)


SUM_ROWS = 16


def _heads_t(t):
    n = t.shape[0]
    lane = _iota2((n, LANES), 1)
    tf = t.astype(F32)
    halves = jnp.concatenate([jnp.where(lane < HEAD_DIM, tf, 0.0).T, jnp.where(lane < HEAD_DIM, 0.0, tf).T], axis=1)
    r, c = _iota2((SUM_ROWS, 2 * n), 0), _iota2((SUM_ROWS, 2 * n), 1)
    ones = jnp.where(((r == 0) & (c < n)) | ((r == 4) & (c >= n)), 1.0, 0.0)
    return jnp.concatenate([halves, ones], axis=0).astype(BF16)


def _attn_bwd(qkv, attn_o, d_attn, rowb, c_rows, after, *, tq):
    s = qkv.shape[0]
    tk = tq
    nb = s // tq
    rows_t = LANES + SUM_ROWS

    def body(q_ref, k_ref, v_ref, o_ref, do_ref, rowb_ref, ck_ref, _, dq_ref, dk_ref, dv_ref, dck_ref, dcq_ref,
             dqt_ref, delta_ref, kp_ref, qp_ref, dob_ref, qt_ref, kt_ref, dot_ref, front_ref):
        lane = _iota2((tq, LANES), 1)
        lo = lane < HEAD_DIM
        first = _iota2((8, LANES), 1) < HEAD_DIM
        sel = jnp.where(_iota2((8, LANES), 0) < 4, jnp.where(first, 1.0, 0.0), jnp.where(first, 0.0, 1.0))

        def prep(b, _):
            st = pl.multiple_of(b * tq, tq)
            do2 = do_ref[pl.ds(st, tq), :]
            delta_ref[b] = _sel_dot(sel, do2 * o_ref[pl.ds(st, tq), :].astype(F32), NT)
            dob_ref[pl.ds(st, tq), :] = do2.astype(BF16)
            dqt_ref[b] = jnp.zeros((rows_t, tq), F32)
            k2 = k_ref[pl.ds(st, tq), :].astype(F32)
            q2 = q_ref[pl.ds(st, tq), :].astype(F32)
            ck = ck_ref[:, pl.ds(st, tq)]
            for h in range(2):
                kp_ref[h * nb + b] = _augment(k2, h, -ck[h:h + 1, :], True)
                qp_ref[h * nb + b] = _augment(q2 * Q_SCALE, h, None, False)
            qt_ref[b] = _heads_t(q2)
            kt_ref[b] = _heads_t(k2)
            dot_ref[b] = _heads_t(do2)[0:LANES, :]
            return 0

        lax.fori_loop(0, nb, prep, 0)

        def split(t):
            z = jnp.zeros_like(t)
            return jnp.where(lo, t, z), jnp.where(lo, z, t)

        def kv_block(j, _):
            st_j = pl.multiple_of(j * tk, tk)
            vs = split(v_ref[pl.ds(st_j, tk), :])
            kt = kt_ref[j]

            def stage(i, slot):
                ic = jnp.minimum(i, nb - 1)
                do2 = dob_ref[pl.ds(pl.multiple_of(ic * tq, tq), tq), :]
                for h in range(2):
                    front_ref[4 * slot + h] = lax.dot_general(kp_ref[h * nb + j], qp_ref[h * nb + ic], NT,
                                                              preferred_element_type=F32)
                    front_ref[4 * slot + 2 + h] = lax.dot_general(vs[h], do2, NT, preferred_element_type=F32)

            def q_block(i, slot, carry, diagonal):
                dkt, dvt = carry
                ic = jnp.minimum(i, nb - 1)
                rb = rowb_ref[ic] + jnp.where(i < nb, 0.0, NEG)
                dl = delta_ref[ic]
                pts, dsts = [], []
                for h in range(2):
                    st = front_ref[4 * slot + h] + rb[h:h + 1, :]
                    if diagonal:
                        st = jnp.where(_iota2((tk, tq), 0) <= _iota2((tk, tq), 1), st, NEG)
                    pt = jnp.exp(st)
                    pts.append(pt.astype(BF16))
                    dsts.append((pt * (front_ref[4 * slot + 2 + h] - dl[4 * h:4 * h + 1, :])).astype(BF16))
                dvt = dvt + lax.dot_general(dot_ref[ic], jnp.concatenate(pts, axis=1), NT, preferred_element_type=F32)
                dkt = dkt + lax.dot_general(qt_ref[ic], jnp.concatenate(dsts, axis=1), NT, preferred_element_type=F32)
                dqt_ref[ic] += jnp.dot(kt, jnp.concatenate(dsts, axis=0), preferred_element_type=F32)
                return dkt, dvt

            def run(i0, steps, carry):
                for d in range(steps):
                    stage(i0 + d + 1, d % 2)
                    carry = q_block(i0 + d, 1 - d % 2, carry, False)
                return carry

            stage(j, 0)
            stage(j + 1, 1)
            carry = q_block(j, 0, (jnp.zeros((rows_t, tk), F32), jnp.zeros((LANES, tk), F32)), True)
            first, left = j + 1, nb - 1 - j
            for size in UNROLLS:
                trips = _shift_div(left + 1 if size == UNROLLS[-1] else left, size)
                carry = lax.fori_loop(0, trips, lambda t, c, i0=first, n=size: run(i0 + n * t, n, c), carry)
                first, left = first + size * trips, left - size * trips
            dkt, dvt = carry
            dk_ref[pl.ds(st_j, tk), :] = (dkt[0:LANES, :].T * Q_SCALE).astype(BF16)
            dv_ref[pl.ds(st_j, tk), :] = dvt.T.astype(BF16)
            dck_ref[j] = dkt[LANES:LANES + 8, :]
            return 0

        lax.fori_loop(0, nb, kv_block, 0)

        def finish(b, _):
            acc = dqt_ref[b]
            dq_ref[pl.ds(pl.multiple_of(b * tq, tq), tq), :] = (acc[0:LANES, :].T * Q_SCALE).astype(BF16)
            dcq_ref[b] = acc[LANES:LANES + 8, :]
            return 0

        lax.fori_loop(0, nb, finish, 0)

    col = lambda off: pl.BlockSpec((s, LANES), lambda p: (0, off + p))
    sums = pl.BlockSpec((None, nb, 8, tq), lambda p: (p, 0, 0, 0))
    return pl.pallas_call(
        body,
        grid=(N_PAIRS,),
        in_specs=[col(0), col(N_PAIRS), col(2 * N_PAIRS), col(0), col(0),
                  pl.BlockSpec((None, nb, 2, tq), lambda p: (p, 0, 0, 0)),
                  pl.BlockSpec((None, 2, s), lambda p: (p, 0, 0)), _UNREAD],
        out_specs=[col(0), col(0), col(0), sums, sums],
        out_shape=[SDS((s, ATTN_W), BF16), SDS((s, ATTN_W), BF16), SDS((s, ATTN_W), BF16),
                   SDS((N_PAIRS, nb, 8, tq), F32), SDS((N_PAIRS, nb, 8, tq), F32)],
        scratch_shapes=[pltpu.VMEM((nb, rows_t, tq), F32), pltpu.VMEM((nb, 8, tq), F32),
                        pltpu.VMEM((2 * nb, tk, LANES), BF16), pltpu.VMEM((2 * nb, tq, LANES), BF16),
                        pltpu.VMEM((s, LANES), BF16), pltpu.VMEM((nb, rows_t, 2 * tq), BF16),
                        pltpu.VMEM((nb, rows_t, 2 * tk), BF16), pltpu.VMEM((nb, LANES, 2 * tq), BF16),
                        pltpu.VMEM((8, tk, tq), F32)],
        compiler_params=_cparams("arbitrary"),
        name="attn_bwd",
    )(qkv, qkv, qkv, attn_o, d_attn, rowb, c_rows, after)


def _forget_bwd(dc_t, fl_t, b_rows):
    rows = fl_t.shape[0]
    nb = rows // N_HEADS

    def body(dc_ref, fl_ref, b_ref, dfl_ref, db_ref):
        dc = dc_ref[...]
        lower = _iota2((LANES, LANES), 0) >= _iota2((LANES, LANES), 1)
        ones = jnp.ones((LANES, LANES), F32)
        rr, cc, same = _head_block_masks(rows, nb)
        dlf = _dot_sel(dc, lower) + _sel_dot(same & (cc > rr), _dot_sel(dc, ones))
        dfl = dlf / (1.0 + jnp.exp(fl_ref[...] + b_ref[...]))
        dfl_ref[...] = dfl
        shift = nb.bit_length() - 1
        hsel = lax.shift_right_logical(_iota2((N_HEADS, rows), 1), shift) == _iota2((N_HEADS, rows), 0)
        db_ref[...] = _sel_dot(hsel, _dot_sel(dfl, ones))

    return pl.pallas_call(body, out_shape=[SDS(fl_t.shape, F32), SDS((N_HEADS, LANES), F32)],
                          compiler_params=_cparams(), name="forget_bwd")(dc_t, fl_t, b_rows)


def _in_bwd(dq, dk, dv, du, dfl, w_in_t, x, r1, g1, dx1, after, *, tm):
    s = x.shape[0]
    pieces = ((0, ATTN_W), (ATTN_W, 2 * ATTN_W), (2 * ATTN_W, QKV_W), (U_OFF, F_OFF), (F_OFF, IN_PAD))

    def body(dq_ref, dk_ref, dv_ref, du_ref, df_ref, w_ref, x_ref, r_ref, g_ref, d_ref, _, dx_ref, dg1_ref):
        @pl.when(pl.program_id(0) == 0)
        def _():
            dg1_ref[...] = jnp.zeros_like(dg1_ref)

        dh = None
        for ref, (c0, c1) in zip((dq_ref, dk_ref, dv_ref, du_ref, df_ref), pieces):
            t = jnp.dot(ref[...], w_ref[c0:c1, :], preferred_element_type=F32)
            dh = t if dh is None else dh + t
        dx, dg1 = _norm_bwd(dh, x_ref[...], r_ref[...], g_ref[...], d_ref[...])
        dx_ref[...] = dx
        dg1_ref[...] += dg1

    row = lambda w: pl.BlockSpec((tm, w), lambda i: (i, 0))
    full = lambda a, b: pl.BlockSpec((a, b), lambda i: (0, 0))
    return pl.pallas_call(
        body,
        grid=(s // tm,),
        in_specs=[row(ATTN_W), row(ATTN_W), row(ATTN_W), row(POOL_W), row(LANES), full(IN_PAD, D_MODEL),
                  row(D_MODEL), row(1), full(1, D_MODEL), row(D_MODEL), _UNREAD],
        out_specs=[row(D_MODEL), full(1, D_MODEL)],
        out_shape=[SDS((s, D_MODEL), F32), SDS((1, D_MODEL), F32)],
        compiler_params=_cparams("arbitrary"),
        name="in_bwd",
    )(dq, dk, dv, du, dfl, w_in_t, x, r1, g1, dx1, after)


def _tiles(s):
    big = min(512, s)
    return dict(row=big, attn=min(256, s // 2), ff_rows=min(256, s), tall=min(1024, s))


def _local_step(x, tgt, p, weight, emit, started):
    s = x.shape[0]
    t = _tiles(s)
    tm, tq = t["row"], t["attn"]
    nb = s // LANES
    nqb = s // tq
    g1, g2, gf = p["norm1_g"], p["norm2_g"], p["final_g"].reshape(1, D_MODEL)
    w_pool, pool_scale = p["w_pool"][0], p["pool_scale"]

    h, r1 = _norm1(x, g1, started, tm=tm)
    w_in_t = weight("w_in", h)
    qkv, fl, pooled, pool_o = _in_proj_pool(h, w_in_t, w_pool, pool_scale, tm=t["tall"])
    fl_t = fl[:, :N_HEADS].T.reshape(N_HEADS * nb, LANES)
    b_rows = jnp.repeat(p["b_forget"].reshape(N_HEADS), nb).reshape(N_HEADS * nb, 1)
    c = _forget_cumsum(fl_t, b_rows).reshape(N_PAIRS, 2, s)
    c_rowblk = c.reshape(N_PAIRS, 2, nqb, tq).transpose(0, 2, 1, 3)
    attn_o, lse = _attn_fwd(qkv, c, tk=tq)
    lse = lse.reshape(N_PAIRS, nqb // 2, 2, 2, tq).transpose(0, 1, 3, 2, 4).reshape(N_PAIRS, nqb, 2, tq)
    w_out = weight("w_out", attn_o)
    wg_t, wu_t = weight("w_gate_up", attn_o)
    x1, h2, r2, gate, up, act = _out_gate_up(attn_o, pool_o, w_out, x, g2, wg_t, wu_t, tm=t["ff_rows"])
    wd = weight("w_down", act)
    dx2, loss_row, d_gf = _down_final(act, wd, x1, gf, tgt, tm=tm, sub=min(128, tm))

    dgate, dup = _swiglu_bwd(dx2, wd, gate, up, tm=t["ff_rows"], tn=D_FF)
    d_wd = _mm_tn_stacked([act], [D_FF], dx2, ts=t["tall"], name="grad_w_down")
    d_wg_t = _mm_tn_stacked([dgate], [D_FF], h2, ts=t["tall"], name="grad_w_gate")
    d_wu_t = _mm_tn_stacked([dup], [D_FF], h2, ts=t["tall"], name="grad_w_up")
    dx1, d_attn, du, d_g2, d_wpool, d_pscale = _mlp_in_pool_bwd(dgate, dup, wg_t, wu_t, w_out, x1, r2, g2, dx2, pooled,
                                                               w_pool, pool_scale, tm=t["ff_rows"])
    d_wo = _mm_tn_stacked([attn_o, pool_o], [ATTN_W, POOL_W], dx1, ts=t["tall"], name="grad_w_out")
    token = emit(("w_down", "w_gate", "w_up", "w_out"), (d_wd, d_wg_t, d_wu_t, d_wo))
    dq, dk, dv, dck, dcq = _attn_bwd(qkv, attn_o, d_attn, c_rowblk - lse, c, token, tq=tq)
    dc_t = (dcq - dck)[:, :, 0::4, :].transpose(0, 2, 1, 3).reshape(N_HEADS * nb, LANES)
    dfl_t, db = _forget_bwd(dc_t, fl_t, b_rows)
    dfl = jnp.pad(dfl_t.reshape(N_HEADS, s).T, ((0, 0), (0, LANES - N_HEADS))).astype(BF16)
    d_w_in_t = _mm_tn_stacked([dq, dk, dv, dfl, du], [ATTN_W, ATTN_W, ATTN_W, N_HEADS, POOL_W], h, ts=t["tall"],
                              name="grad_w_in",
                              windows=(N_DEV, IN_STEP, IN_WINDOW))
    token = emit(("w_in",), (d_w_in_t,))
    dx, d_g1 = _in_bwd(dq, dk, dv, du, dfl, w_in_t, x, r1, g1, dx1, token, tm=tm)

    small = dict(norm1_g=d_g1, b_forget=db[:, 0].reshape(1, N_HEADS), w_pool=d_wpool, pool_scale=d_pscale,
                 norm2_g=d_g2, final_g=d_gf)
    return loss_row, dx, small


def _my_index():
    return 4 * lax.axis_index("x") + 2 * lax.axis_index("y") + lax.axis_index("c")


def _peer(k):
    pos = [lax.axis_index(a) for a in ("x", "y", "c")]
    flipped = tuple(1 - p if (k >> b) & 1 else p for p, b in zip(pos, (2, 1, 0)))
    return flipped, 4 * flipped[0] + 2 * flipped[1] + flipped[2]


_HBM = pl.BlockSpec(memory_space=pltpu.HBM)
_SEM = pl.BlockSpec(memory_space=pltpu.SEMAPHORE)
_DATAFLOW = pltpu.SideEffectType.DATAFLOW_SIDE_EFFECTING


ALL_PEERS = tuple(range(1, N_DEV))
SAME_CORE = (1, 2, 4, 6)


def _peer_copies(ins, lands, send_sems, recv_sems, scatter, peers, arrivals):
    me = _my_index()
    copies = []
    for w in range(len(ins)):
        for k in peers[w]:
            dev, idx = _peer(k)
            copies.append(pltpu.make_async_remote_copy(
                src_ref=ins[w].at[idx] if scatter[w] else ins[w], dst_ref=lands[w].at[idx if arrivals else me],
                send_sem=send_sems[w].at[k - 1], recv_sem=recv_sems[w].at[k - 1], device_id=dev, device_id_type=MESH))
    return copies


def _own_copies(ins, lands, send_sems, scatter):
    me = _my_index()
    return [pltpu.make_async_copy(ins[w].at[me] if scatter[w] else ins[w], lands[w].at[me], send_sems[w].at[N_DEV - 1])
            for w in range(len(ins))]


def _forward_copies(land, send_sems, recv_sems, arrivals):
    sibling, _ = _peer(1)
    copies = []
    for j, k in enumerate(SAME_CORE[1:]):
        src, dst = _peer(k)[1], _peer(k ^ 1 if arrivals else k)[1]
        copies.append(pltpu.make_async_remote_copy(
            src_ref=land.at[src], dst_ref=land.at[dst], send_sem=send_sems.at[j], recv_sem=recv_sems.at[j],
            device_id=sibling, device_id_type=MESH))
    return copies


def _forward_start(land, name):
    def body(land_ref, send_sems, recv_sems, land_thru, token):
        for cp in _forward_copies(land_ref, send_sems, recv_sems, False):
            cp.start()
        token[...] = jnp.zeros_like(token)

    sem = pltpu.SemaphoreType.DMA((len(SAME_CORE) - 1,))
    send, recv, thru, _ = pl.pallas_call(
        body,
        in_specs=[_HBM],
        out_specs=[_SEM, _SEM, _HBM, pl.BlockSpec(memory_space=pltpu.VMEM)],
        out_shape=[sem, sem, pltpu.HBM(land.shape, land.dtype), SDS((8, LANES), F32)],
        input_output_aliases={0: 2},
        compiler_params=pltpu.CompilerParams(has_side_effects=_DATAFLOW),
        name=name,
    )(land)
    return send, recv, thru


def _forward_wait(handle, after, name):
    def body(land_ref, send_sems, recv_sems, after_ref, land_out):
        for cp in _forward_copies(land_ref, send_sems, recv_sems, False):
            cp.wait_send()
        for cp in _forward_copies(land_ref, send_sems, recv_sems, True):
            cp.wait_recv()

    send, recv, land = handle
    return pl.pallas_call(
        body,
        in_specs=[_HBM, _SEM, _SEM, pl.BlockSpec(memory_space=pl.ANY)],
        out_specs=_HBM,
        out_shape=pltpu.HBM(land.shape, land.dtype),
        input_output_aliases={0: 0},
        compiler_params=pltpu.CompilerParams(has_side_effects=_DATAFLOW),
        name=name,
    )(land, send, recv, after)


def _exchange_start(arrays, scatter, name, peers=None, after=None):
    n = len(arrays)
    peers = peers or [ALL_PEERS] * n
    order = [] if after is None else [after]
    land_shapes = [(N_DEV,) + tuple(a.shape[1:] if sc else a.shape) for a, sc in zip(arrays, scatter)]

    def body(*refs):
        ins, lands = refs[:n], refs[n:2 * n]
        outs = refs[2 * n + len(order):]
        send_sems, recv_sems, token = outs[:n], outs[n:2 * n], outs[4 * n]
        for cp in _peer_copies(ins, lands, send_sems, recv_sems, scatter, peers, False):
            cp.start()
        for cp in _own_copies(ins, lands, send_sems, scatter):
            cp.start()
        token[...] = jnp.zeros_like(token)

    sends, recvs = pltpu.SemaphoreType.DMA((N_DEV,)), pltpu.SemaphoreType.DMA((N_DEV - 1,))
    outs = pl.pallas_call(
        body,
        in_specs=[_HBM] * (2 * n) + [_UNREAD] * len(order),
        out_specs=[_SEM] * (2 * n) + [_HBM] * (2 * n) + [pl.BlockSpec(memory_space=pltpu.VMEM)],
        out_shape=[sends] * n + [recvs] * n + [pltpu.HBM(a.shape, a.dtype) for a in arrays]
        + [pltpu.HBM(sh, a.dtype) for sh, a in zip(land_shapes, arrays)] + [SDS((8, LANES), F32)],
        input_output_aliases={i: 2 * n + i for i in range(2 * n)},
        compiler_params=pltpu.CompilerParams(has_side_effects=_DATAFLOW),
        name=name,
    )(*[pltpu.with_memory_space_constraint(a, pltpu.HBM) for a in arrays],
      *[pltpu.with_memory_space_constraint(lax.empty(sh, a.dtype), pltpu.HBM) for sh, a in zip(land_shapes, arrays)],
      *order)
    handles = [dict(send=outs[w], recv=outs[n + w], src=outs[2 * n + w], land=outs[3 * n + w], scatter=scatter[w],
                    peers=peers[w]) for w in range(n)]
    return handles, outs[4 * n]


def _exchange_wait(handles, after, name):
    n = len(handles)
    scatter, peers = [h["scatter"] for h in handles], [h["peers"] for h in handles]

    def body(*refs):
        ins, lands = refs[:n], refs[n:2 * n]
        send_sems, recv_sems = refs[2 * n:3 * n], refs[3 * n:4 * n]
        for cp in _peer_copies(ins, lands, send_sems, recv_sems, scatter, peers, False):
            cp.wait_send()
        for cp in _peer_copies(ins, lands, send_sems, recv_sems, scatter, peers, True):
            cp.wait_recv()
        for cp in _own_copies(ins, lands, send_sems, scatter):
            cp.wait()

    srcs, lands = [h["src"] for h in handles], [h["land"] for h in handles]
    outs = pl.pallas_call(
        body,
        in_specs=[_HBM] * (2 * n) + [_SEM] * (2 * n) + [pl.BlockSpec(memory_space=pl.ANY)],
        out_specs=[_HBM] * (2 * n),
        out_shape=[pltpu.HBM(a.shape, a.dtype) for a in srcs + lands],
        input_output_aliases={i: i for i in range(2 * n)},
        compiler_params=pltpu.CompilerParams(has_side_effects=_DATAFLOW),
        name=name,
    )(*srcs, *lands, *[h["send"] for h in handles], *[h["recv"] for h in handles], after)
    return outs[n:]


def _adamw(parts, w, m, v, name):
    rows, cols = w.shape
    tr = rows // 4 if rows % 32 == 0 else rows

    def body(p_ref, w_ref, m_ref, v_ref, g_ref, d_ref, mo_ref, vo_ref):
        g = p_ref[0].astype(F32)
        for d in range(1, N_DEV):
            g = g + p_ref[d].astype(F32)
        g_ref[...] = g
        d_ref[...], mo_ref[...], vo_ref[...] = _adam_update(g, w_ref[...], m_ref[...], v_ref[...])

    blk = pl.BlockSpec((tr, cols), lambda i: (i, 0))
    return pl.pallas_call(
        body,
        grid=(rows // tr,),
        in_specs=[pl.BlockSpec((N_DEV, tr, cols), lambda i: (0, i, 0)), blk, blk, blk],
        out_specs=[blk] * 4,
        out_shape=[SDS((rows, cols), F32)] * 4,
        compiler_params=_cparams("arbitrary"),
        name=name,
    )(parts, w, m, v)


def _adamw_dense(parts, w, m, v, name, *, rows, shift):
    _, window, cols = parts.shape
    per_row = cols // LANES

    def body(p_ref, w_ref, m_ref, v_ref, g_ref, d_ref, mo_ref, vo_ref, sum_ref):
        g = p_ref[0].astype(F32)
        for d in range(1, N_DEV):
            g = g + p_ref[d].astype(F32)
        sum_ref[...] = g
        me = _my_index()
        for j in range(N_DEV):
            @pl.when(me == j)
            def _(j=j):
                for c in range(per_row):
                    at = (pl.ds(c, rows, stride=per_row), slice(None))
                    gc = sum_ref[j * shift:j * shift + rows, c * LANES:(c + 1) * LANES]
                    g_ref[at] = gc
                    d_ref[at], mo_ref[at], vo_ref[at] = _adam_update(gc, w_ref[at], m_ref[at], v_ref[at])

    return pl.pallas_call(
        body,
        out_shape=[SDS(w.shape, F32)] * 4,
        scratch_shapes=[pltpu.VMEM((window, cols), F32)],
        compiler_params=_cparams(),
        name=name,
    )(parts, w, m, v)


_ROW_OF = dict(norm1_g=(0, D_MODEL), norm2_g=(1, D_MODEL), final_g=(2, D_MODEL), pool_scale=(3, POOL_W),
               b_forget=(4, N_HEADS), loss=(5, 1))


def _pack_rows(vals):
    rows = [jnp.pad(vals[n].reshape(1, width).astype(F32), ((0, 0), (0, D_MODEL - width)))
            for n, (_, width) in sorted(_ROW_OF.items(), key=lambda kv: kv[1][0])]
    return jnp.concatenate(rows + [jnp.zeros((8 - len(rows), D_MODEL), F32)], axis=0)


def _adam_update(g, w, m, v):
    m_new = ADAM_B1 * m + (1.0 - ADAM_B1) * g
    v_new = ADAM_B2 * v + (1.0 - ADAM_B2) * (g * g)
    m_hat = m_new / (1.0 - ADAM_B1 ** ADAM_STEP)
    v_hat = v_new / (1.0 - ADAM_B2 ** ADAM_STEP)
    return -ADAM_LR * (m_hat / (jnp.sqrt(v_hat) + ADAM_EPS) + ADAM_WD * w), m_new, v_new


def _adamw_replicated(parts_rows, parts_pool, w, m, v):
    names = ("norm1_g", "norm2_g", "final_g", "pool_scale", "b_forget", "w_pool")
    shapes = {n: ((len(POOL_WINDOWS), POOL_G, POOL_G) if n == "w_pool" else (1, _ROW_OF[n][1])) for n in names}

    def body(rows_ref, pool_ref, *refs):
        ins, outs = refs[:3 * len(names)], refs[3 * len(names):]

        def total(n):
            if n == "w_pool":
                pieces = [pool_ref[d] for d in range(N_DEV)]
            else:
                row, width = _ROW_OF[n]
                pieces = [rows_ref[d, row:row + 1, 0:width] for d in range(N_DEV)]
            g = pieces[0]
            for p in pieces[1:]:
                g = g + p
            return g

        outs[0][...] = total("loss")
        for k, n in enumerate(names):
            g = total(n)
            delta, m_new, v_new = _adam_update(g, ins[3 * k][...], ins[3 * k + 1][...], ins[3 * k + 2][...])
            for o_ref, val in zip(outs[1 + 4 * k:5 + 4 * k], (g, delta, m_new, v_new)):
                o_ref[...] = val

    args = [d[n].reshape(shapes[n]) for n in names for d in (w, m, v)]
    res = pl.pallas_call(
        body,
        out_shape=[SDS((1, 1), F32)] + [SDS(shapes[n], F32) for n in names for _ in range(4)],
        compiler_params=_cparams(),
        name="adamw_replicated",
    )(parts_rows, parts_pool, *args)
    return res[0], {n: [r.reshape(w[n].shape) for r in res[1 + 4 * k:5 + 4 * k]] for k, n in enumerate(names)}


def kernel(x, norm1_g, w_in, b_forget, w_pool, pool_scale, w_out, norm2_g, w_gate, w_up, w_down, final_g, loss_target, m_norm1_g, m_w_in, m_b_forget, m_w_pool, m_pool_scale, m_w_out, m_norm2_g, m_w_gate, m_w_up, m_w_down, m_final_g, v_norm1_g, v_w_in, v_b_forget, v_w_pool, v_pool_scale, v_w_out, v_norm2_g, v_w_gate, v_w_up, v_w_down, v_final_g):
    big = ("w_in", "w_out", "w_gate", "w_up", "w_down")
    order = ("norm1_g", "w_in", "b_forget", "w_pool", "pool_scale", "w_out", "norm2_g", "w_gate", "w_up", "w_down",
             "final_g")
    w = dict(norm1_g=norm1_g, w_in=w_in, b_forget=b_forget, w_pool=w_pool, pool_scale=pool_scale, w_out=w_out,
             norm2_g=norm2_g, w_gate=w_gate, w_up=w_up, w_down=w_down, final_g=final_g)
    m = dict(norm1_g=m_norm1_g, w_in=m_w_in, b_forget=m_b_forget, w_pool=m_w_pool, pool_scale=m_pool_scale,
             w_out=m_w_out, norm2_g=m_norm2_g, w_gate=m_w_gate, w_up=m_w_up, w_down=m_w_down, final_g=m_final_g)
    v = dict(norm1_g=v_norm1_g, w_in=v_w_in, b_forget=v_b_forget, w_pool=v_w_pool, pool_scale=v_pool_scale,
             w_out=v_w_out, norm2_g=v_norm2_g, w_gate=v_w_gate, w_up=v_w_up, w_down=v_w_down, final_g=v_final_g)

    flipped = ("w_in", "w_gate", "w_up")
    shard = lambda d, n: d[n][0].T if n in flipped else d[n][0]
    cast = lambda n: shard(w, n).astype(BF16)
    (first,), started = _exchange_start([cast("w_in")], [False], "gather_start_w_in", peers=[SAME_CORE])
    gather = dict(w_in=first)

    def gathered(names, after):
        return _exchange_wait([gather[n] for n in names], after, "gather_wait_" + names[0])

    def weight(name, after):
        if name == "w_in":
            handles, token = _exchange_start([cast(n) for n in big[1:]], [False] * len(big[1:]), "gather_start", after=after)
            gather.update(zip(big[1:], handles))
            forward = _forward_start(gathered(["w_in"], token)[0], "gather_forward_start")
            full = _forward_wait(forward, after, "gather_forward_wait").reshape(IN_W, D_MODEL)
            f0 = QKV_W + N_HEADS
            return jnp.concatenate([full[:QKV_W], full[f0:], full[QKV_W:f0],
                                    jnp.zeros((IN_PAD - IN_W, D_MODEL), BF16)], axis=0)
        if name == "w_out":
            return gathered(["w_out"], after)[0].reshape(D_MODEL, D_MODEL)
        if name == "w_gate_up":
            return [g.reshape(D_FF, D_MODEL) for g in gathered(["w_gate", "w_up"], after)]
        return gathered(["w_down"], after)[0].reshape(D_FF, D_MODEL)

    rows = lambda g: g if g.ndim == 3 else g.reshape(N_DEV, g.shape[0] // N_DEV, g.shape[1])
    sent = {}

    def emit(names, grads):
        handles, token = _exchange_start([rows(g) for g in grads], [True] * len(names), "grads_start_" + names[0])
        sent.update(zip(names, handles))
        return token

    loss_row, dx, small_grads = _local_step(x[0], loss_target[0], w, weight, emit, started)

    packed = _pack_rows(dict(small_grads, loss=0.5 / D_MODEL * jnp.sum(loss_row)))
    small_handles, after = _exchange_start([packed, small_grads["w_pool"]], [False, False], "grads_start_replicated")

    outs = {}
    for name in ("w_down", "w_gate", "w_up", "w_out", "w_in"):
        (parts,) = _exchange_wait([sent[name]], after, "grads_wait_" + name)
        if name == "w_in":
            dense = lambda d: d[name].transpose(2, 0, 1).reshape(-1, LANES)
            outs[name] = _adamw_dense(parts, dense(w), dense(m), dense(v), "adamw_" + name, rows=IN_SHARD, shift=IN_SHIFT)
            after = outs[name][0]
            outs[name] = [a.reshape(-1, D_MODEL // LANES, LANES).transpose(1, 2, 0).reshape(1, D_MODEL, -1)
                          for a in outs[name]]
            continue
        outs[name] = _adamw(parts, shard(w, name), shard(m, name), shard(v, name), "adamw_" + name)
        after = outs[name][0]
        outs[name] = [(a.T if name in flipped else a)[None] for a in outs[name]]
    parts_rows, parts_pool = _exchange_wait(small_handles, after, "grads_wait_replicated")
    loss, small = _adamw_replicated(parts_rows, parts_pool, w, m, v)
    outs.update(small)

    return (loss.reshape(()), dx[None]) + tuple(outs[n][k] for k in range(4) for n in order)
```

```python
import jax
import jax.numpy as jnp
from jax import lax
from jax.experimental import pallas as pl
from jax.experimental.pallas import tpu as pltpu

F32 = jnp.float32
BF16 = jnp.bfloat16
SDS = jax.ShapeDtypeStruct

D_MODEL = 1024
ATTN_W = 512
N_HEADS = 8
HEAD_DIM = 64
Q_SCALE = HEAD_DIM ** -0.5
N_PAIRS = N_HEADS // 2
POOL_W = 512
POOL_WINDOWS = (2, 4, 8, 16)
POOL_G = 128
HALO = 16
IN_W = 3 * ATTN_W + N_HEADS + POOL_W
QKV_W = 3 * ATTN_W
U_OFF = QKV_W
F_OFF = QKV_W + POOL_W
IN_PAD = F_OFF + 128
D_FF = 2816
EPS = 1e-6
NEG = -1e30
N_DEV = 8
LANES = 128
BF16_ROWS = 16

IN_SHARD = IN_W // N_DEV
IN_STEP = IN_SHARD // BF16_ROWS * BF16_ROWS
IN_SHIFT = IN_SHARD - IN_STEP
IN_WINDOW = -(-((N_DEV - 1) * IN_SHIFT + IN_SHARD) // BF16_ROWS) * BF16_ROWS

ADAM_LR = 0.001
ADAM_B1 = 0.9
ADAM_B2 = 0.999
ADAM_EPS = 1e-08
ADAM_WD = 0.01
ADAM_STEP = 10

VMEM_LIMIT_BYTES = 56 * 1024 * 1024
MESH = pl.DeviceIdType.MESH
NT = (((1,), (1,)), ((), ()))
TN = (((0,), (0,)), ((), ()))


_UNREAD = pl.BlockSpec(memory_space=pl.ANY)


def _cparams(*sem):
    return pltpu.CompilerParams(dimension_semantics=sem or None, vmem_limit_bytes=VMEM_LIMIT_BYTES)


def _split3(a):
    hi = a.astype(BF16)
    r1 = a - hi.astype(F32)
    mid = r1.astype(BF16)
    lo = (r1 - mid.astype(F32)).astype(BF16)
    return hi, mid, lo


def _dot_sel(a, sel, dims=None):
    sb = sel.astype(BF16)
    if dims is None:
        return sum(jnp.dot(p, sb, preferred_element_type=F32) for p in _split3(a))
    return sum(lax.dot_general(p, sb, dims, preferred_element_type=F32) for p in _split3(a))


def _sel_dot(sel, a, dims=None):
    sb = sel.astype(BF16)
    if dims is None:
        return sum(jnp.dot(sb, p, preferred_element_type=F32) for p in _split3(a))
    return sum(lax.dot_general(sb, p, dims, preferred_element_type=F32) for p in _split3(a))


def _iota2(shape, dim):
    return lax.broadcasted_iota(jnp.int32, shape, dim)


UNROLLS = (8, 4, 2)


def _shift_div(x, n):
    return lax.shift_right_logical(x, n.bit_length() - 1)


def _norm1(x, g1, after, *, tm):
    s = x.shape[0]

    def body(x_ref, g_ref, _, h_ref, r_ref):
        xv = x_ref[...]
        r = lax.rsqrt(jnp.mean(xv * xv, axis=-1, keepdims=True) + EPS)
        h_ref[...] = (xv * r * g_ref[...]).astype(BF16)
        r_ref[...] = r

    row = lambda w: pl.BlockSpec((tm, w), lambda i: (i, 0))
    return pl.pallas_call(
        body,
        grid=(s // tm,),
        in_specs=[row(D_MODEL), pl.BlockSpec((1, D_MODEL), lambda i: (0, 0)), _UNREAD],
        out_specs=[row(D_MODEL), row(1)],
        out_shape=[SDS((s, D_MODEL), BF16), SDS((s, 1), F32)],
        compiler_params=_cparams("arbitrary"),
        name="norm1",
    )(x, g1, after)


def _in_proj_pool(h, w_in_t, w_pool, pool_scale, *, tm):
    s = h.shape[0]

    def body(h_ref, w_ref, wp_ref, sc_ref, qkv_ref, fl_ref, pooled_ref, po_ref, tail_ref):
        i = pl.program_id(0)

        @pl.when(i == 0)
        def _():
            tail_ref[...] = jnp.zeros_like(tail_ref)

        hv = h_ref[...]
        uv = lax.dot_general(hv, w_ref[U_OFF:F_OFF, :], NT, preferred_element_type=F32)
        qkv_ref[...] = lax.dot_general(hv, w_ref[0:QKV_W, :], NT, preferred_element_type=F32).astype(BF16)
        fl_ref[...] = lax.dot_general(hv, w_ref[F_OFF:IN_PAD, :], NT, preferred_element_type=F32)
        ext = jnp.concatenate([tail_ref[...], uv], axis=0)
        tail_ref[...] = uv[tm - HALO:, :]
        for g, w in enumerate(POOL_WINDOWS):
            cols = slice(g * POOL_G, (g + 1) * POOL_G)
            acc = ext[:, cols]
            k = 1
            while k < w:
                acc = acc + pltpu.roll(acc, k, axis=0)
                k *= 2
            pooled = (acc[HALO:, :] / _pool_counts(i * tm, tm, w) - uv[:, cols]).astype(BF16)
            pooled_ref[:, cols] = pooled
            mixed = jnp.dot(pooled, wp_ref[g].astype(BF16), preferred_element_type=F32)
            po_ref[:, cols] = (mixed * sc_ref[:, cols]).astype(BF16)

    row = lambda w: pl.BlockSpec((tm, w), lambda i: (i, 0))
    return pl.pallas_call(
        body,
        grid=(s // tm,),
        in_specs=[row(D_MODEL), pl.BlockSpec((IN_PAD, D_MODEL), lambda i: (0, 0)),
                  pl.BlockSpec((len(POOL_WINDOWS), POOL_G, POOL_G), lambda i: (0, 0, 0)),
                  pl.BlockSpec((1, POOL_W), lambda i: (0, 0))],
        out_specs=[row(QKV_W), row(LANES), row(POOL_W), row(POOL_W)],
        out_shape=[SDS((s, QKV_W), BF16), SDS((s, LANES), F32), SDS((s, POOL_W), BF16), SDS((s, POOL_W), BF16)],
        scratch_shapes=[pltpu.VMEM((HALO, POOL_W), F32)],
        compiler_params=_cparams("arbitrary"),
        name="in_proj_pool",
    )(h, w_in_t, w_pool, pool_scale)


def _head_block_masks(rows, nb):
    shift = nb.bit_length() - 1
    rr, cc = _iota2((rows, rows), 0), _iota2((rows, rows), 1)
    same = lax.shift_right_logical(rr, shift) == lax.shift_right_logical(cc, shift)
    return rr, cc, same


def _forget_cumsum(fl_t, b_rows):
    rows = fl_t.shape[0]
    nb = rows // N_HEADS

    def body(fl_ref, b_ref, c_ref):
        z = fl_ref[...] + b_ref[...]
        lf = jnp.minimum(z, 0.0) - jnp.log1p(jnp.exp(-jnp.abs(z)))
        upper = _iota2((LANES, LANES), 0) <= _iota2((LANES, LANES), 1)
        within = _dot_sel(lf, upper)
        tot = _dot_sel(lf, jnp.ones((LANES, LANES), F32))
        rr, cc, same = _head_block_masks(rows, nb)
        c_ref[...] = within + _sel_dot(same & (cc < rr), tot)

    return pl.pallas_call(body, out_shape=SDS(fl_t.shape, F32), compiler_params=_cparams(), name="forget_cumsum")(
        fl_t, b_rows)


BIAS_LANES = 3


def _augment(t, h, bias, col_first):
    n = t.shape[0]
    lane = _iota2((n, LANES), 1)
    own = (lane < HEAD_DIM) if h == 0 else (lane >= HEAD_DIM)
    b0 = HEAD_DIM if h == 0 else 0
    c0, o0 = (b0, b0 + BIAS_LANES) if col_first else (b0 + BIAS_LANES, b0)
    x = 0.0
    if bias is not None:
        row = _iota2((BF16_ROWS, n), 0)
        pieces = jnp.zeros((BF16_ROWS, n), F32)
        for off, piece in enumerate(_split3(bias)):
            pieces = jnp.where(row == off, piece.astype(F32), pieces)
        r, ln = _iota2((BF16_ROWS, LANES), 0), _iota2((BF16_ROWS, LANES), 1)
        place = jnp.where(r < BIAS_LANES, jnp.where(ln == c0 + r, 1.0, 0.0), 0.0).astype(BF16)
        x = lax.dot_general(pieces.astype(BF16), place, TN, preferred_element_type=F32)
    x = jnp.where(own, t, x)
    x = jnp.where((lane >= o0) & (lane < o0 + BIAS_LANES), 1.0, x)
    return x.astype(BF16)


def _attn_fwd(qkv, c_rows, *, tk):
    s = qkv.shape[0]
    tq = 2 * tk
    nb = s // tk

    def body(q_ref, k_ref, v_ref, cq_ref, ck_ref, o_ref, lse_ref, kp_ref, vt_ref, st_ref):
        i = pl.program_id(1)

        @pl.when(i == 0)
        def _():
            def prep(jb, _):
                st = pl.multiple_of(jb * tk, tk)
                k2 = k_ref[pl.ds(st, tk), :].astype(F32)
                ck = ck_ref[:, pl.ds(st, tk)]
                for h in range(2):
                    kp_ref[h * nb + jb] = _augment(k2, h, -ck[h:h + 1, :], True)
                vt_ref[jb] = v_ref[pl.ds(st, tk), :].astype(F32).T.astype(BF16)
                return 0

            lax.fori_loop(0, nb, prep, 0)

        qs = q_ref[...].astype(F32) * Q_SCALE
        cq = cq_ref[...]
        qp = [_augment(qs, h, cq[h:h + 1, :], False) for h in range(2)]

        def logits(j):
            return tuple(lax.dot_general(kp_ref[h * nb + j], qp[h], NT, preferred_element_type=F32) for h in range(2))

        def softmax_pv(j, slot, stats, masked):
            out = []
            for h in range(2):
                m, l, acc = stats[h]
                st = st_ref[2 * slot + h]
                if masked:
                    st = jnp.where(j * tk + _iota2((tk, tq), 0) <= i * tq + _iota2((tk, tq), 1), st, NEG)
                m_new = jnp.maximum(m, jnp.max(st, axis=0, keepdims=True))
                alpha = jnp.exp(m - m_new)
                p = jnp.exp(st - m_new)
                l = alpha * l + jnp.sum(p, axis=0, keepdims=True)
                vt = vt_ref[j, h * HEAD_DIM:(h + 1) * HEAD_DIM, :]
                acc = alpha * acc + jnp.dot(vt, p.astype(BF16), preferred_element_type=F32)
                out.append((m_new, l, acc))
            return tuple(out)

        def put(slot, j):
            for h, st in enumerate(logits(j)):
                st_ref[2 * slot + h] = st

        def run(j0, steps, stats):
            for d in range(steps):
                put(1 - d % 2, j0 + d + 1)
                stats = softmax_pv(j0 + d, d % 2, stats, False)
            return stats

        init = tuple((jnp.full((1, tq), NEG, F32), jnp.zeros((1, tq), F32), jnp.zeros((HEAD_DIM, tq), F32))
                     for _ in range(2))
        put(0, 0)
        first, left, stats = 0, 2 * i, init
        for size in UNROLLS:
            trips = _shift_div(left, size)
            stats = lax.fori_loop(0, trips, lambda t, st, j0=first, n=size: run(j0 + n * t, n, st), stats)
            first, left = first + size * trips, left - size * trips
        put(1, 2 * i + 1)
        stats = softmax_pv(2 * i, 0, stats, True)
        (ma, la, acca), (mb, lb, accb) = softmax_pv(2 * i + 1, 1, stats, True)
        o_ref[...] = jnp.concatenate([acca / la, accb / lb], axis=0).T.astype(BF16)
        lse_ref[...] = jnp.where(_iota2((2, tq), 0) == 0, ma + jnp.log(la), mb + jnp.log(lb))

    return pl.pallas_call(
        body,
        grid=(N_PAIRS, s // tq),
        in_specs=[
            pl.BlockSpec((tq, LANES), lambda p, i: (i, p)),
            pl.BlockSpec((s, LANES), lambda p, i: (0, N_PAIRS + p)),
            pl.BlockSpec((s, LANES), lambda p, i: (0, 2 * N_PAIRS + p)),
            pl.BlockSpec((None, 2, tq), lambda p, i: (p, 0, i)),
            pl.BlockSpec((None, 2, s), lambda p, i: (p, 0, 0)),
        ],
        out_specs=[
            pl.BlockSpec((tq, LANES), lambda p, i: (i, p)),
            pl.BlockSpec((None, None, 2, tq), lambda p, i: (p, i, 0, 0)),
        ],
        out_shape=[SDS((s, ATTN_W), BF16), SDS((N_PAIRS, s // tq, 2, tq), F32)],
        scratch_shapes=[pltpu.VMEM((2 * nb, tk, LANES), BF16), pltpu.VMEM((nb, LANES, tk), BF16),
                        pltpu.VMEM((4, tk, tq), F32)],
        compiler_params=_cparams("arbitrary", "arbitrary"),
        name="attn_fwd",
    )(qkv, qkv, qkv, c_rows, c_rows)


def _pool_counts(row0, tm, w):
    t = row0 + _iota2((tm, 1), 0)
    return jnp.minimum(t + 1, w).astype(F32)


def _out_gate_up(attn_o, pool_o, w_out, x, g2, wg_t, wu_t, *, tm):
    s = x.shape[0]

    def body(a_ref, p_ref, wo_ref, x_ref, g_ref, wg_ref, wu_ref, x1_ref, h2_ref, r_ref, gate_ref, up_ref, act_ref):
        x1 = (x_ref[...] + jnp.dot(a_ref[...], wo_ref[0:ATTN_W, :], preferred_element_type=F32)
              + jnp.dot(p_ref[...], wo_ref[ATTN_W:, :], preferred_element_type=F32))
        r = lax.rsqrt(jnp.mean(x1 * x1, axis=-1, keepdims=True) + EPS)
        x1_ref[...] = x1
        r_ref[...] = r
        h2 = (x1 * r * g_ref[...]).astype(BF16)
        h2_ref[...] = h2
        gate = lax.dot_general(h2, wg_ref[...], NT, preferred_element_type=F32)
        up = lax.dot_general(h2, wu_ref[...], NT, preferred_element_type=F32)
        gate_ref[...] = gate.astype(BF16)
        up_ref[...] = up.astype(BF16)
        act_ref[...] = (gate * jax.nn.sigmoid(gate) * up).astype(BF16)

    row = lambda w: pl.BlockSpec((tm, w), lambda i: (i, 0))
    full = lambda a, b: pl.BlockSpec((a, b), lambda i: (0, 0))
    return pl.pallas_call(
        body,
        grid=(s // tm,),
        in_specs=[row(ATTN_W), row(POOL_W), full(D_MODEL, D_MODEL), row(D_MODEL), full(1, D_MODEL),
                  full(D_FF, D_MODEL), full(D_FF, D_MODEL)],
        out_specs=[row(D_MODEL), row(D_MODEL), row(1), row(D_FF), row(D_FF), row(D_FF)],
        out_shape=[SDS((s, D_MODEL), F32), SDS((s, D_MODEL), BF16), SDS((s, 1), F32), SDS((s, D_FF), BF16),
                   SDS((s, D_FF), BF16), SDS((s, D_FF), BF16)],
        compiler_params=_cparams("arbitrary"),
        name="out_gate_up",
    )(attn_o, pool_o, w_out, x, g2, wg_t, wu_t)


def _staggered(n, start, finish):
    pending = start(0)
    for k in range(n):
        following = start(k + 1) if k + 1 < n else None
        finish(k, pending)
        pending = following


def _down_final(act, wd, x1, gf, tgt, *, tm, sub):
    s = x1.shape[0]

    def body(a_ref, w_ref, x1_ref, g_ref, t_ref, dx2_ref, loss_ref, dgf_ref):
        @pl.when(pl.program_id(0) == 0)
        def _():
            loss_ref[...] = jnp.zeros_like(loss_ref)
            dgf_ref[...] = jnp.zeros_like(dgf_ref)

        g = g_ref[...]

        def matmul(k):
            return jnp.dot(a_ref[k * sub:(k + 1) * sub, :], w_ref[...], preferred_element_type=F32)

        def rest(k, mm):
            rows = slice(k * sub, (k + 1) * sub)
            x2 = x1_ref[rows, :] + mm
            r = lax.rsqrt(jnp.mean(x2 * x2, axis=-1, keepdims=True) + EPS)
            xn = x2 * r
            diff = xn * g - t_ref[rows, :]
            loss_ref[...] += jnp.sum(diff * diff, axis=0, keepdims=True)
            dy = diff * (1.0 / D_MODEL)
            dgf_ref[...] += jnp.sum(dy * xn, axis=0, keepdims=True)
            dxn = dy * g
            dx2_ref[rows, :] = r * (dxn - xn * jnp.mean(dxn * xn, axis=-1, keepdims=True))

        _staggered(tm // sub, matmul, rest)

    row = lambda w: pl.BlockSpec((tm, w), lambda i: (i, 0))
    full = lambda a, b: pl.BlockSpec((a, b), lambda i: (0, 0))
    return pl.pallas_call(
        body,
        grid=(s // tm,),
        in_specs=[row(D_FF), full(D_FF, D_MODEL), row(D_MODEL), full(1, D_MODEL), row(D_MODEL)],
        out_specs=[row(D_MODEL), full(1, D_MODEL), full(1, D_MODEL)],
        out_shape=[SDS((s, D_MODEL), F32), SDS((1, D_MODEL), F32), SDS((1, D_MODEL), F32)],
        compiler_params=_cparams("arbitrary"),
        name="down_final",
    )(act, wd, x1, gf, tgt)


def _swiglu_bwd(dx2, wd, gate, up, *, tm, tn):
    s = dx2.shape[0]

    def body(d_ref, w_ref, gate_ref, up_ref, dgate_ref, dup_ref):
        dact = lax.dot_general(d_ref[...].astype(BF16), w_ref[...], NT, preferred_element_type=F32)
        gate = gate_ref[...].astype(F32)
        sg = jax.nn.sigmoid(gate)
        dup_ref[...] = (dact * (gate * sg)).astype(BF16)
        dgate_ref[...] = (dact * up_ref[...].astype(F32) * (sg * (1.0 + gate * (1.0 - sg)))).astype(BF16)

    ospec = pl.BlockSpec((tm, tn), lambda c, r: (r, c))
    return pl.pallas_call(
        body,
        grid=(D_FF // tn, s // tm),
        in_specs=[pl.BlockSpec((tm, D_MODEL), lambda c, r: (r, 0)), pl.BlockSpec((tn, D_MODEL), lambda c, r: (c, 0)),
                  ospec, ospec],
        out_specs=[ospec, ospec],
        out_shape=[SDS((s, D_FF), BF16), SDS((s, D_FF), BF16)],
        compiler_params=_cparams("arbitrary", "arbitrary"),
        name="swiglu_bwd",
    )(dx2, wd, gate, up)


def _mm_tn_stacked(as_, rows, b, *, ts, name, windows=None):
    s, nb_ = b.shape
    n = len(as_)
    offsets = [sum(rows[:i]) for i in range(n)]
    total = sum(rows)
    if windows is None:
        acc_rows, out_shape = total, (total, nb_)
    else:
        count, step, size = windows
        acc_rows, out_shape = max(total, (count - 1) * step + size), (count, size, nb_)

    def body(*refs):
        a_refs, b_ref, o_ref, acc_ref = refs[:n], refs[n], refs[n + 1], refs[n + 2]
        k = pl.program_id(0)

        @pl.when(k == 0)
        def _():
            acc_ref[...] = jnp.zeros_like(acc_ref)

        bv = b_ref[...].astype(BF16)
        for a_ref, off, cnt in zip(a_refs, offsets, rows):
            part = lax.dot_general(a_ref[...].astype(BF16), bv, TN, preferred_element_type=F32)
            acc_ref[off:off + cnt, :] += part[0:cnt, :]

        @pl.when(k == s // ts - 1)
        def _():
            if windows is None:
                o_ref[...] = acc_ref[...].astype(BF16)
            else:
                for d in range(count):
                    o_ref[d] = acc_ref[d * step:d * step + size, :].astype(BF16)

    return pl.pallas_call(
        body,
        grid=(s // ts,),
        in_specs=[pl.BlockSpec((ts, a.shape[1]), lambda k: (k, 0)) for a in as_] + [pl.BlockSpec((ts, nb_), lambda k: (k, 0))],
        out_specs=pl.BlockSpec(out_shape, lambda k: (0,) * len(out_shape)),
        out_shape=SDS(out_shape, BF16),
        scratch_shapes=[pltpu.VMEM((acc_rows, nb_), F32)],
        compiler_params=_cparams("arbitrary"),
        name=name,
    )(*as_, b)


def _mm_tn_phases(as_, bs, *, ts, name):
    n = len(as_)
    s, r = as_[0].shape
    c = bs[0].shape[1]
    steps = s // ts
    operands, phases, where = [], [], []
    for i, arr in enumerate(list(as_) + list(bs)):
        known = [j for j, o in enumerate(operands) if o is arr]
        if known:
            phases[known[0]].append(i % n)
        else:
            operands.append(arr)
            phases.append([i % n])
        where.append(known[0] if known else len(operands) - 1)
    m = len(operands)

    def body(*refs):
        outs, acc_ref, stage_ref, sem = refs[m:m + n], refs[m + n], refs[m + n + 1], refs[m + n + 2]
        p, k = pl.program_id(0), pl.program_id(1)
        written = lambda i: pltpu.make_async_copy(stage_ref, outs[i], sem.at[i])

        @pl.when(k == 0)
        def _():
            acc_ref[...] = jnp.zeros_like(acc_ref)

        for i in range(n):
            @pl.when(p == i)
            def _(i=i):
                a, b = refs[where[i]][...].astype(BF16), refs[where[n + i]][...].astype(BF16)
                acc_ref[...] += lax.dot_general(a, b, TN, preferred_element_type=F32)

            @pl.when((p == i) & (k == steps - 1))
            def _(i=i):
                if i > 0:
                    written(i - 1).wait()
                stage_ref[...] = acc_ref[...].astype(BF16)
                written(i).start()
                if i == n - 1:
                    written(i).wait()

    def spec(arr, ph):
        lo, hi = min(ph), max(ph)
        return pl.BlockSpec((ts, arr.shape[1]),
                            lambda p, k: (jnp.where(p < lo, 0, jnp.where(p > hi, steps - 1, k)), 0))

    return pl.pallas_call(
        body,
        grid=(n, steps),
        in_specs=[spec(arr, ph) for arr, ph in zip(operands, phases)],
        out_specs=[pl.BlockSpec(memory_space=pl.ANY)] * n,
        out_shape=[SDS((r, c), BF16)] * n,
        scratch_shapes=[pltpu.VMEM((r, c), F32), pltpu.VMEM((r, c), BF16), pltpu.SemaphoreType.DMA((n,))],
        compiler_params=_cparams("arbitrary", "arbitrary"),
        name=name,
    )(*operands)


def _norm_bwd(dh, x, r, g, dres):
    xn = x * r
    dxn = dh * g
    dx = dres + r * (dxn - xn * jnp.mean(dxn * xn, axis=-1, keepdims=True))
    return dx, jnp.sum(dh * xn, axis=0, keepdims=True)


def _mlp_in_pool_bwd(dgate, dup, wg_t, wu_t, w_out, x1, r2, g2, dx2, pooled, w_pool, pool_scale, *, tm):
    s = x1.shape[0]
    nt = s // tm
    ng = len(POOL_WINDOWS)

    def body(dg_ref, dup_ref, wg_ref, wu_ref, wo_ref, x_ref, r_ref, g_ref, d_ref, p_ref, w_ref, sc_ref,
             dx1_ref, dattn_ref, du_ref, dg2_ref, dw_ref, dsc_ref, head_ref):
        i = pl.program_id(0)

        @pl.when(i == 0)
        def _():
            dg2_ref[...] = jnp.zeros_like(dg2_ref)
            head_ref[...] = jnp.zeros_like(head_ref)
            dw_ref[...] = jnp.zeros_like(dw_ref)
            dsc_ref[...] = jnp.zeros_like(dsc_ref)

        dh2 = (jnp.dot(dg_ref[...], wg_ref[...], preferred_element_type=F32)
               + jnp.dot(dup_ref[...], wu_ref[...], preferred_element_type=F32))
        dx1, dg2 = _norm_bwd(dh2, x_ref[...], r_ref[...], g_ref[...], d_ref[...])
        dg2_ref[...] += dg2
        dx1_ref[...] = dx1
        dmix = lax.dot_general(dx1.astype(BF16), wo_ref[...], NT, preferred_element_type=F32)
        dattn_ref[...] = dmix[:, 0:ATTN_W]
        row0 = (nt - 1 - i) * tm
        for g, w in enumerate(POOL_WINDOWS):
            cols = slice(g * POOL_G, (g + 1) * POOL_G)
            wb = w_ref[g].astype(BF16)
            pooled_g = p_ref[:, cols]
            dpo = dmix[:, ATTN_W + g * POOL_G:ATTN_W + (g + 1) * POOL_G]
            mixed = jnp.dot(pooled_g, wb, preferred_element_type=F32)
            dsc_ref[:, cols] += jnp.sum(dpo * mixed, axis=0, keepdims=True)
            dmp = (dpo * sc_ref[:, cols]).astype(BF16)
            dw_ref[g] += lax.dot_general(pooled_g, dmp, TN, preferred_element_type=F32)
            dpooled = lax.dot_general(dmp, wb, NT, preferred_element_type=F32)
            a = dpooled / _pool_counts(row0, tm, w)
            acc = jnp.concatenate([a, head_ref[:, cols]], axis=0)
            head_ref[:, cols] = a[0:HALO, :]
            k = 1
            while k < w:
                acc = acc + pltpu.roll(acc, tm + HALO - k, axis=0)
                k *= 2
            du_ref[:, cols] = (acc[0:tm, :] - dpooled).astype(BF16)

    row = lambda w: pl.BlockSpec((tm, w), lambda i: (nt - 1 - i, 0))
    full = lambda a, b: pl.BlockSpec((a, b), lambda i: (0, 0))
    pool_w = pl.BlockSpec((ng, POOL_G, POOL_G), lambda i: (0, 0, 0))
    return pl.pallas_call(
        body,
        grid=(nt,),
        in_specs=[row(D_FF), row(D_FF), full(D_FF, D_MODEL), full(D_FF, D_MODEL), full(D_MODEL, D_MODEL),
                  row(D_MODEL), row(1), full(1, D_MODEL), row(D_MODEL), row(POOL_W), pool_w, full(1, POOL_W)],
        out_specs=[row(D_MODEL), row(ATTN_W), row(POOL_W), full(1, D_MODEL), pool_w, full(1, POOL_W)],
        out_shape=[SDS((s, D_MODEL), F32), SDS((s, ATTN_W), F32), SDS((s, POOL_W), BF16), SDS((1, D_MODEL), F32),
                   SDS((ng, POOL_G, POOL_G), F32), SDS((1, POOL_W), F32)],
        scratch_shapes=[pltpu.VMEM((HALO, POOL_W), F32)],
        compiler_params=_cparams("arbitrary"),
        name="mlp_in_pool_bwd",
    )(dgate, dup, wg_t, wu_t, w_out, x1, r2, g2, dx2, pooled, w_pool, pool_scale)


SUM_ROWS = 16


def _heads_t(t):
    n = t.shape[0]
    lane = _iota2((n, LANES), 1)
    tf = t.astype(F32)
    halves = jnp.concatenate([jnp.where(lane < HEAD_DIM, tf, 0.0).T, jnp.where(lane < HEAD_DIM, 0.0, tf).T], axis=1)
    r, c = _iota2((SUM_ROWS, 2 * n), 0), _iota2((SUM_ROWS, 2 * n), 1)
    ones = jnp.where(((r == 0) & (c < n)) | ((r == 4) & (c >= n)), 1.0, 0.0)
    return jnp.concatenate([halves, ones], axis=0).astype(BF16)


def _attn_bwd(qkv, attn_o, d_attn, rowb, c_rows, after, *, tq):
    s = qkv.shape[0]
    tk = tq
    nb = s // tq
    rows_t = LANES + SUM_ROWS

    def body(q_ref, k_ref, v_ref, o_ref, do_ref, rowb_ref, ck_ref, _, dq_ref, dk_ref, dv_ref, dck_ref, dcq_ref,
             dqt_ref, delta_ref, kp_ref, qp_ref, dob_ref, qt_ref, kt_ref, dot_ref, front_ref):
        lane = _iota2((tq, LANES), 1)
        lo = lane < HEAD_DIM
        first = _iota2((8, LANES), 1) < HEAD_DIM
        sel = jnp.where(_iota2((8, LANES), 0) < 4, jnp.where(first, 1.0, 0.0), jnp.where(first, 0.0, 1.0))

        def prep(b, _):
            st = pl.multiple_of(b * tq, tq)
            do2 = do_ref[pl.ds(st, tq), :]
            delta_ref[b] = _sel_dot(sel, do2 * o_ref[pl.ds(st, tq), :].astype(F32), NT)
            dob_ref[pl.ds(st, tq), :] = do2.astype(BF16)
            dqt_ref[b] = jnp.zeros((rows_t, tq), F32)
            k2 = k_ref[pl.ds(st, tq), :].astype(F32)
            q2 = q_ref[pl.ds(st, tq), :].astype(F32)
            ck = ck_ref[:, pl.ds(st, tq)]
            for h in range(2):
                kp_ref[h * nb + b] = _augment(k2, h, -ck[h:h + 1, :], True)
                qp_ref[h * nb + b] = _augment(q2 * Q_SCALE, h, None, False)
            qt_ref[b] = _heads_t(q2)
            kt_ref[b] = _heads_t(k2)
            dot_ref[b] = _heads_t(do2)[0:LANES, :]
            return 0

        lax.fori_loop(0, nb, prep, 0)

        def split(t):
            z = jnp.zeros_like(t)
            return jnp.where(lo, t, z), jnp.where(lo, z, t)

        def kv_block(j, _):
            st_j = pl.multiple_of(j * tk, tk)
            vs = split(v_ref[pl.ds(st_j, tk), :])
            kt = kt_ref[j]

            def stage(i, slot):
                ic = jnp.minimum(i, nb - 1)
                do2 = dob_ref[pl.ds(pl.multiple_of(ic * tq, tq), tq), :]
                for h in range(2):
                    front_ref[4 * slot + h] = lax.dot_general(kp_ref[h * nb + j], qp_ref[h * nb + ic], NT,
                                                              preferred_element_type=F32)
                    front_ref[4 * slot + 2 + h] = lax.dot_general(vs[h], do2, NT, preferred_element_type=F32)

            def q_block(i, slot, carry, diagonal):
                dkt, dvt = carry
                ic = jnp.minimum(i, nb - 1)
                rb = rowb_ref[ic] + jnp.where(i < nb, 0.0, NEG)
                dl = delta_ref[ic]
                pts, dsts = [], []
                for h in range(2):
                    st = front_ref[4 * slot + h] + rb[h:h + 1, :]
                    if diagonal:
                        st = jnp.where(_iota2((tk, tq), 0) <= _iota2((tk, tq), 1), st, NEG)
                    pt = jnp.exp(st)
                    pts.append(pt.astype(BF16))
                    dsts.append((pt * (front_ref[4 * slot + 2 + h] - dl[4 * h:4 * h + 1, :])).astype(BF16))
                dvt = dvt + lax.dot_general(dot_ref[ic], jnp.concatenate(pts, axis=1), NT, preferred_element_type=F32)
                dkt = dkt + lax.dot_general(qt_ref[ic], jnp.concatenate(dsts, axis=1), NT, preferred_element_type=F32)
                dqt_ref[ic] += jnp.dot(kt, jnp.concatenate(dsts, axis=0), preferred_element_type=F32)
                return dkt, dvt

            def run(i0, steps, carry):
                for d in range(steps):
                    stage(i0 + d + 1, d % 2)
                    carry = q_block(i0 + d, 1 - d % 2, carry, False)
                return carry

            stage(j, 0)
            stage(j + 1, 1)
            carry = q_block(j, 0, (jnp.zeros((rows_t, tk), F32), jnp.zeros((LANES, tk), F32)), True)
            first, left = j + 1, nb - 1 - j
            for size in UNROLLS:
                trips = _shift_div(left + 1 if size == UNROLLS[-1] else left, size)
                carry = lax.fori_loop(0, trips, lambda t, c, i0=first, n=size: run(i0 + n * t, n, c), carry)
                first, left = first + size * trips, left - size * trips
            dkt, dvt = carry
            dk_ref[pl.ds(st_j, tk), :] = (dkt[0:LANES, :].T * Q_SCALE).astype(BF16)
            dv_ref[pl.ds(st_j, tk), :] = dvt.T.astype(BF16)
            dck_ref[j] = dkt[LANES:LANES + 8, :]
            return 0

        lax.fori_loop(0, nb, kv_block, 0)

        def finish(b, _):
            acc = dqt_ref[b]
            dq_ref[pl.ds(pl.multiple_of(b * tq, tq), tq), :] = (acc[0:LANES, :].T * Q_SCALE).astype(BF16)
            dcq_ref[b] = acc[LANES:LANES + 8, :]
            return 0

        lax.fori_loop(0, nb, finish, 0)

    col = lambda off: pl.BlockSpec((s, LANES), lambda p: (0, off + p))
    sums = pl.BlockSpec((None, nb, 8, tq), lambda p: (p, 0, 0, 0))
    return pl.pallas_call(
        body,
        grid=(N_PAIRS,),
        in_specs=[col(0), col(N_PAIRS), col(2 * N_PAIRS), col(0), col(0),
                  pl.BlockSpec((None, nb, 2, tq), lambda p: (p, 0, 0, 0)),
                  pl.BlockSpec((None, 2, s), lambda p: (p, 0, 0)), _UNREAD],
        out_specs=[col(0), col(0), col(0), sums, sums],
        out_shape=[SDS((s, ATTN_W), BF16), SDS((s, ATTN_W), BF16), SDS((s, ATTN_W), BF16),
                   SDS((N_PAIRS, nb, 8, tq), F32), SDS((N_PAIRS, nb, 8, tq), F32)],
        scratch_shapes=[pltpu.VMEM((nb, rows_t, tq), F32), pltpu.VMEM((nb, 8, tq), F32),
                        pltpu.VMEM((2 * nb, tk, LANES), BF16), pltpu.VMEM((2 * nb, tq, LANES), BF16),
                        pltpu.VMEM((s, LANES), BF16), pltpu.VMEM((nb, rows_t, 2 * tq), BF16),
                        pltpu.VMEM((nb, rows_t, 2 * tk), BF16), pltpu.VMEM((nb, LANES, 2 * tq), BF16),
                        pltpu.VMEM((8, tk, tq), F32)],
        compiler_params=_cparams("arbitrary"),
        name="attn_bwd",
    )(qkv, qkv, qkv, attn_o, d_attn, rowb, c_rows, after)


def _forget_bwd(dc_t, fl_t, b_rows):
    rows = fl_t.shape[0]
    nb = rows // N_HEADS

    def body(dc_ref, fl_ref, b_ref, dfl_ref, db_ref):
        dc = dc_ref[...]
        lower = _iota2((LANES, LANES), 0) >= _iota2((LANES, LANES), 1)
        ones = jnp.ones((LANES, LANES), F32)
        rr, cc, same = _head_block_masks(rows, nb)
        dlf = _dot_sel(dc, lower) + _sel_dot(same & (cc > rr), _dot_sel(dc, ones))
        dfl = dlf / (1.0 + jnp.exp(fl_ref[...] + b_ref[...]))
        dfl_ref[...] = dfl
        shift = nb.bit_length() - 1
        hsel = lax.shift_right_logical(_iota2((N_HEADS, rows), 1), shift) == _iota2((N_HEADS, rows), 0)
        db_ref[...] = _sel_dot(hsel, _dot_sel(dfl, ones))

    return pl.pallas_call(body, out_shape=[SDS(fl_t.shape, F32), SDS((N_HEADS, LANES), F32)],
                          compiler_params=_cparams(), name="forget_bwd")(dc_t, fl_t, b_rows)


def _in_bwd(dq, dk, dv, du, dfl, w_in_t, x, r1, g1, dx1, after, *, tm):
    s = x.shape[0]
    pieces = ((0, ATTN_W), (ATTN_W, 2 * ATTN_W), (2 * ATTN_W, QKV_W), (U_OFF, F_OFF), (F_OFF, IN_PAD))

    def body(dq_ref, dk_ref, dv_ref, du_ref, df_ref, w_ref, x_ref, r_ref, g_ref, d_ref, _, dx_ref, dg1_ref):
        @pl.when(pl.program_id(0) == 0)
        def _():
            dg1_ref[...] = jnp.zeros_like(dg1_ref)

        dh = None
        for ref, (c0, c1) in zip((dq_ref, dk_ref, dv_ref, du_ref, df_ref), pieces):
            t = jnp.dot(ref[...], w_ref[c0:c1, :], preferred_element_type=F32)
            dh = t if dh is None else dh + t
        dx, dg1 = _norm_bwd(dh, x_ref[...], r_ref[...], g_ref[...], d_ref[...])
        dx_ref[...] = dx
        dg1_ref[...] += dg1

    row = lambda w: pl.BlockSpec((tm, w), lambda i: (i, 0))
    full = lambda a, b: pl.BlockSpec((a, b), lambda i: (0, 0))
    return pl.pallas_call(
        body,
        grid=(s // tm,),
        in_specs=[row(ATTN_W), row(ATTN_W), row(ATTN_W), row(POOL_W), row(LANES), full(IN_PAD, D_MODEL),
                  row(D_MODEL), row(1), full(1, D_MODEL), row(D_MODEL), _UNREAD],
        out_specs=[row(D_MODEL), full(1, D_MODEL)],
        out_shape=[SDS((s, D_MODEL), F32), SDS((1, D_MODEL), F32)],
        compiler_params=_cparams("arbitrary"),
        name="in_bwd",
    )(dq, dk, dv, du, dfl, w_in_t, x, r1, g1, dx1, after)


def _tiles(s):
    big = min(512, s)
    return dict(row=big, attn=min(256, s // 2), ff_rows=min(256, s), tall=min(1024, s))


def _local_step(x, tgt, p, weight, emit, started):
    s = x.shape[0]
    t = _tiles(s)
    tm, tq = t["row"], t["attn"]
    nb = s // LANES
    nqb = s // tq
    g1, g2, gf = p["norm1_g"], p["norm2_g"], p["final_g"].reshape(1, D_MODEL)
    w_pool, pool_scale = p["w_pool"][0], p["pool_scale"]

    h, r1 = _norm1(x, g1, started, tm=tm)
    w_in_t = weight("w_in", h)
    qkv, fl, pooled, pool_o = _in_proj_pool(h, w_in_t, w_pool, pool_scale, tm=t["tall"])
    fl_t = fl[:, :N_HEADS].T.reshape(N_HEADS * nb, LANES)
    b_rows = jnp.repeat(p["b_forget"].reshape(N_HEADS), nb).reshape(N_HEADS * nb, 1)
    c = _forget_cumsum(fl_t, b_rows).reshape(N_PAIRS, 2, s)
    c_rowblk = c.reshape(N_PAIRS, 2, nqb, tq).transpose(0, 2, 1, 3)
    attn_o, lse = _attn_fwd(qkv, c, tk=tq)
    lse = lse.reshape(N_PAIRS, nqb // 2, 2, 2, tq).transpose(0, 1, 3, 2, 4).reshape(N_PAIRS, nqb, 2, tq)
    w_out = weight("w_out", attn_o)
    wg_t, wu_t = weight("w_gate_up", attn_o)
    x1, h2, r2, gate, up, act = _out_gate_up(attn_o, pool_o, w_out, x, g2, wg_t, wu_t, tm=t["ff_rows"])
    wd = weight("w_down", act)
    dx2, loss_row, d_gf = _down_final(act, wd, x1, gf, tgt, tm=tm, sub=min(128, tm))

    dgate, dup = _swiglu_bwd(dx2, wd, gate, up, tm=t["ff_rows"], tn=D_FF)
    d_wd, d_wg_t, d_wu_t = _mm_tn_phases([act, dgate, dup], [dx2, h2, h2], ts=t["row"], name="grad_w_ff")
    dx1, d_attn, du, d_g2, d_wpool, d_pscale = _mlp_in_pool_bwd(dgate, dup, wg_t, wu_t, w_out, x1, r2, g2, dx2, pooled,
                                                               w_pool, pool_scale, tm=t["ff_rows"])
    d_wo = _mm_tn_stacked([attn_o, pool_o], [ATTN_W, POOL_W], dx1, ts=t["tall"], name="grad_w_out")
    token = emit(("w_down", "w_gate", "w_up", "w_out"), (d_wd, d_wg_t, d_wu_t, d_wo))
    dq, dk, dv, dck, dcq = _attn_bwd(qkv, attn_o, d_attn, c_rowblk - lse, c, token, tq=tq)
    dc_t = (dcq - dck)[:, :, 0::4, :].transpose(0, 2, 1, 3).reshape(N_HEADS * nb, LANES)
    dfl_t, db = _forget_bwd(dc_t, fl_t, b_rows)
    dfl = jnp.pad(dfl_t.reshape(N_HEADS, s).T, ((0, 0), (0, LANES - N_HEADS))).astype(BF16)
    d_w_in_t = _mm_tn_stacked([dq, dk, dv, dfl, du], [ATTN_W, ATTN_W, ATTN_W, N_HEADS, POOL_W], h, ts=t["tall"],
                              name="grad_w_in",
                              windows=(N_DEV, IN_STEP, IN_WINDOW))
    token = emit(("w_in",), (d_w_in_t,))
    dx, d_g1 = _in_bwd(dq, dk, dv, du, dfl, w_in_t, x, r1, g1, dx1, token, tm=tm)

    small = dict(norm1_g=d_g1, b_forget=db[:, 0].reshape(1, N_HEADS), w_pool=d_wpool, pool_scale=d_pscale,
                 norm2_g=d_g2, final_g=d_gf)
    return loss_row, dx, small


def _my_index():
    return 4 * lax.axis_index("x") + 2 * lax.axis_index("y") + lax.axis_index("c")


def _peer(k):
    pos = [lax.axis_index(a) for a in ("x", "y", "c")]
    flipped = tuple(1 - p if (k >> b) & 1 else p for p, b in zip(pos, (2, 1, 0)))
    return flipped, 4 * flipped[0] + 2 * flipped[1] + flipped[2]


_HBM = pl.BlockSpec(memory_space=pltpu.HBM)
_SEM = pl.BlockSpec(memory_space=pltpu.SEMAPHORE)
_DATAFLOW = pltpu.SideEffectType.DATAFLOW_SIDE_EFFECTING


ALL_PEERS = tuple(range(1, N_DEV))
SAME_CORE = (1, 2, 4, 6)


def _peer_copies(ins, lands, send_sems, recv_sems, scatter, peers, arrivals):
    me = _my_index()
    copies = []
    for w in range(len(ins)):
        for k in peers[w]:
            dev, idx = _peer(k)
            copies.append(pltpu.make_async_remote_copy(
                src_ref=ins[w].at[idx] if scatter[w] else ins[w], dst_ref=lands[w].at[idx if arrivals else me],
                send_sem=send_sems[w].at[k - 1], recv_sem=recv_sems[w].at[k - 1], device_id=dev, device_id_type=MESH))
    return copies


def _own_copies(ins, lands, send_sems, scatter):
    me = _my_index()
    return [pltpu.make_async_copy(ins[w].at[me] if scatter[w] else ins[w], lands[w].at[me], send_sems[w].at[N_DEV - 1])
            for w in range(len(ins))]


def _forward_copies(land, send_sems, recv_sems, arrivals):
    sibling, _ = _peer(1)
    copies = []
    for j, k in enumerate(SAME_CORE[1:]):
        src, dst = _peer(k)[1], _peer(k ^ 1 if arrivals else k)[1]
        copies.append(pltpu.make_async_remote_copy(
            src_ref=land.at[src], dst_ref=land.at[dst], send_sem=send_sems.at[j], recv_sem=recv_sems.at[j],
            device_id=sibling, device_id_type=MESH))
    return copies


def _forward_start(land, name):
    def body(land_ref, send_sems, recv_sems, land_thru, token):
        for cp in _forward_copies(land_ref, send_sems, recv_sems, False):
            cp.start()
        token[...] = jnp.zeros_like(token)

    sem = pltpu.SemaphoreType.DMA((len(SAME_CORE) - 1,))
    send, recv, thru, _ = pl.pallas_call(
        body,
        in_specs=[_HBM],
        out_specs=[_SEM, _SEM, _HBM, pl.BlockSpec(memory_space=pltpu.VMEM)],
        out_shape=[sem, sem, pltpu.HBM(land.shape, land.dtype), SDS((8, LANES), F32)],
        input_output_aliases={0: 2},
        compiler_params=pltpu.CompilerParams(has_side_effects=_DATAFLOW),
        name=name,
    )(land)
    return send, recv, thru


def _forward_wait(handle, after, name):
    def body(land_ref, send_sems, recv_sems, after_ref, land_out):
        for cp in _forward_copies(land_ref, send_sems, recv_sems, False):
            cp.wait_send()
        for cp in _forward_copies(land_ref, send_sems, recv_sems, True):
            cp.wait_recv()

    send, recv, land = handle
    return pl.pallas_call(
        body,
        in_specs=[_HBM, _SEM, _SEM, pl.BlockSpec(memory_space=pl.ANY)],
        out_specs=_HBM,
        out_shape=pltpu.HBM(land.shape, land.dtype),
        input_output_aliases={0: 0},
        compiler_params=pltpu.CompilerParams(has_side_effects=_DATAFLOW),
        name=name,
    )(land, send, recv, after)


def _exchange_start(arrays, scatter, name, peers=None, after=None):
    n = len(arrays)
    peers = peers or [ALL_PEERS] * n
    order = [] if after is None else [after]
    land_shapes = [(N_DEV,) + tuple(a.shape[1:] if sc else a.shape) for a, sc in zip(arrays, scatter)]

    def body(*refs):
        ins, lands = refs[:n], refs[n:2 * n]
        outs = refs[2 * n + len(order):]
        send_sems, recv_sems, token = outs[:n], outs[n:2 * n], outs[4 * n]
        for cp in _peer_copies(ins, lands, send_sems, recv_sems, scatter, peers, False):
            cp.start()
        for cp in _own_copies(ins, lands, send_sems, scatter):
            cp.start()
        token[...] = jnp.zeros_like(token)

    sends, recvs = pltpu.SemaphoreType.DMA((N_DEV,)), pltpu.SemaphoreType.DMA((N_DEV - 1,))
    outs = pl.pallas_call(
        body,
        in_specs=[_HBM] * (2 * n) + [_UNREAD] * len(order),
        out_specs=[_SEM] * (2 * n) + [_HBM] * (2 * n) + [pl.BlockSpec(memory_space=pltpu.VMEM)],
        out_shape=[sends] * n + [recvs] * n + [pltpu.HBM(a.shape, a.dtype) for a in arrays]
        + [pltpu.HBM(sh, a.dtype) for sh, a in zip(land_shapes, arrays)] + [SDS((8, LANES), F32)],
        input_output_aliases={i: 2 * n + i for i in range(2 * n)},
        compiler_params=pltpu.CompilerParams(has_side_effects=_DATAFLOW),
        name=name,
    )(*[pltpu.with_memory_space_constraint(a, pltpu.HBM) for a in arrays],
      *[pltpu.with_memory_space_constraint(lax.empty(sh, a.dtype), pltpu.HBM) for sh, a in zip(land_shapes, arrays)],
      *order)
    handles = [dict(send=outs[w], recv=outs[n + w], src=outs[2 * n + w], land=outs[3 * n + w], scatter=scatter[w],
                    peers=peers[w]) for w in range(n)]
    return handles, outs[4 * n]


def _exchange_wait(handles, after, name):
    n = len(handles)
    scatter, peers = [h["scatter"] for h in handles], [h["peers"] for h in handles]

    def body(*refs):
        ins, lands = refs[:n], refs[n:2 * n]
        send_sems, recv_sems = refs[2 * n:3 * n], refs[3 * n:4 * n]
        for cp in _peer_copies(ins, lands, send_sems, recv_sems, scatter, peers, False):
            cp.wait_send()
        for cp in _peer_copies(ins, lands, send_sems, recv_sems, scatter, peers, True):
            cp.wait_recv()
        for cp in _own_copies(ins, lands, send_sems, scatter):
            cp.wait()

    srcs, lands = [h["src"] for h in handles], [h["land"] for h in handles]
    outs = pl.pallas_call(
        body,
        in_specs=[_HBM] * (2 * n) + [_SEM] * (2 * n) + [pl.BlockSpec(memory_space=pl.ANY)],
        out_specs=[_HBM] * (2 * n),
        out_shape=[pltpu.HBM(a.shape, a.dtype) for a in srcs + lands],
        input_output_aliases={i: i for i in range(2 * n)},
        compiler_params=pltpu.CompilerParams(has_side_effects=_DATAFLOW),
        name=name,
    )(*srcs, *lands, *[h["send"] for h in handles], *[h["recv"] for h in handles], after)
    return outs[n:]


def _adamw(parts, w, m, v, name):
    rows, cols = w.shape
    tr = rows // 4 if rows % 32 == 0 else rows

    def body(p_ref, w_ref, m_ref, v_ref, g_ref, d_ref, mo_ref, vo_ref):
        g = p_ref[0].astype(F32)
        for d in range(1, N_DEV):
            g = g + p_ref[d].astype(F32)
        g_ref[...] = g
        d_ref[...], mo_ref[...], vo_ref[...] = _adam_update(g, w_ref[...], m_ref[...], v_ref[...])

    blk = pl.BlockSpec((tr, cols), lambda i: (i, 0))
    return pl.pallas_call(
        body,
        grid=(rows // tr,),
        in_specs=[pl.BlockSpec((N_DEV, tr, cols), lambda i: (0, i, 0)), blk, blk, blk],
        out_specs=[blk] * 4,
        out_shape=[SDS((rows, cols), F32)] * 4,
        compiler_params=_cparams("arbitrary"),
        name=name,
    )(parts, w, m, v)


def _adamw_dense(parts, w, m, v, name, *, rows, shift):
    _, window, cols = parts.shape
    per_row = cols // LANES

    def body(p_ref, w_ref, m_ref, v_ref, g_ref, d_ref, mo_ref, vo_ref, sum_ref):
        g = p_ref[0].astype(F32)
        for d in range(1, N_DEV):
            g = g + p_ref[d].astype(F32)
        sum_ref[...] = g
        me = _my_index()
        for j in range(N_DEV):
            @pl.when(me == j)
            def _(j=j):
                for c in range(per_row):
                    at = (pl.ds(c, rows, stride=per_row), slice(None))
                    gc = sum_ref[j * shift:j * shift + rows, c * LANES:(c + 1) * LANES]
                    g_ref[at] = gc
                    d_ref[at], mo_ref[at], vo_ref[at] = _adam_update(gc, w_ref[at], m_ref[at], v_ref[at])

    return pl.pallas_call(
        body,
        out_shape=[SDS(w.shape, F32)] * 4,
        scratch_shapes=[pltpu.VMEM((window, cols), F32)],
        compiler_params=_cparams(),
        name=name,
    )(parts, w, m, v)


_ROW_OF = dict(norm1_g=(0, D_MODEL), norm2_g=(1, D_MODEL), final_g=(2, D_MODEL), pool_scale=(3, POOL_W),
               b_forget=(4, N_HEADS), loss=(5, 1))


def _pack_rows(vals):
    rows = [jnp.pad(vals[n].reshape(1, width).astype(F32), ((0, 0), (0, D_MODEL - width)))
            for n, (_, width) in sorted(_ROW_OF.items(), key=lambda kv: kv[1][0])]
    return jnp.concatenate(rows + [jnp.zeros((8 - len(rows), D_MODEL), F32)], axis=0)


def _adam_update(g, w, m, v):
    m_new = ADAM_B1 * m + (1.0 - ADAM_B1) * g
    v_new = ADAM_B2 * v + (1.0 - ADAM_B2) * (g * g)
    m_hat = m_new / (1.0 - ADAM_B1 ** ADAM_STEP)
    v_hat = v_new / (1.0 - ADAM_B2 ** ADAM_STEP)
    return -ADAM_LR * (m_hat / (jnp.sqrt(v_hat) + ADAM_EPS) + ADAM_WD * w), m_new, v_new


def _adamw_replicated(parts_rows, parts_pool, w, m, v):
    names = ("norm1_g", "norm2_g", "final_g", "pool_scale", "b_forget", "w_pool")
    shapes = {n: ((len(POOL_WINDOWS), POOL_G, POOL_G) if n == "w_pool" else (1, _ROW_OF[n][1])) for n in names}

    def body(rows_ref, pool_ref, *refs):
        ins, outs = refs[:3 * len(names)], refs[3 * len(names):]

        def total(n):
            if n == "w_pool":
                pieces = [pool_ref[d] for d in range(N_DEV)]
            else:
                row, width = _ROW_OF[n]
                pieces = [rows_ref[d, row:row + 1, 0:width] for d in range(N_DEV)]
            g = pieces[0]
            for p in pieces[1:]:
                g = g + p
            return g

        outs[0][...] = total("loss")
        for k, n in enumerate(names):
            g = total(n)
            delta, m_new, v_new = _adam_update(g, ins[3 * k][...], ins[3 * k + 1][...], ins[3 * k + 2][...])
            for o_ref, val in zip(outs[1 + 4 * k:5 + 4 * k], (g, delta, m_new, v_new)):
                o_ref[...] = val

    args = [d[n].reshape(shapes[n]) for n in names for d in (w, m, v)]
    res = pl.pallas_call(
        body,
        out_shape=[SDS((1, 1), F32)] + [SDS(shapes[n], F32) for n in names for _ in range(4)],
        compiler_params=_cparams(),
        name="adamw_replicated",
    )(parts_rows, parts_pool, *args)
    return res[0], {n: [r.reshape(w[n].shape) for r in res[1 + 4 * k:5 + 4 * k]] for k, n in enumerate(names)}


def kernel(x, norm1_g, w_in, b_forget, w_pool, pool_scale, w_out, norm2_g, w_gate, w_up, w_down, final_g, loss_target, m_norm1_g, m_w_in, m_b_forget, m_w_pool, m_pool_scale, m_w_out, m_norm2_g, m_w_gate, m_w_up, m_w_down, m_final_g, v_norm1_g, v_w_in, v_b_forget, v_w_pool, v_pool_scale, v_w_out, v_norm2_g, v_w_gate, v_w_up, v_w_down, v_final_g):
    big = ("w_in", "w_out", "w_gate", "w_up", "w_down")
    order = ("norm1_g", "w_in", "b_forget", "w_pool", "pool_scale", "w_out", "norm2_g", "w_gate", "w_up", "w_down",
             "final_g")
    w = dict(norm1_g=norm1_g, w_in=w_in, b_forget=b_forget, w_pool=w_pool, pool_scale=pool_scale, w_out=w_out,
             norm2_g=norm2_g, w_gate=w_gate, w_up=w_up, w_down=w_down, final_g=final_g)
    m = dict(norm1_g=m_norm1_g, w_in=m_w_in, b_forget=m_b_forget, w_pool=m_w_pool, pool_scale=m_pool_scale,
             w_out=m_w_out, norm2_g=m_norm2_g, w_gate=m_w_gate, w_up=m_w_up, w_down=m_w_down, final_g=m_final_g)
    v = dict(norm1_g=v_norm1_g, w_in=v_w_in, b_forget=v_b_forget, w_pool=v_w_pool, pool_scale=v_pool_scale,
             w_out=v_w_out, norm2_g=v_norm2_g, w_gate=v_w_gate, w_up=v_w_up, w_down=v_w_down, final_g=v_final_g)

    flipped = ("w_in", "w_gate", "w_up")
    shard = lambda d, n: d[n][0].T if n in flipped else d[n][0]
    cast = lambda n: shard(w, n).astype(BF16)
    (first,), started = _exchange_start([cast("w_in")], [False], "gather_start_w_in", peers=[SAME_CORE])
    gather = dict(w_in=first)

    def gathered(names, after):
        return _exchange_wait([gather[n] for n in names], after, "gather_wait_" + names[0])

    def weight(name, after):
        if name == "w_in":
            handles, token = _exchange_start([cast(n) for n in big[1:]], [False] * len(big[1:]), "gather_start", after=after)
            gather.update(zip(big[1:], handles))
            forward = _forward_start(gathered(["w_in"], token)[0], "gather_forward_start")
            full = _forward_wait(forward, after, "gather_forward_wait").reshape(IN_W, D_MODEL)
            f0 = QKV_W + N_HEADS
            return jnp.concatenate([full[:QKV_W], full[f0:], full[QKV_W:f0],
                                    jnp.zeros((IN_PAD - IN_W, D_MODEL), BF16)], axis=0)
        if name == "w_out":
            return gathered(["w_out"], after)[0].reshape(D_MODEL, D_MODEL)
        if name == "w_gate_up":
            return [g.reshape(D_FF, D_MODEL) for g in gathered(["w_gate", "w_up"], after)]
        return gathered(["w_down"], after)[0].reshape(D_FF, D_MODEL)

    rows = lambda g: g if g.ndim == 3 else g.reshape(N_DEV, g.shape[0] // N_DEV, g.shape[1])
    sent = {}

    def emit(names, grads):
        handles, token = _exchange_start([rows(g) for g in grads], [True] * len(names), "grads_start_" + names[0])
        sent.update(zip(names, handles))
        return token

    loss_row, dx, small_grads = _local_step(x[0], loss_target[0], w, weight, emit, started)

    packed = _pack_rows(dict(small_grads, loss=0.5 / D_MODEL * jnp.sum(loss_row)))
    small_handles, after = _exchange_start([packed, small_grads["w_pool"]], [False, False], "grads_start_replicated")

    outs = {}
    for name in ("w_down", "w_gate", "w_up", "w_out", "w_in"):
        (parts,) = _exchange_wait([sent[name]], after, "grads_wait_" + name)
        if name == "w_in":
            dense = lambda d: d[name].transpose(2, 0, 1).reshape(-1, LANES)
            outs[name] = _adamw_dense(parts, dense(w), dense(m), dense(v), "adamw_" + name, rows=IN_SHARD, shift=IN_SHIFT)
            after = outs[name][0]
            outs[name] = [a.reshape(-1, D_MODEL // LANES, LANES).transpose(1, 2, 0).reshape(1, D_MODEL, -1)
                          for a in outs[name]]
            continue
        outs[name] = _adamw(parts, shard(w, name), shard(m, name), shard(v, name), "adamw_" + name)
        after = outs[name][0]
        outs[name] = [(a.T if name in flipped else a)[None] for a in outs[name]]
    parts_rows, parts_pool = _exchange_wait(small_handles, after, "grads_wait_replicated")
    loss, small = _adamw_replicated(parts_rows, parts_pool, w, m, v)
    outs.update(small)

    return (loss.reshape(()), dx[None]) + tuple(outs[n][k] for k in range(4) for n in order)
```

```python
import jax
import jax.numpy as jnp
from jax import lax
from jax.experimental import pallas as pl
from jax.experimental.pallas import tpu as pltpu

F32 = jnp.float32
BF16 = jnp.bfloat16
SDS = jax.ShapeDtypeStruct

D_MODEL = 1024
ATTN_W = 512
N_HEADS = 8
HEAD_DIM = 64
Q_SCALE = HEAD_DIM ** -0.5
N_PAIRS = N_HEADS // 2
POOL_W = 512
POOL_WINDOWS = (2, 4, 8, 16)
POOL_G = 128
HALO = 16
IN_W = 3 * ATTN_W + N_HEADS + POOL_W
QKV_W = 3 * ATTN_W
U_OFF = QKV_W
F_OFF = QKV_W + POOL_W
IN_PAD = F_OFF + 128
D_FF = 2816
EPS = 1e-6
NEG = -1e30
N_DEV = 8
LANES = 128
BF16_ROWS = 16

IN_SHARD = IN_W // N_DEV
IN_STEP = IN_SHARD // BF16_ROWS * BF16_ROWS
IN_SHIFT = IN_SHARD - IN_STEP
IN_WINDOW = -(-((N_DEV - 1) * IN_SHIFT + IN_SHARD) // BF16_ROWS) * BF16_ROWS

ADAM_LR = 0.001
ADAM_B1 = 0.9
ADAM_B2 = 0.999
ADAM_EPS = 1e-08
ADAM_WD = 0.01
ADAM_STEP = 10

VMEM_LIMIT_BYTES = 56 * 1024 * 1024
MESH = pl.DeviceIdType.MESH
NT = (((1,), (1,)), ((), ()))
TN = (((0,), (0,)), ((), ()))


_UNREAD = pl.BlockSpec(memory_space=pl.ANY)


def _cparams(*sem):
    return pltpu.CompilerParams(dimension_semantics=sem or None, vmem_limit_bytes=VMEM_LIMIT_BYTES)


def _split3(a):
    hi = a.astype(BF16)
    r1 = a - hi.astype(F32)
    mid = r1.astype(BF16)
    lo = (r1 - mid.astype(F32)).astype(BF16)
    return hi, mid, lo


def _dot_sel(a, sel, dims=None):
    sb = sel.astype(BF16)
    if dims is None:
        return sum(jnp.dot(p, sb, preferred_element_type=F32) for p in _split3(a))
    return sum(lax.dot_general(p, sb, dims, preferred_element_type=F32) for p in _split3(a))


def _sel_dot(sel, a, dims=None):
    sb = sel.astype(BF16)
    if dims is None:
        return sum(jnp.dot(sb, p, preferred_element_type=F32) for p in _split3(a))
    return sum(lax.dot_general(sb, p, dims, preferred_element_type=F32) for p in _split3(a))


def _iota2(shape, dim):
    return lax.broadcasted_iota(jnp.int32, shape, dim)


UNROLLS = (8, 4, 2)


def _shift_div(x, n):
    return lax.shift_right_logical(x, n.bit_length() - 1)


def _norm1(x, g1, after, *, tm):
    s = x.shape[0]

    def body(x_ref, g_ref, _, h_ref, r_ref):
        xv = x_ref[...]
        r = lax.rsqrt(jnp.mean(xv * xv, axis=-1, keepdims=True) + EPS)
        h_ref[...] = (xv * r * g_ref[...]).astype(BF16)
        r_ref[...] = r

    row = lambda w: pl.BlockSpec((tm, w), lambda i: (i, 0))
    return pl.pallas_call(
        body,
        grid=(s // tm,),
        in_specs=[row(D_MODEL), pl.BlockSpec((1, D_MODEL), lambda i: (0, 0)), _UNREAD],
        out_specs=[row(D_MODEL), row(1)],
        out_shape=[SDS((s, D_MODEL), BF16), SDS((s, 1), F32)],
        compiler_params=_cparams("arbitrary"),
        name="norm1",
    )(x, g1, after)


def _in_proj_pool(h, w_in_t, w_pool, pool_scale, *, tm):
    s = h.shape[0]

    def body(h_ref, w_ref, wp_ref, sc_ref, qkv_ref, fl_ref, pooled_ref, po_ref, tail_ref):
        i = pl.program_id(0)

        @pl.when(i == 0)
        def _():
            tail_ref[...] = jnp.zeros_like(tail_ref)

        hv = h_ref[...]
        uv = lax.dot_general(hv, w_ref[U_OFF:F_OFF, :], NT, preferred_element_type=F32)
        qkv_ref[...] = lax.dot_general(hv, w_ref[0:QKV_W, :], NT, preferred_element_type=F32).astype(BF16)
        fl_ref[...] = lax.dot_general(hv, w_ref[F_OFF:IN_PAD, :], NT, preferred_element_type=F32)
        ext = jnp.concatenate([tail_ref[...], uv], axis=0)
        tail_ref[...] = uv[tm - HALO:, :]
        for g, w in enumerate(POOL_WINDOWS):
            cols = slice(g * POOL_G, (g + 1) * POOL_G)
            acc = ext[:, cols]
            k = 1
            while k < w:
                acc = acc + pltpu.roll(acc, k, axis=0)
                k *= 2
            pooled = (acc[HALO:, :] / _pool_counts(i * tm, tm, w) - uv[:, cols]).astype(BF16)
            pooled_ref[:, cols] = pooled
            mixed = jnp.dot(pooled, wp_ref[g].astype(BF16), preferred_element_type=F32)
            po_ref[:, cols] = (mixed * sc_ref[:, cols]).astype(BF16)

    row = lambda w: pl.BlockSpec((tm, w), lambda i: (i, 0))
    return pl.pallas_call(
        body,
        grid=(s // tm,),
        in_specs=[row(D_MODEL), pl.BlockSpec((IN_PAD, D_MODEL), lambda i: (0, 0)),
                  pl.BlockSpec((len(POOL_WINDOWS), POOL_G, POOL_G), lambda i: (0, 0, 0)),
                  pl.BlockSpec((1, POOL_W), lambda i: (0, 0))],
        out_specs=[row(QKV_W), row(LANES), row(POOL_W), row(POOL_W)],
        out_shape=[SDS((s, QKV_W), BF16), SDS((s, LANES), F32), SDS((s, POOL_W), BF16), SDS((s, POOL_W), BF16)],
        scratch_shapes=[pltpu.VMEM((HALO, POOL_W), F32)],
        compiler_params=_cparams("arbitrary"),
        name="in_proj_pool",
    )(h, w_in_t, w_pool, pool_scale)


def _head_block_masks(rows, nb):
    shift = nb.bit_length() - 1
    rr, cc = _iota2((rows, rows), 0), _iota2((rows, rows), 1)
    same = lax.shift_right_logical(rr, shift) == lax.shift_right_logical(cc, shift)
    return rr, cc, same


def _forget_cumsum(fl_t, b_rows):
    rows = fl_t.shape[0]
    nb = rows // N_HEADS

    def body(fl_ref, b_ref, c_ref):
        z = fl_ref[...] + b_ref[...]
        lf = jnp.minimum(z, 0.0) - jnp.log1p(jnp.exp(-jnp.abs(z)))
        upper = _iota2((LANES, LANES), 0) <= _iota2((LANES, LANES), 1)
        within = _dot_sel(lf, upper)
        tot = _dot_sel(lf, jnp.ones((LANES, LANES), F32))
        rr, cc, same = _head_block_masks(rows, nb)
        c_ref[...] = within + _sel_dot(same & (cc < rr), tot)

    return pl.pallas_call(body, out_shape=SDS(fl_t.shape, F32), compiler_params=_cparams(), name="forget_cumsum")(
        fl_t, b_rows)


BIAS_LANES = 3


def _augment(t, h, bias, col_first):
    n = t.shape[0]
    lane = _iota2((n, LANES), 1)
    own = (lane < HEAD_DIM) if h == 0 else (lane >= HEAD_DIM)
    b0 = HEAD_DIM if h == 0 else 0
    c0, o0 = (b0, b0 + BIAS_LANES) if col_first else (b0 + BIAS_LANES, b0)
    x = 0.0
    if bias is not None:
        row = _iota2((BF16_ROWS, n), 0)
        pieces = jnp.zeros((BF16_ROWS, n), F32)
        for off, piece in enumerate(_split3(bias)):
            pieces = jnp.where(row == off, piece.astype(F32), pieces)
        r, ln = _iota2((BF16_ROWS, LANES), 0), _iota2((BF16_ROWS, LANES), 1)
        place = jnp.where(r < BIAS_LANES, jnp.where(ln == c0 + r, 1.0, 0.0), 0.0).astype(BF16)
        x = lax.dot_general(pieces.astype(BF16), place, TN, preferred_element_type=F32)
    x = jnp.where(own, t, x)
    x = jnp.where((lane >= o0) & (lane < o0 + BIAS_LANES), 1.0, x)
    return x.astype(BF16)


def _attn_fwd(qkv, c_rows, *, tk):
    s = qkv.shape[0]
    tq = 2 * tk
    nb = s // tk

    def body(q_ref, k_ref, v_ref, cq_ref, ck_ref, o_ref, lse_ref, kp_ref, vt_ref, st_ref):
        i = pl.program_id(1)

        @pl.when(i == 0)
        def _():
            def prep(jb, _):
                st = pl.multiple_of(jb * tk, tk)
                k2 = k_ref[pl.ds(st, tk), :].astype(F32)
                ck = ck_ref[:, pl.ds(st, tk)]
                for h in range(2):
                    kp_ref[h * nb + jb] = _augment(k2, h, -ck[h:h + 1, :], True)
                vt_ref[jb] = v_ref[pl.ds(st, tk), :].astype(F32).T.astype(BF16)
                return 0

            lax.fori_loop(0, nb, prep, 0)

        qs = q_ref[...].astype(F32) * Q_SCALE
        cq = cq_ref[...]
        qp = [_augment(qs, h, cq[h:h + 1, :], False) for h in range(2)]

        def logits(j):
            return tuple(lax.dot_general(kp_ref[h * nb + j], qp[h], NT, preferred_element_type=F32) for h in range(2))

        def softmax_pv(j, slot, stats, masked):
            out = []
            for h in range(2):
                m, l, acc = stats[h]
                st = st_ref[2 * slot + h]
                if masked:
                    st = jnp.where(j * tk + _iota2((tk, tq), 0) <= i * tq + _iota2((tk, tq), 1), st, NEG)
                m_new = jnp.maximum(m, jnp.max(st, axis=0, keepdims=True))
                alpha = jnp.exp(m - m_new)
                p = jnp.exp(st - m_new)
                l = alpha * l + jnp.sum(p, axis=0, keepdims=True)
                vt = vt_ref[j, h * HEAD_DIM:(h + 1) * HEAD_DIM, :]
                acc = alpha * acc + jnp.dot(vt, p.astype(BF16), preferred_element_type=F32)
                out.append((m_new, l, acc))
            return tuple(out)

        def put(slot, j):
            for h, st in enumerate(logits(j)):
                st_ref[2 * slot + h] = st

        def run(j0, steps, stats):
            for d in range(steps):
                put(1 - d % 2, j0 + d + 1)
                stats = softmax_pv(j0 + d, d % 2, stats, False)
            return stats

        init = tuple((jnp.full((1, tq), NEG, F32), jnp.zeros((1, tq), F32), jnp.zeros((HEAD_DIM, tq), F32))
                     for _ in range(2))
        put(0, 0)
        first, left, stats = 0, 2 * i, init
        for size in UNROLLS:
            trips = _shift_div(left, size)
            stats = lax.fori_loop(0, trips, lambda t, st, j0=first, n=size: run(j0 + n * t, n, st), stats)
            first, left = first + size * trips, left - size * trips
        put(1, 2 * i + 1)
        stats = softmax_pv(2 * i, 0, stats, True)
        (ma, la, acca), (mb, lb, accb) = softmax_pv(2 * i + 1, 1, stats, True)
        o_ref[...] = jnp.concatenate([acca / la, accb / lb], axis=0).T.astype(BF16)
        lse_ref[...] = jnp.where(_iota2((2, tq), 0) == 0, ma + jnp.log(la), mb + jnp.log(lb))

    return pl.pallas_call(
        body,
        grid=(N_PAIRS, s // tq),
        in_specs=[
            pl.BlockSpec((tq, LANES), lambda p, i: (i, p)),
            pl.BlockSpec((s, LANES), lambda p, i: (0, N_PAIRS + p)),
            pl.BlockSpec((s, LANES), lambda p, i: (0, 2 * N_PAIRS + p)),
            pl.BlockSpec((None, 2, tq), lambda p, i: (p, 0, i)),
            pl.BlockSpec((None, 2, s), lambda p, i: (p, 0, 0)),
        ],
        out_specs=[
            pl.BlockSpec((tq, LANES), lambda p, i: (i, p)),
            pl.BlockSpec((None, None, 2, tq), lambda p, i: (p, i, 0, 0)),
        ],
        out_shape=[SDS((s, ATTN_W), BF16), SDS((N_PAIRS, s // tq, 2, tq), F32)],
        scratch_shapes=[pltpu.VMEM((2 * nb, tk, LANES), BF16), pltpu.VMEM((nb, LANES, tk), BF16),
                        pltpu.VMEM((4, tk, tq), F32)],
        compiler_params=_cparams("arbitrary", "arbitrary"),
        name="attn_fwd",
    )(qkv, qkv, qkv, c_rows, c_rows)


def _pool_counts(row0, tm, w):
    t = row0 + _iota2((tm, 1), 0)
    return jnp.minimum(t + 1, w).astype(F32)


def _resident(srcs, dsts, sem):
    copies = [pltpu.make_async_copy(src, dst, sem.at[j]) for j, (src, dst) in enumerate(zip(srcs, dsts))]
    first = pl.program_id(0) == 0

    @pl.when(first)
    def _():
        for cp in copies:
            cp.start()

    def ready(j):
        @pl.when(first)
        def _():
            copies[j].wait()

    return ready


def _out_gate_up(attn_o, pool_o, w_out, x, g2, wg_t, wu_t, *, tm):
    s = x.shape[0]

    def body(a_ref, p_ref, wo_hbm, x_ref, g_ref, wg_hbm, wu_hbm, x1_ref, h2_ref, r_ref, gate_ref, up_ref, act_ref,
             wo_ref, wg_ref, wu_ref, sem):
        ready = _resident((wo_hbm, wg_hbm, wu_hbm), (wo_ref, wg_ref, wu_ref), sem)
        ready(0)
        x1 = (x_ref[...] + jnp.dot(a_ref[...], wo_ref[0:ATTN_W, :], preferred_element_type=F32)
              + jnp.dot(p_ref[...], wo_ref[ATTN_W:, :], preferred_element_type=F32))
        r = lax.rsqrt(jnp.mean(x1 * x1, axis=-1, keepdims=True) + EPS)
        x1_ref[...] = x1
        r_ref[...] = r
        h2 = (x1 * r * g_ref[...]).astype(BF16)
        h2_ref[...] = h2
        ready(1)
        gate = lax.dot_general(h2, wg_ref[...], NT, preferred_element_type=F32)
        ready(2)
        up = lax.dot_general(h2, wu_ref[...], NT, preferred_element_type=F32)
        gate_ref[...] = gate.astype(BF16)
        up_ref[...] = up.astype(BF16)
        act_ref[...] = (gate * jax.nn.sigmoid(gate) * up).astype(BF16)

    row = lambda w: pl.BlockSpec((tm, w), lambda i: (i, 0))
    full = lambda a, b: pl.BlockSpec((a, b), lambda i: (0, 0))
    return pl.pallas_call(
        body,
        grid=(s // tm,),
        in_specs=[row(ATTN_W), row(POOL_W), _HBM, row(D_MODEL), full(1, D_MODEL), _HBM, _HBM],
        out_specs=[row(D_MODEL), row(D_MODEL), row(1), row(D_FF), row(D_FF), row(D_FF)],
        out_shape=[SDS((s, D_MODEL), F32), SDS((s, D_MODEL), BF16), SDS((s, 1), F32), SDS((s, D_FF), BF16),
                   SDS((s, D_FF), BF16), SDS((s, D_FF), BF16)],
        scratch_shapes=[pltpu.VMEM(w_out.shape, BF16), pltpu.VMEM(wg_t.shape, BF16), pltpu.VMEM(wu_t.shape, BF16),
                        pltpu.SemaphoreType.DMA((3,))],
        compiler_params=_cparams("arbitrary"),
        name="out_gate_up",
    )(attn_o, pool_o, w_out, x, g2, wg_t, wu_t)


def _staggered(n, start, finish):
    pending = start(0)
    for k in range(n):
        following = start(k + 1) if k + 1 < n else None
        finish(k, pending)
        pending = following


def _down_final(act, wd, x1, gf, tgt, *, tm, sub):
    s = x1.shape[0]

    def body(a_ref, w_ref, x1_ref, g_ref, t_ref, dx2_ref, loss_ref, dgf_ref):
        @pl.when(pl.program_id(0) == 0)
        def _():
            loss_ref[...] = jnp.zeros_like(loss_ref)
            dgf_ref[...] = jnp.zeros_like(dgf_ref)

        g = g_ref[...]

        def matmul(k):
            return jnp.dot(a_ref[k * sub:(k + 1) * sub, :], w_ref[...], preferred_element_type=F32)

        def rest(k, mm):
            rows = slice(k * sub, (k + 1) * sub)
            x2 = x1_ref[rows, :] + mm
            r = lax.rsqrt(jnp.mean(x2 * x2, axis=-1, keepdims=True) + EPS)
            xn = x2 * r
            diff = xn * g - t_ref[rows, :]
            loss_ref[...] += jnp.sum(diff * diff, axis=0, keepdims=True)
            dy = diff * (1.0 / D_MODEL)
            dgf_ref[...] += jnp.sum(dy * xn, axis=0, keepdims=True)
            dxn = dy * g
            dx2_ref[rows, :] = r * (dxn - xn * jnp.mean(dxn * xn, axis=-1, keepdims=True))

        _staggered(tm // sub, matmul, rest)

    row = lambda w: pl.BlockSpec((tm, w), lambda i: (i, 0))
    full = lambda a, b: pl.BlockSpec((a, b), lambda i: (0, 0))
    return pl.pallas_call(
        body,
        grid=(s // tm,),
        in_specs=[row(D_FF), full(D_FF, D_MODEL), row(D_MODEL), full(1, D_MODEL), row(D_MODEL)],
        out_specs=[row(D_MODEL), full(1, D_MODEL), full(1, D_MODEL)],
        out_shape=[SDS((s, D_MODEL), F32), SDS((1, D_MODEL), F32), SDS((1, D_MODEL), F32)],
        compiler_params=_cparams("arbitrary"),
        name="down_final",
    )(act, wd, x1, gf, tgt)


def _swiglu_bwd(dx2, wd, gate, up, *, tm, tn):
    s = dx2.shape[0]

    def body(d_ref, w_ref, gate_ref, up_ref, dgate_ref, dup_ref):
        dact = lax.dot_general(d_ref[...].astype(BF16), w_ref[...], NT, preferred_element_type=F32)
        gate = gate_ref[...].astype(F32)
        sg = jax.nn.sigmoid(gate)
        dup_ref[...] = (dact * (gate * sg)).astype(BF16)
        dgate_ref[...] = (dact * up_ref[...].astype(F32) * (sg * (1.0 + gate * (1.0 - sg)))).astype(BF16)

    ospec = pl.BlockSpec((tm, tn), lambda c, r: (r, c))
    return pl.pallas_call(
        body,
        grid=(D_FF // tn, s // tm),
        in_specs=[pl.BlockSpec((tm, D_MODEL), lambda c, r: (r, 0)), pl.BlockSpec((tn, D_MODEL), lambda c, r: (c, 0)),
                  ospec, ospec],
        out_specs=[ospec, ospec],
        out_shape=[SDS((s, D_FF), BF16), SDS((s, D_FF), BF16)],
        compiler_params=_cparams("arbitrary", "arbitrary"),
        name="swiglu_bwd",
    )(dx2, wd, gate, up)


def _mm_tn_stacked(as_, rows, b, *, ts, name, windows=None):
    s, nb_ = b.shape
    n = len(as_)
    offsets = [sum(rows[:i]) for i in range(n)]
    total = sum(rows)
    if windows is None:
        acc_rows, out_shape = total, (total, nb_)
    else:
        count, step, size = windows
        acc_rows, out_shape = max(total, (count - 1) * step + size), (count, size, nb_)

    def body(*refs):
        a_refs, b_ref, o_ref, acc_ref = refs[:n], refs[n], refs[n + 1], refs[n + 2]
        k = pl.program_id(0)

        @pl.when(k == 0)
        def _():
            acc_ref[...] = jnp.zeros_like(acc_ref)

        bv = b_ref[...].astype(BF16)
        for a_ref, off, cnt in zip(a_refs, offsets, rows):
            part = lax.dot_general(a_ref[...].astype(BF16), bv, TN, preferred_element_type=F32)
            acc_ref[off:off + cnt, :] += part[0:cnt, :]

        @pl.when(k == s // ts - 1)
        def _():
            if windows is None:
                o_ref[...] = acc_ref[...].astype(BF16)
            else:
                for d in range(count):
                    o_ref[d] = acc_ref[d * step:d * step + size, :].astype(BF16)

    return pl.pallas_call(
        body,
        grid=(s // ts,),
        in_specs=[pl.BlockSpec((ts, a.shape[1]), lambda k: (k, 0)) for a in as_] + [pl.BlockSpec((ts, nb_), lambda k: (k, 0))],
        out_specs=pl.BlockSpec(out_shape, lambda k: (0,) * len(out_shape)),
        out_shape=SDS(out_shape, BF16),
        scratch_shapes=[pltpu.VMEM((acc_rows, nb_), F32)],
        compiler_params=_cparams("arbitrary"),
        name=name,
    )(*as_, b)


def _mm_tn_phases(as_, bs, *, ts, name):
    n = len(as_)
    s, r = as_[0].shape
    c = bs[0].shape[1]
    steps = s // ts
    operands, phases, where = [], [], []
    for i, arr in enumerate(list(as_) + list(bs)):
        known = [j for j, o in enumerate(operands) if o is arr]
        if known:
            phases[known[0]].append(i % n)
        else:
            operands.append(arr)
            phases.append([i % n])
        where.append(known[0] if known else len(operands) - 1)
    m = len(operands)

    def body(*refs):
        outs, acc_ref, stage_ref, sem = refs[m:m + n], refs[m + n], refs[m + n + 1], refs[m + n + 2]
        p, k = pl.program_id(0), pl.program_id(1)
        written = lambda i: pltpu.make_async_copy(stage_ref, outs[i], sem.at[i])

        @pl.when(k == 0)
        def _():
            acc_ref[...] = jnp.zeros_like(acc_ref)

        for i in range(n):
            @pl.when(p == i)
            def _(i=i):
                a, b = refs[where[i]][...].astype(BF16), refs[where[n + i]][...].astype(BF16)
                acc_ref[...] += lax.dot_general(a, b, TN, preferred_element_type=F32)

            @pl.when((p == i) & (k == steps - 1))
            def _(i=i):
                if i > 0:
                    written(i - 1).wait()
                stage_ref[...] = acc_ref[...].astype(BF16)
                written(i).start()
                if i == n - 1:
                    written(i).wait()

    def spec(arr, ph):
        lo, hi = min(ph), max(ph)
        return pl.BlockSpec((ts, arr.shape[1]),
                            lambda p, k: (jnp.where(p < lo, 0, jnp.where(p > hi, steps - 1, k)), 0))

    return pl.pallas_call(
        body,
        grid=(n, steps),
        in_specs=[spec(arr, ph) for arr, ph in zip(operands, phases)],
        out_specs=[pl.BlockSpec(memory_space=pl.ANY)] * n,
        out_shape=[SDS((r, c), BF16)] * n,
        scratch_shapes=[pltpu.VMEM((r, c), F32), pltpu.VMEM((r, c), BF16), pltpu.SemaphoreType.DMA((n,))],
        compiler_params=_cparams("arbitrary", "arbitrary"),
        name=name,
    )(*operands)


def _norm_bwd(dh, x, r, g, dres):
    xn = x * r
    dxn = dh * g
    dx = dres + r * (dxn - xn * jnp.mean(dxn * xn, axis=-1, keepdims=True))
    return dx, jnp.sum(dh * xn, axis=0, keepdims=True)


def _mlp_in_pool_bwd(dgate, dup, wg_t, wu_t, w_out, x1, r2, g2, dx2, pooled, w_pool, pool_scale, *, tm):
    s = x1.shape[0]
    nt = s // tm
    ng = len(POOL_WINDOWS)

    def body(dg_ref, dup_ref, wg_hbm, wu_hbm, wo_hbm, x_ref, r_ref, g_ref, d_ref, p_ref, w_ref, sc_ref,
             dx1_ref, dattn_ref, du_ref, dg2_ref, dw_ref, dsc_ref, head_ref, wg_ref, wu_ref, wo_ref, sem):
        i = pl.program_id(0)
        ready = _resident((wg_hbm, wu_hbm, wo_hbm), (wg_ref, wu_ref, wo_ref), sem)

        @pl.when(i == 0)
        def _():
            dg2_ref[...] = jnp.zeros_like(dg2_ref)
            head_ref[...] = jnp.zeros_like(head_ref)
            dw_ref[...] = jnp.zeros_like(dw_ref)
            dsc_ref[...] = jnp.zeros_like(dsc_ref)

        ready(0)
        dh2 = jnp.dot(dg_ref[...], wg_ref[...], preferred_element_type=F32)
        ready(1)
        dh2 = dh2 + jnp.dot(dup_ref[...], wu_ref[...], preferred_element_type=F32)
        dx1, dg2 = _norm_bwd(dh2, x_ref[...], r_ref[...], g_ref[...], d_ref[...])
        dg2_ref[...] += dg2
        dx1_ref[...] = dx1
        ready(2)
        dmix = lax.dot_general(dx1.astype(BF16), wo_ref[...], NT, preferred_element_type=F32)
        dattn_ref[...] = dmix[:, 0:ATTN_W]
        row0 = (nt - 1 - i) * tm
        for g, w in enumerate(POOL_WINDOWS):
            cols = slice(g * POOL_G, (g + 1) * POOL_G)
            wb = w_ref[g].astype(BF16)
            pooled_g = p_ref[:, cols]
            dpo = dmix[:, ATTN_W + g * POOL_G:ATTN_W + (g + 1) * POOL_G]
            mixed = jnp.dot(pooled_g, wb, preferred_element_type=F32)
            dsc_ref[:, cols] += jnp.sum(dpo * mixed, axis=0, keepdims=True)
            dmp = (dpo * sc_ref[:, cols]).astype(BF16)
            dw_ref[g] += lax.dot_general(pooled_g, dmp, TN, preferred_element_type=F32)
            dpooled = lax.dot_general(dmp, wb, NT, preferred_element_type=F32)
            a = dpooled / _pool_counts(row0, tm, w)
            acc = jnp.concatenate([a, head_ref[:, cols]], axis=0)
            head_ref[:, cols] = a[0:HALO, :]
            k = 1
            while k < w:
                acc = acc + pltpu.roll(acc, tm + HALO - k, axis=0)
                k *= 2
            du_ref[:, cols] = (acc[0:tm, :] - dpooled).astype(BF16)

    row = lambda w: pl.BlockSpec((tm, w), lambda i: (nt - 1 - i, 0))
    full = lambda a, b: pl.BlockSpec((a, b), lambda i: (0, 0))
    pool_w = pl.BlockSpec((ng, POOL_G, POOL_G), lambda i: (0, 0, 0))
    return pl.pallas_call(
        body,
        grid=(nt,),
        in_specs=[row(D_FF), row(D_FF), _HBM, _HBM, _HBM,
                  row(D_MODEL), row(1), full(1, D_MODEL), row(D_MODEL), row(POOL_W), pool_w, full(1, POOL_W)],
        out_specs=[row(D_MODEL), row(ATTN_W), row(POOL_W), full(1, D_MODEL), pool_w, full(1, POOL_W)],
        out_shape=[SDS((s, D_MODEL), F32), SDS((s, ATTN_W), F32), SDS((s, POOL_W), BF16), SDS((1, D_MODEL), F32),
                   SDS((ng, POOL_G, POOL_G), F32), SDS((1, POOL_W), F32)],
        scratch_shapes=[pltpu.VMEM((HALO, POOL_W), F32), pltpu.VMEM(wg_t.shape, BF16), pltpu.VMEM(wu_t.shape, BF16),
                        pltpu.VMEM(w_out.shape, BF16), pltpu.SemaphoreType.DMA((3,))],
        compiler_params=_cparams("arbitrary"),
        name="mlp_in_pool_bwd",
    )(dgate, dup, wg_t, wu_t, w_out, x1, r2, g2, dx2, pooled, w_pool, pool_scale)


SUM_ROWS = 16


def _heads_t(t):
    n = t.shape[0]
    lane = _iota2((n, LANES), 1)
    tf = t.astype(F32)
    halves = jnp.concatenate([jnp.where(lane < HEAD_DIM, tf, 0.0).T, jnp.where(lane < HEAD_DIM, 0.0, tf).T], axis=1)
    r, c = _iota2((SUM_ROWS, 2 * n), 0), _iota2((SUM_ROWS, 2 * n), 1)
    ones = jnp.where(((r == 0) & (c < n)) | ((r == 4) & (c >= n)), 1.0, 0.0)
    return jnp.concatenate([halves, ones], axis=0).astype(BF16)


def _attn_bwd(qkv, attn_o, d_attn, rowb, c_rows, after, *, tq):
    s = qkv.shape[0]
    tk = tq
    nb = s // tq
    rows_t = LANES + SUM_ROWS

    def body(q_ref, k_ref, v_ref, o_ref, do_ref, rowb_ref, ck_ref, _, dq_ref, dk_ref, dv_ref, dck_ref, dcq_ref,
             dqt_ref, delta_ref, kp_ref, qp_ref, dob_ref, qt_ref, kt_ref, dot_ref, front_ref):
        lane = _iota2((tq, LANES), 1)
        lo = lane < HEAD_DIM
        first = _iota2((8, LANES), 1) < HEAD_DIM
        sel = jnp.where(_iota2((8, LANES), 0) < 4, jnp.where(first, 1.0, 0.0), jnp.where(first, 0.0, 1.0))

        def prep(b, _):
            st = pl.multiple_of(b * tq, tq)
            do2 = do_ref[pl.ds(st, tq), :]
            delta_ref[b] = _sel_dot(sel, do2 * o_ref[pl.ds(st, tq), :].astype(F32), NT)
            dob_ref[pl.ds(st, tq), :] = do2.astype(BF16)
            dqt_ref[b] = jnp.zeros((rows_t, tq), F32)
            k2 = k_ref[pl.ds(st, tq), :].astype(F32)
            q2 = q_ref[pl.ds(st, tq), :].astype(F32)
            ck = ck_ref[:, pl.ds(st, tq)]
            for h in range(2):
                kp_ref[h * nb + b] = _augment(k2, h, -ck[h:h + 1, :], True)
                qp_ref[h * nb + b] = _augment(q2 * Q_SCALE, h, None, False)
            qt_ref[b] = _heads_t(q2)
            kt_ref[b] = _heads_t(k2)
            dot_ref[b] = _heads_t(do2)[0:LANES, :]
            return 0

        lax.fori_loop(0, nb, prep, 0)

        def split(t):
            z = jnp.zeros_like(t)
            return jnp.where(lo, t, z), jnp.where(lo, z, t)

        def kv_block(j, _):
            st_j = pl.multiple_of(j * tk, tk)
            vs = split(v_ref[pl.ds(st_j, tk), :])
            kt = kt_ref[j]

            def stage(i, slot):
                ic = jnp.minimum(i, nb - 1)
                do2 = dob_ref[pl.ds(pl.multiple_of(ic * tq, tq), tq), :]
                for h in range(2):
                    front_ref[4 * slot + h] = lax.dot_general(kp_ref[h * nb + j], qp_ref[h * nb + ic], NT,
                                                              preferred_element_type=F32)
                    front_ref[4 * slot + 2 + h] = lax.dot_general(vs[h], do2, NT, preferred_element_type=F32)

            def q_block(i, slot, carry, diagonal):
                dkt, dvt = carry
                ic = jnp.minimum(i, nb - 1)
                rb = rowb_ref[ic] + jnp.where(i < nb, 0.0, NEG)
                dl = delta_ref[ic]
                pts, dsts = [], []
                for h in range(2):
                    st = front_ref[4 * slot + h] + rb[h:h + 1, :]
                    if diagonal:
                        st = jnp.where(_iota2((tk, tq), 0) <= _iota2((tk, tq), 1), st, NEG)
                    pt = jnp.exp(st)
                    pts.append(pt.astype(BF16))
                    dsts.append((pt * (front_ref[4 * slot + 2 + h] - dl[4 * h:4 * h + 1, :])).astype(BF16))
                dvt = dvt + lax.dot_general(dot_ref[ic], jnp.concatenate(pts, axis=1), NT, preferred_element_type=F32)
                dkt = dkt + lax.dot_general(qt_ref[ic], jnp.concatenate(dsts, axis=1), NT, preferred_element_type=F32)
                dqt_ref[ic] += jnp.dot(kt, jnp.concatenate(dsts, axis=0), preferred_element_type=F32)
                return dkt, dvt

            def run(i0, steps, carry):
                for d in range(steps):
                    stage(i0 + d + 1, d % 2)
                    carry = q_block(i0 + d, 1 - d % 2, carry, False)
                return carry

            stage(j, 0)
            stage(j + 1, 1)
            carry = q_block(j, 0, (jnp.zeros((rows_t, tk), F32), jnp.zeros((LANES, tk), F32)), True)
            first, left = j + 1, nb - 1 - j
            for size in UNROLLS:
                trips = _shift_div(left + 1 if size == UNROLLS[-1] else left, size)
                carry = lax.fori_loop(0, trips, lambda t, c, i0=first, n=size: run(i0 + n * t, n, c), carry)
                first, left = first + size * trips, left - size * trips
            dkt, dvt = carry
            dk_ref[pl.ds(st_j, tk), :] = (dkt[0:LANES, :].T * Q_SCALE).astype(BF16)
            dv_ref[pl.ds(st_j, tk), :] = dvt.T.astype(BF16)
            dck_ref[j] = dkt[LANES:LANES + 8, :]
            return 0

        lax.fori_loop(0, nb, kv_block, 0)

        def finish(b, _):
            acc = dqt_ref[b]
            dq_ref[pl.ds(pl.multiple_of(b * tq, tq), tq), :] = (acc[0:LANES, :].T * Q_SCALE).astype(BF16)
            dcq_ref[b] = acc[LANES:LANES + 8, :]
            return 0

        lax.fori_loop(0, nb, finish, 0)

    col = lambda off: pl.BlockSpec((s, LANES), lambda p: (0, off + p))
    sums = pl.BlockSpec((None, nb, 8, tq), lambda p: (p, 0, 0, 0))
    return pl.pallas_call(
        body,
        grid=(N_PAIRS,),
        in_specs=[col(0), col(N_PAIRS), col(2 * N_PAIRS), col(0), col(0),
                  pl.BlockSpec((None, nb, 2, tq), lambda p: (p, 0, 0, 0)),
                  pl.BlockSpec((None, 2, s), lambda p: (p, 0, 0)), _UNREAD],
        out_specs=[col(0), col(0), col(0), sums, sums],
        out_shape=[SDS((s, ATTN_W), BF16), SDS((s, ATTN_W), BF16), SDS((s, ATTN_W), BF16),
                   SDS((N_PAIRS, nb, 8, tq), F32), SDS((N_PAIRS, nb, 8, tq), F32)],
        scratch_shapes=[pltpu.VMEM((nb, rows_t, tq), F32), pltpu.VMEM((nb, 8, tq), F32),
                        pltpu.VMEM((2 * nb, tk, LANES), BF16), pltpu.VMEM((2 * nb, tq, LANES), BF16),
                        pltpu.VMEM((s, LANES), BF16), pltpu.VMEM((nb, rows_t, 2 * tq), BF16),
                        pltpu.VMEM((nb, rows_t, 2 * tk), BF16), pltpu.VMEM((nb, LANES, 2 * tq), BF16),
                        pltpu.VMEM((8, tk, tq), F32)],
        compiler_params=_cparams("arbitrary"),
        name="attn_bwd",
    )(qkv, qkv, qkv, attn_o, d_attn, rowb, c_rows, after)


def _forget_bwd(dc_t, fl_t, b_rows):
    rows = fl_t.shape[0]
    nb = rows // N_HEADS

    def body(dc_ref, fl_ref, b_ref, dfl_ref, db_ref):
        dc = dc_ref[...]
        lower = _iota2((LANES, LANES), 0) >= _iota2((LANES, LANES), 1)
        ones = jnp.ones((LANES, LANES), F32)
        rr, cc, same = _head_block_masks(rows, nb)
        dlf = _dot_sel(dc, lower) + _sel_dot(same & (cc > rr), _dot_sel(dc, ones))
        dfl = dlf / (1.0 + jnp.exp(fl_ref[...] + b_ref[...]))
        dfl_ref[...] = dfl
        shift = nb.bit_length() - 1
        hsel = lax.shift_right_logical(_iota2((N_HEADS, rows), 1), shift) == _iota2((N_HEADS, rows), 0)
        db_ref[...] = _sel_dot(hsel, _dot_sel(dfl, ones))

    return pl.pallas_call(body, out_shape=[SDS(fl_t.shape, F32), SDS((N_HEADS, LANES), F32)],
                          compiler_params=_cparams(), name="forget_bwd")(dc_t, fl_t, b_rows)


def _in_bwd(dq, dk, dv, du, dfl, w_in_t, x, r1, g1, dx1, after, *, tm):
    s = x.shape[0]
    pieces = ((0, ATTN_W), (ATTN_W, 2 * ATTN_W), (2 * ATTN_W, QKV_W), (U_OFF, F_OFF), (F_OFF, IN_PAD))

    def body(dq_ref, dk_ref, dv_ref, du_ref, df_ref, w_ref, x_ref, r_ref, g_ref, d_ref, _, dx_ref, dg1_ref):
        @pl.when(pl.program_id(0) == 0)
        def _():
            dg1_ref[...] = jnp.zeros_like(dg1_ref)

        dh = None
        for ref, (c0, c1) in zip((dq_ref, dk_ref, dv_ref, du_ref, df_ref), pieces):
            t = jnp.dot(ref[...], w_ref[c0:c1, :], preferred_element_type=F32)
            dh = t if dh is None else dh + t
        dx, dg1 = _norm_bwd(dh, x_ref[...], r_ref[...], g_ref[...], d_ref[...])
        dx_ref[...] = dx
        dg1_ref[...] += dg1

    row = lambda w: pl.BlockSpec((tm, w), lambda i: (i, 0))
    full = lambda a, b: pl.BlockSpec((a, b), lambda i: (0, 0))
    return pl.pallas_call(
        body,
        grid=(s // tm,),
        in_specs=[row(ATTN_W), row(ATTN_W), row(ATTN_W), row(POOL_W), row(LANES), full(IN_PAD, D_MODEL),
                  row(D_MODEL), row(1), full(1, D_MODEL), row(D_MODEL), _UNREAD],
        out_specs=[row(D_MODEL), full(1, D_MODEL)],
        out_shape=[SDS((s, D_MODEL), F32), SDS((1, D_MODEL), F32)],
        compiler_params=_cparams("arbitrary"),
        name="in_bwd",
    )(dq, dk, dv, du, dfl, w_in_t, x, r1, g1, dx1, after)


def _tiles(s):
    big = min(512, s)
    return dict(row=big, attn=min(256, s // 2), ff_rows=min(256, s), tall=min(1024, s))


def _local_step(x, tgt, p, weight, emit, started):
    s = x.shape[0]
    t = _tiles(s)
    tm, tq = t["row"], t["attn"]
    nb = s // LANES
    nqb = s // tq
    g1, g2, gf = p["norm1_g"], p["norm2_g"], p["final_g"].reshape(1, D_MODEL)
    w_pool, pool_scale = p["w_pool"][0], p["pool_scale"]

    h, r1 = _norm1(x, g1, started, tm=tm)
    w_in_t = weight("w_in", h)
    qkv, fl, pooled, pool_o = _in_proj_pool(h, w_in_t, w_pool, pool_scale, tm=t["tall"])
    fl_t = fl[:, :N_HEADS].T.reshape(N_HEADS * nb, LANES)
    b_rows = jnp.repeat(p["b_forget"].reshape(N_HEADS), nb).reshape(N_HEADS * nb, 1)
    c = _forget_cumsum(fl_t, b_rows).reshape(N_PAIRS, 2, s)
    c_rowblk = c.reshape(N_PAIRS, 2, nqb, tq).transpose(0, 2, 1, 3)
    attn_o, lse = _attn_fwd(qkv, c, tk=tq)
    lse = lse.reshape(N_PAIRS, nqb // 2, 2, 2, tq).transpose(0, 1, 3, 2, 4).reshape(N_PAIRS, nqb, 2, tq)
    w_out = weight("w_out", attn_o)
    wg_t, wu_t = weight("w_gate_up", attn_o)
    x1, h2, r2, gate, up, act = _out_gate_up(attn_o, pool_o, w_out, x, g2, wg_t, wu_t, tm=t["ff_rows"])
    wd = weight("w_down", act)
    dx2, loss_row, d_gf = _down_final(act, wd, x1, gf, tgt, tm=tm, sub=min(128, tm))

    dgate, dup = _swiglu_bwd(dx2, wd, gate, up, tm=t["ff_rows"], tn=D_FF)
    d_wd, d_wg_t, d_wu_t = _mm_tn_phases([act, dgate, dup], [dx2, h2, h2], ts=t["row"], name="grad_w_ff")
    dx1, d_attn, du, d_g2, d_wpool, d_pscale = _mlp_in_pool_bwd(dgate, dup, wg_t, wu_t, w_out, x1, r2, g2, dx2, pooled,
                                                               w_pool, pool_scale, tm=t["ff_rows"])
    d_wo = _mm_tn_stacked([attn_o, pool_o], [ATTN_W, POOL_W], dx1, ts=t["tall"], name="grad_w_out")
    token = emit(("w_down", "w_gate", "w_up", "w_out"), (d_wd, d_wg_t, d_wu_t, d_wo))
    dq, dk, dv, dck, dcq = _attn_bwd(qkv, attn_o, d_attn, c_rowblk - lse, c, token, tq=tq)
    dc_t = (dcq - dck)[:, :, 0::4, :].transpose(0, 2, 1, 3).reshape(N_HEADS * nb, LANES)
    dfl_t, db = _forget_bwd(dc_t, fl_t, b_rows)
    dfl = jnp.pad(dfl_t.reshape(N_HEADS, s).T, ((0, 0), (0, LANES - N_HEADS))).astype(BF16)
    d_w_in_t = _mm_tn_stacked([dq, dk, dv, dfl, du], [ATTN_W, ATTN_W, ATTN_W, N_HEADS, POOL_W], h, ts=t["tall"],
                              name="grad_w_in",
                              windows=(N_DEV, IN_STEP, IN_WINDOW))
    token = emit(("w_in",), (d_w_in_t,))
    dx, d_g1 = _in_bwd(dq, dk, dv, du, dfl, w_in_t, x, r1, g1, dx1, token, tm=tm)

    small = dict(norm1_g=d_g1, b_forget=db[:, 0].reshape(1, N_HEADS), w_pool=d_wpool, pool_scale=d_pscale,
                 norm2_g=d_g2, final_g=d_gf)
    return loss_row, dx, small


def _my_index():
    return 4 * lax.axis_index("x") + 2 * lax.axis_index("y") + lax.axis_index("c")


def _peer(k):
    pos = [lax.axis_index(a) for a in ("x", "y", "c")]
    flipped = tuple(1 - p if (k >> b) & 1 else p for p, b in zip(pos, (2, 1, 0)))
    return flipped, 4 * flipped[0] + 2 * flipped[1] + flipped[2]


_HBM = pl.BlockSpec(memory_space=pltpu.HBM)
_SEM = pl.BlockSpec(memory_space=pltpu.SEMAPHORE)
_DATAFLOW = pltpu.SideEffectType.DATAFLOW_SIDE_EFFECTING


ALL_PEERS = tuple(range(1, N_DEV))
SAME_CORE = (1, 2, 4, 6)


def _peer_copies(ins, lands, send_sems, recv_sems, scatter, peers, arrivals):
    me = _my_index()
    copies = []
    for w in range(len(ins)):
        for k in peers[w]:
            dev, idx = _peer(k)
            copies.append(pltpu.make_async_remote_copy(
                src_ref=ins[w].at[idx] if scatter[w] else ins[w], dst_ref=lands[w].at[idx if arrivals else me],
                send_sem=send_sems[w].at[k - 1], recv_sem=recv_sems[w].at[k - 1], device_id=dev, device_id_type=MESH))
    return copies


def _own_copies(ins, lands, send_sems, scatter):
    me = _my_index()
    return [pltpu.make_async_copy(ins[w].at[me] if scatter[w] else ins[w], lands[w].at[me], send_sems[w].at[N_DEV - 1])
            for w in range(len(ins))]


def _forward_copies(land, send_sems, recv_sems, arrivals):
    sibling, _ = _peer(1)
    copies = []
    for j, k in enumerate(SAME_CORE[1:]):
        src, dst = _peer(k)[1], _peer(k ^ 1 if arrivals else k)[1]
        copies.append(pltpu.make_async_remote_copy(
            src_ref=land.at[src], dst_ref=land.at[dst], send_sem=send_sems.at[j], recv_sem=recv_sems.at[j],
            device_id=sibling, device_id_type=MESH))
    return copies


def _forward_start(land, name):
    def body(land_ref, send_sems, recv_sems, land_thru, token):
        for cp in _forward_copies(land_ref, send_sems, recv_sems, False):
            cp.start()
        token[...] = jnp.zeros_like(token)

    sem = pltpu.SemaphoreType.DMA((len(SAME_CORE) - 1,))
    send, recv, thru, _ = pl.pallas_call(
        body,
        in_specs=[_HBM],
        out_specs=[_SEM, _SEM, _HBM, pl.BlockSpec(memory_space=pltpu.VMEM)],
        out_shape=[sem, sem, pltpu.HBM(land.shape, land.dtype), SDS((8, LANES), F32)],
        input_output_aliases={0: 2},
        compiler_params=pltpu.CompilerParams(has_side_effects=_DATAFLOW),
        name=name,
    )(land)
    return send, recv, thru


def _forward_wait(handle, after, name):
    def body(land_ref, send_sems, recv_sems, after_ref, land_out):
        for cp in _forward_copies(land_ref, send_sems, recv_sems, False):
            cp.wait_send()
        for cp in _forward_copies(land_ref, send_sems, recv_sems, True):
            cp.wait_recv()

    send, recv, land = handle
    return pl.pallas_call(
        body,
        in_specs=[_HBM, _SEM, _SEM, pl.BlockSpec(memory_space=pl.ANY)],
        out_specs=_HBM,
        out_shape=pltpu.HBM(land.shape, land.dtype),
        input_output_aliases={0: 0},
        compiler_params=pltpu.CompilerParams(has_side_effects=_DATAFLOW),
        name=name,
    )(land, send, recv, after)


def _exchange_start(arrays, scatter, name, peers=None, after=None):
    n = len(arrays)
    peers = peers or [ALL_PEERS] * n
    order = [] if after is None else [after]
    land_shapes = [(N_DEV,) + tuple(a.shape[1:] if sc else a.shape) for a, sc in zip(arrays, scatter)]

    def body(*refs):
        ins, lands = refs[:n], refs[n:2 * n]
        outs = refs[2 * n + len(order):]
        send_sems, recv_sems, token = outs[:n], outs[n:2 * n], outs[4 * n]
        for cp in _peer_copies(ins, lands, send_sems, recv_sems, scatter, peers, False):
            cp.start()
        for cp in _own_copies(ins, lands, send_sems, scatter):
            cp.start()
        token[...] = jnp.zeros_like(token)

    sends, recvs = pltpu.SemaphoreType.DMA((N_DEV,)), pltpu.SemaphoreType.DMA((N_DEV - 1,))
    outs = pl.pallas_call(
        body,
        in_specs=[_HBM] * (2 * n) + [_UNREAD] * len(order),
        out_specs=[_SEM] * (2 * n) + [_HBM] * (2 * n) + [pl.BlockSpec(memory_space=pltpu.VMEM)],
        out_shape=[sends] * n + [recvs] * n + [pltpu.HBM(a.shape, a.dtype) for a in arrays]
        + [pltpu.HBM(sh, a.dtype) for sh, a in zip(land_shapes, arrays)] + [SDS((8, LANES), F32)],
        input_output_aliases={i: 2 * n + i for i in range(2 * n)},
        compiler_params=pltpu.CompilerParams(has_side_effects=_DATAFLOW),
        name=name,
    )(*[pltpu.with_memory_space_constraint(a, pltpu.HBM) for a in arrays],
      *[pltpu.with_memory_space_constraint(lax.empty(sh, a.dtype), pltpu.HBM) for sh, a in zip(land_shapes, arrays)],
      *order)
    handles = [dict(send=outs[w], recv=outs[n + w], src=outs[2 * n + w], land=outs[3 * n + w], scatter=scatter[w],
                    peers=peers[w]) for w in range(n)]
    return handles, outs[4 * n]


def _exchange_wait(handles, after, name):
    n = len(handles)
    scatter, peers = [h["scatter"] for h in handles], [h["peers"] for h in handles]

    def body(*refs):
        ins, lands = refs[:n], refs[n:2 * n]
        send_sems, recv_sems = refs[2 * n:3 * n], refs[3 * n:4 * n]
        for cp in _peer_copies(ins, lands, send_sems, recv_sems, scatter, peers, False):
            cp.wait_send()
        for cp in _peer_copies(ins, lands, send_sems, recv_sems, scatter, peers, True):
            cp.wait_recv()
        for cp in _own_copies(ins, lands, send_sems, scatter):
            cp.wait()

    srcs, lands = [h["src"] for h in handles], [h["land"] for h in handles]
    outs = pl.pallas_call(
        body,
        in_specs=[_HBM] * (2 * n) + [_SEM] * (2 * n) + [pl.BlockSpec(memory_space=pl.ANY)],
        out_specs=[_HBM] * (2 * n),
        out_shape=[pltpu.HBM(a.shape, a.dtype) for a in srcs + lands],
        input_output_aliases={i: i for i in range(2 * n)},
        compiler_params=pltpu.CompilerParams(has_side_effects=_DATAFLOW),
        name=name,
    )(*srcs, *lands, *[h["send"] for h in handles], *[h["recv"] for h in handles], after)
    return outs[n:]


def _adamw(parts, w, m, v, name):
    rows, cols = w.shape
    tr = rows // 4 if rows % 32 == 0 else rows

    def body(p_ref, w_ref, m_ref, v_ref, g_ref, d_ref, mo_ref, vo_ref):
        g = p_ref[0].astype(F32)
        for d in range(1, N_DEV):
            g = g + p_ref[d].astype(F32)
        g_ref[...] = g
        d_ref[...], mo_ref[...], vo_ref[...] = _adam_update(g, w_ref[...], m_ref[...], v_ref[...])

    blk = pl.BlockSpec((tr, cols), lambda i: (i, 0))
    return pl.pallas_call(
        body,
        grid=(rows // tr,),
        in_specs=[pl.BlockSpec((N_DEV, tr, cols), lambda i: (0, i, 0)), blk, blk, blk],
        out_specs=[blk] * 4,
        out_shape=[SDS((rows, cols), F32)] * 4,
        compiler_params=_cparams("arbitrary"),
        name=name,
    )(parts, w, m, v)


def _adamw_dense(parts, w, m, v, name, *, rows, shift):
    _, window, cols = parts.shape
    per_row = cols // LANES

    def body(p_ref, w_ref, m_ref, v_ref, g_ref, d_ref, mo_ref, vo_ref, sum_ref):
        g = p_ref[0].astype(F32)
        for d in range(1, N_DEV):
            g = g + p_ref[d].astype(F32)
        sum_ref[...] = g
        me = _my_index()
        for j in range(N_DEV):
            @pl.when(me == j)
            def _(j=j):
                for c in range(per_row):
                    at = (pl.ds(c, rows, stride=per_row), slice(None))
                    gc = sum_ref[j * shift:j * shift + rows, c * LANES:(c + 1) * LANES]
                    g_ref[at] = gc
                    d_ref[at], mo_ref[at], vo_ref[at] = _adam_update(gc, w_ref[at], m_ref[at], v_ref[at])

    return pl.pallas_call(
        body,
        out_shape=[SDS(w.shape, F32)] * 4,
        scratch_shapes=[pltpu.VMEM((window, cols), F32)],
        compiler_params=_cparams(),
        name=name,
    )(parts, w, m, v)


_ROW_OF = dict(norm1_g=(0, D_MODEL), norm2_g=(1, D_MODEL), final_g=(2, D_MODEL), pool_scale=(3, POOL_W),
               b_forget=(4, N_HEADS), loss=(5, 1))


def _pack_rows(vals):
    rows = [jnp.pad(vals[n].reshape(1, width).astype(F32), ((0, 0), (0, D_MODEL - width)))
            for n, (_, width) in sorted(_ROW_OF.items(), key=lambda kv: kv[1][0])]
    return jnp.concatenate(rows + [jnp.zeros((8 - len(rows), D_MODEL), F32)], axis=0)


def _adam_update(g, w, m, v):
    m_new = ADAM_B1 * m + (1.0 - ADAM_B1) * g
    v_new = ADAM_B2 * v + (1.0 - ADAM_B2) * (g * g)
    m_hat = m_new / (1.0 - ADAM_B1 ** ADAM_STEP)
    v_hat = v_new / (1.0 - ADAM_B2 ** ADAM_STEP)
    return -ADAM_LR * (m_hat / (jnp.sqrt(v_hat) + ADAM_EPS) + ADAM_WD * w), m_new, v_new


def _adamw_replicated(parts_rows, parts_pool, w, m, v):
    names = ("norm1_g", "norm2_g", "final_g", "pool_scale", "b_forget", "w_pool")
    shapes = {n: ((len(POOL_WINDOWS), POOL_G, POOL_G) if n == "w_pool" else (1, _ROW_OF[n][1])) for n in names}

    def body(rows_ref, pool_ref, *refs):
        ins, outs = refs[:3 * len(names)], refs[3 * len(names):]

        def total(n):
            if n == "w_pool":
                pieces = [pool_ref[d] for d in range(N_DEV)]
            else:
                row, width = _ROW_OF[n]
                pieces = [rows_ref[d, row:row + 1, 0:width] for d in range(N_DEV)]
            g = pieces[0]
            for p in pieces[1:]:
                g = g + p
            return g

        outs[0][...] = total("loss")
        for k, n in enumerate(names):
            g = total(n)
            delta, m_new, v_new = _adam_update(g, ins[3 * k][...], ins[3 * k + 1][...], ins[3 * k + 2][...])
            for o_ref, val in zip(outs[1 + 4 * k:5 + 4 * k], (g, delta, m_new, v_new)):
                o_ref[...] = val

    args = [d[n].reshape(shapes[n]) for n in names for d in (w, m, v)]
    res = pl.pallas_call(
        body,
        out_shape=[SDS((1, 1), F32)] + [SDS(shapes[n], F32) for n in names for _ in range(4)],
        compiler_params=_cparams(),
        name="adamw_replicated",
    )(parts_rows, parts_pool, *args)
    return res[0], {n: [r.reshape(w[n].shape) for r in res[1 + 4 * k:5 + 4 * k]] for k, n in enumerate(names)}


def kernel(x, norm1_g, w_in, b_forget, w_pool, pool_scale, w_out, norm2_g, w_gate, w_up, w_down, final_g, loss_target, m_norm1_g, m_w_in, m_b_forget, m_w_pool, m_pool_scale, m_w_out, m_norm2_g, m_w_gate, m_w_up, m_w_down, m_final_g, v_norm1_g, v_w_in, v_b_forget, v_w_pool, v_pool_scale, v_w_out, v_norm2_g, v_w_gate, v_w_up, v_w_down, v_final_g):
    big = ("w_in", "w_out", "w_gate", "w_up", "w_down")
    order = ("norm1_g", "w_in", "b_forget", "w_pool", "pool_scale", "w_out", "norm2_g", "w_gate", "w_up", "w_down",
             "final_g")
    w = dict(norm1_g=norm1_g, w_in=w_in, b_forget=b_forget, w_pool=w_pool, pool_scale=pool_scale, w_out=w_out,
             norm2_g=norm2_g, w_gate=w_gate, w_up=w_up, w_down=w_down, final_g=final_g)
    m = dict(norm1_g=m_norm1_g, w_in=m_w_in, b_forget=m_b_forget, w_pool=m_w_pool, pool_scale=m_pool_scale,
             w_out=m_w_out, norm2_g=m_norm2_g, w_gate=m_w_gate, w_up=m_w_up, w_down=m_w_down, final_g=m_final_g)
    v = dict(norm1_g=v_norm1_g, w_in=v_w_in, b_forget=v_b_forget, w_pool=v_w_pool, pool_scale=v_pool_scale,
             w_out=v_w_out, norm2_g=v_norm2_g, w_gate=v_w_gate, w_up=v_w_up, w_down=v_w_down, final_g=v_final_g)

    flipped = ("w_in", "w_gate", "w_up")
    shard = lambda d, n: d[n][0].T if n in flipped else d[n][0]
    cast = lambda n: shard(w, n).astype(BF16)
    (first,), started = _exchange_start([cast("w_in")], [False], "gather_start_w_in", peers=[SAME_CORE])
    gather = dict(w_in=first)

    def gathered(names, after):
        return _exchange_wait([gather[n] for n in names], after, "gather_wait_" + names[0])

    def weight(name, after):
        if name == "w_in":
            handles, token = _exchange_start([cast(n) for n in big[1:]], [False] * len(big[1:]), "gather_start", after=after)
            gather.update(zip(big[1:], handles))
            forward = _forward_start(gathered(["w_in"], token)[0], "gather_forward_start")
            full = _forward_wait(forward, after, "gather_forward_wait").reshape(IN_W, D_MODEL)
            f0 = QKV_W + N_HEADS
            return jnp.concatenate([full[:QKV_W], full[f0:], full[QKV_W:f0],
                                    jnp.zeros((IN_PAD - IN_W, D_MODEL), BF16)], axis=0)
        if name == "w_out":
            return gathered(["w_out"], after)[0].reshape(D_MODEL, D_MODEL)
        if name == "w_gate_up":
            return [g.reshape(D_FF, D_MODEL) for g in gathered(["w_gate", "w_up"], after)]
        return gathered(["w_down"], after)[0].reshape(D_FF, D_MODEL)

    rows = lambda g: g if g.ndim == 3 else g.reshape(N_DEV, g.shape[0] // N_DEV, g.shape[1])
    sent = {}

    def emit(names, grads):
        handles, token = _exchange_start([rows(g) for g in grads], [True] * len(names), "grads_start_" + names[0])
        sent.update(zip(names, handles))
        return token

    loss_row, dx, small_grads = _local_step(x[0], loss_target[0], w, weight, emit, started)

    packed = _pack_rows(dict(small_grads, loss=0.5 / D_MODEL * jnp.sum(loss_row)))
    small_handles, after = _exchange_start([packed, small_grads["w_pool"]], [False, False], "grads_start_replicated")

    outs = {}
    for name in ("w_down", "w_gate", "w_up", "w_out", "w_in"):
        (parts,) = _exchange_wait([sent[name]], after, "grads_wait_" + name)
        if name == "w_in":
            dense = lambda d: d[name].transpose(2, 0, 1).reshape(-1, LANES)
            outs[name] = _adamw_dense(parts, dense(w), dense(m), dense(v), "adamw_" + name, rows=IN_SHARD, shift=IN_SHIFT)
            after = outs[name][0]
            outs[name] = [a.reshape(-1, D_MODEL // LANES, LANES).transpose(1, 2, 0).reshape(1, D_MODEL, -1)
                          for a in outs[name]]
            continue
        outs[name] = _adamw(parts, shard(w, name), shard(m, name), shard(v, name), "adamw_" + name)
        after = outs[name][0]
        outs[name] = [(a.T if name in flipped else a)[None] for a in outs[name]]
    parts_rows, parts_pool = _exchange_wait(small_handles, after, "grads_wait_replicated")
    loss, small = _adamw_replicated(parts_rows, parts_pool, w, m, v)
    outs.update(small)

    return (loss.reshape(()), dx[None]) + tuple(outs[n][k] for k in range(4) for n in order)
```

```python
import jax
import jax.numpy as jnp
from jax import lax
from jax.experimental import pallas as pl
from jax.experimental.pallas import tpu as pltpu

F32 = jnp.float32
BF16 = jnp.bfloat16
SDS = jax.ShapeDtypeStruct

D_MODEL = 1024
ATTN_W = 512
N_HEADS = 8
HEAD_DIM = 64
Q_SCALE = HEAD_DIM ** -0.5
N_PAIRS = N_HEADS // 2
POOL_W = 512
POOL_WINDOWS = (2, 4, 8, 16)
POOL_G = 128
HALO = 16
IN_W = 3 * ATTN_W + N_HEADS + POOL_W
QKV_W = 3 * ATTN_W
U_OFF = QKV_W
F_OFF = QKV_W + POOL_W
IN_PAD = F_OFF + 128
D_FF = 2816
EPS = 1e-6
NEG = -1e30
N_DEV = 8
LANES = 128
BF16_ROWS = 16

IN_SHARD = IN_W // N_DEV
IN_STEP = IN_SHARD // BF16_ROWS * BF16_ROWS
IN_SHIFT = IN_SHARD - IN_STEP
IN_WINDOW = -(-((N_DEV - 1) * IN_SHIFT + IN_SHARD) // BF16_ROWS) * BF16_ROWS

ADAM_LR = 0.001
ADAM_B1 = 0.9
ADAM_B2 = 0.999
ADAM_EPS = 1e-08
ADAM_WD = 0.01
ADAM_STEP = 10

VMEM_LIMIT_BYTES = 56 * 1024 * 1024
MESH = pl.DeviceIdType.MESH
NT = (((1,), (1,)), ((), ()))
TN = (((0,), (0,)), ((), ()))


_UNREAD = pl.BlockSpec(memory_space=pl.ANY)


def _cparams(*sem):
    return pltpu.CompilerParams(dimension_semantics=sem or None, vmem_limit_bytes=VMEM_LIMIT_BYTES)


def _split3(a):
    hi = a.astype(BF16)
    r1 = a - hi.astype(F32)
    mid = r1.astype(BF16)
    lo = (r1 - mid.astype(F32)).astype(BF16)
    return hi, mid, lo


def _dot_sel(a, sel, dims=None):
    sb = sel.astype(BF16)
    if dims is None:
        return sum(jnp.dot(p, sb, preferred_element_type=F32) for p in _split3(a))
    return sum(lax.dot_general(p, sb, dims, preferred_element_type=F32) for p in _split3(a))


def _sel_dot(sel, a, dims=None):
    sb = sel.astype(BF16)
    if dims is None:
        return sum(jnp.dot(sb, p, preferred_element_type=F32) for p in _split3(a))
    return sum(lax.dot_general(sb, p, dims, preferred_element_type=F32) for p in _split3(a))


def _iota2(shape, dim):
    return lax.broadcasted_iota(jnp.int32, shape, dim)


UNROLLS = (8, 4, 2)


def _shift_div(x, n):
    return lax.shift_right_logical(x, n.bit_length() - 1)


def _norm1(x, g1, after, *, tm):
    s = x.shape[0]

    def body(x_ref, g_ref, _, h_ref, r_ref):
        xv = x_ref[...]
        r = lax.rsqrt(jnp.mean(xv * xv, axis=-1, keepdims=True) + EPS)
        h_ref[...] = (xv * r * g_ref[...]).astype(BF16)
        r_ref[...] = r

    row = lambda w: pl.BlockSpec((tm, w), lambda i: (i, 0))
    return pl.pallas_call(
        body,
        grid=(s // tm,),
        in_specs=[row(D_MODEL), pl.BlockSpec((1, D_MODEL), lambda i: (0, 0)), _UNREAD],
        out_specs=[row(D_MODEL), row(1)],
        out_shape=[SDS((s, D_MODEL), BF16), SDS((s, 1), F32)],
        compiler_params=_cparams("arbitrary"),
        name="norm1",
    )(x, g1, after)


def _in_proj_pool(h, w_in_t, w_pool, pool_scale, *, tm):
    s = h.shape[0]

    def body(h_ref, w_ref, wp_ref, sc_ref, qkv_ref, fl_ref, pooled_ref, po_ref, tail_ref):
        i = pl.program_id(0)

        @pl.when(i == 0)
        def _():
            tail_ref[...] = jnp.zeros_like(tail_ref)

        hv = h_ref[...]
        uv = lax.dot_general(hv, w_ref[U_OFF:F_OFF, :], NT, preferred_element_type=F32)
        qkv_ref[...] = lax.dot_general(hv, w_ref[0:QKV_W, :], NT, preferred_element_type=F32).astype(BF16)
        fl_ref[...] = lax.dot_general(hv, w_ref[F_OFF:IN_PAD, :], NT, preferred_element_type=F32)
        ext = jnp.concatenate([tail_ref[...], uv], axis=0)
        tail_ref[...] = uv[tm - HALO:, :]
        for g, w in enumerate(POOL_WINDOWS):
            cols = slice(g * POOL_G, (g + 1) * POOL_G)
            acc = ext[:, cols]
            k = 1
            while k < w:
                acc = acc + pltpu.roll(acc, k, axis=0)
                k *= 2
            pooled = (acc[HALO:, :] / _pool_counts(i * tm, tm, w) - uv[:, cols]).astype(BF16)
            pooled_ref[:, cols] = pooled
            mixed = jnp.dot(pooled, wp_ref[g].astype(BF16), preferred_element_type=F32)
            po_ref[:, cols] = (mixed * sc_ref[:, cols]).astype(BF16)

    row = lambda w: pl.BlockSpec((tm, w), lambda i: (i, 0))
    return pl.pallas_call(
        body,
        grid=(s // tm,),
        in_specs=[row(D_MODEL), pl.BlockSpec((IN_PAD, D_MODEL), lambda i: (0, 0)),
                  pl.BlockSpec((len(POOL_WINDOWS), POOL_G, POOL_G), lambda i: (0, 0, 0)),
                  pl.BlockSpec((1, POOL_W), lambda i: (0, 0))],
        out_specs=[row(QKV_W), row(LANES), row(POOL_W), row(POOL_W)],
        out_shape=[SDS((s, QKV_W), BF16), SDS((s, LANES), F32), SDS((s, POOL_W), BF16), SDS((s, POOL_W), BF16)],
        scratch_shapes=[pltpu.VMEM((HALO, POOL_W), F32)],
        compiler_params=_cparams("arbitrary"),
        name="in_proj_pool",
    )(h, w_in_t, w_pool, pool_scale)


def _head_block_masks(rows, nb):
    shift = nb.bit_length() - 1
    rr, cc = _iota2((rows, rows), 0), _iota2((rows, rows), 1)
    same = lax.shift_right_logical(rr, shift) == lax.shift_right_logical(cc, shift)
    return rr, cc, same


def _forget_cumsum(fl_t, b_rows):
    rows = fl_t.shape[0]
    nb = rows // N_HEADS

    def body(fl_ref, b_ref, c_ref):
        z = fl_ref[...] + b_ref[...]
        lf = jnp.minimum(z, 0.0) - jnp.log1p(jnp.exp(-jnp.abs(z)))
        upper = _iota2((LANES, LANES), 0) <= _iota2((LANES, LANES), 1)
        within = _dot_sel(lf, upper)
        tot = _dot_sel(lf, jnp.ones((LANES, LANES), F32))
        rr, cc, same = _head_block_masks(rows, nb)
        c_ref[...] = within + _sel_dot(same & (cc < rr), tot)

    return pl.pallas_call(body, out_shape=SDS(fl_t.shape, F32), compiler_params=_cparams(), name="forget_cumsum")(
        fl_t, b_rows)


BIAS_LANES = 3


def _augment(t, h, bias, col_first):
    n = t.shape[0]
    lane = _iota2((n, LANES), 1)
    own = (lane < HEAD_DIM) if h == 0 else (lane >= HEAD_DIM)
    b0 = HEAD_DIM if h == 0 else 0
    c0, o0 = (b0, b0 + BIAS_LANES) if col_first else (b0 + BIAS_LANES, b0)
    x = 0.0
    if bias is not None:
        row = _iota2((BF16_ROWS, n), 0)
        pieces = jnp.zeros((BF16_ROWS, n), F32)
        for off, piece in enumerate(_split3(bias)):
            pieces = jnp.where(row == off, piece.astype(F32), pieces)
        r, ln = _iota2((BF16_ROWS, LANES), 0), _iota2((BF16_ROWS, LANES), 1)
        place = jnp.where(r < BIAS_LANES, jnp.where(ln == c0 + r, 1.0, 0.0), 0.0).astype(BF16)
        x = lax.dot_general(pieces.astype(BF16), place, TN, preferred_element_type=F32)
    x = jnp.where(own, t, x)
    x = jnp.where((lane >= o0) & (lane < o0 + BIAS_LANES), 1.0, x)
    return x.astype(BF16)


def _attn_fwd(qkv, c_rows, *, tk):
    s = qkv.shape[0]
    tq = 2 * tk
    nb = s // tk

    def body(q_ref, k_ref, v_ref, cq_ref, ck_ref, o_ref, lse_ref, kp_ref, vt_ref, st_ref):
        i = pl.program_id(1)

        @pl.when(i == 0)
        def _():
            def prep(jb, _):
                st = pl.multiple_of(jb * tk, tk)
                k2 = k_ref[pl.ds(st, tk), :].astype(F32)
                ck = ck_ref[:, pl.ds(st, tk)]
                for h in range(2):
                    kp_ref[h * nb + jb] = _augment(k2, h, -ck[h:h + 1, :], True)
                vt_ref[jb] = v_ref[pl.ds(st, tk), :].astype(F32).T.astype(BF16)
                return 0

            lax.fori_loop(0, nb, prep, 0)

        qs = q_ref[...].astype(F32) * Q_SCALE
        cq = cq_ref[...]
        qp = [_augment(qs, h, cq[h:h + 1, :], False) for h in range(2)]

        def logits(j):
            return tuple(lax.dot_general(kp_ref[h * nb + j], qp[h], NT, preferred_element_type=F32) for h in range(2))

        def softmax_pv(j, slot, stats, masked):
            out = []
            for h in range(2):
                m, l, acc = stats[h]
                st = st_ref[2 * slot + h]
                if masked:
                    st = jnp.where(j * tk + _iota2((tk, tq), 0) <= i * tq + _iota2((tk, tq), 1), st, NEG)
                m_new = jnp.maximum(m, jnp.max(st, axis=0, keepdims=True))
                alpha = jnp.exp(m - m_new)
                p = jnp.exp(st - m_new)
                l = alpha * l + jnp.sum(p, axis=0, keepdims=True)
                vt = vt_ref[j, h * HEAD_DIM:(h + 1) * HEAD_DIM, :]
                acc = alpha * acc + jnp.dot(vt, p.astype(BF16), preferred_element_type=F32)
                out.append((m_new, l, acc))
            return tuple(out)

        def put(slot, j):
            for h, st in enumerate(logits(j)):
                st_ref[2 * slot + h] = st

        def run(j0, steps, stats):
            for d in range(steps):
                put(1 - d % 2, j0 + d + 1)
                stats = softmax_pv(j0 + d, d % 2, stats, False)
            return stats

        init = tuple((jnp.full((1, tq), NEG, F32), jnp.zeros((1, tq), F32), jnp.zeros((HEAD_DIM, tq), F32))
                     for _ in range(2))
        put(0, 0)
        first, left, stats = 0, 2 * i, init
        for size in UNROLLS:
            trips = _shift_div(left, size)
            stats = lax.fori_loop(0, trips, lambda t, st, j0=first, n=size: run(j0 + n * t, n, st), stats)
            first, left = first + size * trips, left - size * trips
        put(1, 2 * i + 1)
        stats = softmax_pv(2 * i, 0, stats, True)
        (ma, la, acca), (mb, lb, accb) = softmax_pv(2 * i + 1, 1, stats, True)
        o_ref[...] = jnp.concatenate([acca / la, accb / lb], axis=0).T.astype(BF16)
        lse_ref[...] = jnp.where(_iota2((2, tq), 0) == 0, ma + jnp.log(la), mb + jnp.log(lb))

    return pl.pallas_call(
        body,
        grid=(N_PAIRS, s // tq),
        in_specs=[
            pl.BlockSpec((tq, LANES), lambda p, i: (i, p)),
            pl.BlockSpec((s, LANES), lambda p, i: (0, N_PAIRS + p)),
            pl.BlockSpec((s, LANES), lambda p, i: (0, 2 * N_PAIRS + p)),
            pl.BlockSpec((None, 2, tq), lambda p, i: (p, 0, i)),
            pl.BlockSpec((None, 2, s), lambda p, i: (p, 0, 0)),
        ],
        out_specs=[
            pl.BlockSpec((tq, LANES), lambda p, i: (i, p)),
            pl.BlockSpec((None, None, 2, tq), lambda p, i: (p, i, 0, 0)),
        ],
        out_shape=[SDS((s, ATTN_W), BF16), SDS((N_PAIRS, s // tq, 2, tq), F32)],
        scratch_shapes=[pltpu.VMEM((2 * nb, tk, LANES), BF16), pltpu.VMEM((nb, LANES, tk), BF16),
                        pltpu.VMEM((4, tk, tq), F32)],
        compiler_params=_cparams("arbitrary", "arbitrary"),
        name="attn_fwd",
    )(qkv, qkv, qkv, c_rows, c_rows)


def _pool_counts(row0, tm, w):
    t = row0 + _iota2((tm, 1), 0)
    return jnp.minimum(t + 1, w).astype(F32)


def _out_gate_up(attn_o, pool_o, w_out, x, g2, wg_t, wu_t, *, tm):
    s = x.shape[0]

    def body(a_ref, p_ref, wo_ref, x_ref, g_ref, wg_ref, wu_ref, x1_ref, h2_ref, r_ref, gate_ref, up_ref, act_ref):
        x1 = (x_ref[...] + jnp.dot(a_ref[...], wo_ref[0:ATTN_W, :], preferred_element_type=F32)
              + jnp.dot(p_ref[...], wo_ref[ATTN_W:, :], preferred_element_type=F32))
        r = lax.rsqrt(jnp.mean(x1 * x1, axis=-1, keepdims=True) + EPS)
        x1_ref[...] = x1
        r_ref[...] = r
        h2 = (x1 * r * g_ref[...]).astype(BF16)
        h2_ref[...] = h2
        gate = lax.dot_general(h2, wg_ref[...], NT, preferred_element_type=F32)
        up = lax.dot_general(h2, wu_ref[...], NT, preferred_element_type=F32)
        gate_ref[...] = gate.astype(BF16)
        up_ref[...] = up.astype(BF16)
        act_ref[...] = (gate * jax.nn.sigmoid(gate) * up).astype(BF16)

    row = lambda w: pl.BlockSpec((tm, w), lambda i: (i, 0))
    full = lambda a, b: pl.BlockSpec((a, b), lambda i: (0, 0))
    return pl.pallas_call(
        body,
        grid=(s // tm,),
        in_specs=[row(ATTN_W), row(POOL_W), full(D_MODEL, D_MODEL), row(D_MODEL), full(1, D_MODEL),
                  full(D_FF, D_MODEL), full(D_FF, D_MODEL)],
        out_specs=[row(D_MODEL), row(D_MODEL), row(1), row(D_FF), row(D_FF), row(D_FF)],
        out_shape=[SDS((s, D_MODEL), F32), SDS((s, D_MODEL), BF16), SDS((s, 1), F32), SDS((s, D_FF), BF16),
                   SDS((s, D_FF), BF16), SDS((s, D_FF), BF16)],
        compiler_params=_cparams("arbitrary"),
        name="out_gate_up",
    )(attn_o, pool_o, w_out, x, g2, wg_t, wu_t)


def _staggered(n, start, finish):
    pending = start(0)
    for k in range(n):
        following = start(k + 1) if k + 1 < n else None
        finish(k, pending)
        pending = following


def _down_final(act, wd, x1, gf, tgt, *, tm, sub):
    s = x1.shape[0]

    def body(a_ref, w_ref, x1_ref, g_ref, t_ref, dx2_ref, loss_ref, dgf_ref):
        @pl.when(pl.program_id(0) == 0)
        def _():
            loss_ref[...] = jnp.zeros_like(loss_ref)
            dgf_ref[...] = jnp.zeros_like(dgf_ref)

        g = g_ref[...]

        def matmul(k):
            return jnp.dot(a_ref[k * sub:(k + 1) * sub, :], w_ref[...], preferred_element_type=F32)

        def rest(k, mm):
            rows = slice(k * sub, (k + 1) * sub)
            x2 = x1_ref[rows, :] + mm
            r = lax.rsqrt(jnp.mean(x2 * x2, axis=-1, keepdims=True) + EPS)
            xn = x2 * r
            diff = xn * g - t_ref[rows, :]
            loss_ref[...] += jnp.sum(diff * diff, axis=0, keepdims=True)
            dy = diff * (1.0 / D_MODEL)
            dgf_ref[...] += jnp.sum(dy * xn, axis=0, keepdims=True)
            dxn = dy * g
            dx2_ref[rows, :] = r * (dxn - xn * jnp.mean(dxn * xn, axis=-1, keepdims=True))

        _staggered(tm // sub, matmul, rest)

    row = lambda w: pl.BlockSpec((tm, w), lambda i: (i, 0))
    full = lambda a, b: pl.BlockSpec((a, b), lambda i: (0, 0))
    return pl.pallas_call(
        body,
        grid=(s // tm,),
        in_specs=[row(D_FF), full(D_FF, D_MODEL), row(D_MODEL), full(1, D_MODEL), row(D_MODEL)],
        out_specs=[row(D_MODEL), full(1, D_MODEL), full(1, D_MODEL)],
        out_shape=[SDS((s, D_MODEL), F32), SDS((1, D_MODEL), F32), SDS((1, D_MODEL), F32)],
        compiler_params=_cparams("arbitrary"),
        name="down_final",
    )(act, wd, x1, gf, tgt)


def _swiglu_bwd(dx2, wd, gate, up, *, tm, tn):
    s = dx2.shape[0]

    def body(d_ref, w_ref, gate_ref, up_ref, dgate_ref, dup_ref):
        dact = lax.dot_general(d_ref[...].astype(BF16), w_ref[...], NT, preferred_element_type=F32)
        gate = gate_ref[...].astype(F32)
        sg = jax.nn.sigmoid(gate)
        dup_ref[...] = (dact * (gate * sg)).astype(BF16)
        dgate_ref[...] = (dact * up_ref[...].astype(F32) * (sg * (1.0 + gate * (1.0 - sg)))).astype(BF16)

    ospec = pl.BlockSpec((tm, tn), lambda c, r: (r, c))
    return pl.pallas_call(
        body,
        grid=(D_FF // tn, s // tm),
        in_specs=[pl.BlockSpec((tm, D_MODEL), lambda c, r: (r, 0)), pl.BlockSpec((tn, D_MODEL), lambda c, r: (c, 0)),
                  ospec, ospec],
        out_specs=[ospec, ospec],
        out_shape=[SDS((s, D_FF), BF16), SDS((s, D_FF), BF16)],
        compiler_params=_cparams("arbitrary", "arbitrary"),
        name="swiglu_bwd",
    )(dx2, wd, gate, up)


def _mm_tn_stacked(as_, rows, b, *, ts, name, windows=None):
    s, nb_ = b.shape
    n = len(as_)
    offsets = [sum(rows[:i]) for i in range(n)]
    total = sum(rows)
    if windows is None:
        acc_rows, out_shape = total, (total, nb_)
    else:
        count, step, size = windows
        acc_rows, out_shape = max(total, (count - 1) * step + size), (count, size, nb_)

    def body(*refs):
        a_refs, b_ref, o_ref, acc_ref = refs[:n], refs[n], refs[n + 1], refs[n + 2]
        k = pl.program_id(0)

        @pl.when(k == 0)
        def _():
            acc_ref[...] = jnp.zeros_like(acc_ref)

        bv = b_ref[...].astype(BF16)
        for a_ref, off, cnt in zip(a_refs, offsets, rows):
            part = lax.dot_general(a_ref[...].astype(BF16), bv, TN, preferred_element_type=F32)
            acc_ref[off:off + cnt, :] += part[0:cnt, :]

        @pl.when(k == s // ts - 1)
        def _():
            if windows is None:
                o_ref[...] = acc_ref[...].astype(BF16)
            else:
                for d in range(count):
                    o_ref[d] = acc_ref[d * step:d * step + size, :].astype(BF16)

    return pl.pallas_call(
        body,
        grid=(s // ts,),
        in_specs=[pl.BlockSpec((ts, a.shape[1]), lambda k: (k, 0)) for a in as_] + [pl.BlockSpec((ts, nb_), lambda k: (k, 0))],
        out_specs=pl.BlockSpec(out_shape, lambda k: (0,) * len(out_shape)),
        out_shape=SDS(out_shape, BF16),
        scratch_shapes=[pltpu.VMEM((acc_rows, nb_), F32)],
        compiler_params=_cparams("arbitrary"),
        name=name,
    )(*as_, b)


def _mm_tn_phases(as_, bs, *, ts, name):
    n = len(as_)
    s, r = as_[0].shape
    c = bs[0].shape[1]
    steps = s // ts
    operands, phases, where = [], [], []
    for i, arr in enumerate(list(as_) + list(bs)):
        known = [j for j, o in enumerate(operands) if o is arr]
        if known:
            phases[known[0]].append(i % n)
        else:
            operands.append(arr)
            phases.append([i % n])
        where.append(known[0] if known else len(operands) - 1)
    m = len(operands)

    def body(*refs):
        outs, acc_ref, stage_ref, sem = refs[m:m + n], refs[m + n], refs[m + n + 1], refs[m + n + 2]
        p, k = pl.program_id(0), pl.program_id(1)
        written = lambda i: pltpu.make_async_copy(stage_ref, outs[i], sem.at[i])

        @pl.when(k == 0)
        def _():
            acc_ref[...] = jnp.zeros_like(acc_ref)

        for i in range(n):
            @pl.when(p == i)
            def _(i=i):
                a, b = refs[where[i]][...].astype(BF16), refs[where[n + i]][...].astype(BF16)
                acc_ref[...] += lax.dot_general(a, b, TN, preferred_element_type=F32)

            @pl.when((p == i) & (k == steps - 1))
            def _(i=i):
                if i > 0:
                    written(i - 1).wait()
                stage_ref[...] = acc_ref[...].astype(BF16)
                written(i).start()
                if i == n - 1:
                    written(i).wait()

    def spec(arr, ph):
        lo, hi = min(ph), max(ph)
        return pl.BlockSpec((ts, arr.shape[1]),
                            lambda p, k: (jnp.where(p < lo, 0, jnp.where(p > hi, steps - 1, k)), 0))

    return pl.pallas_call(
        body,
        grid=(n, steps),
        in_specs=[spec(arr, ph) for arr, ph in zip(operands, phases)],
        out_specs=[pl.BlockSpec(memory_space=pl.ANY)] * n,
        out_shape=[SDS((r, c), BF16)] * n,
        scratch_shapes=[pltpu.VMEM((r, c), F32), pltpu.VMEM((r, c), BF16), pltpu.SemaphoreType.DMA((n,))],
        compiler_params=_cparams("arbitrary", "arbitrary"),
        name=name,
    )(*operands)


def _norm_bwd(dh, x, r, g, dres):
    xn = x * r
    dxn = dh * g
    dx = dres + r * (dxn - xn * jnp.mean(dxn * xn, axis=-1, keepdims=True))
    return dx, jnp.sum(dh * xn, axis=0, keepdims=True)


def _mlp_in_pool_bwd(dgate, dup, wg_t, wu_t, w_out, x1, r2, g2, dx2, pooled, w_pool, pool_scale, *, tm):
    s = x1.shape[0]
    nt = s // tm
    ng = len(POOL_WINDOWS)

    def body(dg_ref, dup_ref, wg_ref, wu_ref, wo_ref, x_ref, r_ref, g_ref, d_ref, p_ref, w_ref, sc_ref,
             dx1_ref, dattn_ref, du_ref, dg2_ref, dw_ref, dsc_ref, head_ref):
        i = pl.program_id(0)

        @pl.when(i == 0)
        def _():
            dg2_ref[...] = jnp.zeros_like(dg2_ref)
            head_ref[...] = jnp.zeros_like(head_ref)
            dw_ref[...] = jnp.zeros_like(dw_ref)
            dsc_ref[...] = jnp.zeros_like(dsc_ref)

        dh2 = (jnp.dot(dg_ref[...], wg_ref[...], preferred_element_type=F32)
               + jnp.dot(dup_ref[...], wu_ref[...], preferred_element_type=F32))
        dx1, dg2 = _norm_bwd(dh2, x_ref[...], r_ref[...], g_ref[...], d_ref[...])
        dg2_ref[...] += dg2
        dx1_ref[...] = dx1
        dmix = lax.dot_general(dx1.astype(BF16), wo_ref[...], NT, preferred_element_type=F32)
        dattn_ref[...] = dmix[:, 0:ATTN_W]
        row0 = (nt - 1 - i) * tm
        for g, w in enumerate(POOL_WINDOWS):
            cols = slice(g * POOL_G, (g + 1) * POOL_G)
            wb = w_ref[g].astype(BF16)
            pooled_g = p_ref[:, cols]
            dpo = dmix[:, ATTN_W + g * POOL_G:ATTN_W + (g + 1) * POOL_G]
            mixed = jnp.dot(pooled_g, wb, preferred_element_type=F32)
            dsc_ref[:, cols] += jnp.sum(dpo * mixed, axis=0, keepdims=True)
            dmp = (dpo * sc_ref[:, cols]).astype(BF16)
            dw_ref[g] += lax.dot_general(pooled_g, dmp, TN, preferred_element_type=F32)
            dpooled = lax.dot_general(dmp, wb, NT, preferred_element_type=F32)
            a = dpooled / _pool_counts(row0, tm, w)
            acc = jnp.concatenate([a, head_ref[:, cols]], axis=0)
            head_ref[:, cols] = a[0:HALO, :]
            k = 1
            while k < w:
                acc = acc + pltpu.roll(acc, tm + HALO - k, axis=0)
                k *= 2
            du_ref[:, cols] = (acc[0:tm, :] - dpooled).astype(BF16)

    row = lambda w: pl.BlockSpec((tm, w), lambda i: (nt - 1 - i, 0))
    full = lambda a, b: pl.BlockSpec((a, b), lambda i: (0, 0))
    pool_w = pl.BlockSpec((ng, POOL_G, POOL_G), lambda i: (0, 0, 0))
    return pl.pallas_call(
        body,
        grid=(nt,),
        in_specs=[row(D_FF), row(D_FF), full(D_FF, D_MODEL), full(D_FF, D_MODEL), full(D_MODEL, D_MODEL),
                  row(D_MODEL), row(1), full(1, D_MODEL), row(D_MODEL), row(POOL_W), pool_w, full(1, POOL_W)],
        out_specs=[row(D_MODEL), row(ATTN_W), row(POOL_W), full(1, D_MODEL), pool_w, full(1, POOL_W)],
        out_shape=[SDS((s, D_MODEL), F32), SDS((s, ATTN_W), F32), SDS((s, POOL_W), BF16), SDS((1, D_MODEL), F32),
                   SDS((ng, POOL_G, POOL_G), F32), SDS((1, POOL_W), F32)],
        scratch_shapes=[pltpu.VMEM((HALO, POOL_W), F32)],
        compiler_params=_cparams("arbitrary"),
        name="mlp_in_pool_bwd",
    )(dgate, dup, wg_t, wu_t, w_out, x1, r2, g2, dx2, pooled, w_pool, pool_scale)


SUM_ROWS = 16


def _heads_t(t):
    n = t.shape[0]
    lane = _iota2((n, LANES), 1)
    tf = t.astype(F32)
    halves = jnp.concatenate([jnp.where(lane < HEAD_DIM, tf, 0.0).T, jnp.where(lane < HEAD_DIM, 0.0, tf).T], axis=1)
    r, c = _iota2((SUM_ROWS, 2 * n), 0), _iota2((SUM_ROWS, 2 * n), 1)
    ones = jnp.where(((r == 0) & (c < n)) | ((r == 4) & (c >= n)), 1.0, 0.0)
    return jnp.concatenate([halves, ones], axis=0).astype(BF16)


def _attn_bwd(qkv, attn_o, d_attn, rowb, c_rows, after, *, tq):
    s = qkv.shape[0]
    tk = tq
    nb = s // tq
    rows_t = LANES + SUM_ROWS

    def body(q_ref, k_ref, v_ref, o_ref, do_ref, rowb_ref, ck_ref, _, dq_ref, dk_ref, dv_ref, dck_ref, dcq_ref,
             dqt_ref, delta_ref, kp_ref, qp_ref, dob_ref, qt_ref, kt_ref, dot_ref, front_ref):
        lane = _iota2((tq, LANES), 1)
        lo = lane < HEAD_DIM
        first = _iota2((8, LANES), 1) < HEAD_DIM
        sel = jnp.where(_iota2((8, LANES), 0) < 4, jnp.where(first, 1.0, 0.0), jnp.where(first, 0.0, 1.0))

        def prep(b, _):
            st = pl.multiple_of(b * tq, tq)
            do2 = do_ref[pl.ds(st, tq), :]
            delta_ref[b] = _sel_dot(sel, do2 * o_ref[pl.ds(st, tq), :].astype(F32), NT)
            dob_ref[pl.ds(st, tq), :] = do2.astype(BF16)
            dqt_ref[b] = jnp.zeros((rows_t, tq), F32)
            k2 = k_ref[pl.ds(st, tq), :].astype(F32)
            q2 = q_ref[pl.ds(st, tq), :].astype(F32)
            ck = ck_ref[:, pl.ds(st, tq)]
            for h in range(2):
                kp_ref[h * nb + b] = _augment(k2, h, -ck[h:h + 1, :], True)
                qp_ref[h * nb + b] = _augment(q2 * Q_SCALE, h, None, False)
            qt_ref[b] = _heads_t(q2)
            kt_ref[b] = _heads_t(k2)
            dot_ref[b] = _heads_t(do2)[0:LANES, :]
            return 0

        lax.fori_loop(0, nb, prep, 0)

        def split(t):
            z = jnp.zeros_like(t)
            return jnp.where(lo, t, z), jnp.where(lo, z, t)

        def kv_block(j, _):
            st_j = pl.multiple_of(j * tk, tk)
            vs = split(v_ref[pl.ds(st_j, tk), :])
            kt = kt_ref[j]

            def stage(i, slot):
                ic = jnp.minimum(i, nb - 1)
                do2 = dob_ref[pl.ds(pl.multiple_of(ic * tq, tq), tq), :]
                for h in range(2):
                    front_ref[4 * slot + h] = lax.dot_general(kp_ref[h * nb + j], qp_ref[h * nb + ic], NT,
                                                              preferred_element_type=F32)
                    front_ref[4 * slot + 2 + h] = lax.dot_general(vs[h], do2, NT, preferred_element_type=F32)

            def q_block(i, slot, carry, diagonal):
                dkt, dvt = carry
                ic = jnp.minimum(i, nb - 1)
                rb = rowb_ref[ic] + jnp.where(i < nb, 0.0, NEG)
                dl = delta_ref[ic]
                pts, dsts = [], []
                for h in range(2):
                    st = front_ref[4 * slot + h] + rb[h:h + 1, :]
                    if diagonal:
                        st = jnp.where(_iota2((tk, tq), 0) <= _iota2((tk, tq), 1), st, NEG)
                    pt = jnp.exp(st)
                    pts.append(pt.astype(BF16))
                    dsts.append((pt * (front_ref[4 * slot + 2 + h] - dl[4 * h:4 * h + 1, :])).astype(BF16))
                dvt = dvt + lax.dot_general(dot_ref[ic], jnp.concatenate(pts, axis=1), NT, preferred_element_type=F32)
                dkt = dkt + lax.dot_general(qt_ref[ic], jnp.concatenate(dsts, axis=1), NT, preferred_element_type=F32)
                dqt_ref[ic] += jnp.dot(kt, jnp.concatenate(dsts, axis=0), preferred_element_type=F32)
                return dkt, dvt

            def run(i0, steps, carry):
                for d in range(steps):
                    stage(i0 + d + 1, d % 2)
                    carry = q_block(i0 + d, 1 - d % 2, carry, False)
                return carry

            stage(j, 0)
            stage(j + 1, 1)
            carry = q_block(j, 0, (jnp.zeros((rows_t, tk), F32), jnp.zeros((LANES, tk), F32)), True)
            first, left = j + 1, nb - 1 - j
            for size in UNROLLS:
                trips = _shift_div(left + 1 if size == UNROLLS[-1] else left, size)
                carry = lax.fori_loop(0, trips, lambda t, c, i0=first, n=size: run(i0 + n * t, n, c), carry)
                first, left = first + size * trips, left - size * trips
            dkt, dvt = carry
            dk_ref[pl.ds(st_j, tk), :] = (dkt[0:LANES, :].T * Q_SCALE).astype(BF16)
            dv_ref[pl.ds(st_j, tk), :] = dvt.T.astype(BF16)
            dck_ref[j] = dkt[LANES:LANES + 8, :]
            return 0

        lax.fori_loop(0, nb, kv_block, 0)

        def finish(b, _):
            acc = dqt_ref[b]
            dq_ref[pl.ds(pl.multiple_of(b * tq, tq), tq), :] = (acc[0:LANES, :].T * Q_SCALE).astype(BF16)
            dcq_ref[b] = acc[LANES:LANES + 8, :]
            return 0

        lax.fori_loop(0, nb, finish, 0)

    col = lambda off: pl.BlockSpec((s, LANES), lambda p: (0, off + p))
    sums = pl.BlockSpec((None, nb, 8, tq), lambda p: (p, 0, 0, 0))
    return pl.pallas_call(
        body,
        grid=(N_PAIRS,),
        in_specs=[col(0), col(N_PAIRS), col(2 * N_PAIRS), col(0), col(0),
                  pl.BlockSpec((None, nb, 2, tq), lambda p: (p, 0, 0, 0)),
                  pl.BlockSpec((None, 2, s), lambda p: (p, 0, 0)), _UNREAD],
        out_specs=[col(0), col(0), col(0), sums, sums],
        out_shape=[SDS((s, ATTN_W), BF16), SDS((s, ATTN_W), BF16), SDS((s, ATTN_W), BF16),
                   SDS((N_PAIRS, nb, 8, tq), F32), SDS((N_PAIRS, nb, 8, tq), F32)],
        scratch_shapes=[pltpu.VMEM((nb, rows_t, tq), F32), pltpu.VMEM((nb, 8, tq), F32),
                        pltpu.VMEM((2 * nb, tk, LANES), BF16), pltpu.VMEM((2 * nb, tq, LANES), BF16),
                        pltpu.VMEM((s, LANES), BF16), pltpu.VMEM((nb, rows_t, 2 * tq), BF16),
                        pltpu.VMEM((nb, rows_t, 2 * tk), BF16), pltpu.VMEM((nb, LANES, 2 * tq), BF16),
                        pltpu.VMEM((8, tk, tq), F32)],
        compiler_params=_cparams("arbitrary"),
        name="attn_bwd",
    )(qkv, qkv, qkv, attn_o, d_attn, rowb, c_rows, after)


def _forget_bwd(dc_t, fl_t, b_rows):
    rows = fl_t.shape[0]
    nb = rows // N_HEADS

    def body(dc_ref, fl_ref, b_ref, dfl_ref, db_ref):
        dc = dc_ref[...]
        lower = _iota2((LANES, LANES), 0) >= _iota2((LANES, LANES), 1)
        ones = jnp.ones((LANES, LANES), F32)
        rr, cc, same = _head_block_masks(rows, nb)
        dlf = _dot_sel(dc, lower) + _sel_dot(same & (cc > rr), _dot_sel(dc, ones))
        dfl = dlf / (1.0 + jnp.exp(fl_ref[...] + b_ref[...]))
        dfl_ref[...] = dfl
        shift = nb.bit_length() - 1
        hsel = lax.shift_right_logical(_iota2((N_HEADS, rows), 1), shift) == _iota2((N_HEADS, rows), 0)
        db_ref[...] = _sel_dot(hsel, _dot_sel(dfl, ones))

    return pl.pallas_call(body, out_shape=[SDS(fl_t.shape, F32), SDS((N_HEADS, LANES), F32)],
                          compiler_params=_cparams(), name="forget_bwd")(dc_t, fl_t, b_rows)


def _in_bwd(dq, dk, dv, du, dfl, w_in_t, x, r1, g1, dx1, after, *, tm):
    s = x.shape[0]
    pieces = ((0, ATTN_W), (ATTN_W, 2 * ATTN_W), (2 * ATTN_W, QKV_W), (U_OFF, F_OFF), (F_OFF, IN_PAD))

    def body(dq_ref, dk_ref, dv_ref, du_ref, df_ref, w_ref, x_ref, r_ref, g_ref, d_ref, _, dx_ref, dg1_ref):
        @pl.when(pl.program_id(0) == 0)
        def _():
            dg1_ref[...] = jnp.zeros_like(dg1_ref)

        dh = None
        for ref, (c0, c1) in zip((dq_ref, dk_ref, dv_ref, du_ref, df_ref), pieces):
            t = jnp.dot(ref[...], w_ref[c0:c1, :], preferred_element_type=F32)
            dh = t if dh is None else dh + t
        dx, dg1 = _norm_bwd(dh, x_ref[...], r_ref[...], g_ref[...], d_ref[...])
        dx_ref[...] = dx
        dg1_ref[...] += dg1

    row = lambda w: pl.BlockSpec((tm, w), lambda i: (i, 0))
    full = lambda a, b: pl.BlockSpec((a, b), lambda i: (0, 0))
    return pl.pallas_call(
        body,
        grid=(s // tm,),
        in_specs=[row(ATTN_W), row(ATTN_W), row(ATTN_W), row(POOL_W), row(LANES), full(IN_PAD, D_MODEL),
                  row(D_MODEL), row(1), full(1, D_MODEL), row(D_MODEL), _UNREAD],
        out_specs=[row(D_MODEL), full(1, D_MODEL)],
        out_shape=[SDS((s, D_MODEL), F32), SDS((1, D_MODEL), F32)],
        compiler_params=_cparams("arbitrary"),
        name="in_bwd",
    )(dq, dk, dv, du, dfl, w_in_t, x, r1, g1, dx1, after)


def _tiles(s):
    big = min(512, s)
    return dict(row=big, attn=min(256, s // 2), ff_rows=min(256, s), tall=min(1024, s))


def _local_step(x, tgt, p, weight, emit, started):
    s = x.shape[0]
    t = _tiles(s)
    tm, tq = t["row"], t["attn"]
    nb = s // LANES
    nqb = s // tq
    g1, g2, gf = p["norm1_g"], p["norm2_g"], p["final_g"].reshape(1, D_MODEL)
    w_pool, pool_scale = p["w_pool"][0], p["pool_scale"]

    h, r1 = _norm1(x, g1, started, tm=tm)
    w_in_t = weight("w_in", h)
    qkv, fl, pooled, pool_o = _in_proj_pool(h, w_in_t, w_pool, pool_scale, tm=t["tall"])
    fl_t = fl[:, :N_HEADS].T.reshape(N_HEADS * nb, LANES)
    b_rows = jnp.repeat(p["b_forget"].reshape(N_HEADS), nb).reshape(N_HEADS * nb, 1)
    c = _forget_cumsum(fl_t, b_rows).reshape(N_PAIRS, 2, s)
    c_rowblk = c.reshape(N_PAIRS, 2, nqb, tq).transpose(0, 2, 1, 3)
    attn_o, lse = _attn_fwd(qkv, c, tk=tq)
    lse = lse.reshape(N_PAIRS, nqb // 2, 2, 2, tq).transpose(0, 1, 3, 2, 4).reshape(N_PAIRS, nqb, 2, tq)
    w_out = weight("w_out", attn_o)
    wg_t, wu_t = weight("w_gate_up", attn_o)
    x1, h2, r2, gate, up, act = _out_gate_up(attn_o, pool_o, w_out, x, g2, wg_t, wu_t, tm=t["ff_rows"])
    wd = weight("w_down", act)
    dx2, loss_row, d_gf = _down_final(act, wd, x1, gf, tgt, tm=tm, sub=min(128, tm))

    dgate, dup = _swiglu_bwd(dx2, wd, gate, up, tm=t["ff_rows"], tn=D_FF)
    d_wd, d_wg_t, d_wu_t = _mm_tn_phases([act, dgate, dup], [dx2, h2, h2], ts=t["row"], name="grad_w_ff")
    dx1, d_attn, du, d_g2, d_wpool, d_pscale = _mlp_in_pool_bwd(dgate, dup, wg_t, wu_t, w_out, x1, r2, g2, dx2, pooled,
                                                               w_pool, pool_scale, tm=t["ff_rows"])
    d_wo = _mm_tn_stacked([attn_o, pool_o], [ATTN_W, POOL_W], dx1, ts=t["tall"], name="grad_w_out")
    token = emit(("w_down", "w_gate", "w_up", "w_out"), (d_wd, d_wg_t, d_wu_t, d_wo))
    dq, dk, dv, dck, dcq = _attn_bwd(qkv, attn_o, d_attn, c_rowblk - lse, c, token, tq=tq)
    dc_t = (dcq - dck)[:, :, 0::4, :].transpose(0, 2, 1, 3).reshape(N_HEADS * nb, LANES)
    dfl_t, db = _forget_bwd(dc_t, fl_t, b_rows)
    dfl = jnp.pad(dfl_t.reshape(N_HEADS, s).T, ((0, 0), (0, LANES - N_HEADS))).astype(BF16)
    d_w_in_t = _mm_tn_stacked([dq, dk, dv, dfl, du], [ATTN_W, ATTN_W, ATTN_W, N_HEADS, POOL_W], h, ts=t["tall"],
                              name="grad_w_in",
                              windows=(N_DEV, IN_STEP, IN_WINDOW))
    token = emit(("w_in",), (d_w_in_t,))
    dx, d_g1 = _in_bwd(dq, dk, dv, du, dfl, w_in_t, x, r1, g1, dx1, token, tm=tm)

    small = dict(norm1_g=d_g1, b_forget=db[:, 0].reshape(1, N_HEADS), w_pool=d_wpool, pool_scale=d_pscale,
                 norm2_g=d_g2, final_g=d_gf)
    return loss_row, dx, small


def _my_index():
    return 4 * lax.axis_index("x") + 2 * lax.axis_index("y") + lax.axis_index("c")


def _peer(k):
    pos = [lax.axis_index(a) for a in ("x", "y", "c")]
    flipped = tuple(1 - p if (k >> b) & 1 else p for p, b in zip(pos, (2, 1, 0)))
    return flipped, 4 * flipped[0] + 2 * flipped[1] + flipped[2]


_HBM = pl.BlockSpec(memory_space=pltpu.HBM)
_SEM = pl.BlockSpec(memory_space=pltpu.SEMAPHORE)
_DATAFLOW = pltpu.SideEffectType.DATAFLOW_SIDE_EFFECTING


ALL_PEERS = tuple(range(1, N_DEV))
SAME_CORE = (1, 2, 4, 6)


def _peer_copies(ins, lands, send_sems, recv_sems, scatter, peers, arrivals):
    me = _my_index()
    copies = []
    for w in range(len(ins)):
        for k in peers[w]:
            dev, idx = _peer(k)
            copies.append(pltpu.make_async_remote_copy(
                src_ref=ins[w].at[idx] if scatter[w] else ins[w], dst_ref=lands[w].at[idx if arrivals else me],
                send_sem=send_sems[w].at[k - 1], recv_sem=recv_sems[w].at[k - 1], device_id=dev, device_id_type=MESH))
    return copies


def _own_copies(ins, lands, send_sems, scatter):
    me = _my_index()
    return [pltpu.make_async_copy(ins[w].at[me] if scatter[w] else ins[w], lands[w].at[me], send_sems[w].at[N_DEV - 1])
            for w in range(len(ins))]


def _forward_copies(land, send_sems, recv_sems, arrivals):
    sibling, _ = _peer(1)
    copies = []
    for j, k in enumerate(SAME_CORE[1:]):
        src, dst = _peer(k)[1], _peer(k ^ 1 if arrivals else k)[1]
        copies.append(pltpu.make_async_remote_copy(
            src_ref=land.at[src], dst_ref=land.at[dst], send_sem=send_sems.at[j], recv_sem=recv_sems.at[j],
            device_id=sibling, device_id_type=MESH))
    return copies


def _forward_start(land, name):
    def body(land_ref, send_sems, recv_sems, land_thru, token):
        for cp in _forward_copies(land_ref, send_sems, recv_sems, False):
            cp.start()
        token[...] = jnp.zeros_like(token)

    sem = pltpu.SemaphoreType.DMA((len(SAME_CORE) - 1,))
    send, recv, thru, _ = pl.pallas_call(
        body,
        in_specs=[_HBM],
        out_specs=[_SEM, _SEM, _HBM, pl.BlockSpec(memory_space=pltpu.VMEM)],
        out_shape=[sem, sem, pltpu.HBM(land.shape, land.dtype), SDS((8, LANES), F32)],
        input_output_aliases={0: 2},
        compiler_params=pltpu.CompilerParams(has_side_effects=_DATAFLOW),
        name=name,
    )(land)
    return send, recv, thru


def _forward_wait(handle, after, name):
    def body(land_ref, send_sems, recv_sems, after_ref, land_out):
        for cp in _forward_copies(land_ref, send_sems, recv_sems, False):
            cp.wait_send()
        for cp in _forward_copies(land_ref, send_sems, recv_sems, True):
            cp.wait_recv()

    send, recv, land = handle
    return pl.pallas_call(
        body,
        in_specs=[_HBM, _SEM, _SEM, pl.BlockSpec(memory_space=pl.ANY)],
        out_specs=_HBM,
        out_shape=pltpu.HBM(land.shape, land.dtype),
        input_output_aliases={0: 0},
        compiler_params=pltpu.CompilerParams(has_side_effects=_DATAFLOW),
        name=name,
    )(land, send, recv, after)


def _exchange_start(arrays, scatter, name, peers=None, after=None):
    n = len(arrays)
    peers = peers or [ALL_PEERS] * n
    order = [] if after is None else [after]
    land_shapes = [(N_DEV,) + tuple(a.shape[1:] if sc else a.shape) for a, sc in zip(arrays, scatter)]

    def body(*refs):
        ins, lands = refs[:n], refs[n:2 * n]
        outs = refs[2 * n + len(order):]
        send_sems, recv_sems, token = outs[:n], outs[n:2 * n], outs[4 * n]
        for cp in _peer_copies(ins, lands, send_sems, recv_sems, scatter, peers, False):
            cp.start()
        for cp in _own_copies(ins, lands, send_sems, scatter):
            cp.start()
        token[...] = jnp.zeros_like(token)

    sends, recvs = pltpu.SemaphoreType.DMA((N_DEV,)), pltpu.SemaphoreType.DMA((N_DEV - 1,))
    outs = pl.pallas_call(
        body,
        in_specs=[_HBM] * (2 * n) + [_UNREAD] * len(order),
        out_specs=[_SEM] * (2 * n) + [_HBM] * (2 * n) + [pl.BlockSpec(memory_space=pltpu.VMEM)],
        out_shape=[sends] * n + [recvs] * n + [pltpu.HBM(a.shape, a.dtype) for a in arrays]
        + [pltpu.HBM(sh, a.dtype) for sh, a in zip(land_shapes, arrays)] + [SDS((8, LANES), F32)],
        input_output_aliases={i: 2 * n + i for i in range(2 * n)},
        compiler_params=pltpu.CompilerParams(has_side_effects=_DATAFLOW),
        name=name,
    )(*[pltpu.with_memory_space_constraint(a, pltpu.HBM) for a in arrays],
      *[pltpu.with_memory_space_constraint(lax.empty(sh, a.dtype), pltpu.HBM) for sh, a in zip(land_shapes, arrays)],
      *order)
    handles = [dict(send=outs[w], recv=outs[n + w], src=outs[2 * n + w], land=outs[3 * n + w], scatter=scatter[w],
                    peers=peers[w]) for w in range(n)]
    return handles, outs[4 * n]


def _exchange_wait(handles, after, name):
    n = len(handles)
    scatter, peers = [h["scatter"] for h in handles], [h["peers"] for h in handles]

    def body(*refs):
        ins, lands = refs[:n], refs[n:2 * n]
        send_sems, recv_sems = refs[2 * n:3 * n], refs[3 * n:4 * n]
        for cp in _peer_copies(ins, lands, send_sems, recv_sems, scatter, peers, False):
            cp.wait_send()
        for cp in _peer_copies(ins, lands, send_sems, recv_sems, scatter, peers, True):
            cp.wait_recv()
        for cp in _own_copies(ins, lands, send_sems, scatter):
            cp.wait()

    srcs, lands = [h["src"] for h in handles], [h["land"] for h in handles]
    outs = pl.pallas_call(
        body,
        in_specs=[_HBM] * (2 * n) + [_SEM] * (2 * n) + [pl.BlockSpec(memory_space=pl.ANY)],
        out_specs=[_HBM] * (2 * n),
        out_shape=[pltpu.HBM(a.shape, a.dtype) for a in srcs + lands],
        input_output_aliases={i: i for i in range(2 * n)},
        compiler_params=pltpu.CompilerParams(has_side_effects=_DATAFLOW),
        name=name,
    )(*srcs, *lands, *[h["send"] for h in handles], *[h["recv"] for h in handles], after)
    return outs[n:]


def _adamw(parts, w, m, v, name):
    rows, cols = w.shape
    tr = rows // 4 if rows % 32 == 0 else rows

    def body(p_ref, w_ref, m_ref, v_ref, g_ref, d_ref, mo_ref, vo_ref):
        g = p_ref[0].astype(F32)
        for d in range(1, N_DEV):
            g = g + p_ref[d].astype(F32)
        g_ref[...] = g
        d_ref[...], mo_ref[...], vo_ref[...] = _adam_update(g, w_ref[...], m_ref[...], v_ref[...])

    blk = pl.BlockSpec((tr, cols), lambda i: (i, 0))
    return pl.pallas_call(
        body,
        grid=(rows // tr,),
        in_specs=[pl.BlockSpec((N_DEV, tr, cols), lambda i: (0, i, 0)), blk, blk, blk],
        out_specs=[blk] * 4,
        out_shape=[SDS((rows, cols), F32)] * 4,
        compiler_params=_cparams("arbitrary"),
        name=name,
    )(parts, w, m, v)


def _adamw_phases(parts, ws, ms, vs, name):
    n = len(ws)
    rows, cols = ws[0].shape
    steps = 4
    tr = rows // steps

    def blk(i):
        return lambda p, k: jnp.where(p < i, 0, jnp.where(p > i, steps - 1, k))

    def body(*refs):
        ins, outs = refs[:4 * n], refs[4 * n:]
        for i in range(n):
            @pl.when(pl.program_id(0) == i)
            def _(i=i):
                p_ref, w_ref, m_ref, v_ref = ins[4 * i:4 * i + 4]
                g = p_ref[0].astype(F32)
                for d in range(1, N_DEV):
                    g = g + p_ref[d].astype(F32)
                g_ref, d_ref, mo_ref, vo_ref = outs[4 * i:4 * i + 4]
                g_ref[...] = g
                d_ref[...], mo_ref[...], vo_ref[...] = _adam_update(g, w_ref[...], m_ref[...], v_ref[...])

    in_specs, operands = [], []
    for i in range(n):
        in_specs.append(pl.BlockSpec((N_DEV, tr, cols), lambda p, k, f=blk(i): (0, f(p, k), 0)))
        in_specs += [pl.BlockSpec((tr, cols), lambda p, k, f=blk(i): (f(p, k), 0))] * 3
        operands += [parts[i], ws[i], ms[i], vs[i]]
    out_specs = [pl.BlockSpec((tr, cols), lambda p, k, f=blk(i): (f(p, k), 0)) for i in range(n) for _ in range(4)]
    outs = pl.pallas_call(
        body,
        grid=(n, steps),
        in_specs=in_specs,
        out_specs=out_specs,
        out_shape=[SDS((rows, cols), F32)] * (4 * n),
        compiler_params=_cparams("arbitrary", "arbitrary"),
        name=name,
    )(*operands)
    return [outs[4 * i:4 * i + 4] for i in range(n)]


def _adamw_dense(parts, w, m, v, name, *, rows, shift):
    _, window, cols = parts.shape
    per_row = cols // LANES

    def body(p_ref, w_ref, m_ref, v_ref, g_ref, d_ref, mo_ref, vo_ref, sum_ref):
        g = p_ref[0].astype(F32)
        for d in range(1, N_DEV):
            g = g + p_ref[d].astype(F32)
        sum_ref[...] = g
        me = _my_index()
        for j in range(N_DEV):
            @pl.when(me == j)
            def _(j=j):
                for c in range(per_row):
                    at = (pl.ds(c, rows, stride=per_row), slice(None))
                    gc = sum_ref[j * shift:j * shift + rows, c * LANES:(c + 1) * LANES]
                    g_ref[at] = gc
                    d_ref[at], mo_ref[at], vo_ref[at] = _adam_update(gc, w_ref[at], m_ref[at], v_ref[at])

    return pl.pallas_call(
        body,
        out_shape=[SDS(w.shape, F32)] * 4,
        scratch_shapes=[pltpu.VMEM((window, cols), F32)],
        compiler_params=_cparams(),
        name=name,
    )(parts, w, m, v)


_ROW_OF = dict(norm1_g=(0, D_MODEL), norm2_g=(1, D_MODEL), final_g=(2, D_MODEL), pool_scale=(3, POOL_W),
               b_forget=(4, N_HEADS), loss=(5, 1))


def _pack_rows(vals):
    rows = [jnp.pad(vals[n].reshape(1, width).astype(F32), ((0, 0), (0, D_MODEL - width)))
            for n, (_, width) in sorted(_ROW_OF.items(), key=lambda kv: kv[1][0])]
    return jnp.concatenate(rows + [jnp.zeros((8 - len(rows), D_MODEL), F32)], axis=0)


def _adam_update(g, w, m, v):
    m_new = ADAM_B1 * m + (1.0 - ADAM_B1) * g
    v_new = ADAM_B2 * v + (1.0 - ADAM_B2) * (g * g)
    m_hat = m_new / (1.0 - ADAM_B1 ** ADAM_STEP)
    v_hat = v_new / (1.0 - ADAM_B2 ** ADAM_STEP)
    return -ADAM_LR * (m_hat / (jnp.sqrt(v_hat) + ADAM_EPS) + ADAM_WD * w), m_new, v_new


def _adamw_replicated(parts_rows, parts_pool, w, m, v):
    names = ("norm1_g", "norm2_g", "final_g", "pool_scale", "b_forget", "w_pool")
    shapes = {n: ((len(POOL_WINDOWS), POOL_G, POOL_G) if n == "w_pool" else (1, _ROW_OF[n][1])) for n in names}

    def body(rows_ref, pool_ref, *refs):
        ins, outs = refs[:3 * len(names)], refs[3 * len(names):]

        def total(n):
            if n == "w_pool":
                pieces = [pool_ref[d] for d in range(N_DEV)]
            else:
                row, width = _ROW_OF[n]
                pieces = [rows_ref[d, row:row + 1, 0:width] for d in range(N_DEV)]
            g = pieces[0]
            for p in pieces[1:]:
                g = g + p
            return g

        outs[0][...] = total("loss")
        for k, n in enumerate(names):
            g = total(n)
            delta, m_new, v_new = _adam_update(g, ins[3 * k][...], ins[3 * k + 1][...], ins[3 * k + 2][...])
            for o_ref, val in zip(outs[1 + 4 * k:5 + 4 * k], (g, delta, m_new, v_new)):
                o_ref[...] = val

    args = [d[n].reshape(shapes[n]) for n in names for d in (w, m, v)]
    res = pl.pallas_call(
        body,
        out_shape=[SDS((1, 1), F32)] + [SDS(shapes[n], F32) for n in names for _ in range(4)],
        compiler_params=_cparams(),
        name="adamw_replicated",
    )(parts_rows, parts_pool, *args)
    return res[0], {n: [r.reshape(w[n].shape) for r in res[1 + 4 * k:5 + 4 * k]] for k, n in enumerate(names)}


def kernel(x, norm1_g, w_in, b_forget, w_pool, pool_scale, w_out, norm2_g, w_gate, w_up, w_down, final_g, loss_target, m_norm1_g, m_w_in, m_b_forget, m_w_pool, m_pool_scale, m_w_out, m_norm2_g, m_w_gate, m_w_up, m_w_down, m_final_g, v_norm1_g, v_w_in, v_b_forget, v_w_pool, v_pool_scale, v_w_out, v_norm2_g, v_w_gate, v_w_up, v_w_down, v_final_g):
    big = ("w_in", "w_out", "w_gate", "w_up", "w_down")
    order = ("norm1_g", "w_in", "b_forget", "w_pool", "pool_scale", "w_out", "norm2_g", "w_gate", "w_up", "w_down",
             "final_g")
    w = dict(norm1_g=norm1_g, w_in=w_in, b_forget=b_forget, w_pool=w_pool, pool_scale=pool_scale, w_out=w_out,
             norm2_g=norm2_g, w_gate=w_gate, w_up=w_up, w_down=w_down, final_g=final_g)
    m = dict(norm1_g=m_norm1_g, w_in=m_w_in, b_forget=m_b_forget, w_pool=m_w_pool, pool_scale=m_pool_scale,
             w_out=m_w_out, norm2_g=m_norm2_g, w_gate=m_w_gate, w_up=m_w_up, w_down=m_w_down, final_g=m_final_g)
    v = dict(norm1_g=v_norm1_g, w_in=v_w_in, b_forget=v_b_forget, w_pool=v_w_pool, pool_scale=v_pool_scale,
             w_out=v_w_out, norm2_g=v_norm2_g, w_gate=v_w_gate, w_up=v_w_up, w_down=v_w_down, final_g=v_final_g)

    flipped = ("w_in", "w_gate", "w_up")
    shard = lambda d, n: d[n][0].T if n in flipped else d[n][0]
    cast = lambda n: shard(w, n).astype(BF16)
    (first,), started = _exchange_start([cast("w_in")], [False], "gather_start_w_in", peers=[SAME_CORE])
    gather = dict(w_in=first)

    def gathered(names, after):
        return _exchange_wait([gather[n] for n in names], after, "gather_wait_" + names[0])

    def weight(name, after):
        if name == "w_in":
            handles, token = _exchange_start([cast(n) for n in big[1:]], [False] * len(big[1:]), "gather_start", after=after)
            gather.update(zip(big[1:], handles))
            forward = _forward_start(gathered(["w_in"], token)[0], "gather_forward_start")
            full = _forward_wait(forward, after, "gather_forward_wait").reshape(IN_W, D_MODEL)
            f0 = QKV_W + N_HEADS
            return jnp.concatenate([full[:QKV_W], full[f0:], full[QKV_W:f0],
                                    jnp.zeros((IN_PAD - IN_W, D_MODEL), BF16)], axis=0)
        if name == "w_out":
            return gathered(["w_out"], after)[0].reshape(D_MODEL, D_MODEL)
        if name == "w_gate_up":
            return [g.reshape(D_FF, D_MODEL) for g in gathered(["w_gate", "w_up"], after)]
        return gathered(["w_down"], after)[0].reshape(D_FF, D_MODEL)

    rows = lambda g: g if g.ndim == 3 else g.reshape(N_DEV, g.shape[0] // N_DEV, g.shape[1])
    sent = {}

    def emit(names, grads):
        handles, token = _exchange_start([rows(g) for g in grads], [True] * len(names), "grads_start_" + names[0])
        sent.update(zip(names, handles))
        return token

    loss_row, dx, small_grads = _local_step(x[0], loss_target[0], w, weight, emit, started)

    packed = _pack_rows(dict(small_grads, loss=0.5 / D_MODEL * jnp.sum(loss_row)))
    small_handles, after = _exchange_start([packed, small_grads["w_pool"]], [False, False], "grads_start_replicated")

    outs = {}
    ff = ("w_down", "w_gate", "w_up")
    ff_parts = _exchange_wait([sent[n] for n in ff], after, "grads_wait_w_ff")
    ff_outs = _adamw_phases(ff_parts, *[[shard(d, n) for n in ff] for d in (w, m, v)], "adamw_w_ff")
    after = ff_outs[-1][0]
    for name, res in zip(ff, ff_outs):
        outs[name] = [(a.T if name in flipped else a)[None] for a in res]
    for name in ("w_out", "w_in"):
        (parts,) = _exchange_wait([sent[name]], after, "grads_wait_" + name)
        if name == "w_in":
            dense = lambda d: d[name].transpose(2, 0, 1).reshape(-1, LANES)
            outs[name] = _adamw_dense(parts, dense(w), dense(m), dense(v), "adamw_" + name, rows=IN_SHARD, shift=IN_SHIFT)
            after = outs[name][0]
            outs[name] = [a.reshape(-1, D_MODEL // LANES, LANES).transpose(1, 2, 0).reshape(1, D_MODEL, -1)
                          for a in outs[name]]
            continue
        outs[name] = _adamw(parts, shard(w, name), shard(m, name), shard(v, name), "adamw_" + name)
        after = outs[name][0]
        outs[name] = [(a.T if name in flipped else a)[None] for a in outs[name]]
    parts_rows, parts_pool = _exchange_wait(small_handles, after, "grads_wait_replicated")
    loss, small = _adamw_replicated(parts_rows, parts_pool, w, m, v)
    outs.update(small)

    return (loss.reshape(()), dx[None]) + tuple(outs[n][k] for k in range(4) for n in order)
```
